```python
import jax, jax.numpy as jnp
from jax import lax
import numpy as np

D_MODEL = 4096
BATCH = 8
SEQ = 4096
DEPTH = 1

GRID_W = 64
CTX_LEN = 256
EXPAND = 2
MIX_W = EXPAND * D_MODEL
LRU_W = MIX_W // 2
POOL_W = MIX_W - LRU_W
LRU_HEADS = 16
LRU_HEAD_DIM = LRU_W // LRU_HEADS
CONV_W = 4
CONV_LEFT = 1
LRU_C = 8.0
POOL_WINDOWS = (2, 4, 8, 16)
POOL_GROUPS = len(POOL_WINDOWS)
POOL_GROUP_DIM = POOL_W // POOL_GROUPS
EPS = 1e-6

kernel_name = 'hybrid_rglru_pool_dit_block'


def rmsnorm(x, g):
    xf = x.astype(jnp.float32)
    y = xf * lax.rsqrt(jnp.mean(xf * xf, axis=-1, keepdims=True) + EPS) * g.astype(jnp.float32)
    return y.astype(x.dtype)


def modulate(h, shift, scale):
    return h * (1.0 + scale) + shift


def depthwise_conv(u, w, b):
    L = u.shape[1]
    up = jnp.pad(u, ((0, 0), (CONV_LEFT, CONV_W - 1 - CONV_LEFT), (0, 0)))
    y = b
    for k in range(CONV_W):
        y = y + up[:, k:k + L] * w[k]
    return y


def rglru_coeffs(u, lam, w_r, b_r, w_i, b_i):
    uf = u.astype(jnp.float32)
    Bn, L, _ = uf.shape
    uh = uf.reshape(Bn, L, LRU_HEADS, LRU_HEAD_DIM)
    r = jax.nn.sigmoid(jnp.einsum('blhi,hij->blhj', uh, w_r.astype(jnp.float32)).reshape(Bn, L, LRU_W) + b_r.astype(jnp.float32))
    i = jax.nn.sigmoid(jnp.einsum('blhi,hij->blhj', uh, w_i.astype(jnp.float32)).reshape(Bn, L, LRU_W) + b_i.astype(jnp.float32))
    log_a = (-LRU_C * jax.nn.softplus(-lam.astype(jnp.float32))) * r
    a = jnp.exp(log_a)
    b = jnp.sqrt(-jnp.expm1(2.0 * log_a)) * (i * uf)
    return a, b


def linear_scan(a, b, h0, reverse, return_seq):
    aT = jnp.swapaxes(a, 0, 1)
    bT = jnp.swapaxes(b, 0, 1)

    def step(h, ab):
        at, bt = ab
        h = at * h + bt
        return h, (h if return_seq else None)

    h_last, hs = lax.scan(step, h0, (aT, bT), reverse=reverse)
    return (jnp.swapaxes(hs, 0, 1) if return_seq else None), h_last


def rglru_branch(xa_lat, xa_ctx, conv_w, conv_b, lam, w_r, b_r, w_i, b_i, ctx_out):
    u_lat = depthwise_conv(xa_lat, conv_w, conv_b)
    u_ctx = depthwise_conv(xa_ctx, conv_w, conv_b)
    ys_lat, ys_ctx = [], []
    for d, reverse in enumerate((False, True)):
        a_c, b_c = rglru_coeffs(u_ctx, lam[d], w_r[d], b_r[d], w_i[d], b_i[d])
        h0 = jnp.zeros((u_ctx.shape[0], LRU_W), jnp.float32)
        hs_c, h_c = linear_scan(a_c, b_c, h0, reverse, ctx_out)
        a_l, b_l = rglru_coeffs(u_lat, lam[d], w_r[d], b_r[d], w_i[d], b_i[d])
        hs_l, _ = linear_scan(a_l, b_l, h_c, reverse, True)
        ys_lat.append(hs_l)
        ys_ctx.append(hs_c)
    y_lat = ys_lat[0] + ys_lat[1]
    y_ctx = (ys_ctx[0] + ys_ctx[1]) if ctx_out else None
    return y_lat, y_ctx


def centred_mean(v, w):
    L = v.shape[-2]
    left = w // 2
    right = w - 1 - left
    S = jnp.concatenate([jnp.zeros_like(v[..., :1, :]), lax.cumsum(v, axis=v.ndim - 2)], axis=-2)
    t = np.arange(L)
    lo = np.maximum(t - left, 0)
    hi = np.minimum(t + right, L - 1) + 1
    sums = jnp.take(S, jnp.asarray(hi), axis=-2) - jnp.take(S, jnp.asarray(lo), axis=-2)
    cnt = jnp.asarray((hi - lo).astype(np.float32))[:, None]
    return sums / cnt


def pool_branch(u, w_pool, b_pool, scale):
    uf = u.astype(jnp.float32)
    grp = uf.reshape(uf.shape[:-1] + (POOL_GROUPS, POOL_GROUP_DIM))
    outs = [centred_mean(grp[..., g, :], w) - grp[..., g, :] for g, w in enumerate(POOL_WINDOWS)]
    z = jnp.stack(outs, axis=-2)
    y = jnp.einsum('...gi,gij->...gj', z, w_pool.astype(jnp.float32)).reshape(uf.shape) + b_pool
    return y * scale


def layer(x, ctx, c, c_ctx, w_ada, b_ada, g_norm, w_in, conv_w, conv_b, lam,
          w_r, b_r, w_i, b_i, w_pool, b_pool, pool_scale, w_out, last):
    D = D_MODEL
    Bn, L, _ = x.shape
    rows = L // GRID_W
    ctx_out = not last
    mod = jax.nn.silu(c) @ w_ada + b_ada
    shift, scale, gate = jnp.split(mod, 3, axis=-1)
    h = modulate(rmsnorm(x, g_norm), shift[:, None], scale[:, None])
    proj = h @ w_in
    xa, xb, ga, gb = jnp.split(proj, [LRU_W, MIX_W, MIX_W + LRU_W], axis=-1)
    if ctx_out:
        mod_c = jax.nn.silu(c_ctx) @ w_ada + b_ada
        shift_c, scale_c, gate_c = jnp.split(mod_c, 3, axis=-1)
        hc = modulate(rmsnorm(ctx, g_norm), shift_c, scale_c)
        proj_c = hc @ w_in
    else:
        mod_c = jax.nn.silu(c_ctx) @ w_ada[:, :2 * D] + b_ada[:2 * D]
        shift_c, scale_c = jnp.split(mod_c, 2, axis=-1)
        hc = modulate(rmsnorm(ctx, g_norm), shift_c, scale_c)
        proj_c = hc @ w_in[:, :LRU_W]
    xa_c = proj_c[..., :LRU_W]
    ya, ya_c = rglru_branch(xa, xa_c, conv_w, conv_b, lam, w_r, b_r, w_i, b_i, ctx_out)
    yb = pool_branch(xb.reshape(Bn, rows, GRID_W, POOL_W), w_pool, b_pool, pool_scale).reshape(Bn, L, POOL_W)
    mixed = jnp.concatenate([ya * jax.nn.silu(ga.astype(jnp.float32)),
                             yb * jax.nn.silu(gb.astype(jnp.float32))], axis=-1).astype(x.dtype)
    x_new = (x + gate[:, None] * (mixed @ w_out)).astype(x.dtype)
    if ctx_out:
        _, xb_c, ga_c, gb_c = jnp.split(proj_c, [LRU_W, MIX_W, MIX_W + LRU_W], axis=-1)
        yb_c = pool_branch(xb_c, w_pool, b_pool, pool_scale)
        mixed_c = jnp.concatenate([ya_c * jax.nn.silu(ga_c.astype(jnp.float32)),
                                   yb_c * jax.nn.silu(gb_c.astype(jnp.float32))], axis=-1).astype(ctx.dtype)
        ctx_new = (ctx + gate_c * (mixed_c @ w_out)).astype(ctx.dtype)
    else:
        ctx_new = None
    return x_new, ctx_new


def _fwd_setup_inputs(seed: int = 0) -> dict:
    key = jax.random.key(seed)
    ks = jax.random.split(key, 20)
    D = D_MODEL
    f32 = jnp.float32
    x = jax.random.normal(ks[0], (BATCH, SEQ, D), f32)
    c = jax.random.normal(ks[1], (BATCH, D), f32)
    ctx = jax.random.normal(ks[2], (BATCH, CTX_LEN, D), f32)
    c_ctx = jax.random.normal(ks[3], (D,), f32)
    w_ada = jax.random.normal(ks[4], (DEPTH, D, 3 * D), f32) * (0.5 * D ** -0.5)
    b_ada = 0.01 * jax.random.normal(ks[5], (DEPTH, 3 * D), f32)
    g_norm = 1.0 + 0.05 * jax.random.normal(ks[6], (DEPTH, D), f32)
    w_in = jax.random.normal(ks[7], (DEPTH, D, 2 * MIX_W), f32) * D ** -0.5
    conv_w = jax.random.normal(ks[8], (DEPTH, CONV_W, LRU_W), f32) * CONV_W ** -0.5
    conv_b = 0.01 * jax.random.normal(ks[9], (DEPTH, LRU_W), f32)
    a_target = jax.random.uniform(ks[10], (DEPTH, 2, LRU_W), f32, minval=0.9, maxval=0.999) ** (1.0 / LRU_C)
    lru_lambda = jnp.log(a_target) - jnp.log1p(-a_target)
    w_rgate = jax.random.normal(ks[11], (DEPTH, 2, LRU_HEADS, LRU_HEAD_DIM, LRU_HEAD_DIM), f32) * LRU_HEAD_DIM ** -0.5
    b_rgate = 0.01 * jax.random.normal(ks[12], (DEPTH, 2, LRU_W), f32)
    w_igate = jax.random.normal(ks[13], (DEPTH, 2, LRU_HEADS, LRU_HEAD_DIM, LRU_HEAD_DIM), f32) * LRU_HEAD_DIM ** -0.5
    b_igate = 0.01 * jax.random.normal(ks[14], (DEPTH, 2, LRU_W), f32)
    w_pool = jax.random.normal(ks[15], (DEPTH, POOL_GROUPS, POOL_GROUP_DIM, POOL_GROUP_DIM), f32) * POOL_GROUP_DIM ** -0.5
    b_pool = 0.01 * jax.random.normal(ks[16], (DEPTH, POOL_W), f32)
    pool_scale = 1.0 + 0.1 * jax.random.normal(ks[17], (DEPTH, POOL_W), f32)
    w_out = jax.random.normal(ks[18], (DEPTH, MIX_W, D), f32) * MIX_W ** -0.5
    g_final = 1.0 + 0.05 * jax.random.normal(ks[19], (D,), f32)
    return {'x': x, 'c': c, 'ctx': ctx, 'c_ctx': c_ctx, 'w_ada': w_ada, 'b_ada': b_ada,
            'g_norm': g_norm, 'w_in': w_in, 'conv_w': conv_w, 'conv_b': conv_b,
            'lru_lambda': lru_lambda, 'w_rgate': w_rgate, 'b_rgate': b_rgate,
            'w_igate': w_igate, 'b_igate': b_igate, 'w_pool': w_pool, 'b_pool': b_pool,
            'pool_scale': pool_scale, 'w_out': w_out, 'g_final': g_final}


def _fwd_reference(x, c, ctx, c_ctx, w_ada, b_ada, g_norm, w_in, conv_w, conv_b, lru_lambda,
              w_rgate, b_rgate, w_igate, b_igate, w_pool, b_pool, pool_scale, w_out, g_final):
    for l in range(DEPTH):
        x, ctx = layer(x, ctx, c, c_ctx, w_ada[l], b_ada[l], g_norm[l], w_in[l], conv_w[l], conv_b[l],
                       lru_lambda[l], w_rgate[l], b_rgate[l], w_igate[l], b_igate[l],
                       w_pool[l], b_pool[l], pool_scale[l], w_out[l], last=(l == DEPTH - 1))
    return rmsnorm(x, g_final)


import jax as _jax
import jax.numpy as _jnp

TWIN_FORMAT = 'train_step'
FWD_PARAMS = ['x', 'c', 'ctx', 'c_ctx', 'w_ada', 'b_ada', 'g_norm', 'w_in', 'conv_w', 'conv_b', 'lru_lambda', 'w_rgate', 'b_rgate', 'w_igate', 'b_igate', 'w_pool', 'b_pool', 'pool_scale', 'w_out', 'g_final']
TWIN_WEIGHTS = ['c_ctx', 'w_ada', 'b_ada', 'g_norm', 'w_in', 'conv_w', 'conv_b', 'lru_lambda', 'w_rgate', 'b_rgate', 'w_igate', 'b_igate', 'w_pool', 'b_pool', 'pool_scale', 'w_out', 'g_final']
TWIN_DIFF_INPUT = 'x'
TWIN_INPUTS = ['x', 'c', 'ctx', 'c_ctx', 'w_ada', 'b_ada', 'g_norm', 'w_in', 'conv_w', 'conv_b', 'lru_lambda', 'w_rgate', 'b_rgate', 'w_igate', 'b_igate', 'w_pool', 'b_pool', 'pool_scale', 'w_out', 'g_final', 'loss_target', 'm_c_ctx', 'm_w_ada', 'm_b_ada', 'm_g_norm', 'm_w_in', 'm_conv_w', 'm_conv_b', 'm_lru_lambda', 'm_w_rgate', 'm_b_rgate', 'm_w_igate', 'm_b_igate', 'm_w_pool', 'm_b_pool', 'm_pool_scale', 'm_w_out', 'm_g_final', 'v_c_ctx', 'v_w_ada', 'v_b_ada', 'v_g_norm', 'v_w_in', 'v_conv_w', 'v_conv_b', 'v_lru_lambda', 'v_w_rgate', 'v_b_rgate', 'v_w_igate', 'v_b_igate', 'v_w_pool', 'v_b_pool', 'v_pool_scale', 'v_w_out', 'v_g_final']
TWIN_OUTPUTS = ['loss', 'grad_x', 'grad_c_ctx', 'grad_w_ada', 'grad_b_ada', 'grad_g_norm', 'grad_w_in', 'grad_conv_w', 'grad_conv_b', 'grad_lru_lambda', 'grad_w_rgate', 'grad_b_rgate', 'grad_w_igate', 'grad_b_igate', 'grad_w_pool', 'grad_b_pool', 'grad_pool_scale', 'grad_w_out', 'grad_g_final', 'delta_c_ctx', 'delta_w_ada', 'delta_b_ada', 'delta_g_norm', 'delta_w_in', 'delta_conv_w', 'delta_conv_b', 'delta_lru_lambda', 'delta_w_rgate', 'delta_b_rgate', 'delta_w_igate', 'delta_b_igate', 'delta_w_pool', 'delta_b_pool', 'delta_pool_scale', 'delta_w_out', 'delta_g_final', 'new_m_c_ctx', 'new_m_w_ada', 'new_m_b_ada', 'new_m_g_norm', 'new_m_w_in', 'new_m_conv_w', 'new_m_conv_b', 'new_m_lru_lambda', 'new_m_w_rgate', 'new_m_b_rgate', 'new_m_w_igate', 'new_m_b_igate', 'new_m_w_pool', 'new_m_b_pool', 'new_m_pool_scale', 'new_m_w_out', 'new_m_g_final', 'new_v_c_ctx', 'new_v_w_ada', 'new_v_b_ada', 'new_v_g_norm', 'new_v_w_in', 'new_v_conv_w', 'new_v_conv_b', 'new_v_lru_lambda', 'new_v_w_rgate', 'new_v_b_rgate', 'new_v_w_igate', 'new_v_b_igate', 'new_v_w_pool', 'new_v_b_pool', 'new_v_pool_scale', 'new_v_w_out', 'new_v_g_final']
TWIN_LEAF_KINDS = {'loss': 'loss', 'grad_x': 'grad_x', 'grad_c_ctx': 'grad_w', 'grad_w_ada': 'grad_w', 'grad_b_ada': 'grad_w', 'grad_g_norm': 'grad_w', 'grad_w_in': 'grad_w', 'grad_conv_w': 'grad_w', 'grad_conv_b': 'grad_w', 'grad_lru_lambda': 'grad_w', 'grad_w_rgate': 'grad_w', 'grad_b_rgate': 'grad_w', 'grad_w_igate': 'grad_w', 'grad_b_igate': 'grad_w', 'grad_w_pool': 'grad_w', 'grad_b_pool': 'grad_w', 'grad_pool_scale': 'grad_w', 'grad_w_out': 'grad_w', 'grad_g_final': 'grad_w', 'delta_c_ctx': 'delta_w', 'delta_w_ada': 'delta_w', 'delta_b_ada': 'delta_w', 'delta_g_norm': 'delta_w', 'delta_w_in': 'delta_w', 'delta_conv_w': 'delta_w', 'delta_conv_b': 'delta_w', 'delta_lru_lambda': 'delta_w', 'delta_w_rgate': 'delta_w', 'delta_b_rgate': 'delta_w', 'delta_w_igate': 'delta_w', 'delta_b_igate': 'delta_w', 'delta_w_pool': 'delta_w', 'delta_b_pool': 'delta_w', 'delta_pool_scale': 'delta_w', 'delta_w_out': 'delta_w', 'delta_g_final': 'delta_w', 'new_m_c_ctx': 'new_m', 'new_m_w_ada': 'new_m', 'new_m_b_ada': 'new_m', 'new_m_g_norm': 'new_m', 'new_m_w_in': 'new_m', 'new_m_conv_w': 'new_m', 'new_m_conv_b': 'new_m', 'new_m_lru_lambda': 'new_m', 'new_m_w_rgate': 'new_m', 'new_m_b_rgate': 'new_m', 'new_m_w_igate': 'new_m', 'new_m_b_igate': 'new_m', 'new_m_w_pool': 'new_m', 'new_m_b_pool': 'new_m', 'new_m_pool_scale': 'new_m', 'new_m_w_out': 'new_m', 'new_m_g_final': 'new_m', 'new_v_c_ctx': 'new_v', 'new_v_w_ada': 'new_v', 'new_v_b_ada': 'new_v', 'new_v_g_norm': 'new_v', 'new_v_w_in': 'new_v', 'new_v_conv_w': 'new_v', 'new_v_conv_b': 'new_v', 'new_v_lru_lambda': 'new_v', 'new_v_w_rgate': 'new_v', 'new_v_b_rgate': 'new_v', 'new_v_w_igate': 'new_v', 'new_v_b_igate': 'new_v', 'new_v_w_pool': 'new_v', 'new_v_b_pool': 'new_v', 'new_v_pool_scale': 'new_v', 'new_v_w_out': 'new_v', 'new_v_g_final': 'new_v'}


def _forward(args):
    return _fwd_reference(*[args[k] for k in FWD_PARAMS])


def _output_shape():
    out = _jax.eval_shape(lambda: _forward(_fwd_setup_inputs(0)))
    return out.shape, out.dtype

N_MICROBATCH = 1
ADAM_LR = 0.001
ADAM_B1 = 0.9
ADAM_B2 = 0.999
ADAM_EPS = 1e-08
ADAM_WD = 0.01
ADAM_STEP = 10
PER_EXAMPLE_BATCH_AXIS = {'x': 0, 'c': 0, 'ctx': 0, 'loss_target': 0}
SHARED_INPUTS = []
_WEIGHT_DTYPES = {'c_ctx': _jnp.float32, 'w_ada': _jnp.float32, 'b_ada': _jnp.float32, 'g_norm': _jnp.float32, 'w_in': _jnp.float32, 'conv_w': _jnp.float32, 'conv_b': _jnp.float32, 'lru_lambda': _jnp.float32, 'w_rgate': _jnp.float32, 'b_rgate': _jnp.float32, 'w_igate': _jnp.float32, 'b_igate': _jnp.float32, 'w_pool': _jnp.float32, 'b_pool': _jnp.float32, 'pool_scale': _jnp.float32, 'w_out': _jnp.float32, 'g_final': _jnp.float32}
MOMENT_SCALE = {'c_ctx': 6.174561e-03, 'w_ada': 8.475474e-02, 'b_ada': 1.604276e-01, 'g_norm': 2.374838e-02, 'w_in': 1.910599e-02, 'conv_w': 2.874084e-02, 'conv_b': 8.505067e-02, 'lru_lambda': 6.029988e-03, 'w_rgate': 1.288031e-03, 'b_rgate': 2.211865e-03, 'w_igate': 2.745739e-03, 'b_igate': 6.002855e-03, 'w_pool': 5.318919e-03, 'b_pool': 6.246234e-03, 'pool_scale': 5.329420e-03, 'w_out': 2.732152e-02, 'g_final': 8.067251e+00}


def _to_microbatches(a, axis):
    t = _jnp.moveaxis(a, axis, 0)
    t = t.reshape((N_MICROBATCH, t.shape[0] // N_MICROBATCH) + t.shape[1:])
    return _jnp.moveaxis(t, 1, axis + 1)


def setup_inputs(seed: int = 0) -> dict:
    inp = _fwd_setup_inputs(seed)
    key = _jax.random.fold_in(_jax.random.key(seed), 7919)
    shape, _ = _output_shape()
    out = dict(inp)
    out["loss_target"] = _jax.random.normal(_jax.random.fold_in(key, 0), shape, _jnp.float32)
    for i, name in enumerate(TWIN_WEIGHTS):
        w = inp[name].astype(_jnp.float32)
        if MOMENT_SCALE is None:
            s = _jnp.sqrt(_jnp.mean(_jnp.square(w)) + 1e-30)
        else:
            s = MOMENT_SCALE[name]
        km, kv = _jax.random.split(_jax.random.fold_in(key, i + 1))
        out[name] = w
        out["m_" + name] = s * _jax.random.normal(km, w.shape, _jnp.float32)
        out["v_" + name] = (s * s) * _jax.random.uniform(kv, w.shape, _jnp.float32, 0.5, 1.5)
    if N_MICROBATCH > 1:
        for name, axis in PER_EXAMPLE_BATCH_AXIS.items():
            out[name] = _to_microbatches(out[name], axis)
    return {'x': out['x'], 'c': out['c'], 'ctx': out['ctx'], 'c_ctx': out['c_ctx'], 'w_ada': out['w_ada'], 'b_ada': out['b_ada'], 'g_norm': out['g_norm'], 'w_in': out['w_in'], 'conv_w': out['conv_w'], 'conv_b': out['conv_b'], 'lru_lambda': out['lru_lambda'], 'w_rgate': out['w_rgate'], 'b_rgate': out['b_rgate'], 'w_igate': out['w_igate'], 'b_igate': out['b_igate'], 'w_pool': out['w_pool'], 'b_pool': out['b_pool'], 'pool_scale': out['pool_scale'], 'w_out': out['w_out'], 'g_final': out['g_final'], 'loss_target': out['loss_target'], 'm_c_ctx': out['m_c_ctx'], 'm_w_ada': out['m_w_ada'], 'm_b_ada': out['m_b_ada'], 'm_g_norm': out['m_g_norm'], 'm_w_in': out['m_w_in'], 'm_conv_w': out['m_conv_w'], 'm_conv_b': out['m_conv_b'], 'm_lru_lambda': out['m_lru_lambda'], 'm_w_rgate': out['m_w_rgate'], 'm_b_rgate': out['m_b_rgate'], 'm_w_igate': out['m_w_igate'], 'm_b_igate': out['m_b_igate'], 'm_w_pool': out['m_w_pool'], 'm_b_pool': out['m_b_pool'], 'm_pool_scale': out['m_pool_scale'], 'm_w_out': out['m_w_out'], 'm_g_final': out['m_g_final'], 'v_c_ctx': out['v_c_ctx'], 'v_w_ada': out['v_w_ada'], 'v_b_ada': out['v_b_ada'], 'v_g_norm': out['v_g_norm'], 'v_w_in': out['v_w_in'], 'v_conv_w': out['v_conv_w'], 'v_conv_b': out['v_conv_b'], 'v_lru_lambda': out['v_lru_lambda'], 'v_w_rgate': out['v_w_rgate'], 'v_b_rgate': out['v_b_rgate'], 'v_w_igate': out['v_w_igate'], 'v_b_igate': out['v_b_igate'], 'v_w_pool': out['v_w_pool'], 'v_b_pool': out['v_b_pool'], 'v_pool_scale': out['v_pool_scale'], 'v_w_out': out['v_w_out'], 'v_g_final': out['v_g_final']}


def _loss(weights, diff, rest, loss_target):
    with _jax.named_scope("forward"):
        args = {**rest, TWIN_DIFF_INPUT: diff, **{k: w.astype(_WEIGHT_DTYPES[k]) for k, w in weights.items()}}
        y = _forward(args)
    with _jax.named_scope("loss_head"):
        err = _jnp.square(y.astype(_jnp.float32) - loss_target)
        return 0.5 * _jnp.sum(_jnp.mean(err, axis=-1)) if err.ndim else 0.5 * err


def _adamw(w, g, m, v):
    m = ADAM_B1 * m + (1.0 - ADAM_B1) * g
    v = ADAM_B2 * v + (1.0 - ADAM_B2) * _jnp.square(g)
    m_hat = m / (1.0 - ADAM_B1 ** ADAM_STEP)
    v_hat = v / (1.0 - ADAM_B2 ** ADAM_STEP)
    delta = -ADAM_LR * (m_hat / (_jnp.sqrt(v_hat) + ADAM_EPS) + ADAM_WD * w)
    return delta, m, v


def reference(x, c, ctx, c_ctx, w_ada, b_ada, g_norm, w_in, conv_w, conv_b, lru_lambda, w_rgate, b_rgate, w_igate, b_igate, w_pool, b_pool, pool_scale, w_out, g_final, loss_target, m_c_ctx, m_w_ada, m_b_ada, m_g_norm, m_w_in, m_conv_w, m_conv_b, m_lru_lambda, m_w_rgate, m_b_rgate, m_w_igate, m_b_igate, m_w_pool, m_b_pool, m_pool_scale, m_w_out, m_g_final, v_c_ctx, v_w_ada, v_b_ada, v_g_norm, v_w_in, v_conv_w, v_conv_b, v_lru_lambda, v_w_rgate, v_b_rgate, v_w_igate, v_b_igate, v_w_pool, v_b_pool, v_pool_scale, v_w_out, v_g_final):
    given = dict(x=x, c=c, ctx=ctx, c_ctx=c_ctx, w_ada=w_ada, b_ada=b_ada, g_norm=g_norm, w_in=w_in, conv_w=conv_w, conv_b=conv_b, lru_lambda=lru_lambda, w_rgate=w_rgate, b_rgate=b_rgate, w_igate=w_igate, b_igate=b_igate, w_pool=w_pool, b_pool=b_pool, pool_scale=pool_scale, w_out=w_out, g_final=g_final, loss_target=loss_target, m_c_ctx=m_c_ctx, m_w_ada=m_w_ada, m_b_ada=m_b_ada, m_g_norm=m_g_norm, m_w_in=m_w_in, m_conv_w=m_conv_w, m_conv_b=m_conv_b, m_lru_lambda=m_lru_lambda, m_w_rgate=m_w_rgate, m_b_rgate=m_b_rgate, m_w_igate=m_w_igate, m_b_igate=m_b_igate, m_w_pool=m_w_pool, m_b_pool=m_b_pool, m_pool_scale=m_pool_scale, m_w_out=m_w_out, m_g_final=m_g_final, v_c_ctx=v_c_ctx, v_w_ada=v_w_ada, v_b_ada=v_b_ada, v_g_norm=v_g_norm, v_w_in=v_w_in, v_conv_w=v_conv_w, v_conv_b=v_conv_b, v_lru_lambda=v_lru_lambda, v_w_rgate=v_w_rgate, v_b_rgate=v_b_rgate, v_w_igate=v_w_igate, v_b_igate=v_b_igate, v_w_pool=v_w_pool, v_b_pool=v_b_pool, v_pool_scale=v_pool_scale, v_w_out=v_w_out, v_g_final=v_g_final)
    weights = {n: given[n] for n in TWIN_WEIGHTS}
    shared = {n: given[n] for n in SHARED_INPUTS}
    per_example = {n: given[n] for n in ['x', 'c', 'ctx']}
    grad_fn = _jax.value_and_grad(_loss, argnums=(0, 1))

    def one_microbatch(ex, loss_target):
        ex = dict(ex)
        diff = ex.pop(TWIN_DIFF_INPUT)
        return grad_fn(weights, diff, {**shared, **ex}, loss_target)

    if N_MICROBATCH == 1:
        loss, (grad_w, grad_x) = one_microbatch(per_example, given["loss_target"])
    else:
        def body(carry, xs):
            loss_sum, grad_sum = carry
            l_k, (gw_k, gx_k) = one_microbatch(xs[0], xs[1])
            with _jax.named_scope("update"):
                return (loss_sum + l_k, _jax.tree.map(_jnp.add, grad_sum, gw_k)), gx_k

        init = (_jnp.zeros((), _jnp.float32), _jax.tree.map(_jnp.zeros_like, weights))
        (loss, grad_w), grad_x = _jax.lax.scan(body, init, (per_example, given["loss_target"]))
    with _jax.named_scope("update"):
        delta_w, new_m, new_v = {}, {}, {}
        for n in TWIN_WEIGHTS:
            delta_w[n], new_m[n], new_v[n] = _adamw(weights[n], grad_w[n], given["m_" + n], given["v_" + n])
    return (loss, grad_x, *[grad_w[n] for n in TWIN_WEIGHTS], *[delta_w[n] for n in TWIN_WEIGHTS],
            *[new_m[n] for n in TWIN_WEIGHTS], *[new_v[n] for n in TWIN_WEIGHTS])
```

```python
import functools

import jax
import jax.numpy as jnp
from jax import lax
from jax.experimental import pallas as pl
from jax.experimental.pallas import tpu as pltpu

NDEV = 8
GRID_W = 64
POOL_WINDOWS = (2, 4, 8, 16)
LRU_C = 8.0
EPS = 1e-6
ADAM_LR = 0.001
ADAM_B1 = 0.9
ADAM_B2 = 0.999
ADAM_EPS = 1e-08
ADAM_WD = 0.01
ADAM_STEP = 10

F32 = jnp.float32
MXU = jnp.bfloat16

VMEM_BYTES = 64 * 1024 * 1024
VMEM_SLACK = 8 * 1024 * 1024
SUB = 8
SUB16 = 16
LANE = 128

TM = 1152
TN = 1024
TK = 576
TL = 256
TL_FINAL = 128
CB_SEQ = 256
CB_SCAN = 128
CB_MIX = 512
TR_CONV = 576

MESH_ID = pl.DeviceIdType.MESH


def _tile(n, pref, align):
    if n <= pref:
        return n
    for t in range(pref - pref % align, 0, -align):
        if n % t == 0:
            return t
    return n


def _nbytes(shape, dtype):
    n = 1
    for s in shape:
        if s is not None:
            n *= s
    return n * jnp.dtype(dtype).itemsize


def _params(blocks, scratch=(), dims=None):
    need = 2 * sum(_nbytes(s, d) for s, d in blocks) + sum(_nbytes(s, d) for s, d in scratch) + VMEM_SLACK
    kw = dict(vmem_limit_bytes=int(min(max(need, 2 * VMEM_SLACK), VMEM_BYTES - VMEM_SLACK // 2)))
    if dims is not None:
        kw["dimension_semantics"] = dims
    return pltpu.CompilerParams(**kw)


def _sds(shape, dtype):
    return jax.ShapeDtypeStruct(tuple(shape), dtype)


ANY = pl.BlockSpec(memory_space=pl.ANY)


def _ids():
    return lax.axis_index("x"), lax.axis_index("y"), lax.axis_index("c")


def _sigmoid(v):
    return jax.nn.sigmoid(v)


def _neg_expm1(v):
    series = -v * (1.0 + v * (0.5 + v * (1.0 / 6.0 + v * (1.0 / 24.0))))
    return jnp.where(v > -1e-2, series, 1.0 - jnp.exp(v))


def _softplus(v):
    return jnp.maximum(v, 0.0) + jnp.log1p(jnp.exp(-jnp.abs(v)))


def _all_gather(xs, name):
    n = len(xs)

    def body(*refs):
        x_refs, o_refs = refs[:n], refs[n:2 * n]
        send_sems, recv_sems, local_sems = refs[2 * n:]
        x, y, c = _ids()
        me, sibling = (x, y, c), (x, y, 1 - c)
        chips = [(1 - x, y), (x, 1 - y), (1 - x, 1 - y)]

        def slot(a, p):
            return o_refs[a].at[4 * p[0] + 2 * p[1] + p[2]]

        def copy(a, k, block, to, src=None):
            return pltpu.make_async_remote_copy(
                src_ref=slot(a, block) if src is None else src, dst_ref=slot(a, block),
                send_sem=send_sems.at[7 * a + k], recv_sem=recv_sems.at[7 * a + k],
                device_id=to, device_id_type=MESH_ID)

        mine, first, passed = [], [], []
        for a in range(n):
            m = pltpu.make_async_copy(x_refs[a], slot(a, me), local_sems.at[a])
            m.start()
            mine.append(m)
            f = [copy(a, 0, me, sibling, src=x_refs[a])]
            f += [copy(a, 1 + j, me, (*chip, c), src=x_refs[a]) for j, chip in enumerate(chips)]
            for cp in f:
                cp.start()
            first += f
        for a in range(n):
            for j, chip in enumerate(chips):
                copy(a, 1 + j, (*chip, c), me).wait_recv()
                p = copy(a, 4 + j, (*chip, c), sibling)
                p.start()
                passed.append(p)
        for a in range(n):
            copy(a, 0, sibling, me).wait_recv()
            for j, chip in enumerate(chips):
                copy(a, 4 + j, (*chip, 1 - c), me).wait_recv()
        for cp in first + passed:
            cp.wait_send()
        for m in mine:
            m.wait()

    return pl.pallas_call(
        body, name=name,
        out_shape=[_sds((NDEV,) + v.shape, v.dtype) for v in xs],
        in_specs=[ANY] * n, out_specs=[ANY] * n,
        scratch_shapes=[pltpu.SemaphoreType.DMA((7 * n,)), pltpu.SemaphoreType.DMA((7 * n,)),
                        pltpu.SemaphoreType.DMA((n,))],
    )(*xs)


def _exchange(srcs, name):
    n = len(srcs)

    def body(*refs):
        s_refs, r_refs = refs[:n], refs[n:2 * n]
        send_sems, recv_sems, local_sems = refs[2 * n:]
        x, y, c = _ids()
        me = 4 * x + 2 * y + c
        peers = []
        for k in range(1, NDEV):
            px = 1 - x if k & 4 else x
            py = 1 - y if k & 2 else y
            pc = 1 - c if k & 1 else c
            peers.append(((px, py, pc), 4 * px + 2 * py + pc))

        def copy(a, k, slot_src, slot_dst):
            return pltpu.make_async_remote_copy(
                src_ref=s_refs[a].at[slot_src], dst_ref=r_refs[a].at[slot_dst],
                send_sem=send_sems.at[7 * a + k], recv_sem=recv_sems.at[7 * a + k],
                device_id=peers[k][0], device_id_type=MESH_ID)

        mine, sends = [], []
        for a in range(n):
            m = pltpu.make_async_copy(s_refs[a].at[me], r_refs[a].at[me], local_sems.at[a])
            m.start()
            mine.append(m)
            for k in range(NDEV - 1):
                cp = copy(a, k, peers[k][1], me)
                cp.start()
                sends.append(cp)
        for a in range(n):
            for k in range(NDEV - 1):
                copy(a, k, peers[k][1], peers[k][1]).wait_recv()
        for cp in sends:
            cp.wait_send()
        for m in mine:
            m.wait()

    return pl.pallas_call(
        body, name=name,
        out_shape=[_sds(v.shape, v.dtype) for v in srcs],
        in_specs=[ANY] * n, out_specs=[ANY] * n,
        scratch_shapes=[pltpu.SemaphoreType.DMA((7 * n,)), pltpu.SemaphoreType.DMA((7 * n,)),
                        pltpu.SemaphoreType.DMA((n,))],
    )(*srcs)


NN = (((1,), (0,)), ((), ()))
NT = (((1,), (1,)), ((), ()))
TN_DIMS = (((0,), (0,)), ((), ()))


def _mm(a, b, *, grid, a_spec, b_spec, o_spec, out_shape, acc_shape, dims, name):
    k_axis = len(grid) - 1
    nk = grid[k_axis]

    def body(a_ref, b_ref, o_ref, acc_ref):
        k = pl.program_id(k_axis)

        @pl.when(k == 0)
        def _():
            acc_ref[...] = jnp.zeros_like(acc_ref)

        acc_ref[...] += lax.dot_general(a_ref[...], b_ref[...], dims, preferred_element_type=F32)

        @pl.when(k == nk - 1)
        def _():
            o_ref[...] = acc_ref[...].astype(o_ref.dtype)

    blocks = [(a_spec.block_shape, a.dtype), (b_spec.block_shape, b.dtype), (o_spec.block_shape, out_shape.dtype)]
    return pl.pallas_call(
        body, name=name, grid=grid, in_specs=[a_spec, b_spec], out_specs=o_spec, out_shape=out_shape,
        scratch_shapes=[pltpu.VMEM(acc_shape, F32)],
        compiler_params=_params(blocks, [(acc_shape, F32)], ("parallel",) * k_axis + ("arbitrary",)),
    )(a, b)


def _mm_plain(a, b, dims, out_dtype, name):
    if dims == TN_DIMS:
        (K, M), N = a.shape, b.shape[1]
    elif dims == NT:
        (M, K), N = a.shape, b.shape[0]
    else:
        (M, K), N = a.shape, b.shape[1]
    tm, tn = _tile(M, TM, LANE), _tile(N, TN, LANE)
    tk = _tile(K, TK, LANE if dims != TN_DIMS else SUB16)
    if dims == TN_DIMS:
        a_spec = pl.BlockSpec((tk, tm), lambda i, j, k: (k, i))
    else:
        a_spec = pl.BlockSpec((tm, tk), lambda i, j, k: (i, k))
    if dims == NT:
        b_spec = pl.BlockSpec((tn, tk), lambda i, j, k: (j, k))
    else:
        b_spec = pl.BlockSpec((tk, tn), lambda i, j, k: (k, j))
    return _mm(a, b, grid=(M // tm, N // tn, K // tk), a_spec=a_spec, b_spec=b_spec,
               o_spec=pl.BlockSpec((tm, tn), lambda i, j, k: (i, j)),
               out_shape=_sds((M, N), out_dtype), acc_shape=(tm, tn), dims=dims, name=name)


def _mm_proj(h_all, win_all, name):
    n, D = h_all.shape
    nb = win_all.shape[2]
    tm, tn, tk = _tile(n, TM, SUB16), _tile(nb, TN, LANE), _tile(D, TK, LANE)
    nbn = nb // tn
    return _mm(h_all, win_all, grid=(n // tm, NDEV * nbn, D // tk),
               a_spec=pl.BlockSpec((tm, tk), lambda i, j, k: (i, k)),
               b_spec=pl.BlockSpec((None, tk, tn), lambda i, j, k: (j // nbn, k, j % nbn)),
               o_spec=pl.BlockSpec((tm, tn), lambda i, j, k: (i, j)),
               out_shape=_sds((n, NDEV * nb), F32), acc_shape=(tm, tn), dims=NN, name=name)


def _mm_dh(dproj, win_all, name):
    n = dproj.shape[0]
    _, D, nb = win_all.shape
    tm, tn, tk = _tile(n, TM, SUB16), _tile(D, TN, LANE), _tile(nb, TK, LANE)
    nbk = nb // tk
    return _mm(dproj, win_all, grid=(n // tm, D // tn, NDEV * nbk),
               a_spec=pl.BlockSpec((tm, tk), lambda i, j, k: (i, k)),
               b_spec=pl.BlockSpec((None, tn, tk), lambda i, j, k: (k // nbk, j, k % nbk)),
               o_spec=pl.BlockSpec((tm, tn), lambda i, j, k: (i, j)),
               out_shape=_sds((n, D), F32), acc_shape=(tm, tn), dims=NT, name=name)


def _mm_gwin(h_all, dproj, nb, name):
    n, D = h_all.shape
    tm, tn, tk = _tile(D, TM, LANE), _tile(nb, TN, LANE), _tile(n, TK, SUB16)
    nbn = nb // tn
    return _mm(h_all, dproj, grid=(D // tm, NDEV * nbn, n // tk),
               a_spec=pl.BlockSpec((tk, tm), lambda i, j, k: (k, i)),
               b_spec=pl.BlockSpec((tk, tn), lambda i, j, k: (k, j)),
               o_spec=pl.BlockSpec((None, tm, tn), lambda i, j, k: (j // nbn, i, j % nbn)),
               out_shape=_sds((NDEV, D, nb), MXU), acc_shape=(tm, tn), dims=TN_DIMS, name=name)


def _mm_group(a, b, mode, out_dtype, name):
    if mode == "wgrad":
        L, W = a.shape
        G = len(POOL_WINDOWS)
        pd = W // G
        tm, tn, tk = _tile(pd, TM, LANE), _tile(pd, TN, LANE), _tile(L, TK, SUB16)
        nm, nn = pd // tm, pd // tn
        return _mm(a, b, grid=(G, nm, nn, L // tk),
                   a_spec=pl.BlockSpec((tk, tm), lambda g, i, j, k: (k, g * nm + i)),
                   b_spec=pl.BlockSpec((tk, tn), lambda g, i, j, k: (k, g * nn + j)),
                   o_spec=pl.BlockSpec((None, tm, tn), lambda g, i, j, k: (g, i, j)),
                   out_shape=_sds((G, pd, pd), out_dtype), acc_shape=(tm, tn), dims=TN_DIMS, name=name)
    L, W = a.shape
    G, pd, _ = b.shape
    tm, tn, tk = _tile(L, TM, SUB16), _tile(pd, TN, LANE), _tile(pd, TK, LANE)
    nn, nk = pd // tn, pd // tk
    if mode == "fwd":
        b_spec = pl.BlockSpec((None, tk, tn), lambda g, i, j, k: (g, k, j))
        dims = NN
    else:
        b_spec = pl.BlockSpec((None, tn, tk), lambda g, i, j, k: (g, j, k))
        dims = NT
    return _mm(a, b, grid=(G, L // tm, nn, nk),
               a_spec=pl.BlockSpec((tm, tk), lambda g, i, j, k: (i, g * nk + k)),
               b_spec=b_spec,
               o_spec=pl.BlockSpec((tm, tn), lambda g, i, j, k: (i, g * nn + j)),
               out_shape=_sds((L, W), out_dtype), acc_shape=(tm, tn), dims=dims, name=name)


def _ada_fwd(cc, w_loc, b_loc, name):
    R, D = cc.shape
    na = w_loc.shape[1]
    tk = _tile(D, 512, LANE)

    def body(c_ref, w_ref, b_ref, mod_ref, s_ref):
        k = pl.program_id(0)
        cv = c_ref[...]
        s = cv * _sigmoid(cv)
        s_ref[...] = s

        @pl.when(k == 0)
        def _():
            mod_ref[...] = jnp.broadcast_to(b_ref[...], mod_ref.shape)

        mod_ref[...] += lax.dot_general(s.astype(MXU), w_ref[...].astype(MXU), NN, preferred_element_type=F32)

    blocks = [((R, tk), F32), ((tk, na), F32), ((1, na), F32), ((R, na), F32), ((R, tk), F32)]
    return pl.pallas_call(
        body, name=name, grid=(D // tk,),
        in_specs=[pl.BlockSpec((R, tk), lambda k: (0, k)), pl.BlockSpec((tk, na), lambda k: (k, 0)),
                  pl.BlockSpec((1, na), lambda k: (0, 0))],
        out_specs=[pl.BlockSpec((R, na), lambda k: (0, 0)), pl.BlockSpec((R, tk), lambda k: (0, k))],
        out_shape=[_sds((R, na), F32), _sds((R, D), F32)],
        compiler_params=_params(blocks, dims=("arbitrary",)),
    )(cc, w_loc, b_loc)


def _adam(w, g, m, v):
    m = ADAM_B1 * m + (1.0 - ADAM_B1) * g
    v = ADAM_B2 * v + (1.0 - ADAM_B2) * (g * g)
    m_hat = m / (1.0 - ADAM_B1 ** ADAM_STEP)
    v_hat = v / (1.0 - ADAM_B2 ** ADAM_STEP)
    delta = -ADAM_LR * (m_hat / (jnp.sqrt(v_hat) + ADAM_EPS) + ADAM_WD * w)
    return delta, m, v


def _ada_bwd(s_all, ga, gc, w_loc, m_loc, v_loc, name):
    D, na = w_loc.shape
    tr = _tile(D, 256, LANE)

    def body(s_ref, ga_ref, gc_ref, w_ref, m_ref, v_ref, g_ref, d_ref, nm_ref, nv_ref, pc_ref):
        dmc = gc_ref[0:1, :]
        for p in range(1, NDEV):
            dmc = dmc + gc_ref[p:p + 1, :]
        rows = lax.broadcasted_iota(jnp.int32, (NDEV, na), 0)
        dmc8 = jnp.where(rows == 0, jnp.broadcast_to(dmc, (NDEV, na)), 0.0)
        dm = jnp.concatenate([ga_ref[...], dmc8], axis=0).astype(MXU)
        dmc16 = jnp.concatenate([dmc8, jnp.zeros_like(dmc8)], axis=0).astype(MXU)
        w = w_ref[...]
        g = lax.dot_general(s_ref[...].astype(MXU), dm, TN_DIMS, preferred_element_type=F32)
        pc_ref[...] = lax.dot_general(dmc16, w.astype(MXU), NT, preferred_element_type=F32)
        delta, nm, nv = _adam(w, g, m_ref[...], v_ref[...])
        g_ref[...] = g
        d_ref[...] = delta
        nm_ref[...] = nm
        nv_ref[...] = nv

    big = pl.BlockSpec((tr, na), lambda i: (i, 0))
    full = pl.BlockSpec((NDEV, na), lambda i: (0, 0))
    srow = pl.BlockSpec((2 * NDEV, tr), lambda i: (0, i))
    blocks = [((2 * NDEV, tr), F32)] * 2 + [((NDEV, na), F32)] * 2 + [((tr, na), F32)] * 7
    return pl.pallas_call(
        body, name=name, grid=(D // tr,),
        in_specs=[srow, full, full, big, big, big],
        out_specs=[big, big, big, big, srow],
        out_shape=[_sds((D, na), F32)] * 4 + [_sds((2 * NDEV, D), F32)],
        compiler_params=_params(blocks, dims=("parallel",)),
    )(s_all, ga, gc, w_loc, m_loc, v_loc)


def _norm_mod(x2, g, shift, scale, n, row0, h_prev, name):
    R, D = x2.shape
    tl = _tile(R, TL, SUB16)
    assert row0 % tl == 0
    b0 = row0 // tl

    def body(x_ref, g_ref, sh_ref, sc_ref, *rest):
        o_ref = rest[-1]
        xv = x_ref[...]
        s = lax.rsqrt(jnp.mean(xv * xv, axis=-1, keepdims=True) + EPS)
        nrm = xv * s * g_ref[...]
        o_ref[...] = (nrm * (1.0 + sc_ref[...]) + sh_ref[...]).astype(o_ref.dtype)

    vec = pl.BlockSpec((1, D), lambda i: (0, 0))
    in_specs = [pl.BlockSpec((tl, D), lambda i: (i, 0)), vec, vec, vec]
    args = [x2, g, shift, scale]
    aliases = {}
    if h_prev is not None:
        in_specs.append(ANY)
        args.append(h_prev)
        aliases = {4: 0}
    blocks = [((tl, D), F32), ((tl, D), MXU)] + [((1, D), F32)] * 3
    return pl.pallas_call(
        body, name=name, grid=(R // tl,), in_specs=in_specs,
        out_specs=pl.BlockSpec((tl, D), lambda i: (i + b0, 0)),
        out_shape=_sds((n, D), MXU), input_output_aliases=aliases,
        compiler_params=_params(blocks, dims=("parallel",)),
    )(*args)


def _norm_bwd(x2, dh_all, row0, g, scale, dxn, ggn0, name):
    R, D = x2.shape
    tl = _tile(R, TL_FINAL, SUB)
    assert row0 % tl == 0
    b0 = row0 // tl
    with_x = dxn is not None

    def body(*refs):
        if with_x:
            x_ref, dh_ref, g_ref, sc_ref, gg0_ref, dxn_ref, gx_ref, dsh_ref, dsc_ref, gg_ref = refs
        else:
            x_ref, dh_ref, g_ref, sc_ref, gg0_ref, dsh_ref, dsc_ref, gg_ref = refs
        i = pl.program_id(0)

        @pl.when(i == 0)
        def _():
            dsh_ref[...] = jnp.zeros_like(dsh_ref)
            dsc_ref[...] = jnp.zeros_like(dsc_ref)
            gg_ref[...] = gg0_ref[...]

        xv = x_ref[...]
        dh = dh_ref[...]
        gv = g_ref[...]
        s = lax.rsqrt(jnp.mean(xv * xv, axis=-1, keepdims=True) + EPS)
        xh = xv * s
        dsh_ref[...] += jnp.sum(dh, axis=0, keepdims=True)
        dsc_ref[...] += jnp.sum(dh * (xh * gv), axis=0, keepdims=True)
        dn = dh * (1.0 + sc_ref[...])
        gg_ref[...] += jnp.sum(dn * xh, axis=0, keepdims=True)
        if with_x:
            dxh = dn * gv
            dx = s * (dxh - xh * jnp.mean(dxh * xh, axis=-1, keepdims=True))
            gx_ref[...] = dx + dxn_ref[...]

    vec = pl.BlockSpec((1, D), lambda i: (0, 0))
    row = pl.BlockSpec((tl, D), lambda i: (i, 0))
    in_specs = [row, pl.BlockSpec((tl, D), lambda i: (i + b0, 0)), vec, vec, vec]
    args = [x2, dh_all, g, scale, ggn0]
    out_specs = [vec, vec, vec]
    out_shape = [_sds((1, D), F32)] * 3
    if with_x:
        in_specs.append(row)
        args.append(dxn)
        out_specs = [row] + out_specs
        out_shape = [_sds((R, D), F32)] + out_shape
    blocks = [((tl, D), F32)] * (4 if with_x else 2) + [((1, D), F32)] * 6
    outs = pl.pallas_call(
        body, name=name, grid=(R // tl,), in_specs=in_specs, out_specs=out_specs, out_shape=out_shape,
        compiler_params=_params(blocks, dims=("arbitrary",)),
    )(*args)
    return tuple(outs) if with_x else (None,) + tuple(outs)


def _tap_valid(t, o, lc, n):
    tt = t + o
    in_ctx = t < lc
    return (tt >= jnp.where(in_ctx, 0, lc)) & (tt < jnp.where(in_ctx, lc, n))


def _conv_fwd(proj_all, cw, cb, lc, W, name):
    n = proj_all.shape[0]
    cbk = _tile(W, CB_SEQ, LANE)
    tr = _tile(n, TR_CONV, SUB16)
    ext = tr + 2 * SUB

    def body(x_ref, w_ref, b_ref, u_ref, xp_ref):
        xp_ref[0:SUB, :] = jnp.zeros((SUB, cbk), F32)
        xp_ref[n + SUB:n + 2 * SUB, :] = jnp.zeros((SUB, cbk), F32)
        xp_ref[SUB:n + SUB, :] = x_ref[...]
        w = w_ref[...]
        bias = b_ref[...]

        def chunk(ci, carry):
            r0 = pl.multiple_of(ci * tr, SUB16)
            xe = xp_ref[pl.ds(r0, ext), :]
            t = r0 + lax.broadcasted_iota(jnp.int32, (tr, cbk), 0)
            acc = jnp.broadcast_to(bias, (tr, cbk))
            for k in range(4):
                o = k - 1
                sh = xe if o == 0 else pltpu.roll(xe, (-o) % ext, 0)
                acc = acc + jnp.where(_tap_valid(t, o, lc, n), sh[SUB:tr + SUB], 0.0) * w[k:k + 1]
            u_ref[pl.ds(r0, tr), :] = acc
            return carry

        lax.fori_loop(0, n // tr, chunk, 0)

    blocks = [((n, cbk), F32)] * 2 + [((4, cbk), F32), ((1, cbk), F32)]
    scratch = [((n + 2 * SUB, cbk), F32)]
    return pl.pallas_call(
        body, name=name, grid=(W // cbk,),
        in_specs=[pl.BlockSpec((n, cbk), lambda j: (0, j)), pl.BlockSpec((4, cbk), lambda j: (0, j)),
                  pl.BlockSpec((1, cbk), lambda j: (0, j))],
        out_specs=pl.BlockSpec((n, cbk), lambda j: (0, j)),
        out_shape=_sds((n, W), F32),
        scratch_shapes=[pltpu.VMEM(s, d) for s, d in scratch],
        compiler_params=_params(blocks, scratch, ("parallel",)),
    )(proj_all, cw, cb)


def _conv_bwd(du_all, proj_all, cw, dproj, lc, W, name):
    n = du_all.shape[0]
    cbk = _tile(W, CB_SEQ, LANE)
    tr = _tile(n, TR_CONV, SUB16)
    ext = tr + 2 * SUB

    def body(du_ref, x_ref, w_ref, dp_in, dx_ref, gw_ref, gb_ref, dp_ref, xp_ref):
        del dp_in
        for ref, src in ((dp_ref, du_ref), (xp_ref, x_ref)):
            ref[0:SUB, :] = jnp.zeros((SUB, cbk), F32)
            ref[n + SUB:n + 2 * SUB, :] = jnp.zeros((SUB, cbk), F32)
            ref[SUB:n + SUB, :] = src[...]
        w = w_ref[...]

        def fold(v):
            return jnp.sum(v.reshape(tr // SUB, SUB, cbk), axis=0)

        def chunk(ci, carry):
            r0 = pl.multiple_of(ci * tr, SUB16)
            de = dp_ref[pl.ds(r0, ext), :]
            xe = xp_ref[pl.ds(r0, ext), :]
            t = r0 + lax.broadcasted_iota(jnp.int32, (tr, cbk), 0)
            d0 = de[SUB:tr + SUB]
            dx = jnp.zeros((tr, cbk), F32)
            new = []
            for k in range(4):
                o = k - 1
                dsh = de if o == 0 else pltpu.roll(de, o % ext, 0)
                dx = dx + jnp.where(_tap_valid(t, -o, lc, n), dsh[SUB:tr + SUB], 0.0) * w[k:k + 1]
                xsh = xe if o == 0 else pltpu.roll(xe, (-o) % ext, 0)
                new.append(carry[k] + fold(d0 * jnp.where(_tap_valid(t, o, lc, n), xsh[SUB:tr + SUB], 0.0)))
            new.append(carry[4] + fold(d0))
            dx_ref[pl.ds(r0, tr), :] = dx.astype(dx_ref.dtype)
            return tuple(new)

        zero = jnp.zeros((SUB, cbk), F32)
        acc = lax.fori_loop(0, n // tr, chunk, (zero,) * 5)
        for k in range(4):
            gw_ref[k:k + 1, :] = jnp.sum(acc[k], axis=0, keepdims=True)
        gb_ref[...] = jnp.sum(acc[4], axis=0, keepdims=True)

    col = pl.BlockSpec((n, cbk), lambda j: (0, j))
    blocks = [((n, cbk), F32)] * 2 + [((n, cbk), MXU), ((4, cbk), F32), ((4, cbk), F32), ((1, cbk), F32)]
    scratch = [((n + 2 * SUB, cbk), F32)] * 2
    return pl.pallas_call(
        body, name=name, grid=(W // cbk,),
        in_specs=[col, col, pl.BlockSpec((4, cbk), lambda j: (0, j)), ANY],
        out_specs=[col, pl.BlockSpec((4, cbk), lambda j: (0, j)), pl.BlockSpec((1, cbk), lambda j: (0, j))],
        out_shape=[_sds(dproj.shape, dproj.dtype), _sds((4, W), F32), _sds((1, W), F32)],
        input_output_aliases={3: 0},
        scratch_shapes=[pltpu.VMEM(s, d) for s, d in scratch],
        compiler_params=_params(blocks, scratch, ("parallel",)),
    )(du_all, proj_all, cw, dproj)


def _gate_coeffs(ub, u, d, wr_ref, wi_ref, br_ref, bi_ref, lam_ref):
    c = -LRU_C * _softplus(-lam_ref[d:d + 1, :])
    r = _sigmoid(lax.dot_general(ub, wr_ref[d], NN, preferred_element_type=F32) + br_ref[d:d + 1, :])
    ig = _sigmoid(lax.dot_general(ub, wi_ref[d], NN, preferred_element_type=F32) + bi_ref[d:d + 1, :])
    la = c * r
    a = jnp.exp(la)
    sq = jnp.sqrt(_neg_expm1(2.0 * la))
    return c, r, ig, a, sq


def _gate_specs(tl, hd):
    w_spec = pl.BlockSpec((2, None, hd, hd), lambda h, i: (0, h, 0, 0))
    v_spec = pl.BlockSpec((2, hd), lambda h, i: (0, h))
    return w_spec, v_spec


def _gates_fwd(u_all, wr, wi, br, bi, lam, name):
    n, W = u_all.shape
    heads, hd = wr.shape[1], wr.shape[2]
    tl = _tile(n, TL, SUB16)

    def body(u_ref, wr_ref, wi_ref, br_ref, bi_ref, lam_ref, a_ref, b_ref):
        u = u_ref[...]
        ub = u.astype(MXU)
        for d in range(2):
            _, _, ig, a, sq = _gate_coeffs(ub, u, d, wr_ref, wi_ref, br_ref, bi_ref, lam_ref)
            a_ref[d] = a
            b_ref[d] = sq * (ig * u)

    w_spec, v_spec = _gate_specs(tl, hd)
    o_spec = pl.BlockSpec((2, tl, hd), lambda h, i: (0, i, h))
    blocks = [((tl, hd), F32), ((2, hd, hd), MXU), ((2, hd, hd), MXU)] + [((2, hd), F32)] * 3 + [((2, tl, hd), F32)] * 2
    return pl.pallas_call(
        body, name=name, grid=(heads, n // tl),
        in_specs=[pl.BlockSpec((tl, hd), lambda h, i: (i, h)), w_spec, w_spec, v_spec, v_spec, v_spec],
        out_specs=[o_spec, o_spec], out_shape=[_sds((2, n, W), F32)] * 2,
        compiler_params=_params(blocks, dims=("parallel", "parallel")),
    )(u_all, wr, wi, br, bi, lam)


def _gates_bwd(u_all, da, db, wr, wi, br, bi, lam, name):
    n, W = u_all.shape
    heads, hd = wr.shape[1], wr.shape[2]
    tl = _tile(n, TL, SUB16)
    ni = n // tl

    def body(u_ref, da_ref, db_ref, wr_ref, wi_ref, br_ref, bi_ref, lam_ref,
             du_ref, gwr_ref, gwi_ref, gbr_ref, gbi_ref, gc_ref, accr_ref, acci_ref):
        i = pl.program_id(1)

        @pl.when(i == 0)
        def _():
            accr_ref[...] = jnp.zeros_like(accr_ref)
            acci_ref[...] = jnp.zeros_like(acci_ref)
            gbr_ref[...] = jnp.zeros_like(gbr_ref)
            gbi_ref[...] = jnp.zeros_like(gbi_ref)
            gc_ref[...] = jnp.zeros_like(gc_ref)

        u = u_ref[...]
        ub = u.astype(MXU)
        du = jnp.zeros_like(u)
        for d in range(2):
            c, r, ig, a, sq = _gate_coeffs(ub, u, d, wr_ref, wi_ref, br_ref, bi_ref, lam_ref)
            dbv = db_ref[d]
            t = dbv * sq
            du = du + t * ig
            d_la = da_ref[d] * a - (dbv * ig * u) * (a * a) / sq
            gc_ref[d:d + 1, :] += jnp.sum(d_la * r, axis=0, keepdims=True)
            d_pr = (d_la * c) * (r * (1.0 - r))
            d_pi = (t * u) * (ig * (1.0 - ig))
            gbr_ref[d:d + 1, :] += jnp.sum(d_pr, axis=0, keepdims=True)
            gbi_ref[d:d + 1, :] += jnp.sum(d_pi, axis=0, keepdims=True)
            pb = d_pr.astype(MXU)
            qb = d_pi.astype(MXU)
            du = du + lax.dot_general(pb, wr_ref[d], NT, preferred_element_type=F32)
            du = du + lax.dot_general(qb, wi_ref[d], NT, preferred_element_type=F32)
            accr_ref[d] += lax.dot_general(ub, pb, TN_DIMS, preferred_element_type=F32)
            acci_ref[d] += lax.dot_general(ub, qb, TN_DIMS, preferred_element_type=F32)
        du_ref[...] = du

        @pl.when(i == ni - 1)
        def _():
            gwr_ref[...] = accr_ref[...].astype(gwr_ref.dtype)
            gwi_ref[...] = acci_ref[...].astype(gwi_ref.dtype)

    w_spec, v_spec = _gate_specs(tl, hd)
    u_spec = pl.BlockSpec((tl, hd), lambda h, i: (i, h))
    ab_spec = pl.BlockSpec((2, tl, hd), lambda h, i: (0, i, h))
    blocks = ([((tl, hd), F32)] * 2 + [((2, tl, hd), F32)] * 2 + [((2, hd, hd), MXU)] * 4 + [((2, hd), F32)] * 6)
    scratch = [((2, hd, hd), F32)] * 2
    return pl.pallas_call(
        body, name=name, grid=(heads, ni),
        in_specs=[u_spec, ab_spec, ab_spec, w_spec, w_spec, v_spec, v_spec, v_spec],
        out_specs=[u_spec, w_spec, w_spec, v_spec, v_spec, v_spec],
        out_shape=[_sds((n, W), F32), _sds(wr.shape, MXU), _sds(wi.shape, MXU)] + [_sds((2, W), F32)] * 3,
        scratch_shapes=[pltpu.VMEM(s, d) for s, d in scratch],
        compiler_params=_params(blocks, scratch, ("parallel", "arbitrary")),
    )(u_all, da, db, wr, wi, br, bi, lam)


def _tile_scan(A, B, rows, reverse):
    for s in (1, 2, 4):
        if reverse:
            As, Bs, m = pltpu.roll(A, SUB - s, 0), pltpu.roll(B, SUB - s, 0), rows < SUB - s
        else:
            As, Bs, m = pltpu.roll(A, s, 0), pltpu.roll(B, s, 0), rows >= s
        B = jnp.where(m, A * Bs + B, B)
        A = jnp.where(m, A * As, A)
    return A, B


def _scan_fwd(a_all, b_all, lc, name):
    _, n, W = a_all.shape
    cb = _tile(W, CB_SCAN, LANE)
    nt, ntc = n // SUB, lc // SUB

    def body(a_ref, b_ref, h_ref):
        rows = lax.broadcasted_iota(jnp.int32, (SUB, cb), 0)

        def step(s, carry):
            h0, h1 = carry
            r0 = pl.multiple_of(s * SUB, SUB)
            A, B = _tile_scan(a_ref[0, pl.ds(r0, SUB), :], b_ref[0, pl.ds(r0, SUB), :], rows, False)
            H0 = A * h0 + B
            h_ref[0, pl.ds(r0, SUB), :] = H0
            j1 = jnp.where(s < ntc, ntc - 1 - s, nt - 1 - (s - ntc))
            r1 = pl.multiple_of(j1 * SUB, SUB)
            A, B = _tile_scan(a_ref[1, pl.ds(r1, SUB), :], b_ref[1, pl.ds(r1, SUB), :], rows, True)
            H1 = A * h1 + B
            h_ref[1, pl.ds(r1, SUB), :] = H1
            return H0[SUB - 1:SUB, :], H1[0:1, :]

        zero = jnp.zeros((1, cb), F32)
        lax.fori_loop(0, nt, step, (zero, zero))

    spec = pl.BlockSpec((2, n, cb), lambda j: (0, 0, j))
    return pl.pallas_call(
        body, name=name, grid=(W // cb,), in_specs=[spec, spec], out_specs=spec,
        out_shape=_sds((2, n, W), F32),
        compiler_params=_params([((2, n, cb), F32)] * 3, dims=("parallel",)),
    )(a_all, b_all)


def _scan_bwd(a_all, h_all, dya, lc, name):
    _, n, W = a_all.shape
    cb = _tile(W, CB_SCAN, LANE)
    nt, ntc = n // SUB, lc // SUB
    nl = nt - ntc

    def body(a_ref, h_ref, g_ref, da_ref, db_ref):
        rows = lax.broadcasted_iota(jnp.int32, (SUB, cb), 0)

        def tile(ref, d, j):
            return ref[d, pl.ds(pl.multiple_of(j * SUB, SUB), SUB), :]

        def grad_tile(j):
            jl = jnp.maximum(j - ntc, 0)
            g = g_ref[pl.ds(pl.multiple_of(jl * SUB, SUB), SUB), :]
            return jnp.where(j >= ntc, g, 0.0)

        def put(ref, d, j, v):
            ref[d, pl.ds(pl.multiple_of(j * SUB, SUB), SUB), :] = v

        def step(s, carry):
            mu0, mu1 = carry
            j = nt - 1 - s
            a_t = tile(a_ref, 0, j)
            ap = jnp.where(rows < SUB - 1, pltpu.roll(a_t, SUB - 1, 0), 1.0)
            A, B = _tile_scan(ap, grad_tile(j), rows, True)
            lam = A * mu0 + B
            below = jnp.where(j > 0, tile(h_ref, 0, jnp.maximum(j - 1, 0))[SUB - 1:SUB, :], 0.0)
            hprev = jnp.where(rows >= 1, pltpu.roll(tile(h_ref, 0, j), 1, 0), below)
            put(da_ref, 0, j, lam * hprev)
            put(db_ref, 0, j, lam)
            mu0 = a_t[0:1, :] * lam[0:1, :]
            j = jnp.where(s < nl, ntc + s, s - nl)
            a_t = tile(a_ref, 1, j)
            ap = jnp.where(rows >= 1, pltpu.roll(a_t, 1, 0), 1.0)
            A, B = _tile_scan(ap, grad_tile(j), rows, False)
            lam = A * mu1 + B
            jn = jnp.where(j == nt - 1, 0, jnp.minimum(j + 1, nt - 1))
            above = jnp.where(j == ntc - 1, 0.0, tile(h_ref, 1, jn)[0:1, :])
            hprev = jnp.where(rows < SUB - 1, pltpu.roll(tile(h_ref, 1, j), SUB - 1, 0), above)
            put(da_ref, 1, j, lam * hprev)
            put(db_ref, 1, j, lam)
            mu1 = a_t[SUB - 1:SUB, :] * lam[SUB - 1:SUB, :]
            return mu0, mu1

        zero = jnp.zeros((1, cb), F32)
        lax.fori_loop(0, nt, step, (zero, zero))

    spec = pl.BlockSpec((2, n, cb), lambda j: (0, 0, j))
    g_spec = pl.BlockSpec((n - lc, cb), lambda j: (0, j))
    return pl.pallas_call(
        body, name=name, grid=(W // cb,), in_specs=[spec, spec, g_spec], out_specs=[spec, spec],
        out_shape=[_sds((2, n, W), F32)] * 2,
        compiler_params=_params([((2, n, cb), F32)] * 4 + [((n - lc, cb), F32)], dims=("parallel",)),
    )(a_all, h_all, dya)


def _pool_window(v, w, tl, cb, transpose):
    left = w // 2
    right = w - 1 - left
    pos = lax.broadcasted_iota(jnp.int32, (tl, cb), 0) % GRID_W
    cnt = (jnp.minimum(pos + right, GRID_W - 1) - jnp.maximum(pos - left, 0) + 1).astype(F32)
    src = v / cnt if transpose else v
    lo, hi = (-right, left) if transpose else (-left, right)
    acc = src
    for o in range(lo, hi + 1):
        if o != 0:
            ok = (pos + o >= 0) & (pos + o < GRID_W)
            acc = acc + jnp.where(ok, pltpu.roll(src, (-o) % tl, 0), 0.0)
    return acc - v if transpose else acc / cnt - v


def _pool_z(src, row0, col0, L, W, transpose, dproj, name):
    G = len(POOL_WINDOWS)
    pd = W // G
    tl = _tile(L, TL, GRID_W)
    cb = _tile(pd, CB_SEQ, LANE)
    assert row0 % tl == 0 and col0 % cb == 0
    rb, cbk = row0 // tl, col0 // cb
    nj = pd // cb

    def body(x_ref, *rest):
        o_ref = rest[-1]
        for gi, w in enumerate(POOL_WINDOWS):
            @pl.when(pl.program_id(0) == gi)
            def _(w=w):
                o_ref[...] = _pool_window(x_ref[...], w, tl, cb, transpose).astype(o_ref.dtype)

    plain = pl.BlockSpec((tl, cb), lambda g, i, j: (i, g * nj + j))
    window = pl.BlockSpec((tl, cb), lambda g, i, j: (i + rb, cbk + g * nj + j))
    blocks = [((tl, cb), F32), ((tl, cb), MXU)]
    if transpose:
        return pl.pallas_call(
            body, name=name, grid=(G, L // tl, nj), in_specs=[plain, ANY], out_specs=window,
            out_shape=_sds(dproj.shape, dproj.dtype), input_output_aliases={1: 0},
            compiler_params=_params(blocks, dims=("parallel",) * 3),
        )(src, dproj)
    return pl.pallas_call(
        body, name=name, grid=(G, L // tl, nj), in_specs=[window], out_specs=plain,
        out_shape=_sds((L, W), MXU),
        compiler_params=_params(blocks, dims=("parallel",) * 3),
    )(src)


def _mix_fwd(hs, proj_all, ypre, b_pool, pool_scale, lc, name):
    L, W = ypre.shape
    tl = _tile(L, TL, SUB16)
    cb = _tile(W, CB_MIX, LANE)
    nj = W // cb
    assert lc % tl == 0
    rb = lc // tl

    def body(hs_ref, ga_ref, yp_ref, gb_ref, bp_ref, ps_ref, o_ref):
        p = pl.program_id(2)

        @pl.when(p == 0)
        def _():
            g = ga_ref[...]
            o_ref[...] = ((hs_ref[0] + hs_ref[1]) * (g * _sigmoid(g))).astype(o_ref.dtype)

        @pl.when(p == 1)
        def _():
            g = gb_ref[...]
            yb = (yp_ref[...] + bp_ref[...]) * ps_ref[...]
            o_ref[...] = (yb * (g * _sigmoid(g))).astype(o_ref.dtype)

    vec = pl.BlockSpec((1, cb), lambda i, j, p: (0, j))
    blocks = [((2, tl, cb), F32)] + [((tl, cb), F32)] * 3 + [((tl, cb), MXU)]
    return pl.pallas_call(
        body, name=name, grid=(L // tl, nj, 2),
        in_specs=[pl.BlockSpec((2, tl, cb), lambda i, j, p: (0, i + rb, j)),
                  pl.BlockSpec((tl, cb), lambda i, j, p: (i + rb, 2 * nj + j)),
                  pl.BlockSpec((tl, cb), lambda i, j, p: (i, j)),
                  pl.BlockSpec((tl, cb), lambda i, j, p: (i + rb, 3 * nj + j)), vec, vec],
        out_specs=pl.BlockSpec((tl, cb), lambda i, j, p: (i, p * nj + j)),
        out_shape=_sds((L, 2 * W), MXU),
        compiler_params=_params(blocks, dims=("parallel", "parallel", "arbitrary")),
    )(hs, proj_all, ypre, proj_all, b_pool, pool_scale)


def _dsilu(g, sg):
    return sg * (1.0 + g * (1.0 - sg))


def _mixa_bwd(dmixed, hs, proj_all, dproj, lc, W, name):
    L = dmixed.shape[0]
    tl = _tile(L, TL, SUB16)
    cb = _tile(W, CB_MIX, LANE)
    nj = W // cb
    rb = lc // tl

    def body(dm_ref, hs_ref, ga_ref, dp_in, dya_ref, dga_ref):
        del dp_in
        g = ga_ref[...]
        sg = _sigmoid(g)
        dm = dm_ref[...]
        dya_ref[...] = dm * (g * sg)
        dga_ref[...] = (dm * (hs_ref[0] + hs_ref[1]) * _dsilu(g, sg)).astype(dga_ref.dtype)

    blocks = [((tl, cb), F32)] * 3 + [((2, tl, cb), F32), ((tl, cb), MXU)]
    return pl.pallas_call(
        body, name=name, grid=(L // tl, nj),
        in_specs=[pl.BlockSpec((tl, cb), lambda i, j: (i, j)),
                  pl.BlockSpec((2, tl, cb), lambda i, j: (0, i + rb, j)),
                  pl.BlockSpec((tl, cb), lambda i, j: (i + rb, 2 * nj + j)), ANY],
        out_specs=[pl.BlockSpec((tl, cb), lambda i, j: (i, j)),
                   pl.BlockSpec((tl, cb), lambda i, j: (i + rb, 2 * nj + j))],
        out_shape=[_sds((L, W), F32), _sds(dproj.shape, dproj.dtype)],
        input_output_aliases={3: 1},
        compiler_params=_params(blocks, dims=("parallel", "parallel")),
    )(dmixed, hs, proj_all, dproj)


def _mixb_bwd(dmixed, ypre, proj_all, b_pool, pool_scale, dproj, lc, W, name):
    L = dmixed.shape[0]
    tl = _tile(L, TL, SUB16)
    cb = _tile(W, CB_MIX, LANE)
    nj = W // cb
    rb = lc // tl

    def body(dm_ref, yp_ref, gb_ref, bp_ref, ps_ref, dp_in, dyp_ref, dgb_ref, gbp_ref, gps_ref):
        del dp_in
        i = pl.program_id(1)

        @pl.when(i == 0)
        def _():
            gbp_ref[...] = jnp.zeros_like(gbp_ref)
            gps_ref[...] = jnp.zeros_like(gps_ref)

        g = gb_ref[...]
        sg = _sigmoid(g)
        dm = dm_ref[...]
        yp = yp_ref[...] + bp_ref[...]
        ps = ps_ref[...]
        dyb = dm * (g * sg)
        dyp = dyb * ps
        dgb_ref[...] = (dm * (yp * ps) * _dsilu(g, sg)).astype(dgb_ref.dtype)
        dyp_ref[...] = dyp.astype(dyp_ref.dtype)
        gbp_ref[...] += jnp.sum(dyp, axis=0, keepdims=True)
        gps_ref[...] += jnp.sum(dyb * yp, axis=0, keepdims=True)

    vec = pl.BlockSpec((1, cb), lambda j, i: (0, j))
    blocks = [((tl, cb), F32)] * 3 + [((tl, cb), MXU)] * 2 + [((1, cb), F32)] * 4
    return pl.pallas_call(
        body, name=name, grid=(nj, L // tl),
        in_specs=[pl.BlockSpec((tl, cb), lambda j, i: (i, nj + j)),
                  pl.BlockSpec((tl, cb), lambda j, i: (i, j)),
                  pl.BlockSpec((tl, cb), lambda j, i: (i + rb, 3 * nj + j)), vec, vec, ANY],
        out_specs=[pl.BlockSpec((tl, cb), lambda j, i: (i, j)),
                   pl.BlockSpec((tl, cb), lambda j, i: (i + rb, 3 * nj + j)), vec, vec],
        out_shape=[_sds((L, W), MXU), _sds(dproj.shape, dproj.dtype), _sds((1, W), F32), _sds((1, W), F32)],
        input_output_aliases={5: 1},
        compiler_params=_params(blocks, dims=("parallel", "arbitrary")),
    )(dmixed, ypre, proj_all, b_pool, pool_scale, dproj)


def _dproj_init(n, lc, W, name):
    cb = _tile(W, CB_MIX, LANE)
    nj = W // cb

    def body(o_ref):
        o_ref[...] = jnp.zeros_like(o_ref)

    return pl.pallas_call(
        body, name=name, grid=(3 * nj,), in_specs=[],
        out_specs=pl.BlockSpec((lc, cb), lambda j: (0, nj + j)),
        out_shape=_sds((n, 4 * W), MXU),
        compiler_params=_params([((lc, cb), MXU)], dims=("parallel",)),
    )()


def _final(x2, out, tgt, gate, gfin, name):
    L, D = x2.shape
    tl = _tile(L, TL_FINAL, SUB16)

    def body(x_ref, o_ref, t_ref, gate_ref, g_ref, dout_ref, dxn_ref, loss_ref, ggf_ref, dgate_ref):
        i = pl.program_id(0)

        @pl.when(i == 0)
        def _():
            loss_ref[...] = jnp.zeros_like(loss_ref)
            ggf_ref[...] = jnp.zeros_like(ggf_ref)
            dgate_ref[...] = jnp.zeros_like(dgate_ref)

        o = o_ref[...]
        gate_v = gate_ref[...]
        gv = g_ref[...]
        xn = x_ref[...] + gate_v * o
        s = lax.rsqrt(jnp.mean(xn * xn, axis=-1, keepdims=True) + EPS)
        xh = xn * s
        err = xh * gv - t_ref[...]
        tok = jnp.mean(err * err, axis=-1, keepdims=True)
        loss_ref[...] += 0.5 * jnp.sum(tok, axis=0, keepdims=True)
        dy = err / D
        ggf_ref[...] += jnp.sum(dy * xh, axis=0, keepdims=True)
        dxh = dy * gv
        dxn = s * (dxh - xh * jnp.mean(dxh * xh, axis=-1, keepdims=True))
        dgate_ref[...] += jnp.sum(dxn * o, axis=0, keepdims=True)
        dout_ref[...] = (gate_v * dxn).astype(dout_ref.dtype)
        dxn_ref[...] = dxn

    row = pl.BlockSpec((tl, D), lambda i: (i, 0))
    vec = pl.BlockSpec((1, D), lambda i: (0, 0))
    blocks = [((tl, D), F32)] * 4 + [((tl, D), MXU)] + [((1, D), F32)] * 4
    return pl.pallas_call(
        body, name=name, grid=(L // tl,), in_specs=[row, row, row, vec, vec],
        out_specs=[row, row, pl.BlockSpec((1, 1), lambda i: (0, 0)), vec, vec],
        out_shape=[_sds((L, D), MXU), _sds((L, D), F32), _sds((1, 1), F32), _sds((1, D), F32), _sds((1, D), F32)],
        compiler_params=_params(blocks, dims=("arbitrary",)),
    )(x2, out, tgt, gate, gfin)


def _adamw_parts(w2, parts, m2, v2, name):
    R, C = w2.shape
    tr = _tile(R, max(SUB16, (256 * 1024) // C), SUB16)

    def body(w_ref, p_ref, m_ref, v_ref, g_ref, d_ref, nm_ref, nv_ref):
        g = p_ref[0].astype(F32)
        for p in range(1, NDEV):
            g = g + p_ref[p].astype(F32)
        delta, nm, nv = _adam(w_ref[...], g, m_ref[...], v_ref[...])
        g_ref[...] = g
        d_ref[...] = delta
        nm_ref[...] = nm
        nv_ref[...] = nv

    row = pl.BlockSpec((tr, C), lambda i: (i, 0))
    blocks = [((tr, C), F32)] * 7 + [((NDEV, tr, C), parts.dtype)]
    return pl.pallas_call(
        body, name=name, grid=(R // tr,),
        in_specs=[row, pl.BlockSpec((NDEV, tr, C), lambda i: (0, i, 0)), row, row],
        out_specs=[row] * 4, out_shape=[_sds((R, C), F32)] * 4,
        compiler_params=_params(blocks, dims=("parallel",)),
    )(w2, parts, m2, v2)


def _small_sum(vs, ga, gc, name):
    ns, nm = vs.shape[1], ga.shape[1]

    def body(v_ref, ga_ref, gc_ref, tot_ref, gb_ref):
        tot = v_ref[0:1, :]
        gb = ga_ref[0:1, :]
        for p in range(1, NDEV):
            tot = tot + v_ref[p:p + 1, :]
            gb = gb + ga_ref[p:p + 1, :]
        for p in range(NDEV):
            gb = gb + gc_ref[p:p + 1, :]
        tot_ref[...] = tot
        gb_ref[...] = gb

    blocks = [((NDEV, ns), F32), ((NDEV, nm), F32), ((NDEV, nm), F32), ((1, ns), F32), ((1, nm), F32)]
    return pl.pallas_call(
        body, name=name, out_shape=[_sds((1, ns), F32), _sds((1, nm), F32)],
        compiler_params=_params(blocks),
    )(vs, ga, gc)


def _adamw_small(g_raw, w, m, v, lam_range, cctx_range, name):
    npk = w.shape[1]

    def body(g_ref, w_ref, m_ref, v_ref, go_ref, d_ref, nm_ref, nv_ref):
        wv = w_ref[...]
        g = g_ref[...]
        idx = lax.broadcasted_iota(jnp.int32, (1, npk), 1)
        in_lam = (idx >= lam_range[0]) & (idx < lam_range[1])
        in_cc = (idx >= cctx_range[0]) & (idx < cctx_range[1])
        sg = _sigmoid(wv)
        g = jnp.where(in_lam, g * (LRU_C * _sigmoid(-wv)), jnp.where(in_cc, g * _dsilu(wv, sg), g))
        delta, nm, nv = _adam(wv, g, m_ref[...], v_ref[...])
        go_ref[...] = g
        d_ref[...] = delta
        nm_ref[...] = nm
        nv_ref[...] = nv

    return pl.pallas_call(
        body, name=name, out_shape=[_sds((1, npk), F32)] * 4,
        compiler_params=_params([((1, npk), F32)] * 8),
    )(g_raw, w, m, v)


def _pack(pieces):
    return jnp.concatenate([p.reshape(1, -1) for p in pieces], axis=1)


def kernel(x, c, ctx, c_ctx, w_ada, b_ada, g_norm, w_in, conv_w, conv_b, lru_lambda, w_rgate, b_rgate, w_igate, b_igate, w_pool, b_pool, pool_scale, w_out, g_final, loss_target, m_c_ctx, m_w_ada, m_b_ada, m_g_norm, m_w_in, m_conv_w, m_conv_b, m_lru_lambda, m_w_rgate, m_b_rgate, m_w_igate, m_b_igate, m_w_pool, m_b_pool, m_pool_scale, m_w_out, m_g_final, v_c_ctx, v_w_ada, v_b_ada, v_g_norm, v_w_in, v_conv_w, v_conv_b, v_lru_lambda, v_w_rgate, v_b_rgate, v_w_igate, v_b_igate, v_w_pool, v_b_pool, v_pool_scale, v_w_out, v_g_final):
    L, D = x.shape[1], x.shape[2]
    lc = ctx.shape[1]
    n = lc + L
    W = conv_b.shape[1]
    heads, hd = w_rgate.shape[2], w_rgate.shape[4]
    G, pd = w_pool.shape[1], w_pool.shape[3]
    na = w_ada.shape[2]
    nb = w_in.shape[2]
    ws = W // NDEV
    me = 4 * lax.axis_index("x") + 2 * lax.axis_index("y") + lax.axis_index("c")

    (win_all, wout_all, wpool_all, wr_all, wi_all, cw_all, lam_all, br_all, bi_all, c_all) = _all_gather(
        [w_in[0].astype(MXU), w_out[0].astype(MXU), w_pool[0].astype(MXU), w_rgate[0].astype(MXU),
         w_igate[0].astype(MXU), conv_w[0], lru_lambda[0], b_rgate[0], b_igate[0], c], "gather_weights")
    wout = wout_all.reshape(2 * W, D)
    wpool = wpool_all.transpose(1, 0, 2, 3).reshape(G, pd, pd)
    wr = wr_all.transpose(1, 2, 0, 3, 4).reshape(2, heads, hd, hd)
    wi = wi_all.transpose(1, 2, 0, 3, 4).reshape(2, heads, hd, hd)
    cw = cw_all.transpose(1, 0, 2).reshape(4, W)
    lam = lam_all.transpose(1, 0, 2).reshape(2, W)
    br = br_all.transpose(1, 0, 2).reshape(2, W)
    bi = bi_all.transpose(1, 0, 2).reshape(2, W)

    cc = jnp.concatenate([c_all.reshape(NDEV, D), c_ctx.reshape(1, D), jnp.zeros((NDEV - 1, D), F32)], axis=0)
    b_loc = lax.dynamic_slice(b_ada, (0, me * na), (1, na))
    mod_loc, s_all = _ada_fwd(cc, w_ada[0], b_loc, "ada_fwd")
    (mod_all,) = _all_gather([mod_loc], "gather_mod")
    mod = mod_all.transpose(1, 0, 2).reshape(2 * NDEV, NDEV * na)
    mod_me = lax.dynamic_slice(mod, (me, 0), (1, 3 * D))
    shift, scale, gate = mod_me[:, :D], mod_me[:, D:2 * D], mod_me[:, 2 * D:]
    shift_c, scale_c = mod[NDEV:NDEV + 1, :D], mod[NDEV:NDEV + 1, D:2 * D]

    x2, ctx2, tgt = x[0], ctx[0], loss_target[0]
    gfin = g_final.reshape(1, D)
    h_all = _norm_mod(x2, g_norm, shift, scale, n, lc, None, "norm_lat")
    h_all = _norm_mod(ctx2, g_norm, shift_c, scale_c, n, 0, h_all, "norm_ctx")
    proj_all = _mm_proj(h_all, win_all, "mm_proj")
    u_all = _conv_fwd(proj_all, cw, conv_b, lc, W, "conv_fwd")
    a_all, b_all = _gates_fwd(u_all, wr, wi, br, bi, lam, "gates_fwd")
    hs = _scan_fwd(a_all, b_all, lc, "scan_fwd")
    z = _pool_z(proj_all, lc, W, L, W, False, None, "pool_z")
    ypre = _mm_group(z, wpool, "fwd", F32, "mm_pool")
    mixed = _mix_fwd(hs, proj_all, ypre, b_pool, pool_scale, lc, "mix_fwd")
    out = _mm_plain(mixed, wout, NN, F32, "mm_out")
    d_out, dxn, loss_p, ggf, dgate = _final(x2, out, tgt, gate, gfin, "final")

    dmixed = _mm_plain(d_out, wout, NT, F32, "mm_dmixed")
    gwout = _mm_plain(mixed, d_out, TN_DIMS, MXU, "mm_gwout")
    dproj = _dproj_init(n, lc, W, "dproj_init")
    dya, dproj = _mixa_bwd(dmixed, hs, proj_all, dproj, lc, W, "mixa_bwd")
    dypre, dproj, gbp, gps = _mixb_bwd(dmixed, ypre, proj_all, b_pool, pool_scale, dproj, lc, W, "mixb_bwd")
    dz = _mm_group(dypre, wpool, "bwd", F32, "mm_dz")
    gwpool = _mm_group(z, dypre, "wgrad", MXU, "mm_gwpool")
    dproj = _pool_z(dz, lc, W, L, W, True, dproj, "pool_z_bwd")
    da, db = _scan_bwd(a_all, hs, dya, lc, "scan_bwd")
    du, gwr, gwi, gbr, gbi, gcl = _gates_bwd(u_all, da, db, wr, wi, br, bi, lam, "gates_bwd")
    dproj, gcw, gcb = _conv_bwd(du, proj_all, cw, dproj, lc, W, "conv_bwd")
    dh_all = _mm_dh(dproj, win_all, "mm_dh")
    gwin = _mm_gwin(h_all, dproj, nb, "mm_gwin")
    grad_x, dshift, dscale, ggn = _norm_bwd(x2, dh_all, lc, g_norm, scale, dxn, jnp.zeros((1, D), F32), "norm_bwd_lat")
    _, dshift_c, dscale_c, ggn = _norm_bwd(ctx2, dh_all, 0, g_norm, scale_c, None, ggn, "norm_bwd_ctx")

    recv = _exchange(
        [gwin, gwout.reshape(NDEV, 2 * W // NDEV, D),
         gwpool.reshape(G, NDEV, pd // NDEV, pd).transpose(1, 0, 2, 3),
         gwr.reshape(2, heads, NDEV, hd // NDEV, hd).transpose(2, 0, 1, 3, 4),
         gwi.reshape(2, heads, NDEV, hd // NDEV, hd).transpose(2, 0, 1, 3, 4)], "exchange_grads")
    dmod_me = jnp.concatenate([dshift, dscale, dgate], axis=1)
    dmod_c = jnp.concatenate([dshift_c, dscale_c, jnp.zeros((1, D), F32)], axis=1)
    smalls = [ggf, ggn, gcw, gcb, gcl, gbr, gbi, gbp, gps, jnp.pad(loss_p, ((0, 0), (0, LANE - 1)))]
    sizes = [s.size for s in smalls]
    small_all, dmod_all, dmodc_all = _all_gather([_pack(smalls), dmod_me, dmod_c], "gather_small")
    ga = lax.dynamic_slice(dmod_all.reshape(NDEV, 3 * D), (0, me * na), (NDEV, na))
    gc = lax.dynamic_slice(dmodc_all.reshape(NDEV, 3 * D), (0, me * na), (NDEV, na))
    g_wada, d_wada, nm_wada, nv_wada, pc = _ada_bwd(s_all, ga, gc, w_ada[0], m_w_ada[0], v_w_ada[0], "ada_bwd")
    (pc_all,) = _all_gather([pc[0:1]], "gather_cctx")
    tot, gb_ada = _small_sum(
        jnp.concatenate([small_all.reshape(NDEV, -1), pc_all.reshape(NDEV, D)], axis=1),
        dmod_all.reshape(NDEV, 3 * D), dmodc_all.reshape(NDEV, 3 * D), "small_sum")
    offs = [0]
    for s in sizes + [D]:
        offs.append(offs[-1] + s)
    t_ggf, t_ggn, t_gcw, t_gcb, t_gcl, t_gbr, t_gbi, t_gbp, t_gps, t_loss, t_pc = [
        tot[:, offs[i]:offs[i + 1]] for i in range(len(offs) - 1)]

    def shard(t, rows):
        return lax.dynamic_slice(t.reshape(rows, W), (0, me * ws), (rows, ws))

    def big(wv, parts, mv, vv, name):
        shp = wv.shape
        C = shp[-1]
        outs = _adamw_parts(wv.reshape(-1, C), parts.reshape(NDEV, -1, C), mv.reshape(-1, C), vv.reshape(-1, C), name)
        return [o.reshape(shp) for o in outs]

    r_win = big(w_in, recv[0], m_w_in, v_w_in, "adamw_w_in")
    r_wout = big(w_out, recv[1], m_w_out, v_w_out, "adamw_w_out")
    r_wpool = big(w_pool, recv[2], m_w_pool, v_w_pool, "adamw_w_pool")
    r_wr = big(w_rgate, recv[3], m_w_rgate, v_w_rgate, "adamw_w_rgate")
    r_wi = big(w_igate, recv[4], m_w_igate, v_w_igate, "adamw_w_igate")
    r_wada = [o.reshape(w_ada.shape) for o in (g_wada, d_wada, nm_wada, nv_wada)]

    names = ["c_ctx", "b_ada", "g_norm", "conv_w", "conv_b", "lru_lambda", "b_rgate", "b_igate", "b_pool",
             "pool_scale", "g_final"]
    sw = [c_ctx, b_ada, g_norm, conv_w, conv_b, lru_lambda, b_rgate, b_igate, b_pool, pool_scale, g_final]
    sm = [m_c_ctx, m_b_ada, m_g_norm, m_conv_w, m_conv_b, m_lru_lambda, m_b_rgate, m_b_igate, m_b_pool,
          m_pool_scale, m_g_final]
    sv = [v_c_ctx, v_b_ada, v_g_norm, v_conv_w, v_conv_b, v_lru_lambda, v_b_rgate, v_b_igate, v_b_pool,
          v_pool_scale, v_g_final]
    sg = [t_pc, gb_ada, t_ggn, shard(t_gcw, 4), t_gcb, shard(t_gcl, 2), shard(t_gbr, 2), shard(t_gbi, 2), t_gbp,
          t_gps, t_ggf]
    poffs = [0]
    for wv in sw:
        poffs.append(poffs[-1] + wv.size)
    lam_range = (poffs[5], poffs[6])
    cctx_range = (poffs[0], poffs[1])
    small_out = _adamw_small(_pack(sg), _pack(sw), _pack(sm), _pack(sv), lam_range, cctx_range, "adamw_small")
    r_small = {}
    for i, nm in enumerate(names):
        r_small[nm] = [o[:, poffs[i]:poffs[i + 1]].reshape(sw[i].shape) for o in small_out]

    res = dict(r_small)
    res.update(w_ada=r_wada, w_in=r_win, w_rgate=r_wr, w_igate=r_wi, w_pool=r_wpool, w_out=r_wout)
    order = ["c_ctx", "w_ada", "b_ada", "g_norm", "w_in", "conv_w", "conv_b", "lru_lambda", "w_rgate", "b_rgate",
             "w_igate", "b_igate", "w_pool", "b_pool", "pool_scale", "w_out", "g_final"]
    loss = t_loss[0, 0]
    outs = [loss, grad_x.reshape(x.shape)]
    for q in range(4):
        outs += [res[nm][q] for nm in order]
    return tuple(outs)
```

```python
import functools

import jax
import jax.numpy as jnp
from jax import lax
from jax.experimental import pallas as pl
from jax.experimental.pallas import tpu as pltpu

NDEV = 8
GRID_W = 64
POOL_WINDOWS = (2, 4, 8, 16)
LRU_C = 8.0
EPS = 1e-6
ADAM_LR = 0.001
ADAM_B1 = 0.9
ADAM_B2 = 0.999
ADAM_EPS = 1e-08
ADAM_WD = 0.01
ADAM_STEP = 10

F32 = jnp.float32
MXU = jnp.bfloat16

VMEM_BYTES = 64 * 1024 * 1024
VMEM_SLACK = 8 * 1024 * 1024
SUB = 8
SUB16 = 16
LANE = 128

TM = 1152
TN = 1024
TK = 576
TL = 256
TL_FINAL = 128
CB_SEQ = 256
CB_SCAN = 128
CB_MIX = 512
TR_CONV = 576

MESH_ID = pl.DeviceIdType.MESH


def _tile(n, pref, align):
    if n <= pref:
        return n
    for t in range(pref - pref % align, 0, -align):
        if n % t == 0:
            return t
    return n


def _nbytes(shape, dtype):
    n = 1
    for s in shape:
        if s is not None:
            n *= s
    return n * jnp.dtype(dtype).itemsize


def _params(blocks, scratch=(), dims=None):
    need = 2 * sum(_nbytes(s, d) for s, d in blocks) + sum(_nbytes(s, d) for s, d in scratch) + VMEM_SLACK
    kw = dict(vmem_limit_bytes=int(min(max(need, 2 * VMEM_SLACK), VMEM_BYTES - VMEM_SLACK // 2)))
    if dims is not None:
        kw["dimension_semantics"] = dims
    return pltpu.CompilerParams(**kw)


def _sds(shape, dtype):
    return jax.ShapeDtypeStruct(tuple(shape), dtype)


ANY = pl.BlockSpec(memory_space=pl.ANY)


def _ids():
    return lax.axis_index("x"), lax.axis_index("y"), lax.axis_index("c")


def _sigmoid(v):
    return jax.nn.sigmoid(v)


def _neg_expm1(v):
    series = -v * (1.0 + v * (0.5 + v * (1.0 / 6.0 + v * (1.0 / 24.0))))
    return jnp.where(v > -1e-2, series, 1.0 - jnp.exp(v))


def _softplus(v):
    return jnp.maximum(v, 0.0) + jnp.log1p(jnp.exp(-jnp.abs(v)))


def _all_gather(xs, name):
    n = len(xs)

    def body(*refs):
        x_refs, o_refs = refs[:n], refs[n:2 * n]
        send_sems, recv_sems, local_sems = refs[2 * n:]
        x, y, c = _ids()
        me, sibling = (x, y, c), (x, y, 1 - c)
        chips = [(1 - x, y), (x, 1 - y), (1 - x, 1 - y)]

        def slot(a, p):
            return o_refs[a].at[4 * p[0] + 2 * p[1] + p[2]]

        def copy(a, k, block, to, src=None):
            return pltpu.make_async_remote_copy(
                src_ref=slot(a, block) if src is None else src, dst_ref=slot(a, block),
                send_sem=send_sems.at[7 * a + k], recv_sem=recv_sems.at[7 * a + k],
                device_id=to, device_id_type=MESH_ID)

        mine, first, passed = [], [], []
        for a in range(n):
            m = pltpu.make_async_copy(x_refs[a], slot(a, me), local_sems.at[a])
            m.start()
            mine.append(m)
            f = [copy(a, 0, me, sibling, src=x_refs[a])]
            f += [copy(a, 1 + j, me, (*chip, c), src=x_refs[a]) for j, chip in enumerate(chips)]
            for cp in f:
                cp.start()
            first += f
        for a in range(n):
            for j, chip in enumerate(chips):
                copy(a, 1 + j, (*chip, c), me).wait_recv()
                p = copy(a, 4 + j, (*chip, c), sibling)
                p.start()
                passed.append(p)
        for a in range(n):
            copy(a, 0, sibling, me).wait_recv()
            for j, chip in enumerate(chips):
                copy(a, 4 + j, (*chip, 1 - c), me).wait_recv()
        for cp in first + passed:
            cp.wait_send()
        for m in mine:
            m.wait()

    return pl.pallas_call(
        body, name=name,
        out_shape=[_sds((NDEV,) + v.shape, v.dtype) for v in xs],
        in_specs=[ANY] * n, out_specs=[ANY] * n,
        scratch_shapes=[pltpu.SemaphoreType.DMA((7 * n,)), pltpu.SemaphoreType.DMA((7 * n,)),
                        pltpu.SemaphoreType.DMA((n,))],
    )(*xs)


HBM = pl.BlockSpec(memory_space=pltpu.HBM)
SEM = pl.BlockSpec(memory_space=pltpu.SEMAPHORE)
EFFECT = pltpu.SideEffectType.DATAFLOW_SIDE_EFFECTING


def _peers():
    x, y, c = _ids()
    out = []
    for k in range(1, NDEV):
        px = 1 - x if k & 4 else x
        py = 1 - y if k & 2 else y
        pc = 1 - c if k & 1 else c
        out.append(((px, py, pc), 4 * px + 2 * py + pc))
    return out, 4 * x + 2 * y + c


def _place(srcs, from_slot, name):
    n = len(srcs)

    def body(*refs):
        s_refs, l_refs, sems = refs[:n], refs[n:2 * n], refs[2 * n]
        x, y, c = _ids()
        me = 4 * x + 2 * y + c
        cps = [pltpu.make_async_copy(s_refs[a].at[me] if from_slot else s_refs[a], l_refs[a].at[me], sems.at[a])
               for a in range(n)]
        for cp in cps:
            cp.start()
        for cp in cps:
            cp.wait()

    shapes = [v.shape if from_slot else (NDEV,) + v.shape for v in srcs]
    return pl.pallas_call(
        body, name=name, out_shape=[_sds(s, v.dtype) for s, v in zip(shapes, srcs)],
        in_specs=[ANY] * n, out_specs=[ANY] * n, scratch_shapes=[pltpu.SemaphoreType.DMA((n,))],
    )(*srcs)


def _send_copies(s_refs, l_refs, ssem, rsem, from_slot):
    peers, me = _peers()
    out = []
    for a in range(len(s_refs)):
        for k, (dev, idx) in enumerate(peers):
            src = s_refs[a].at[idx] if from_slot else s_refs[a]
            send = pltpu.make_async_remote_copy(
                src_ref=src, dst_ref=l_refs[a].at[me], send_sem=ssem.at[7 * a + k], recv_sem=rsem.at[7 * a + k],
                device_id=dev, device_id_type=MESH_ID)
            recv = pltpu.make_async_remote_copy(
                src_ref=src, dst_ref=l_refs[a].at[idx], send_sem=ssem.at[7 * a + k], recv_sem=rsem.at[7 * a + k],
                device_id=dev, device_id_type=MESH_ID)
            out.append((send, recv))
    return out


def _send_start(srcs, lands, from_slot, name):
    n = len(srcs)

    def body(*refs):
        s_refs, l_refs = refs[:n], refs[n:2 * n]
        ssem, rsem = refs[2 * n], refs[2 * n + 1]
        token = refs[-1]
        for send, _ in _send_copies(s_refs, l_refs, ssem, rsem, from_slot):
            send.start()
        token[...] = jnp.zeros_like(token)

    bufs = list(srcs) + list(lands)
    outs = pl.pallas_call(
        body, name=name,
        out_shape=[pltpu.SemaphoreType.DMA((7 * n,)), pltpu.SemaphoreType.DMA((7 * n,))]
        + [pltpu.HBM(v.shape, v.dtype) for v in bufs] + [_sds((SUB, LANE), F32)],
        in_specs=[HBM] * (2 * n), out_specs=[SEM, SEM] + [HBM] * (2 * n) + [pl.BlockSpec(memory_space=pltpu.VMEM)],
        input_output_aliases={i: 2 + i for i in range(2 * n)},
        compiler_params=pltpu.CompilerParams(has_side_effects=EFFECT),
    )(*[pltpu.with_memory_space_constraint(v, pltpu.HBM) for v in bufs])
    return outs[0], outs[1], list(outs[2:2 + n]), list(outs[2 + n:2 + 2 * n]), outs[-1]


def _send_wait(started, after, from_slot, name):
    ssem, rsem, srcs, lands, _ = started
    n = len(srcs)

    def body(*refs):
        s_refs, l_refs = refs[:n], refs[n:2 * n]
        ssem_ref, rsem_ref = refs[2 * n], refs[2 * n + 1]
        for _, recv in _send_copies(s_refs, l_refs, ssem_ref, rsem_ref, from_slot):
            recv.wait_send()
            recv.wait_recv()

    bufs = list(srcs) + list(lands)
    outs = pl.pallas_call(
        body, name=name, out_shape=[pltpu.HBM(v.shape, v.dtype) for v in bufs],
        in_specs=[HBM] * (2 * n) + [SEM, SEM, ANY], out_specs=[HBM] * (2 * n),
        input_output_aliases={i: i for i in range(2 * n)},
        compiler_params=pltpu.CompilerParams(has_side_effects=EFFECT),
    )(*bufs, ssem, rsem, after)
    return list(outs[n:])


NN = (((1,), (0,)), ((), ()))
NT = (((1,), (1,)), ((), ()))
TN_DIMS = (((0,), (0,)), ((), ()))


def _mm(a, b, *, grid, a_spec, b_spec, o_spec, out_shape, acc_shape, dims, name, dep=None):
    k_axis = len(grid) - 1
    nk = grid[k_axis]
    extra = [] if dep is None else [dep]

    def body(a_ref, b_ref, *rest):
        o_ref, acc_ref = rest[-2], rest[-1]
        k = pl.program_id(k_axis)

        @pl.when(k == 0)
        def _():
            acc_ref[...] = jnp.zeros_like(acc_ref)

        acc_ref[...] += lax.dot_general(a_ref[...], b_ref[...], dims, preferred_element_type=F32)

        @pl.when(k == nk - 1)
        def _():
            o_ref[...] = acc_ref[...].astype(o_ref.dtype)

    blocks = [(a_spec.block_shape, a.dtype), (b_spec.block_shape, b.dtype), (o_spec.block_shape, out_shape.dtype)]
    return pl.pallas_call(
        body, name=name, grid=grid, in_specs=[a_spec, b_spec] + [ANY] * len(extra), out_specs=o_spec,
        out_shape=out_shape, scratch_shapes=[pltpu.VMEM(acc_shape, F32)],
        compiler_params=_params(blocks, [(acc_shape, F32)], ("parallel",) * k_axis + ("arbitrary",)),
    )(a, b, *extra)


def _mm_plain(a, b, dims, out_dtype, name):
    if dims == TN_DIMS:
        (K, M), N = a.shape, b.shape[1]
    elif dims == NT:
        (M, K), N = a.shape, b.shape[0]
    else:
        (M, K), N = a.shape, b.shape[1]
    tm, tn = _tile(M, TM, LANE), _tile(N, TN, LANE)
    tk = _tile(K, TK, LANE if dims != TN_DIMS else SUB16)
    if dims == TN_DIMS:
        a_spec = pl.BlockSpec((tk, tm), lambda i, j, k: (k, i))
    else:
        a_spec = pl.BlockSpec((tm, tk), lambda i, j, k: (i, k))
    if dims == NT:
        b_spec = pl.BlockSpec((tn, tk), lambda i, j, k: (j, k))
    else:
        b_spec = pl.BlockSpec((tk, tn), lambda i, j, k: (k, j))
    return _mm(a, b, grid=(M // tm, N // tn, K // tk), a_spec=a_spec, b_spec=b_spec,
               o_spec=pl.BlockSpec((tm, tn), lambda i, j, k: (i, j)),
               out_shape=_sds((M, N), out_dtype), acc_shape=(tm, tn), dims=dims, name=name)


def _mm_proj(h_all, win_all, name):
    n, D = h_all.shape
    nb = win_all.shape[2]
    tm, tn, tk = _tile(n, TM, SUB16), _tile(nb, TN, LANE), _tile(D, TK, LANE)
    nbn = nb // tn
    return _mm(h_all, win_all, grid=(n // tm, NDEV * nbn, D // tk),
               a_spec=pl.BlockSpec((tm, tk), lambda i, j, k: (i, k)),
               b_spec=pl.BlockSpec((None, tk, tn), lambda i, j, k: (j // nbn, k, j % nbn)),
               o_spec=pl.BlockSpec((tm, tn), lambda i, j, k: (i, j)),
               out_shape=_sds((n, NDEV * nb), F32), acc_shape=(tm, tn), dims=NN, name=name)


def _mm_dh(dproj, win_all, name, dep=None):
    n = dproj.shape[0]
    _, D, nb = win_all.shape
    tm, tn, tk = _tile(n, TM, SUB16), _tile(D, TN, LANE), _tile(nb, TK, LANE)
    nbk = nb // tk
    return _mm(dproj, win_all, grid=(n // tm, D // tn, NDEV * nbk),
               a_spec=pl.BlockSpec((tm, tk), lambda i, j, k: (i, k)),
               b_spec=pl.BlockSpec((None, tn, tk), lambda i, j, k: (k // nbk, j, k % nbk)),
               o_spec=pl.BlockSpec((tm, tn), lambda i, j, k: (i, j)),
               out_shape=_sds((n, D), F32), acc_shape=(tm, tn), dims=NT, name=name, dep=dep)


def _mm_gwin(h_all, dproj, nb, name):
    n, D = h_all.shape
    tm, tn, tk = _tile(D, TM, LANE), _tile(nb, TN, LANE), _tile(n, TK, SUB16)
    nbn = nb // tn
    return _mm(h_all, dproj, grid=(D // tm, NDEV * nbn, n // tk),
               a_spec=pl.BlockSpec((tk, tm), lambda i, j, k: (k, i)),
               b_spec=pl.BlockSpec((tk, tn), lambda i, j, k: (k, j)),
               o_spec=pl.BlockSpec((None, tm, tn), lambda i, j, k: (j // nbn, i, j % nbn)),
               out_shape=_sds((NDEV, D, nb), MXU), acc_shape=(tm, tn), dims=TN_DIMS, name=name)


def _mm_group(a, b, mode, out_dtype, name):
    if mode == "wgrad":
        L, W = a.shape
        G = len(POOL_WINDOWS)
        pd = W // G
        tm, tn, tk = _tile(pd, TM, LANE), _tile(pd, TN, LANE), _tile(L, TK, SUB16)
        nm, nn = pd // tm, pd // tn
        return _mm(a, b, grid=(G, nm, nn, L // tk),
                   a_spec=pl.BlockSpec((tk, tm), lambda g, i, j, k: (k, g * nm + i)),
                   b_spec=pl.BlockSpec((tk, tn), lambda g, i, j, k: (k, g * nn + j)),
                   o_spec=pl.BlockSpec((None, tm, tn), lambda g, i, j, k: (g, i, j)),
                   out_shape=_sds((G, pd, pd), out_dtype), acc_shape=(tm, tn), dims=TN_DIMS, name=name)
    L, W = a.shape
    G, pd, _ = b.shape
    tm, tn, tk = _tile(L, TM, SUB16), _tile(pd, TN, LANE), _tile(pd, TK, LANE)
    nn, nk = pd // tn, pd // tk
    if mode == "fwd":
        b_spec = pl.BlockSpec((None, tk, tn), lambda g, i, j, k: (g, k, j))
        dims = NN
    else:
        b_spec = pl.BlockSpec((None, tn, tk), lambda g, i, j, k: (g, j, k))
        dims = NT
    return _mm(a, b, grid=(G, L // tm, nn, nk),
               a_spec=pl.BlockSpec((tm, tk), lambda g, i, j, k: (i, g * nk + k)),
               b_spec=b_spec,
               o_spec=pl.BlockSpec((tm, tn), lambda g, i, j, k: (i, g * nn + j)),
               out_shape=_sds((L, W), out_dtype), acc_shape=(tm, tn), dims=dims, name=name)


def _ada_fwd(cc, w_loc, b_loc, name):
    R, D = cc.shape
    na = w_loc.shape[1]
    tk = _tile(D, 512, LANE)

    def body(c_ref, w_ref, b_ref, mod_ref, s_ref):
        k = pl.program_id(0)
        cv = c_ref[...]
        s = cv * _sigmoid(cv)
        s_ref[...] = s

        @pl.when(k == 0)
        def _():
            mod_ref[...] = jnp.broadcast_to(b_ref[...], mod_ref.shape)

        mod_ref[...] += lax.dot_general(s.astype(MXU), w_ref[...].astype(MXU), NN, preferred_element_type=F32)

    blocks = [((R, tk), F32), ((tk, na), F32), ((1, na), F32), ((R, na), F32), ((R, tk), F32)]
    return pl.pallas_call(
        body, name=name, grid=(D // tk,),
        in_specs=[pl.BlockSpec((R, tk), lambda k: (0, k)), pl.BlockSpec((tk, na), lambda k: (k, 0)),
                  pl.BlockSpec((1, na), lambda k: (0, 0))],
        out_specs=[pl.BlockSpec((R, na), lambda k: (0, 0)), pl.BlockSpec((R, tk), lambda k: (0, k))],
        out_shape=[_sds((R, na), F32), _sds((R, D), F32)],
        compiler_params=_params(blocks, dims=("arbitrary",)),
    )(cc, w_loc, b_loc)


def _adam(w, g, m, v):
    m = ADAM_B1 * m + (1.0 - ADAM_B1) * g
    v = ADAM_B2 * v + (1.0 - ADAM_B2) * (g * g)
    m_hat = m / (1.0 - ADAM_B1 ** ADAM_STEP)
    v_hat = v / (1.0 - ADAM_B2 ** ADAM_STEP)
    delta = -ADAM_LR * (m_hat / (jnp.sqrt(v_hat) + ADAM_EPS) + ADAM_WD * w)
    return delta, m, v


def _ada_bwd(s_all, ga, gc, w_loc, m_loc, v_loc, name):
    D, na = w_loc.shape
    tr = _tile(D, 256, LANE)

    def body(s_ref, ga_ref, gc_ref, w_ref, m_ref, v_ref, g_ref, d_ref, nm_ref, nv_ref, pc_ref):
        dmc = gc_ref[0:1, :]
        for p in range(1, NDEV):
            dmc = dmc + gc_ref[p:p + 1, :]
        rows = lax.broadcasted_iota(jnp.int32, (NDEV, na), 0)
        dmc8 = jnp.where(rows == 0, jnp.broadcast_to(dmc, (NDEV, na)), 0.0)
        dm = jnp.concatenate([ga_ref[...], dmc8], axis=0).astype(MXU)
        dmc16 = jnp.concatenate([dmc8, jnp.zeros_like(dmc8)], axis=0).astype(MXU)
        w = w_ref[...]
        g = lax.dot_general(s_ref[...].astype(MXU), dm, TN_DIMS, preferred_element_type=F32)
        pc_ref[...] = lax.dot_general(dmc16, w.astype(MXU), NT, preferred_element_type=F32)
        delta, nm, nv = _adam(w, g, m_ref[...], v_ref[...])
        g_ref[...] = g
        d_ref[...] = delta
        nm_ref[...] = nm
        nv_ref[...] = nv

    big = pl.BlockSpec((tr, na), lambda i: (i, 0))
    full = pl.BlockSpec((NDEV, na), lambda i: (0, 0))
    srow = pl.BlockSpec((2 * NDEV, tr), lambda i: (0, i))
    blocks = [((2 * NDEV, tr), F32)] * 2 + [((NDEV, na), F32)] * 2 + [((tr, na), F32)] * 7
    return pl.pallas_call(
        body, name=name, grid=(D // tr,),
        in_specs=[srow, full, full, big, big, big],
        out_specs=[big, big, big, big, srow],
        out_shape=[_sds((D, na), F32)] * 4 + [_sds((2 * NDEV, D), F32)],
        compiler_params=_params(blocks, dims=("parallel",)),
    )(s_all, ga, gc, w_loc, m_loc, v_loc)


def _norm_mod(x2, g, shift, scale, n, row0, h_prev, name):
    R, D = x2.shape
    tl = _tile(R, TL, SUB16)
    assert row0 % tl == 0
    b0 = row0 // tl

    def body(x_ref, g_ref, sh_ref, sc_ref, *rest):
        o_ref = rest[-1]
        xv = x_ref[...]
        s = lax.rsqrt(jnp.mean(xv * xv, axis=-1, keepdims=True) + EPS)
        nrm = xv * s * g_ref[...]
        o_ref[...] = (nrm * (1.0 + sc_ref[...]) + sh_ref[...]).astype(o_ref.dtype)

    vec = pl.BlockSpec((1, D), lambda i: (0, 0))
    in_specs = [pl.BlockSpec((tl, D), lambda i: (i, 0)), vec, vec, vec]
    args = [x2, g, shift, scale]
    aliases = {}
    if h_prev is not None:
        in_specs.append(ANY)
        args.append(h_prev)
        aliases = {4: 0}
    blocks = [((tl, D), F32), ((tl, D), MXU)] + [((1, D), F32)] * 3
    return pl.pallas_call(
        body, name=name, grid=(R // tl,), in_specs=in_specs,
        out_specs=pl.BlockSpec((tl, D), lambda i: (i + b0, 0)),
        out_shape=_sds((n, D), MXU), input_output_aliases=aliases,
        compiler_params=_params(blocks, dims=("parallel",)),
    )(*args)


def _norm_bwd(x2, dh_all, row0, g, scale, dxn, ggn0, name):
    R, D = x2.shape
    tl = _tile(R, TL_FINAL, SUB)
    assert row0 % tl == 0
    b0 = row0 // tl
    with_x = dxn is not None

    def body(*refs):
        if with_x:
            x_ref, dh_ref, g_ref, sc_ref, gg0_ref, dxn_ref, gx_ref, dsh_ref, dsc_ref, gg_ref = refs
        else:
            x_ref, dh_ref, g_ref, sc_ref, gg0_ref, dsh_ref, dsc_ref, gg_ref = refs
        i = pl.program_id(0)

        @pl.when(i == 0)
        def _():
            dsh_ref[...] = jnp.zeros_like(dsh_ref)
            dsc_ref[...] = jnp.zeros_like(dsc_ref)
            gg_ref[...] = gg0_ref[...]

        xv = x_ref[...]
        dh = dh_ref[...]
        gv = g_ref[...]
        s = lax.rsqrt(jnp.mean(xv * xv, axis=-1, keepdims=True) + EPS)
        xh = xv * s
        dsh_ref[...] += jnp.sum(dh, axis=0, keepdims=True)
        dsc_ref[...] += jnp.sum(dh * (xh * gv), axis=0, keepdims=True)
        dn = dh * (1.0 + sc_ref[...])
        gg_ref[...] += jnp.sum(dn * xh, axis=0, keepdims=True)
        if with_x:
            dxh = dn * gv
            dx = s * (dxh - xh * jnp.mean(dxh * xh, axis=-1, keepdims=True))
            gx_ref[...] = dx + dxn_ref[...]

    vec = pl.BlockSpec((1, D), lambda i: (0, 0))
    row = pl.BlockSpec((tl, D), lambda i: (i, 0))
    in_specs = [row, pl.BlockSpec((tl, D), lambda i: (i + b0, 0)), vec, vec, vec]
    args = [x2, dh_all, g, scale, ggn0]
    out_specs = [vec, vec, vec]
    out_shape = [_sds((1, D), F32)] * 3
    if with_x:
        in_specs.append(row)
        args.append(dxn)
        out_specs = [row] + out_specs
        out_shape = [_sds((R, D), F32)] + out_shape
    blocks = [((tl, D), F32)] * (4 if with_x else 2) + [((1, D), F32)] * 6
    outs = pl.pallas_call(
        body, name=name, grid=(R // tl,), in_specs=in_specs, out_specs=out_specs, out_shape=out_shape,
        compiler_params=_params(blocks, dims=("arbitrary",)),
    )(*args)
    return tuple(outs) if with_x else (None,) + tuple(outs)


def _tap_valid(t, o, lc, n):
    tt = t + o
    in_ctx = t < lc
    return (tt >= jnp.where(in_ctx, 0, lc)) & (tt < jnp.where(in_ctx, lc, n))


def _conv_fwd(proj_all, cw, cb, lc, W, name):
    n = proj_all.shape[0]
    cbk = _tile(W, CB_SEQ, LANE)
    tr = _tile(n, TR_CONV, SUB16)
    ext = tr + 2 * SUB

    def body(x_ref, w_ref, b_ref, u_ref, xp_ref):
        xp_ref[0:SUB, :] = jnp.zeros((SUB, cbk), F32)
        xp_ref[n + SUB:n + 2 * SUB, :] = jnp.zeros((SUB, cbk), F32)
        xp_ref[SUB:n + SUB, :] = x_ref[...]
        w = w_ref[...]
        bias = b_ref[...]

        def chunk(ci, carry):
            r0 = pl.multiple_of(ci * tr, SUB16)
            xe = xp_ref[pl.ds(r0, ext), :]
            t = r0 + lax.broadcasted_iota(jnp.int32, (tr, cbk), 0)
            acc = jnp.broadcast_to(bias, (tr, cbk))
            for k in range(4):
                o = k - 1
                sh = xe if o == 0 else pltpu.roll(xe, (-o) % ext, 0)
                acc = acc + jnp.where(_tap_valid(t, o, lc, n), sh[SUB:tr + SUB], 0.0) * w[k:k + 1]
            u_ref[pl.ds(r0, tr), :] = acc
            return carry

        lax.fori_loop(0, n // tr, chunk, 0)

    blocks = [((n, cbk), F32)] * 2 + [((4, cbk), F32), ((1, cbk), F32)]
    scratch = [((n + 2 * SUB, cbk), F32)]
    return pl.pallas_call(
        body, name=name, grid=(W // cbk,),
        in_specs=[pl.BlockSpec((n, cbk), lambda j: (0, j)), pl.BlockSpec((4, cbk), lambda j: (0, j)),
                  pl.BlockSpec((1, cbk), lambda j: (0, j))],
        out_specs=pl.BlockSpec((n, cbk), lambda j: (0, j)),
        out_shape=_sds((n, W), F32),
        scratch_shapes=[pltpu.VMEM(s, d) for s, d in scratch],
        compiler_params=_params(blocks, scratch, ("parallel",)),
    )(proj_all, cw, cb)


def _conv_bwd(du_all, proj_all, cw, dproj, lc, W, name):
    n = du_all.shape[0]
    cbk = _tile(W, CB_SEQ, LANE)
    tr = _tile(n, TR_CONV, SUB16)
    ext = tr + 2 * SUB

    def body(du_ref, x_ref, w_ref, dp_in, dx_ref, gw_ref, gb_ref, dp_ref, xp_ref):
        del dp_in
        for ref, src in ((dp_ref, du_ref), (xp_ref, x_ref)):
            ref[0:SUB, :] = jnp.zeros((SUB, cbk), F32)
            ref[n + SUB:n + 2 * SUB, :] = jnp.zeros((SUB, cbk), F32)
            ref[SUB:n + SUB, :] = src[...]
        w = w_ref[...]

        def fold(v):
            return jnp.sum(v.reshape(tr // SUB, SUB, cbk), axis=0)

        def chunk(ci, carry):
            r0 = pl.multiple_of(ci * tr, SUB16)
            de = dp_ref[pl.ds(r0, ext), :]
            xe = xp_ref[pl.ds(r0, ext), :]
            t = r0 + lax.broadcasted_iota(jnp.int32, (tr, cbk), 0)
            d0 = de[SUB:tr + SUB]
            dx = jnp.zeros((tr, cbk), F32)
            new = []
            for k in range(4):
                o = k - 1
                dsh = de if o == 0 else pltpu.roll(de, o % ext, 0)
                dx = dx + jnp.where(_tap_valid(t, -o, lc, n), dsh[SUB:tr + SUB], 0.0) * w[k:k + 1]
                xsh = xe if o == 0 else pltpu.roll(xe, (-o) % ext, 0)
                new.append(carry[k] + fold(d0 * jnp.where(_tap_valid(t, o, lc, n), xsh[SUB:tr + SUB], 0.0)))
            new.append(carry[4] + fold(d0))
            dx_ref[pl.ds(r0, tr), :] = dx.astype(dx_ref.dtype)
            return tuple(new)

        zero = jnp.zeros((SUB, cbk), F32)
        acc = lax.fori_loop(0, n // tr, chunk, (zero,) * 5)
        for k in range(4):
            gw_ref[k:k + 1, :] = jnp.sum(acc[k], axis=0, keepdims=True)
        gb_ref[...] = jnp.sum(acc[4], axis=0, keepdims=True)

    col = pl.BlockSpec((n, cbk), lambda j: (0, j))
    blocks = [((n, cbk), F32)] * 2 + [((n, cbk), MXU), ((4, cbk), F32), ((4, cbk), F32), ((1, cbk), F32)]
    scratch = [((n + 2 * SUB, cbk), F32)] * 2
    return pl.pallas_call(
        body, name=name, grid=(W // cbk,),
        in_specs=[col, col, pl.BlockSpec((4, cbk), lambda j: (0, j)), ANY],
        out_specs=[col, pl.BlockSpec((4, cbk), lambda j: (0, j)), pl.BlockSpec((1, cbk), lambda j: (0, j))],
        out_shape=[_sds(dproj.shape, dproj.dtype), _sds((4, W), F32), _sds((1, W), F32)],
        input_output_aliases={3: 0},
        scratch_shapes=[pltpu.VMEM(s, d) for s, d in scratch],
        compiler_params=_params(blocks, scratch, ("parallel",)),
    )(du_all, proj_all, cw, dproj)


def _gate_coeffs(ub, u, d, wr_ref, wi_ref, br_ref, bi_ref, lam_ref):
    c = -LRU_C * _softplus(-lam_ref[d:d + 1, :])
    r = _sigmoid(lax.dot_general(ub, wr_ref[d], NN, preferred_element_type=F32) + br_ref[d:d + 1, :])
    ig = _sigmoid(lax.dot_general(ub, wi_ref[d], NN, preferred_element_type=F32) + bi_ref[d:d + 1, :])
    la = c * r
    a = jnp.exp(la)
    sq = jnp.sqrt(_neg_expm1(2.0 * la))
    return c, r, ig, a, sq


def _gate_specs(tl, hd):
    w_spec = pl.BlockSpec((2, None, hd, hd), lambda h, i: (0, h, 0, 0))
    v_spec = pl.BlockSpec((2, hd), lambda h, i: (0, h))
    return w_spec, v_spec


def _gates_fwd(u_all, wr, wi, br, bi, lam, name):
    n, W = u_all.shape
    heads, hd = wr.shape[1], wr.shape[2]
    tl = _tile(n, TL, SUB16)

    def body(u_ref, wr_ref, wi_ref, br_ref, bi_ref, lam_ref, a_ref, b_ref):
        u = u_ref[...]
        ub = u.astype(MXU)
        for d in range(2):
            _, _, ig, a, sq = _gate_coeffs(ub, u, d, wr_ref, wi_ref, br_ref, bi_ref, lam_ref)
            a_ref[d] = a
            b_ref[d] = sq * (ig * u)

    w_spec, v_spec = _gate_specs(tl, hd)
    o_spec = pl.BlockSpec((2, tl, hd), lambda h, i: (0, i, h))
    blocks = [((tl, hd), F32), ((2, hd, hd), MXU), ((2, hd, hd), MXU)] + [((2, hd), F32)] * 3 + [((2, tl, hd), F32)] * 2
    return pl.pallas_call(
        body, name=name, grid=(heads, n // tl),
        in_specs=[pl.BlockSpec((tl, hd), lambda h, i: (i, h)), w_spec, w_spec, v_spec, v_spec, v_spec],
        out_specs=[o_spec, o_spec], out_shape=[_sds((2, n, W), F32)] * 2,
        compiler_params=_params(blocks, dims=("parallel", "parallel")),
    )(u_all, wr, wi, br, bi, lam)


def _gates_bwd(u_all, da, db, wr, wi, br, bi, lam, name):
    n, W = u_all.shape
    heads, hd = wr.shape[1], wr.shape[2]
    tl = _tile(n, TL, SUB16)
    ni = n // tl

    def body(u_ref, da_ref, db_ref, wr_ref, wi_ref, br_ref, bi_ref, lam_ref,
             du_ref, gwr_ref, gwi_ref, gbr_ref, gbi_ref, gc_ref, accr_ref, acci_ref):
        i = pl.program_id(1)

        @pl.when(i == 0)
        def _():
            accr_ref[...] = jnp.zeros_like(accr_ref)
            acci_ref[...] = jnp.zeros_like(acci_ref)
            gbr_ref[...] = jnp.zeros_like(gbr_ref)
            gbi_ref[...] = jnp.zeros_like(gbi_ref)
            gc_ref[...] = jnp.zeros_like(gc_ref)

        u = u_ref[...]
        ub = u.astype(MXU)
        du = jnp.zeros_like(u)
        for d in range(2):
            c, r, ig, a, sq = _gate_coeffs(ub, u, d, wr_ref, wi_ref, br_ref, bi_ref, lam_ref)
            dbv = db_ref[d]
            t = dbv * sq
            du = du + t * ig
            d_la = da_ref[d] * a - (dbv * ig * u) * (a * a) / sq
            gc_ref[d:d + 1, :] += jnp.sum(d_la * r, axis=0, keepdims=True)
            d_pr = (d_la * c) * (r * (1.0 - r))
            d_pi = (t * u) * (ig * (1.0 - ig))
            gbr_ref[d:d + 1, :] += jnp.sum(d_pr, axis=0, keepdims=True)
            gbi_ref[d:d + 1, :] += jnp.sum(d_pi, axis=0, keepdims=True)
            pb = d_pr.astype(MXU)
            qb = d_pi.astype(MXU)
            du = du + lax.dot_general(pb, wr_ref[d], NT, preferred_element_type=F32)
            du = du + lax.dot_general(qb, wi_ref[d], NT, preferred_element_type=F32)
            accr_ref[d] += lax.dot_general(ub, pb, TN_DIMS, preferred_element_type=F32)
            acci_ref[d] += lax.dot_general(ub, qb, TN_DIMS, preferred_element_type=F32)
        du_ref[...] = du

        @pl.when(i == ni - 1)
        def _():
            gwr_ref[...] = accr_ref[...].astype(gwr_ref.dtype)
            gwi_ref[...] = acci_ref[...].astype(gwi_ref.dtype)

    w_spec, v_spec = _gate_specs(tl, hd)
    u_spec = pl.BlockSpec((tl, hd), lambda h, i: (i, h))
    ab_spec = pl.BlockSpec((2, tl, hd), lambda h, i: (0, i, h))
    blocks = ([((tl, hd), F32)] * 2 + [((2, tl, hd), F32)] * 2 + [((2, hd, hd), MXU)] * 4 + [((2, hd), F32)] * 6)
    scratch = [((2, hd, hd), F32)] * 2
    return pl.pallas_call(
        body, name=name, grid=(heads, ni),
        in_specs=[u_spec, ab_spec, ab_spec, w_spec, w_spec, v_spec, v_spec, v_spec],
        out_specs=[u_spec, w_spec, w_spec, v_spec, v_spec, v_spec],
        out_shape=[_sds((n, W), F32), _sds(wr.shape, MXU), _sds(wi.shape, MXU)] + [_sds((2, W), F32)] * 3,
        scratch_shapes=[pltpu.VMEM(s, d) for s, d in scratch],
        compiler_params=_params(blocks, scratch, ("parallel", "arbitrary")),
    )(u_all, da, db, wr, wi, br, bi, lam)


def _tile_scan(A, B, rows, reverse):
    for s in (1, 2, 4):
        if reverse:
            As, Bs, m = pltpu.roll(A, SUB - s, 0), pltpu.roll(B, SUB - s, 0), rows < SUB - s
        else:
            As, Bs, m = pltpu.roll(A, s, 0), pltpu.roll(B, s, 0), rows >= s
        B = jnp.where(m, A * Bs + B, B)
        A = jnp.where(m, A * As, A)
    return A, B


def _scan_fwd(a_all, b_all, lc, name):
    _, n, W = a_all.shape
    cb = _tile(W, CB_SCAN, LANE)
    nt, ntc = n // SUB, lc // SUB

    def body(a_ref, b_ref, h_ref):
        rows = lax.broadcasted_iota(jnp.int32, (SUB, cb), 0)

        def step(s, carry):
            h0, h1 = carry
            r0 = pl.multiple_of(s * SUB, SUB)
            A, B = _tile_scan(a_ref[0, pl.ds(r0, SUB), :], b_ref[0, pl.ds(r0, SUB), :], rows, False)
            H0 = A * h0 + B
            h_ref[0, pl.ds(r0, SUB), :] = H0
            j1 = jnp.where(s < ntc, ntc - 1 - s, nt - 1 - (s - ntc))
            r1 = pl.multiple_of(j1 * SUB, SUB)
            A, B = _tile_scan(a_ref[1, pl.ds(r1, SUB), :], b_ref[1, pl.ds(r1, SUB), :], rows, True)
            H1 = A * h1 + B
            h_ref[1, pl.ds(r1, SUB), :] = H1
            return H0[SUB - 1:SUB, :], H1[0:1, :]

        zero = jnp.zeros((1, cb), F32)
        lax.fori_loop(0, nt, step, (zero, zero))

    spec = pl.BlockSpec((2, n, cb), lambda j: (0, 0, j))
    return pl.pallas_call(
        body, name=name, grid=(W // cb,), in_specs=[spec, spec], out_specs=spec,
        out_shape=_sds((2, n, W), F32),
        compiler_params=_params([((2, n, cb), F32)] * 3, dims=("parallel",)),
    )(a_all, b_all)


def _scan_bwd(a_all, h_all, dya, lc, name):
    _, n, W = a_all.shape
    cb = _tile(W, CB_SCAN, LANE)
    nt, ntc = n // SUB, lc // SUB
    nl = nt - ntc

    def body(a_ref, h_ref, g_ref, da_ref, db_ref):
        rows = lax.broadcasted_iota(jnp.int32, (SUB, cb), 0)

        def tile(ref, d, j):
            return ref[d, pl.ds(pl.multiple_of(j * SUB, SUB), SUB), :]

        def grad_tile(j):
            jl = jnp.maximum(j - ntc, 0)
            g = g_ref[pl.ds(pl.multiple_of(jl * SUB, SUB), SUB), :]
            return jnp.where(j >= ntc, g, 0.0)

        def put(ref, d, j, v):
            ref[d, pl.ds(pl.multiple_of(j * SUB, SUB), SUB), :] = v

        def step(s, carry):
            mu0, mu1 = carry
            j = nt - 1 - s
            a_t = tile(a_ref, 0, j)
            ap = jnp.where(rows < SUB - 1, pltpu.roll(a_t, SUB - 1, 0), 1.0)
            A, B = _tile_scan(ap, grad_tile(j), rows, True)
            lam = A * mu0 + B
            below = jnp.where(j > 0, tile(h_ref, 0, jnp.maximum(j - 1, 0))[SUB - 1:SUB, :], 0.0)
            hprev = jnp.where(rows >= 1, pltpu.roll(tile(h_ref, 0, j), 1, 0), below)
            put(da_ref, 0, j, lam * hprev)
            put(db_ref, 0, j, lam)
            mu0 = a_t[0:1, :] * lam[0:1, :]
            j = jnp.where(s < nl, ntc + s, s - nl)
            a_t = tile(a_ref, 1, j)
            ap = jnp.where(rows >= 1, pltpu.roll(a_t, 1, 0), 1.0)
            A, B = _tile_scan(ap, grad_tile(j), rows, False)
            lam = A * mu1 + B
            jn = jnp.where(j == nt - 1, 0, jnp.minimum(j + 1, nt - 1))
            above = jnp.where(j == ntc - 1, 0.0, tile(h_ref, 1, jn)[0:1, :])
            hprev = jnp.where(rows < SUB - 1, pltpu.roll(tile(h_ref, 1, j), SUB - 1, 0), above)
            put(da_ref, 1, j, lam * hprev)
            put(db_ref, 1, j, lam)
            mu1 = a_t[SUB - 1:SUB, :] * lam[SUB - 1:SUB, :]
            return mu0, mu1

        zero = jnp.zeros((1, cb), F32)
        lax.fori_loop(0, nt, step, (zero, zero))

    spec = pl.BlockSpec((2, n, cb), lambda j: (0, 0, j))
    g_spec = pl.BlockSpec((n - lc, cb), lambda j: (0, j))
    return pl.pallas_call(
        body, name=name, grid=(W // cb,), in_specs=[spec, spec, g_spec], out_specs=[spec, spec],
        out_shape=[_sds((2, n, W), F32)] * 2,
        compiler_params=_params([((2, n, cb), F32)] * 4 + [((n - lc, cb), F32)], dims=("parallel",)),
    )(a_all, h_all, dya)


def _pool_window(v, w, tl, cb, transpose):
    left = w // 2
    right = w - 1 - left
    pos = lax.broadcasted_iota(jnp.int32, (tl, cb), 0) % GRID_W
    cnt = (jnp.minimum(pos + right, GRID_W - 1) - jnp.maximum(pos - left, 0) + 1).astype(F32)
    src = v / cnt if transpose else v
    lo, hi = (-right, left) if transpose else (-left, right)
    acc = src
    for o in range(lo, hi + 1):
        if o != 0:
            ok = (pos + o >= 0) & (pos + o < GRID_W)
            acc = acc + jnp.where(ok, pltpu.roll(src, (-o) % tl, 0), 0.0)
    return acc - v if transpose else acc / cnt - v


def _pool_z(src, row0, col0, L, W, transpose, dproj, name):
    G = len(POOL_WINDOWS)
    pd = W // G
    tl = _tile(L, TL, GRID_W)
    cb = _tile(pd, CB_SEQ, LANE)
    assert row0 % tl == 0 and col0 % cb == 0
    rb, cbk = row0 // tl, col0 // cb
    nj = pd // cb

    def body(x_ref, *rest):
        o_ref = rest[-1]
        for gi, w in enumerate(POOL_WINDOWS):
            @pl.when(pl.program_id(0) == gi)
            def _(w=w):
                o_ref[...] = _pool_window(x_ref[...], w, tl, cb, transpose).astype(o_ref.dtype)

    plain = pl.BlockSpec((tl, cb), lambda g, i, j: (i, g * nj + j))
    window = pl.BlockSpec((tl, cb), lambda g, i, j: (i + rb, cbk + g * nj + j))
    blocks = [((tl, cb), F32), ((tl, cb), MXU)]
    if transpose:
        return pl.pallas_call(
            body, name=name, grid=(G, L // tl, nj), in_specs=[plain, ANY], out_specs=window,
            out_shape=_sds(dproj.shape, dproj.dtype), input_output_aliases={1: 0},
            compiler_params=_params(blocks, dims=("parallel",) * 3),
        )(src, dproj)
    return pl.pallas_call(
        body, name=name, grid=(G, L // tl, nj), in_specs=[window], out_specs=plain,
        out_shape=_sds((L, W), MXU),
        compiler_params=_params(blocks, dims=("parallel",) * 3),
    )(src)


def _mix_fwd(hs, proj_all, ypre, b_pool, pool_scale, lc, name):
    L, W = ypre.shape
    tl = _tile(L, TL, SUB16)
    cb = _tile(W, CB_MIX, LANE)
    nj = W // cb
    assert lc % tl == 0
    rb = lc // tl

    def body(hs_ref, ga_ref, yp_ref, gb_ref, bp_ref, ps_ref, o_ref):
        p = pl.program_id(2)

        @pl.when(p == 0)
        def _():
            g = ga_ref[...]
            o_ref[...] = ((hs_ref[0] + hs_ref[1]) * (g * _sigmoid(g))).astype(o_ref.dtype)

        @pl.when(p == 1)
        def _():
            g = gb_ref[...]
            yb = (yp_ref[...] + bp_ref[...]) * ps_ref[...]
            o_ref[...] = (yb * (g * _sigmoid(g))).astype(o_ref.dtype)

    vec = pl.BlockSpec((1, cb), lambda i, j, p: (0, j))
    blocks = [((2, tl, cb), F32)] + [((tl, cb), F32)] * 3 + [((tl, cb), MXU)]
    return pl.pallas_call(
        body, name=name, grid=(L // tl, nj, 2),
        in_specs=[pl.BlockSpec((2, tl, cb), lambda i, j, p: (0, i + rb, j)),
                  pl.BlockSpec((tl, cb), lambda i, j, p: (i + rb, 2 * nj + j)),
                  pl.BlockSpec((tl, cb), lambda i, j, p: (i, j)),
                  pl.BlockSpec((tl, cb), lambda i, j, p: (i + rb, 3 * nj + j)), vec, vec],
        out_specs=pl.BlockSpec((tl, cb), lambda i, j, p: (i, p * nj + j)),
        out_shape=_sds((L, 2 * W), MXU),
        compiler_params=_params(blocks, dims=("parallel", "parallel", "arbitrary")),
    )(hs, proj_all, ypre, proj_all, b_pool, pool_scale)


def _dsilu(g, sg):
    return sg * (1.0 + g * (1.0 - sg))


def _mixa_bwd(dmixed, hs, proj_all, dproj, lc, W, name):
    L = dmixed.shape[0]
    tl = _tile(L, TL, SUB16)
    cb = _tile(W, CB_MIX, LANE)
    nj = W // cb
    rb = lc // tl

    def body(dm_ref, hs_ref, ga_ref, dp_in, dya_ref, dga_ref):
        del dp_in
        g = ga_ref[...]
        sg = _sigmoid(g)
        dm = dm_ref[...]
        dya_ref[...] = dm * (g * sg)
        dga_ref[...] = (dm * (hs_ref[0] + hs_ref[1]) * _dsilu(g, sg)).astype(dga_ref.dtype)

    blocks = [((tl, cb), F32)] * 3 + [((2, tl, cb), F32), ((tl, cb), MXU)]
    return pl.pallas_call(
        body, name=name, grid=(L // tl, nj),
        in_specs=[pl.BlockSpec((tl, cb), lambda i, j: (i, j)),
                  pl.BlockSpec((2, tl, cb), lambda i, j: (0, i + rb, j)),
                  pl.BlockSpec((tl, cb), lambda i, j: (i + rb, 2 * nj + j)), ANY],
        out_specs=[pl.BlockSpec((tl, cb), lambda i, j: (i, j)),
                   pl.BlockSpec((tl, cb), lambda i, j: (i + rb, 2 * nj + j))],
        out_shape=[_sds((L, W), F32), _sds(dproj.shape, dproj.dtype)],
        input_output_aliases={3: 1},
        compiler_params=_params(blocks, dims=("parallel", "parallel")),
    )(dmixed, hs, proj_all, dproj)


def _mixb_bwd(dmixed, ypre, proj_all, b_pool, pool_scale, dproj, lc, W, name):
    L = dmixed.shape[0]
    tl = _tile(L, TL, SUB16)
    cb = _tile(W, CB_MIX, LANE)
    nj = W // cb
    rb = lc // tl

    def body(dm_ref, yp_ref, gb_ref, bp_ref, ps_ref, dp_in, dyp_ref, dgb_ref, gbp_ref, gps_ref):
        del dp_in
        i = pl.program_id(1)

        @pl.when(i == 0)
        def _():
            gbp_ref[...] = jnp.zeros_like(gbp_ref)
            gps_ref[...] = jnp.zeros_like(gps_ref)

        g = gb_ref[...]
        sg = _sigmoid(g)
        dm = dm_ref[...]
        yp = yp_ref[...] + bp_ref[...]
        ps = ps_ref[...]
        dyb = dm * (g * sg)
        dyp = dyb * ps
        dgb_ref[...] = (dm * (yp * ps) * _dsilu(g, sg)).astype(dgb_ref.dtype)
        dyp_ref[...] = dyp.astype(dyp_ref.dtype)
        gbp_ref[...] += jnp.sum(dyp, axis=0, keepdims=True)
        gps_ref[...] += jnp.sum(dyb * yp, axis=0, keepdims=True)

    vec = pl.BlockSpec((1, cb), lambda j, i: (0, j))
    blocks = [((tl, cb), F32)] * 3 + [((tl, cb), MXU)] * 2 + [((1, cb), F32)] * 4
    return pl.pallas_call(
        body, name=name, grid=(nj, L // tl),
        in_specs=[pl.BlockSpec((tl, cb), lambda j, i: (i, nj + j)),
                  pl.BlockSpec((tl, cb), lambda j, i: (i, j)),
                  pl.BlockSpec((tl, cb), lambda j, i: (i + rb, 3 * nj + j)), vec, vec, ANY],
        out_specs=[pl.BlockSpec((tl, cb), lambda j, i: (i, j)),
                   pl.BlockSpec((tl, cb), lambda j, i: (i + rb, 3 * nj + j)), vec, vec],
        out_shape=[_sds((L, W), MXU), _sds(dproj.shape, dproj.dtype), _sds((1, W), F32), _sds((1, W), F32)],
        input_output_aliases={5: 1},
        compiler_params=_params(blocks, dims=("parallel", "arbitrary")),
    )(dmixed, ypre, proj_all, b_pool, pool_scale, dproj)


def _dproj_init(n, lc, W, name):
    cb = _tile(W, CB_MIX, LANE)
    nj = W // cb

    def body(o_ref):
        o_ref[...] = jnp.zeros_like(o_ref)

    return pl.pallas_call(
        body, name=name, grid=(3 * nj,), in_specs=[],
        out_specs=pl.BlockSpec((lc, cb), lambda j: (0, nj + j)),
        out_shape=_sds((n, 4 * W), MXU),
        compiler_params=_params([((lc, cb), MXU)], dims=("parallel",)),
    )()


def _final(x2, out, tgt, gate, gfin, name):
    L, D = x2.shape
    tl = _tile(L, TL_FINAL, SUB16)

    def body(x_ref, o_ref, t_ref, gate_ref, g_ref, dout_ref, dxn_ref, loss_ref, ggf_ref, dgate_ref):
        i = pl.program_id(0)

        @pl.when(i == 0)
        def _():
            loss_ref[...] = jnp.zeros_like(loss_ref)
            ggf_ref[...] = jnp.zeros_like(ggf_ref)
            dgate_ref[...] = jnp.zeros_like(dgate_ref)

        o = o_ref[...]
        gate_v = gate_ref[...]
        gv = g_ref[...]
        xn = x_ref[...] + gate_v * o
        s = lax.rsqrt(jnp.mean(xn * xn, axis=-1, keepdims=True) + EPS)
        xh = xn * s
        err = xh * gv - t_ref[...]
        tok = jnp.mean(err * err, axis=-1, keepdims=True)
        loss_ref[...] += 0.5 * jnp.sum(tok, axis=0, keepdims=True)
        dy = err / D
        ggf_ref[...] += jnp.sum(dy * xh, axis=0, keepdims=True)
        dxh = dy * gv
        dxn = s * (dxh - xh * jnp.mean(dxh * xh, axis=-1, keepdims=True))
        dgate_ref[...] += jnp.sum(dxn * o, axis=0, keepdims=True)
        dout_ref[...] = (gate_v * dxn).astype(dout_ref.dtype)
        dxn_ref[...] = dxn

    row = pl.BlockSpec((tl, D), lambda i: (i, 0))
    vec = pl.BlockSpec((1, D), lambda i: (0, 0))
    blocks = [((tl, D), F32)] * 4 + [((tl, D), MXU)] + [((1, D), F32)] * 4
    return pl.pallas_call(
        body, name=name, grid=(L // tl,), in_specs=[row, row, row, vec, vec],
        out_specs=[row, row, pl.BlockSpec((1, 1), lambda i: (0, 0)), vec, vec],
        out_shape=[_sds((L, D), MXU), _sds((L, D), F32), _sds((1, 1), F32), _sds((1, D), F32), _sds((1, D), F32)],
        compiler_params=_params(blocks, dims=("arbitrary",)),
    )(x2, out, tgt, gate, gfin)


def _adamw_parts(w2, parts, m2, v2, name):
    R, C = w2.shape
    tr = _tile(R, max(SUB16, (256 * 1024) // C), SUB16)

    def body(w_ref, p_ref, m_ref, v_ref, g_ref, d_ref, nm_ref, nv_ref):
        g = p_ref[0].astype(F32)
        for p in range(1, NDEV):
            g = g + p_ref[p].astype(F32)
        delta, nm, nv = _adam(w_ref[...], g, m_ref[...], v_ref[...])
        g_ref[...] = g
        d_ref[...] = delta
        nm_ref[...] = nm
        nv_ref[...] = nv

    row = pl.BlockSpec((tr, C), lambda i: (i, 0))
    blocks = [((tr, C), F32)] * 7 + [((NDEV, tr, C), parts.dtype)]
    return pl.pallas_call(
        body, name=name, grid=(R // tr,),
        in_specs=[row, pl.BlockSpec((NDEV, tr, C), lambda i: (0, i, 0)), row, row],
        out_specs=[row] * 4, out_shape=[_sds((R, C), F32)] * 4,
        compiler_params=_params(blocks, dims=("parallel",)),
    )(w2, parts, m2, v2)


def _small_sum(vs, ga, gc, name):
    ns, nm = vs.shape[1], ga.shape[1]

    def body(v_ref, ga_ref, gc_ref, tot_ref, gb_ref):
        tot = v_ref[0:1, :]
        gb = ga_ref[0:1, :]
        for p in range(1, NDEV):
            tot = tot + v_ref[p:p + 1, :]
            gb = gb + ga_ref[p:p + 1, :]
        for p in range(NDEV):
            gb = gb + gc_ref[p:p + 1, :]
        tot_ref[...] = tot
        gb_ref[...] = gb

    blocks = [((NDEV, ns), F32), ((NDEV, nm), F32), ((NDEV, nm), F32), ((1, ns), F32), ((1, nm), F32)]
    return pl.pallas_call(
        body, name=name, out_shape=[_sds((1, ns), F32), _sds((1, nm), F32)],
        compiler_params=_params(blocks),
    )(vs, ga, gc)


def _adamw_small(g_raw, w, m, v, lam_range, cctx_range, name):
    npk = w.shape[1]

    def body(g_ref, w_ref, m_ref, v_ref, go_ref, d_ref, nm_ref, nv_ref):
        wv = w_ref[...]
        g = g_ref[...]
        idx = lax.broadcasted_iota(jnp.int32, (1, npk), 1)
        in_lam = (idx >= lam_range[0]) & (idx < lam_range[1])
        in_cc = (idx >= cctx_range[0]) & (idx < cctx_range[1])
        sg = _sigmoid(wv)
        g = jnp.where(in_lam, g * (LRU_C * _sigmoid(-wv)), jnp.where(in_cc, g * _dsilu(wv, sg), g))
        delta, nm, nv = _adam(wv, g, m_ref[...], v_ref[...])
        go_ref[...] = g
        d_ref[...] = delta
        nm_ref[...] = nm
        nv_ref[...] = nv

    return pl.pallas_call(
        body, name=name, out_shape=[_sds((1, npk), F32)] * 4,
        compiler_params=_params([((1, npk), F32)] * 8),
    )(g_raw, w, m, v)


def _pack(pieces):
    return jnp.concatenate([p.reshape(1, -1) for p in pieces], axis=1)


def kernel(x, c, ctx, c_ctx, w_ada, b_ada, g_norm, w_in, conv_w, conv_b, lru_lambda, w_rgate, b_rgate, w_igate, b_igate, w_pool, b_pool, pool_scale, w_out, g_final, loss_target, m_c_ctx, m_w_ada, m_b_ada, m_g_norm, m_w_in, m_conv_w, m_conv_b, m_lru_lambda, m_w_rgate, m_b_rgate, m_w_igate, m_b_igate, m_w_pool, m_b_pool, m_pool_scale, m_w_out, m_g_final, v_c_ctx, v_w_ada, v_b_ada, v_g_norm, v_w_in, v_conv_w, v_conv_b, v_lru_lambda, v_w_rgate, v_b_rgate, v_w_igate, v_b_igate, v_w_pool, v_b_pool, v_pool_scale, v_w_out, v_g_final):
    L, D = x.shape[1], x.shape[2]
    lc = ctx.shape[1]
    n = lc + L
    W = conv_b.shape[1]
    heads, hd = w_rgate.shape[2], w_rgate.shape[4]
    G, pd = w_pool.shape[1], w_pool.shape[3]
    na = w_ada.shape[2]
    nb = w_in.shape[2]
    ws = W // NDEV
    me = 4 * lax.axis_index("x") + 2 * lax.axis_index("y") + lax.axis_index("c")

    (win_all, cw_all, lam_all, br_all, bi_all, c_all) = _all_gather(
        [w_in[0].astype(MXU), conv_w[0], lru_lambda[0], b_rgate[0], b_igate[0], c], "gather_w_in")
    gate_w = [w_rgate[0].astype(MXU), w_igate[0].astype(MXU)]
    sent_gw = _send_start(gate_w, _place(gate_w, False, "place_gate_w"), False, "start_gate_w")
    rest_w = [w_pool[0].astype(MXU), w_out[0].astype(MXU)]
    sent_rw = _send_start(rest_w, _place(rest_w, False, "place_rest_w"), False, "start_rest_w")
    cw = cw_all.transpose(1, 0, 2).reshape(4, W)
    lam = lam_all.transpose(1, 0, 2).reshape(2, W)
    br = br_all.transpose(1, 0, 2).reshape(2, W)
    bi = bi_all.transpose(1, 0, 2).reshape(2, W)

    cc = jnp.concatenate([c_all.reshape(NDEV, D), c_ctx.reshape(1, D), jnp.zeros((NDEV - 1, D), F32)], axis=0)
    b_loc = lax.dynamic_slice(b_ada, (0, me * na), (1, na))
    mod_loc, s_all = _ada_fwd(cc, w_ada[0], b_loc, "ada_fwd")
    (mod_all,) = _all_gather([mod_loc], "gather_mod")
    mod = mod_all.transpose(1, 0, 2).reshape(2 * NDEV, NDEV * na)
    mod_me = lax.dynamic_slice(mod, (me, 0), (1, 3 * D))
    shift, scale, gate = mod_me[:, :D], mod_me[:, D:2 * D], mod_me[:, 2 * D:]
    shift_c, scale_c = mod[NDEV:NDEV + 1, :D], mod[NDEV:NDEV + 1, D:2 * D]

    x2, ctx2, tgt = x[0], ctx[0], loss_target[0]
    gfin = g_final.reshape(1, D)
    h_all = _norm_mod(x2, g_norm, shift, scale, n, lc, None, "norm_lat")
    h_all = _norm_mod(ctx2, g_norm, shift_c, scale_c, n, 0, h_all, "norm_ctx")
    proj_all = _mm_proj(h_all, win_all, "mm_proj")
    u_all = _conv_fwd(proj_all, cw, conv_b, lc, W, "conv_fwd")
    wr_all, wi_all = _send_wait(sent_gw, u_all, False, "wait_gate_w")
    wr = wr_all.transpose(1, 2, 0, 3, 4).reshape(2, heads, hd, hd)
    wi = wi_all.transpose(1, 2, 0, 3, 4).reshape(2, heads, hd, hd)
    a_all, b_all = _gates_fwd(u_all, wr, wi, br, bi, lam, "gates_fwd")
    hs = _scan_fwd(a_all, b_all, lc, "scan_fwd")
    z = _pool_z(proj_all, lc, W, L, W, False, None, "pool_z")
    wpool_all, wout_all = _send_wait(sent_rw, z, False, "wait_rest_w")
    wpool = wpool_all.transpose(1, 0, 2, 3).reshape(G, pd, pd)
    wout = wout_all.reshape(2 * W, D)
    ypre = _mm_group(z, wpool, "fwd", F32, "mm_pool")
    mixed = _mix_fwd(hs, proj_all, ypre, b_pool, pool_scale, lc, "mix_fwd")
    out = _mm_plain(mixed, wout, NN, F32, "mm_out")
    d_out, dxn, loss_p, ggf, dgate = _final(x2, out, tgt, gate, gfin, "final")

    dmixed = _mm_plain(d_out, wout, NT, F32, "mm_dmixed")
    gwout = _mm_plain(mixed, d_out, TN_DIMS, MXU, "mm_gwout")
    ex_o = [gwout.reshape(NDEV, 2 * W // NDEV, D)]
    sent_o = _send_start(ex_o, _place(ex_o, True, "place_gwout"), True, "start_gwout")
    dproj = _dproj_init(n, lc, W, "dproj_init")
    dya, dproj = _mixa_bwd(dmixed, hs, proj_all, dproj, lc, W, "mixa_bwd")
    dypre, dproj, gbp, gps = _mixb_bwd(dmixed, ypre, proj_all, b_pool, pool_scale, dproj, lc, W, "mixb_bwd")
    dz = _mm_group(dypre, wpool, "bwd", F32, "mm_dz")
    gwpool = _mm_group(z, dypre, "wgrad", MXU, "mm_gwpool")
    dproj = _pool_z(dz, lc, W, L, W, True, dproj, "pool_z_bwd")
    da, db = _scan_bwd(a_all, hs, dya, lc, "scan_bwd")
    du, gwr, gwi, gbr, gbi, gcl = _gates_bwd(u_all, da, db, wr, wi, br, bi, lam, "gates_bwd")
    ex_s = [gwpool.reshape(G, NDEV, pd // NDEV, pd).transpose(1, 0, 2, 3),
            gwr.reshape(2, heads, NDEV, hd // NDEV, hd).transpose(2, 0, 1, 3, 4),
            gwi.reshape(2, heads, NDEV, hd // NDEV, hd).transpose(2, 0, 1, 3, 4)]
    sent_s = _send_start(ex_s, _place(ex_s, True, "place_gsmall"), True, "start_gsmall")
    dproj, gcw, gcb = _conv_bwd(du, proj_all, cw, dproj, lc, W, "conv_bwd")
    gwin = _mm_gwin(h_all, dproj, nb, "mm_gwin")
    sent_i = _send_start([gwin], _place([gwin], True, "place_gwin"), True, "start_gwin")
    dh_all = _mm_dh(dproj, win_all, "mm_dh", dep=sent_i[4])
    grad_x, dshift, dscale, ggn = _norm_bwd(x2, dh_all, lc, g_norm, scale, dxn, jnp.zeros((1, D), F32), "norm_bwd_lat")
    _, dshift_c, dscale_c, ggn = _norm_bwd(ctx2, dh_all, 0, g_norm, scale_c, None, ggn, "norm_bwd_ctx")

    dmod_me = jnp.concatenate([dshift, dscale, dgate], axis=1)
    dmod_c = jnp.concatenate([dshift_c, dscale_c, jnp.zeros((1, D), F32)], axis=1)
    smalls = [ggf, ggn, gcw, gcb, gcl, gbr, gbi, gbp, gps, jnp.pad(loss_p, ((0, 0), (0, LANE - 1)))]
    sizes = [s.size for s in smalls]
    small_all, dmod_all, dmodc_all = _all_gather([_pack(smalls), dmod_me, dmod_c], "gather_small")
    ga = lax.dynamic_slice(dmod_all.reshape(NDEV, 3 * D), (0, me * na), (NDEV, na))
    gc = lax.dynamic_slice(dmodc_all.reshape(NDEV, 3 * D), (0, me * na), (NDEV, na))
    g_wada, d_wada, nm_wada, nv_wada, pc = _ada_bwd(s_all, ga, gc, w_ada[0], m_w_ada[0], v_w_ada[0], "ada_bwd")
    (pc_all,) = _all_gather([pc[0:1]], "gather_cctx")
    tot, gb_ada = _small_sum(
        jnp.concatenate([small_all.reshape(NDEV, -1), pc_all.reshape(NDEV, D)], axis=1),
        dmod_all.reshape(NDEV, 3 * D), dmodc_all.reshape(NDEV, 3 * D), "small_sum")
    offs = [0]
    for s in sizes + [D]:
        offs.append(offs[-1] + s)
    t_ggf, t_ggn, t_gcw, t_gcb, t_gcl, t_gbr, t_gbi, t_gbp, t_gps, t_loss, t_pc = [
        tot[:, offs[i]:offs[i + 1]] for i in range(len(offs) - 1)]

    def shard(t, rows):
        return lax.dynamic_slice(t.reshape(rows, W), (0, me * ws), (rows, ws))

    def big(wv, parts, mv, vv, name):
        shp = wv.shape
        C = shp[-1]
        outs = _adamw_parts(wv.reshape(-1, C), parts.reshape(NDEV, -1, C), mv.reshape(-1, C), vv.reshape(-1, C), name)
        return [o.reshape(shp) for o in outs]

    (recv_o,) = _send_wait(sent_o, tot, True, "wait_gwout")
    recv_p, recv_r, recv_i = _send_wait(sent_s, tot, True, "wait_gsmall")
    r_wout = big(w_out, recv_o, m_w_out, v_w_out, "adamw_w_out")
    r_wpool = big(w_pool, recv_p, m_w_pool, v_w_pool, "adamw_w_pool")
    r_wr = big(w_rgate, recv_r, m_w_rgate, v_w_rgate, "adamw_w_rgate")
    r_wi = big(w_igate, recv_i, m_w_igate, v_w_igate, "adamw_w_igate")
    r_wada = [o.reshape(w_ada.shape) for o in (g_wada, d_wada, nm_wada, nv_wada)]

    names = ["c_ctx", "b_ada", "g_norm", "conv_w", "conv_b", "lru_lambda", "b_rgate", "b_igate", "b_pool",
             "pool_scale", "g_final"]
    sw = [c_ctx, b_ada, g_norm, conv_w, conv_b, lru_lambda, b_rgate, b_igate, b_pool, pool_scale, g_final]
    sm = [m_c_ctx, m_b_ada, m_g_norm, m_conv_w, m_conv_b, m_lru_lambda, m_b_rgate, m_b_igate, m_b_pool,
          m_pool_scale, m_g_final]
    sv = [v_c_ctx, v_b_ada, v_g_norm, v_conv_w, v_conv_b, v_lru_lambda, v_b_rgate, v_b_igate, v_b_pool,
          v_pool_scale, v_g_final]
    sg = [t_pc, gb_ada, t_ggn, shard(t_gcw, 4), t_gcb, shard(t_gcl, 2), shard(t_gbr, 2), shard(t_gbi, 2), t_gbp,
          t_gps, t_ggf]
    poffs = [0]
    for wv in sw:
        poffs.append(poffs[-1] + wv.size)
    lam_range = (poffs[5], poffs[6])
    cctx_range = (poffs[0], poffs[1])
    small_out = _adamw_small(_pack(sg), _pack(sw), _pack(sm), _pack(sv), lam_range, cctx_range, "adamw_small")
    (recv_w,) = _send_wait(sent_i, small_out[0], True, "wait_gwin")
    r_win = big(w_in, recv_w, m_w_in, v_w_in, "adamw_w_in")
    r_small = {}
    for i, nm in enumerate(names):
        r_small[nm] = [o[:, poffs[i]:poffs[i + 1]].reshape(sw[i].shape) for o in small_out]

    res = dict(r_small)
    res.update(w_ada=r_wada, w_in=r_win, w_rgate=r_wr, w_igate=r_wi, w_pool=r_wpool, w_out=r_wout)
    order = ["c_ctx", "w_ada", "b_ada", "g_norm", "w_in", "conv_w", "conv_b", "lru_lambda", "w_rgate", "b_rgate",
             "w_igate", "b_igate", "w_pool", "b_pool", "pool_scale", "w_out", "g_final"]
    loss = t_loss[0, 0]
    outs = [loss, grad_x.reshape(x.shape)]
    for q in range(4):
        outs += [res[nm][q] for nm in order]
    return tuple(outs)
```

```python
import functools

import jax
import jax.numpy as jnp
from jax import lax
from jax.experimental import pallas as pl
from jax.experimental.pallas import tpu as pltpu

NDEV = 8
GRID_W = 64
POOL_WINDOWS = (2, 4, 8, 16)
LRU_C = 8.0
EPS = 1e-6
ADAM_LR = 0.001
ADAM_B1 = 0.9
ADAM_B2 = 0.999
ADAM_EPS = 1e-08
ADAM_WD = 0.01
ADAM_STEP = 10

F32 = jnp.float32
MXU = jnp.bfloat16

VMEM_BYTES = 64 * 1024 * 1024
VMEM_SLACK = 8 * 1024 * 1024
SUB = 8
SUB16 = 16
LANE = 128

TM = 1152
TN = 1024
TK = 2048
TL = 256
TL_FINAL = 128
CB_SEQ = 256
CB_SCAN = 128
CB_MIX = 512
TR_CONV = 576
PLACE_CHUNKS = 8

MESH_ID = pl.DeviceIdType.MESH


def _tile(n, pref, align):
    if n <= pref:
        return n
    for t in range(pref - pref % align, 0, -align):
        if n % t == 0:
            return t
    return n


def _nbytes(shape, dtype):
    n = 1
    for s in shape:
        if s is not None:
            n *= s
    return n * jnp.dtype(dtype).itemsize


def _params(blocks, scratch=(), dims=None):
    need = 2 * sum(_nbytes(s, d) for s, d in blocks) + sum(_nbytes(s, d) for s, d in scratch) + VMEM_SLACK
    kw = dict(vmem_limit_bytes=int(min(max(need, 2 * VMEM_SLACK), VMEM_BYTES - VMEM_SLACK // 2)))
    if dims is not None:
        kw["dimension_semantics"] = dims
    return pltpu.CompilerParams(**kw)


def _sds(shape, dtype):
    return jax.ShapeDtypeStruct(tuple(shape), dtype)


ANY = pl.BlockSpec(memory_space=pl.ANY)


def _ids():
    return lax.axis_index("x"), lax.axis_index("y"), lax.axis_index("c")


def _sigmoid(v):
    return jax.nn.sigmoid(v)


def _neg_expm1(v):
    series = -v * (1.0 + v * (0.5 + v * (1.0 / 6.0 + v * (1.0 / 24.0))))
    return jnp.where(v > -1e-2, series, 1.0 - jnp.exp(v))


def _softplus(v):
    return jnp.maximum(v, 0.0) + jnp.log1p(jnp.exp(-jnp.abs(v)))


def _all_gather(xs, name):
    n = len(xs)

    def body(*refs):
        x_refs, o_refs = refs[:n], refs[n:2 * n]
        send_sems, recv_sems, local_sems = refs[2 * n:]
        x, y, c = _ids()
        me, sibling = (x, y, c), (x, y, 1 - c)
        chips = [(1 - x, y), (x, 1 - y), (1 - x, 1 - y)]

        def slot(a, p):
            return o_refs[a].at[4 * p[0] + 2 * p[1] + p[2]]

        def copy(a, k, block, to, src=None):
            return pltpu.make_async_remote_copy(
                src_ref=slot(a, block) if src is None else src, dst_ref=slot(a, block),
                send_sem=send_sems.at[7 * a + k], recv_sem=recv_sems.at[7 * a + k],
                device_id=to, device_id_type=MESH_ID)

        mine, first, passed = [], [], []
        for a in range(n):
            m = pltpu.make_async_copy(x_refs[a], slot(a, me), local_sems.at[a])
            m.start()
            mine.append(m)
            f = [copy(a, 0, me, sibling, src=x_refs[a])]
            f += [copy(a, 1 + j, me, (*chip, c), src=x_refs[a]) for j, chip in enumerate(chips)]
            for cp in f:
                cp.start()
            first += f
        for a in range(n):
            for j, chip in enumerate(chips):
                copy(a, 1 + j, (*chip, c), me).wait_recv()
                p = copy(a, 4 + j, (*chip, c), sibling)
                p.start()
                passed.append(p)
        for a in range(n):
            copy(a, 0, sibling, me).wait_recv()
            for j, chip in enumerate(chips):
                copy(a, 4 + j, (*chip, 1 - c), me).wait_recv()
        for cp in first + passed:
            cp.wait_send()
        for m in mine:
            m.wait()

    return pl.pallas_call(
        body, name=name,
        out_shape=[_sds((NDEV,) + v.shape, v.dtype) for v in xs],
        in_specs=[ANY] * n, out_specs=[ANY] * n,
        scratch_shapes=[pltpu.SemaphoreType.DMA((7 * n,)), pltpu.SemaphoreType.DMA((7 * n,)),
                        pltpu.SemaphoreType.DMA((n,))],
    )(*xs)


HBM = pl.BlockSpec(memory_space=pltpu.HBM)
SEM = pl.BlockSpec(memory_space=pltpu.SEMAPHORE)
EFFECT = pltpu.SideEffectType.DATAFLOW_SIDE_EFFECTING


def _peers():
    x, y, c = _ids()
    out = []
    for k in range(1, NDEV):
        px = 1 - x if k & 4 else x
        py = 1 - y if k & 2 else y
        pc = 1 - c if k & 1 else c
        out.append(((px, py, pc), 4 * px + 2 * py + pc))
    return out, 4 * x + 2 * y + c


def _place(srcs, from_slot, name):
    n = len(srcs)
    blks = [v.shape[1:] if from_slot else v.shape for v in srcs]
    nch = [PLACE_CHUNKS if b[0] % PLACE_CHUNKS == 0 else 1 for b in blks]

    def body(*refs):
        s_refs, l_refs, sems = refs[:n], refs[n:2 * n], refs[2 * n]
        x, y, c = _ids()
        me = 4 * x + 2 * y + c
        cps = []
        for a in range(n):
            src = s_refs[a].at[me] if from_slot else s_refs[a]
            dst = l_refs[a].at[me]
            step = blks[a][0] // nch[a]
            for q in range(nch[a]):
                rows = pl.ds(q * step, step)
                cps.append(pltpu.make_async_copy(src.at[rows], dst.at[rows], sems.at[len(cps)]))
        for cp in cps:
            cp.start()
        for cp in cps:
            cp.wait()

    return pl.pallas_call(
        body, name=name, out_shape=[_sds((NDEV,) + b, v.dtype) for b, v in zip(blks, srcs)],
        in_specs=[ANY] * n, out_specs=[ANY] * n, scratch_shapes=[pltpu.SemaphoreType.DMA((sum(nch),))],
    )(*srcs)


def _send_copies(s_refs, l_refs, ssem, rsem, from_slot, receiving):
    peers, me = _peers()
    out = []
    for a in range(len(s_refs)):
        for k, (dev, idx) in enumerate(peers):
            out.append(pltpu.make_async_remote_copy(
                src_ref=s_refs[a].at[idx] if from_slot else s_refs[a],
                dst_ref=l_refs[a].at[idx if receiving else me],
                send_sem=ssem.at[7 * a + k], recv_sem=rsem.at[7 * a + k], device_id=dev, device_id_type=MESH_ID))
    return out


def _send_start(srcs, lands, from_slot, name):
    n = len(srcs)

    def body(*refs):
        s_refs, l_refs = refs[:n], refs[n:2 * n]
        ssem, rsem = refs[2 * n], refs[2 * n + 1]
        token = refs[-1]
        for send in _send_copies(s_refs, l_refs, ssem, rsem, from_slot, False):
            send.start()
        token[...] = jnp.zeros_like(token)

    bufs = list(srcs) + list(lands)
    outs = pl.pallas_call(
        body, name=name,
        out_shape=[pltpu.SemaphoreType.DMA((7 * n,)), pltpu.SemaphoreType.DMA((7 * n,))]
        + [pltpu.HBM(v.shape, v.dtype) for v in bufs] + [_sds((SUB, LANE), F32)],
        in_specs=[HBM] * (2 * n), out_specs=[SEM, SEM] + [HBM] * (2 * n) + [pl.BlockSpec(memory_space=pltpu.VMEM)],
        input_output_aliases={i: 2 + i for i in range(2 * n)},
        compiler_params=pltpu.CompilerParams(has_side_effects=EFFECT),
    )(*[pltpu.with_memory_space_constraint(v, pltpu.HBM) for v in bufs])
    return outs[0], outs[1], list(outs[2:2 + n]), list(outs[2 + n:2 + 2 * n]), outs[-1]


def _send_wait(started, after, from_slot, name):
    ssem, rsem, srcs, lands, _ = started
    n = len(srcs)

    def body(*refs):
        s_refs, l_refs = refs[:n], refs[n:2 * n]
        ssem_ref, rsem_ref = refs[2 * n], refs[2 * n + 1]
        for recv in _send_copies(s_refs, l_refs, ssem_ref, rsem_ref, from_slot, True):
            recv.wait_send()
            recv.wait_recv()

    bufs = list(srcs) + list(lands)
    outs = pl.pallas_call(
        body, name=name, out_shape=[pltpu.HBM(v.shape, v.dtype) for v in bufs],
        in_specs=[HBM] * (2 * n) + [SEM, SEM, ANY], out_specs=[HBM] * (2 * n),
        input_output_aliases={i: i for i in range(2 * n)},
        compiler_params=pltpu.CompilerParams(has_side_effects=EFFECT),
    )(*bufs, ssem, rsem, after)
    return list(outs[n:])


NN = (((1,), (0,)), ((), ()))
NT = (((1,), (1,)), ((), ()))
TN_DIMS = (((0,), (0,)), ((), ()))


def _mm(a, b, *, grid, a_spec, b_spec, o_spec, out_shape, acc_shape, dims, name, dep=None):
    k_axis = len(grid) - 1
    nk = grid[k_axis]
    extra = [] if dep is None else [dep]

    def body(a_ref, b_ref, *rest):
        o_ref, acc_ref = rest[-2], rest[-1]
        k = pl.program_id(k_axis)

        def prod():
            return lax.dot_general(a_ref[...], b_ref[...], dims, preferred_element_type=F32)

        if nk == 1:
            o_ref[...] = prod().astype(o_ref.dtype)
            return

        @pl.when(k == 0)
        def _():
            acc_ref[...] = prod()

        if nk > 2:
            @pl.when((k > 0) & (k < nk - 1))
            def _():
                acc_ref[...] += prod()

        @pl.when(k == nk - 1)
        def _():
            o_ref[...] = (acc_ref[...] + prod()).astype(o_ref.dtype)

    blocks = [(a_spec.block_shape, a.dtype), (b_spec.block_shape, b.dtype), (o_spec.block_shape, out_shape.dtype)]
    return pl.pallas_call(
        body, name=name, grid=grid, in_specs=[a_spec, b_spec] + [ANY] * len(extra), out_specs=o_spec,
        out_shape=out_shape, scratch_shapes=[pltpu.VMEM(acc_shape, F32)],
        compiler_params=_params(blocks, [(acc_shape, F32)], ("parallel",) * k_axis + ("arbitrary",)),
    )(a, b, *extra)


def _mm_plain(a, b, dims, out_dtype, name):
    if dims == TN_DIMS:
        (K, M), N = a.shape, b.shape[1]
    elif dims == NT:
        (M, K), N = a.shape, b.shape[0]
    else:
        (M, K), N = a.shape, b.shape[1]
    tm, tn = _tile(M, TM, LANE), _tile(N, TN, LANE)
    tk = _tile(K, TK, LANE if dims != TN_DIMS else SUB16)
    if dims == TN_DIMS:
        a_spec = pl.BlockSpec((tk, tm), lambda i, j, k: (k, i))
    else:
        a_spec = pl.BlockSpec((tm, tk), lambda i, j, k: (i, k))
    if dims == NT:
        b_spec = pl.BlockSpec((tn, tk), lambda i, j, k: (j, k))
    else:
        b_spec = pl.BlockSpec((tk, tn), lambda i, j, k: (k, j))
    return _mm(a, b, grid=(M // tm, N // tn, K // tk), a_spec=a_spec, b_spec=b_spec,
               o_spec=pl.BlockSpec((tm, tn), lambda i, j, k: (i, j)),
               out_shape=_sds((M, N), out_dtype), acc_shape=(tm, tn), dims=dims, name=name)


def _mm_proj(h_all, win_all, name):
    n, D = h_all.shape
    nb = win_all.shape[2]
    tm, tn, tk = _tile(n, TM, SUB16), _tile(nb, TN, LANE), _tile(D, TK, LANE)
    nbn = nb // tn
    return _mm(h_all, win_all, grid=(n // tm, NDEV * nbn, D // tk),
               a_spec=pl.BlockSpec((tm, tk), lambda i, j, k: (i, k)),
               b_spec=pl.BlockSpec((None, tk, tn), lambda i, j, k: (j // nbn, k, j % nbn)),
               o_spec=pl.BlockSpec((tm, tn), lambda i, j, k: (i, j)),
               out_shape=_sds((n, NDEV * nb), F32), acc_shape=(tm, tn), dims=NN, name=name)


def _mm_dh(dproj, win_all, name, dep=None):
    n = dproj.shape[0]
    _, D, nb = win_all.shape
    tm, tn, tk = _tile(n, TM, SUB16), _tile(D, TN, LANE), _tile(nb, TK, LANE)
    nbk = nb // tk
    return _mm(dproj, win_all, grid=(n // tm, D // tn, NDEV * nbk),
               a_spec=pl.BlockSpec((tm, tk), lambda i, j, k: (i, k)),
               b_spec=pl.BlockSpec((None, tn, tk), lambda i, j, k: (k // nbk, j, k % nbk)),
               o_spec=pl.BlockSpec((tm, tn), lambda i, j, k: (i, j)),
               out_shape=_sds((n, D), F32), acc_shape=(tm, tn), dims=NT, name=name, dep=dep)


def _mm_gwin(h_all, dproj, nb, name):
    n, D = h_all.shape
    tm, tn, tk = _tile(D, TM, LANE), _tile(nb, TN, LANE), _tile(n, TK, SUB16)
    nbn = nb // tn
    return _mm(h_all, dproj, grid=(D // tm, NDEV * nbn, n // tk),
               a_spec=pl.BlockSpec((tk, tm), lambda i, j, k: (k, i)),
               b_spec=pl.BlockSpec((tk, tn), lambda i, j, k: (k, j)),
               o_spec=pl.BlockSpec((None, tm, tn), lambda i, j, k: (j // nbn, i, j % nbn)),
               out_shape=_sds((NDEV, D, nb), MXU), acc_shape=(tm, tn), dims=TN_DIMS, name=name)


def _mm_group(a, b, mode, out_dtype, name):
    if mode == "wgrad":
        L, W = a.shape
        G = len(POOL_WINDOWS)
        pd = W // G
        tm, tn, tk = _tile(pd, TM, LANE), _tile(pd, TN, LANE), _tile(L, TK, SUB16)
        nm, nn = pd // tm, pd // tn
        return _mm(a, b, grid=(G, nm, nn, L // tk),
                   a_spec=pl.BlockSpec((tk, tm), lambda g, i, j, k: (k, g * nm + i)),
                   b_spec=pl.BlockSpec((tk, tn), lambda g, i, j, k: (k, g * nn + j)),
                   o_spec=pl.BlockSpec((None, tm, tn), lambda g, i, j, k: (g, i, j)),
                   out_shape=_sds((G, pd, pd), out_dtype), acc_shape=(tm, tn), dims=TN_DIMS, name=name)
    L, W = a.shape
    G, pd, _ = b.shape
    tm, tn, tk = _tile(L, TM, SUB16), _tile(pd, TN, LANE), _tile(pd, TK, LANE)
    nn, nk = pd // tn, pd // tk
    if mode == "fwd":
        b_spec = pl.BlockSpec((None, tk, tn), lambda g, i, j, k: (g, k, j))
        dims = NN
    else:
        b_spec = pl.BlockSpec((None, tn, tk), lambda g, i, j, k: (g, j, k))
        dims = NT
    return _mm(a, b, grid=(G, L // tm, nn, nk),
               a_spec=pl.BlockSpec((tm, tk), lambda g, i, j, k: (i, g * nk + k)),
               b_spec=b_spec,
               o_spec=pl.BlockSpec((tm, tn), lambda g, i, j, k: (i, g * nn + j)),
               out_shape=_sds((L, W), out_dtype), acc_shape=(tm, tn), dims=dims, name=name)


def _ada_fwd(cc, w_loc, b_loc, name):
    R, D = cc.shape
    na = w_loc.shape[1]
    tk = _tile(D, 512, LANE)

    def body(c_ref, w_ref, b_ref, mod_ref, s_ref):
        k = pl.program_id(0)
        cv = c_ref[...]
        s = cv * _sigmoid(cv)
        s_ref[...] = s

        @pl.when(k == 0)
        def _():
            mod_ref[...] = jnp.broadcast_to(b_ref[...], mod_ref.shape)

        mod_ref[...] += lax.dot_general(s.astype(MXU), w_ref[...].astype(MXU), NN, preferred_element_type=F32)

    blocks = [((R, tk), F32), ((tk, na), F32), ((1, na), F32), ((R, na), F32), ((R, tk), F32)]
    return pl.pallas_call(
        body, name=name, grid=(D // tk,),
        in_specs=[pl.BlockSpec((R, tk), lambda k: (0, k)), pl.BlockSpec((tk, na), lambda k: (k, 0)),
                  pl.BlockSpec((1, na), lambda k: (0, 0))],
        out_specs=[pl.BlockSpec((R, na), lambda k: (0, 0)), pl.BlockSpec((R, tk), lambda k: (0, k))],
        out_shape=[_sds((R, na), F32), _sds((R, D), F32)],
        compiler_params=_params(blocks, dims=("arbitrary",)),
    )(cc, w_loc, b_loc)


def _adam(w, g, m, v):
    m = ADAM_B1 * m + (1.0 - ADAM_B1) * g
    v = ADAM_B2 * v + (1.0 - ADAM_B2) * (g * g)
    m_hat = m / (1.0 - ADAM_B1 ** ADAM_STEP)
    v_hat = v / (1.0 - ADAM_B2 ** ADAM_STEP)
    delta = -ADAM_LR * (m_hat / (jnp.sqrt(v_hat) + ADAM_EPS) + ADAM_WD * w)
    return delta, m, v


def _ada_bwd(s_all, ga, gc, w_loc, m_loc, v_loc, name):
    D, na = w_loc.shape
    tr = _tile(D, 256, LANE)

    def body(s_ref, ga_ref, gc_ref, w_ref, m_ref, v_ref, g_ref, d_ref, nm_ref, nv_ref, pc_ref):
        dmc = gc_ref[0:1, :]
        for p in range(1, NDEV):
            dmc = dmc + gc_ref[p:p + 1, :]
        rows = lax.broadcasted_iota(jnp.int32, (NDEV, na), 0)
        dmc8 = jnp.where(rows == 0, jnp.broadcast_to(dmc, (NDEV, na)), 0.0)
        dm = jnp.concatenate([ga_ref[...], dmc8], axis=0).astype(MXU)
        dmc16 = jnp.concatenate([dmc8, jnp.zeros_like(dmc8)], axis=0).astype(MXU)
        w = w_ref[...]
        g = lax.dot_general(s_ref[...].astype(MXU), dm, TN_DIMS, preferred_element_type=F32)
        pc_ref[...] = lax.dot_general(dmc16, w.astype(MXU), NT, preferred_element_type=F32)
        delta, nm, nv = _adam(w, g, m_ref[...], v_ref[...])
        g_ref[...] = g
        d_ref[...] = delta
        nm_ref[...] = nm
        nv_ref[...] = nv

    big = pl.BlockSpec((tr, na), lambda i: (i, 0))
    full = pl.BlockSpec((NDEV, na), lambda i: (0, 0))
    srow = pl.BlockSpec((2 * NDEV, tr), lambda i: (0, i))
    blocks = [((2 * NDEV, tr), F32)] * 2 + [((NDEV, na), F32)] * 2 + [((tr, na), F32)] * 7
    return pl.pallas_call(
        body, name=name, grid=(D // tr,),
        in_specs=[srow, full, full, big, big, big],
        out_specs=[big, big, big, big, srow],
        out_shape=[_sds((D, na), F32)] * 4 + [_sds((2 * NDEV, D), F32)],
        compiler_params=_params(blocks, dims=("parallel",)),
    )(s_all, ga, gc, w_loc, m_loc, v_loc)


def _norm_mod(x2, g, shift, scale, n, row0, h_prev, name):
    R, D = x2.shape
    tl = _tile(R, TL, SUB16)
    assert row0 % tl == 0
    b0 = row0 // tl

    def body(x_ref, g_ref, sh_ref, sc_ref, *rest):
        o_ref = rest[-1]
        xv = x_ref[...]
        s = lax.rsqrt(jnp.mean(xv * xv, axis=-1, keepdims=True) + EPS)
        nrm = xv * s * g_ref[...]
        o_ref[...] = (nrm * (1.0 + sc_ref[...]) + sh_ref[...]).astype(o_ref.dtype)

    vec = pl.BlockSpec((1, D), lambda i: (0, 0))
    in_specs = [pl.BlockSpec((tl, D), lambda i: (i, 0)), vec, vec, vec]
    args = [x2, g, shift, scale]
    aliases = {}
    if h_prev is not None:
        in_specs.append(ANY)
        args.append(h_prev)
        aliases = {4: 0}
    blocks = [((tl, D), F32), ((tl, D), MXU)] + [((1, D), F32)] * 3
    return pl.pallas_call(
        body, name=name, grid=(R // tl,), in_specs=in_specs,
        out_specs=pl.BlockSpec((tl, D), lambda i: (i + b0, 0)),
        out_shape=_sds((n, D), MXU), input_output_aliases=aliases,
        compiler_params=_params(blocks, dims=("parallel",)),
    )(*args)


def _norm_bwd(x2, dh_all, row0, g, scale, dxn, ggn0, name):
    R, D = x2.shape
    tl = _tile(R, TL_FINAL, SUB)
    assert row0 % tl == 0
    b0 = row0 // tl
    with_x = dxn is not None

    def body(*refs):
        if with_x:
            x_ref, dh_ref, g_ref, sc_ref, gg0_ref, dxn_ref, gx_ref, dsh_ref, dsc_ref, gg_ref = refs
        else:
            x_ref, dh_ref, g_ref, sc_ref, gg0_ref, dsh_ref, dsc_ref, gg_ref = refs
        i = pl.program_id(0)

        @pl.when(i == 0)
        def _():
            dsh_ref[...] = jnp.zeros_like(dsh_ref)
            dsc_ref[...] = jnp.zeros_like(dsc_ref)
            gg_ref[...] = gg0_ref[...]

        xv = x_ref[...]
        dh = dh_ref[...]
        gv = g_ref[...]
        s = lax.rsqrt(jnp.mean(xv * xv, axis=-1, keepdims=True) + EPS)
        xh = xv * s
        dsh_ref[...] += jnp.sum(dh, axis=0, keepdims=True)
        dsc_ref[...] += jnp.sum(dh * (xh * gv), axis=0, keepdims=True)
        dn = dh * (1.0 + sc_ref[...])
        gg_ref[...] += jnp.sum(dn * xh, axis=0, keepdims=True)
        if with_x:
            dxh = dn * gv
            dx = s * (dxh - xh * jnp.mean(dxh * xh, axis=-1, keepdims=True))
            gx_ref[...] = dx + dxn_ref[...]

    vec = pl.BlockSpec((1, D), lambda i: (0, 0))
    row = pl.BlockSpec((tl, D), lambda i: (i, 0))
    in_specs = [row, pl.BlockSpec((tl, D), lambda i: (i + b0, 0)), vec, vec, vec]
    args = [x2, dh_all, g, scale, ggn0]
    out_specs = [vec, vec, vec]
    out_shape = [_sds((1, D), F32)] * 3
    if with_x:
        in_specs.append(row)
        args.append(dxn)
        out_specs = [row] + out_specs
        out_shape = [_sds((R, D), F32)] + out_shape
    blocks = [((tl, D), F32)] * (4 if with_x else 2) + [((1, D), F32)] * 6
    outs = pl.pallas_call(
        body, name=name, grid=(R // tl,), in_specs=in_specs, out_specs=out_specs, out_shape=out_shape,
        compiler_params=_params(blocks, dims=("arbitrary",)),
    )(*args)
    return tuple(outs) if with_x else (None,) + tuple(outs)


def _tap_valid(t, o, lc, n):
    tt = t + o
    in_ctx = t < lc
    return (tt >= jnp.where(in_ctx, 0, lc)) & (tt < jnp.where(in_ctx, lc, n))


def _conv_fwd(proj_all, cw, cb, lc, W, name):
    n = proj_all.shape[0]
    cbk = _tile(W, CB_SEQ, LANE)
    tr = _tile(n, TR_CONV, SUB16)
    ext = tr + 2 * SUB

    def body(x_ref, w_ref, b_ref, u_ref, xp_ref):
        xp_ref[0:SUB, :] = jnp.zeros((SUB, cbk), F32)
        xp_ref[n + SUB:n + 2 * SUB, :] = jnp.zeros((SUB, cbk), F32)
        xp_ref[SUB:n + SUB, :] = x_ref[...]
        w = w_ref[...]
        bias = b_ref[...]

        def chunk(ci, carry):
            r0 = pl.multiple_of(ci * tr, SUB16)
            xe = xp_ref[pl.ds(r0, ext), :]
            t = r0 + lax.broadcasted_iota(jnp.int32, (tr, cbk), 0)
            acc = jnp.broadcast_to(bias, (tr, cbk))
            for k in range(4):
                o = k - 1
                sh = xe if o == 0 else pltpu.roll(xe, (-o) % ext, 0)
                acc = acc + jnp.where(_tap_valid(t, o, lc, n), sh[SUB:tr + SUB], 0.0) * w[k:k + 1]
            u_ref[pl.ds(r0, tr), :] = acc
            return carry

        lax.fori_loop(0, n // tr, chunk, 0)

    blocks = [((n, cbk), F32)] * 2 + [((4, cbk), F32), ((1, cbk), F32)]
    scratch = [((n + 2 * SUB, cbk), F32)]
    return pl.pallas_call(
        body, name=name, grid=(W // cbk,),
        in_specs=[pl.BlockSpec((n, cbk), lambda j: (0, j)), pl.BlockSpec((4, cbk), lambda j: (0, j)),
                  pl.BlockSpec((1, cbk), lambda j: (0, j))],
        out_specs=pl.BlockSpec((n, cbk), lambda j: (0, j)),
        out_shape=_sds((n, W), F32),
        scratch_shapes=[pltpu.VMEM(s, d) for s, d in scratch],
        compiler_params=_params(blocks, scratch, ("parallel",)),
    )(proj_all, cw, cb)


def _conv_bwd(du_all, proj_all, cw, dproj, lc, W, name):
    n = du_all.shape[0]
    cbk = _tile(W, CB_SEQ, LANE)
    tr = _tile(n, TR_CONV, SUB16)
    ext = tr + 2 * SUB

    def body(du_ref, x_ref, w_ref, dp_in, dx_ref, gw_ref, gb_ref, dp_ref, xp_ref):
        del dp_in
        for ref, src in ((dp_ref, du_ref), (xp_ref, x_ref)):
            ref[0:SUB, :] = jnp.zeros((SUB, cbk), F32)
            ref[n + SUB:n + 2 * SUB, :] = jnp.zeros((SUB, cbk), F32)
            ref[SUB:n + SUB, :] = src[...]
        w = w_ref[...]

        def fold(v):
            return jnp.sum(v.reshape(tr // SUB, SUB, cbk), axis=0)

        def chunk(ci, carry):
            r0 = pl.multiple_of(ci * tr, SUB16)
            de = dp_ref[pl.ds(r0, ext), :]
            xe = xp_ref[pl.ds(r0, ext), :]
            t = r0 + lax.broadcasted_iota(jnp.int32, (tr, cbk), 0)
            d0 = de[SUB:tr + SUB]
            dx = jnp.zeros((tr, cbk), F32)
            new = []
            for k in range(4):
                o = k - 1
                dsh = de if o == 0 else pltpu.roll(de, o % ext, 0)
                dx = dx + jnp.where(_tap_valid(t, -o, lc, n), dsh[SUB:tr + SUB], 0.0) * w[k:k + 1]
                xsh = xe if o == 0 else pltpu.roll(xe, (-o) % ext, 0)
                new.append(carry[k] + fold(d0 * jnp.where(_tap_valid(t, o, lc, n), xsh[SUB:tr + SUB], 0.0)))
            new.append(carry[4] + fold(d0))
            dx_ref[pl.ds(r0, tr), :] = dx.astype(dx_ref.dtype)
            return tuple(new)

        zero = jnp.zeros((SUB, cbk), F32)
        acc = lax.fori_loop(0, n // tr, chunk, (zero,) * 5)
        for k in range(4):
            gw_ref[k:k + 1, :] = jnp.sum(acc[k], axis=0, keepdims=True)
        gb_ref[...] = jnp.sum(acc[4], axis=0, keepdims=True)

    col = pl.BlockSpec((n, cbk), lambda j: (0, j))
    blocks = [((n, cbk), F32)] * 2 + [((n, cbk), MXU), ((4, cbk), F32), ((4, cbk), F32), ((1, cbk), F32)]
    scratch = [((n + 2 * SUB, cbk), F32)] * 2
    return pl.pallas_call(
        body, name=name, grid=(W // cbk,),
        in_specs=[col, col, pl.BlockSpec((4, cbk), lambda j: (0, j)), ANY],
        out_specs=[col, pl.BlockSpec((4, cbk), lambda j: (0, j)), pl.BlockSpec((1, cbk), lambda j: (0, j))],
        out_shape=[_sds(dproj.shape, dproj.dtype), _sds((4, W), F32), _sds((1, W), F32)],
        input_output_aliases={3: 0},
        scratch_shapes=[pltpu.VMEM(s, d) for s, d in scratch],
        compiler_params=_params(blocks, scratch, ("parallel",)),
    )(du_all, proj_all, cw, dproj)


def _gate_coeffs(ub, u, d, wr_ref, wi_ref, br_ref, bi_ref, lam_ref):
    c = -LRU_C * _softplus(-lam_ref[d:d + 1, :])
    r = _sigmoid(lax.dot_general(ub, wr_ref[d], NN, preferred_element_type=F32) + br_ref[d:d + 1, :])
    ig = _sigmoid(lax.dot_general(ub, wi_ref[d], NN, preferred_element_type=F32) + bi_ref[d:d + 1, :])
    la = c * r
    a = jnp.exp(la)
    sq = jnp.sqrt(_neg_expm1(2.0 * la))
    return c, r, ig, a, sq


def _gate_specs(tl, hd):
    w_spec = pl.BlockSpec((2, None, hd, hd), lambda h, i: (0, h, 0, 0))
    v_spec = pl.BlockSpec((2, hd), lambda h, i: (0, h))
    return w_spec, v_spec


def _gates_fwd(u_all, wr, wi, br, bi, lam, name):
    n, W = u_all.shape
    heads, hd = wr.shape[1], wr.shape[2]
    tl = _tile(n, TL, SUB16)

    def body(u_ref, wr_ref, wi_ref, br_ref, bi_ref, lam_ref, a_ref, b_ref):
        u = u_ref[...]
        ub = u.astype(MXU)
        for d in range(2):
            _, _, ig, a, sq = _gate_coeffs(ub, u, d, wr_ref, wi_ref, br_ref, bi_ref, lam_ref)
            a_ref[d] = a
            b_ref[d] = sq * (ig * u)

    w_spec, v_spec = _gate_specs(tl, hd)
    o_spec = pl.BlockSpec((2, tl, hd), lambda h, i: (0, i, h))
    blocks = [((tl, hd), F32), ((2, hd, hd), MXU), ((2, hd, hd), MXU)] + [((2, hd), F32)] * 3 + [((2, tl, hd), F32)] * 2
    return pl.pallas_call(
        body, name=name, grid=(heads, n // tl),
        in_specs=[pl.BlockSpec((tl, hd), lambda h, i: (i, h)), w_spec, w_spec, v_spec, v_spec, v_spec],
        out_specs=[o_spec, o_spec], out_shape=[_sds((2, n, W), F32)] * 2,
        compiler_params=_params(blocks, dims=("parallel", "parallel")),
    )(u_all, wr, wi, br, bi, lam)


def _gates_bwd(u_all, da, db, wr, wi, br, bi, lam, name):
    n, W = u_all.shape
    heads, hd = wr.shape[1], wr.shape[2]
    tl = _tile(n, TL, SUB16)
    ni = n // tl

    def body(u_ref, da_ref, db_ref, wr_ref, wi_ref, br_ref, bi_ref, lam_ref,
             du_ref, gwr_ref, gwi_ref, gbr_ref, gbi_ref, gc_ref, accr_ref, acci_ref):
        i = pl.program_id(1)

        @pl.when(i == 0)
        def _():
            accr_ref[...] = jnp.zeros_like(accr_ref)
            acci_ref[...] = jnp.zeros_like(acci_ref)
            gbr_ref[...] = jnp.zeros_like(gbr_ref)
            gbi_ref[...] = jnp.zeros_like(gbi_ref)
            gc_ref[...] = jnp.zeros_like(gc_ref)

        u = u_ref[...]
        ub = u.astype(MXU)
        du = jnp.zeros_like(u)
        for d in range(2):
            c, r, ig, a, sq = _gate_coeffs(ub, u, d, wr_ref, wi_ref, br_ref, bi_ref, lam_ref)
            dbv = db_ref[d]
            t = dbv * sq
            du = du + t * ig
            d_la = da_ref[d] * a - (dbv * ig * u) * (a * a) / sq
            gc_ref[d:d + 1, :] += jnp.sum(d_la * r, axis=0, keepdims=True)
            d_pr = (d_la * c) * (r * (1.0 - r))
            d_pi = (t * u) * (ig * (1.0 - ig))
            gbr_ref[d:d + 1, :] += jnp.sum(d_pr, axis=0, keepdims=True)
            gbi_ref[d:d + 1, :] += jnp.sum(d_pi, axis=0, keepdims=True)
            pb = d_pr.astype(MXU)
            qb = d_pi.astype(MXU)
            du = du + lax.dot_general(pb, wr_ref[d], NT, preferred_element_type=F32)
            du = du + lax.dot_general(qb, wi_ref[d], NT, preferred_element_type=F32)
            accr_ref[d] += lax.dot_general(ub, pb, TN_DIMS, preferred_element_type=F32)
            acci_ref[d] += lax.dot_general(ub, qb, TN_DIMS, preferred_element_type=F32)
        du_ref[...] = du

        @pl.when(i == ni - 1)
        def _():
            gwr_ref[...] = accr_ref[...].astype(gwr_ref.dtype)
            gwi_ref[...] = acci_ref[...].astype(gwi_ref.dtype)

    w_spec, v_spec = _gate_specs(tl, hd)
    u_spec = pl.BlockSpec((tl, hd), lambda h, i: (i, h))
    ab_spec = pl.BlockSpec((2, tl, hd), lambda h, i: (0, i, h))
    blocks = ([((tl, hd), F32)] * 2 + [((2, tl, hd), F32)] * 2 + [((2, hd, hd), MXU)] * 4 + [((2, hd), F32)] * 6)
    scratch = [((2, hd, hd), F32)] * 2
    return pl.pallas_call(
        body, name=name, grid=(heads, ni),
        in_specs=[u_spec, ab_spec, ab_spec, w_spec, w_spec, v_spec, v_spec, v_spec],
        out_specs=[u_spec, w_spec, w_spec, v_spec, v_spec, v_spec],
        out_shape=[_sds((n, W), F32), _sds(wr.shape, MXU), _sds(wi.shape, MXU)] + [_sds((2, W), F32)] * 3,
        scratch_shapes=[pltpu.VMEM(s, d) for s, d in scratch],
        compiler_params=_params(blocks, scratch, ("parallel", "arbitrary")),
    )(u_all, da, db, wr, wi, br, bi, lam)


def _tile_scan(A, B, rows, reverse):
    for s in (1, 2, 4):
        if reverse:
            As, Bs, m = pltpu.roll(A, SUB - s, 0), pltpu.roll(B, SUB - s, 0), rows < SUB - s
        else:
            As, Bs, m = pltpu.roll(A, s, 0), pltpu.roll(B, s, 0), rows >= s
        B = jnp.where(m, A * Bs + B, B)
        A = jnp.where(m, A * As, A)
    return A, B


def _scan_fwd(a_all, b_all, lc, name):
    _, n, W = a_all.shape
    cb = _tile(W, CB_SCAN, LANE)
    nt, ntc = n // SUB, lc // SUB

    def body(a_ref, b_ref, h_ref):
        rows = lax.broadcasted_iota(jnp.int32, (SUB, cb), 0)

        def step(s, carry):
            h0, h1 = carry
            r0 = pl.multiple_of(s * SUB, SUB)
            A, B = _tile_scan(a_ref[0, pl.ds(r0, SUB), :], b_ref[0, pl.ds(r0, SUB), :], rows, False)
            H0 = A * h0 + B
            h_ref[0, pl.ds(r0, SUB), :] = H0
            j1 = jnp.where(s < ntc, ntc - 1 - s, nt - 1 - (s - ntc))
            r1 = pl.multiple_of(j1 * SUB, SUB)
            A, B = _tile_scan(a_ref[1, pl.ds(r1, SUB), :], b_ref[1, pl.ds(r1, SUB), :], rows, True)
            H1 = A * h1 + B
            h_ref[1, pl.ds(r1, SUB), :] = H1
            return H0[SUB - 1:SUB, :], H1[0:1, :]

        zero = jnp.zeros((1, cb), F32)
        lax.fori_loop(0, nt, step, (zero, zero))

    spec = pl.BlockSpec((2, n, cb), lambda j: (0, 0, j))
    return pl.pallas_call(
        body, name=name, grid=(W // cb,), in_specs=[spec, spec], out_specs=spec,
        out_shape=_sds((2, n, W), F32),
        compiler_params=_params([((2, n, cb), F32)] * 3, dims=("parallel",)),
    )(a_all, b_all)


def _scan_bwd(a_all, h_all, dya, lc, name):
    _, n, W = a_all.shape
    cb = _tile(W, CB_SCAN, LANE)
    nt, ntc = n // SUB, lc // SUB
    nl = nt - ntc

    def body(a_ref, h_ref, g_ref, da_ref, db_ref):
        rows = lax.broadcasted_iota(jnp.int32, (SUB, cb), 0)

        def tile(ref, d, j):
            return ref[d, pl.ds(pl.multiple_of(j * SUB, SUB), SUB), :]

        def grad_tile(j):
            jl = jnp.maximum(j - ntc, 0)
            g = g_ref[pl.ds(pl.multiple_of(jl * SUB, SUB), SUB), :]
            return jnp.where(j >= ntc, g, 0.0)

        def put(ref, d, j, v):
            ref[d, pl.ds(pl.multiple_of(j * SUB, SUB), SUB), :] = v

        def step(s, carry):
            mu0, mu1 = carry
            j = nt - 1 - s
            a_t = tile(a_ref, 0, j)
            ap = jnp.where(rows < SUB - 1, pltpu.roll(a_t, SUB - 1, 0), 1.0)
            A, B = _tile_scan(ap, grad_tile(j), rows, True)
            lam = A * mu0 + B
            below = jnp.where(j > 0, tile(h_ref, 0, jnp.maximum(j - 1, 0))[SUB - 1:SUB, :], 0.0)
            hprev = jnp.where(rows >= 1, pltpu.roll(tile(h_ref, 0, j), 1, 0), below)
            put(da_ref, 0, j, lam * hprev)
            put(db_ref, 0, j, lam)
            mu0 = a_t[0:1, :] * lam[0:1, :]
            j = jnp.where(s < nl, ntc + s, s - nl)
            a_t = tile(a_ref, 1, j)
            ap = jnp.where(rows >= 1, pltpu.roll(a_t, 1, 0), 1.0)
            A, B = _tile_scan(ap, grad_tile(j), rows, False)
            lam = A * mu1 + B
            jn = jnp.where(j == nt - 1, 0, jnp.minimum(j + 1, nt - 1))
            above = jnp.where(j == ntc - 1, 0.0, tile(h_ref, 1, jn)[0:1, :])
            hprev = jnp.where(rows < SUB - 1, pltpu.roll(tile(h_ref, 1, j), SUB - 1, 0), above)
            put(da_ref, 1, j, lam * hprev)
            put(db_ref, 1, j, lam)
            mu1 = a_t[SUB - 1:SUB, :] * lam[SUB - 1:SUB, :]
            return mu0, mu1

        zero = jnp.zeros((1, cb), F32)
        lax.fori_loop(0, nt, step, (zero, zero))

    spec = pl.BlockSpec((2, n, cb), lambda j: (0, 0, j))
    g_spec = pl.BlockSpec((n - lc, cb), lambda j: (0, j))
    return pl.pallas_call(
        body, name=name, grid=(W // cb,), in_specs=[spec, spec, g_spec], out_specs=[spec, spec],
        out_shape=[_sds((2, n, W), F32)] * 2,
        compiler_params=_params([((2, n, cb), F32)] * 4 + [((n - lc, cb), F32)], dims=("parallel",)),
    )(a_all, h_all, dya)


def _pool_window(v, w, tl, cb, transpose):
    left = w // 2
    right = w - 1 - left
    pos = lax.broadcasted_iota(jnp.int32, (tl, cb), 0) % GRID_W
    cnt = (jnp.minimum(pos + right, GRID_W - 1) - jnp.maximum(pos - left, 0) + 1).astype(F32)
    src = v / cnt if transpose else v
    lo, hi = (-right, left) if transpose else (-left, right)
    acc = src
    for o in range(lo, hi + 1):
        if o != 0:
            ok = (pos + o >= 0) & (pos + o < GRID_W)
            acc = acc + jnp.where(ok, pltpu.roll(src, (-o) % tl, 0), 0.0)
    return acc - v if transpose else acc / cnt - v


def _pool_z(src, row0, col0, L, W, transpose, dproj, name):
    G = len(POOL_WINDOWS)
    pd = W // G
    tl = _tile(L, TL, GRID_W)
    cb = _tile(pd, CB_SEQ, LANE)
    assert row0 % tl == 0 and col0 % cb == 0
    rb, cbk = row0 // tl, col0 // cb
    nj = pd // cb

    def body(x_ref, *rest):
        o_ref = rest[-1]
        for gi, w in enumerate(POOL_WINDOWS):
            @pl.when(pl.program_id(0) == gi)
            def _(w=w):
                o_ref[...] = _pool_window(x_ref[...], w, tl, cb, transpose).astype(o_ref.dtype)

    plain = pl.BlockSpec((tl, cb), lambda g, i, j: (i, g * nj + j))
    window = pl.BlockSpec((tl, cb), lambda g, i, j: (i + rb, cbk + g * nj + j))
    blocks = [((tl, cb), F32), ((tl, cb), MXU)]
    if transpose:
        return pl.pallas_call(
            body, name=name, grid=(G, L // tl, nj), in_specs=[plain, ANY], out_specs=window,
            out_shape=_sds(dproj.shape, dproj.dtype), input_output_aliases={1: 0},
            compiler_params=_params(blocks, dims=("parallel",) * 3),
        )(src, dproj)
    return pl.pallas_call(
        body, name=name, grid=(G, L // tl, nj), in_specs=[window], out_specs=plain,
        out_shape=_sds((L, W), MXU),
        compiler_params=_params(blocks, dims=("parallel",) * 3),
    )(src)


def _mix_fwd(hs, proj_all, ypre, b_pool, pool_scale, lc, name):
    L, W = ypre.shape
    tl = _tile(L, TL, SUB16)
    cb = _tile(W, CB_MIX, LANE)
    nj = W // cb
    assert lc % tl == 0
    rb = lc // tl

    def body(hs_ref, ga_ref, yp_ref, gb_ref, bp_ref, ps_ref, o_ref):
        p = pl.program_id(2)

        @pl.when(p == 0)
        def _():
            g = ga_ref[...]
            o_ref[...] = ((hs_ref[0] + hs_ref[1]) * (g * _sigmoid(g))).astype(o_ref.dtype)

        @pl.when(p == 1)
        def _():
            g = gb_ref[...]
            yb = (yp_ref[...] + bp_ref[...]) * ps_ref[...]
            o_ref[...] = (yb * (g * _sigmoid(g))).astype(o_ref.dtype)

    vec = pl.BlockSpec((1, cb), lambda i, j, p: (0, j))
    blocks = [((2, tl, cb), F32)] + [((tl, cb), F32)] * 3 + [((tl, cb), MXU)]
    return pl.pallas_call(
        body, name=name, grid=(L // tl, nj, 2),
        in_specs=[pl.BlockSpec((2, tl, cb), lambda i, j, p: (0, i + rb, j)),
                  pl.BlockSpec((tl, cb), lambda i, j, p: (i + rb, 2 * nj + j)),
                  pl.BlockSpec((tl, cb), lambda i, j, p: (i, j)),
                  pl.BlockSpec((tl, cb), lambda i, j, p: (i + rb, 3 * nj + j)), vec, vec],
        out_specs=pl.BlockSpec((tl, cb), lambda i, j, p: (i, p * nj + j)),
        out_shape=_sds((L, 2 * W), MXU),
        compiler_params=_params(blocks, dims=("parallel", "parallel", "arbitrary")),
    )(hs, proj_all, ypre, proj_all, b_pool, pool_scale)


def _dsilu(g, sg):
    return sg * (1.0 + g * (1.0 - sg))


def _mixa_bwd(dmixed, hs, proj_all, dproj, lc, W, name):
    L = dmixed.shape[0]
    tl = _tile(L, TL, SUB16)
    cb = _tile(W, CB_MIX, LANE)
    nj = W // cb
    rb = lc // tl

    def body(dm_ref, hs_ref, ga_ref, dp_in, dya_ref, dga_ref):
        del dp_in
        g = ga_ref[...]
        sg = _sigmoid(g)
        dm = dm_ref[...]
        dya_ref[...] = dm * (g * sg)
        dga_ref[...] = (dm * (hs_ref[0] + hs_ref[1]) * _dsilu(g, sg)).astype(dga_ref.dtype)

    blocks = [((tl, cb), F32)] * 3 + [((2, tl, cb), F32), ((tl, cb), MXU)]
    return pl.pallas_call(
        body, name=name, grid=(L // tl, nj),
        in_specs=[pl.BlockSpec((tl, cb), lambda i, j: (i, j)),
                  pl.BlockSpec((2, tl, cb), lambda i, j: (0, i + rb, j)),
                  pl.BlockSpec((tl, cb), lambda i, j: (i + rb, 2 * nj + j)), ANY],
        out_specs=[pl.BlockSpec((tl, cb), lambda i, j: (i, j)),
                   pl.BlockSpec((tl, cb), lambda i, j: (i + rb, 2 * nj + j))],
        out_shape=[_sds((L, W), F32), _sds(dproj.shape, dproj.dtype)],
        input_output_aliases={3: 1},
        compiler_params=_params(blocks, dims=("parallel", "parallel")),
    )(dmixed, hs, proj_all, dproj)


def _mixb_bwd(dmixed, ypre, proj_all, b_pool, pool_scale, dproj, lc, W, name):
    L = dmixed.shape[0]
    tl = _tile(L, TL, SUB16)
    cb = _tile(W, CB_MIX, LANE)
    nj = W // cb
    rb = lc // tl

    def body(dm_ref, yp_ref, gb_ref, bp_ref, ps_ref, dp_in, dyp_ref, dgb_ref, gbp_ref, gps_ref):
        del dp_in
        i = pl.program_id(1)

        @pl.when(i == 0)
        def _():
            gbp_ref[...] = jnp.zeros_like(gbp_ref)
            gps_ref[...] = jnp.zeros_like(gps_ref)

        g = gb_ref[...]
        sg = _sigmoid(g)
        dm = dm_ref[...]
        yp = yp_ref[...] + bp_ref[...]
        ps = ps_ref[...]
        dyb = dm * (g * sg)
        dyp = dyb * ps
        dgb_ref[...] = (dm * (yp * ps) * _dsilu(g, sg)).astype(dgb_ref.dtype)
        dyp_ref[...] = dyp.astype(dyp_ref.dtype)
        gbp_ref[...] += jnp.sum(dyp, axis=0, keepdims=True)
        gps_ref[...] += jnp.sum(dyb * yp, axis=0, keepdims=True)

    vec = pl.BlockSpec((1, cb), lambda j, i: (0, j))
    blocks = [((tl, cb), F32)] * 3 + [((tl, cb), MXU)] * 2 + [((1, cb), F32)] * 4
    return pl.pallas_call(
        body, name=name, grid=(nj, L // tl),
        in_specs=[pl.BlockSpec((tl, cb), lambda j, i: (i, nj + j)),
                  pl.BlockSpec((tl, cb), lambda j, i: (i, j)),
                  pl.BlockSpec((tl, cb), lambda j, i: (i + rb, 3 * nj + j)), vec, vec, ANY],
        out_specs=[pl.BlockSpec((tl, cb), lambda j, i: (i, j)),
                   pl.BlockSpec((tl, cb), lambda j, i: (i + rb, 3 * nj + j)), vec, vec],
        out_shape=[_sds((L, W), MXU), _sds(dproj.shape, dproj.dtype), _sds((1, W), F32), _sds((1, W), F32)],
        input_output_aliases={5: 1},
        compiler_params=_params(blocks, dims=("parallel", "arbitrary")),
    )(dmixed, ypre, proj_all, b_pool, pool_scale, dproj)


def _dproj_init(n, lc, W, name):
    cb = _tile(W, CB_MIX, LANE)
    nj = W // cb

    def body(o_ref):
        o_ref[...] = jnp.zeros_like(o_ref)

    return pl.pallas_call(
        body, name=name, grid=(3 * nj,), in_specs=[],
        out_specs=pl.BlockSpec((lc, cb), lambda j: (0, nj + j)),
        out_shape=_sds((n, 4 * W), MXU),
        compiler_params=_params([((lc, cb), MXU)], dims=("parallel",)),
    )()


def _final(x2, out, tgt, gate, gfin, name):
    L, D = x2.shape
    tl = _tile(L, TL_FINAL, SUB16)

    def body(x_ref, o_ref, t_ref, gate_ref, g_ref, dout_ref, dxn_ref, loss_ref, ggf_ref, dgate_ref):
        i = pl.program_id(0)

        @pl.when(i == 0)
        def _():
            loss_ref[...] = jnp.zeros_like(loss_ref)
            ggf_ref[...] = jnp.zeros_like(ggf_ref)
            dgate_ref[...] = jnp.zeros_like(dgate_ref)

        o = o_ref[...]
        gate_v = gate_ref[...]
        gv = g_ref[...]
        xn = x_ref[...] + gate_v * o
        s = lax.rsqrt(jnp.mean(xn * xn, axis=-1, keepdims=True) + EPS)
        xh = xn * s
        err = xh * gv - t_ref[...]
        tok = jnp.mean(err * err, axis=-1, keepdims=True)
        loss_ref[...] += 0.5 * jnp.sum(tok, axis=0, keepdims=True)
        dy = err / D
        ggf_ref[...] += jnp.sum(dy * xh, axis=0, keepdims=True)
        dxh = dy * gv
        dxn = s * (dxh - xh * jnp.mean(dxh * xh, axis=-1, keepdims=True))
        dgate_ref[...] += jnp.sum(dxn * o, axis=0, keepdims=True)
        dout_ref[...] = (gate_v * dxn).astype(dout_ref.dtype)
        dxn_ref[...] = dxn

    row = pl.BlockSpec((tl, D), lambda i: (i, 0))
    vec = pl.BlockSpec((1, D), lambda i: (0, 0))
    blocks = [((tl, D), F32)] * 4 + [((tl, D), MXU)] + [((1, D), F32)] * 4
    return pl.pallas_call(
        body, name=name, grid=(L // tl,), in_specs=[row, row, row, vec, vec],
        out_specs=[row, row, pl.BlockSpec((1, 1), lambda i: (0, 0)), vec, vec],
        out_shape=[_sds((L, D), MXU), _sds((L, D), F32), _sds((1, 1), F32), _sds((1, D), F32), _sds((1, D), F32)],
        compiler_params=_params(blocks, dims=("arbitrary",)),
    )(x2, out, tgt, gate, gfin)


def _adamw_parts(w2, parts, m2, v2, name):
    R, C = w2.shape
    tr = _tile(R, max(SUB16, (256 * 1024) // C), SUB16)

    def body(w_ref, p_ref, m_ref, v_ref, g_ref, d_ref, nm_ref, nv_ref):
        g = p_ref[0].astype(F32)
        for p in range(1, NDEV):
            g = g + p_ref[p].astype(F32)
        delta, nm, nv = _adam(w_ref[...], g, m_ref[...], v_ref[...])
        g_ref[...] = g
        d_ref[...] = delta
        nm_ref[...] = nm
        nv_ref[...] = nv

    row = pl.BlockSpec((tr, C), lambda i: (i, 0))
    blocks = [((tr, C), F32)] * 7 + [((NDEV, tr, C), parts.dtype)]
    return pl.pallas_call(
        body, name=name, grid=(R // tr,),
        in_specs=[row, pl.BlockSpec((NDEV, tr, C), lambda i: (0, i, 0)), row, row],
        out_specs=[row] * 4, out_shape=[_sds((R, C), F32)] * 4,
        compiler_params=_params(blocks, dims=("parallel",)),
    )(w2, parts, m2, v2)


def _small_sum(vs, ga, gc, name):
    ns, nm = vs.shape[1], ga.shape[1]

    def body(v_ref, ga_ref, gc_ref, tot_ref, gb_ref):
        tot = v_ref[0:1, :]
        gb = ga_ref[0:1, :]
        for p in range(1, NDEV):
            tot = tot + v_ref[p:p + 1, :]
            gb = gb + ga_ref[p:p + 1, :]
        for p in range(NDEV):
            gb = gb + gc_ref[p:p + 1, :]
        tot_ref[...] = tot
        gb_ref[...] = gb

    blocks = [((NDEV, ns), F32), ((NDEV, nm), F32), ((NDEV, nm), F32), ((1, ns), F32), ((1, nm), F32)]
    return pl.pallas_call(
        body, name=name, out_shape=[_sds((1, ns), F32), _sds((1, nm), F32)],
        compiler_params=_params(blocks),
    )(vs, ga, gc)


def _adamw_small(g_raw, w, m, v, lam_range, cctx_range, name):
    npk = w.shape[1]

    def body(g_ref, w_ref, m_ref, v_ref, go_ref, d_ref, nm_ref, nv_ref):
        wv = w_ref[...]
        g = g_ref[...]
        idx = lax.broadcasted_iota(jnp.int32, (1, npk), 1)
        in_lam = (idx >= lam_range[0]) & (idx < lam_range[1])
        in_cc = (idx >= cctx_range[0]) & (idx < cctx_range[1])
        sg = _sigmoid(wv)
        g = jnp.where(in_lam, g * (LRU_C * _sigmoid(-wv)), jnp.where(in_cc, g * _dsilu(wv, sg), g))
        delta, nm, nv = _adam(wv, g, m_ref[...], v_ref[...])
        go_ref[...] = g
        d_ref[...] = delta
        nm_ref[...] = nm
        nv_ref[...] = nv

    return pl.pallas_call(
        body, name=name, out_shape=[_sds((1, npk), F32)] * 4,
        compiler_params=_params([((1, npk), F32)] * 8),
    )(g_raw, w, m, v)


def _pack(pieces):
    return jnp.concatenate([p.reshape(1, -1) for p in pieces], axis=1)


def kernel(x, c, ctx, c_ctx, w_ada, b_ada, g_norm, w_in, conv_w, conv_b, lru_lambda, w_rgate, b_rgate, w_igate, b_igate, w_pool, b_pool, pool_scale, w_out, g_final, loss_target, m_c_ctx, m_w_ada, m_b_ada, m_g_norm, m_w_in, m_conv_w, m_conv_b, m_lru_lambda, m_w_rgate, m_b_rgate, m_w_igate, m_b_igate, m_w_pool, m_b_pool, m_pool_scale, m_w_out, m_g_final, v_c_ctx, v_w_ada, v_b_ada, v_g_norm, v_w_in, v_conv_w, v_conv_b, v_lru_lambda, v_w_rgate, v_b_rgate, v_w_igate, v_b_igate, v_w_pool, v_b_pool, v_pool_scale, v_w_out, v_g_final):
    L, D = x.shape[1], x.shape[2]
    lc = ctx.shape[1]
    n = lc + L
    W = conv_b.shape[1]
    heads, hd = w_rgate.shape[2], w_rgate.shape[4]
    G, pd = w_pool.shape[1], w_pool.shape[3]
    na = w_ada.shape[2]
    nb = w_in.shape[2]
    ws = W // NDEV
    me = 4 * lax.axis_index("x") + 2 * lax.axis_index("y") + lax.axis_index("c")

    (win_all, cw_all, lam_all, br_all, bi_all, c_all) = _all_gather(
        [w_in[0].astype(MXU), conv_w[0], lru_lambda[0], b_rgate[0], b_igate[0], c], "gather_w_in")
    *later_w, c_all = lax.optimization_barrier(
        (w_rgate[0].astype(MXU), w_igate[0].astype(MXU), w_pool[0].astype(MXU), w_out[0].astype(MXU), c_all))
    gate_w, rest_w = later_w[:2], later_w[2:]
    sent_gw = _send_start(gate_w, _place(gate_w, False, "place_gate_w"), False, "start_gate_w")
    sent_rw = _send_start(rest_w, _place(rest_w, False, "place_rest_w"), False, "start_rest_w")
    c_all, _, _ = lax.optimization_barrier((c_all, sent_gw[4], sent_rw[4]))
    cw = cw_all.transpose(1, 0, 2).reshape(4, W)
    lam = lam_all.transpose(1, 0, 2).reshape(2, W)
    br = br_all.transpose(1, 0, 2).reshape(2, W)
    bi = bi_all.transpose(1, 0, 2).reshape(2, W)

    cc = jnp.concatenate([c_all.reshape(NDEV, D), c_ctx.reshape(1, D), jnp.zeros((NDEV - 1, D), F32)], axis=0)
    b_loc = lax.dynamic_slice(b_ada, (0, me * na), (1, na))
    mod_loc, s_all = _ada_fwd(cc, w_ada[0], b_loc, "ada_fwd")
    (mod_all,) = _all_gather([mod_loc], "gather_mod")
    mod = mod_all.transpose(1, 0, 2).reshape(2 * NDEV, NDEV * na)
    mod_me = lax.dynamic_slice(mod, (me, 0), (1, 3 * D))
    shift, scale, gate = mod_me[:, :D], mod_me[:, D:2 * D], mod_me[:, 2 * D:]
    shift_c, scale_c = mod[NDEV:NDEV + 1, :D], mod[NDEV:NDEV + 1, D:2 * D]

    x2, ctx2, tgt = x[0], ctx[0], loss_target[0]
    gfin = g_final.reshape(1, D)
    h_all = _norm_mod(x2, g_norm, shift, scale, n, lc, None, "norm_lat")
    h_all = _norm_mod(ctx2, g_norm, shift_c, scale_c, n, 0, h_all, "norm_ctx")
    proj_all = _mm_proj(h_all, win_all, "mm_proj")
    u_all = _conv_fwd(proj_all, cw, conv_b, lc, W, "conv_fwd")
    wr_all, wi_all = _send_wait(sent_gw, u_all, False, "wait_gate_w")
    wr = wr_all.transpose(1, 2, 0, 3, 4).reshape(2, heads, hd, hd)
    wi = wi_all.transpose(1, 2, 0, 3, 4).reshape(2, heads, hd, hd)
    a_all, b_all = _gates_fwd(u_all, wr, wi, br, bi, lam, "gates_fwd")
    hs = _scan_fwd(a_all, b_all, lc, "scan_fwd")
    z = _pool_z(proj_all, lc, W, L, W, False, None, "pool_z")
    wpool_all, wout_all = _send_wait(sent_rw, z, False, "wait_rest_w")
    wpool = wpool_all.transpose(1, 0, 2, 3).reshape(G, pd, pd)
    wout = wout_all.reshape(2 * W, D)
    ypre = _mm_group(z, wpool, "fwd", F32, "mm_pool")
    mixed = _mix_fwd(hs, proj_all, ypre, b_pool, pool_scale, lc, "mix_fwd")
    out = _mm_plain(mixed, wout, NN, F32, "mm_out")
    d_out, dxn, loss_p, ggf, dgate = _final(x2, out, tgt, gate, gfin, "final")

    dmixed = _mm_plain(d_out, wout, NT, F32, "mm_dmixed")
    gwout = _mm_plain(mixed, d_out, TN_DIMS, MXU, "mm_gwout")
    ex_o = [gwout.reshape(NDEV, 2 * W // NDEV, D)]
    sent_o = _send_start(ex_o, _place(ex_o, True, "place_gwout"), True, "start_gwout")
    dmixed, _ = lax.optimization_barrier((dmixed, sent_o[4]))
    dproj = _dproj_init(n, lc, W, "dproj_init")
    dya, dproj = _mixa_bwd(dmixed, hs, proj_all, dproj, lc, W, "mixa_bwd")
    dypre, dproj, gbp, gps = _mixb_bwd(dmixed, ypre, proj_all, b_pool, pool_scale, dproj, lc, W, "mixb_bwd")
    dz = _mm_group(dypre, wpool, "bwd", F32, "mm_dz")
    gwpool = _mm_group(z, dypre, "wgrad", MXU, "mm_gwpool")
    dproj = _pool_z(dz, lc, W, L, W, True, dproj, "pool_z_bwd")
    da, db = _scan_bwd(a_all, hs, dya, lc, "scan_bwd")
    du, gwr, gwi, gbr, gbi, gcl = _gates_bwd(u_all, da, db, wr, wi, br, bi, lam, "gates_bwd")
    ex_s = [gwpool.reshape(G, NDEV, pd // NDEV, pd).transpose(1, 0, 2, 3),
            gwr.reshape(2, heads, NDEV, hd // NDEV, hd).transpose(2, 0, 1, 3, 4),
            gwi.reshape(2, heads, NDEV, hd // NDEV, hd).transpose(2, 0, 1, 3, 4)]
    sent_s = _send_start(ex_s, _place(ex_s, True, "place_gsmall"), True, "start_gsmall")
    du, _ = lax.optimization_barrier((du, sent_s[4]))
    dproj, gcw, gcb = _conv_bwd(du, proj_all, cw, dproj, lc, W, "conv_bwd")
    gwin = _mm_gwin(h_all, dproj, nb, "mm_gwin")
    sent_i = _send_start([gwin], _place([gwin], True, "place_gwin"), True, "start_gwin")
    dh_all = _mm_dh(dproj, win_all, "mm_dh", dep=sent_i[4])
    grad_x, dshift, dscale, ggn = _norm_bwd(x2, dh_all, lc, g_norm, scale, dxn, jnp.zeros((1, D), F32), "norm_bwd_lat")
    _, dshift_c, dscale_c, ggn = _norm_bwd(ctx2, dh_all, 0, g_norm, scale_c, None, ggn, "norm_bwd_ctx")

    dmod_me = jnp.concatenate([dshift, dscale, dgate], axis=1)
    dmod_c = jnp.concatenate([dshift_c, dscale_c, jnp.zeros((1, D), F32)], axis=1)
    smalls = [ggf, ggn, gcw, gcb, gcl, gbr, gbi, gbp, gps, jnp.pad(loss_p, ((0, 0), (0, LANE - 1)))]
    sizes = [s.size for s in smalls]
    small_all, dmod_all, dmodc_all = _all_gather([_pack(smalls), dmod_me, dmod_c], "gather_small")
    ga = lax.dynamic_slice(dmod_all.reshape(NDEV, 3 * D), (0, me * na), (NDEV, na))
    gc = lax.dynamic_slice(dmodc_all.reshape(NDEV, 3 * D), (0, me * na), (NDEV, na))
    g_wada, d_wada, nm_wada, nv_wada, pc = _ada_bwd(s_all, ga, gc, w_ada[0], m_w_ada[0], v_w_ada[0], "ada_bwd")
    (pc_all,) = _all_gather([pc[0:1]], "gather_cctx")
    tot, gb_ada = _small_sum(
        jnp.concatenate([small_all.reshape(NDEV, -1), pc_all.reshape(NDEV, D)], axis=1),
        dmod_all.reshape(NDEV, 3 * D), dmodc_all.reshape(NDEV, 3 * D), "small_sum")
    offs = [0]
    for s in sizes + [D]:
        offs.append(offs[-1] + s)
    t_ggf, t_ggn, t_gcw, t_gcb, t_gcl, t_gbr, t_gbi, t_gbp, t_gps, t_loss, t_pc = [
        tot[:, offs[i]:offs[i + 1]] for i in range(len(offs) - 1)]

    def shard(t, rows):
        return lax.dynamic_slice(t.reshape(rows, W), (0, me * ws), (rows, ws))

    def big(wv, parts, mv, vv, name):
        shp = wv.shape
        C = shp[-1]
        outs = _adamw_parts(wv.reshape(-1, C), parts.reshape(NDEV, -1, C), mv.reshape(-1, C), vv.reshape(-1, C), name)
        return [o.reshape(shp) for o in outs]

    (recv_o,) = _send_wait(sent_o, tot, True, "wait_gwout")
    recv_p, recv_r, recv_i = _send_wait(sent_s, tot, True, "wait_gsmall")
    r_wout = big(w_out, recv_o, m_w_out, v_w_out, "adamw_w_out")
    r_wpool = big(w_pool, recv_p, m_w_pool, v_w_pool, "adamw_w_pool")
    r_wr = big(w_rgate, recv_r, m_w_rgate, v_w_rgate, "adamw_w_rgate")
    r_wi = big(w_igate, recv_i, m_w_igate, v_w_igate, "adamw_w_igate")
    r_wada = [o.reshape(w_ada.shape) for o in (g_wada, d_wada, nm_wada, nv_wada)]

    names = ["c_ctx", "b_ada", "g_norm", "conv_w", "conv_b", "lru_lambda", "b_rgate", "b_igate", "b_pool",
             "pool_scale", "g_final"]
    sw = [c_ctx, b_ada, g_norm, conv_w, conv_b, lru_lambda, b_rgate, b_igate, b_pool, pool_scale, g_final]
    sm = [m_c_ctx, m_b_ada, m_g_norm, m_conv_w, m_conv_b, m_lru_lambda, m_b_rgate, m_b_igate, m_b_pool,
          m_pool_scale, m_g_final]
    sv = [v_c_ctx, v_b_ada, v_g_norm, v_conv_w, v_conv_b, v_lru_lambda, v_b_rgate, v_b_igate, v_b_pool,
          v_pool_scale, v_g_final]
    sg = [t_pc, gb_ada, t_ggn, shard(t_gcw, 4), t_gcb, shard(t_gcl, 2), shard(t_gbr, 2), shard(t_gbi, 2), t_gbp,
          t_gps, t_ggf]
    poffs = [0]
    for wv in sw:
        poffs.append(poffs[-1] + wv.size)
    lam_range = (poffs[5], poffs[6])
    cctx_range = (poffs[0], poffs[1])
    small_out = _adamw_small(_pack(sg), _pack(sw), _pack(sm), _pack(sv), lam_range, cctx_range, "adamw_small")
    (recv_w,) = _send_wait(sent_i, small_out[0], True, "wait_gwin")
    r_win = big(w_in, recv_w, m_w_in, v_w_in, "adamw_w_in")
    r_small = {}
    for i, nm in enumerate(names):
        r_small[nm] = [o[:, poffs[i]:poffs[i + 1]].reshape(sw[i].shape) for o in small_out]

    res = dict(r_small)
    res.update(w_ada=r_wada, w_in=r_win, w_rgate=r_wr, w_igate=r_wi, w_pool=r_wpool, w_out=r_wout)
    order = ["c_ctx", "w_ada", "b_ada", "g_norm", "w_in", "conv_w", "conv_b", "lru_lambda", "w_rgate", "b_rgate",
             "w_igate", "b_igate", "w_pool", "b_pool", "pool_scale", "w_out", "g_final"]
    loss = t_loss[0, 0]
    outs = [loss, grad_x.reshape(x.shape)]
    for q in range(4):
        outs += [res[nm][q] for nm in order]
    return tuple(outs)
```

```python
import functools

import jax
import jax.numpy as jnp
from jax import lax
from jax.experimental import pallas as pl
from jax.experimental.pallas import tpu as pltpu

NDEV = 8
GRID_W = 64
POOL_WINDOWS = (2, 4, 8, 16)
LRU_C = 8.0
EPS = 1e-6
ADAM_LR = 0.001
ADAM_B1 = 0.9
ADAM_B2 = 0.999
ADAM_EPS = 1e-08
ADAM_WD = 0.01
ADAM_STEP = 10

F32 = jnp.float32
MXU = jnp.bfloat16

VMEM_BYTES = 64 * 1024 * 1024
VMEM_SLACK = 8 * 1024 * 1024
SUB = 8
SUB16 = 16
LANE = 128

TM = 1152
TN = 1024
TK = 2048
TL = 256
TL_FINAL = 128
CB_SEQ = 256
CB_SCAN = 128
CB_MIX = 512
TR_CONV = 576

MESH_ID = pl.DeviceIdType.MESH


def _tile(n, pref, align):
    if n <= pref:
        return n
    for t in range(pref - pref % align, 0, -align):
        if n % t == 0:
            return t
    return n


def _nbytes(shape, dtype):
    n = 1
    for s in shape:
        if s is not None:
            n *= s
    return n * jnp.dtype(dtype).itemsize


def _params(blocks, scratch=(), dims=None):
    need = 2 * sum(_nbytes(s, d) for s, d in blocks) + sum(_nbytes(s, d) for s, d in scratch) + VMEM_SLACK
    kw = dict(vmem_limit_bytes=int(min(max(need, 2 * VMEM_SLACK), VMEM_BYTES - VMEM_SLACK // 2)))
    if dims is not None:
        kw["dimension_semantics"] = dims
    return pltpu.CompilerParams(**kw)


def _sds(shape, dtype):
    return jax.ShapeDtypeStruct(tuple(shape), dtype)


ANY = pl.BlockSpec(memory_space=pl.ANY)


def _ids():
    return lax.axis_index("x"), lax.axis_index("y"), lax.axis_index("c")


def _sigmoid(v):
    return jax.nn.sigmoid(v)


def _neg_expm1(v):
    series = -v * (1.0 + v * (0.5 + v * (1.0 / 6.0 + v * (1.0 / 24.0))))
    return jnp.where(v > -1e-2, series, 1.0 - jnp.exp(v))


def _softplus(v):
    return jnp.maximum(v, 0.0) + jnp.log1p(jnp.exp(-jnp.abs(v)))


def _all_gather(xs, name):
    n = len(xs)

    def body(*refs):
        x_refs, o_refs = refs[:n], refs[n:2 * n]
        send_sems, recv_sems, local_sems = refs[2 * n:]
        x, y, c = _ids()
        me, sibling = (x, y, c), (x, y, 1 - c)
        chips = [(1 - x, y), (x, 1 - y), (1 - x, 1 - y)]

        def slot(a, p):
            return o_refs[a].at[4 * p[0] + 2 * p[1] + p[2]]

        def copy(a, k, block, to, src=None):
            return pltpu.make_async_remote_copy(
                src_ref=slot(a, block) if src is None else src, dst_ref=slot(a, block),
                send_sem=send_sems.at[7 * a + k], recv_sem=recv_sems.at[7 * a + k],
                device_id=to, device_id_type=MESH_ID)

        mine, first, passed = [], [], []
        for a in range(n):
            m = pltpu.make_async_copy(x_refs[a], slot(a, me), local_sems.at[a])
            m.start()
            mine.append(m)
            f = [copy(a, 0, me, sibling, src=x_refs[a])]
            f += [copy(a, 1 + j, me, (*chip, c), src=x_refs[a]) for j, chip in enumerate(chips)]
            for cp in f:
                cp.start()
            first += f
        for a in range(n):
            for j, chip in enumerate(chips):
                copy(a, 1 + j, (*chip, c), me).wait_recv()
                p = copy(a, 4 + j, (*chip, c), sibling)
                p.start()
                passed.append(p)
        for a in range(n):
            copy(a, 0, sibling, me).wait_recv()
            for j, chip in enumerate(chips):
                copy(a, 4 + j, (*chip, 1 - c), me).wait_recv()
        for cp in first + passed:
            cp.wait_send()
        for m in mine:
            m.wait()

    return pl.pallas_call(
        body, name=name,
        out_shape=[_sds((NDEV,) + v.shape, v.dtype) for v in xs],
        in_specs=[ANY] * n, out_specs=[ANY] * n,
        scratch_shapes=[pltpu.SemaphoreType.DMA((7 * n,)), pltpu.SemaphoreType.DMA((7 * n,)),
                        pltpu.SemaphoreType.DMA((n,))],
    )(*xs)


HBM = pl.BlockSpec(memory_space=pltpu.HBM)
SEM = pl.BlockSpec(memory_space=pltpu.SEMAPHORE)
EFFECT = pltpu.SideEffectType.DATAFLOW_SIDE_EFFECTING


def _peers():
    x, y, c = _ids()
    out = []
    for k in range(1, NDEV):
        px = 1 - x if k & 4 else x
        py = 1 - y if k & 2 else y
        pc = 1 - c if k & 1 else c
        out.append(((px, py, pc), 4 * px + 2 * py + pc))
    return out, 4 * x + 2 * y + c


def _tie(v, deps, name):
    def body(v_ref, *rest):
        rest[-1][...] = v_ref[...]

    vmem = pl.BlockSpec(memory_space=pltpu.VMEM)
    return pl.pallas_call(
        body, name=name, out_shape=_sds(v.shape, v.dtype), in_specs=[vmem] + [ANY] * len(deps), out_specs=vmem,
    )(v, *deps)


def _place(srcs, from_slot, name, deps=()):
    n = len(srcs)
    blks = [v.shape[1:] if from_slot else v.shape for v in srcs]

    nd = len(deps)

    def body(*refs):
        s_refs, l_refs = refs[:n], refs[n + nd:2 * n + nd]
        bufs, sems = refs[2 * n + nd:3 * n + nd], refs[3 * n + nd]
        x, y, c = _ids()
        me = 4 * x + 2 * y + c
        ins = [pltpu.make_async_copy(s_refs[a].at[me] if from_slot else s_refs[a], bufs[a], sems.at[a])
               for a in range(n)]
        outs = [pltpu.make_async_copy(bufs[a], l_refs[a].at[me], sems.at[n + a]) for a in range(n)]
        for cp in ins:
            cp.start()
        for a in range(n):
            ins[a].wait()
            outs[a].start()
        for cp in outs:
            cp.wait()

    scratch = [(b, v.dtype) for b, v in zip(blks, srcs)]
    return pl.pallas_call(
        body, name=name, out_shape=[_sds((NDEV,) + b, v.dtype) for b, v in zip(blks, srcs)],
        in_specs=[ANY] * (n + nd), out_specs=[ANY] * n,
        scratch_shapes=[pltpu.VMEM(b, d) for b, d in scratch] + [pltpu.SemaphoreType.DMA((2 * n,))],
        compiler_params=_params([], scratch),
    )(*srcs, *deps)


def _send_copies(s_refs, l_refs, ssem, rsem, from_slot, receiving):
    peers, me = _peers()
    out = []
    for a in range(len(s_refs)):
        for k, (dev, idx) in enumerate(peers):
            out.append(pltpu.make_async_remote_copy(
                src_ref=s_refs[a].at[idx] if from_slot else s_refs[a],
                dst_ref=l_refs[a].at[idx if receiving else me],
                send_sem=ssem.at[7 * a + k], recv_sem=rsem.at[7 * a + k], device_id=dev, device_id_type=MESH_ID))
    return out


def _send_start(srcs, lands, from_slot, name):
    n = len(srcs)

    def body(*refs):
        s_refs, l_refs = refs[:n], refs[n:2 * n]
        ssem, rsem = refs[2 * n], refs[2 * n + 1]
        token = refs[-1]
        for send in _send_copies(s_refs, l_refs, ssem, rsem, from_slot, False):
            send.start()
        token[...] = jnp.zeros_like(token)

    bufs = list(srcs) + list(lands)
    outs = pl.pallas_call(
        body, name=name,
        out_shape=[pltpu.SemaphoreType.DMA((7 * n,)), pltpu.SemaphoreType.DMA((7 * n,))]
        + [pltpu.HBM(v.shape, v.dtype) for v in bufs] + [_sds((SUB, LANE), F32)],
        in_specs=[HBM] * (2 * n), out_specs=[SEM, SEM] + [HBM] * (2 * n) + [pl.BlockSpec(memory_space=pltpu.VMEM)],
        input_output_aliases={i: 2 + i for i in range(2 * n)},
        compiler_params=pltpu.CompilerParams(has_side_effects=EFFECT),
    )(*[pltpu.with_memory_space_constraint(v, pltpu.HBM) for v in bufs])
    return outs[0], outs[1], list(outs[2:2 + n]), list(outs[2 + n:2 + 2 * n]), outs[-1]


def _send_wait(started, after, from_slot, name):
    ssem, rsem, srcs, lands, _ = started
    n = len(srcs)

    def body(*refs):
        s_refs, l_refs = refs[:n], refs[n:2 * n]
        ssem_ref, rsem_ref = refs[2 * n], refs[2 * n + 1]
        for recv in _send_copies(s_refs, l_refs, ssem_ref, rsem_ref, from_slot, True):
            recv.wait_send()
            recv.wait_recv()

    bufs = list(srcs) + list(lands)
    outs = pl.pallas_call(
        body, name=name, out_shape=[pltpu.HBM(v.shape, v.dtype) for v in bufs],
        in_specs=[HBM] * (2 * n) + [SEM, SEM, ANY], out_specs=[HBM] * (2 * n),
        input_output_aliases={i: i for i in range(2 * n)},
        compiler_params=pltpu.CompilerParams(has_side_effects=EFFECT),
    )(*bufs, ssem, rsem, after)
    return list(outs[n:])


NN = (((1,), (0,)), ((), ()))
NT = (((1,), (1,)), ((), ()))
TN_DIMS = (((0,), (0,)), ((), ()))


def _mm(a, b, *, grid, a_spec, b_spec, o_spec, out_shape, acc_shape, dims, name, dep=None):
    k_axis = len(grid) - 1
    nk = grid[k_axis]
    extra = [] if dep is None else [dep]

    def body(a_ref, b_ref, *rest):
        o_ref, acc_ref = rest[-2], rest[-1]
        k = pl.program_id(k_axis)

        def prod():
            return lax.dot_general(a_ref[...], b_ref[...], dims, preferred_element_type=F32)

        if nk == 1:
            o_ref[...] = prod().astype(o_ref.dtype)
            return

        @pl.when(k == 0)
        def _():
            acc_ref[...] = prod()

        if nk > 2:
            @pl.when((k > 0) & (k < nk - 1))
            def _():
                acc_ref[...] += prod()

        @pl.when(k == nk - 1)
        def _():
            o_ref[...] = (acc_ref[...] + prod()).astype(o_ref.dtype)

    blocks = [(a_spec.block_shape, a.dtype), (b_spec.block_shape, b.dtype), (o_spec.block_shape, out_shape.dtype)]
    return pl.pallas_call(
        body, name=name, grid=grid, in_specs=[a_spec, b_spec] + [ANY] * len(extra), out_specs=o_spec,
        out_shape=out_shape, scratch_shapes=[pltpu.VMEM(acc_shape, F32)],
        compiler_params=_params(blocks, [(acc_shape, F32)], ("parallel",) * k_axis + ("arbitrary",)),
    )(a, b, *extra)


def _mm_plain(a, b, dims, out_dtype, name):
    if dims == TN_DIMS:
        (K, M), N = a.shape, b.shape[1]
    elif dims == NT:
        (M, K), N = a.shape, b.shape[0]
    else:
        (M, K), N = a.shape, b.shape[1]
    tm, tn = _tile(M, TM, LANE), _tile(N, TN, LANE)
    tk = _tile(K, TK, LANE if dims != TN_DIMS else SUB16)
    if dims == TN_DIMS:
        a_spec = pl.BlockSpec((tk, tm), lambda i, j, k: (k, i))
    else:
        a_spec = pl.BlockSpec((tm, tk), lambda i, j, k: (i, k))
    if dims == NT:
        b_spec = pl.BlockSpec((tn, tk), lambda i, j, k: (j, k))
    else:
        b_spec = pl.BlockSpec((tk, tn), lambda i, j, k: (k, j))
    return _mm(a, b, grid=(M // tm, N // tn, K // tk), a_spec=a_spec, b_spec=b_spec,
               o_spec=pl.BlockSpec((tm, tn), lambda i, j, k: (i, j)),
               out_shape=_sds((M, N), out_dtype), acc_shape=(tm, tn), dims=dims, name=name)


def _mm_proj(h_all, win_all, name):
    n, D = h_all.shape
    nb = win_all.shape[2]
    tm, tn, tk = _tile(n, TM, SUB16), _tile(nb, TN, LANE), _tile(D, TK, LANE)
    nbn = nb // tn
    return _mm(h_all, win_all, grid=(n // tm, NDEV * nbn, D // tk),
               a_spec=pl.BlockSpec((tm, tk), lambda i, j, k: (i, k)),
               b_spec=pl.BlockSpec((None, tk, tn), lambda i, j, k: (j // nbn, k, j % nbn)),
               o_spec=pl.BlockSpec((tm, tn), lambda i, j, k: (i, j)),
               out_shape=_sds((n, NDEV * nb), F32), acc_shape=(tm, tn), dims=NN, name=name)


def _mm_dh(dproj, win_all, name, dep=None):
    n = dproj.shape[0]
    _, D, nb = win_all.shape
    tm, tn, tk = _tile(n, TM, SUB16), _tile(D, TN, LANE), _tile(nb, TK, LANE)
    nbk = nb // tk
    return _mm(dproj, win_all, grid=(n // tm, D // tn, NDEV * nbk),
               a_spec=pl.BlockSpec((tm, tk), lambda i, j, k: (i, k)),
               b_spec=pl.BlockSpec((None, tn, tk), lambda i, j, k: (k // nbk, j, k % nbk)),
               o_spec=pl.BlockSpec((tm, tn), lambda i, j, k: (i, j)),
               out_shape=_sds((n, D), F32), acc_shape=(tm, tn), dims=NT, name=name, dep=dep)


def _mm_gwin(h_all, dproj, nb, part, nparts, name, dep=None):
    n, D = h_all.shape
    nbp = nb // nparts
    tm, tn, tk = _tile(D, TM, LANE), _tile(nbp, TN, LANE), _tile(n, TK, SUB16)
    nbn = nbp // tn
    return _mm(h_all, dproj, grid=(D // tm, NDEV * nbn, n // tk),
               a_spec=pl.BlockSpec((tk, tm), lambda i, j, k: (k, i)),
               b_spec=pl.BlockSpec((tk, tn), lambda i, j, k: (k, (j // nbn) * (nb // tn) + part * nbn + j % nbn)),
               o_spec=pl.BlockSpec((None, tm, tn), lambda i, j, k: (j // nbn, i, j % nbn)),
               out_shape=_sds((NDEV, D, nbp), MXU), acc_shape=(tm, tn), dims=TN_DIMS, name=name, dep=dep)


def _mm_group(a, b, mode, out_dtype, name):
    if mode == "wgrad":
        L, W = a.shape
        G = len(POOL_WINDOWS)
        pd = W // G
        tm, tn, tk = _tile(pd, TM, LANE), _tile(pd, TN, LANE), _tile(L, TK, SUB16)
        nm, nn = pd // tm, pd // tn
        return _mm(a, b, grid=(G, nm, nn, L // tk),
                   a_spec=pl.BlockSpec((tk, tm), lambda g, i, j, k: (k, g * nm + i)),
                   b_spec=pl.BlockSpec((tk, tn), lambda g, i, j, k: (k, g * nn + j)),
                   o_spec=pl.BlockSpec((None, tm, tn), lambda g, i, j, k: (g, i, j)),
                   out_shape=_sds((G, pd, pd), out_dtype), acc_shape=(tm, tn), dims=TN_DIMS, name=name)
    L, W = a.shape
    G, pd, _ = b.shape
    tm, tn, tk = _tile(L, TM, SUB16), _tile(pd, TN, LANE), _tile(pd, TK, LANE)
    nn, nk = pd // tn, pd // tk
    if mode == "fwd":
        b_spec = pl.BlockSpec((None, tk, tn), lambda g, i, j, k: (g, k, j))
        dims = NN
    else:
        b_spec = pl.BlockSpec((None, tn, tk), lambda g, i, j, k: (g, j, k))
        dims = NT
    return _mm(a, b, grid=(G, L // tm, nn, nk),
               a_spec=pl.BlockSpec((tm, tk), lambda g, i, j, k: (i, g * nk + k)),
               b_spec=b_spec,
               o_spec=pl.BlockSpec((tm, tn), lambda g, i, j, k: (i, g * nn + j)),
               out_shape=_sds((L, W), out_dtype), acc_shape=(tm, tn), dims=dims, name=name)


def _ada_fwd(cc, w_loc, b_loc, name):
    R, D = cc.shape
    na = w_loc.shape[1]
    tk = _tile(D, 512, LANE)

    def body(c_ref, w_ref, b_ref, mod_ref, s_ref):
        k = pl.program_id(0)
        cv = c_ref[...]
        s = cv * _sigmoid(cv)
        s_ref[...] = s

        @pl.when(k == 0)
        def _():
            mod_ref[...] = jnp.broadcast_to(b_ref[...], mod_ref.shape)

        mod_ref[...] += lax.dot_general(s.astype(MXU), w_ref[...].astype(MXU), NN, preferred_element_type=F32)

    blocks = [((R, tk), F32), ((tk, na), F32), ((1, na), F32), ((R, na), F32), ((R, tk), F32)]
    return pl.pallas_call(
        body, name=name, grid=(D // tk,),
        in_specs=[pl.BlockSpec((R, tk), lambda k: (0, k)), pl.BlockSpec((tk, na), lambda k: (k, 0)),
                  pl.BlockSpec((1, na), lambda k: (0, 0))],
        out_specs=[pl.BlockSpec((R, na), lambda k: (0, 0)), pl.BlockSpec((R, tk), lambda k: (0, k))],
        out_shape=[_sds((R, na), F32), _sds((R, D), F32)],
        compiler_params=_params(blocks, dims=("arbitrary",)),
    )(cc, w_loc, b_loc)


def _adam(w, g, m, v):
    m = ADAM_B1 * m + (1.0 - ADAM_B1) * g
    v = ADAM_B2 * v + (1.0 - ADAM_B2) * (g * g)
    m_hat = m / (1.0 - ADAM_B1 ** ADAM_STEP)
    v_hat = v / (1.0 - ADAM_B2 ** ADAM_STEP)
    delta = -ADAM_LR * (m_hat / (jnp.sqrt(v_hat) + ADAM_EPS) + ADAM_WD * w)
    return delta, m, v


def _ada_bwd(s_all, ga, gc, w_loc, m_loc, v_loc, name):
    D, na = w_loc.shape
    tr = _tile(D, 256, LANE)

    def body(s_ref, ga_ref, gc_ref, w_ref, m_ref, v_ref, g_ref, d_ref, nm_ref, nv_ref, pc_ref):
        dmc = gc_ref[0:1, :]
        for p in range(1, NDEV):
            dmc = dmc + gc_ref[p:p + 1, :]
        rows = lax.broadcasted_iota(jnp.int32, (NDEV, na), 0)
        dmc8 = jnp.where(rows == 0, jnp.broadcast_to(dmc, (NDEV, na)), 0.0)
        dm = jnp.concatenate([ga_ref[...], dmc8], axis=0).astype(MXU)
        dmc16 = jnp.concatenate([dmc8, jnp.zeros_like(dmc8)], axis=0).astype(MXU)
        w = w_ref[...]
        g = lax.dot_general(s_ref[...].astype(MXU), dm, TN_DIMS, preferred_element_type=F32)
        pc_ref[...] = lax.dot_general(dmc16, w.astype(MXU), NT, preferred_element_type=F32)
        delta, nm, nv = _adam(w, g, m_ref[...], v_ref[...])
        g_ref[...] = g
        d_ref[...] = delta
        nm_ref[...] = nm
        nv_ref[...] = nv

    big = pl.BlockSpec((tr, na), lambda i: (i, 0))
    full = pl.BlockSpec((NDEV, na), lambda i: (0, 0))
    srow = pl.BlockSpec((2 * NDEV, tr), lambda i: (0, i))
    blocks = [((2 * NDEV, tr), F32)] * 2 + [((NDEV, na), F32)] * 2 + [((tr, na), F32)] * 7
    return pl.pallas_call(
        body, name=name, grid=(D // tr,),
        in_specs=[srow, full, full, big, big, big],
        out_specs=[big, big, big, big, srow],
        out_shape=[_sds((D, na), F32)] * 4 + [_sds((2 * NDEV, D), F32)],
        compiler_params=_params(blocks, dims=("parallel",)),
    )(s_all, ga, gc, w_loc, m_loc, v_loc)


def _norm_mod(x2, g, shift, scale, n, row0, h_prev, name):
    R, D = x2.shape
    tl = _tile(R, TL, SUB16)
    assert row0 % tl == 0
    b0 = row0 // tl

    def body(x_ref, g_ref, sh_ref, sc_ref, *rest):
        o_ref = rest[-1]
        xv = x_ref[...]
        s = lax.rsqrt(jnp.mean(xv * xv, axis=-1, keepdims=True) + EPS)
        nrm = xv * s * g_ref[...]
        o_ref[...] = (nrm * (1.0 + sc_ref[...]) + sh_ref[...]).astype(o_ref.dtype)

    vec = pl.BlockSpec((1, D), lambda i: (0, 0))
    in_specs = [pl.BlockSpec((tl, D), lambda i: (i, 0)), vec, vec, vec]
    args = [x2, g, shift, scale]
    aliases = {}
    if h_prev is not None:
        in_specs.append(ANY)
        args.append(h_prev)
        aliases = {4: 0}
    blocks = [((tl, D), F32), ((tl, D), MXU)] + [((1, D), F32)] * 3
    return pl.pallas_call(
        body, name=name, grid=(R // tl,), in_specs=in_specs,
        out_specs=pl.BlockSpec((tl, D), lambda i: (i + b0, 0)),
        out_shape=_sds((n, D), MXU), input_output_aliases=aliases,
        compiler_params=_params(blocks, dims=("parallel",)),
    )(*args)


def _norm_bwd(x2, dh_all, row0, g, scale, dxn, ggn0, name):
    R, D = x2.shape
    tl = _tile(R, TL_FINAL, SUB)
    assert row0 % tl == 0
    b0 = row0 // tl
    with_x = dxn is not None

    def body(*refs):
        if with_x:
            x_ref, dh_ref, g_ref, sc_ref, gg0_ref, dxn_ref, gx_ref, dsh_ref, dsc_ref, gg_ref = refs
        else:
            x_ref, dh_ref, g_ref, sc_ref, gg0_ref, dsh_ref, dsc_ref, gg_ref = refs
        i = pl.program_id(0)

        @pl.when(i == 0)
        def _():
            dsh_ref[...] = jnp.zeros_like(dsh_ref)
            dsc_ref[...] = jnp.zeros_like(dsc_ref)
            gg_ref[...] = gg0_ref[...]

        xv = x_ref[...]
        dh = dh_ref[...]
        gv = g_ref[...]
        s = lax.rsqrt(jnp.mean(xv * xv, axis=-1, keepdims=True) + EPS)
        xh = xv * s
        dsh_ref[...] += jnp.sum(dh, axis=0, keepdims=True)
        dsc_ref[...] += jnp.sum(dh * (xh * gv), axis=0, keepdims=True)
        dn = dh * (1.0 + sc_ref[...])
        gg_ref[...] += jnp.sum(dn * xh, axis=0, keepdims=True)
        if with_x:
            dxh = dn * gv
            dx = s * (dxh - xh * jnp.mean(dxh * xh, axis=-1, keepdims=True))
            gx_ref[...] = dx + dxn_ref[...]

    vec = pl.BlockSpec((1, D), lambda i: (0, 0))
    row = pl.BlockSpec((tl, D), lambda i: (i, 0))
    in_specs = [row, pl.BlockSpec((tl, D), lambda i: (i + b0, 0)), vec, vec, vec]
    args = [x2, dh_all, g, scale, ggn0]
    out_specs = [vec, vec, vec]
    out_shape = [_sds((1, D), F32)] * 3
    if with_x:
        in_specs.append(row)
        args.append(dxn)
        out_specs = [row] + out_specs
        out_shape = [_sds((R, D), F32)] + out_shape
    blocks = [((tl, D), F32)] * (4 if with_x else 2) + [((1, D), F32)] * 6
    outs = pl.pallas_call(
        body, name=name, grid=(R // tl,), in_specs=in_specs, out_specs=out_specs, out_shape=out_shape,
        compiler_params=_params(blocks, dims=("arbitrary",)),
    )(*args)
    return tuple(outs) if with_x else (None,) + tuple(outs)


def _tap_valid(t, o, lc, n):
    tt = t + o
    in_ctx = t < lc
    return (tt >= jnp.where(in_ctx, 0, lc)) & (tt < jnp.where(in_ctx, lc, n))


def _conv_fwd(proj_all, cw, cb, lc, W, name):
    n = proj_all.shape[0]
    cbk = _tile(W, CB_SEQ, LANE)
    tr = _tile(n, TR_CONV, SUB16)
    ext = tr + 2 * SUB

    def body(x_ref, w_ref, b_ref, u_ref, xp_ref):
        xp_ref[0:SUB, :] = jnp.zeros((SUB, cbk), F32)
        xp_ref[n + SUB:n + 2 * SUB, :] = jnp.zeros((SUB, cbk), F32)
        xp_ref[SUB:n + SUB, :] = x_ref[...]
        w = w_ref[...]
        bias = b_ref[...]

        def chunk(ci, carry):
            r0 = pl.multiple_of(ci * tr, SUB16)
            xe = xp_ref[pl.ds(r0, ext), :]
            t = r0 + lax.broadcasted_iota(jnp.int32, (tr, cbk), 0)
            acc = jnp.broadcast_to(bias, (tr, cbk))
            for k in range(4):
                o = k - 1
                sh = xe if o == 0 else pltpu.roll(xe, (-o) % ext, 0)
                acc = acc + jnp.where(_tap_valid(t, o, lc, n), sh[SUB:tr + SUB], 0.0) * w[k:k + 1]
            u_ref[pl.ds(r0, tr), :] = acc
            return carry

        lax.fori_loop(0, n // tr, chunk, 0)

    blocks = [((n, cbk), F32)] * 2 + [((4, cbk), F32), ((1, cbk), F32)]
    scratch = [((n + 2 * SUB, cbk), F32)]
    return pl.pallas_call(
        body, name=name, grid=(W // cbk,),
        in_specs=[pl.BlockSpec((n, cbk), lambda j: (0, j)), pl.BlockSpec((4, cbk), lambda j: (0, j)),
                  pl.BlockSpec((1, cbk), lambda j: (0, j))],
        out_specs=pl.BlockSpec((n, cbk), lambda j: (0, j)),
        out_shape=_sds((n, W), F32),
        scratch_shapes=[pltpu.VMEM(s, d) for s, d in scratch],
        compiler_params=_params(blocks, scratch, ("parallel",)),
    )(proj_all, cw, cb)


def _conv_bwd(du_all, proj_all, cw, dproj, lc, W, name):
    n = du_all.shape[0]
    cbk = _tile(W, CB_SEQ, LANE)
    tr = _tile(n, TR_CONV, SUB16)
    ext = tr + 2 * SUB

    def body(du_ref, x_ref, w_ref, dp_in, dx_ref, gw_ref, gb_ref, dp_ref, xp_ref):
        del dp_in
        for ref, src in ((dp_ref, du_ref), (xp_ref, x_ref)):
            ref[0:SUB, :] = jnp.zeros((SUB, cbk), F32)
            ref[n + SUB:n + 2 * SUB, :] = jnp.zeros((SUB, cbk), F32)
            ref[SUB:n + SUB, :] = src[...]
        w = w_ref[...]

        def fold(v):
            return jnp.sum(v.reshape(tr // SUB, SUB, cbk), axis=0)

        def chunk(ci, carry):
            r0 = pl.multiple_of(ci * tr, SUB16)
            de = dp_ref[pl.ds(r0, ext), :]
            xe = xp_ref[pl.ds(r0, ext), :]
            t = r0 + lax.broadcasted_iota(jnp.int32, (tr, cbk), 0)
            d0 = de[SUB:tr + SUB]
            dx = jnp.zeros((tr, cbk), F32)
            new = []
            for k in range(4):
                o = k - 1
                dsh = de if o == 0 else pltpu.roll(de, o % ext, 0)
                dx = dx + jnp.where(_tap_valid(t, -o, lc, n), dsh[SUB:tr + SUB], 0.0) * w[k:k + 1]
                xsh = xe if o == 0 else pltpu.roll(xe, (-o) % ext, 0)
                new.append(carry[k] + fold(d0 * jnp.where(_tap_valid(t, o, lc, n), xsh[SUB:tr + SUB], 0.0)))
            new.append(carry[4] + fold(d0))
            dx_ref[pl.ds(r0, tr), :] = dx.astype(dx_ref.dtype)
            return tuple(new)

        zero = jnp.zeros((SUB, cbk), F32)
        acc = lax.fori_loop(0, n // tr, chunk, (zero,) * 5)
        for k in range(4):
            gw_ref[k:k + 1, :] = jnp.sum(acc[k], axis=0, keepdims=True)
        gb_ref[...] = jnp.sum(acc[4], axis=0, keepdims=True)

    col = pl.BlockSpec((n, cbk), lambda j: (0, j))
    blocks = [((n, cbk), F32)] * 2 + [((n, cbk), MXU), ((4, cbk), F32), ((4, cbk), F32), ((1, cbk), F32)]
    scratch = [((n + 2 * SUB, cbk), F32)] * 2
    return pl.pallas_call(
        body, name=name, grid=(W // cbk,),
        in_specs=[col, col, pl.BlockSpec((4, cbk), lambda j: (0, j)), ANY],
        out_specs=[col, pl.BlockSpec((4, cbk), lambda j: (0, j)), pl.BlockSpec((1, cbk), lambda j: (0, j))],
        out_shape=[_sds(dproj.shape, dproj.dtype), _sds((4, W), F32), _sds((1, W), F32)],
        input_output_aliases={3: 0},
        scratch_shapes=[pltpu.VMEM(s, d) for s, d in scratch],
        compiler_params=_params(blocks, scratch, ("parallel",)),
    )(du_all, proj_all, cw, dproj)


def _gate_coeffs(ub, u, d, wr_ref, wi_ref, br_ref, bi_ref, lam_ref):
    c = -LRU_C * _softplus(-lam_ref[d:d + 1, :])
    r = _sigmoid(lax.dot_general(ub, wr_ref[d], NN, preferred_element_type=F32) + br_ref[d:d + 1, :])
    ig = _sigmoid(lax.dot_general(ub, wi_ref[d], NN, preferred_element_type=F32) + bi_ref[d:d + 1, :])
    la = c * r
    a = jnp.exp(la)
    sq = jnp.sqrt(_neg_expm1(2.0 * la))
    return c, r, ig, a, sq


def _gate_specs(tl, hd):
    w_spec = pl.BlockSpec((2, None, hd, hd), lambda h, i: (0, h, 0, 0))
    v_spec = pl.BlockSpec((2, hd), lambda h, i: (0, h))
    return w_spec, v_spec


def _gates_fwd(u_all, wr, wi, br, bi, lam, name):
    n, W = u_all.shape
    heads, hd = wr.shape[1], wr.shape[2]
    tl = _tile(n, TL, SUB16)

    def body(u_ref, wr_ref, wi_ref, br_ref, bi_ref, lam_ref, a_ref, b_ref):
        u = u_ref[...]
        ub = u.astype(MXU)
        for d in range(2):
            _, _, ig, a, sq = _gate_coeffs(ub, u, d, wr_ref, wi_ref, br_ref, bi_ref, lam_ref)
            a_ref[d] = a
            b_ref[d] = sq * (ig * u)

    w_spec, v_spec = _gate_specs(tl, hd)
    o_spec = pl.BlockSpec((2, tl, hd), lambda h, i: (0, i, h))
    blocks = [((tl, hd), F32), ((2, hd, hd), MXU), ((2, hd, hd), MXU)] + [((2, hd), F32)] * 3 + [((2, tl, hd), F32)] * 2
    return pl.pallas_call(
        body, name=name, grid=(heads, n // tl),
        in_specs=[pl.BlockSpec((tl, hd), lambda h, i: (i, h)), w_spec, w_spec, v_spec, v_spec, v_spec],
        out_specs=[o_spec, o_spec], out_shape=[_sds((2, n, W), F32)] * 2,
        compiler_params=_params(blocks, dims=("parallel", "parallel")),
    )(u_all, wr, wi, br, bi, lam)


def _gates_bwd(u_all, da, db, wr, wi, br, bi, lam, name):
    n, W = u_all.shape
    heads, hd = wr.shape[1], wr.shape[2]
    tl = _tile(n, TL, SUB16)
    ni = n // tl

    def body(u_ref, da_ref, db_ref, wr_ref, wi_ref, br_ref, bi_ref, lam_ref,
             du_ref, gwr_ref, gwi_ref, gbr_ref, gbi_ref, gc_ref, accr_ref, acci_ref):
        i = pl.program_id(1)

        @pl.when(i == 0)
        def _():
            accr_ref[...] = jnp.zeros_like(accr_ref)
            acci_ref[...] = jnp.zeros_like(acci_ref)
            gbr_ref[...] = jnp.zeros_like(gbr_ref)
            gbi_ref[...] = jnp.zeros_like(gbi_ref)
            gc_ref[...] = jnp.zeros_like(gc_ref)

        u = u_ref[...]
        ub = u.astype(MXU)
        du = jnp.zeros_like(u)
        for d in range(2):
            c, r, ig, a, sq = _gate_coeffs(ub, u, d, wr_ref, wi_ref, br_ref, bi_ref, lam_ref)
            dbv = db_ref[d]
            t = dbv * sq
            du = du + t * ig
            d_la = da_ref[d] * a - (dbv * ig * u) * (a * a) / sq
            gc_ref[d:d + 1, :] += jnp.sum(d_la * r, axis=0, keepdims=True)
            d_pr = (d_la * c) * (r * (1.0 - r))
            d_pi = (t * u) * (ig * (1.0 - ig))
            gbr_ref[d:d + 1, :] += jnp.sum(d_pr, axis=0, keepdims=True)
            gbi_ref[d:d + 1, :] += jnp.sum(d_pi, axis=0, keepdims=True)
            pb = d_pr.astype(MXU)
            qb = d_pi.astype(MXU)
            du = du + lax.dot_general(pb, wr_ref[d], NT, preferred_element_type=F32)
            du = du + lax.dot_general(qb, wi_ref[d], NT, preferred_element_type=F32)
            accr_ref[d] += lax.dot_general(ub, pb, TN_DIMS, preferred_element_type=F32)
            acci_ref[d] += lax.dot_general(ub, qb, TN_DIMS, preferred_element_type=F32)
        du_ref[...] = du

        @pl.when(i == ni - 1)
        def _():
            gwr_ref[...] = accr_ref[...].astype(gwr_ref.dtype)
            gwi_ref[...] = acci_ref[...].astype(gwi_ref.dtype)

    w_spec, v_spec = _gate_specs(tl, hd)
    u_spec = pl.BlockSpec((tl, hd), lambda h, i: (i, h))
    ab_spec = pl.BlockSpec((2, tl, hd), lambda h, i: (0, i, h))
    blocks = ([((tl, hd), F32)] * 2 + [((2, tl, hd), F32)] * 2 + [((2, hd, hd), MXU)] * 4 + [((2, hd), F32)] * 6)
    scratch = [((2, hd, hd), F32)] * 2
    return pl.pallas_call(
        body, name=name, grid=(heads, ni),
        in_specs=[u_spec, ab_spec, ab_spec, w_spec, w_spec, v_spec, v_spec, v_spec],
        out_specs=[u_spec, w_spec, w_spec, v_spec, v_spec, v_spec],
        out_shape=[_sds((n, W), F32), _sds(wr.shape, MXU), _sds(wi.shape, MXU)] + [_sds((2, W), F32)] * 3,
        scratch_shapes=[pltpu.VMEM(s, d) for s, d in scratch],
        compiler_params=_params(blocks, scratch, ("parallel", "arbitrary")),
    )(u_all, da, db, wr, wi, br, bi, lam)


def _tile_scan(A, B, rows, reverse):
    for s in (1, 2, 4):
        if reverse:
            As, Bs, m = pltpu.roll(A, SUB - s, 0), pltpu.roll(B, SUB - s, 0), rows < SUB - s
        else:
            As, Bs, m = pltpu.roll(A, s, 0), pltpu.roll(B, s, 0), rows >= s
        B = jnp.where(m, A * Bs + B, B)
        A = jnp.where(m, A * As, A)
    return A, B


def _scan_fwd(a_all, b_all, lc, name):
    _, n, W = a_all.shape
    cb = _tile(W, CB_SCAN, LANE)
    nt, ntc = n // SUB, lc // SUB

    def body(a_ref, b_ref, h_ref):
        rows = lax.broadcasted_iota(jnp.int32, (SUB, cb), 0)

        def step(s, carry):
            h0, h1 = carry
            r0 = pl.multiple_of(s * SUB, SUB)
            A, B = _tile_scan(a_ref[0, pl.ds(r0, SUB), :], b_ref[0, pl.ds(r0, SUB), :], rows, False)
            H0 = A * h0 + B
            h_ref[0, pl.ds(r0, SUB), :] = H0
            j1 = jnp.where(s < ntc, ntc - 1 - s, nt - 1 - (s - ntc))
            r1 = pl.multiple_of(j1 * SUB, SUB)
            A, B = _tile_scan(a_ref[1, pl.ds(r1, SUB), :], b_ref[1, pl.ds(r1, SUB), :], rows, True)
            H1 = A * h1 + B
            h_ref[1, pl.ds(r1, SUB), :] = H1
            return H0[SUB - 1:SUB, :], H1[0:1, :]

        zero = jnp.zeros((1, cb), F32)
        lax.fori_loop(0, nt, step, (zero, zero))

    spec = pl.BlockSpec((2, n, cb), lambda j: (0, 0, j))
    return pl.pallas_call(
        body, name=name, grid=(W // cb,), in_specs=[spec, spec], out_specs=spec,
        out_shape=_sds((2, n, W), F32),
        compiler_params=_params([((2, n, cb), F32)] * 3, dims=("parallel",)),
    )(a_all, b_all)


def _scan_bwd(a_all, h_all, dya, lc, name):
    _, n, W = a_all.shape
    cb = _tile(W, CB_SCAN, LANE)
    nt, ntc = n // SUB, lc // SUB
    nl = nt - ntc

    def body(a_ref, h_ref, g_ref, da_ref, db_ref):
        rows = lax.broadcasted_iota(jnp.int32, (SUB, cb), 0)

        def tile(ref, d, j):
            return ref[d, pl.ds(pl.multiple_of(j * SUB, SUB), SUB), :]

        def grad_tile(j):
            jl = jnp.maximum(j - ntc, 0)
            g = g_ref[pl.ds(pl.multiple_of(jl * SUB, SUB), SUB), :]
            return jnp.where(j >= ntc, g, 0.0)

        def put(ref, d, j, v):
            ref[d, pl.ds(pl.multiple_of(j * SUB, SUB), SUB), :] = v

        def step(s, carry):
            mu0, mu1 = carry
            j = nt - 1 - s
            a_t = tile(a_ref, 0, j)
            ap = jnp.where(rows < SUB - 1, pltpu.roll(a_t, SUB - 1, 0), 1.0)
            A, B = _tile_scan(ap, grad_tile(j), rows, True)
            lam = A * mu0 + B
            below = jnp.where(j > 0, tile(h_ref, 0, jnp.maximum(j - 1, 0))[SUB - 1:SUB, :], 0.0)
            hprev = jnp.where(rows >= 1, pltpu.roll(tile(h_ref, 0, j), 1, 0), below)
            put(da_ref, 0, j, lam * hprev)
            put(db_ref, 0, j, lam)
            mu0 = a_t[0:1, :] * lam[0:1, :]
            j = jnp.where(s < nl, ntc + s, s - nl)
            a_t = tile(a_ref, 1, j)
            ap = jnp.where(rows >= 1, pltpu.roll(a_t, 1, 0), 1.0)
            A, B = _tile_scan(ap, grad_tile(j), rows, False)
            lam = A * mu1 + B
            jn = jnp.where(j == nt - 1, 0, jnp.minimum(j + 1, nt - 1))
            above = jnp.where(j == ntc - 1, 0.0, tile(h_ref, 1, jn)[0:1, :])
            hprev = jnp.where(rows < SUB - 1, pltpu.roll(tile(h_ref, 1, j), SUB - 1, 0), above)
            put(da_ref, 1, j, lam * hprev)
            put(db_ref, 1, j, lam)
            mu1 = a_t[SUB - 1:SUB, :] * lam[SUB - 1:SUB, :]
            return mu0, mu1

        zero = jnp.zeros((1, cb), F32)
        lax.fori_loop(0, nt, step, (zero, zero))

    spec = pl.BlockSpec((2, n, cb), lambda j: (0, 0, j))
    g_spec = pl.BlockSpec((n - lc, cb), lambda j: (0, j))
    return pl.pallas_call(
        body, name=name, grid=(W // cb,), in_specs=[spec, spec, g_spec], out_specs=[spec, spec],
        out_shape=[_sds((2, n, W), F32)] * 2,
        compiler_params=_params([((2, n, cb), F32)] * 4 + [((n - lc, cb), F32)], dims=("parallel",)),
    )(a_all, h_all, dya)


def _pool_window(v, w, tl, cb, transpose):
    left = w // 2
    right = w - 1 - left
    pos = lax.broadcasted_iota(jnp.int32, (tl, cb), 0) % GRID_W
    cnt = (jnp.minimum(pos + right, GRID_W - 1) - jnp.maximum(pos - left, 0) + 1).astype(F32)
    src = v / cnt if transpose else v
    lo, hi = (-right, left) if transpose else (-left, right)
    acc = src
    for o in range(lo, hi + 1):
        if o != 0:
            ok = (pos + o >= 0) & (pos + o < GRID_W)
            acc = acc + jnp.where(ok, pltpu.roll(src, (-o) % tl, 0), 0.0)
    return acc - v if transpose else acc / cnt - v


def _pool_z(src, row0, col0, L, W, transpose, dproj, name):
    G = len(POOL_WINDOWS)
    pd = W // G
    tl = _tile(L, TL, GRID_W)
    cb = _tile(pd, CB_SEQ, LANE)
    assert row0 % tl == 0 and col0 % cb == 0
    rb, cbk = row0 // tl, col0 // cb
    nj = pd // cb

    def body(x_ref, *rest):
        o_ref = rest[-1]
        for gi, w in enumerate(POOL_WINDOWS):
            @pl.when(pl.program_id(0) == gi)
            def _(w=w):
                o_ref[...] = _pool_window(x_ref[...], w, tl, cb, transpose).astype(o_ref.dtype)

    plain = pl.BlockSpec((tl, cb), lambda g, i, j: (i, g * nj + j))
    window = pl.BlockSpec((tl, cb), lambda g, i, j: (i + rb, cbk + g * nj + j))
    blocks = [((tl, cb), F32), ((tl, cb), MXU)]
    if transpose:
        return pl.pallas_call(
            body, name=name, grid=(G, L // tl, nj), in_specs=[plain, ANY], out_specs=window,
            out_shape=_sds(dproj.shape, dproj.dtype), input_output_aliases={1: 0},
            compiler_params=_params(blocks, dims=("parallel",) * 3),
        )(src, dproj)
    return pl.pallas_call(
        body, name=name, grid=(G, L // tl, nj), in_specs=[window], out_specs=plain,
        out_shape=_sds((L, W), MXU),
        compiler_params=_params(blocks, dims=("parallel",) * 3),
    )(src)


def _mix_fwd(hs, proj_all, ypre, b_pool, pool_scale, lc, name):
    L, W = ypre.shape
    tl = _tile(L, TL, SUB16)
    cb = _tile(W, CB_MIX, LANE)
    nj = W // cb
    assert lc % tl == 0
    rb = lc // tl

    def body(hs_ref, ga_ref, yp_ref, gb_ref, bp_ref, ps_ref, o_ref):
        p = pl.program_id(2)

        @pl.when(p == 0)
        def _():
            g = ga_ref[...]
            o_ref[...] = ((hs_ref[0] + hs_ref[1]) * (g * _sigmoid(g))).astype(o_ref.dtype)

        @pl.when(p == 1)
        def _():
            g = gb_ref[...]
            yb = (yp_ref[...] + bp_ref[...]) * ps_ref[...]
            o_ref[...] = (yb * (g * _sigmoid(g))).astype(o_ref.dtype)

    vec = pl.BlockSpec((1, cb), lambda i, j, p: (0, j))
    blocks = [((2, tl, cb), F32)] + [((tl, cb), F32)] * 3 + [((tl, cb), MXU)]
    return pl.pallas_call(
        body, name=name, grid=(L // tl, nj, 2),
        in_specs=[pl.BlockSpec((2, tl, cb), lambda i, j, p: (0, i + rb, j)),
                  pl.BlockSpec((tl, cb), lambda i, j, p: (i + rb, 2 * nj + j)),
                  pl.BlockSpec((tl, cb), lambda i, j, p: (i, j)),
                  pl.BlockSpec((tl, cb), lambda i, j, p: (i + rb, 3 * nj + j)), vec, vec],
        out_specs=pl.BlockSpec((tl, cb), lambda i, j, p: (i, p * nj + j)),
        out_shape=_sds((L, 2 * W), MXU),
        compiler_params=_params(blocks, dims=("parallel", "parallel", "arbitrary")),
    )(hs, proj_all, ypre, proj_all, b_pool, pool_scale)


def _dsilu(g, sg):
    return sg * (1.0 + g * (1.0 - sg))


def _mixa_bwd(dmixed, hs, proj_all, dproj, lc, W, name):
    L = dmixed.shape[0]
    tl = _tile(L, TL, SUB16)
    cb = _tile(W, CB_MIX, LANE)
    nj = W // cb
    rb = lc // tl

    def body(dm_ref, hs_ref, ga_ref, dp_in, dya_ref, dga_ref):
        del dp_in
        g = ga_ref[...]
        sg = _sigmoid(g)
        dm = dm_ref[...]
        dya_ref[...] = dm * (g * sg)
        dga_ref[...] = (dm * (hs_ref[0] + hs_ref[1]) * _dsilu(g, sg)).astype(dga_ref.dtype)

    blocks = [((tl, cb), F32)] * 3 + [((2, tl, cb), F32), ((tl, cb), MXU)]
    return pl.pallas_call(
        body, name=name, grid=(L // tl, nj),
        in_specs=[pl.BlockSpec((tl, cb), lambda i, j: (i, j)),
                  pl.BlockSpec((2, tl, cb), lambda i, j: (0, i + rb, j)),
                  pl.BlockSpec((tl, cb), lambda i, j: (i + rb, 2 * nj + j)), ANY],
        out_specs=[pl.BlockSpec((tl, cb), lambda i, j: (i, j)),
                   pl.BlockSpec((tl, cb), lambda i, j: (i + rb, 2 * nj + j))],
        out_shape=[_sds((L, W), F32), _sds(dproj.shape, dproj.dtype)],
        input_output_aliases={3: 1},
        compiler_params=_params(blocks, dims=("parallel", "parallel")),
    )(dmixed, hs, proj_all, dproj)


def _mixb_bwd(dmixed, ypre, proj_all, b_pool, pool_scale, dproj, lc, W, name):
    L = dmixed.shape[0]
    tl = _tile(L, TL, SUB16)
    cb = _tile(W, CB_MIX, LANE)
    nj = W // cb
    rb = lc // tl

    def body(dm_ref, yp_ref, gb_ref, bp_ref, ps_ref, dp_in, dyp_ref, dgb_ref, gbp_ref, gps_ref):
        del dp_in
        i = pl.program_id(1)

        @pl.when(i == 0)
        def _():
            gbp_ref[...] = jnp.zeros_like(gbp_ref)
            gps_ref[...] = jnp.zeros_like(gps_ref)

        g = gb_ref[...]
        sg = _sigmoid(g)
        dm = dm_ref[...]
        yp = yp_ref[...] + bp_ref[...]
        ps = ps_ref[...]
        dyb = dm * (g * sg)
        dyp = dyb * ps
        dgb_ref[...] = (dm * (yp * ps) * _dsilu(g, sg)).astype(dgb_ref.dtype)
        dyp_ref[...] = dyp.astype(dyp_ref.dtype)
        gbp_ref[...] += jnp.sum(dyp, axis=0, keepdims=True)
        gps_ref[...] += jnp.sum(dyb * yp, axis=0, keepdims=True)

    vec = pl.BlockSpec((1, cb), lambda j, i: (0, j))
    blocks = [((tl, cb), F32)] * 3 + [((tl, cb), MXU)] * 2 + [((1, cb), F32)] * 4
    return pl.pallas_call(
        body, name=name, grid=(nj, L // tl),
        in_specs=[pl.BlockSpec((tl, cb), lambda j, i: (i, nj + j)),
                  pl.BlockSpec((tl, cb), lambda j, i: (i, j)),
                  pl.BlockSpec((tl, cb), lambda j, i: (i + rb, 3 * nj + j)), vec, vec, ANY],
        out_specs=[pl.BlockSpec((tl, cb), lambda j, i: (i, j)),
                   pl.BlockSpec((tl, cb), lambda j, i: (i + rb, 3 * nj + j)), vec, vec],
        out_shape=[_sds((L, W), MXU), _sds(dproj.shape, dproj.dtype), _sds((1, W), F32), _sds((1, W), F32)],
        input_output_aliases={5: 1},
        compiler_params=_params(blocks, dims=("parallel", "arbitrary")),
    )(dmixed, ypre, proj_all, b_pool, pool_scale, dproj)


def _dproj_init(n, lc, W, name):
    cb = _tile(W, CB_MIX, LANE)
    nj = W // cb

    def body(o_ref):
        o_ref[...] = jnp.zeros_like(o_ref)

    return pl.pallas_call(
        body, name=name, grid=(3 * nj,), in_specs=[],
        out_specs=pl.BlockSpec((lc, cb), lambda j: (0, nj + j)),
        out_shape=_sds((n, 4 * W), MXU),
        compiler_params=_params([((lc, cb), MXU)], dims=("parallel",)),
    )()


def _final(x2, out, tgt, gate, gfin, name):
    L, D = x2.shape
    tl = _tile(L, TL_FINAL, SUB16)

    def body(x_ref, o_ref, t_ref, gate_ref, g_ref, dout_ref, dxn_ref, loss_ref, ggf_ref, dgate_ref):
        i = pl.program_id(0)

        @pl.when(i == 0)
        def _():
            loss_ref[...] = jnp.zeros_like(loss_ref)
            ggf_ref[...] = jnp.zeros_like(ggf_ref)
            dgate_ref[...] = jnp.zeros_like(dgate_ref)

        o = o_ref[...]
        gate_v = gate_ref[...]
        gv = g_ref[...]
        xn = x_ref[...] + gate_v * o
        s = lax.rsqrt(jnp.mean(xn * xn, axis=-1, keepdims=True) + EPS)
        xh = xn * s
        err = xh * gv - t_ref[...]
        tok = jnp.mean(err * err, axis=-1, keepdims=True)
        loss_ref[...] += 0.5 * jnp.sum(tok, axis=0, keepdims=True)
        dy = err / D
        ggf_ref[...] += jnp.sum(dy * xh, axis=0, keepdims=True)
        dxh = dy * gv
        dxn = s * (dxh - xh * jnp.mean(dxh * xh, axis=-1, keepdims=True))
        dgate_ref[...] += jnp.sum(dxn * o, axis=0, keepdims=True)
        dout_ref[...] = (gate_v * dxn).astype(dout_ref.dtype)
        dxn_ref[...] = dxn

    row = pl.BlockSpec((tl, D), lambda i: (i, 0))
    vec = pl.BlockSpec((1, D), lambda i: (0, 0))
    blocks = [((tl, D), F32)] * 4 + [((tl, D), MXU)] + [((1, D), F32)] * 4
    return pl.pallas_call(
        body, name=name, grid=(L // tl,), in_specs=[row, row, row, vec, vec],
        out_specs=[row, row, pl.BlockSpec((1, 1), lambda i: (0, 0)), vec, vec],
        out_shape=[_sds((L, D), MXU), _sds((L, D), F32), _sds((1, 1), F32), _sds((1, D), F32), _sds((1, D), F32)],
        compiler_params=_params(blocks, dims=("arbitrary",)),
    )(x2, out, tgt, gate, gfin)


def _adamw_parts(w2, parts, m2, v2, name):
    R, C = w2.shape
    nh = len(parts)
    ch = C // nh
    tr = _tile(R, max(SUB16, (256 * 1024) // ch), SUB16)

    def body(w_ref, *rest):
        p_refs = rest[:nh]
        m_ref, v_ref, g_ref, d_ref, nm_ref, nv_ref = rest[nh:]
        for q in range(nh):
            @pl.when(pl.program_id(1) == q)
            def _(p_ref=p_refs[q]):
                g = p_ref[0].astype(F32)
                for p in range(1, NDEV):
                    g = g + p_ref[p].astype(F32)
                delta, nm, nv = _adam(w_ref[...], g, m_ref[...], v_ref[...])
                g_ref[...] = g
                d_ref[...] = delta
                nm_ref[...] = nm
                nv_ref[...] = nv

    blk = pl.BlockSpec((tr, ch), lambda i, h: (i, h))
    p_spec = pl.BlockSpec((NDEV, tr, ch), lambda i, h: (0, i, 0))
    blocks = [((tr, ch), F32)] * 7 + [((NDEV, tr, ch), parts[0].dtype)] * nh
    return pl.pallas_call(
        body, name=name, grid=(R // tr, nh),
        in_specs=[blk] + [p_spec] * nh + [blk, blk],
        out_specs=[blk] * 4, out_shape=[_sds((R, C), F32)] * 4,
        compiler_params=_params(blocks, dims=("parallel", "arbitrary")),
    )(w2, *parts, m2, v2)


def _small_sum(vs, ga, gc, name):
    ns, nm = vs.shape[1], ga.shape[1]

    def body(v_ref, ga_ref, gc_ref, tot_ref, gb_ref):
        tot = v_ref[0:1, :]
        gb = ga_ref[0:1, :]
        for p in range(1, NDEV):
            tot = tot + v_ref[p:p + 1, :]
            gb = gb + ga_ref[p:p + 1, :]
        for p in range(NDEV):
            gb = gb + gc_ref[p:p + 1, :]
        tot_ref[...] = tot
        gb_ref[...] = gb

    blocks = [((NDEV, ns), F32), ((NDEV, nm), F32), ((NDEV, nm), F32), ((1, ns), F32), ((1, nm), F32)]
    return pl.pallas_call(
        body, name=name, out_shape=[_sds((1, ns), F32), _sds((1, nm), F32)],
        compiler_params=_params(blocks),
    )(vs, ga, gc)


def _adamw_small(g_raw, w, m, v, lam_range, cctx_range, name):
    npk = w.shape[1]

    def body(g_ref, w_ref, m_ref, v_ref, go_ref, d_ref, nm_ref, nv_ref):
        wv = w_ref[...]
        g = g_ref[...]
        idx = lax.broadcasted_iota(jnp.int32, (1, npk), 1)
        in_lam = (idx >= lam_range[0]) & (idx < lam_range[1])
        in_cc = (idx >= cctx_range[0]) & (idx < cctx_range[1])
        sg = _sigmoid(wv)
        g = jnp.where(in_lam, g * (LRU_C * _sigmoid(-wv)), jnp.where(in_cc, g * _dsilu(wv, sg), g))
        delta, nm, nv = _adam(wv, g, m_ref[...], v_ref[...])
        go_ref[...] = g
        d_ref[...] = delta
        nm_ref[...] = nm
        nv_ref[...] = nv

    return pl.pallas_call(
        body, name=name, out_shape=[_sds((1, npk), F32)] * 4,
        compiler_params=_params([((1, npk), F32)] * 8),
    )(g_raw, w, m, v)


def _pack(pieces):
    return jnp.concatenate([p.reshape(1, -1) for p in pieces], axis=1)


def kernel(x, c, ctx, c_ctx, w_ada, b_ada, g_norm, w_in, conv_w, conv_b, lru_lambda, w_rgate, b_rgate, w_igate, b_igate, w_pool, b_pool, pool_scale, w_out, g_final, loss_target, m_c_ctx, m_w_ada, m_b_ada, m_g_norm, m_w_in, m_conv_w, m_conv_b, m_lru_lambda, m_w_rgate, m_b_rgate, m_w_igate, m_b_igate, m_w_pool, m_b_pool, m_pool_scale, m_w_out, m_g_final, v_c_ctx, v_w_ada, v_b_ada, v_g_norm, v_w_in, v_conv_w, v_conv_b, v_lru_lambda, v_w_rgate, v_b_rgate, v_w_igate, v_b_igate, v_w_pool, v_b_pool, v_pool_scale, v_w_out, v_g_final):
    L, D = x.shape[1], x.shape[2]
    lc = ctx.shape[1]
    n = lc + L
    W = conv_b.shape[1]
    heads, hd = w_rgate.shape[2], w_rgate.shape[4]
    G, pd = w_pool.shape[1], w_pool.shape[3]
    na = w_ada.shape[2]
    nb = w_in.shape[2]
    ws = W // NDEV
    me = 4 * lax.axis_index("x") + 2 * lax.axis_index("y") + lax.axis_index("c")

    (win_all, cw_all, lam_all, br_all, bi_all, c_all) = _all_gather(
        [w_in[0].astype(MXU), conv_w[0], lru_lambda[0], b_rgate[0], b_igate[0], c], "gather_w_in")
    gate_w = [w_rgate[0].astype(MXU), w_igate[0].astype(MXU)]
    rest_w = [w_pool[0].astype(MXU), w_out[0].astype(MXU)]
    sent_gw = _send_start(gate_w, _place(gate_w, False, "place_gate_w", [c_all]), False, "start_gate_w")
    sent_rw = _send_start(rest_w, _place(rest_w, False, "place_rest_w", [sent_gw[4]]), False, "start_rest_w")
    cw = cw_all.transpose(1, 0, 2).reshape(4, W)
    lam = lam_all.transpose(1, 0, 2).reshape(2, W)
    br = br_all.transpose(1, 0, 2).reshape(2, W)
    bi = bi_all.transpose(1, 0, 2).reshape(2, W)

    cc = jnp.concatenate([c_all.reshape(NDEV, D), c_ctx.reshape(1, D), jnp.zeros((NDEV - 1, D), F32)], axis=0)
    b_loc = lax.dynamic_slice(b_ada, (0, me * na), (1, na))
    cc = _tie(cc, [sent_gw[4], sent_rw[4]], "tie_weights")
    mod_loc, s_all = _ada_fwd(cc, w_ada[0], b_loc, "ada_fwd")
    (mod_all,) = _all_gather([mod_loc], "gather_mod")
    mod = mod_all.transpose(1, 0, 2).reshape(2 * NDEV, NDEV * na)
    mod_me = lax.dynamic_slice(mod, (me, 0), (1, 3 * D))
    shift, scale, gate = mod_me[:, :D], mod_me[:, D:2 * D], mod_me[:, 2 * D:]
    shift_c, scale_c = mod[NDEV:NDEV + 1, :D], mod[NDEV:NDEV + 1, D:2 * D]

    x2, ctx2, tgt = x[0], ctx[0], loss_target[0]
    gfin = g_final.reshape(1, D)
    h_all = _norm_mod(x2, g_norm, shift, scale, n, lc, None, "norm_lat")
    h_all = _norm_mod(ctx2, g_norm, shift_c, scale_c, n, 0, h_all, "norm_ctx")
    proj_all = _mm_proj(h_all, win_all, "mm_proj")
    u_all = _conv_fwd(proj_all, cw, conv_b, lc, W, "conv_fwd")
    wr_all, wi_all = _send_wait(sent_gw, u_all, False, "wait_gate_w")
    wr = wr_all.transpose(1, 2, 0, 3, 4).reshape(2, heads, hd, hd)
    wi = wi_all.transpose(1, 2, 0, 3, 4).reshape(2, heads, hd, hd)
    a_all, b_all = _gates_fwd(u_all, wr, wi, br, bi, lam, "gates_fwd")
    hs = _scan_fwd(a_all, b_all, lc, "scan_fwd")
    z = _pool_z(proj_all, lc, W, L, W, False, None, "pool_z")
    wpool_all, wout_all = _send_wait(sent_rw, z, False, "wait_rest_w")
    wpool = wpool_all.transpose(1, 0, 2, 3).reshape(G, pd, pd)
    wout = wout_all.reshape(2 * W, D)
    ypre = _mm_group(z, wpool, "fwd", F32, "mm_pool")
    mixed = _mix_fwd(hs, proj_all, ypre, b_pool, pool_scale, lc, "mix_fwd")
    out = _mm_plain(mixed, wout, NN, F32, "mm_out")
    d_out, dxn, loss_p, ggf, dgate = _final(x2, out, tgt, gate, gfin, "final")

    dmixed = _mm_plain(d_out, wout, NT, F32, "mm_dmixed")
    gwout = _mm_plain(mixed, d_out, TN_DIMS, MXU, "mm_gwout")
    ex_o = [gwout.reshape(NDEV, 2 * W // NDEV, D)]
    sent_o = _send_start(ex_o, _place(ex_o, True, "place_gwout"), True, "start_gwout")
    dproj = _dproj_init(n, lc, W, "dproj_init")
    dya, dproj = _mixa_bwd(dmixed, hs, proj_all, dproj, lc, W, "mixa_bwd")
    dypre, dproj, gbp, gps = _mixb_bwd(dmixed, ypre, proj_all, _tie(b_pool, [sent_o[4]], "tie_gwout"), pool_scale,
                                       dproj, lc, W, "mixb_bwd")
    dz = _mm_group(dypre, wpool, "bwd", F32, "mm_dz")
    gwpool = _mm_group(z, dypre, "wgrad", MXU, "mm_gwpool")
    dproj = _pool_z(dz, lc, W, L, W, True, dproj, "pool_z_bwd")
    da, db = _scan_bwd(a_all, hs, dya, lc, "scan_bwd")
    du, gwr, gwi, gbr, gbi, gcl = _gates_bwd(u_all, da, db, wr, wi, br, bi, lam, "gates_bwd")
    ex_s = [gwpool.reshape(G, NDEV, pd // NDEV, pd).transpose(1, 0, 2, 3),
            gwr.reshape(2, heads, NDEV, hd // NDEV, hd).transpose(2, 0, 1, 3, 4),
            gwi.reshape(2, heads, NDEV, hd // NDEV, hd).transpose(2, 0, 1, 3, 4)]
    sent_s = _send_start(ex_s, _place(ex_s, True, "place_gsmall"), True, "start_gsmall")
    dproj, gcw, gcb = _conv_bwd(du, proj_all, _tie(cw, [sent_s[4]], "tie_gsmall"), dproj, lc, W, "conv_bwd")
    gwin_a = _mm_gwin(h_all, dproj, nb, 0, 2, "mm_gwin_a")
    sent_ia = _send_start([gwin_a], _place([gwin_a], True, "place_gwin_a"), True, "start_gwin_a")
    gwin_b = _mm_gwin(h_all, dproj, nb, 1, 2, "mm_gwin_b", dep=sent_ia[4])
    sent_ib = _send_start([gwin_b], _place([gwin_b], True, "place_gwin_b"), True, "start_gwin_b")
    dh_all = _mm_dh(dproj, win_all, "mm_dh", dep=sent_ib[4])
    grad_x, dshift, dscale, ggn = _norm_bwd(x2, dh_all, lc, g_norm, scale, dxn, jnp.zeros((1, D), F32), "norm_bwd_lat")
    _, dshift_c, dscale_c, ggn = _norm_bwd(ctx2, dh_all, 0, g_norm, scale_c, None, ggn, "norm_bwd_ctx")

    dmod_me = jnp.concatenate([dshift, dscale, dgate], axis=1)
    dmod_c = jnp.concatenate([dshift_c, dscale_c, jnp.zeros((1, D), F32)], axis=1)
    smalls = [ggf, ggn, gcw, gcb, gcl, gbr, gbi, gbp, gps, jnp.pad(loss_p, ((0, 0), (0, LANE - 1)))]
    sizes = [s.size for s in smalls]
    small_all, dmod_all, dmodc_all = _all_gather([_pack(smalls), dmod_me, dmod_c], "gather_small")
    ga = lax.dynamic_slice(dmod_all.reshape(NDEV, 3 * D), (0, me * na), (NDEV, na))
    gc = lax.dynamic_slice(dmodc_all.reshape(NDEV, 3 * D), (0, me * na), (NDEV, na))
    g_wada, d_wada, nm_wada, nv_wada, pc = _ada_bwd(s_all, ga, gc, w_ada[0], m_w_ada[0], v_w_ada[0], "ada_bwd")
    (pc_all,) = _all_gather([pc[0:1]], "gather_cctx")
    tot, gb_ada = _small_sum(
        jnp.concatenate([small_all.reshape(NDEV, -1), pc_all.reshape(NDEV, D)], axis=1),
        dmod_all.reshape(NDEV, 3 * D), dmodc_all.reshape(NDEV, 3 * D), "small_sum")
    offs = [0]
    for s in sizes + [D]:
        offs.append(offs[-1] + s)
    t_ggf, t_ggn, t_gcw, t_gcb, t_gcl, t_gbr, t_gbi, t_gbp, t_gps, t_loss, t_pc = [
        tot[:, offs[i]:offs[i + 1]] for i in range(len(offs) - 1)]

    def shard(t, rows):
        return lax.dynamic_slice(t.reshape(rows, W), (0, me * ws), (rows, ws))

    def big(wv, parts, mv, vv, name):
        shp = wv.shape
        C = shp[-1]
        if not isinstance(parts, list):
            parts = [parts]
        parts = [p.reshape(NDEV, -1, C // len(parts)) for p in parts]
        outs = _adamw_parts(wv.reshape(-1, C), parts, mv.reshape(-1, C), vv.reshape(-1, C), name)
        return [o.reshape(shp) for o in outs]

    (recv_o,) = _send_wait(sent_o, tot, True, "wait_gwout")
    recv_p, recv_r, recv_i = _send_wait(sent_s, tot, True, "wait_gsmall")
    r_wout = big(w_out, recv_o, m_w_out, v_w_out, "adamw_w_out")
    r_wpool = big(w_pool, recv_p, m_w_pool, v_w_pool, "adamw_w_pool")
    r_wr = big(w_rgate, recv_r, m_w_rgate, v_w_rgate, "adamw_w_rgate")
    r_wi = big(w_igate, recv_i, m_w_igate, v_w_igate, "adamw_w_igate")
    r_wada = [o.reshape(w_ada.shape) for o in (g_wada, d_wada, nm_wada, nv_wada)]

    names = ["c_ctx", "b_ada", "g_norm", "conv_w", "conv_b", "lru_lambda", "b_rgate", "b_igate", "b_pool",
             "pool_scale", "g_final"]
    sw = [c_ctx, b_ada, g_norm, conv_w, conv_b, lru_lambda, b_rgate, b_igate, b_pool, pool_scale, g_final]
    sm = [m_c_ctx, m_b_ada, m_g_norm, m_conv_w, m_conv_b, m_lru_lambda, m_b_rgate, m_b_igate, m_b_pool,
          m_pool_scale, m_g_final]
    sv = [v_c_ctx, v_b_ada, v_g_norm, v_conv_w, v_conv_b, v_lru_lambda, v_b_rgate, v_b_igate, v_b_pool,
          v_pool_scale, v_g_final]
    sg = [t_pc, gb_ada, t_ggn, shard(t_gcw, 4), t_gcb, shard(t_gcl, 2), shard(t_gbr, 2), shard(t_gbi, 2), t_gbp,
          t_gps, t_ggf]
    poffs = [0]
    for wv in sw:
        poffs.append(poffs[-1] + wv.size)
    lam_range = (poffs[5], poffs[6])
    cctx_range = (poffs[0], poffs[1])
    small_out = _adamw_small(_pack(sg), _pack(sw), _pack(sm), _pack(sv), lam_range, cctx_range, "adamw_small")
    (recv_wa,) = _send_wait(sent_ia, small_out[0], True, "wait_gwin_a")
    (recv_wb,) = _send_wait(sent_ib, small_out[0], True, "wait_gwin_b")
    r_win = big(w_in, [recv_wa, recv_wb], m_w_in, v_w_in, "adamw_w_in")
    r_small = {}
    for i, nm in enumerate(names):
        r_small[nm] = [o[:, poffs[i]:poffs[i + 1]].reshape(sw[i].shape) for o in small_out]

    res = dict(r_small)
    res.update(w_ada=r_wada, w_in=r_win, w_rgate=r_wr, w_igate=r_wi, w_pool=r_wpool, w_out=r_wout)
    order = ["c_ctx", "w_ada", "b_ada", "g_norm", "w_in", "conv_w", "conv_b", "lru_lambda", "w_rgate", "b_rgate",
             "w_igate", "b_igate", "w_pool", "b_pool", "pool_scale", "w_out", "g_final"]
    loss = t_loss[0, 0]
    outs = [loss, grad_x.reshape(x.shape)]
    for q in range(4):
        outs += [res[nm][q] for nm in order]
    return tuple(outs)
```

```python
import functools

import jax
import jax.numpy as jnp
from jax import lax
from jax.experimental import pallas as pl
from jax.experimental.pallas import tpu as pltpu

NDEV = 8
GRID_W = 64
POOL_WINDOWS = (2, 4, 8, 16)
LRU_C = 8.0
EPS = 1e-6
ADAM_LR = 0.001
ADAM_B1 = 0.9
ADAM_B2 = 0.999
ADAM_EPS = 1e-08
ADAM_WD = 0.01
ADAM_STEP = 10

F32 = jnp.float32
MXU = jnp.bfloat16

VMEM_BYTES = 64 * 1024 * 1024
VMEM_SLACK = 8 * 1024 * 1024
SUB = 8
SUB16 = 16
LANE = 128

TM = 1152
TN = 1024
TK = 2048
TL = 256
TL_FINAL = 128
CB_SEQ = 256
CB_SCAN = 1024
CB_MIX = 512
TR_CONV = 576
GWIN_PARTS = 4

MESH_ID = pl.DeviceIdType.MESH


def _tile(n, pref, align):
    if n <= pref:
        return n
    for t in range(pref - pref % align, 0, -align):
        if n % t == 0:
            return t
    return n


def _nbytes(shape, dtype):
    n = 1
    for s in shape:
        if s is not None:
            n *= s
    return n * jnp.dtype(dtype).itemsize


def _params(blocks, scratch=(), dims=None):
    need = 2 * sum(_nbytes(s, d) for s, d in blocks) + sum(_nbytes(s, d) for s, d in scratch) + VMEM_SLACK
    kw = dict(vmem_limit_bytes=int(min(max(need, 2 * VMEM_SLACK), VMEM_BYTES - VMEM_SLACK // 2)))
    if dims is not None:
        kw["dimension_semantics"] = dims
    return pltpu.CompilerParams(**kw)


def _sds(shape, dtype):
    return jax.ShapeDtypeStruct(tuple(shape), dtype)


ANY = pl.BlockSpec(memory_space=pl.ANY)


def _ids():
    return lax.axis_index("x"), lax.axis_index("y"), lax.axis_index("c")


def _sigmoid(v):
    return jax.nn.sigmoid(v)


def _neg_expm1(v):
    series = -v * (1.0 + v * (0.5 + v * (1.0 / 6.0 + v * (1.0 / 24.0))))
    return jnp.where(v > -1e-2, series, 1.0 - jnp.exp(v))


def _softplus(v):
    return jnp.maximum(v, 0.0) + jnp.log1p(jnp.exp(-jnp.abs(v)))


def _all_gather(xs, name):
    n = len(xs)

    def body(*refs):
        x_refs, o_refs = refs[:n], refs[n:2 * n]
        send_sems, recv_sems, local_sems = refs[2 * n:]
        x, y, c = _ids()
        me, sibling = (x, y, c), (x, y, 1 - c)
        chips = [(1 - x, y), (x, 1 - y), (1 - x, 1 - y)]

        def slot(a, p):
            return o_refs[a].at[4 * p[0] + 2 * p[1] + p[2]]

        def copy(a, k, block, to, src=None):
            return pltpu.make_async_remote_copy(
                src_ref=slot(a, block) if src is None else src, dst_ref=slot(a, block),
                send_sem=send_sems.at[7 * a + k], recv_sem=recv_sems.at[7 * a + k],
                device_id=to, device_id_type=MESH_ID)

        mine, first, passed = [], [], []
        for a in range(n):
            m = pltpu.make_async_copy(x_refs[a], slot(a, me), local_sems.at[a])
            m.start()
            mine.append(m)
            f = [copy(a, 0, me, sibling, src=x_refs[a])]
            f += [copy(a, 1 + j, me, (*chip, c), src=x_refs[a]) for j, chip in enumerate(chips)]
            for cp in f:
                cp.start()
            first += f
        for a in range(n):
            for j, chip in enumerate(chips):
                copy(a, 1 + j, (*chip, c), me).wait_recv()
                p = copy(a, 4 + j, (*chip, c), sibling)
                p.start()
                passed.append(p)
        for a in range(n):
            copy(a, 0, sibling, me).wait_recv()
            for j, chip in enumerate(chips):
                copy(a, 4 + j, (*chip, 1 - c), me).wait_recv()
        for cp in first + passed:
            cp.wait_send()
        for m in mine:
            m.wait()

    return pl.pallas_call(
        body, name=name,
        out_shape=[_sds((NDEV,) + v.shape, v.dtype) for v in xs],
        in_specs=[ANY] * n, out_specs=[ANY] * n,
        scratch_shapes=[pltpu.SemaphoreType.DMA((7 * n,)), pltpu.SemaphoreType.DMA((7 * n,)),
                        pltpu.SemaphoreType.DMA((n,))],
    )(*xs)


HBM = pl.BlockSpec(memory_space=pltpu.HBM)
SEM = pl.BlockSpec(memory_space=pltpu.SEMAPHORE)
EFFECT = pltpu.SideEffectType.DATAFLOW_SIDE_EFFECTING


def _peers():
    x, y, c = _ids()
    out = []
    for k in range(1, NDEV):
        px = 1 - x if k & 4 else x
        py = 1 - y if k & 2 else y
        pc = 1 - c if k & 1 else c
        out.append(((px, py, pc), 4 * px + 2 * py + pc))
    return out, 4 * x + 2 * y + c


def _tie(v, deps, name):
    def body(v_ref, *rest):
        rest[-1][...] = v_ref[...]

    vmem = pl.BlockSpec(memory_space=pltpu.VMEM)
    return pl.pallas_call(
        body, name=name, out_shape=_sds(v.shape, v.dtype), in_specs=[vmem] + [ANY] * len(deps), out_specs=vmem,
    )(v, *deps)


def _place(srcs, from_slot, name, deps=()):
    n = len(srcs)
    blks = [v.shape[1:] if from_slot else v.shape for v in srcs]

    nd = len(deps)

    def body(*refs):
        s_refs, l_refs = refs[:n], refs[n + nd:2 * n + nd]
        bufs, sems = refs[2 * n + nd:3 * n + nd], refs[3 * n + nd]
        x, y, c = _ids()
        me = 4 * x + 2 * y + c
        ins = [pltpu.make_async_copy(s_refs[a].at[me] if from_slot else s_refs[a], bufs[a], sems.at[a])
               for a in range(n)]
        outs = [pltpu.make_async_copy(bufs[a], l_refs[a].at[me], sems.at[n + a]) for a in range(n)]
        for cp in ins:
            cp.start()
        for a in range(n):
            ins[a].wait()
            outs[a].start()
        for cp in outs:
            cp.wait()

    scratch = [(b, v.dtype) for b, v in zip(blks, srcs)]
    return pl.pallas_call(
        body, name=name, out_shape=[_sds((NDEV,) + b, v.dtype) for b, v in zip(blks, srcs)],
        in_specs=[ANY] * (n + nd), out_specs=[ANY] * n,
        scratch_shapes=[pltpu.VMEM(b, d) for b, d in scratch] + [pltpu.SemaphoreType.DMA((2 * n,))],
        compiler_params=_params([], scratch),
    )(*srcs, *deps)


def _send_copies(s_refs, l_refs, ssem, rsem, from_slot, receiving):
    peers, me = _peers()
    out = []
    for a in range(len(s_refs)):
        for k, (dev, idx) in enumerate(peers):
            out.append(pltpu.make_async_remote_copy(
                src_ref=s_refs[a].at[idx] if from_slot else s_refs[a],
                dst_ref=l_refs[a].at[idx if receiving else me],
                send_sem=ssem.at[7 * a + k], recv_sem=rsem.at[7 * a + k], device_id=dev, device_id_type=MESH_ID))
    return out


def _send_start(srcs, lands, from_slot, name):
    n = len(srcs)

    def body(*refs):
        s_refs, l_refs = refs[:n], refs[n:2 * n]
        ssem, rsem = refs[2 * n], refs[2 * n + 1]
        token = refs[-1]
        for send in _send_copies(s_refs, l_refs, ssem, rsem, from_slot, False):
            send.start()
        token[...] = jnp.zeros_like(token)

    bufs = list(srcs) + list(lands)
    outs = pl.pallas_call(
        body, name=name,
        out_shape=[pltpu.SemaphoreType.DMA((7 * n,)), pltpu.SemaphoreType.DMA((7 * n,))]
        + [pltpu.HBM(v.shape, v.dtype) for v in bufs] + [_sds((SUB, LANE), F32)],
        in_specs=[HBM] * (2 * n), out_specs=[SEM, SEM] + [HBM] * (2 * n) + [pl.BlockSpec(memory_space=pltpu.VMEM)],
        input_output_aliases={i: 2 + i for i in range(2 * n)},
        compiler_params=pltpu.CompilerParams(has_side_effects=EFFECT),
    )(*[pltpu.with_memory_space_constraint(v, pltpu.HBM) for v in bufs])
    return outs[0], outs[1], list(outs[2:2 + n]), list(outs[2 + n:2 + 2 * n]), outs[-1]


def _send_wait(started, after, from_slot, name):
    ssem, rsem, srcs, lands, _ = started
    n = len(srcs)

    def body(*refs):
        s_refs, l_refs = refs[:n], refs[n:2 * n]
        ssem_ref, rsem_ref = refs[2 * n], refs[2 * n + 1]
        for recv in _send_copies(s_refs, l_refs, ssem_ref, rsem_ref, from_slot, True):
            recv.wait_send()
            recv.wait_recv()

    bufs = list(srcs) + list(lands)
    outs = pl.pallas_call(
        body, name=name, out_shape=[pltpu.HBM(v.shape, v.dtype) for v in bufs],
        in_specs=[HBM] * (2 * n) + [SEM, SEM, ANY], out_specs=[HBM] * (2 * n),
        input_output_aliases={i: i for i in range(2 * n)},
        compiler_params=pltpu.CompilerParams(has_side_effects=EFFECT),
    )(*bufs, ssem, rsem, after)
    return list(outs[n:])


NN = (((1,), (0,)), ((), ()))
NT = (((1,), (1,)), ((), ()))
TN_DIMS = (((0,), (0,)), ((), ()))


def _mm(a, b, *, grid, a_spec, b_spec, o_spec, out_shape, acc_shape, dims, name, dep=None):
    k_axis = len(grid) - 1
    nk = grid[k_axis]
    extra = [] if dep is None else [dep]

    def body(a_ref, b_ref, *rest):
        o_ref, acc_ref = rest[-2], rest[-1]
        k = pl.program_id(k_axis)

        def prod():
            return lax.dot_general(a_ref[...], b_ref[...], dims, preferred_element_type=F32)

        if nk == 1:
            o_ref[...] = prod().astype(o_ref.dtype)
            return

        @pl.when(k == 0)
        def _():
            acc_ref[...] = prod()

        if nk > 2:
            @pl.when((k > 0) & (k < nk - 1))
            def _():
                acc_ref[...] += prod()

        @pl.when(k == nk - 1)
        def _():
            o_ref[...] = (acc_ref[...] + prod()).astype(o_ref.dtype)

    blocks = [(a_spec.block_shape, a.dtype), (b_spec.block_shape, b.dtype), (o_spec.block_shape, out_shape.dtype)]
    return pl.pallas_call(
        body, name=name, grid=grid, in_specs=[a_spec, b_spec] + [ANY] * len(extra), out_specs=o_spec,
        out_shape=out_shape, scratch_shapes=[pltpu.VMEM(acc_shape, F32)],
        compiler_params=_params(blocks, [(acc_shape, F32)], ("parallel",) * k_axis + ("arbitrary",)),
    )(a, b, *extra)


def _mm_plain(a, b, dims, out_dtype, name):
    if dims == TN_DIMS:
        (K, M), N = a.shape, b.shape[1]
    elif dims == NT:
        (M, K), N = a.shape, b.shape[0]
    else:
        (M, K), N = a.shape, b.shape[1]
    tm, tn = _tile(M, TM, LANE), _tile(N, TN, LANE)
    tk = _tile(K, TK, LANE if dims != TN_DIMS else SUB16)
    if dims == TN_DIMS:
        a_spec = pl.BlockSpec((tk, tm), lambda i, j, k: (k, i))
    else:
        a_spec = pl.BlockSpec((tm, tk), lambda i, j, k: (i, k))
    if dims == NT:
        b_spec = pl.BlockSpec((tn, tk), lambda i, j, k: (j, k))
    else:
        b_spec = pl.BlockSpec((tk, tn), lambda i, j, k: (k, j))
    return _mm(a, b, grid=(M // tm, N // tn, K // tk), a_spec=a_spec, b_spec=b_spec,
               o_spec=pl.BlockSpec((tm, tn), lambda i, j, k: (i, j)),
               out_shape=_sds((M, N), out_dtype), acc_shape=(tm, tn), dims=dims, name=name)


def _mm_proj(h_all, win_all, name):
    n, D = h_all.shape
    nb = win_all.shape[2]
    tm, tn, tk = _tile(n, TM, SUB16), _tile(nb, TN, LANE), _tile(D, TK, LANE)
    nbn = nb // tn
    return _mm(h_all, win_all, grid=(n // tm, NDEV * nbn, D // tk),
               a_spec=pl.BlockSpec((tm, tk), lambda i, j, k: (i, k)),
               b_spec=pl.BlockSpec((None, tk, tn), lambda i, j, k: (j // nbn, k, j % nbn)),
               o_spec=pl.BlockSpec((tm, tn), lambda i, j, k: (i, j)),
               out_shape=_sds((n, NDEV * nb), F32), acc_shape=(tm, tn), dims=NN, name=name)


def _mm_dh(dproj, win_all, name, dep=None):
    n = dproj.shape[0]
    _, D, nb = win_all.shape
    tm, tn, tk = _tile(n, TM, SUB16), _tile(D, TN, LANE), _tile(nb, TK, LANE)
    nbk = nb // tk
    return _mm(dproj, win_all, grid=(n // tm, D // tn, NDEV * nbk),
               a_spec=pl.BlockSpec((tm, tk), lambda i, j, k: (i, k)),
               b_spec=pl.BlockSpec((None, tn, tk), lambda i, j, k: (k // nbk, j, k % nbk)),
               o_spec=pl.BlockSpec((tm, tn), lambda i, j, k: (i, j)),
               out_shape=_sds((n, D), F32), acc_shape=(tm, tn), dims=NT, name=name, dep=dep)


def _mm_gwin(h_all, dproj, nb, part, nparts, name, dep=None):
    n, D = h_all.shape
    nbp = nb // nparts
    tm, tn, tk = _tile(D, TM, LANE), _tile(nbp, TN, LANE), _tile(n, TK, SUB16)
    nbn = nbp // tn
    return _mm(h_all, dproj, grid=(D // tm, NDEV * nbn, n // tk),
               a_spec=pl.BlockSpec((tk, tm), lambda i, j, k: (k, i)),
               b_spec=pl.BlockSpec((tk, tn), lambda i, j, k: (k, (j // nbn) * (nb // tn) + part * nbn + j % nbn)),
               o_spec=pl.BlockSpec((None, tm, tn), lambda i, j, k: (j // nbn, i, j % nbn)),
               out_shape=_sds((NDEV, D, nbp), MXU), acc_shape=(tm, tn), dims=TN_DIMS, name=name, dep=dep)


def _mm_group(a, b, mode, out_dtype, name):
    if mode == "wgrad":
        L, W = a.shape
        G = len(POOL_WINDOWS)
        pd = W // G
        tm, tn, tk = _tile(pd, TM, LANE), _tile(pd, TN, LANE), _tile(L, TK, SUB16)
        nm, nn = pd // tm, pd // tn
        return _mm(a, b, grid=(G, nm, nn, L // tk),
                   a_spec=pl.BlockSpec((tk, tm), lambda g, i, j, k: (k, g * nm + i)),
                   b_spec=pl.BlockSpec((tk, tn), lambda g, i, j, k: (k, g * nn + j)),
                   o_spec=pl.BlockSpec((None, tm, tn), lambda g, i, j, k: (g, i, j)),
                   out_shape=_sds((G, pd, pd), out_dtype), acc_shape=(tm, tn), dims=TN_DIMS, name=name)
    L, W = a.shape
    G, pd, _ = b.shape
    tm, tn, tk = _tile(L, TM, SUB16), _tile(pd, TN, LANE), _tile(pd, TK, LANE)
    nn, nk = pd // tn, pd // tk
    if mode == "fwd":
        b_spec = pl.BlockSpec((None, tk, tn), lambda g, i, j, k: (g, k, j))
        dims = NN
    else:
        b_spec = pl.BlockSpec((None, tn, tk), lambda g, i, j, k: (g, j, k))
        dims = NT
    return _mm(a, b, grid=(G, L // tm, nn, nk),
               a_spec=pl.BlockSpec((tm, tk), lambda g, i, j, k: (i, g * nk + k)),
               b_spec=b_spec,
               o_spec=pl.BlockSpec((tm, tn), lambda g, i, j, k: (i, g * nn + j)),
               out_shape=_sds((L, W), out_dtype), acc_shape=(tm, tn), dims=dims, name=name)


def _ada_fwd(cc, w_loc, b_loc, name):
    R, D = cc.shape
    na = w_loc.shape[1]
    tk = _tile(D, 512, LANE)

    def body(c_ref, w_ref, b_ref, mod_ref, s_ref):
        k = pl.program_id(0)
        cv = c_ref[...]
        s = cv * _sigmoid(cv)
        s_ref[...] = s

        @pl.when(k == 0)
        def _():
            mod_ref[...] = jnp.broadcast_to(b_ref[...], mod_ref.shape)

        mod_ref[...] += lax.dot_general(s.astype(MXU), w_ref[...].astype(MXU), NN, preferred_element_type=F32)

    blocks = [((R, tk), F32), ((tk, na), F32), ((1, na), F32), ((R, na), F32), ((R, tk), F32)]
    return pl.pallas_call(
        body, name=name, grid=(D // tk,),
        in_specs=[pl.BlockSpec((R, tk), lambda k: (0, k)), pl.BlockSpec((tk, na), lambda k: (k, 0)),
                  pl.BlockSpec((1, na), lambda k: (0, 0))],
        out_specs=[pl.BlockSpec((R, na), lambda k: (0, 0)), pl.BlockSpec((R, tk), lambda k: (0, k))],
        out_shape=[_sds((R, na), F32), _sds((R, D), F32)],
        compiler_params=_params(blocks, dims=("arbitrary",)),
    )(cc, w_loc, b_loc)


def _adam(w, g, m, v):
    m = ADAM_B1 * m + (1.0 - ADAM_B1) * g
    v = ADAM_B2 * v + (1.0 - ADAM_B2) * (g * g)
    m_hat = m / (1.0 - ADAM_B1 ** ADAM_STEP)
    v_hat = v / (1.0 - ADAM_B2 ** ADAM_STEP)
    delta = -ADAM_LR * (m_hat / (jnp.sqrt(v_hat) + ADAM_EPS) + ADAM_WD * w)
    return delta, m, v


def _ada_bwd(s_all, ga, gc, w_loc, m_loc, v_loc, name):
    D, na = w_loc.shape
    tr = _tile(D, 256, LANE)

    def body(s_ref, ga_ref, gc_ref, w_ref, m_ref, v_ref, g_ref, d_ref, nm_ref, nv_ref, pc_ref):
        dmc = gc_ref[0:1, :]
        for p in range(1, NDEV):
            dmc = dmc + gc_ref[p:p + 1, :]
        rows = lax.broadcasted_iota(jnp.int32, (NDEV, na), 0)
        dmc8 = jnp.where(rows == 0, jnp.broadcast_to(dmc, (NDEV, na)), 0.0)
        dm = jnp.concatenate([ga_ref[...], dmc8], axis=0).astype(MXU)
        dmc16 = jnp.concatenate([dmc8, jnp.zeros_like(dmc8)], axis=0).astype(MXU)
        w = w_ref[...]
        g = lax.dot_general(s_ref[...].astype(MXU), dm, TN_DIMS, preferred_element_type=F32)
        pc_ref[...] = lax.dot_general(dmc16, w.astype(MXU), NT, preferred_element_type=F32)
        delta, nm, nv = _adam(w, g, m_ref[...], v_ref[...])
        g_ref[...] = g
        d_ref[...] = delta
        nm_ref[...] = nm
        nv_ref[...] = nv

    big = pl.BlockSpec((tr, na), lambda i: (i, 0))
    full = pl.BlockSpec((NDEV, na), lambda i: (0, 0))
    srow = pl.BlockSpec((2 * NDEV, tr), lambda i: (0, i))
    blocks = [((2 * NDEV, tr), F32)] * 2 + [((NDEV, na), F32)] * 2 + [((tr, na), F32)] * 7
    return pl.pallas_call(
        body, name=name, grid=(D // tr,),
        in_specs=[srow, full, full, big, big, big],
        out_specs=[big, big, big, big, srow],
        out_shape=[_sds((D, na), F32)] * 4 + [_sds((2 * NDEV, D), F32)],
        compiler_params=_params(blocks, dims=("parallel",)),
    )(s_all, ga, gc, w_loc, m_loc, v_loc)


def _norm_mod(x2, g, shift, scale, n, row0, h_prev, name):
    R, D = x2.shape
    tl = _tile(R, TL, SUB16)
    assert row0 % tl == 0
    b0 = row0 // tl

    def body(x_ref, g_ref, sh_ref, sc_ref, *rest):
        o_ref = rest[-1]
        xv = x_ref[...]
        s = lax.rsqrt(jnp.mean(xv * xv, axis=-1, keepdims=True) + EPS)
        nrm = xv * s * g_ref[...]
        o_ref[...] = (nrm * (1.0 + sc_ref[...]) + sh_ref[...]).astype(o_ref.dtype)

    vec = pl.BlockSpec((1, D), lambda i: (0, 0))
    in_specs = [pl.BlockSpec((tl, D), lambda i: (i, 0)), vec, vec, vec]
    args = [x2, g, shift, scale]
    aliases = {}
    if h_prev is not None:
        in_specs.append(ANY)
        args.append(h_prev)
        aliases = {4: 0}
    blocks = [((tl, D), F32), ((tl, D), MXU)] + [((1, D), F32)] * 3
    return pl.pallas_call(
        body, name=name, grid=(R // tl,), in_specs=in_specs,
        out_specs=pl.BlockSpec((tl, D), lambda i: (i + b0, 0)),
        out_shape=_sds((n, D), MXU), input_output_aliases=aliases,
        compiler_params=_params(blocks, dims=("parallel",)),
    )(*args)


def _norm_bwd(x2, dh_all, row0, g, scale, dxn, ggn0, name):
    R, D = x2.shape
    tl = _tile(R, TL_FINAL, SUB)
    assert row0 % tl == 0
    b0 = row0 // tl
    with_x = dxn is not None

    def body(*refs):
        if with_x:
            x_ref, dh_ref, g_ref, sc_ref, gg0_ref, dxn_ref, gx_ref, dsh_ref, dsc_ref, gg_ref = refs
        else:
            x_ref, dh_ref, g_ref, sc_ref, gg0_ref, dsh_ref, dsc_ref, gg_ref = refs
        i = pl.program_id(0)

        @pl.when(i == 0)
        def _():
            dsh_ref[...] = jnp.zeros_like(dsh_ref)
            dsc_ref[...] = jnp.zeros_like(dsc_ref)
            gg_ref[...] = gg0_ref[...]

        xv = x_ref[...]
        dh = dh_ref[...]
        gv = g_ref[...]
        s = lax.rsqrt(jnp.mean(xv * xv, axis=-1, keepdims=True) + EPS)
        xh = xv * s
        dsh_ref[...] += jnp.sum(dh, axis=0, keepdims=True)
        dsc_ref[...] += jnp.sum(dh * (xh * gv), axis=0, keepdims=True)
        dn = dh * (1.0 + sc_ref[...])
        gg_ref[...] += jnp.sum(dn * xh, axis=0, keepdims=True)
        if with_x:
            dxh = dn * gv
            dx = s * (dxh - xh * jnp.mean(dxh * xh, axis=-1, keepdims=True))
            gx_ref[...] = dx + dxn_ref[...]

    vec = pl.BlockSpec((1, D), lambda i: (0, 0))
    row = pl.BlockSpec((tl, D), lambda i: (i, 0))
    in_specs = [row, pl.BlockSpec((tl, D), lambda i: (i + b0, 0)), vec, vec, vec]
    args = [x2, dh_all, g, scale, ggn0]
    out_specs = [vec, vec, vec]
    out_shape = [_sds((1, D), F32)] * 3
    if with_x:
        in_specs.append(row)
        args.append(dxn)
        out_specs = [row] + out_specs
        out_shape = [_sds((R, D), F32)] + out_shape
    blocks = [((tl, D), F32)] * (4 if with_x else 2) + [((1, D), F32)] * 6
    outs = pl.pallas_call(
        body, name=name, grid=(R // tl,), in_specs=in_specs, out_specs=out_specs, out_shape=out_shape,
        compiler_params=_params(blocks, dims=("arbitrary",)),
    )(*args)
    return tuple(outs) if with_x else (None,) + tuple(outs)


def _tap_valid(t, o, lc, n):
    tt = t + o
    in_ctx = t < lc
    return (tt >= jnp.where(in_ctx, 0, lc)) & (tt < jnp.where(in_ctx, lc, n))


def _conv_fwd(proj_all, cw, cb, lc, W, name):
    n = proj_all.shape[0]
    cbk = _tile(W, CB_SEQ, LANE)
    tr = _tile(n, TR_CONV, SUB16)
    ext = tr + 2 * SUB

    def body(x_ref, w_ref, b_ref, u_ref, xp_ref):
        xp_ref[0:SUB, :] = jnp.zeros((SUB, cbk), F32)
        xp_ref[n + SUB:n + 2 * SUB, :] = jnp.zeros((SUB, cbk), F32)
        xp_ref[SUB:n + SUB, :] = x_ref[...]
        w = w_ref[...]
        bias = b_ref[...]

        def chunk(ci, carry):
            r0 = pl.multiple_of(ci * tr, SUB16)
            xe = xp_ref[pl.ds(r0, ext), :]
            t = r0 + lax.broadcasted_iota(jnp.int32, (tr, cbk), 0)
            acc = jnp.broadcast_to(bias, (tr, cbk))
            for k in range(4):
                o = k - 1
                sh = xe if o == 0 else pltpu.roll(xe, (-o) % ext, 0)
                acc = acc + jnp.where(_tap_valid(t, o, lc, n), sh[SUB:tr + SUB], 0.0) * w[k:k + 1]
            u_ref[pl.ds(r0, tr), :] = acc
            return carry

        lax.fori_loop(0, n // tr, chunk, 0)

    blocks = [((n, cbk), F32)] * 2 + [((4, cbk), F32), ((1, cbk), F32)]
    scratch = [((n + 2 * SUB, cbk), F32)]
    return pl.pallas_call(
        body, name=name, grid=(W // cbk,),
        in_specs=[pl.BlockSpec((n, cbk), lambda j: (0, j)), pl.BlockSpec((4, cbk), lambda j: (0, j)),
                  pl.BlockSpec((1, cbk), lambda j: (0, j))],
        out_specs=pl.BlockSpec((n, cbk), lambda j: (0, j)),
        out_shape=_sds((n, W), F32),
        scratch_shapes=[pltpu.VMEM(s, d) for s, d in scratch],
        compiler_params=_params(blocks, scratch, ("parallel",)),
    )(proj_all, cw, cb)


def _conv_bwd(du_all, proj_all, cw, dproj, lc, W, name):
    n = du_all.shape[0]
    cbk = _tile(W, CB_SEQ, LANE)
    tr = _tile(n, TR_CONV, SUB16)
    ext = tr + 2 * SUB

    def body(du_ref, x_ref, w_ref, dp_in, dx_ref, gw_ref, gb_ref, dp_ref, xp_ref):
        del dp_in
        for ref, src in ((dp_ref, du_ref), (xp_ref, x_ref)):
            ref[0:SUB, :] = jnp.zeros((SUB, cbk), F32)
            ref[n + SUB:n + 2 * SUB, :] = jnp.zeros((SUB, cbk), F32)
            ref[SUB:n + SUB, :] = src[...]
        w = w_ref[...]

        def fold(v):
            return jnp.sum(v.reshape(tr // SUB, SUB, cbk), axis=0)

        def chunk(ci, carry):
            r0 = pl.multiple_of(ci * tr, SUB16)
            de = dp_ref[pl.ds(r0, ext), :]
            xe = xp_ref[pl.ds(r0, ext), :]
            t = r0 + lax.broadcasted_iota(jnp.int32, (tr, cbk), 0)
            d0 = de[SUB:tr + SUB]
            dx = jnp.zeros((tr, cbk), F32)
            new = []
            for k in range(4):
                o = k - 1
                dsh = de if o == 0 else pltpu.roll(de, o % ext, 0)
                dx = dx + jnp.where(_tap_valid(t, -o, lc, n), dsh[SUB:tr + SUB], 0.0) * w[k:k + 1]
                xsh = xe if o == 0 else pltpu.roll(xe, (-o) % ext, 0)
                new.append(carry[k] + fold(d0 * jnp.where(_tap_valid(t, o, lc, n), xsh[SUB:tr + SUB], 0.0)))
            new.append(carry[4] + fold(d0))
            dx_ref[pl.ds(r0, tr), :] = dx.astype(dx_ref.dtype)
            return tuple(new)

        zero = jnp.zeros((SUB, cbk), F32)
        acc = lax.fori_loop(0, n // tr, chunk, (zero,) * 5)
        for k in range(4):
            gw_ref[k:k + 1, :] = jnp.sum(acc[k], axis=0, keepdims=True)
        gb_ref[...] = jnp.sum(acc[4], axis=0, keepdims=True)

    col = pl.BlockSpec((n, cbk), lambda j: (0, j))
    blocks = [((n, cbk), F32)] * 2 + [((n, cbk), MXU), ((4, cbk), F32), ((4, cbk), F32), ((1, cbk), F32)]
    scratch = [((n + 2 * SUB, cbk), F32)] * 2
    return pl.pallas_call(
        body, name=name, grid=(W // cbk,),
        in_specs=[col, col, pl.BlockSpec((4, cbk), lambda j: (0, j)), ANY],
        out_specs=[col, pl.BlockSpec((4, cbk), lambda j: (0, j)), pl.BlockSpec((1, cbk), lambda j: (0, j))],
        out_shape=[_sds(dproj.shape, dproj.dtype), _sds((4, W), F32), _sds((1, W), F32)],
        input_output_aliases={3: 0},
        scratch_shapes=[pltpu.VMEM(s, d) for s, d in scratch],
        compiler_params=_params(blocks, scratch, ("parallel",)),
    )(du_all, proj_all, cw, dproj)


def _gate_coeffs(ub, u, d, wr_ref, wi_ref, br_ref, bi_ref, lam_ref):
    c = -LRU_C * _softplus(-lam_ref[d:d + 1, :])
    r = _sigmoid(lax.dot_general(ub, wr_ref[d], NN, preferred_element_type=F32) + br_ref[d:d + 1, :])
    ig = _sigmoid(lax.dot_general(ub, wi_ref[d], NN, preferred_element_type=F32) + bi_ref[d:d + 1, :])
    la = c * r
    a = jnp.exp(la)
    sq = jnp.sqrt(_neg_expm1(2.0 * la))
    return c, r, ig, a, sq


def _gate_specs(tl, hd):
    w_spec = pl.BlockSpec((2, None, hd, hd), lambda h, i: (0, h, 0, 0))
    v_spec = pl.BlockSpec((2, hd), lambda h, i: (0, h))
    return w_spec, v_spec


def _gates_fwd(u_all, wr, wi, br, bi, lam, name):
    n, W = u_all.shape
    heads, hd = wr.shape[1], wr.shape[2]
    tl = _tile(n, TL, SUB16)

    def body(u_ref, wr_ref, wi_ref, br_ref, bi_ref, lam_ref, a_ref, b_ref):
        u = u_ref[...]
        ub = u.astype(MXU)
        for d in range(2):
            _, _, ig, a, sq = _gate_coeffs(ub, u, d, wr_ref, wi_ref, br_ref, bi_ref, lam_ref)
            a_ref[d] = a
            b_ref[d] = sq * (ig * u)

    w_spec, v_spec = _gate_specs(tl, hd)
    o_spec = pl.BlockSpec((2, tl, hd), lambda h, i: (0, i, h))
    blocks = [((tl, hd), F32), ((2, hd, hd), MXU), ((2, hd, hd), MXU)] + [((2, hd), F32)] * 3 + [((2, tl, hd), F32)] * 2
    return pl.pallas_call(
        body, name=name, grid=(heads, n // tl),
        in_specs=[pl.BlockSpec((tl, hd), lambda h, i: (i, h)), w_spec, w_spec, v_spec, v_spec, v_spec],
        out_specs=[o_spec, o_spec], out_shape=[_sds((2, n, W), F32)] * 2,
        compiler_params=_params(blocks, dims=("parallel", "parallel")),
    )(u_all, wr, wi, br, bi, lam)


def _gates_bwd(u_all, da, db, wr, wi, br, bi, lam, name):
    n, W = u_all.shape
    heads, hd = wr.shape[1], wr.shape[2]
    tl = _tile(n, TL, SUB16)
    ni = n // tl

    def body(u_ref, da_ref, db_ref, wr_ref, wi_ref, br_ref, bi_ref, lam_ref,
             du_ref, gwr_ref, gwi_ref, gbr_ref, gbi_ref, gc_ref, accr_ref, acci_ref):
        i = pl.program_id(1)

        @pl.when(i == 0)
        def _():
            accr_ref[...] = jnp.zeros_like(accr_ref)
            acci_ref[...] = jnp.zeros_like(acci_ref)
            gbr_ref[...] = jnp.zeros_like(gbr_ref)
            gbi_ref[...] = jnp.zeros_like(gbi_ref)
            gc_ref[...] = jnp.zeros_like(gc_ref)

        u = u_ref[...]
        ub = u.astype(MXU)
        du = jnp.zeros_like(u)
        for d in range(2):
            c, r, ig, a, sq = _gate_coeffs(ub, u, d, wr_ref, wi_ref, br_ref, bi_ref, lam_ref)
            dbv = db_ref[d]
            t = dbv * sq
            du = du + t * ig
            d_la = da_ref[d] * a - (dbv * ig * u) * (a * a) / sq
            gc_ref[d:d + 1, :] += jnp.sum(d_la * r, axis=0, keepdims=True)
            d_pr = (d_la * c) * (r * (1.0 - r))
            d_pi = (t * u) * (ig * (1.0 - ig))
            gbr_ref[d:d + 1, :] += jnp.sum(d_pr, axis=0, keepdims=True)
            gbi_ref[d:d + 1, :] += jnp.sum(d_pi, axis=0, keepdims=True)
            pb = d_pr.astype(MXU)
            qb = d_pi.astype(MXU)
            du = du + lax.dot_general(pb, wr_ref[d], NT, preferred_element_type=F32)
            du = du + lax.dot_general(qb, wi_ref[d], NT, preferred_element_type=F32)
            accr_ref[d] += lax.dot_general(ub, pb, TN_DIMS, preferred_element_type=F32)
            acci_ref[d] += lax.dot_general(ub, qb, TN_DIMS, preferred_element_type=F32)
        du_ref[...] = du

        @pl.when(i == ni - 1)
        def _():
            gwr_ref[...] = accr_ref[...].astype(gwr_ref.dtype)
            gwi_ref[...] = acci_ref[...].astype(gwi_ref.dtype)

    w_spec, v_spec = _gate_specs(tl, hd)
    u_spec = pl.BlockSpec((tl, hd), lambda h, i: (i, h))
    ab_spec = pl.BlockSpec((2, tl, hd), lambda h, i: (0, i, h))
    blocks = ([((tl, hd), F32)] * 2 + [((2, tl, hd), F32)] * 2 + [((2, hd, hd), MXU)] * 4 + [((2, hd), F32)] * 6)
    scratch = [((2, hd, hd), F32)] * 2
    return pl.pallas_call(
        body, name=name, grid=(heads, ni),
        in_specs=[u_spec, ab_spec, ab_spec, w_spec, w_spec, v_spec, v_spec, v_spec],
        out_specs=[u_spec, w_spec, w_spec, v_spec, v_spec, v_spec],
        out_shape=[_sds((n, W), F32), _sds(wr.shape, MXU), _sds(wi.shape, MXU)] + [_sds((2, W), F32)] * 3,
        scratch_shapes=[pltpu.VMEM(s, d) for s, d in scratch],
        compiler_params=_params(blocks, scratch, ("parallel", "arbitrary")),
    )(u_all, da, db, wr, wi, br, bi, lam)


def _tile_scan(A, B, rows, reverse):
    for s in (1, 2, 4):
        if reverse:
            As, Bs, m = pltpu.roll(A, SUB - s, 0), pltpu.roll(B, SUB - s, 0), rows < SUB - s
        else:
            As, Bs, m = pltpu.roll(A, s, 0), pltpu.roll(B, s, 0), rows >= s
        B = jnp.where(m, A * Bs + B, B)
        A = jnp.where(m, A * As, A)
    return A, B


def _scan_chunks(n, lc):
    tc = _tile(lc, TL, SUB)
    assert n % tc == 0 and lc % tc == 0
    return tc, n // tc, lc // tc


def _scan_fwd(a_all, b_all, lc, name):
    _, n, W = a_all.shape
    cb = _tile(W, CB_SCAN, LANE)
    tc, nch, ncc = _scan_chunks(n, lc)
    ntile = tc // SUB

    def chunk(d, t):
        return jnp.where(d == 0, t, jnp.where(t < ncc, ncc - 1 - t, nch - 1 - (t - ncc)))

    def body(a_ref, b_ref, h_ref, carry_ref):
        rows = lax.broadcasted_iota(jnp.int32, (SUB, cb), 0)

        @pl.when(pl.program_id(2) == 0)
        def _():
            carry_ref[...] = jnp.zeros_like(carry_ref)

        def run(reverse):
            def step(i, h):
                r = pl.multiple_of(((ntile - 1 - i) if reverse else i) * SUB, SUB)
                A, B = _tile_scan(a_ref[pl.ds(r, SUB), :], b_ref[pl.ds(r, SUB), :], rows, reverse)
                H = A * h + B
                h_ref[pl.ds(r, SUB), :] = H
                return H[0:1, :] if reverse else H[SUB - 1:SUB, :]

            carry_ref[...] = lax.fori_loop(0, ntile, step, carry_ref[...], unroll=2)

        @pl.when(pl.program_id(1) == 0)
        def _():
            run(False)

        @pl.when(pl.program_id(1) == 1)
        def _():
            run(True)

    spec = pl.BlockSpec((None, tc, cb), lambda j, d, t: (d, chunk(d, t), j))
    return pl.pallas_call(
        body, name=name, grid=(W // cb, 2, nch), in_specs=[spec, spec], out_specs=spec,
        out_shape=_sds((2, n, W), F32), scratch_shapes=[pltpu.VMEM((1, cb), F32)],
        compiler_params=_params([((tc, cb), F32)] * 3, [((1, cb), F32)], ("parallel", "arbitrary", "arbitrary")),
    )(a_all, b_all)


def _scan_bwd(a_all, h_all, dya, lc, name):
    _, n, W = a_all.shape
    cb = _tile(W, CB_SCAN, LANE)
    tc, nch, ncc = _scan_chunks(n, lc)
    ntile = tc // SUB
    nl = nch - ncc

    def chunk(d, t):
        return jnp.where(d == 0, nch - 1 - t, jnp.where(t < nl, ncc + t, t - nl))

    def neighbour(d, t):
        c = chunk(d, t)
        below = jnp.maximum(c * ntile - 1, 0)
        above = jnp.where(c == nch - 1, 0, jnp.minimum((c + 1) * ntile, nch * ntile - 1))
        return jnp.where(d == 0, below, above)

    def body(a_ref, h_ref, hn_ref, g_ref, da_ref, db_ref, mu_ref):
        rows = lax.broadcasted_iota(jnp.int32, (SUB, cb), 0)
        d, t = pl.program_id(1), pl.program_id(2)
        c = chunk(d, t)
        has_g = c >= ncc

        @pl.when(t == 0)
        def _():
            mu_ref[...] = jnp.zeros_like(mu_ref)

        def tile(ref, j):
            return ref[pl.ds(pl.multiple_of(j * SUB, SUB), SUB), :]

        def run(up):
            if up:
                edge = jnp.where(c == ncc - 1, 0.0, hn_ref[0:1, :])
            else:
                edge = jnp.where(c > 0, hn_ref[SUB - 1:SUB, :], 0.0)

            def step(i, mu):
                j = i if up else ntile - 1 - i
                a_t = tile(a_ref, j)
                g_t = jnp.where(has_g, tile(g_ref, j), 0.0)
                if up:
                    ap = jnp.where(rows >= 1, pltpu.roll(a_t, 1, 0), 1.0)
                    nb_row = jnp.where(j < ntile - 1, tile(h_ref, jnp.minimum(j + 1, ntile - 1))[0:1, :], edge)
                    hprev = jnp.where(rows < SUB - 1, pltpu.roll(tile(h_ref, j), SUB - 1, 0), nb_row)
                else:
                    ap = jnp.where(rows < SUB - 1, pltpu.roll(a_t, SUB - 1, 0), 1.0)
                    nb_row = jnp.where(j > 0, tile(h_ref, jnp.maximum(j - 1, 0))[SUB - 1:SUB, :], edge)
                    hprev = jnp.where(rows >= 1, pltpu.roll(tile(h_ref, j), 1, 0), nb_row)
                A, B = _tile_scan(ap, g_t, rows, not up)
                lam = A * mu + B
                r = pl.multiple_of(j * SUB, SUB)
                da_ref[pl.ds(r, SUB), :] = lam * hprev
                db_ref[pl.ds(r, SUB), :] = lam
                return a_t[SUB - 1:SUB, :] * lam[SUB - 1:SUB, :] if up else a_t[0:1, :] * lam[0:1, :]

            mu_ref[...] = lax.fori_loop(0, ntile, step, mu_ref[...], unroll=2)

        @pl.when(d == 0)
        def _():
            run(False)

        @pl.when(d == 1)
        def _():
            run(True)

    spec = pl.BlockSpec((None, tc, cb), lambda j, d, t: (d, chunk(d, t), j))
    n_spec = pl.BlockSpec((None, SUB, cb), lambda j, d, t: (d, neighbour(d, t), j))
    g_spec = pl.BlockSpec((tc, cb), lambda j, d, t: (jnp.maximum(chunk(d, t) - ncc, 0), j))
    blocks = [((tc, cb), F32)] * 5 + [((SUB, cb), F32)]
    return pl.pallas_call(
        body, name=name, grid=(W // cb, 2, nch), in_specs=[spec, spec, n_spec, g_spec], out_specs=[spec, spec],
        out_shape=[_sds((2, n, W), F32)] * 2, scratch_shapes=[pltpu.VMEM((1, cb), F32)],
        compiler_params=_params(blocks, [((1, cb), F32)], ("parallel", "arbitrary", "arbitrary")),
    )(a_all, h_all, h_all, dya)


def _pool_window(v, w, tl, cb, transpose):
    left = w // 2
    right = w - 1 - left
    pos = lax.broadcasted_iota(jnp.int32, (tl, cb), 0) % GRID_W
    cnt = (jnp.minimum(pos + right, GRID_W - 1) - jnp.maximum(pos - left, 0) + 1).astype(F32)
    src = v / cnt if transpose else v
    lo, hi = (-right, left) if transpose else (-left, right)
    acc = src
    for o in range(lo, hi + 1):
        if o != 0:
            ok = (pos + o >= 0) & (pos + o < GRID_W)
            acc = acc + jnp.where(ok, pltpu.roll(src, (-o) % tl, 0), 0.0)
    return acc - v if transpose else acc / cnt - v


def _pool_z(src, row0, col0, L, W, transpose, dproj, name):
    G = len(POOL_WINDOWS)
    pd = W // G
    tl = _tile(L, TL, GRID_W)
    cb = _tile(pd, CB_SEQ, LANE)
    assert row0 % tl == 0 and col0 % cb == 0
    rb, cbk = row0 // tl, col0 // cb
    nj = pd // cb

    def body(x_ref, *rest):
        o_ref = rest[-1]
        for gi, w in enumerate(POOL_WINDOWS):
            @pl.when(pl.program_id(0) == gi)
            def _(w=w):
                o_ref[...] = _pool_window(x_ref[...], w, tl, cb, transpose).astype(o_ref.dtype)

    plain = pl.BlockSpec((tl, cb), lambda g, i, j: (i, g * nj + j))
    window = pl.BlockSpec((tl, cb), lambda g, i, j: (i + rb, cbk + g * nj + j))
    blocks = [((tl, cb), F32), ((tl, cb), MXU)]
    if transpose:
        return pl.pallas_call(
            body, name=name, grid=(G, L // tl, nj), in_specs=[plain, ANY], out_specs=window,
            out_shape=_sds(dproj.shape, dproj.dtype), input_output_aliases={1: 0},
            compiler_params=_params(blocks, dims=("parallel",) * 3),
        )(src, dproj)
    return pl.pallas_call(
        body, name=name, grid=(G, L // tl, nj), in_specs=[window], out_specs=plain,
        out_shape=_sds((L, W), MXU),
        compiler_params=_params(blocks, dims=("parallel",) * 3),
    )(src)


def _mix_fwd(hs, proj_all, ypre, b_pool, pool_scale, lc, name):
    L, W = ypre.shape
    tl = _tile(L, TL, SUB16)
    cb = _tile(W, CB_MIX, LANE)
    nj = W // cb
    assert lc % tl == 0
    rb = lc // tl

    def body(hs_ref, ga_ref, yp_ref, gb_ref, bp_ref, ps_ref, o_ref):
        p = pl.program_id(2)

        @pl.when(p == 0)
        def _():
            g = ga_ref[...]
            o_ref[...] = ((hs_ref[0] + hs_ref[1]) * (g * _sigmoid(g))).astype(o_ref.dtype)

        @pl.when(p == 1)
        def _():
            g = gb_ref[...]
            yb = (yp_ref[...] + bp_ref[...]) * ps_ref[...]
            o_ref[...] = (yb * (g * _sigmoid(g))).astype(o_ref.dtype)

    vec = pl.BlockSpec((1, cb), lambda i, j, p: (0, j))
    blocks = [((2, tl, cb), F32)] + [((tl, cb), F32)] * 3 + [((tl, cb), MXU)]
    return pl.pallas_call(
        body, name=name, grid=(L // tl, nj, 2),
        in_specs=[pl.BlockSpec((2, tl, cb), lambda i, j, p: (0, i + rb, j)),
                  pl.BlockSpec((tl, cb), lambda i, j, p: (i + rb, 2 * nj + j)),
                  pl.BlockSpec((tl, cb), lambda i, j, p: (i, j)),
                  pl.BlockSpec((tl, cb), lambda i, j, p: (i + rb, 3 * nj + j)), vec, vec],
        out_specs=pl.BlockSpec((tl, cb), lambda i, j, p: (i, p * nj + j)),
        out_shape=_sds((L, 2 * W), MXU),
        compiler_params=_params(blocks, dims=("parallel", "parallel", "arbitrary")),
    )(hs, proj_all, ypre, proj_all, b_pool, pool_scale)


def _dsilu(g, sg):
    return sg * (1.0 + g * (1.0 - sg))


def _mixa_bwd(dmixed, hs, proj_all, dproj, lc, W, name):
    L = dmixed.shape[0]
    tl = _tile(L, TL, SUB16)
    cb = _tile(W, CB_MIX, LANE)
    nj = W // cb
    rb = lc // tl

    def body(dm_ref, hs_ref, ga_ref, dp_in, dya_ref, dga_ref):
        del dp_in
        g = ga_ref[...]
        sg = _sigmoid(g)
        dm = dm_ref[...]
        dya_ref[...] = dm * (g * sg)
        dga_ref[...] = (dm * (hs_ref[0] + hs_ref[1]) * _dsilu(g, sg)).astype(dga_ref.dtype)

    blocks = [((tl, cb), F32)] * 3 + [((2, tl, cb), F32), ((tl, cb), MXU)]
    return pl.pallas_call(
        body, name=name, grid=(L // tl, nj),
        in_specs=[pl.BlockSpec((tl, cb), lambda i, j: (i, j)),
                  pl.BlockSpec((2, tl, cb), lambda i, j: (0, i + rb, j)),
                  pl.BlockSpec((tl, cb), lambda i, j: (i + rb, 2 * nj + j)), ANY],
        out_specs=[pl.BlockSpec((tl, cb), lambda i, j: (i, j)),
                   pl.BlockSpec((tl, cb), lambda i, j: (i + rb, 2 * nj + j))],
        out_shape=[_sds((L, W), F32), _sds(dproj.shape, dproj.dtype)],
        input_output_aliases={3: 1},
        compiler_params=_params(blocks, dims=("parallel", "parallel")),
    )(dmixed, hs, proj_all, dproj)


def _mixb_bwd(dmixed, ypre, proj_all, b_pool, pool_scale, dproj, lc, W, name):
    L = dmixed.shape[0]
    tl = _tile(L, TL, SUB16)
    cb = _tile(W, CB_MIX, LANE)
    nj = W // cb
    rb = lc // tl

    def body(dm_ref, yp_ref, gb_ref, bp_ref, ps_ref, dp_in, dyp_ref, dgb_ref, gbp_ref, gps_ref):
        del dp_in
        i = pl.program_id(1)

        @pl.when(i == 0)
        def _():
            gbp_ref[...] = jnp.zeros_like(gbp_ref)
            gps_ref[...] = jnp.zeros_like(gps_ref)

        g = gb_ref[...]
        sg = _sigmoid(g)
        dm = dm_ref[...]
        yp = yp_ref[...] + bp_ref[...]
        ps = ps_ref[...]
        dyb = dm * (g * sg)
        dyp = dyb * ps
        dgb_ref[...] = (dm * (yp * ps) * _dsilu(g, sg)).astype(dgb_ref.dtype)
        dyp_ref[...] = dyp.astype(dyp_ref.dtype)
        gbp_ref[...] += jnp.sum(dyp, axis=0, keepdims=True)
        gps_ref[...] += jnp.sum(dyb * yp, axis=0, keepdims=True)

    vec = pl.BlockSpec((1, cb), lambda j, i: (0, j))
    blocks = [((tl, cb), F32)] * 3 + [((tl, cb), MXU)] * 2 + [((1, cb), F32)] * 4
    return pl.pallas_call(
        body, name=name, grid=(nj, L // tl),
        in_specs=[pl.BlockSpec((tl, cb), lambda j, i: (i, nj + j)),
                  pl.BlockSpec((tl, cb), lambda j, i: (i, j)),
                  pl.BlockSpec((tl, cb), lambda j, i: (i + rb, 3 * nj + j)), vec, vec, ANY],
        out_specs=[pl.BlockSpec((tl, cb), lambda j, i: (i, j)),
                   pl.BlockSpec((tl, cb), lambda j, i: (i + rb, 3 * nj + j)), vec, vec],
        out_shape=[_sds((L, W), MXU), _sds(dproj.shape, dproj.dtype), _sds((1, W), F32), _sds((1, W), F32)],
        input_output_aliases={5: 1},
        compiler_params=_params(blocks, dims=("parallel", "arbitrary")),
    )(dmixed, ypre, proj_all, b_pool, pool_scale, dproj)


def _dproj_init(n, lc, W, name):
    cb = _tile(W, CB_MIX, LANE)
    nj = W // cb

    def body(o_ref):
        o_ref[...] = jnp.zeros_like(o_ref)

    return pl.pallas_call(
        body, name=name, grid=(3 * nj,), in_specs=[],
        out_specs=pl.BlockSpec((lc, cb), lambda j: (0, nj + j)),
        out_shape=_sds((n, 4 * W), MXU),
        compiler_params=_params([((lc, cb), MXU)], dims=("parallel",)),
    )()


def _final(x2, out, tgt, gate, gfin, name):
    L, D = x2.shape
    tl = _tile(L, TL_FINAL, SUB16)

    def body(x_ref, o_ref, t_ref, gate_ref, g_ref, dout_ref, dxn_ref, loss_ref, ggf_ref, dgate_ref):
        i = pl.program_id(0)

        @pl.when(i == 0)
        def _():
            loss_ref[...] = jnp.zeros_like(loss_ref)
            ggf_ref[...] = jnp.zeros_like(ggf_ref)
            dgate_ref[...] = jnp.zeros_like(dgate_ref)

        o = o_ref[...]
        gate_v = gate_ref[...]
        gv = g_ref[...]
        xn = x_ref[...] + gate_v * o
        s = lax.rsqrt(jnp.mean(xn * xn, axis=-1, keepdims=True) + EPS)
        xh = xn * s
        err = xh * gv - t_ref[...]
        tok = jnp.mean(err * err, axis=-1, keepdims=True)
        loss_ref[...] += 0.5 * jnp.sum(tok, axis=0, keepdims=True)
        dy = err / D
        ggf_ref[...] += jnp.sum(dy * xh, axis=0, keepdims=True)
        dxh = dy * gv
        dxn = s * (dxh - xh * jnp.mean(dxh * xh, axis=-1, keepdims=True))
        dgate_ref[...] += jnp.sum(dxn * o, axis=0, keepdims=True)
        dout_ref[...] = (gate_v * dxn).astype(dout_ref.dtype)
        dxn_ref[...] = dxn

    row = pl.BlockSpec((tl, D), lambda i: (i, 0))
    vec = pl.BlockSpec((1, D), lambda i: (0, 0))
    blocks = [((tl, D), F32)] * 4 + [((tl, D), MXU)] + [((1, D), F32)] * 4
    return pl.pallas_call(
        body, name=name, grid=(L // tl,), in_specs=[row, row, row, vec, vec],
        out_specs=[row, row, pl.BlockSpec((1, 1), lambda i: (0, 0)), vec, vec],
        out_shape=[_sds((L, D), MXU), _sds((L, D), F32), _sds((1, 1), F32), _sds((1, D), F32), _sds((1, D), F32)],
        compiler_params=_params(blocks, dims=("arbitrary",)),
    )(x2, out, tgt, gate, gfin)


def _adamw_parts(w2, parts, m2, v2, name):
    R, C = w2.shape
    nh = len(parts)
    ch = C // nh
    tr = _tile(R, max(SUB16, (512 * 1024) // (ch * (nh + 1))), SUB16)

    def body(w_ref, *rest):
        p_refs = rest[:nh]
        m_ref, v_ref, g_ref, d_ref, nm_ref, nv_ref = rest[nh:]
        for q in range(nh):
            @pl.when(pl.program_id(1) == q)
            def _(p_ref=p_refs[q]):
                g = p_ref[0].astype(F32)
                for p in range(1, NDEV):
                    g = g + p_ref[p].astype(F32)
                delta, nm, nv = _adam(w_ref[...], g, m_ref[...], v_ref[...])
                g_ref[...] = g
                d_ref[...] = delta
                nm_ref[...] = nm
                nv_ref[...] = nv

    blk = pl.BlockSpec((tr, ch), lambda i, h: (i, h))
    p_spec = pl.BlockSpec((NDEV, tr, ch), lambda i, h: (0, i, 0))
    blocks = [((tr, ch), F32)] * 7 + [((NDEV, tr, ch), parts[0].dtype)] * nh
    return pl.pallas_call(
        body, name=name, grid=(R // tr, nh),
        in_specs=[blk] + [p_spec] * nh + [blk, blk],
        out_specs=[blk] * 4, out_shape=[_sds((R, C), F32)] * 4,
        compiler_params=_params(blocks, dims=("parallel", "arbitrary")),
    )(w2, *parts, m2, v2)


def _small_sum(vs, ga, gc, name):
    ns, nm = vs.shape[1], ga.shape[1]

    def body(v_ref, ga_ref, gc_ref, tot_ref, gb_ref):
        tot = v_ref[0:1, :]
        gb = ga_ref[0:1, :]
        for p in range(1, NDEV):
            tot = tot + v_ref[p:p + 1, :]
            gb = gb + ga_ref[p:p + 1, :]
        for p in range(NDEV):
            gb = gb + gc_ref[p:p + 1, :]
        tot_ref[...] = tot
        gb_ref[...] = gb

    blocks = [((NDEV, ns), F32), ((NDEV, nm), F32), ((NDEV, nm), F32), ((1, ns), F32), ((1, nm), F32)]
    return pl.pallas_call(
        body, name=name, out_shape=[_sds((1, ns), F32), _sds((1, nm), F32)],
        compiler_params=_params(blocks),
    )(vs, ga, gc)


def _adamw_small(g_raw, w, m, v, lam_range, cctx_range, name):
    npk = w.shape[1]

    def body(g_ref, w_ref, m_ref, v_ref, go_ref, d_ref, nm_ref, nv_ref):
        wv = w_ref[...]
        g = g_ref[...]
        idx = lax.broadcasted_iota(jnp.int32, (1, npk), 1)
        in_lam = (idx >= lam_range[0]) & (idx < lam_range[1])
        in_cc = (idx >= cctx_range[0]) & (idx < cctx_range[1])
        sg = _sigmoid(wv)
        g = jnp.where(in_lam, g * (LRU_C * _sigmoid(-wv)), jnp.where(in_cc, g * _dsilu(wv, sg), g))
        delta, nm, nv = _adam(wv, g, m_ref[...], v_ref[...])
        go_ref[...] = g
        d_ref[...] = delta
        nm_ref[...] = nm
        nv_ref[...] = nv

    return pl.pallas_call(
        body, name=name, out_shape=[_sds((1, npk), F32)] * 4,
        compiler_params=_params([((1, npk), F32)] * 8),
    )(g_raw, w, m, v)


def _pack(pieces):
    return jnp.concatenate([p.reshape(1, -1) for p in pieces], axis=1)


def kernel(x, c, ctx, c_ctx, w_ada, b_ada, g_norm, w_in, conv_w, conv_b, lru_lambda, w_rgate, b_rgate, w_igate, b_igate, w_pool, b_pool, pool_scale, w_out, g_final, loss_target, m_c_ctx, m_w_ada, m_b_ada, m_g_norm, m_w_in, m_conv_w, m_conv_b, m_lru_lambda, m_w_rgate, m_b_rgate, m_w_igate, m_b_igate, m_w_pool, m_b_pool, m_pool_scale, m_w_out, m_g_final, v_c_ctx, v_w_ada, v_b_ada, v_g_norm, v_w_in, v_conv_w, v_conv_b, v_lru_lambda, v_w_rgate, v_b_rgate, v_w_igate, v_b_igate, v_w_pool, v_b_pool, v_pool_scale, v_w_out, v_g_final):
    L, D = x.shape[1], x.shape[2]
    lc = ctx.shape[1]
    n = lc + L
    W = conv_b.shape[1]
    heads, hd = w_rgate.shape[2], w_rgate.shape[4]
    G, pd = w_pool.shape[1], w_pool.shape[3]
    na = w_ada.shape[2]
    nb = w_in.shape[2]
    ws = W // NDEV
    me = 4 * lax.axis_index("x") + 2 * lax.axis_index("y") + lax.axis_index("c")

    (win_all, cw_all, lam_all, br_all, bi_all, c_all) = _all_gather(
        [w_in[0].astype(MXU), conv_w[0], lru_lambda[0], b_rgate[0], b_igate[0], c], "gather_w_in")
    cw = cw_all.transpose(1, 0, 2).reshape(4, W)
    lam = lam_all.transpose(1, 0, 2).reshape(2, W)
    br = br_all.transpose(1, 0, 2).reshape(2, W)
    bi = bi_all.transpose(1, 0, 2).reshape(2, W)

    cc = jnp.concatenate([c_all.reshape(NDEV, D), c_ctx.reshape(1, D), jnp.zeros((NDEV - 1, D), F32)], axis=0)
    b_loc = lax.dynamic_slice(b_ada, (0, me * na), (1, na))
    mod_loc, s_all = _ada_fwd(cc, w_ada[0], b_loc, "ada_fwd")
    (mod_all,) = _all_gather([mod_loc], "gather_mod")
    gate_w = [w_rgate[0].astype(MXU), w_igate[0].astype(MXU)]
    rest_w = [w_pool[0].astype(MXU), w_out[0].astype(MXU)]
    sent_gw = _send_start(gate_w, _place(gate_w, False, "place_gate_w", [mod_all]), False, "start_gate_w")
    sent_rw = _send_start(rest_w, _place(rest_w, False, "place_rest_w", [sent_gw[4]]), False, "start_rest_w")
    mod = mod_all.transpose(1, 0, 2).reshape(2 * NDEV, NDEV * na)
    mod_me = lax.dynamic_slice(mod, (me, 0), (1, 3 * D))
    shift, scale, gate = mod_me[:, :D], mod_me[:, D:2 * D], mod_me[:, 2 * D:]
    shift = _tie(shift, [sent_gw[4], sent_rw[4]], "tie_weights")
    shift_c, scale_c = mod[NDEV:NDEV + 1, :D], mod[NDEV:NDEV + 1, D:2 * D]

    x2, ctx2, tgt = x[0], ctx[0], loss_target[0]
    gfin = g_final.reshape(1, D)
    h_all = _norm_mod(x2, g_norm, shift, scale, n, lc, None, "norm_lat")
    h_all = _norm_mod(ctx2, g_norm, shift_c, scale_c, n, 0, h_all, "norm_ctx")
    proj_all = _mm_proj(h_all, win_all, "mm_proj")
    u_all = _conv_fwd(proj_all, cw, conv_b, lc, W, "conv_fwd")
    wr_all, wi_all = _send_wait(sent_gw, u_all, False, "wait_gate_w")
    wr = wr_all.transpose(1, 2, 0, 3, 4).reshape(2, heads, hd, hd)
    wi = wi_all.transpose(1, 2, 0, 3, 4).reshape(2, heads, hd, hd)
    a_all, b_all = _gates_fwd(u_all, wr, wi, br, bi, lam, "gates_fwd")
    hs = _scan_fwd(a_all, b_all, lc, "scan_fwd")
    z = _pool_z(proj_all, lc, W, L, W, False, None, "pool_z")
    wpool_all, wout_all = _send_wait(sent_rw, z, False, "wait_rest_w")
    wpool = wpool_all.transpose(1, 0, 2, 3).reshape(G, pd, pd)
    wout = wout_all.reshape(2 * W, D)
    ypre = _mm_group(z, wpool, "fwd", F32, "mm_pool")
    mixed = _mix_fwd(hs, proj_all, ypre, b_pool, pool_scale, lc, "mix_fwd")
    out = _mm_plain(mixed, wout, NN, F32, "mm_out")
    d_out, dxn, loss_p, ggf, dgate = _final(x2, out, tgt, gate, gfin, "final")

    dmixed = _mm_plain(d_out, wout, NT, F32, "mm_dmixed")
    gwout = _mm_plain(mixed, d_out, TN_DIMS, MXU, "mm_gwout")
    ex_o = [gwout.reshape(NDEV, 2 * W // NDEV, D)]
    sent_o = _send_start(ex_o, _place(ex_o, True, "place_gwout"), True, "start_gwout")
    dproj = _dproj_init(n, lc, W, "dproj_init")
    dya, dproj = _mixa_bwd(dmixed, hs, proj_all, dproj, lc, W, "mixa_bwd")
    dypre, dproj, gbp, gps = _mixb_bwd(dmixed, ypre, proj_all, _tie(b_pool, [sent_o[4]], "tie_gwout"), pool_scale,
                                       dproj, lc, W, "mixb_bwd")
    dz = _mm_group(dypre, wpool, "bwd", F32, "mm_dz")
    gwpool = _mm_group(z, dypre, "wgrad", MXU, "mm_gwpool")
    dproj = _pool_z(dz, lc, W, L, W, True, dproj, "pool_z_bwd")
    da, db = _scan_bwd(a_all, hs, dya, lc, "scan_bwd")
    du, gwr, gwi, gbr, gbi, gcl = _gates_bwd(u_all, da, db, wr, wi, br, bi, lam, "gates_bwd")
    ex_s = [gwpool.reshape(G, NDEV, pd // NDEV, pd).transpose(1, 0, 2, 3),
            gwr.reshape(2, heads, NDEV, hd // NDEV, hd).transpose(2, 0, 1, 3, 4),
            gwi.reshape(2, heads, NDEV, hd // NDEV, hd).transpose(2, 0, 1, 3, 4)]
    sent_s = _send_start(ex_s, _place(ex_s, True, "place_gsmall"), True, "start_gsmall")
    dproj, gcw, gcb = _conv_bwd(du, proj_all, _tie(cw, [sent_s[4]], "tie_gsmall"), dproj, lc, W, "conv_bwd")
    sent_i, tok = [], None
    for q in range(GWIN_PARTS):
        part = _mm_gwin(h_all, dproj, nb, q, GWIN_PARTS, f"mm_gwin_{q}", dep=tok)
        sent_i.append(_send_start([part], _place([part], True, f"place_gwin_{q}"), True, f"start_gwin_{q}"))
        tok = sent_i[-1][4]
    dh_all = _mm_dh(dproj, win_all, "mm_dh", dep=tok)
    grad_x, dshift, dscale, ggn = _norm_bwd(x2, dh_all, lc, g_norm, scale, dxn, jnp.zeros((1, D), F32), "norm_bwd_lat")
    _, dshift_c, dscale_c, ggn = _norm_bwd(ctx2, dh_all, 0, g_norm, scale_c, None, ggn, "norm_bwd_ctx")

    dmod_me = jnp.concatenate([dshift, dscale, dgate], axis=1)
    dmod_c = jnp.concatenate([dshift_c, dscale_c, jnp.zeros((1, D), F32)], axis=1)
    smalls = [ggf, ggn, gcw, gcb, gcl, gbr, gbi, gbp, gps, jnp.pad(loss_p, ((0, 0), (0, LANE - 1)))]
    sizes = [s.size for s in smalls]
    small_all, dmod_all, dmodc_all = _all_gather([_pack(smalls), dmod_me, dmod_c], "gather_small")
    ga = lax.dynamic_slice(dmod_all.reshape(NDEV, 3 * D), (0, me * na), (NDEV, na))
    gc = lax.dynamic_slice(dmodc_all.reshape(NDEV, 3 * D), (0, me * na), (NDEV, na))
    g_wada, d_wada, nm_wada, nv_wada, pc = _ada_bwd(s_all, ga, gc, w_ada[0], m_w_ada[0], v_w_ada[0], "ada_bwd")
    (pc_all,) = _all_gather([pc[0:1]], "gather_cctx")
    tot, gb_ada = _small_sum(
        jnp.concatenate([small_all.reshape(NDEV, -1), pc_all.reshape(NDEV, D)], axis=1),
        dmod_all.reshape(NDEV, 3 * D), dmodc_all.reshape(NDEV, 3 * D), "small_sum")
    offs = [0]
    for s in sizes + [D]:
        offs.append(offs[-1] + s)
    t_ggf, t_ggn, t_gcw, t_gcb, t_gcl, t_gbr, t_gbi, t_gbp, t_gps, t_loss, t_pc = [
        tot[:, offs[i]:offs[i + 1]] for i in range(len(offs) - 1)]

    def shard(t, rows):
        return lax.dynamic_slice(t.reshape(rows, W), (0, me * ws), (rows, ws))

    def big(wv, parts, mv, vv, name):
        shp = wv.shape
        C = shp[-1]
        if not isinstance(parts, list):
            parts = [parts]
        parts = [p.reshape(NDEV, -1, C // len(parts)) for p in parts]
        outs = _adamw_parts(wv.reshape(-1, C), parts, mv.reshape(-1, C), vv.reshape(-1, C), name)
        return [o.reshape(shp) for o in outs]

    (recv_o,) = _send_wait(sent_o, tot, True, "wait_gwout")
    recv_p, recv_r, recv_i = _send_wait(sent_s, tot, True, "wait_gsmall")
    r_wout = big(w_out, recv_o, m_w_out, v_w_out, "adamw_w_out")
    r_wpool = big(w_pool, recv_p, m_w_pool, v_w_pool, "adamw_w_pool")
    r_wr = big(w_rgate, recv_r, m_w_rgate, v_w_rgate, "adamw_w_rgate")
    r_wi = big(w_igate, recv_i, m_w_igate, v_w_igate, "adamw_w_igate")
    r_wada = [o.reshape(w_ada.shape) for o in (g_wada, d_wada, nm_wada, nv_wada)]

    names = ["c_ctx", "b_ada", "g_norm", "conv_w", "conv_b", "lru_lambda", "b_rgate", "b_igate", "b_pool",
             "pool_scale", "g_final"]
    sw = [c_ctx, b_ada, g_norm, conv_w, conv_b, lru_lambda, b_rgate, b_igate, b_pool, pool_scale, g_final]
    sm = [m_c_ctx, m_b_ada, m_g_norm, m_conv_w, m_conv_b, m_lru_lambda, m_b_rgate, m_b_igate, m_b_pool,
          m_pool_scale, m_g_final]
    sv = [v_c_ctx, v_b_ada, v_g_norm, v_conv_w, v_conv_b, v_lru_lambda, v_b_rgate, v_b_igate, v_b_pool,
          v_pool_scale, v_g_final]
    sg = [t_pc, gb_ada, t_ggn, shard(t_gcw, 4), t_gcb, shard(t_gcl, 2), shard(t_gbr, 2), shard(t_gbi, 2), t_gbp,
          t_gps, t_ggf]
    poffs = [0]
    for wv in sw:
        poffs.append(poffs[-1] + wv.size)
    lam_range = (poffs[5], poffs[6])
    cctx_range = (poffs[0], poffs[1])
    small_out = _adamw_small(_pack(sg), _pack(sw), _pack(sm), _pack(sv), lam_range, cctx_range, "adamw_small")
    recv_w = [_send_wait(sent_i[q], small_out[0], True, f"wait_gwin_{q}")[0] for q in range(GWIN_PARTS)]
    r_win = big(w_in, recv_w, m_w_in, v_w_in, "adamw_w_in")
    r_small = {}
    for i, nm in enumerate(names):
        r_small[nm] = [o[:, poffs[i]:poffs[i + 1]].reshape(sw[i].shape) for o in small_out]

    res = dict(r_small)
    res.update(w_ada=r_wada, w_in=r_win, w_rgate=r_wr, w_igate=r_wi, w_pool=r_wpool, w_out=r_wout)
    order = ["c_ctx", "w_ada", "b_ada", "g_norm", "w_in", "conv_w", "conv_b", "lru_lambda", "w_rgate", "b_rgate",
             "w_igate", "b_igate", "w_pool", "b_pool", "pool_scale", "w_out", "g_final"]
    loss = t_loss[0, 0]
    outs = [loss, grad_x.reshape(x.shape)]
    for q in range(4):
        outs += [res[nm][q] for nm in order]
    return tuple(outs)
```

```python
import functools

import jax
import jax.numpy as jnp
from jax import lax
from jax.experimental import pallas as pl
from jax.experimental.pallas import tpu as pltpu

NDEV = 8
GRID_W = 64
POOL_WINDOWS = (2, 4, 8, 16)
LRU_C = 8.0
EPS = 1e-6
ADAM_LR = 0.001
ADAM_B1 = 0.9
ADAM_B2 = 0.999
ADAM_EPS = 1e-08
ADAM_WD = 0.01
ADAM_STEP = 10

F32 = jnp.float32
MXU = jnp.bfloat16

VMEM_BYTES = 64 * 1024 * 1024
VMEM_SLACK = 8 * 1024 * 1024
SUB = 8
SUB16 = 16
LANE = 128

TM = 1152
TN = 1024
TK = 2048
TL = 256
TL_FINAL = 128
CB_SEQ = 256
CB_SCAN = 1024
CB_MIX = 2048
TR_CONV = 576
GWIN_PARTS = 4

MESH_ID = pl.DeviceIdType.MESH


def _tile(n, pref, align):
    if n <= pref:
        return n
    for t in range(pref - pref % align, 0, -align):
        if n % t == 0:
            return t
    return n


def _nbytes(shape, dtype):
    n = 1
    for s in shape:
        if s is not None:
            n *= s
    return n * jnp.dtype(dtype).itemsize


def _params(blocks, scratch=(), dims=None):
    need = 2 * sum(_nbytes(s, d) for s, d in blocks) + sum(_nbytes(s, d) for s, d in scratch) + VMEM_SLACK
    kw = dict(vmem_limit_bytes=int(min(max(need, 2 * VMEM_SLACK), VMEM_BYTES - VMEM_SLACK // 2)))
    if dims is not None:
        kw["dimension_semantics"] = dims
    return pltpu.CompilerParams(**kw)


def _sds(shape, dtype):
    return jax.ShapeDtypeStruct(tuple(shape), dtype)


ANY = pl.BlockSpec(memory_space=pl.ANY)


def _ids():
    return lax.axis_index("x"), lax.axis_index("y"), lax.axis_index("c")


def _sigmoid(v):
    return 0.5 * jnp.tanh(0.5 * v) + 0.5


def _sigmoid_small(v):
    return jax.nn.sigmoid(v)


def _softplus(v):
    return jnp.maximum(v, 0.0) + jnp.log1p(jnp.exp(-jnp.abs(v)))


def _all_gather(xs, name):
    n = len(xs)

    def body(*refs):
        x_refs, o_refs = refs[:n], refs[n:2 * n]
        send_sems, recv_sems, local_sems = refs[2 * n:]
        x, y, c = _ids()
        me, sibling = (x, y, c), (x, y, 1 - c)
        chips = [(1 - x, y), (x, 1 - y), (1 - x, 1 - y)]

        def slot(a, p):
            return o_refs[a].at[4 * p[0] + 2 * p[1] + p[2]]

        def copy(a, k, block, to, src=None):
            return pltpu.make_async_remote_copy(
                src_ref=slot(a, block) if src is None else src, dst_ref=slot(a, block),
                send_sem=send_sems.at[7 * a + k], recv_sem=recv_sems.at[7 * a + k],
                device_id=to, device_id_type=MESH_ID)

        mine, first, passed = [], [], []
        for a in range(n):
            m = pltpu.make_async_copy(x_refs[a], slot(a, me), local_sems.at[a])
            m.start()
            mine.append(m)
            f = [copy(a, 0, me, sibling, src=x_refs[a])]
            f += [copy(a, 1 + j, me, (*chip, c), src=x_refs[a]) for j, chip in enumerate(chips)]
            for cp in f:
                cp.start()
            first += f
        for a in range(n):
            for j, chip in enumerate(chips):
                copy(a, 1 + j, (*chip, c), me).wait_recv()
                p = copy(a, 4 + j, (*chip, c), sibling)
                p.start()
                passed.append(p)
        for a in range(n):
            copy(a, 0, sibling, me).wait_recv()
            for j, chip in enumerate(chips):
                copy(a, 4 + j, (*chip, 1 - c), me).wait_recv()
        for cp in first + passed:
            cp.wait_send()
        for m in mine:
            m.wait()

    return pl.pallas_call(
        body, name=name,
        out_shape=[_sds((NDEV,) + v.shape, v.dtype) for v in xs],
        in_specs=[ANY] * n, out_specs=[ANY] * n,
        scratch_shapes=[pltpu.SemaphoreType.DMA((7 * n,)), pltpu.SemaphoreType.DMA((7 * n,)),
                        pltpu.SemaphoreType.DMA((n,))],
    )(*xs)


HBM = pl.BlockSpec(memory_space=pltpu.HBM)
SEM = pl.BlockSpec(memory_space=pltpu.SEMAPHORE)
EFFECT = pltpu.SideEffectType.DATAFLOW_SIDE_EFFECTING


def _peers():
    x, y, c = _ids()
    out = []
    for k in range(1, NDEV):
        px = 1 - x if k & 4 else x
        py = 1 - y if k & 2 else y
        pc = 1 - c if k & 1 else c
        out.append(((px, py, pc), 4 * px + 2 * py + pc))
    return out, 4 * x + 2 * y + c


def _tie(v, deps, name):
    def body(v_ref, *rest):
        rest[-1][...] = v_ref[...]

    vmem = pl.BlockSpec(memory_space=pltpu.VMEM)
    return pl.pallas_call(
        body, name=name, out_shape=_sds(v.shape, v.dtype), in_specs=[vmem] + [ANY] * len(deps), out_specs=vmem,
    )(v, *deps)


def _place(srcs, from_slot, name, deps=()):
    n = len(srcs)
    blks = [v.shape[1:] if from_slot else v.shape for v in srcs]

    nd = len(deps)

    def body(*refs):
        s_refs, l_refs = refs[:n], refs[n + nd:2 * n + nd]
        bufs, sems = refs[2 * n + nd:3 * n + nd], refs[3 * n + nd]
        x, y, c = _ids()
        me = 4 * x + 2 * y + c
        ins = [pltpu.make_async_copy(s_refs[a].at[me] if from_slot else s_refs[a], bufs[a], sems.at[a])
               for a in range(n)]
        outs = [pltpu.make_async_copy(bufs[a], l_refs[a].at[me], sems.at[n + a]) for a in range(n)]
        for cp in ins:
            cp.start()
        for a in range(n):
            ins[a].wait()
            outs[a].start()
        for cp in outs:
            cp.wait()

    scratch = [(b, v.dtype) for b, v in zip(blks, srcs)]
    return pl.pallas_call(
        body, name=name, out_shape=[_sds((NDEV,) + b, v.dtype) for b, v in zip(blks, srcs)],
        in_specs=[ANY] * (n + nd), out_specs=[ANY] * n,
        scratch_shapes=[pltpu.VMEM(b, d) for b, d in scratch] + [pltpu.SemaphoreType.DMA((2 * n,))],
        compiler_params=_params([], scratch),
    )(*srcs, *deps)


def _send_copies(s_refs, l_refs, ssem, rsem, from_slot, receiving):
    peers, me = _peers()
    out = []
    for a in range(len(s_refs)):
        for k, (dev, idx) in enumerate(peers):
            out.append(pltpu.make_async_remote_copy(
                src_ref=s_refs[a].at[idx] if from_slot else s_refs[a],
                dst_ref=l_refs[a].at[idx if receiving else me],
                send_sem=ssem.at[7 * a + k], recv_sem=rsem.at[7 * a + k], device_id=dev, device_id_type=MESH_ID))
    return out


def _send_start(srcs, lands, from_slot, name):
    n = len(srcs)

    def body(*refs):
        s_refs, l_refs = refs[:n], refs[n:2 * n]
        ssem, rsem = refs[2 * n], refs[2 * n + 1]
        token = refs[-1]
        for send in _send_copies(s_refs, l_refs, ssem, rsem, from_slot, False):
            send.start()
        token[...] = jnp.zeros_like(token)

    bufs = list(srcs) + list(lands)
    outs = pl.pallas_call(
        body, name=name,
        out_shape=[pltpu.SemaphoreType.DMA((7 * n,)), pltpu.SemaphoreType.DMA((7 * n,))]
        + [pltpu.HBM(v.shape, v.dtype) for v in bufs] + [_sds((SUB, LANE), F32)],
        in_specs=[HBM] * (2 * n), out_specs=[SEM, SEM] + [HBM] * (2 * n) + [pl.BlockSpec(memory_space=pltpu.VMEM)],
        input_output_aliases={i: 2 + i for i in range(2 * n)},
        compiler_params=pltpu.CompilerParams(has_side_effects=EFFECT),
    )(*[pltpu.with_memory_space_constraint(v, pltpu.HBM) for v in bufs])
    return outs[0], outs[1], list(outs[2:2 + n]), list(outs[2 + n:2 + 2 * n]), outs[-1]


def _send_wait(started, after, from_slot, name):
    ssem, rsem, srcs, lands, _ = started
    n = len(srcs)

    def body(*refs):
        s_refs, l_refs = refs[:n], refs[n:2 * n]
        ssem_ref, rsem_ref = refs[2 * n], refs[2 * n + 1]
        for recv in _send_copies(s_refs, l_refs, ssem_ref, rsem_ref, from_slot, True):
            recv.wait_send()
            recv.wait_recv()

    bufs = list(srcs) + list(lands)
    outs = pl.pallas_call(
        body, name=name, out_shape=[pltpu.HBM(v.shape, v.dtype) for v in bufs],
        in_specs=[HBM] * (2 * n) + [SEM, SEM, ANY], out_specs=[HBM] * (2 * n),
        input_output_aliases={i: i for i in range(2 * n)},
        compiler_params=pltpu.CompilerParams(has_side_effects=EFFECT),
    )(*bufs, ssem, rsem, after)
    return list(outs[n:])


NN = (((1,), (0,)), ((), ()))
NT = (((1,), (1,)), ((), ()))
TN_DIMS = (((0,), (0,)), ((), ()))


def _mm(a, b, *, grid, a_spec, b_spec, o_spec, out_shape, acc_shape, dims, name, dep=None):
    k_axis = len(grid) - 1
    nk = grid[k_axis]
    extra = [] if dep is None else [dep]

    def body(a_ref, b_ref, *rest):
        o_ref, acc_ref = rest[-2], rest[-1]
        k = pl.program_id(k_axis)

        def prod():
            return lax.dot_general(a_ref[...], b_ref[...], dims, preferred_element_type=F32)

        if nk == 1:
            o_ref[...] = prod().astype(o_ref.dtype)
            return

        @pl.when(k == 0)
        def _():
            acc_ref[...] = prod()

        if nk > 2:
            @pl.when((k > 0) & (k < nk - 1))
            def _():
                acc_ref[...] += prod()

        @pl.when(k == nk - 1)
        def _():
            o_ref[...] = (acc_ref[...] + prod()).astype(o_ref.dtype)

    blocks = [(a_spec.block_shape, a.dtype), (b_spec.block_shape, b.dtype), (o_spec.block_shape, out_shape.dtype)]
    return pl.pallas_call(
        body, name=name, grid=grid, in_specs=[a_spec, b_spec] + [ANY] * len(extra), out_specs=o_spec,
        out_shape=out_shape, scratch_shapes=[pltpu.VMEM(acc_shape, F32)],
        compiler_params=_params(blocks, [(acc_shape, F32)], ("parallel",) * k_axis + ("arbitrary",)),
    )(a, b, *extra)


def _mm_plain(a, b, dims, out_dtype, name):
    if dims == TN_DIMS:
        (K, M), N = a.shape, b.shape[1]
    elif dims == NT:
        (M, K), N = a.shape, b.shape[0]
    else:
        (M, K), N = a.shape, b.shape[1]
    tm, tn = _tile(M, TM, LANE), _tile(N, TN, LANE)
    tk = _tile(K, TK, LANE if dims != TN_DIMS else SUB16)
    if dims == TN_DIMS:
        a_spec = pl.BlockSpec((tk, tm), lambda i, j, k: (k, i))
    else:
        a_spec = pl.BlockSpec((tm, tk), lambda i, j, k: (i, k))
    if dims == NT:
        b_spec = pl.BlockSpec((tn, tk), lambda i, j, k: (j, k))
    else:
        b_spec = pl.BlockSpec((tk, tn), lambda i, j, k: (k, j))
    return _mm(a, b, grid=(M // tm, N // tn, K // tk), a_spec=a_spec, b_spec=b_spec,
               o_spec=pl.BlockSpec((tm, tn), lambda i, j, k: (i, j)),
               out_shape=_sds((M, N), out_dtype), acc_shape=(tm, tn), dims=dims, name=name)


def _mm_proj(h_all, win_all, name):
    n, D = h_all.shape
    nb = win_all.shape[2]
    tm, tn, tk = _tile(n, TM, SUB16), _tile(nb, TN, LANE), _tile(D, TK, LANE)
    nbn = nb // tn
    return _mm(h_all, win_all, grid=(n // tm, NDEV * nbn, D // tk),
               a_spec=pl.BlockSpec((tm, tk), lambda i, j, k: (i, k)),
               b_spec=pl.BlockSpec((None, tk, tn), lambda i, j, k: (j // nbn, k, j % nbn)),
               o_spec=pl.BlockSpec((tm, tn), lambda i, j, k: (i, j)),
               out_shape=_sds((n, NDEV * nb), F32), acc_shape=(tm, tn), dims=NN, name=name)


def _mm_dh(dproj, win_all, name, dep=None):
    n = dproj.shape[0]
    _, D, nb = win_all.shape
    tm, tn, tk = _tile(n, TM, SUB16), _tile(D, TN, LANE), _tile(nb, TK, LANE)
    nbk = nb // tk
    return _mm(dproj, win_all, grid=(n // tm, D // tn, NDEV * nbk),
               a_spec=pl.BlockSpec((tm, tk), lambda i, j, k: (i, k)),
               b_spec=pl.BlockSpec((None, tn, tk), lambda i, j, k: (k // nbk, j, k % nbk)),
               o_spec=pl.BlockSpec((tm, tn), lambda i, j, k: (i, j)),
               out_shape=_sds((n, D), F32), acc_shape=(tm, tn), dims=NT, name=name, dep=dep)


def _transpose(x, name):
    R, C = x.shape
    tr, tc = _tile(R, TL, LANE), _tile(C, 2 * TL, LANE)

    def body(x_ref, o_ref):
        o_ref[...] = x_ref[...].T

    return pl.pallas_call(
        body, name=name, grid=(R // tr, C // tc),
        in_specs=[pl.BlockSpec((tr, tc), lambda i, j: (i, j))],
        out_specs=pl.BlockSpec((tc, tr), lambda i, j: (j, i)),
        out_shape=_sds((C, R), x.dtype),
        compiler_params=_params([((tr, tc), x.dtype)] * 2, dims=("parallel", "parallel")),
    )(x)


def _mm_gwin(h_t, dproj, nb, part, nparts, name, dep=None):
    D, n = h_t.shape
    nbp = nb // nparts
    tm, tn, tk = _tile(D, TM, LANE), _tile(nbp, TN, LANE), _tile(n, TK + TK // 8, LANE)
    nbn = nbp // tn
    return _mm(h_t, dproj, grid=(D // tm, NDEV * nbn, n // tk),
               a_spec=pl.BlockSpec((tm, tk), lambda i, j, k: (i, k)),
               b_spec=pl.BlockSpec((tk, tn), lambda i, j, k: (k, (j // nbn) * (nb // tn) + part * nbn + j % nbn)),
               o_spec=pl.BlockSpec((None, tm, tn), lambda i, j, k: (j // nbn, i, j % nbn)),
               out_shape=_sds((NDEV, D, nbp), MXU), acc_shape=(tm, tn), dims=NN, name=name, dep=dep)


def _mm_group(a, b, mode, out_dtype, name):
    if mode == "wgrad":
        L, W = a.shape
        G = len(POOL_WINDOWS)
        pd = W // G
        tm, tn, tk = _tile(pd, TM, LANE), _tile(pd, TN, LANE), _tile(L, TK, SUB16)
        nm, nn = pd // tm, pd // tn
        return _mm(a, b, grid=(G, nm, nn, L // tk),
                   a_spec=pl.BlockSpec((tk, tm), lambda g, i, j, k: (k, g * nm + i)),
                   b_spec=pl.BlockSpec((tk, tn), lambda g, i, j, k: (k, g * nn + j)),
                   o_spec=pl.BlockSpec((None, tm, tn), lambda g, i, j, k: (g, i, j)),
                   out_shape=_sds((G, pd, pd), out_dtype), acc_shape=(tm, tn), dims=TN_DIMS, name=name)
    L, W = a.shape
    G, pd, _ = b.shape
    tm, tn, tk = _tile(L, TM, SUB16), _tile(pd, TN, LANE), _tile(pd, TK, LANE)
    nn, nk = pd // tn, pd // tk
    if mode == "fwd":
        b_spec = pl.BlockSpec((None, tk, tn), lambda g, i, j, k: (g, k, j))
        dims = NN
    else:
        b_spec = pl.BlockSpec((None, tn, tk), lambda g, i, j, k: (g, j, k))
        dims = NT
    return _mm(a, b, grid=(G, L // tm, nn, nk),
               a_spec=pl.BlockSpec((tm, tk), lambda g, i, j, k: (i, g * nk + k)),
               b_spec=b_spec,
               o_spec=pl.BlockSpec((tm, tn), lambda g, i, j, k: (i, g * nn + j)),
               out_shape=_sds((L, W), out_dtype), acc_shape=(tm, tn), dims=dims, name=name)


def _ada_fwd(cc, w_loc, b_loc, name):
    R, D = cc.shape
    na = w_loc.shape[1]
    tk = _tile(D, 512, LANE)

    def body(c_ref, w_ref, b_ref, mod_ref, s_ref):
        k = pl.program_id(0)
        cv = c_ref[...]
        s = cv * _sigmoid_small(cv)
        s_ref[...] = s

        @pl.when(k == 0)
        def _():
            mod_ref[...] = jnp.broadcast_to(b_ref[...], mod_ref.shape)

        mod_ref[...] += lax.dot_general(s.astype(MXU), w_ref[...].astype(MXU), NN, preferred_element_type=F32)

    blocks = [((R, tk), F32), ((tk, na), F32), ((1, na), F32), ((R, na), F32), ((R, tk), F32)]
    return pl.pallas_call(
        body, name=name, grid=(D // tk,),
        in_specs=[pl.BlockSpec((R, tk), lambda k: (0, k)), pl.BlockSpec((tk, na), lambda k: (k, 0)),
                  pl.BlockSpec((1, na), lambda k: (0, 0))],
        out_specs=[pl.BlockSpec((R, na), lambda k: (0, 0)), pl.BlockSpec((R, tk), lambda k: (0, k))],
        out_shape=[_sds((R, na), F32), _sds((R, D), F32)],
        compiler_params=_params(blocks, dims=("arbitrary",)),
    )(cc, w_loc, b_loc)


def _adam(w, g, m, v):
    m = ADAM_B1 * m + (1.0 - ADAM_B1) * g
    v = ADAM_B2 * v + (1.0 - ADAM_B2) * (g * g)
    m_hat = m / (1.0 - ADAM_B1 ** ADAM_STEP)
    v_hat = v / (1.0 - ADAM_B2 ** ADAM_STEP)
    delta = -ADAM_LR * (m_hat / (jnp.sqrt(v_hat) + ADAM_EPS) + ADAM_WD * w)
    return delta, m, v


def _ada_bwd(s_all, ga, gc, w_loc, m_loc, v_loc, name):
    D, na = w_loc.shape
    tr = _tile(D, 256, LANE)

    def body(s_ref, ga_ref, gc_ref, w_ref, m_ref, v_ref, g_ref, d_ref, nm_ref, nv_ref, pc_ref):
        dmc = gc_ref[0:1, :]
        for p in range(1, NDEV):
            dmc = dmc + gc_ref[p:p + 1, :]
        rows = lax.broadcasted_iota(jnp.int32, (NDEV, na), 0)
        dmc8 = jnp.where(rows == 0, jnp.broadcast_to(dmc, (NDEV, na)), 0.0)
        dm = jnp.concatenate([ga_ref[...], dmc8], axis=0).astype(MXU)
        dmc16 = jnp.concatenate([dmc8, jnp.zeros_like(dmc8)], axis=0).astype(MXU)
        w = w_ref[...]
        g = lax.dot_general(s_ref[...].astype(MXU), dm, TN_DIMS, preferred_element_type=F32)
        pc_ref[...] = lax.dot_general(dmc16, w.astype(MXU), NT, preferred_element_type=F32)
        delta, nm, nv = _adam(w, g, m_ref[...], v_ref[...])
        g_ref[...] = g
        d_ref[...] = delta
        nm_ref[...] = nm
        nv_ref[...] = nv

    big = pl.BlockSpec((tr, na), lambda i: (i, 0))
    full = pl.BlockSpec((NDEV, na), lambda i: (0, 0))
    srow = pl.BlockSpec((2 * NDEV, tr), lambda i: (0, i))
    blocks = [((2 * NDEV, tr), F32)] * 2 + [((NDEV, na), F32)] * 2 + [((tr, na), F32)] * 7
    return pl.pallas_call(
        body, name=name, grid=(D // tr,),
        in_specs=[srow, full, full, big, big, big],
        out_specs=[big, big, big, big, srow],
        out_shape=[_sds((D, na), F32)] * 4 + [_sds((2 * NDEV, D), F32)],
        compiler_params=_params(blocks, dims=("parallel",)),
    )(s_all, ga, gc, w_loc, m_loc, v_loc)


def _norm_mod(x2, g, shift, scale, n, row0, h_prev, name):
    R, D = x2.shape
    tl = _tile(R, TL, SUB16)
    assert row0 % tl == 0
    b0 = row0 // tl

    def body(x_ref, g_ref, sh_ref, sc_ref, *rest):
        o_ref = rest[-1]
        xv = x_ref[...]
        s = lax.rsqrt(jnp.mean(xv * xv, axis=-1, keepdims=True) + EPS)
        nrm = xv * s * g_ref[...]
        o_ref[...] = (nrm * (1.0 + sc_ref[...]) + sh_ref[...]).astype(o_ref.dtype)

    vec = pl.BlockSpec((1, D), lambda i: (0, 0))
    in_specs = [pl.BlockSpec((tl, D), lambda i: (i, 0)), vec, vec, vec]
    args = [x2, g, shift, scale]
    aliases = {}
    if h_prev is not None:
        in_specs.append(ANY)
        args.append(h_prev)
        aliases = {4: 0}
    blocks = [((tl, D), F32), ((tl, D), MXU)] + [((1, D), F32)] * 3
    return pl.pallas_call(
        body, name=name, grid=(R // tl,), in_specs=in_specs,
        out_specs=pl.BlockSpec((tl, D), lambda i: (i + b0, 0)),
        out_shape=_sds((n, D), MXU), input_output_aliases=aliases,
        compiler_params=_params(blocks, dims=("parallel",)),
    )(*args)


def _norm_bwd(x2, dh_all, row0, g, scale, dxn, ggn0, name):
    R, D = x2.shape
    tl = _tile(R, TL_FINAL, SUB)
    assert row0 % tl == 0
    b0 = row0 // tl
    with_x = dxn is not None

    def body(*refs):
        if with_x:
            x_ref, dh_ref, g_ref, sc_ref, gg0_ref, dxn_ref, gx_ref, dsh_ref, dsc_ref, gg_ref = refs
        else:
            x_ref, dh_ref, g_ref, sc_ref, gg0_ref, dsh_ref, dsc_ref, gg_ref = refs
        i = pl.program_id(0)

        @pl.when(i == 0)
        def _():
            dsh_ref[...] = jnp.zeros_like(dsh_ref)
            dsc_ref[...] = jnp.zeros_like(dsc_ref)
            gg_ref[...] = gg0_ref[...]

        xv = x_ref[...]
        dh = dh_ref[...]
        gv = g_ref[...]
        s = lax.rsqrt(jnp.mean(xv * xv, axis=-1, keepdims=True) + EPS)
        xh = xv * s
        dsh_ref[...] += jnp.sum(dh, axis=0, keepdims=True)
        dsc_ref[...] += jnp.sum(dh * (xh * gv), axis=0, keepdims=True)
        dn = dh * (1.0 + sc_ref[...])
        gg_ref[...] += jnp.sum(dn * xh, axis=0, keepdims=True)
        if with_x:
            dxh = dn * gv
            dx = s * (dxh - xh * jnp.mean(dxh * xh, axis=-1, keepdims=True))
            gx_ref[...] = dx + dxn_ref[...]

    vec = pl.BlockSpec((1, D), lambda i: (0, 0))
    row = pl.BlockSpec((tl, D), lambda i: (i, 0))
    in_specs = [row, pl.BlockSpec((tl, D), lambda i: (i + b0, 0)), vec, vec, vec]
    args = [x2, dh_all, g, scale, ggn0]
    out_specs = [vec, vec, vec]
    out_shape = [_sds((1, D), F32)] * 3
    if with_x:
        in_specs.append(row)
        args.append(dxn)
        out_specs = [row] + out_specs
        out_shape = [_sds((R, D), F32)] + out_shape
    blocks = [((tl, D), F32)] * (4 if with_x else 2) + [((1, D), F32)] * 6
    outs = pl.pallas_call(
        body, name=name, grid=(R // tl,), in_specs=in_specs, out_specs=out_specs, out_shape=out_shape,
        compiler_params=_params(blocks, dims=("arbitrary",)),
    )(*args)
    return tuple(outs) if with_x else (None,) + tuple(outs)


def _tap_valid(t, o, lc, n):
    tt = t + o
    in_ctx = t < lc
    return (tt >= jnp.where(in_ctx, 0, lc)) & (tt < jnp.where(in_ctx, lc, n))


def _conv_fwd(proj_all, cw, cb, lc, W, name):
    n = proj_all.shape[0]
    cbk = _tile(W, CB_SEQ, LANE)
    tr = _tile(n, TR_CONV, SUB16)
    ext = tr + 2 * SUB

    def body(x_ref, w_ref, b_ref, u_ref, xp_ref):
        xp_ref[0:SUB, :] = jnp.zeros((SUB, cbk), F32)
        xp_ref[n + SUB:n + 2 * SUB, :] = jnp.zeros((SUB, cbk), F32)
        xp_ref[SUB:n + SUB, :] = x_ref[...]
        w = w_ref[...]
        bias = b_ref[...]

        def chunk(ci, carry):
            r0 = pl.multiple_of(ci * tr, SUB16)
            xe = xp_ref[pl.ds(r0, ext), :]
            t = r0 + lax.broadcasted_iota(jnp.int32, (tr, cbk), 0)
            acc = jnp.broadcast_to(bias, (tr, cbk))
            for k in range(4):
                o = k - 1
                sh = xe if o == 0 else pltpu.roll(xe, (-o) % ext, 0)
                acc = acc + jnp.where(_tap_valid(t, o, lc, n), sh[SUB:tr + SUB], 0.0) * w[k:k + 1]
            u_ref[pl.ds(r0, tr), :] = acc
            return carry

        lax.fori_loop(0, n // tr, chunk, 0)

    blocks = [((n, cbk), F32)] * 2 + [((4, cbk), F32), ((1, cbk), F32)]
    scratch = [((n + 2 * SUB, cbk), F32)]
    return pl.pallas_call(
        body, name=name, grid=(W // cbk,),
        in_specs=[pl.BlockSpec((n, cbk), lambda j: (0, j)), pl.BlockSpec((4, cbk), lambda j: (0, j)),
                  pl.BlockSpec((1, cbk), lambda j: (0, j))],
        out_specs=pl.BlockSpec((n, cbk), lambda j: (0, j)),
        out_shape=_sds((n, W), F32),
        scratch_shapes=[pltpu.VMEM(s, d) for s, d in scratch],
        compiler_params=_params(blocks, scratch, ("parallel",)),
    )(proj_all, cw, cb)


def _conv_bwd(du_all, proj_all, cw, dproj, lc, W, name):
    n = du_all.shape[0]
    cbk = _tile(W, CB_SEQ, LANE)
    tr = _tile(n, TR_CONV, SUB16)
    ext = tr + 2 * SUB

    def body(du_ref, x_ref, w_ref, dp_in, dx_ref, gw_ref, gb_ref, dp_ref, xp_ref):
        del dp_in
        for ref, src in ((dp_ref, du_ref), (xp_ref, x_ref)):
            ref[0:SUB, :] = jnp.zeros((SUB, cbk), F32)
            ref[n + SUB:n + 2 * SUB, :] = jnp.zeros((SUB, cbk), F32)
            ref[SUB:n + SUB, :] = src[...]
        w = w_ref[...]

        def fold(v):
            return jnp.sum(v.reshape(tr // SUB, SUB, cbk), axis=0)

        def chunk(ci, carry):
            r0 = pl.multiple_of(ci * tr, SUB16)
            de = dp_ref[pl.ds(r0, ext), :]
            xe = xp_ref[pl.ds(r0, ext), :]
            t = r0 + lax.broadcasted_iota(jnp.int32, (tr, cbk), 0)
            d0 = de[SUB:tr + SUB]
            dx = jnp.zeros((tr, cbk), F32)
            new = []
            for k in range(4):
                o = k - 1
                dsh = de if o == 0 else pltpu.roll(de, o % ext, 0)
                dx = dx + jnp.where(_tap_valid(t, -o, lc, n), dsh[SUB:tr + SUB], 0.0) * w[k:k + 1]
                xsh = xe if o == 0 else pltpu.roll(xe, (-o) % ext, 0)
                new.append(carry[k] + fold(d0 * jnp.where(_tap_valid(t, o, lc, n), xsh[SUB:tr + SUB], 0.0)))
            new.append(carry[4] + fold(d0))
            dx_ref[pl.ds(r0, tr), :] = dx.astype(dx_ref.dtype)
            return tuple(new)

        zero = jnp.zeros((SUB, cbk), F32)
        acc = lax.fori_loop(0, n // tr, chunk, (zero,) * 5)
        for k in range(4):
            gw_ref[k:k + 1, :] = jnp.sum(acc[k], axis=0, keepdims=True)
        gb_ref[...] = jnp.sum(acc[4], axis=0, keepdims=True)

    col = pl.BlockSpec((n, cbk), lambda j: (0, j))
    blocks = [((n, cbk), F32)] * 2 + [((n, cbk), MXU), ((4, cbk), F32), ((4, cbk), F32), ((1, cbk), F32)]
    scratch = [((n + 2 * SUB, cbk), F32)] * 2
    return pl.pallas_call(
        body, name=name, grid=(W // cbk,),
        in_specs=[col, col, pl.BlockSpec((4, cbk), lambda j: (0, j)), ANY],
        out_specs=[col, pl.BlockSpec((4, cbk), lambda j: (0, j)), pl.BlockSpec((1, cbk), lambda j: (0, j))],
        out_shape=[_sds(dproj.shape, dproj.dtype), _sds((4, W), F32), _sds((1, W), F32)],
        input_output_aliases={3: 0},
        scratch_shapes=[pltpu.VMEM(s, d) for s, d in scratch],
        compiler_params=_params(blocks, scratch, ("parallel",)),
    )(du_all, proj_all, cw, dproj)


def _gate_coeffs(ub, u, d, wr_ref, wi_ref, br_ref, bi_ref, lam_ref):
    c = -LRU_C * _softplus(-lam_ref[d:d + 1, :])
    r = _sigmoid(lax.dot_general(ub, wr_ref[d], NN, preferred_element_type=F32) + br_ref[d:d + 1, :])
    ig = _sigmoid(lax.dot_general(ub, wi_ref[d], NN, preferred_element_type=F32) + bi_ref[d:d + 1, :])
    la = c * r
    a = jnp.exp(la)
    sq = jnp.sqrt(-jnp.tanh(la) * (1.0 + a * a))
    return c, r, ig, a, sq


def _gate_specs(tl, hd):
    w_spec = pl.BlockSpec((2, None, hd, hd), lambda h, i: (0, h, 0, 0))
    v_spec = pl.BlockSpec((2, hd), lambda h, i: (0, h))
    return w_spec, v_spec


def _gates_fwd(u_all, wr, wi, br, bi, lam, name):
    n, W = u_all.shape
    heads, hd = wr.shape[1], wr.shape[2]
    tl = _tile(n, TL, SUB16)

    def body(u_ref, wr_ref, wi_ref, br_ref, bi_ref, lam_ref, a_ref, b_ref):
        u = u_ref[...]
        ub = u.astype(MXU)
        for d in range(2):
            _, _, ig, a, sq = _gate_coeffs(ub, u, d, wr_ref, wi_ref, br_ref, bi_ref, lam_ref)
            a_ref[d] = a
            b_ref[d] = sq * (ig * u)

    w_spec, v_spec = _gate_specs(tl, hd)
    o_spec = pl.BlockSpec((2, tl, hd), lambda h, i: (0, i, h))
    blocks = [((tl, hd), F32), ((2, hd, hd), MXU), ((2, hd, hd), MXU)] + [((2, hd), F32)] * 3 + [((2, tl, hd), F32)] * 2
    return pl.pallas_call(
        body, name=name, grid=(heads, n // tl),
        in_specs=[pl.BlockSpec((tl, hd), lambda h, i: (i, h)), w_spec, w_spec, v_spec, v_spec, v_spec],
        out_specs=[o_spec, o_spec], out_shape=[_sds((2, n, W), F32)] * 2,
        compiler_params=_params(blocks, dims=("parallel", "parallel")),
    )(u_all, wr, wi, br, bi, lam)


def _gates_bwd(u_all, da, db, wr, wi, br, bi, lam, name):
    n, W = u_all.shape
    heads, hd = wr.shape[1], wr.shape[2]
    tl = _tile(n, TL, SUB16)
    ni = n // tl

    def body(u_ref, da_ref, db_ref, wr_ref, wi_ref, br_ref, bi_ref, lam_ref,
             du_ref, gwr_ref, gwi_ref, gbr_ref, gbi_ref, gc_ref, accr_ref, acci_ref):
        i = pl.program_id(1)

        @pl.when(i == 0)
        def _():
            accr_ref[...] = jnp.zeros_like(accr_ref)
            acci_ref[...] = jnp.zeros_like(acci_ref)
            gbr_ref[...] = jnp.zeros_like(gbr_ref)
            gbi_ref[...] = jnp.zeros_like(gbi_ref)
            gc_ref[...] = jnp.zeros_like(gc_ref)

        u = u_ref[...]
        ub = u.astype(MXU)
        du = jnp.zeros_like(u)
        for d in range(2):
            c, r, ig, a, sq = _gate_coeffs(ub, u, d, wr_ref, wi_ref, br_ref, bi_ref, lam_ref)
            dbv = db_ref[d]
            t = dbv * sq
            du = du + t * ig
            d_la = da_ref[d] * a - (dbv * ig * u) * (a * a) / sq
            gc_ref[d:d + 1, :] += jnp.sum(d_la * r, axis=0, keepdims=True)
            d_pr = (d_la * c) * (r * (1.0 - r))
            d_pi = (t * u) * (ig * (1.0 - ig))
            gbr_ref[d:d + 1, :] += jnp.sum(d_pr, axis=0, keepdims=True)
            gbi_ref[d:d + 1, :] += jnp.sum(d_pi, axis=0, keepdims=True)
            pb = d_pr.astype(MXU)
            qb = d_pi.astype(MXU)
            du = du + lax.dot_general(pb, wr_ref[d], NT, preferred_element_type=F32)
            du = du + lax.dot_general(qb, wi_ref[d], NT, preferred_element_type=F32)
            accr_ref[d] += lax.dot_general(ub, pb, TN_DIMS, preferred_element_type=F32)
            acci_ref[d] += lax.dot_general(ub, qb, TN_DIMS, preferred_element_type=F32)
        du_ref[...] = du

        @pl.when(i == ni - 1)
        def _():
            gwr_ref[...] = accr_ref[...].astype(gwr_ref.dtype)
            gwi_ref[...] = acci_ref[...].astype(gwi_ref.dtype)

    w_spec, v_spec = _gate_specs(tl, hd)
    u_spec = pl.BlockSpec((tl, hd), lambda h, i: (i, h))
    ab_spec = pl.BlockSpec((2, tl, hd), lambda h, i: (0, i, h))
    blocks = ([((tl, hd), F32)] * 2 + [((2, tl, hd), F32)] * 2 + [((2, hd, hd), MXU)] * 4 + [((2, hd), F32)] * 6)
    scratch = [((2, hd, hd), F32)] * 2
    return pl.pallas_call(
        body, name=name, grid=(heads, ni),
        in_specs=[u_spec, ab_spec, ab_spec, w_spec, w_spec, v_spec, v_spec, v_spec],
        out_specs=[u_spec, w_spec, w_spec, v_spec, v_spec, v_spec],
        out_shape=[_sds((n, W), F32), _sds(wr.shape, MXU), _sds(wi.shape, MXU)] + [_sds((2, W), F32)] * 3,
        scratch_shapes=[pltpu.VMEM(s, d) for s, d in scratch],
        compiler_params=_params(blocks, scratch, ("parallel", "arbitrary")),
    )(u_all, da, db, wr, wi, br, bi, lam)


def _tile_scan(A, B, rows, reverse):
    for s in (1, 2, 4):
        if reverse:
            As, Bs, m = pltpu.roll(A, SUB - s, 0), pltpu.roll(B, SUB - s, 0), rows < SUB - s
        else:
            As, Bs, m = pltpu.roll(A, s, 0), pltpu.roll(B, s, 0), rows >= s
        B = jnp.where(m, A * Bs + B, B)
        A = jnp.where(m, A * As, A)
    return A, B


def _scan_chunks(n, lc):
    tc = _tile(lc, TL, SUB)
    assert n % tc == 0 and lc % tc == 0
    return tc, n // tc, lc // tc


def _scan_fwd(a_all, b_all, lc, name):
    _, n, W = a_all.shape
    cb = _tile(W, CB_SCAN, LANE)
    tc, nch, ncc = _scan_chunks(n, lc)
    ntile = tc // SUB

    def chunk(d, t):
        return jnp.where(d == 0, t, jnp.where(t < ncc, ncc - 1 - t, nch - 1 - (t - ncc)))

    def body(a_ref, b_ref, h_ref, carry_ref):
        rows = lax.broadcasted_iota(jnp.int32, (SUB, cb), 0)

        @pl.when(pl.program_id(2) == 0)
        def _():
            carry_ref[...] = jnp.zeros_like(carry_ref)

        def run(reverse):
            def step(i, h):
                r = pl.multiple_of(((ntile - 1 - i) if reverse else i) * SUB, SUB)
                A, B = _tile_scan(a_ref[pl.ds(r, SUB), :], b_ref[pl.ds(r, SUB), :], rows, reverse)
                H = A * h + B
                h_ref[pl.ds(r, SUB), :] = H
                return H[0:1, :] if reverse else H[SUB - 1:SUB, :]

            carry_ref[...] = lax.fori_loop(0, ntile, step, carry_ref[...], unroll=2)

        @pl.when(pl.program_id(1) == 0)
        def _():
            run(False)

        @pl.when(pl.program_id(1) == 1)
        def _():
            run(True)

    spec = pl.BlockSpec((None, tc, cb), lambda j, d, t: (d, chunk(d, t), j))
    return pl.pallas_call(
        body, name=name, grid=(W // cb, 2, nch), in_specs=[spec, spec], out_specs=spec,
        out_shape=_sds((2, n, W), F32), scratch_shapes=[pltpu.VMEM((1, cb), F32)],
        compiler_params=_params([((tc, cb), F32)] * 3, [((1, cb), F32)], ("parallel", "arbitrary", "arbitrary")),
    )(a_all, b_all)


def _scan_bwd(a_all, h_all, dya, lc, name):
    _, n, W = a_all.shape
    cb = _tile(W, CB_SCAN, LANE)
    tc, nch, ncc = _scan_chunks(n, lc)
    ntile = tc // SUB
    nl = nch - ncc

    def chunk(d, t):
        return jnp.where(d == 0, nch - 1 - t, jnp.where(t < nl, ncc + t, t - nl))

    def neighbour(d, t):
        c = chunk(d, t)
        below = jnp.maximum(c * ntile - 1, 0)
        above = jnp.where(c == nch - 1, 0, jnp.minimum((c + 1) * ntile, nch * ntile - 1))
        return jnp.where(d == 0, below, above)

    def body(a_ref, h_ref, hn_ref, g_ref, da_ref, db_ref, mu_ref):
        rows = lax.broadcasted_iota(jnp.int32, (SUB, cb), 0)
        d, t = pl.program_id(1), pl.program_id(2)
        c = chunk(d, t)
        has_g = c >= ncc

        @pl.when(t == 0)
        def _():
            mu_ref[...] = jnp.zeros_like(mu_ref)

        def tile(ref, j):
            return ref[pl.ds(pl.multiple_of(j * SUB, SUB), SUB), :]

        def run(up):
            if up:
                edge = jnp.where(c == ncc - 1, 0.0, hn_ref[0:1, :])
            else:
                edge = jnp.where(c > 0, hn_ref[SUB - 1:SUB, :], 0.0)

            def step(i, mu):
                j = i if up else ntile - 1 - i
                a_t = tile(a_ref, j)
                g_t = jnp.where(has_g, tile(g_ref, j), 0.0)
                if up:
                    ap = jnp.where(rows >= 1, pltpu.roll(a_t, 1, 0), 1.0)
                    nb_row = jnp.where(j < ntile - 1, tile(h_ref, jnp.minimum(j + 1, ntile - 1))[0:1, :], edge)
                    hprev = jnp.where(rows < SUB - 1, pltpu.roll(tile(h_ref, j), SUB - 1, 0), nb_row)
                else:
                    ap = jnp.where(rows < SUB - 1, pltpu.roll(a_t, SUB - 1, 0), 1.0)
                    nb_row = jnp.where(j > 0, tile(h_ref, jnp.maximum(j - 1, 0))[SUB - 1:SUB, :], edge)
                    hprev = jnp.where(rows >= 1, pltpu.roll(tile(h_ref, j), 1, 0), nb_row)
                A, B = _tile_scan(ap, g_t, rows, not up)
                lam = A * mu + B
                r = pl.multiple_of(j * SUB, SUB)
                da_ref[pl.ds(r, SUB), :] = lam * hprev
                db_ref[pl.ds(r, SUB), :] = lam
                return a_t[SUB - 1:SUB, :] * lam[SUB - 1:SUB, :] if up else a_t[0:1, :] * lam[0:1, :]

            mu_ref[...] = lax.fori_loop(0, ntile, step, mu_ref[...], unroll=2)

        @pl.when(d == 0)
        def _():
            run(False)

        @pl.when(d == 1)
        def _():
            run(True)

    spec = pl.BlockSpec((None, tc, cb), lambda j, d, t: (d, chunk(d, t), j))
    n_spec = pl.BlockSpec((None, SUB, cb), lambda j, d, t: (d, neighbour(d, t), j))
    g_spec = pl.BlockSpec((tc, cb), lambda j, d, t: (jnp.maximum(chunk(d, t) - ncc, 0), j))
    blocks = [((tc, cb), F32)] * 5 + [((SUB, cb), F32)]
    return pl.pallas_call(
        body, name=name, grid=(W // cb, 2, nch), in_specs=[spec, spec, n_spec, g_spec], out_specs=[spec, spec],
        out_shape=[_sds((2, n, W), F32)] * 2, scratch_shapes=[pltpu.VMEM((1, cb), F32)],
        compiler_params=_params(blocks, [((1, cb), F32)], ("parallel", "arbitrary", "arbitrary")),
    )(a_all, h_all, h_all, dya)


def _pool_window(v, w, tl, cb, transpose):
    left = w // 2
    right = w - 1 - left
    pos = lax.broadcasted_iota(jnp.int32, (tl, cb), 0) % GRID_W
    cnt = (jnp.minimum(pos + right, GRID_W - 1) - jnp.maximum(pos - left, 0) + 1).astype(F32)
    src = v / cnt if transpose else v
    lo, hi = (-right, left) if transpose else (-left, right)
    acc = src
    for o in range(lo, hi + 1):
        if o != 0:
            ok = (pos + o >= 0) & (pos + o < GRID_W)
            acc = acc + jnp.where(ok, pltpu.roll(src, (-o) % tl, 0), 0.0)
    return acc - v if transpose else acc / cnt - v


def _pool_z(src, row0, col0, L, W, transpose, dproj, name):
    G = len(POOL_WINDOWS)
    pd = W // G
    tl = _tile(L, TL, GRID_W)
    cb = _tile(pd, CB_SEQ, LANE)
    assert row0 % tl == 0 and col0 % cb == 0
    rb, cbk = row0 // tl, col0 // cb
    nj = pd // cb

    def body(x_ref, *rest):
        o_ref = rest[-1]
        for gi, w in enumerate(POOL_WINDOWS):
            @pl.when(pl.program_id(0) == gi)
            def _(w=w):
                o_ref[...] = _pool_window(x_ref[...], w, tl, cb, transpose).astype(o_ref.dtype)

    plain = pl.BlockSpec((tl, cb), lambda g, i, j: (i, g * nj + j))
    window = pl.BlockSpec((tl, cb), lambda g, i, j: (i + rb, cbk + g * nj + j))
    blocks = [((tl, cb), F32), ((tl, cb), MXU)]
    if transpose:
        return pl.pallas_call(
            body, name=name, grid=(G, L // tl, nj), in_specs=[plain, ANY], out_specs=window,
            out_shape=_sds(dproj.shape, dproj.dtype), input_output_aliases={1: 0},
            compiler_params=_params(blocks, dims=("parallel",) * 3),
        )(src, dproj)
    return pl.pallas_call(
        body, name=name, grid=(G, L // tl, nj), in_specs=[window], out_specs=plain,
        out_shape=_sds((L, W), MXU),
        compiler_params=_params(blocks, dims=("parallel",) * 3),
    )(src)


def _mix_fwd(hs, proj_all, ypre, b_pool, pool_scale, lc, name):
    L, W = ypre.shape
    tl = _tile(L, TL, SUB16)
    cb = _tile(W, CB_MIX, LANE)
    nj = W // cb
    assert lc % tl == 0
    rb = lc // tl

    def body(hs_ref, ga_ref, yp_ref, gb_ref, bp_ref, ps_ref, o_ref):
        p = pl.program_id(2)

        @pl.when(p == 0)
        def _():
            g = ga_ref[...]
            o_ref[...] = ((hs_ref[0] + hs_ref[1]) * (g * _sigmoid(g))).astype(o_ref.dtype)

        @pl.when(p == 1)
        def _():
            g = gb_ref[...]
            yb = (yp_ref[...] + bp_ref[...]) * ps_ref[...]
            o_ref[...] = (yb * (g * _sigmoid(g))).astype(o_ref.dtype)

    vec = pl.BlockSpec((1, cb), lambda i, j, p: (0, j))
    blocks = [((2, tl, cb), F32)] + [((tl, cb), F32)] * 3 + [((tl, cb), MXU)]
    return pl.pallas_call(
        body, name=name, grid=(L // tl, nj, 2),
        in_specs=[pl.BlockSpec((2, tl, cb), lambda i, j, p: (0, i + rb, j)),
                  pl.BlockSpec((tl, cb), lambda i, j, p: (i + rb, 2 * nj + j)),
                  pl.BlockSpec((tl, cb), lambda i, j, p: (i, j)),
                  pl.BlockSpec((tl, cb), lambda i, j, p: (i + rb, 3 * nj + j)), vec, vec],
        out_specs=pl.BlockSpec((tl, cb), lambda i, j, p: (i, p * nj + j)),
        out_shape=_sds((L, 2 * W), MXU),
        compiler_params=_params(blocks, dims=("parallel", "parallel", "arbitrary")),
    )(hs, proj_all, ypre, proj_all, b_pool, pool_scale)


def _dsilu(g, sg):
    return sg * (1.0 + g * (1.0 - sg))


def _mixa_bwd(dmixed, hs, proj_all, dproj, lc, W, name):
    L = dmixed.shape[0]
    tl = _tile(L, TL, SUB16)
    cb = _tile(W, CB_MIX, LANE)
    nj = W // cb
    rb = lc // tl

    def body(dm_ref, hs_ref, ga_ref, dp_in, dya_ref, dga_ref):
        del dp_in
        g = ga_ref[...]
        sg = _sigmoid(g)
        dm = dm_ref[...]
        dya_ref[...] = dm * (g * sg)
        dga_ref[...] = (dm * (hs_ref[0] + hs_ref[1]) * _dsilu(g, sg)).astype(dga_ref.dtype)

    blocks = [((tl, cb), F32)] * 3 + [((2, tl, cb), F32), ((tl, cb), MXU)]
    return pl.pallas_call(
        body, name=name, grid=(L // tl, nj),
        in_specs=[pl.BlockSpec((tl, cb), lambda i, j: (i, j)),
                  pl.BlockSpec((2, tl, cb), lambda i, j: (0, i + rb, j)),
                  pl.BlockSpec((tl, cb), lambda i, j: (i + rb, 2 * nj + j)), ANY],
        out_specs=[pl.BlockSpec((tl, cb), lambda i, j: (i, j)),
                   pl.BlockSpec((tl, cb), lambda i, j: (i + rb, 2 * nj + j))],
        out_shape=[_sds((L, W), F32), _sds(dproj.shape, dproj.dtype)],
        input_output_aliases={3: 1},
        compiler_params=_params(blocks, dims=("parallel", "parallel")),
    )(dmixed, hs, proj_all, dproj)


def _mixb_bwd(dmixed, ypre, proj_all, b_pool, pool_scale, dproj, lc, W, name):
    L = dmixed.shape[0]
    tl = _tile(L, TL, SUB16)
    cb = _tile(W, CB_MIX, LANE)
    nj = W // cb
    rb = lc // tl

    def body(dm_ref, yp_ref, gb_ref, bp_ref, ps_ref, dp_in, dyp_ref, dgb_ref, gbp_ref, gps_ref):
        del dp_in
        i = pl.program_id(1)

        @pl.when(i == 0)
        def _():
            gbp_ref[...] = jnp.zeros_like(gbp_ref)
            gps_ref[...] = jnp.zeros_like(gps_ref)

        g = gb_ref[...]
        sg = _sigmoid(g)
        dm = dm_ref[...]
        yp = yp_ref[...] + bp_ref[...]
        ps = ps_ref[...]
        dyb = dm * (g * sg)
        dyp = dyb * ps
        dgb_ref[...] = (dm * (yp * ps) * _dsilu(g, sg)).astype(dgb_ref.dtype)
        dyp_ref[...] = dyp.astype(dyp_ref.dtype)
        gbp_ref[...] += jnp.sum(dyp, axis=0, keepdims=True)
        gps_ref[...] += jnp.sum(dyb * yp, axis=0, keepdims=True)

    vec = pl.BlockSpec((1, cb), lambda j, i: (0, j))
    blocks = [((tl, cb), F32)] * 3 + [((tl, cb), MXU)] * 2 + [((1, cb), F32)] * 4
    return pl.pallas_call(
        body, name=name, grid=(nj, L // tl),
        in_specs=[pl.BlockSpec((tl, cb), lambda j, i: (i, nj + j)),
                  pl.BlockSpec((tl, cb), lambda j, i: (i, j)),
                  pl.BlockSpec((tl, cb), lambda j, i: (i + rb, 3 * nj + j)), vec, vec, ANY],
        out_specs=[pl.BlockSpec((tl, cb), lambda j, i: (i, j)),
                   pl.BlockSpec((tl, cb), lambda j, i: (i + rb, 3 * nj + j)), vec, vec],
        out_shape=[_sds((L, W), MXU), _sds(dproj.shape, dproj.dtype), _sds((1, W), F32), _sds((1, W), F32)],
        input_output_aliases={5: 1},
        compiler_params=_params(blocks, dims=("parallel", "arbitrary")),
    )(dmixed, ypre, proj_all, b_pool, pool_scale, dproj)


def _dproj_init(n, lc, W, name):
    cb = _tile(W, CB_MIX, LANE)
    nj = W // cb

    def body(o_ref):
        o_ref[...] = jnp.zeros_like(o_ref)

    return pl.pallas_call(
        body, name=name, grid=(3 * nj,), in_specs=[],
        out_specs=pl.BlockSpec((lc, cb), lambda j: (0, nj + j)),
        out_shape=_sds((n, 4 * W), MXU),
        compiler_params=_params([((lc, cb), MXU)], dims=("parallel",)),
    )()


def _final(x2, out, tgt, gate, gfin, name):
    L, D = x2.shape
    tl = _tile(L, TL_FINAL, SUB16)

    def body(x_ref, o_ref, t_ref, gate_ref, g_ref, dout_ref, dxn_ref, loss_ref, ggf_ref, dgate_ref):
        i = pl.program_id(0)

        @pl.when(i == 0)
        def _():
            loss_ref[...] = jnp.zeros_like(loss_ref)
            ggf_ref[...] = jnp.zeros_like(ggf_ref)
            dgate_ref[...] = jnp.zeros_like(dgate_ref)

        o = o_ref[...]
        gate_v = gate_ref[...]
        gv = g_ref[...]
        xn = x_ref[...] + gate_v * o
        s = lax.rsqrt(jnp.mean(xn * xn, axis=-1, keepdims=True) + EPS)
        xh = xn * s
        err = xh * gv - t_ref[...]
        tok = jnp.mean(err * err, axis=-1, keepdims=True)
        loss_ref[...] += 0.5 * jnp.sum(tok, axis=0, keepdims=True)
        dy = err / D
        ggf_ref[...] += jnp.sum(dy * xh, axis=0, keepdims=True)
        dxh = dy * gv
        dxn = s * (dxh - xh * jnp.mean(dxh * xh, axis=-1, keepdims=True))
        dgate_ref[...] += jnp.sum(dxn * o, axis=0, keepdims=True)
        dout_ref[...] = (gate_v * dxn).astype(dout_ref.dtype)
        dxn_ref[...] = dxn

    row = pl.BlockSpec((tl, D), lambda i: (i, 0))
    vec = pl.BlockSpec((1, D), lambda i: (0, 0))
    blocks = [((tl, D), F32)] * 4 + [((tl, D), MXU)] + [((1, D), F32)] * 4
    return pl.pallas_call(
        body, name=name, grid=(L // tl,), in_specs=[row, row, row, vec, vec],
        out_specs=[row, row, pl.BlockSpec((1, 1), lambda i: (0, 0)), vec, vec],
        out_shape=[_sds((L, D), MXU), _sds((L, D), F32), _sds((1, 1), F32), _sds((1, D), F32), _sds((1, D), F32)],
        compiler_params=_params(blocks, dims=("arbitrary",)),
    )(x2, out, tgt, gate, gfin)


def _adamw_parts(w2, parts, m2, v2, name):
    R, C = w2.shape
    nh = len(parts)
    ch = C // nh
    tr = _tile(R, max(SUB16, (512 * 1024) // (ch * (nh + 1))), SUB16)

    def body(w_ref, *rest):
        p_refs = rest[:nh]
        m_ref, v_ref, g_ref, d_ref, nm_ref, nv_ref = rest[nh:]
        for q in range(nh):
            @pl.when(pl.program_id(1) == q)
            def _(p_ref=p_refs[q]):
                g = p_ref[0].astype(F32)
                for p in range(1, NDEV):
                    g = g + p_ref[p].astype(F32)
                delta, nm, nv = _adam(w_ref[...], g, m_ref[...], v_ref[...])
                g_ref[...] = g
                d_ref[...] = delta
                nm_ref[...] = nm
                nv_ref[...] = nv

    blk = pl.BlockSpec((tr, ch), lambda i, h: (i, h))
    p_spec = pl.BlockSpec((NDEV, tr, ch), lambda i, h: (0, i, 0))
    blocks = [((tr, ch), F32)] * 7 + [((NDEV, tr, ch), parts[0].dtype)] * nh
    return pl.pallas_call(
        body, name=name, grid=(R // tr, nh),
        in_specs=[blk] + [p_spec] * nh + [blk, blk],
        out_specs=[blk] * 4, out_shape=[_sds((R, C), F32)] * 4,
        compiler_params=_params(blocks, dims=("parallel", "arbitrary")),
    )(w2, *parts, m2, v2)


def _small_sum(vs, ga, gc, name):
    ns, nm = vs.shape[1], ga.shape[1]

    def body(v_ref, ga_ref, gc_ref, tot_ref, gb_ref):
        tot = v_ref[0:1, :]
        gb = ga_ref[0:1, :]
        for p in range(1, NDEV):
            tot = tot + v_ref[p:p + 1, :]
            gb = gb + ga_ref[p:p + 1, :]
        for p in range(NDEV):
            gb = gb + gc_ref[p:p + 1, :]
        tot_ref[...] = tot
        gb_ref[...] = gb

    blocks = [((NDEV, ns), F32), ((NDEV, nm), F32), ((NDEV, nm), F32), ((1, ns), F32), ((1, nm), F32)]
    return pl.pallas_call(
        body, name=name, out_shape=[_sds((1, ns), F32), _sds((1, nm), F32)],
        compiler_params=_params(blocks),
    )(vs, ga, gc)


def _adamw_small(g_raw, w, m, v, lam_range, cctx_range, name):
    npk = w.shape[1]

    def body(g_ref, w_ref, m_ref, v_ref, go_ref, d_ref, nm_ref, nv_ref):
        wv = w_ref[...]
        g = g_ref[...]
        idx = lax.broadcasted_iota(jnp.int32, (1, npk), 1)
        in_lam = (idx >= lam_range[0]) & (idx < lam_range[1])
        in_cc = (idx >= cctx_range[0]) & (idx < cctx_range[1])
        sg = _sigmoid_small(wv)
        g = jnp.where(in_lam, g * (LRU_C * _sigmoid_small(-wv)), jnp.where(in_cc, g * _dsilu(wv, sg), g))
        delta, nm, nv = _adam(wv, g, m_ref[...], v_ref[...])
        go_ref[...] = g
        d_ref[...] = delta
        nm_ref[...] = nm
        nv_ref[...] = nv

    return pl.pallas_call(
        body, name=name, out_shape=[_sds((1, npk), F32)] * 4,
        compiler_params=_params([((1, npk), F32)] * 8),
    )(g_raw, w, m, v)


def _pack(pieces):
    return jnp.concatenate([p.reshape(1, -1) for p in pieces], axis=1)


def kernel(x, c, ctx, c_ctx, w_ada, b_ada, g_norm, w_in, conv_w, conv_b, lru_lambda, w_rgate, b_rgate, w_igate, b_igate, w_pool, b_pool, pool_scale, w_out, g_final, loss_target, m_c_ctx, m_w_ada, m_b_ada, m_g_norm, m_w_in, m_conv_w, m_conv_b, m_lru_lambda, m_w_rgate, m_b_rgate, m_w_igate, m_b_igate, m_w_pool, m_b_pool, m_pool_scale, m_w_out, m_g_final, v_c_ctx, v_w_ada, v_b_ada, v_g_norm, v_w_in, v_conv_w, v_conv_b, v_lru_lambda, v_w_rgate, v_b_rgate, v_w_igate, v_b_igate, v_w_pool, v_b_pool, v_pool_scale, v_w_out, v_g_final):
    L, D = x.shape[1], x.shape[2]
    lc = ctx.shape[1]
    n = lc + L
    W = conv_b.shape[1]
    heads, hd = w_rgate.shape[2], w_rgate.shape[4]
    G, pd = w_pool.shape[1], w_pool.shape[3]
    na = w_ada.shape[2]
    nb = w_in.shape[2]
    ws = W // NDEV
    me = 4 * lax.axis_index("x") + 2 * lax.axis_index("y") + lax.axis_index("c")

    (win_all, cw_all, lam_all, br_all, bi_all, c_all) = _all_gather(
        [w_in[0].astype(MXU), conv_w[0], lru_lambda[0], b_rgate[0], b_igate[0], c], "gather_w_in")
    cw = cw_all.transpose(1, 0, 2).reshape(4, W)
    lam = lam_all.transpose(1, 0, 2).reshape(2, W)
    br = br_all.transpose(1, 0, 2).reshape(2, W)
    bi = bi_all.transpose(1, 0, 2).reshape(2, W)

    cc = jnp.concatenate([c_all.reshape(NDEV, D), c_ctx.reshape(1, D), jnp.zeros((NDEV - 1, D), F32)], axis=0)
    b_loc = lax.dynamic_slice(b_ada, (0, me * na), (1, na))
    mod_loc, s_all = _ada_fwd(cc, w_ada[0], b_loc, "ada_fwd")
    (mod_all,) = _all_gather([mod_loc], "gather_mod")
    gate_w = [w_rgate[0].astype(MXU), w_igate[0].astype(MXU)]
    rest_w = [w_pool[0].astype(MXU), w_out[0].astype(MXU)]
    sent_gw = _send_start(gate_w, _place(gate_w, False, "place_gate_w", [mod_all]), False, "start_gate_w")
    sent_rw = _send_start(rest_w, _place(rest_w, False, "place_rest_w", [sent_gw[4]]), False, "start_rest_w")
    mod = mod_all.transpose(1, 0, 2).reshape(2 * NDEV, NDEV * na)
    mod_me = lax.dynamic_slice(mod, (me, 0), (1, 3 * D))
    shift, scale, gate = mod_me[:, :D], mod_me[:, D:2 * D], mod_me[:, 2 * D:]
    shift = _tie(shift, [sent_gw[4], sent_rw[4]], "tie_weights")
    shift_c, scale_c = mod[NDEV:NDEV + 1, :D], mod[NDEV:NDEV + 1, D:2 * D]

    x2, ctx2, tgt = x[0], ctx[0], loss_target[0]
    gfin = g_final.reshape(1, D)
    h_all = _norm_mod(x2, g_norm, shift, scale, n, lc, None, "norm_lat")
    h_all = _norm_mod(ctx2, g_norm, shift_c, scale_c, n, 0, h_all, "norm_ctx")
    proj_all = _mm_proj(h_all, win_all, "mm_proj")
    u_all = _conv_fwd(proj_all, cw, conv_b, lc, W, "conv_fwd")
    wr_all, wi_all = _send_wait(sent_gw, u_all, False, "wait_gate_w")
    wr = wr_all.transpose(1, 2, 0, 3, 4).reshape(2, heads, hd, hd)
    wi = wi_all.transpose(1, 2, 0, 3, 4).reshape(2, heads, hd, hd)
    a_all, b_all = _gates_fwd(u_all, wr, wi, br, bi, lam, "gates_fwd")
    hs = _scan_fwd(a_all, b_all, lc, "scan_fwd")
    z = _pool_z(proj_all, lc, W, L, W, False, None, "pool_z")
    wpool_all, wout_all = _send_wait(sent_rw, z, False, "wait_rest_w")
    wpool = wpool_all.transpose(1, 0, 2, 3).reshape(G, pd, pd)
    wout = wout_all.reshape(2 * W, D)
    ypre = _mm_group(z, wpool, "fwd", F32, "mm_pool")
    mixed = _mix_fwd(hs, proj_all, ypre, b_pool, pool_scale, lc, "mix_fwd")
    out = _mm_plain(mixed, wout, NN, F32, "mm_out")
    d_out, dxn, loss_p, ggf, dgate = _final(x2, out, tgt, gate, gfin, "final")

    dmixed = _mm_plain(d_out, wout, NT, F32, "mm_dmixed")
    gwout = _mm_plain(mixed, d_out, TN_DIMS, MXU, "mm_gwout")
    ex_o = [gwout.reshape(NDEV, 2 * W // NDEV, D)]
    sent_o = _send_start(ex_o, _place(ex_o, True, "place_gwout"), True, "start_gwout")
    dproj = _dproj_init(n, lc, W, "dproj_init")
    dya, dproj = _mixa_bwd(dmixed, hs, proj_all, dproj, lc, W, "mixa_bwd")
    dypre, dproj, gbp, gps = _mixb_bwd(dmixed, ypre, proj_all, _tie(b_pool, [sent_o[4]], "tie_gwout"), pool_scale,
                                       dproj, lc, W, "mixb_bwd")
    dz = _mm_group(dypre, wpool, "bwd", F32, "mm_dz")
    gwpool = _mm_group(z, dypre, "wgrad", MXU, "mm_gwpool")
    dproj = _pool_z(dz, lc, W, L, W, True, dproj, "pool_z_bwd")
    da, db = _scan_bwd(a_all, hs, dya, lc, "scan_bwd")
    du, gwr, gwi, gbr, gbi, gcl = _gates_bwd(u_all, da, db, wr, wi, br, bi, lam, "gates_bwd")
    ex_s = [gwpool.reshape(G, NDEV, pd // NDEV, pd).transpose(1, 0, 2, 3),
            gwr.reshape(2, heads, NDEV, hd // NDEV, hd).transpose(2, 0, 1, 3, 4),
            gwi.reshape(2, heads, NDEV, hd // NDEV, hd).transpose(2, 0, 1, 3, 4)]
    sent_s = _send_start(ex_s, _place(ex_s, True, "place_gsmall"), True, "start_gsmall")
    dproj, gcw, gcb = _conv_bwd(du, proj_all, _tie(cw, [sent_s[4]], "tie_gsmall"), dproj, lc, W, "conv_bwd")
    h_t = _transpose(h_all, "transpose_h")
    sent_i, tok = [], None
    for q in range(GWIN_PARTS):
        part = _mm_gwin(h_t, dproj, nb, q, GWIN_PARTS, f"mm_gwin_{q}", dep=tok)
        part = pltpu.with_memory_space_constraint(part, pltpu.HBM)
        sent_i.append(_send_start([part], _place([part], True, f"place_gwin_{q}"), True, f"start_gwin_{q}"))
        tok = sent_i[-1][4]
    dh_all = _mm_dh(dproj, win_all, "mm_dh", dep=tok)
    grad_x, dshift, dscale, ggn = _norm_bwd(x2, dh_all, lc, g_norm, scale, dxn, jnp.zeros((1, D), F32), "norm_bwd_lat")
    _, dshift_c, dscale_c, ggn = _norm_bwd(ctx2, dh_all, 0, g_norm, scale_c, None, ggn, "norm_bwd_ctx")

    dmod_me = jnp.concatenate([dshift, dscale, dgate], axis=1)
    dmod_c = jnp.concatenate([dshift_c, dscale_c, jnp.zeros((1, D), F32)], axis=1)
    smalls = [ggf, ggn, gcw, gcb, gcl, gbr, gbi, gbp, gps, jnp.pad(loss_p, ((0, 0), (0, LANE - 1)))]
    sizes = [s.size for s in smalls]
    small_all, dmod_all, dmodc_all = _all_gather([_pack(smalls), dmod_me, dmod_c], "gather_small")
    ga = lax.dynamic_slice(dmod_all.reshape(NDEV, 3 * D), (0, me * na), (NDEV, na))
    gc = lax.dynamic_slice(dmodc_all.reshape(NDEV, 3 * D), (0, me * na), (NDEV, na))
    g_wada, d_wada, nm_wada, nv_wada, pc = _ada_bwd(s_all, ga, gc, w_ada[0], m_w_ada[0], v_w_ada[0], "ada_bwd")
    (pc_all,) = _all_gather([pc[0:1]], "gather_cctx")
    tot, gb_ada = _small_sum(
        jnp.concatenate([small_all.reshape(NDEV, -1), pc_all.reshape(NDEV, D)], axis=1),
        dmod_all.reshape(NDEV, 3 * D), dmodc_all.reshape(NDEV, 3 * D), "small_sum")
    offs = [0]
    for s in sizes + [D]:
        offs.append(offs[-1] + s)
    t_ggf, t_ggn, t_gcw, t_gcb, t_gcl, t_gbr, t_gbi, t_gbp, t_gps, t_loss, t_pc = [
        tot[:, offs[i]:offs[i + 1]] for i in range(len(offs) - 1)]

    def shard(t, rows):
        return lax.dynamic_slice(t.reshape(rows, W), (0, me * ws), (rows, ws))

    def big(wv, parts, mv, vv, name):
        shp = wv.shape
        C = shp[-1]
        if not isinstance(parts, list):
            parts = [parts]
        parts = [p.reshape(NDEV, -1, C // len(parts)) for p in parts]
        outs = _adamw_parts(wv.reshape(-1, C), parts, mv.reshape(-1, C), vv.reshape(-1, C), name)
        return [o.reshape(shp) for o in outs]

    (recv_o,) = _send_wait(sent_o, tot, True, "wait_gwout")
    recv_p, recv_r, recv_i = _send_wait(sent_s, tot, True, "wait_gsmall")
    r_wout = big(w_out, recv_o, m_w_out, v_w_out, "adamw_w_out")
    r_wpool = big(w_pool, recv_p, m_w_pool, v_w_pool, "adamw_w_pool")
    r_wr = big(w_rgate, recv_r, m_w_rgate, v_w_rgate, "adamw_w_rgate")
    r_wi = big(w_igate, recv_i, m_w_igate, v_w_igate, "adamw_w_igate")
    r_wada = [o.reshape(w_ada.shape) for o in (g_wada, d_wada, nm_wada, nv_wada)]

    names = ["c_ctx", "b_ada", "g_norm", "conv_w", "conv_b", "lru_lambda", "b_rgate", "b_igate", "b_pool",
             "pool_scale", "g_final"]
    sw = [c_ctx, b_ada, g_norm, conv_w, conv_b, lru_lambda, b_rgate, b_igate, b_pool, pool_scale, g_final]
    sm = [m_c_ctx, m_b_ada, m_g_norm, m_conv_w, m_conv_b, m_lru_lambda, m_b_rgate, m_b_igate, m_b_pool,
          m_pool_scale, m_g_final]
    sv = [v_c_ctx, v_b_ada, v_g_norm, v_conv_w, v_conv_b, v_lru_lambda, v_b_rgate, v_b_igate, v_b_pool,
          v_pool_scale, v_g_final]
    sg = [t_pc, gb_ada, t_ggn, shard(t_gcw, 4), t_gcb, shard(t_gcl, 2), shard(t_gbr, 2), shard(t_gbi, 2), t_gbp,
          t_gps, t_ggf]
    poffs = [0]
    for wv in sw:
        poffs.append(poffs[-1] + wv.size)
    lam_range = (poffs[5], poffs[6])
    cctx_range = (poffs[0], poffs[1])
    small_out = _adamw_small(_pack(sg), _pack(sw), _pack(sm), _pack(sv), lam_range, cctx_range, "adamw_small")
    recv_w = [_send_wait(sent_i[q], small_out[0], True, f"wait_gwin_{q}")[0] for q in range(GWIN_PARTS)]
    r_win = big(w_in, recv_w, m_w_in, v_w_in, "adamw_w_in")
    r_small = {}
    for i, nm in enumerate(names):
        r_small[nm] = [o[:, poffs[i]:poffs[i + 1]].reshape(sw[i].shape) for o in small_out]

    res = dict(r_small)
    res.update(w_ada=r_wada, w_in=r_win, w_rgate=r_wr, w_igate=r_wi, w_pool=r_wpool, w_out=r_wout)
    order = ["c_ctx", "w_ada", "b_ada", "g_norm", "w_in", "conv_w", "conv_b", "lru_lambda", "w_rgate", "b_rgate",
             "w_igate", "b_igate", "w_pool", "b_pool", "pool_scale", "w_out", "g_final"]
    loss = t_loss[0, 0]
    outs = [loss, grad_x.reshape(x.shape)]
    for q in range(4):
        outs += [res[nm][q] for nm in order]
    return tuple(outs)
```

```python
import functools

import jax
import jax.numpy as jnp
from jax import lax
from jax.experimental import pallas as pl
from jax.experimental.pallas import tpu as pltpu

NDEV = 8
GRID_W = 64
POOL_WINDOWS = (2, 4, 8, 16)
LRU_C = 8.0
EPS = 1e-6
ADAM_LR = 0.001
ADAM_B1 = 0.9
ADAM_B2 = 0.999
ADAM_EPS = 1e-08
ADAM_WD = 0.01
ADAM_STEP = 10

F32 = jnp.float32
MXU = jnp.bfloat16

VMEM_BYTES = 64 * 1024 * 1024
VMEM_SLACK = 8 * 1024 * 1024
SUB = 8
SUB16 = 16
LANE = 128

TM = 1152
TN = 1024
TK = 2048
TL = 256
TL_FINAL = 128
CB_SEQ = 256
CB_SCAN = 1024
CB_MIX = 2048
TR_CONV = 576
GWIN_PARTS = 4
WIN_PARTS = 2

MESH_ID = pl.DeviceIdType.MESH


def _tile(n, pref, align):
    if n <= pref:
        return n
    for t in range(pref - pref % align, 0, -align):
        if n % t == 0:
            return t
    return n


def _nbytes(shape, dtype):
    n = 1
    for s in shape:
        if s is not None:
            n *= s
    return n * jnp.dtype(dtype).itemsize


def _params(blocks, scratch=(), dims=None):
    need = 2 * sum(_nbytes(s, d) for s, d in blocks) + sum(_nbytes(s, d) for s, d in scratch) + VMEM_SLACK
    kw = dict(vmem_limit_bytes=int(min(max(need, 2 * VMEM_SLACK), VMEM_BYTES - VMEM_SLACK // 2)))
    if dims is not None:
        kw["dimension_semantics"] = dims
    return pltpu.CompilerParams(**kw)


def _sds(shape, dtype):
    return jax.ShapeDtypeStruct(tuple(shape), dtype)


ANY = pl.BlockSpec(memory_space=pl.ANY)


def _ids():
    return lax.axis_index("x"), lax.axis_index("y"), lax.axis_index("c")


def _sigmoid(v):
    return 0.5 * jnp.tanh(0.5 * v) + 0.5


def _sigmoid_small(v):
    return jax.nn.sigmoid(v)


def _softplus(v):
    return jnp.maximum(v, 0.0) + jnp.log1p(jnp.exp(-jnp.abs(v)))


def _all_gather(xs, name):
    n = len(xs)

    def body(*refs):
        x_refs, o_refs = refs[:n], refs[n:2 * n]
        send_sems, recv_sems, local_sems = refs[2 * n:]
        x, y, c = _ids()
        me, sibling = (x, y, c), (x, y, 1 - c)
        chips = [(1 - x, y), (x, 1 - y), (1 - x, 1 - y)]

        def slot(a, p):
            return o_refs[a].at[4 * p[0] + 2 * p[1] + p[2]]

        def copy(a, k, block, to, src=None):
            return pltpu.make_async_remote_copy(
                src_ref=slot(a, block) if src is None else src, dst_ref=slot(a, block),
                send_sem=send_sems.at[7 * a + k], recv_sem=recv_sems.at[7 * a + k],
                device_id=to, device_id_type=MESH_ID)

        mine, first, passed = [], [], []
        for a in range(n):
            m = pltpu.make_async_copy(x_refs[a], slot(a, me), local_sems.at[a])
            m.start()
            mine.append(m)
            f = [copy(a, 0, me, sibling, src=x_refs[a])]
            f += [copy(a, 1 + j, me, (*chip, c), src=x_refs[a]) for j, chip in enumerate(chips)]
            for cp in f:
                cp.start()
            first += f
        for a in range(n):
            for j, chip in enumerate(chips):
                copy(a, 1 + j, (*chip, c), me).wait_recv()
                p = copy(a, 4 + j, (*chip, c), sibling)
                p.start()
                passed.append(p)
        for a in range(n):
            copy(a, 0, sibling, me).wait_recv()
            for j, chip in enumerate(chips):
                copy(a, 4 + j, (*chip, 1 - c), me).wait_recv()
        for cp in first + passed:
            cp.wait_send()
        for m in mine:
            m.wait()

    return pl.pallas_call(
        body, name=name,
        out_shape=[_sds((NDEV,) + v.shape, v.dtype) for v in xs],
        in_specs=[ANY] * n, out_specs=[ANY] * n,
        scratch_shapes=[pltpu.SemaphoreType.DMA((7 * n,)), pltpu.SemaphoreType.DMA((7 * n,)),
                        pltpu.SemaphoreType.DMA((n,))],
    )(*xs)


HBM = pl.BlockSpec(memory_space=pltpu.HBM)
SEM = pl.BlockSpec(memory_space=pltpu.SEMAPHORE)
EFFECT = pltpu.SideEffectType.DATAFLOW_SIDE_EFFECTING


def _peers():
    x, y, c = _ids()
    out = []
    for k in range(1, NDEV):
        px = 1 - x if k & 4 else x
        py = 1 - y if k & 2 else y
        pc = 1 - c if k & 1 else c
        out.append(((px, py, pc), 4 * px + 2 * py + pc))
    return out, 4 * x + 2 * y + c


def _tie(v, deps, name):
    def body(v_ref, *rest):
        rest[-1][...] = v_ref[...]

    vmem = pl.BlockSpec(memory_space=pltpu.VMEM)
    return pl.pallas_call(
        body, name=name, out_shape=_sds(v.shape, v.dtype), in_specs=[vmem] + [ANY] * len(deps), out_specs=vmem,
    )(v, *deps)


def _place(srcs, from_slot, name, deps=()):
    n = len(srcs)
    blks = [v.shape[1:] if from_slot else v.shape for v in srcs]

    nd = len(deps)

    def body(*refs):
        s_refs, l_refs = refs[:n], refs[n + nd:2 * n + nd]
        bufs, sems = refs[2 * n + nd:3 * n + nd], refs[3 * n + nd]
        x, y, c = _ids()
        me = 4 * x + 2 * y + c
        ins = [pltpu.make_async_copy(s_refs[a].at[me] if from_slot else s_refs[a], bufs[a], sems.at[a])
               for a in range(n)]
        outs = [pltpu.make_async_copy(bufs[a], l_refs[a].at[me], sems.at[n + a]) for a in range(n)]
        for cp in ins:
            cp.start()
        for a in range(n):
            ins[a].wait()
            outs[a].start()
        for cp in outs:
            cp.wait()

    scratch = [(b, v.dtype) for b, v in zip(blks, srcs)]
    return pl.pallas_call(
        body, name=name, out_shape=[_sds((NDEV,) + b, v.dtype) for b, v in zip(blks, srcs)],
        in_specs=[ANY] * (n + nd), out_specs=[ANY] * n,
        scratch_shapes=[pltpu.VMEM(b, d) for b, d in scratch] + [pltpu.SemaphoreType.DMA((2 * n,))],
        compiler_params=_params([], scratch),
    )(*srcs, *deps)


SEND_PEERS = {True: 7, False: 7, "level1": 4, "level2": 3}


def _send_copies(s_refs, l_refs, ssem, rsem, mode, receiving):
    peers, me = _peers()
    x, y, c = _ids()
    sibling = (x, y, 1 - c)
    chips = [(1 - x, y), (x, 1 - y), (1 - x, 1 - y)]
    npeer = SEND_PEERS[mode]
    out = []
    for a in range(len(l_refs)):
        if mode == "level2":
            for k, (px, py) in enumerate(chips):
                slot = 4 * px + 2 * py + (1 - c if receiving else c)
                out.append(pltpu.make_async_remote_copy(
                    src_ref=l_refs[a].at[slot], dst_ref=l_refs[a].at[slot], send_sem=ssem.at[npeer * a + k],
                    recv_sem=rsem.at[npeer * a + k], device_id=sibling, device_id_type=MESH_ID))
            continue
        targets = peers
        if mode == "level1":
            targets = [(sibling, 4 * x + 2 * y + 1 - c)] + [((px, py, c), 4 * px + 2 * py + c) for px, py in chips]
        for k, (dev, idx) in enumerate(targets):
            out.append(pltpu.make_async_remote_copy(
                src_ref=s_refs[a].at[idx] if mode is True else s_refs[a],
                dst_ref=l_refs[a].at[idx if receiving else me],
                send_sem=ssem.at[npeer * a + k], recv_sem=rsem.at[npeer * a + k], device_id=dev, device_id_type=MESH_ID))
    return out


def _send_start(srcs, lands, mode, name):
    ns, n = len(srcs), len(lands)
    nsem = SEND_PEERS[mode] * n

    def body(*refs):
        s_refs, l_refs = refs[:ns], refs[ns:ns + n]
        ssem, rsem = refs[ns + n], refs[ns + n + 1]
        token = refs[-1]
        for send in _send_copies(s_refs, l_refs, ssem, rsem, mode, False):
            send.start()
        token[...] = jnp.zeros_like(token)

    bufs = list(srcs) + list(lands)
    outs = pl.pallas_call(
        body, name=name,
        out_shape=[pltpu.SemaphoreType.DMA((nsem,)), pltpu.SemaphoreType.DMA((nsem,))]
        + [pltpu.HBM(v.shape, v.dtype) for v in bufs] + [_sds((SUB, LANE), F32)],
        in_specs=[HBM] * (ns + n), out_specs=[SEM, SEM] + [HBM] * (ns + n) + [pl.BlockSpec(memory_space=pltpu.VMEM)],
        input_output_aliases={i: 2 + i for i in range(ns + n)},
        compiler_params=pltpu.CompilerParams(has_side_effects=EFFECT),
    )(*[pltpu.with_memory_space_constraint(v, pltpu.HBM) for v in bufs])
    return outs[0], outs[1], list(outs[2:2 + ns]), list(outs[2 + ns:2 + ns + n]), outs[-1]


def _send_wait(started, after, mode, name):
    ssem, rsem, srcs, lands, _ = started
    ns, n = len(srcs), len(lands)

    def body(*refs):
        s_refs, l_refs = refs[:ns], refs[ns:ns + n]
        ssem_ref, rsem_ref = refs[ns + n], refs[ns + n + 1]
        for recv in _send_copies(s_refs, l_refs, ssem_ref, rsem_ref, mode, True):
            recv.wait_send()
            recv.wait_recv()

    bufs = list(srcs) + list(lands)
    outs = pl.pallas_call(
        body, name=name, out_shape=[pltpu.HBM(v.shape, v.dtype) for v in bufs],
        in_specs=[HBM] * (ns + n) + [SEM, SEM, ANY], out_specs=[HBM] * (ns + n),
        input_output_aliases={i: i for i in range(ns + n)},
        compiler_params=pltpu.CompilerParams(has_side_effects=EFFECT),
    )(*bufs, ssem, rsem, after)
    return list(outs[ns:])


NN = (((1,), (0,)), ((), ()))
NT = (((1,), (1,)), ((), ()))
TN_DIMS = (((0,), (0,)), ((), ()))


def _mm(a, b, *, grid, a_spec, b_spec, o_spec, out_shape, acc_shape, dims, name, dep=None, init=None, fill=None):
    k_axis = len(grid) - 1
    nk = grid[k_axis]
    extra = [v for v in (init, dep, fill) if v is not None]
    extra_specs = ([o_spec] if init is not None else []) + [ANY] * ((dep is not None) + (fill is not None))
    aliases = {} if fill is None else {1 + len(extra): 0}

    def body(a_ref, b_ref, *rest):
        o_ref, acc_ref = rest[-2], rest[-1]
        k = pl.program_id(k_axis)

        def prod():
            return lax.dot_general(a_ref[...], b_ref[...], dims, preferred_element_type=F32)

        def last(v):
            if init is not None:
                v = v + rest[0][...]
            o_ref[...] = v.astype(o_ref.dtype)

        if nk == 1:
            last(prod())
            return

        @pl.when(k == 0)
        def _():
            acc_ref[...] = prod()

        if nk > 2:
            @pl.when((k > 0) & (k < nk - 1))
            def _():
                acc_ref[...] += prod()

        @pl.when(k == nk - 1)
        def _():
            last(acc_ref[...] + prod())

    blocks = [(a_spec.block_shape, a.dtype), (b_spec.block_shape, b.dtype), (o_spec.block_shape, out_shape.dtype)]
    if init is not None:
        blocks.append((o_spec.block_shape, init.dtype))
    return pl.pallas_call(
        body, name=name, grid=grid, in_specs=[a_spec, b_spec] + extra_specs, out_specs=o_spec,
        out_shape=out_shape, scratch_shapes=[pltpu.VMEM(acc_shape, F32)], input_output_aliases=aliases,
        compiler_params=_params(blocks, [(acc_shape, F32)], ("parallel",) * k_axis + ("arbitrary",)),
    )(a, b, *extra)


def _mm_plain(a, b, dims, out_dtype, name):
    if dims == TN_DIMS:
        (K, M), N = a.shape, b.shape[1]
    elif dims == NT:
        (M, K), N = a.shape, b.shape[0]
    else:
        (M, K), N = a.shape, b.shape[1]
    tm, tn = _tile(M, TM, LANE), _tile(N, TN, LANE)
    tk = _tile(K, TK, LANE if dims != TN_DIMS else SUB16)
    if dims == TN_DIMS:
        a_spec = pl.BlockSpec((tk, tm), lambda i, j, k: (k, i))
    else:
        a_spec = pl.BlockSpec((tm, tk), lambda i, j, k: (i, k))
    if dims == NT:
        b_spec = pl.BlockSpec((tn, tk), lambda i, j, k: (j, k))
    else:
        b_spec = pl.BlockSpec((tk, tn), lambda i, j, k: (k, j))
    return _mm(a, b, grid=(M // tm, N // tn, K // tk), a_spec=a_spec, b_spec=b_spec,
               o_spec=pl.BlockSpec((tm, tn), lambda i, j, k: (i, j)),
               out_shape=_sds((M, N), out_dtype), acc_shape=(tm, tn), dims=dims, name=name)


def _mm_proj(h_all, win_q, q, nparts, fill, name):
    n, D = h_all.shape
    nbp = win_q.shape[2]
    nb = nbp * nparts
    tm, tn, tk = _tile(n, TM, SUB16), _tile(nbp, TN, LANE), _tile(D, TK, LANE)
    nbn = nbp // tn
    return _mm(h_all, win_q, grid=(n // tm, NDEV * nbn, D // tk),
               a_spec=pl.BlockSpec((tm, tk), lambda i, j, k: (i, k)),
               b_spec=pl.BlockSpec((None, tk, tn), lambda i, j, k: (j // nbn, k, j % nbn)),
               o_spec=pl.BlockSpec((tm, tn), lambda i, j, k: (i, (j // nbn) * (nb // tn) + q * nbn + j % nbn)),
               out_shape=_sds((n, NDEV * nb), F32), acc_shape=(tm, tn), dims=NN, name=name, fill=fill)


def _mm_dh(dproj, win_q, q, nparts, init, name, dep=None):
    n = dproj.shape[0]
    _, D, nbp = win_q.shape
    nb = nbp * nparts
    tm, tn, tk = _tile(n, TM, SUB16), _tile(D, TN, LANE), _tile(nbp, TK, LANE)
    nbk = nbp // tk
    return _mm(dproj, win_q, grid=(n // tm, D // tn, NDEV * nbk),
               a_spec=pl.BlockSpec((tm, tk), lambda i, j, k: (i, (k // nbk) * (nb // tk) + q * nbk + k % nbk)),
               b_spec=pl.BlockSpec((None, tn, tk), lambda i, j, k: (k // nbk, j, k % nbk)),
               o_spec=pl.BlockSpec((tm, tn), lambda i, j, k: (i, j)),
               out_shape=_sds((n, D), F32), acc_shape=(tm, tn), dims=NT, name=name, dep=dep, init=init)


def _transpose(x, name):
    R, C = x.shape
    tr, tc = _tile(R, TL, LANE), _tile(C, 2 * TL, LANE)

    def body(x_ref, o_ref):
        o_ref[...] = x_ref[...].T

    return pl.pallas_call(
        body, name=name, grid=(R // tr, C // tc),
        in_specs=[pl.BlockSpec((tr, tc), lambda i, j: (i, j))],
        out_specs=pl.BlockSpec((tc, tr), lambda i, j: (j, i)),
        out_shape=_sds((C, R), x.dtype),
        compiler_params=_params([((tr, tc), x.dtype)] * 2, dims=("parallel", "parallel")),
    )(x)


def _mm_gwin(h_t, dproj, nb, part, nparts, name, dep=None):
    D, n = h_t.shape
    nbp = nb // nparts
    tm, tn, tk = _tile(D, TM, LANE), _tile(nbp, TN, LANE), n
    nbn = nbp // tn
    return _mm(h_t, dproj, grid=(D // tm, NDEV * nbn, n // tk),
               a_spec=pl.BlockSpec((tm, tk), lambda i, j, k: (i, k)),
               b_spec=pl.BlockSpec((tk, tn), lambda i, j, k: (k, (j // nbn) * (nb // tn) + part * nbn + j % nbn)),
               o_spec=pl.BlockSpec((None, tm, tn), lambda i, j, k: (j // nbn, i, j % nbn)),
               out_shape=_sds((NDEV, D, nbp), MXU), acc_shape=(tm, tn), dims=NN, name=name, dep=dep)


def _mm_group(a, b, mode, out_dtype, name):
    if mode == "wgrad":
        L, W = a.shape
        G = len(POOL_WINDOWS)
        pd = W // G
        tm, tn, tk = _tile(pd, TM, LANE), _tile(pd, TN, LANE), _tile(L, TK, SUB16)
        nm, nn = pd // tm, pd // tn
        return _mm(a, b, grid=(G, nm, nn, L // tk),
                   a_spec=pl.BlockSpec((tk, tm), lambda g, i, j, k: (k, g * nm + i)),
                   b_spec=pl.BlockSpec((tk, tn), lambda g, i, j, k: (k, g * nn + j)),
                   o_spec=pl.BlockSpec((None, tm, tn), lambda g, i, j, k: (g, i, j)),
                   out_shape=_sds((G, pd, pd), out_dtype), acc_shape=(tm, tn), dims=TN_DIMS, name=name)
    L, W = a.shape
    G, pd, _ = b.shape
    tm, tn, tk = _tile(L, TM, SUB16), _tile(pd, TN, LANE), _tile(pd, TK, LANE)
    nn, nk = pd // tn, pd // tk
    if mode == "fwd":
        b_spec = pl.BlockSpec((None, tk, tn), lambda g, i, j, k: (g, k, j))
        dims = NN
    else:
        b_spec = pl.BlockSpec((None, tn, tk), lambda g, i, j, k: (g, j, k))
        dims = NT
    return _mm(a, b, grid=(G, L // tm, nn, nk),
               a_spec=pl.BlockSpec((tm, tk), lambda g, i, j, k: (i, g * nk + k)),
               b_spec=b_spec,
               o_spec=pl.BlockSpec((tm, tn), lambda g, i, j, k: (i, g * nn + j)),
               out_shape=_sds((L, W), out_dtype), acc_shape=(tm, tn), dims=dims, name=name)


def _ada_fwd(cc, w_loc, b_loc, name):
    R, D = cc.shape
    na = w_loc.shape[1]
    tk = _tile(D, 512, LANE)

    def body(c_ref, w_ref, b_ref, mod_ref, s_ref):
        k = pl.program_id(0)
        cv = c_ref[...]
        s = cv * _sigmoid_small(cv)
        s_ref[...] = s

        @pl.when(k == 0)
        def _():
            mod_ref[...] = jnp.broadcast_to(b_ref[...], mod_ref.shape)

        mod_ref[...] += lax.dot_general(s.astype(MXU), w_ref[...].astype(MXU), NN, preferred_element_type=F32)

    blocks = [((R, tk), F32), ((tk, na), F32), ((1, na), F32), ((R, na), F32), ((R, tk), F32)]
    return pl.pallas_call(
        body, name=name, grid=(D // tk,),
        in_specs=[pl.BlockSpec((R, tk), lambda k: (0, k)), pl.BlockSpec((tk, na), lambda k: (k, 0)),
                  pl.BlockSpec((1, na), lambda k: (0, 0))],
        out_specs=[pl.BlockSpec((R, na), lambda k: (0, 0)), pl.BlockSpec((R, tk), lambda k: (0, k))],
        out_shape=[_sds((R, na), F32), _sds((R, D), F32)],
        compiler_params=_params(blocks, dims=("arbitrary",)),
    )(cc, w_loc, b_loc)


def _adam(w, g, m, v):
    m = ADAM_B1 * m + (1.0 - ADAM_B1) * g
    v = ADAM_B2 * v + (1.0 - ADAM_B2) * (g * g)
    m_hat = m / (1.0 - ADAM_B1 ** ADAM_STEP)
    v_hat = v / (1.0 - ADAM_B2 ** ADAM_STEP)
    delta = -ADAM_LR * (m_hat / (jnp.sqrt(v_hat) + ADAM_EPS) + ADAM_WD * w)
    return delta, m, v


def _ada_bwd(s_all, ga, gc, w_loc, m_loc, v_loc, name):
    D, na = w_loc.shape
    tr = _tile(D, 256, LANE)

    def body(s_ref, ga_ref, gc_ref, w_ref, m_ref, v_ref, g_ref, d_ref, nm_ref, nv_ref, pc_ref):
        dmc = gc_ref[0:1, :]
        for p in range(1, NDEV):
            dmc = dmc + gc_ref[p:p + 1, :]
        rows = lax.broadcasted_iota(jnp.int32, (NDEV, na), 0)
        dmc8 = jnp.where(rows == 0, jnp.broadcast_to(dmc, (NDEV, na)), 0.0)
        dm = jnp.concatenate([ga_ref[...], dmc8], axis=0).astype(MXU)
        dmc16 = jnp.concatenate([dmc8, jnp.zeros_like(dmc8)], axis=0).astype(MXU)
        w = w_ref[...]
        g = lax.dot_general(s_ref[...].astype(MXU), dm, TN_DIMS, preferred_element_type=F32)
        pc_ref[...] = lax.dot_general(dmc16, w.astype(MXU), NT, preferred_element_type=F32)
        delta, nm, nv = _adam(w, g, m_ref[...], v_ref[...])
        g_ref[...] = g
        d_ref[...] = delta
        nm_ref[...] = nm
        nv_ref[...] = nv

    big = pl.BlockSpec((tr, na), lambda i: (i, 0))
    full = pl.BlockSpec((NDEV, na), lambda i: (0, 0))
    srow = pl.BlockSpec((2 * NDEV, tr), lambda i: (0, i))
    blocks = [((2 * NDEV, tr), F32)] * 2 + [((NDEV, na), F32)] * 2 + [((tr, na), F32)] * 7
    return pl.pallas_call(
        body, name=name, grid=(D // tr,),
        in_specs=[srow, full, full, big, big, big],
        out_specs=[big, big, big, big, srow],
        out_shape=[_sds((D, na), F32)] * 4 + [_sds((2 * NDEV, D), F32)],
        compiler_params=_params(blocks, dims=("parallel",)),
    )(s_all, ga, gc, w_loc, m_loc, v_loc)


def _norm_mod(x2, g, shift, scale, n, row0, h_prev, name):
    R, D = x2.shape
    tl = _tile(R, TL, SUB16)
    assert row0 % tl == 0
    b0 = row0 // tl

    def body(x_ref, g_ref, sh_ref, sc_ref, *rest):
        o_ref = rest[-1]
        xv = x_ref[...]
        s = lax.rsqrt(jnp.mean(xv * xv, axis=-1, keepdims=True) + EPS)
        nrm = xv * s * g_ref[...]
        o_ref[...] = (nrm * (1.0 + sc_ref[...]) + sh_ref[...]).astype(o_ref.dtype)

    vec = pl.BlockSpec((1, D), lambda i: (0, 0))
    in_specs = [pl.BlockSpec((tl, D), lambda i: (i, 0)), vec, vec, vec]
    args = [x2, g, shift, scale]
    aliases = {}
    if h_prev is not None:
        in_specs.append(ANY)
        args.append(h_prev)
        aliases = {4: 0}
    blocks = [((tl, D), F32), ((tl, D), MXU)] + [((1, D), F32)] * 3
    return pl.pallas_call(
        body, name=name, grid=(R // tl,), in_specs=in_specs,
        out_specs=pl.BlockSpec((tl, D), lambda i: (i + b0, 0)),
        out_shape=_sds((n, D), MXU), input_output_aliases=aliases,
        compiler_params=_params(blocks, dims=("parallel",)),
    )(*args)


def _norm_bwd(x2, dh_all, row0, g, scale, dxn, ggn0, name):
    R, D = x2.shape
    tl = _tile(R, TL_FINAL, SUB)
    assert row0 % tl == 0
    b0 = row0 // tl
    with_x = dxn is not None

    def body(*refs):
        if with_x:
            x_ref, dh_ref, g_ref, sc_ref, gg0_ref, dxn_ref, gx_ref, dsh_ref, dsc_ref, gg_ref = refs
        else:
            x_ref, dh_ref, g_ref, sc_ref, gg0_ref, dsh_ref, dsc_ref, gg_ref = refs
        i = pl.program_id(0)

        @pl.when(i == 0)
        def _():
            dsh_ref[...] = jnp.zeros_like(dsh_ref)
            dsc_ref[...] = jnp.zeros_like(dsc_ref)
            gg_ref[...] = gg0_ref[...]

        xv = x_ref[...]
        dh = dh_ref[...]
        gv = g_ref[...]
        s = lax.rsqrt(jnp.mean(xv * xv, axis=-1, keepdims=True) + EPS)
        xh = xv * s
        dsh_ref[...] += jnp.sum(dh, axis=0, keepdims=True)
        dsc_ref[...] += jnp.sum(dh * (xh * gv), axis=0, keepdims=True)
        dn = dh * (1.0 + sc_ref[...])
        gg_ref[...] += jnp.sum(dn * xh, axis=0, keepdims=True)
        if with_x:
            dxh = dn * gv
            dx = s * (dxh - xh * jnp.mean(dxh * xh, axis=-1, keepdims=True))
            gx_ref[...] = dx + dxn_ref[...]

    vec = pl.BlockSpec((1, D), lambda i: (0, 0))
    row = pl.BlockSpec((tl, D), lambda i: (i, 0))
    in_specs = [row, pl.BlockSpec((tl, D), lambda i: (i + b0, 0)), vec, vec, vec]
    args = [x2, dh_all, g, scale, ggn0]
    out_specs = [vec, vec, vec]
    out_shape = [_sds((1, D), F32)] * 3
    if with_x:
        in_specs.append(row)
        args.append(dxn)
        out_specs = [row] + out_specs
        out_shape = [_sds((R, D), F32)] + out_shape
    blocks = [((tl, D), F32)] * (4 if with_x else 2) + [((1, D), F32)] * 6
    outs = pl.pallas_call(
        body, name=name, grid=(R // tl,), in_specs=in_specs, out_specs=out_specs, out_shape=out_shape,
        compiler_params=_params(blocks, dims=("arbitrary",)),
    )(*args)
    return tuple(outs) if with_x else (None,) + tuple(outs)


def _tap_valid(t, o, lc, n):
    tt = t + o
    in_ctx = t < lc
    return (tt >= jnp.where(in_ctx, 0, lc)) & (tt < jnp.where(in_ctx, lc, n))


def _conv_fwd(proj_all, cw, cb, lc, W, name):
    n = proj_all.shape[0]
    cbk = _tile(W, CB_SEQ, LANE)
    tr = _tile(n, TR_CONV, SUB16)
    ext = tr + 2 * SUB

    def body(x_ref, w_ref, b_ref, u_ref, xp_ref):
        xp_ref[0:SUB, :] = jnp.zeros((SUB, cbk), F32)
        xp_ref[n + SUB:n + 2 * SUB, :] = jnp.zeros((SUB, cbk), F32)
        xp_ref[SUB:n + SUB, :] = x_ref[...]
        w = w_ref[...]
        bias = b_ref[...]

        def chunk(ci, carry):
            r0 = pl.multiple_of(ci * tr, SUB16)
            xe = xp_ref[pl.ds(r0, ext), :]
            t = r0 + lax.broadcasted_iota(jnp.int32, (tr, cbk), 0)
            acc = jnp.broadcast_to(bias, (tr, cbk))
            for k in range(4):
                o = k - 1
                sh = xe if o == 0 else pltpu.roll(xe, (-o) % ext, 0)
                acc = acc + jnp.where(_tap_valid(t, o, lc, n), sh[SUB:tr + SUB], 0.0) * w[k:k + 1]
            u_ref[pl.ds(r0, tr), :] = acc
            return carry

        lax.fori_loop(0, n // tr, chunk, 0)

    blocks = [((n, cbk), F32)] * 2 + [((4, cbk), F32), ((1, cbk), F32)]
    scratch = [((n + 2 * SUB, cbk), F32)]
    return pl.pallas_call(
        body, name=name, grid=(W // cbk,),
        in_specs=[pl.BlockSpec((n, cbk), lambda j: (0, j)), pl.BlockSpec((4, cbk), lambda j: (0, j)),
                  pl.BlockSpec((1, cbk), lambda j: (0, j))],
        out_specs=pl.BlockSpec((n, cbk), lambda j: (0, j)),
        out_shape=_sds((n, W), F32),
        scratch_shapes=[pltpu.VMEM(s, d) for s, d in scratch],
        compiler_params=_params(blocks, scratch, ("parallel",)),
    )(proj_all, cw, cb)


def _conv_bwd(du_all, proj_all, cw, dproj, lc, W, name):
    n = du_all.shape[0]
    cbk = _tile(W, CB_SEQ, LANE)
    tr = _tile(n, TR_CONV, SUB16)
    ext = tr + 2 * SUB

    def body(du_ref, x_ref, w_ref, dp_in, dx_ref, gw_ref, gb_ref, dp_ref, xp_ref):
        del dp_in
        for ref, src in ((dp_ref, du_ref), (xp_ref, x_ref)):
            ref[0:SUB, :] = jnp.zeros((SUB, cbk), F32)
            ref[n + SUB:n + 2 * SUB, :] = jnp.zeros((SUB, cbk), F32)
            ref[SUB:n + SUB, :] = src[...]
        w = w_ref[...]

        def fold(v):
            return jnp.sum(v.reshape(tr // SUB, SUB, cbk), axis=0)

        def chunk(ci, carry):
            r0 = pl.multiple_of(ci * tr, SUB16)
            de = dp_ref[pl.ds(r0, ext), :]
            xe = xp_ref[pl.ds(r0, ext), :]
            t = r0 + lax.broadcasted_iota(jnp.int32, (tr, cbk), 0)
            d0 = de[SUB:tr + SUB]
            dx = jnp.zeros((tr, cbk), F32)
            new = []
            for k in range(4):
                o = k - 1
                dsh = de if o == 0 else pltpu.roll(de, o % ext, 0)
                dx = dx + jnp.where(_tap_valid(t, -o, lc, n), dsh[SUB:tr + SUB], 0.0) * w[k:k + 1]
                xsh = xe if o == 0 else pltpu.roll(xe, (-o) % ext, 0)
                new.append(carry[k] + fold(d0 * jnp.where(_tap_valid(t, o, lc, n), xsh[SUB:tr + SUB], 0.0)))
            new.append(carry[4] + fold(d0))
            dx_ref[pl.ds(r0, tr), :] = dx.astype(dx_ref.dtype)
            return tuple(new)

        zero = jnp.zeros((SUB, cbk), F32)
        acc = lax.fori_loop(0, n // tr, chunk, (zero,) * 5)
        for k in range(4):
            gw_ref[k:k + 1, :] = jnp.sum(acc[k], axis=0, keepdims=True)
        gb_ref[...] = jnp.sum(acc[4], axis=0, keepdims=True)

    col = pl.BlockSpec((n, cbk), lambda j: (0, j))
    blocks = [((n, cbk), F32)] * 2 + [((n, cbk), MXU), ((4, cbk), F32), ((4, cbk), F32), ((1, cbk), F32)]
    scratch = [((n + 2 * SUB, cbk), F32)] * 2
    return pl.pallas_call(
        body, name=name, grid=(W // cbk,),
        in_specs=[col, col, pl.BlockSpec((4, cbk), lambda j: (0, j)), ANY],
        out_specs=[col, pl.BlockSpec((4, cbk), lambda j: (0, j)), pl.BlockSpec((1, cbk), lambda j: (0, j))],
        out_shape=[_sds(dproj.shape, dproj.dtype), _sds((4, W), F32), _sds((1, W), F32)],
        input_output_aliases={3: 0},
        scratch_shapes=[pltpu.VMEM(s, d) for s, d in scratch],
        compiler_params=_params(blocks, scratch, ("parallel",)),
    )(du_all, proj_all, cw, dproj)


def _gate_coeffs(ub, u, d, wr_ref, wi_ref, br_ref, bi_ref, lam_ref):
    c = -LRU_C * _softplus(-lam_ref[d:d + 1, :])
    r = _sigmoid(lax.dot_general(ub, wr_ref[d], NN, preferred_element_type=F32) + br_ref[d:d + 1, :])
    ig = _sigmoid(lax.dot_general(ub, wi_ref[d], NN, preferred_element_type=F32) + bi_ref[d:d + 1, :])
    la = c * r
    a = jnp.exp(la)
    sq = jnp.sqrt(-jnp.tanh(la) * (1.0 + a * a))
    return c, r, ig, a, sq


def _gate_specs(tl, hd):
    w_spec = pl.BlockSpec((2, None, hd, hd), lambda h, i: (0, h, 0, 0))
    v_spec = pl.BlockSpec((2, hd), lambda h, i: (0, h))
    return w_spec, v_spec


def _gates_fwd(u_all, wr, wi, br, bi, lam, name):
    n, W = u_all.shape
    heads, hd = wr.shape[1], wr.shape[2]
    tl = _tile(n, TL, SUB16)

    def body(u_ref, wr_ref, wi_ref, br_ref, bi_ref, lam_ref, a_ref, b_ref):
        u = u_ref[...]
        ub = u.astype(MXU)
        for d in range(2):
            _, _, ig, a, sq = _gate_coeffs(ub, u, d, wr_ref, wi_ref, br_ref, bi_ref, lam_ref)
            a_ref[d] = a
            b_ref[d] = sq * (ig * u)

    w_spec, v_spec = _gate_specs(tl, hd)
    o_spec = pl.BlockSpec((2, tl, hd), lambda h, i: (0, i, h))
    blocks = [((tl, hd), F32), ((2, hd, hd), MXU), ((2, hd, hd), MXU)] + [((2, hd), F32)] * 3 + [((2, tl, hd), F32)] * 2
    return pl.pallas_call(
        body, name=name, grid=(heads, n // tl),
        in_specs=[pl.BlockSpec((tl, hd), lambda h, i: (i, h)), w_spec, w_spec, v_spec, v_spec, v_spec],
        out_specs=[o_spec, o_spec], out_shape=[_sds((2, n, W), F32)] * 2,
        compiler_params=_params(blocks, dims=("parallel", "parallel")),
    )(u_all, wr, wi, br, bi, lam)


def _gates_bwd(u_all, da, db, wr, wi, br, bi, lam, name):
    n, W = u_all.shape
    heads, hd = wr.shape[1], wr.shape[2]
    tl = _tile(n, TL, SUB16)
    ni = n // tl

    def body(u_ref, da_ref, db_ref, wr_ref, wi_ref, br_ref, bi_ref, lam_ref,
             du_ref, gwr_ref, gwi_ref, gbr_ref, gbi_ref, gc_ref, accr_ref, acci_ref):
        i = pl.program_id(1)

        @pl.when(i == 0)
        def _():
            accr_ref[...] = jnp.zeros_like(accr_ref)
            acci_ref[...] = jnp.zeros_like(acci_ref)
            gbr_ref[...] = jnp.zeros_like(gbr_ref)
            gbi_ref[...] = jnp.zeros_like(gbi_ref)
            gc_ref[...] = jnp.zeros_like(gc_ref)

        u = u_ref[...]
        ub = u.astype(MXU)
        du = jnp.zeros_like(u)
        for d in range(2):
            c, r, ig, a, sq = _gate_coeffs(ub, u, d, wr_ref, wi_ref, br_ref, bi_ref, lam_ref)
            dbv = db_ref[d]
            t = dbv * sq
            du = du + t * ig
            d_la = da_ref[d] * a - (dbv * ig * u) * (a * a) / sq
            gc_ref[d:d + 1, :] += jnp.sum(d_la * r, axis=0, keepdims=True)
            d_pr = (d_la * c) * (r * (1.0 - r))
            d_pi = (t * u) * (ig * (1.0 - ig))
            gbr_ref[d:d + 1, :] += jnp.sum(d_pr, axis=0, keepdims=True)
            gbi_ref[d:d + 1, :] += jnp.sum(d_pi, axis=0, keepdims=True)
            pb = d_pr.astype(MXU)
            qb = d_pi.astype(MXU)
            du = du + lax.dot_general(pb, wr_ref[d], NT, preferred_element_type=F32)
            du = du + lax.dot_general(qb, wi_ref[d], NT, preferred_element_type=F32)
            accr_ref[d] += lax.dot_general(ub, pb, TN_DIMS, preferred_element_type=F32)
            acci_ref[d] += lax.dot_general(ub, qb, TN_DIMS, preferred_element_type=F32)
        du_ref[...] = du

        @pl.when(i == ni - 1)
        def _():
            gwr_ref[...] = accr_ref[...].astype(gwr_ref.dtype)
            gwi_ref[...] = acci_ref[...].astype(gwi_ref.dtype)

    w_spec, v_spec = _gate_specs(tl, hd)
    u_spec = pl.BlockSpec((tl, hd), lambda h, i: (i, h))
    ab_spec = pl.BlockSpec((2, tl, hd), lambda h, i: (0, i, h))
    blocks = ([((tl, hd), F32)] * 2 + [((2, tl, hd), F32)] * 2 + [((2, hd, hd), MXU)] * 4 + [((2, hd), F32)] * 6)
    scratch = [((2, hd, hd), F32)] * 2
    return pl.pallas_call(
        body, name=name, grid=(heads, ni),
        in_specs=[u_spec, ab_spec, ab_spec, w_spec, w_spec, v_spec, v_spec, v_spec],
        out_specs=[u_spec, w_spec, w_spec, v_spec, v_spec, v_spec],
        out_shape=[_sds((n, W), F32), _sds(wr.shape, MXU), _sds(wi.shape, MXU)] + [_sds((2, W), F32)] * 3,
        scratch_shapes=[pltpu.VMEM(s, d) for s, d in scratch],
        compiler_params=_params(blocks, scratch, ("parallel", "arbitrary")),
    )(u_all, da, db, wr, wi, br, bi, lam)


def _tile_scan(A, B, rows, reverse):
    for s in (1, 2, 4):
        if reverse:
            As, Bs, m = pltpu.roll(A, SUB - s, 0), pltpu.roll(B, SUB - s, 0), rows < SUB - s
        else:
            As, Bs, m = pltpu.roll(A, s, 0), pltpu.roll(B, s, 0), rows >= s
        B = jnp.where(m, A * Bs + B, B)
        A = jnp.where(m, A * As, A)
    return A, B


def _scan_chunks(n, lc):
    tc = _tile(lc, TL, SUB)
    assert n % tc == 0 and lc % tc == 0
    return tc, n // tc, lc // tc


def _scan_fwd(a_all, b_all, lc, name):
    _, n, W = a_all.shape
    cb = _tile(W, CB_SCAN, LANE)
    tc, nch, ncc = _scan_chunks(n, lc)
    ntile = tc // SUB

    def chunk(d, t):
        return jnp.where(d == 0, t, jnp.where(t < ncc, ncc - 1 - t, nch - 1 - (t - ncc)))

    def body(a_ref, b_ref, h_ref, carry_ref):
        rows = lax.broadcasted_iota(jnp.int32, (SUB, cb), 0)

        @pl.when(pl.program_id(2) == 0)
        def _():
            carry_ref[...] = jnp.zeros_like(carry_ref)

        def run(reverse):
            def step(i, h):
                r = pl.multiple_of(((ntile - 1 - i) if reverse else i) * SUB, SUB)
                A, B = _tile_scan(a_ref[pl.ds(r, SUB), :], b_ref[pl.ds(r, SUB), :], rows, reverse)
                H = A * h + B
                h_ref[pl.ds(r, SUB), :] = H
                return H[0:1, :] if reverse else H[SUB - 1:SUB, :]

            carry_ref[...] = lax.fori_loop(0, ntile, step, carry_ref[...], unroll=2)

        @pl.when(pl.program_id(1) == 0)
        def _():
            run(False)

        @pl.when(pl.program_id(1) == 1)
        def _():
            run(True)

    spec = pl.BlockSpec((None, tc, cb), lambda j, d, t: (d, chunk(d, t), j))
    return pl.pallas_call(
        body, name=name, grid=(W // cb, 2, nch), in_specs=[spec, spec], out_specs=spec,
        out_shape=_sds((2, n, W), F32), scratch_shapes=[pltpu.VMEM((1, cb), F32)],
        compiler_params=_params([((tc, cb), F32)] * 3, [((1, cb), F32)], ("parallel", "arbitrary", "arbitrary")),
    )(a_all, b_all)


def _scan_bwd(a_all, h_all, dya, lc, name):
    _, n, W = a_all.shape
    cb = _tile(W, CB_SCAN, LANE)
    tc, nch, ncc = _scan_chunks(n, lc)
    ntile = tc // SUB
    nl = nch - ncc

    def chunk(d, t):
        return jnp.where(d == 0, nch - 1 - t, jnp.where(t < nl, ncc + t, t - nl))

    def neighbour(d, t):
        c = chunk(d, t)
        below = jnp.maximum(c * ntile - 1, 0)
        above = jnp.where(c == nch - 1, 0, jnp.minimum((c + 1) * ntile, nch * ntile - 1))
        return jnp.where(d == 0, below, above)

    def body(a_ref, h_ref, hn_ref, g_ref, da_ref, db_ref, mu_ref):
        rows = lax.broadcasted_iota(jnp.int32, (SUB, cb), 0)
        d, t = pl.program_id(1), pl.program_id(2)
        c = chunk(d, t)
        has_g = c >= ncc

        @pl.when(t == 0)
        def _():
            mu_ref[...] = jnp.zeros_like(mu_ref)

        def tile(ref, j):
            return ref[pl.ds(pl.multiple_of(j * SUB, SUB), SUB), :]

        def run(up):
            if up:
                edge = jnp.where(c == ncc - 1, 0.0, hn_ref[0:1, :])
            else:
                edge = jnp.where(c > 0, hn_ref[SUB - 1:SUB, :], 0.0)

            def step(i, mu):
                j = i if up else ntile - 1 - i
                a_t = tile(a_ref, j)
                g_t = jnp.where(has_g, tile(g_ref, j), 0.0)
                if up:
                    ap = jnp.where(rows >= 1, pltpu.roll(a_t, 1, 0), 1.0)
                    nb_row = jnp.where(j < ntile - 1, tile(h_ref, jnp.minimum(j + 1, ntile - 1))[0:1, :], edge)
                    hprev = jnp.where(rows < SUB - 1, pltpu.roll(tile(h_ref, j), SUB - 1, 0), nb_row)
                else:
                    ap = jnp.where(rows < SUB - 1, pltpu.roll(a_t, SUB - 1, 0), 1.0)
                    nb_row = jnp.where(j > 0, tile(h_ref, jnp.maximum(j - 1, 0))[SUB - 1:SUB, :], edge)
                    hprev = jnp.where(rows >= 1, pltpu.roll(tile(h_ref, j), 1, 0), nb_row)
                A, B = _tile_scan(ap, g_t, rows, not up)
                lam = A * mu + B
                r = pl.multiple_of(j * SUB, SUB)
                da_ref[pl.ds(r, SUB), :] = lam * hprev
                db_ref[pl.ds(r, SUB), :] = lam
                return a_t[SUB - 1:SUB, :] * lam[SUB - 1:SUB, :] if up else a_t[0:1, :] * lam[0:1, :]

            mu_ref[...] = lax.fori_loop(0, ntile, step, mu_ref[...], unroll=2)

        @pl.when(d == 0)
        def _():
            run(False)

        @pl.when(d == 1)
        def _():
            run(True)

    spec = pl.BlockSpec((None, tc, cb), lambda j, d, t: (d, chunk(d, t), j))
    n_spec = pl.BlockSpec((None, SUB, cb), lambda j, d, t: (d, neighbour(d, t), j))
    g_spec = pl.BlockSpec((tc, cb), lambda j, d, t: (jnp.maximum(chunk(d, t) - ncc, 0), j))
    blocks = [((tc, cb), F32)] * 5 + [((SUB, cb), F32)]
    return pl.pallas_call(
        body, name=name, grid=(W // cb, 2, nch), in_specs=[spec, spec, n_spec, g_spec], out_specs=[spec, spec],
        out_shape=[_sds((2, n, W), F32)] * 2, scratch_shapes=[pltpu.VMEM((1, cb), F32)],
        compiler_params=_params(blocks, [((1, cb), F32)], ("parallel", "arbitrary", "arbitrary")),
    )(a_all, h_all, h_all, dya)


def _pool_window(v, w, tl, cb, transpose):
    half = w // 2
    pos = lax.broadcasted_iota(jnp.int32, (tl, cb), 0) % GRID_W
    cnt = (jnp.minimum(pos + half - 1, GRID_W - 1) - jnp.maximum(pos - half, 0) + 1).astype(F32)
    src = v / cnt if transpose else v

    def run_sum(s, step):
        span = 1
        while span < half:
            ok = (pos + span < GRID_W) if step > 0 else (pos - span >= 0)
            s = s + jnp.where(ok, pltpu.roll(s, (-step * span) % tl, 0), 0.0)
            span *= 2
        return s

    ahead, behind = run_sum(src, 1), run_sum(src, -1)
    if transpose:
        return behind + jnp.where(pos + 1 < GRID_W, pltpu.roll(ahead, tl - 1, 0), 0.0) - v
    return (ahead + jnp.where(pos >= 1, pltpu.roll(behind, 1, 0), 0.0)) / cnt - v


def _pool_z(src, row0, col0, L, W, transpose, dproj, name):
    G = len(POOL_WINDOWS)
    pd = W // G
    tl = _tile(L, TL, GRID_W)
    cb = _tile(pd, CB_SEQ, LANE)
    assert row0 % tl == 0 and col0 % cb == 0
    rb, cbk = row0 // tl, col0 // cb
    nj = pd // cb

    def body(x_ref, *rest):
        o_ref = rest[-1]
        for gi, w in enumerate(POOL_WINDOWS):
            @pl.when(pl.program_id(0) == gi)
            def _(w=w):
                o_ref[...] = _pool_window(x_ref[...], w, tl, cb, transpose).astype(o_ref.dtype)

    plain = pl.BlockSpec((tl, cb), lambda g, i, j: (i, g * nj + j))
    window = pl.BlockSpec((tl, cb), lambda g, i, j: (i + rb, cbk + g * nj + j))
    blocks = [((tl, cb), F32), ((tl, cb), MXU)]
    if transpose:
        return pl.pallas_call(
            body, name=name, grid=(G, L // tl, nj), in_specs=[plain, ANY], out_specs=window,
            out_shape=_sds(dproj.shape, dproj.dtype), input_output_aliases={1: 0},
            compiler_params=_params(blocks, dims=("parallel",) * 3),
        )(src, dproj)
    return pl.pallas_call(
        body, name=name, grid=(G, L // tl, nj), in_specs=[window], out_specs=plain,
        out_shape=_sds((L, W), MXU),
        compiler_params=_params(blocks, dims=("parallel",) * 3),
    )(src)


def _mix_fwd(hs, proj_all, ypre, b_pool, pool_scale, lc, name):
    L, W = ypre.shape
    tl = _tile(L, TL, SUB16)
    cb = _tile(W, CB_MIX, LANE)
    nj = W // cb
    assert lc % tl == 0
    rb = lc // tl

    def body(hs_ref, ga_ref, yp_ref, gb_ref, bp_ref, ps_ref, o_ref):
        p = pl.program_id(2)

        @pl.when(p == 0)
        def _():
            g = ga_ref[...]
            o_ref[...] = ((hs_ref[0] + hs_ref[1]) * (g * _sigmoid(g))).astype(o_ref.dtype)

        @pl.when(p == 1)
        def _():
            g = gb_ref[...]
            yb = (yp_ref[...] + bp_ref[...]) * ps_ref[...]
            o_ref[...] = (yb * (g * _sigmoid(g))).astype(o_ref.dtype)

    vec = pl.BlockSpec((1, cb), lambda i, j, p: (0, j))
    blocks = [((2, tl, cb), F32)] + [((tl, cb), F32)] * 3 + [((tl, cb), MXU)]
    return pl.pallas_call(
        body, name=name, grid=(L // tl, nj, 2),
        in_specs=[pl.BlockSpec((2, tl, cb), lambda i, j, p: (0, i + rb, j)),
                  pl.BlockSpec((tl, cb), lambda i, j, p: (i + rb, 2 * nj + j)),
                  pl.BlockSpec((tl, cb), lambda i, j, p: (i, j)),
                  pl.BlockSpec((tl, cb), lambda i, j, p: (i + rb, 3 * nj + j)), vec, vec],
        out_specs=pl.BlockSpec((tl, cb), lambda i, j, p: (i, p * nj + j)),
        out_shape=_sds((L, 2 * W), MXU),
        compiler_params=_params(blocks, dims=("parallel", "parallel", "arbitrary")),
    )(hs, proj_all, ypre, proj_all, b_pool, pool_scale)


def _dsilu(g, sg):
    return sg * (1.0 + g * (1.0 - sg))


def _mixa_bwd(dmixed, hs, proj_all, dproj, lc, W, name):
    L = dmixed.shape[0]
    tl = _tile(L, TL, SUB16)
    cb = _tile(W, CB_MIX, LANE)
    nj = W // cb
    rb = lc // tl

    def body(dm_ref, hs_ref, ga_ref, dp_in, dya_ref, dga_ref):
        del dp_in
        g = ga_ref[...]
        sg = _sigmoid(g)
        dm = dm_ref[...]
        dya_ref[...] = dm * (g * sg)
        dga_ref[...] = (dm * (hs_ref[0] + hs_ref[1]) * _dsilu(g, sg)).astype(dga_ref.dtype)

    blocks = [((tl, cb), F32)] * 3 + [((2, tl, cb), F32), ((tl, cb), MXU)]
    return pl.pallas_call(
        body, name=name, grid=(L // tl, nj),
        in_specs=[pl.BlockSpec((tl, cb), lambda i, j: (i, j)),
                  pl.BlockSpec((2, tl, cb), lambda i, j: (0, i + rb, j)),
                  pl.BlockSpec((tl, cb), lambda i, j: (i + rb, 2 * nj + j)), ANY],
        out_specs=[pl.BlockSpec((tl, cb), lambda i, j: (i, j)),
                   pl.BlockSpec((tl, cb), lambda i, j: (i + rb, 2 * nj + j))],
        out_shape=[_sds((L, W), F32), _sds(dproj.shape, dproj.dtype)],
        input_output_aliases={3: 1},
        compiler_params=_params(blocks, dims=("parallel", "parallel")),
    )(dmixed, hs, proj_all, dproj)


def _mixb_bwd(dmixed, ypre, proj_all, b_pool, pool_scale, dproj, lc, W, name):
    L = dmixed.shape[0]
    tl = _tile(L, TL, SUB16)
    cb = _tile(W, CB_MIX, LANE)
    nj = W // cb
    rb = lc // tl

    def body(dm_ref, yp_ref, gb_ref, bp_ref, ps_ref, dp_in, dyp_ref, dgb_ref, gbp_ref, gps_ref):
        del dp_in
        i = pl.program_id(1)

        @pl.when(i == 0)
        def _():
            gbp_ref[...] = jnp.zeros_like(gbp_ref)
            gps_ref[...] = jnp.zeros_like(gps_ref)

        g = gb_ref[...]
        sg = _sigmoid(g)
        dm = dm_ref[...]
        yp = yp_ref[...] + bp_ref[...]
        ps = ps_ref[...]
        dyb = dm * (g * sg)
        dyp = dyb * ps
        dgb_ref[...] = (dm * (yp * ps) * _dsilu(g, sg)).astype(dgb_ref.dtype)
        dyp_ref[...] = dyp.astype(dyp_ref.dtype)
        gbp_ref[...] += jnp.sum(dyp, axis=0, keepdims=True)
        gps_ref[...] += jnp.sum(dyb * yp, axis=0, keepdims=True)

    vec = pl.BlockSpec((1, cb), lambda j, i: (0, j))
    blocks = [((tl, cb), F32)] * 3 + [((tl, cb), MXU)] * 2 + [((1, cb), F32)] * 4
    return pl.pallas_call(
        body, name=name, grid=(nj, L // tl),
        in_specs=[pl.BlockSpec((tl, cb), lambda j, i: (i, nj + j)),
                  pl.BlockSpec((tl, cb), lambda j, i: (i, j)),
                  pl.BlockSpec((tl, cb), lambda j, i: (i + rb, 3 * nj + j)), vec, vec, ANY],
        out_specs=[pl.BlockSpec((tl, cb), lambda j, i: (i, j)),
                   pl.BlockSpec((tl, cb), lambda j, i: (i + rb, 3 * nj + j)), vec, vec],
        out_shape=[_sds((L, W), MXU), _sds(dproj.shape, dproj.dtype), _sds((1, W), F32), _sds((1, W), F32)],
        input_output_aliases={5: 1},
        compiler_params=_params(blocks, dims=("parallel", "arbitrary")),
    )(dmixed, ypre, proj_all, b_pool, pool_scale, dproj)


def _dproj_init(n, lc, W, name):
    cb = _tile(W, CB_MIX, LANE)
    nj = W // cb

    def body(o_ref):
        o_ref[...] = jnp.zeros_like(o_ref)

    return pl.pallas_call(
        body, name=name, grid=(3 * nj,), in_specs=[],
        out_specs=pl.BlockSpec((lc, cb), lambda j: (0, nj + j)),
        out_shape=_sds((n, 4 * W), MXU),
        compiler_params=_params([((lc, cb), MXU)], dims=("parallel",)),
    )()


def _final(x2, out, tgt, gate, gfin, name):
    L, D = x2.shape
    tl = _tile(L, TL_FINAL, SUB16)

    def body(x_ref, o_ref, t_ref, gate_ref, g_ref, dout_ref, dxn_ref, loss_ref, ggf_ref, dgate_ref):
        i = pl.program_id(0)

        @pl.when(i == 0)
        def _():
            loss_ref[...] = jnp.zeros_like(loss_ref)
            ggf_ref[...] = jnp.zeros_like(ggf_ref)
            dgate_ref[...] = jnp.zeros_like(dgate_ref)

        o = o_ref[...]
        gate_v = gate_ref[...]
        gv = g_ref[...]
        xn = x_ref[...] + gate_v * o
        s = lax.rsqrt(jnp.mean(xn * xn, axis=-1, keepdims=True) + EPS)
        xh = xn * s
        err = xh * gv - t_ref[...]
        tok = jnp.mean(err * err, axis=-1, keepdims=True)
        loss_ref[...] += 0.5 * jnp.sum(tok, axis=0, keepdims=True)
        dy = err / D
        ggf_ref[...] += jnp.sum(dy * xh, axis=0, keepdims=True)
        dxh = dy * gv
        dxn = s * (dxh - xh * jnp.mean(dxh * xh, axis=-1, keepdims=True))
        dgate_ref[...] += jnp.sum(dxn * o, axis=0, keepdims=True)
        dout_ref[...] = (gate_v * dxn).astype(dout_ref.dtype)
        dxn_ref[...] = dxn

    row = pl.BlockSpec((tl, D), lambda i: (i, 0))
    vec = pl.BlockSpec((1, D), lambda i: (0, 0))
    blocks = [((tl, D), F32)] * 4 + [((tl, D), MXU)] + [((1, D), F32)] * 4
    return pl.pallas_call(
        body, name=name, grid=(L // tl,), in_specs=[row, row, row, vec, vec],
        out_specs=[row, row, pl.BlockSpec((1, 1), lambda i: (0, 0)), vec, vec],
        out_shape=[_sds((L, D), MXU), _sds((L, D), F32), _sds((1, 1), F32), _sds((1, D), F32), _sds((1, D), F32)],
        compiler_params=_params(blocks, dims=("arbitrary",)),
    )(x2, out, tgt, gate, gfin)


def _adamw_parts(w2, parts, m2, v2, name):
    R, C = w2.shape
    nh = len(parts)
    ch = C // nh
    tr = _tile(R, max(SUB16, (512 * 1024) // (ch * (nh + 1))), SUB16)

    def body(w_ref, *rest):
        p_refs = rest[:nh]
        m_ref, v_ref, g_ref, d_ref, nm_ref, nv_ref = rest[nh:]
        for q in range(nh):
            @pl.when(pl.program_id(1) == q)
            def _(p_ref=p_refs[q]):
                g = p_ref[0].astype(F32)
                for p in range(1, NDEV):
                    g = g + p_ref[p].astype(F32)
                delta, nm, nv = _adam(w_ref[...], g, m_ref[...], v_ref[...])
                g_ref[...] = g
                d_ref[...] = delta
                nm_ref[...] = nm
                nv_ref[...] = nv

    blk = pl.BlockSpec((tr, ch), lambda i, h: (i, h))
    p_spec = pl.BlockSpec((NDEV, tr, ch), lambda i, h: (0, i, 0))
    blocks = [((tr, ch), F32)] * 7 + [((NDEV, tr, ch), parts[0].dtype)] * nh
    return pl.pallas_call(
        body, name=name, grid=(R // tr, nh),
        in_specs=[blk] + [p_spec] * nh + [blk, blk],
        out_specs=[blk] * 4, out_shape=[_sds((R, C), F32)] * 4,
        compiler_params=_params(blocks, dims=("parallel", "arbitrary")),
    )(w2, *parts, m2, v2)


def _small_sum(vs, ga, gc, name):
    ns, nm = vs.shape[1], ga.shape[1]

    def body(v_ref, ga_ref, gc_ref, tot_ref, gb_ref):
        tot = v_ref[0:1, :]
        gb = ga_ref[0:1, :]
        for p in range(1, NDEV):
            tot = tot + v_ref[p:p + 1, :]
            gb = gb + ga_ref[p:p + 1, :]
        for p in range(NDEV):
            gb = gb + gc_ref[p:p + 1, :]
        tot_ref[...] = tot
        gb_ref[...] = gb

    blocks = [((NDEV, ns), F32), ((NDEV, nm), F32), ((NDEV, nm), F32), ((1, ns), F32), ((1, nm), F32)]
    return pl.pallas_call(
        body, name=name, out_shape=[_sds((1, ns), F32), _sds((1, nm), F32)],
        compiler_params=_params(blocks),
    )(vs, ga, gc)


def _adamw_small(g_raw, w, m, v, lam_range, cctx_range, name):
    npk = w.shape[1]

    def body(g_ref, w_ref, m_ref, v_ref, go_ref, d_ref, nm_ref, nv_ref):
        wv = w_ref[...]
        g = g_ref[...]
        idx = lax.broadcasted_iota(jnp.int32, (1, npk), 1)
        in_lam = (idx >= lam_range[0]) & (idx < lam_range[1])
        in_cc = (idx >= cctx_range[0]) & (idx < cctx_range[1])
        sg = _sigmoid_small(wv)
        g = jnp.where(in_lam, g * (LRU_C * _sigmoid_small(-wv)), jnp.where(in_cc, g * _dsilu(wv, sg), g))
        delta, nm, nv = _adam(wv, g, m_ref[...], v_ref[...])
        go_ref[...] = g
        d_ref[...] = delta
        nm_ref[...] = nm
        nv_ref[...] = nv

    return pl.pallas_call(
        body, name=name, out_shape=[_sds((1, npk), F32)] * 4,
        compiler_params=_params([((1, npk), F32)] * 8),
    )(g_raw, w, m, v)


def _pack(pieces):
    return jnp.concatenate([p.reshape(1, -1) for p in pieces], axis=1)


def kernel(x, c, ctx, c_ctx, w_ada, b_ada, g_norm, w_in, conv_w, conv_b, lru_lambda, w_rgate, b_rgate, w_igate, b_igate, w_pool, b_pool, pool_scale, w_out, g_final, loss_target, m_c_ctx, m_w_ada, m_b_ada, m_g_norm, m_w_in, m_conv_w, m_conv_b, m_lru_lambda, m_w_rgate, m_b_rgate, m_w_igate, m_b_igate, m_w_pool, m_b_pool, m_pool_scale, m_w_out, m_g_final, v_c_ctx, v_w_ada, v_b_ada, v_g_norm, v_w_in, v_conv_w, v_conv_b, v_lru_lambda, v_w_rgate, v_b_rgate, v_w_igate, v_b_igate, v_w_pool, v_b_pool, v_pool_scale, v_w_out, v_g_final):
    L, D = x.shape[1], x.shape[2]
    lc = ctx.shape[1]
    n = lc + L
    W = conv_b.shape[1]
    heads, hd = w_rgate.shape[2], w_rgate.shape[4]
    G, pd = w_pool.shape[1], w_pool.shape[3]
    na = w_ada.shape[2]
    nb = w_in.shape[2]
    ws = W // NDEV
    me = 4 * lax.axis_index("x") + 2 * lax.axis_index("y") + lax.axis_index("c")

    nbp = nb // WIN_PARTS
    w_in_parts = [w_in[0, :, q * nbp:(q + 1) * nbp].astype(MXU) for q in range(WIN_PARTS)]
    (win_0, cw_all, lam_all, br_all, bi_all, c_all) = _all_gather(
        [w_in_parts[0], conv_w[0], lru_lambda[0], b_rgate[0], b_igate[0], c], "gather_w_in")
    win = [win_0]
    cw = cw_all.transpose(1, 0, 2).reshape(4, W)
    lam = lam_all.transpose(1, 0, 2).reshape(2, W)
    br = br_all.transpose(1, 0, 2).reshape(2, W)
    bi = bi_all.transpose(1, 0, 2).reshape(2, W)

    cc = jnp.concatenate([c_all.reshape(NDEV, D), c_ctx.reshape(1, D), jnp.zeros((NDEV - 1, D), F32)], axis=0)
    b_loc = lax.dynamic_slice(b_ada, (0, me * na), (1, na))
    mod_loc, s_all = _ada_fwd(cc, w_ada[0], b_loc, "ada_fwd")
    (mod_all,) = _all_gather([mod_loc], "gather_mod")
    gate_w = [w_rgate[0].astype(MXU), w_igate[0].astype(MXU)]
    rest_w = [w_pool[0].astype(MXU), w_out[0].astype(MXU)]
    tok = mod_all
    sent_win = []
    for q in range(1, WIN_PARTS):
        part = [w_in_parts[q]]
        sent_win.append(_send_start(part, _place(part, False, f"place_w_in_{q}", [tok]), "level1", f"start_w_in_{q}"))
        tok = sent_win[-1][4]
    sent_gw = _send_start(gate_w, _place(gate_w, False, "place_gate_w", [tok]), False, "start_gate_w")
    sent_rw = _send_start(rest_w, _place(rest_w, False, "place_rest_w", [sent_gw[4]]), False, "start_rest_w")
    mod = mod_all.transpose(1, 0, 2).reshape(2 * NDEV, NDEV * na)
    mod_me = lax.dynamic_slice(mod, (me, 0), (1, 3 * D))
    shift, scale, gate = mod_me[:, :D], mod_me[:, D:2 * D], mod_me[:, 2 * D:]
    shift = _tie(shift, [sent_gw[4], sent_rw[4]], "tie_weights")
    shift_c, scale_c = mod[NDEV:NDEV + 1, :D], mod[NDEV:NDEV + 1, D:2 * D]

    x2, ctx2, tgt = x[0], ctx[0], loss_target[0]
    gfin = g_final.reshape(1, D)
    h_all = _norm_mod(x2, g_norm, shift, scale, n, lc, None, "norm_lat")
    h_all = _norm_mod(ctx2, g_norm, shift_c, scale_c, n, 0, h_all, "norm_ctx")
    proj_all = _mm_proj(h_all, win[0], 0, WIN_PARTS, None, "mm_proj_0")
    for q in range(1, WIN_PARTS):
        lands = _send_wait(sent_win[q - 1], proj_all, "level1", f"wait_w_in_{q}")
        passed = _send_start([], lands, "level2", f"pass_w_in_{q}")
        win.append(_send_wait(passed, proj_all, "level2", f"wait_pass_w_in_{q}")[0])
        proj_all = _mm_proj(h_all, win[q], q, WIN_PARTS, proj_all, f"mm_proj_{q}")
    u_all = _conv_fwd(proj_all, cw, conv_b, lc, W, "conv_fwd")
    wr_all, wi_all = _send_wait(sent_gw, u_all, False, "wait_gate_w")
    wr = wr_all.transpose(1, 2, 0, 3, 4).reshape(2, heads, hd, hd)
    wi = wi_all.transpose(1, 2, 0, 3, 4).reshape(2, heads, hd, hd)
    a_all, b_all = _gates_fwd(u_all, wr, wi, br, bi, lam, "gates_fwd")
    hs = _scan_fwd(a_all, b_all, lc, "scan_fwd")
    z = _pool_z(proj_all, lc, W, L, W, False, None, "pool_z")
    wpool_all, wout_all = _send_wait(sent_rw, z, False, "wait_rest_w")
    wpool = wpool_all.transpose(1, 0, 2, 3).reshape(G, pd, pd)
    wout = wout_all.reshape(2 * W, D)
    ypre = _mm_group(z, wpool, "fwd", F32, "mm_pool")
    mixed = _mix_fwd(hs, proj_all, ypre, b_pool, pool_scale, lc, "mix_fwd")
    out = _mm_plain(mixed, wout, NN, F32, "mm_out")
    d_out, dxn, loss_p, ggf, dgate = _final(x2, out, tgt, gate, gfin, "final")

    dmixed = _mm_plain(d_out, wout, NT, F32, "mm_dmixed")
    gwout = _mm_plain(mixed, d_out, TN_DIMS, MXU, "mm_gwout")
    ex_o = [gwout.reshape(NDEV, 2 * W // NDEV, D)]
    sent_o = _send_start(ex_o, _place(ex_o, True, "place_gwout"), True, "start_gwout")
    dproj = _dproj_init(n, lc, W, "dproj_init")
    dya, dproj = _mixa_bwd(dmixed, hs, proj_all, dproj, lc, W, "mixa_bwd")
    dypre, dproj, gbp, gps = _mixb_bwd(dmixed, ypre, proj_all, _tie(b_pool, [sent_o[4]], "tie_gwout"), pool_scale,
                                       dproj, lc, W, "mixb_bwd")
    dz = _mm_group(dypre, wpool, "bwd", F32, "mm_dz")
    gwpool = _mm_group(z, dypre, "wgrad", MXU, "mm_gwpool")
    dproj = _pool_z(dz, lc, W, L, W, True, dproj, "pool_z_bwd")
    da, db = _scan_bwd(a_all, hs, dya, lc, "scan_bwd")
    du, gwr, gwi, gbr, gbi, gcl = _gates_bwd(u_all, da, db, wr, wi, br, bi, lam, "gates_bwd")
    ex_s = [gwpool.reshape(G, NDEV, pd // NDEV, pd).transpose(1, 0, 2, 3),
            gwr.reshape(2, heads, NDEV, hd // NDEV, hd).transpose(2, 0, 1, 3, 4),
            gwi.reshape(2, heads, NDEV, hd // NDEV, hd).transpose(2, 0, 1, 3, 4)]
    sent_s = _send_start(ex_s, _place(ex_s, True, "place_gsmall"), True, "start_gsmall")
    dproj, gcw, gcb = _conv_bwd(du, proj_all, _tie(cw, [sent_s[4]], "tie_gsmall"), dproj, lc, W, "conv_bwd")
    h_t = _transpose(h_all, "transpose_h")
    sent_i, tok = [], None
    for q in range(GWIN_PARTS):
        part = _mm_gwin(h_t, dproj, nb, q, GWIN_PARTS, f"mm_gwin_{q}", dep=tok)
        part = pltpu.with_memory_space_constraint(part, pltpu.HBM)
        sent_i.append(_send_start([part], _place([part], True, f"place_gwin_{q}"), True, f"start_gwin_{q}"))
        tok = sent_i[-1][4]
    dh_all = None
    for q in range(WIN_PARTS):
        dh_all = _mm_dh(dproj, win[q], q, WIN_PARTS, dh_all, f"mm_dh_{q}", dep=tok if q == 0 else None)
    grad_x, dshift, dscale, ggn = _norm_bwd(x2, dh_all, lc, g_norm, scale, dxn, jnp.zeros((1, D), F32), "norm_bwd_lat")
    _, dshift_c, dscale_c, ggn = _norm_bwd(ctx2, dh_all, 0, g_norm, scale_c, None, ggn, "norm_bwd_ctx")

    dmod_me = jnp.concatenate([dshift, dscale, dgate], axis=1)
    dmod_c = jnp.concatenate([dshift_c, dscale_c, jnp.zeros((1, D), F32)], axis=1)
    smalls = [ggf, ggn, gcw, gcb, gcl, gbr, gbi, gbp, gps, jnp.pad(loss_p, ((0, 0), (0, LANE - 1)))]
    sizes = [s.size for s in smalls]
    small_all, dmod_all, dmodc_all = _all_gather([_pack(smalls), dmod_me, dmod_c], "gather_small")
    ga = lax.dynamic_slice(dmod_all.reshape(NDEV, 3 * D), (0, me * na), (NDEV, na))
    gc = lax.dynamic_slice(dmodc_all.reshape(NDEV, 3 * D), (0, me * na), (NDEV, na))
    g_wada, d_wada, nm_wada, nv_wada, pc = _ada_bwd(s_all, ga, gc, w_ada[0], m_w_ada[0], v_w_ada[0], "ada_bwd")
    (pc_all,) = _all_gather([pc[0:1]], "gather_cctx")
    tot, gb_ada = _small_sum(
        jnp.concatenate([small_all.reshape(NDEV, -1), pc_all.reshape(NDEV, D)], axis=1),
        dmod_all.reshape(NDEV, 3 * D), dmodc_all.reshape(NDEV, 3 * D), "small_sum")
    offs = [0]
    for s in sizes + [D]:
        offs.append(offs[-1] + s)
    t_ggf, t_ggn, t_gcw, t_gcb, t_gcl, t_gbr, t_gbi, t_gbp, t_gps, t_loss, t_pc = [
        tot[:, offs[i]:offs[i + 1]] for i in range(len(offs) - 1)]

    def shard(t, rows):
        return lax.dynamic_slice(t.reshape(rows, W), (0, me * ws), (rows, ws))

    def big(wv, parts, mv, vv, name):
        shp = wv.shape
        C = shp[-1]
        if not isinstance(parts, list):
            parts = [parts]
        parts = [p.reshape(NDEV, -1, C // len(parts)) for p in parts]
        outs = _adamw_parts(wv.reshape(-1, C), parts, mv.reshape(-1, C), vv.reshape(-1, C), name)
        return [o.reshape(shp) for o in outs]

    (recv_o,) = _send_wait(sent_o, tot, True, "wait_gwout")
    recv_p, recv_r, recv_i = _send_wait(sent_s, tot, True, "wait_gsmall")
    r_wout = big(w_out, recv_o, m_w_out, v_w_out, "adamw_w_out")
    r_wpool = big(w_pool, recv_p, m_w_pool, v_w_pool, "adamw_w_pool")
    r_wr = big(w_rgate, recv_r, m_w_rgate, v_w_rgate, "adamw_w_rgate")
    r_wi = big(w_igate, recv_i, m_w_igate, v_w_igate, "adamw_w_igate")
    r_wada = [o.reshape(w_ada.shape) for o in (g_wada, d_wada, nm_wada, nv_wada)]

    names = ["c_ctx", "b_ada", "g_norm", "conv_w", "conv_b", "lru_lambda", "b_rgate", "b_igate", "b_pool",
             "pool_scale", "g_final"]
    sw = [c_ctx, b_ada, g_norm, conv_w, conv_b, lru_lambda, b_rgate, b_igate, b_pool, pool_scale, g_final]
    sm = [m_c_ctx, m_b_ada, m_g_norm, m_conv_w, m_conv_b, m_lru_lambda, m_b_rgate, m_b_igate, m_b_pool,
          m_pool_scale, m_g_final]
    sv = [v_c_ctx, v_b_ada, v_g_norm, v_conv_w, v_conv_b, v_lru_lambda, v_b_rgate, v_b_igate, v_b_pool,
          v_pool_scale, v_g_final]
    sg = [t_pc, gb_ada, t_ggn, shard(t_gcw, 4), t_gcb, shard(t_gcl, 2), shard(t_gbr, 2), shard(t_gbi, 2), t_gbp,
          t_gps, t_ggf]
    poffs = [0]
    for wv in sw:
        poffs.append(poffs[-1] + wv.size)
    lam_range = (poffs[5], poffs[6])
    cctx_range = (poffs[0], poffs[1])
    small_out = _adamw_small(_pack(sg), _pack(sw), _pack(sm), _pack(sv), lam_range, cctx_range, "adamw_small")
    recv_w = [_send_wait(sent_i[q], small_out[0], True, f"wait_gwin_{q}")[0] for q in range(GWIN_PARTS)]
    r_win = big(w_in, recv_w, m_w_in, v_w_in, "adamw_w_in")
    r_small = {}
    for i, nm in enumerate(names):
        r_small[nm] = [o[:, poffs[i]:poffs[i + 1]].reshape(sw[i].shape) for o in small_out]

    res = dict(r_small)
    res.update(w_ada=r_wada, w_in=r_win, w_rgate=r_wr, w_igate=r_wi, w_pool=r_wpool, w_out=r_wout)
    order = ["c_ctx", "w_ada", "b_ada", "g_norm", "w_in", "conv_w", "conv_b", "lru_lambda", "w_rgate", "b_rgate",
             "w_igate", "b_igate", "w_pool", "b_pool", "pool_scale", "w_out", "g_final"]
    loss = t_loss[0, 0]
    outs = [loss, grad_x.reshape(x.shape)]
    for q in range(4):
        outs += [res[nm][q] for nm in order]
    return tuple(outs)
```

```python
import functools

import jax
import jax.numpy as jnp
from jax import lax
from jax.experimental import pallas as pl
from jax.experimental.pallas import tpu as pltpu

NDEV = 8
GRID_W = 64
POOL_WINDOWS = (2, 4, 8, 16)
LRU_C = 8.0
EPS = 1e-6
ADAM_LR = 0.001
ADAM_B1 = 0.9
ADAM_B2 = 0.999
ADAM_EPS = 1e-08
ADAM_WD = 0.01
ADAM_STEP = 10

F32 = jnp.float32
MXU = jnp.bfloat16

VMEM_BYTES = 64 * 1024 * 1024
VMEM_SLACK = 8 * 1024 * 1024
SUB = 8
SUB16 = 16
LANE = 128

TM = 1152
TN = 1024
TK = 2048
TL = 256
TL_FINAL = 128
TL_GATES = 544
CB_POOL = 1024
CB_SEQ = 256
CB_SCAN = 1024
CB_MIX = 2048
TR_CONV = 576
GWIN_PARTS = 4
WIN_PARTS = 2

MESH_ID = pl.DeviceIdType.MESH


def _tile(n, pref, align):
    if n <= pref:
        return n
    for t in range(pref - pref % align, 0, -align):
        if n % t == 0:
            return t
    return n


def _nbytes(shape, dtype):
    n = 1
    for s in shape:
        if s is not None:
            n *= s
    return n * jnp.dtype(dtype).itemsize


def _params(blocks, scratch=(), dims=None):
    need = 2 * sum(_nbytes(s, d) for s, d in blocks) + sum(_nbytes(s, d) for s, d in scratch) + VMEM_SLACK
    kw = dict(vmem_limit_bytes=int(min(max(need, 2 * VMEM_SLACK), VMEM_BYTES - VMEM_SLACK // 2)))
    if dims is not None:
        kw["dimension_semantics"] = dims
    return pltpu.CompilerParams(**kw)


def _sds(shape, dtype):
    return jax.ShapeDtypeStruct(tuple(shape), dtype)


ANY = pl.BlockSpec(memory_space=pl.ANY)


def _ids():
    return lax.axis_index("x"), lax.axis_index("y"), lax.axis_index("c")


def _sigmoid(v):
    return 0.5 * jnp.tanh(0.5 * v) + 0.5


def _sigmoid_small(v):
    return jax.nn.sigmoid(v)


def _softplus(v):
    return jnp.maximum(v, 0.0) + jnp.log1p(jnp.exp(-jnp.abs(v)))


def _all_gather(xs, name):
    n = len(xs)

    def body(*refs):
        x_refs, o_refs = refs[:n], refs[n:2 * n]
        send_sems, recv_sems, local_sems = refs[2 * n:]
        x, y, c = _ids()
        me, sibling = (x, y, c), (x, y, 1 - c)
        chips = [(1 - x, y), (x, 1 - y), (1 - x, 1 - y)]

        def slot(a, p):
            return o_refs[a].at[4 * p[0] + 2 * p[1] + p[2]]

        def copy(a, k, block, to, src=None):
            return pltpu.make_async_remote_copy(
                src_ref=slot(a, block) if src is None else src, dst_ref=slot(a, block),
                send_sem=send_sems.at[7 * a + k], recv_sem=recv_sems.at[7 * a + k],
                device_id=to, device_id_type=MESH_ID)

        mine, first, passed = [], [], []
        for a in range(n):
            m = pltpu.make_async_copy(x_refs[a], slot(a, me), local_sems.at[a])
            m.start()
            mine.append(m)
            f = [copy(a, 0, me, sibling, src=x_refs[a])]
            f += [copy(a, 1 + j, me, (*chip, c), src=x_refs[a]) for j, chip in enumerate(chips)]
            for cp in f:
                cp.start()
            first += f
        for a in range(n):
            for j, chip in enumerate(chips):
                copy(a, 1 + j, (*chip, c), me).wait_recv()
                p = copy(a, 4 + j, (*chip, c), sibling)
                p.start()
                passed.append(p)
        for a in range(n):
            copy(a, 0, sibling, me).wait_recv()
            for j, chip in enumerate(chips):
                copy(a, 4 + j, (*chip, 1 - c), me).wait_recv()
        for cp in first + passed:
            cp.wait_send()
        for m in mine:
            m.wait()

    return pl.pallas_call(
        body, name=name,
        out_shape=[_sds((NDEV,) + v.shape, v.dtype) for v in xs],
        in_specs=[ANY] * n, out_specs=[ANY] * n,
        scratch_shapes=[pltpu.SemaphoreType.DMA((7 * n,)), pltpu.SemaphoreType.DMA((7 * n,)),
                        pltpu.SemaphoreType.DMA((n,))],
    )(*xs)


HBM = pl.BlockSpec(memory_space=pltpu.HBM)
SEM = pl.BlockSpec(memory_space=pltpu.SEMAPHORE)
EFFECT = pltpu.SideEffectType.DATAFLOW_SIDE_EFFECTING


def _peers():
    x, y, c = _ids()
    out = []
    for k in range(1, NDEV):
        px = 1 - x if k & 4 else x
        py = 1 - y if k & 2 else y
        pc = 1 - c if k & 1 else c
        out.append(((px, py, pc), 4 * px + 2 * py + pc))
    return out, 4 * x + 2 * y + c


def _tie(v, deps, name):
    def body(v_ref, *rest):
        rest[-1][...] = v_ref[...]

    vmem = pl.BlockSpec(memory_space=pltpu.VMEM)
    return pl.pallas_call(
        body, name=name, out_shape=_sds(v.shape, v.dtype), in_specs=[vmem] + [ANY] * len(deps), out_specs=vmem,
    )(v, *deps)


def _place(srcs, from_slot, name, deps=()):
    n = len(srcs)
    blks = [v.shape[1:] if from_slot else v.shape for v in srcs]

    nd = len(deps)

    def body(*refs):
        s_refs, l_refs = refs[:n], refs[n + nd:2 * n + nd]
        bufs, sems = refs[2 * n + nd:3 * n + nd], refs[3 * n + nd]
        x, y, c = _ids()
        me = 4 * x + 2 * y + c
        ins = [pltpu.make_async_copy(s_refs[a].at[me] if from_slot else s_refs[a], bufs[a], sems.at[a])
               for a in range(n)]
        outs = [pltpu.make_async_copy(bufs[a], l_refs[a].at[me], sems.at[n + a]) for a in range(n)]
        for cp in ins:
            cp.start()
        for a in range(n):
            ins[a].wait()
            outs[a].start()
        for cp in outs:
            cp.wait()

    scratch = [(b, v.dtype) for b, v in zip(blks, srcs)]
    return pl.pallas_call(
        body, name=name, out_shape=[_sds((NDEV,) + b, v.dtype) for b, v in zip(blks, srcs)],
        in_specs=[ANY] * (n + nd), out_specs=[ANY] * n,
        scratch_shapes=[pltpu.VMEM(b, d) for b, d in scratch] + [pltpu.SemaphoreType.DMA((2 * n,))],
        compiler_params=_params([], scratch),
    )(*srcs, *deps)


SEND_PEERS = {True: 7, False: 7, "level1": 4, "level2": 3}


def _send_copies(s_refs, l_refs, ssem, rsem, mode, receiving):
    peers, me = _peers()
    x, y, c = _ids()
    sibling = (x, y, 1 - c)
    chips = [(1 - x, y), (x, 1 - y), (1 - x, 1 - y)]
    npeer = SEND_PEERS[mode]
    out = []
    for a in range(len(l_refs)):
        if mode == "level2":
            for k, (px, py) in enumerate(chips):
                slot = 4 * px + 2 * py + (1 - c if receiving else c)
                out.append(pltpu.make_async_remote_copy(
                    src_ref=l_refs[a].at[slot], dst_ref=l_refs[a].at[slot], send_sem=ssem.at[npeer * a + k],
                    recv_sem=rsem.at[npeer * a + k], device_id=sibling, device_id_type=MESH_ID))
            continue
        targets = peers
        if mode == "level1":
            targets = [(sibling, 4 * x + 2 * y + 1 - c)] + [((px, py, c), 4 * px + 2 * py + c) for px, py in chips]
        for k, (dev, idx) in enumerate(targets):
            out.append(pltpu.make_async_remote_copy(
                src_ref=s_refs[a].at[idx] if mode is True else s_refs[a],
                dst_ref=l_refs[a].at[idx if receiving else me],
                send_sem=ssem.at[npeer * a + k], recv_sem=rsem.at[npeer * a + k], device_id=dev, device_id_type=MESH_ID))
    return out


def _send_start(srcs, lands, mode, name):
    ns, n = len(srcs), len(lands)
    nsem = SEND_PEERS[mode] * n

    def body(*refs):
        s_refs, l_refs = refs[:ns], refs[ns:ns + n]
        ssem, rsem = refs[ns + n], refs[ns + n + 1]
        token = refs[-1]
        for send in _send_copies(s_refs, l_refs, ssem, rsem, mode, False):
            send.start()
        token[...] = jnp.zeros_like(token)

    bufs = list(srcs) + list(lands)
    outs = pl.pallas_call(
        body, name=name,
        out_shape=[pltpu.SemaphoreType.DMA((nsem,)), pltpu.SemaphoreType.DMA((nsem,))]
        + [pltpu.HBM(v.shape, v.dtype) for v in bufs] + [_sds((SUB, LANE), F32)],
        in_specs=[HBM] * (ns + n), out_specs=[SEM, SEM] + [HBM] * (ns + n) + [pl.BlockSpec(memory_space=pltpu.VMEM)],
        input_output_aliases={i: 2 + i for i in range(ns + n)},
        compiler_params=pltpu.CompilerParams(has_side_effects=EFFECT),
    )(*[pltpu.with_memory_space_constraint(v, pltpu.HBM) for v in bufs])
    return outs[0], outs[1], list(outs[2:2 + ns]), list(outs[2 + ns:2 + ns + n]), outs[-1]


def _send_wait(started, after, mode, name):
    ssem, rsem, srcs, lands, _ = started
    ns, n = len(srcs), len(lands)

    def body(*refs):
        s_refs, l_refs = refs[:ns], refs[ns:ns + n]
        ssem_ref, rsem_ref = refs[ns + n], refs[ns + n + 1]
        for recv in _send_copies(s_refs, l_refs, ssem_ref, rsem_ref, mode, True):
            recv.wait_send()
            recv.wait_recv()

    bufs = list(srcs) + list(lands)
    outs = pl.pallas_call(
        body, name=name, out_shape=[pltpu.HBM(v.shape, v.dtype) for v in bufs],
        in_specs=[HBM] * (ns + n) + [SEM, SEM, ANY], out_specs=[HBM] * (ns + n),
        input_output_aliases={i: i for i in range(ns + n)},
        compiler_params=pltpu.CompilerParams(has_side_effects=EFFECT),
    )(*bufs, ssem, rsem, after)
    return list(outs[ns:])


NN = (((1,), (0,)), ((), ()))
NT = (((1,), (1,)), ((), ()))
TN_DIMS = (((0,), (0,)), ((), ()))


def _mm(a, b, *, grid, a_spec, b_spec, o_spec, out_shape, acc_shape, dims, name, dep=None, fill=None):
    k_axis = len(grid) - 1
    nk = grid[k_axis]
    extra = [v for v in (dep, fill) if v is not None]
    aliases = {} if fill is None else {1 + len(extra): 0}

    def body(a_ref, b_ref, *rest):
        o_ref, acc_ref = rest[-2], rest[-1]
        k = pl.program_id(k_axis)

        def prod():
            return lax.dot_general(a_ref[...], b_ref[...], dims, preferred_element_type=F32)

        if nk == 1:
            o_ref[...] = prod().astype(o_ref.dtype)
            return

        @pl.when(k == 0)
        def _():
            acc_ref[...] = prod()

        if nk > 2:
            @pl.when((k > 0) & (k < nk - 1))
            def _():
                acc_ref[...] += prod()

        @pl.when(k == nk - 1)
        def _():
            o_ref[...] = (acc_ref[...] + prod()).astype(o_ref.dtype)

    blocks = [(a_spec.block_shape, a.dtype), (b_spec.block_shape, b.dtype), (o_spec.block_shape, out_shape.dtype)]
    return pl.pallas_call(
        body, name=name, grid=grid, in_specs=[a_spec, b_spec] + [ANY] * len(extra), out_specs=o_spec,
        out_shape=out_shape, scratch_shapes=[pltpu.VMEM(acc_shape, F32)], input_output_aliases=aliases,
        compiler_params=_params(blocks, [(acc_shape, F32)], ("parallel",) * k_axis + ("arbitrary",)),
    )(a, b, *extra)


def _mm_plain(a, b, dims, out_dtype, name):
    if dims == TN_DIMS:
        (K, M), N = a.shape, b.shape[1]
    elif dims == NT:
        (M, K), N = a.shape, b.shape[0]
    else:
        (M, K), N = a.shape, b.shape[1]
    tm, tn = _tile(M, TM, LANE), _tile(N, TN, LANE)
    tk = _tile(K, TK, LANE if dims != TN_DIMS else SUB16)
    if dims == TN_DIMS:
        a_spec = pl.BlockSpec((tk, tm), lambda i, j, k: (k, i))
    else:
        a_spec = pl.BlockSpec((tm, tk), lambda i, j, k: (i, k))
    if dims == NT:
        b_spec = pl.BlockSpec((tn, tk), lambda i, j, k: (j, k))
    else:
        b_spec = pl.BlockSpec((tk, tn), lambda i, j, k: (k, j))
    return _mm(a, b, grid=(M // tm, N // tn, K // tk), a_spec=a_spec, b_spec=b_spec,
               o_spec=pl.BlockSpec((tm, tn), lambda i, j, k: (i, j)),
               out_shape=_sds((M, N), out_dtype), acc_shape=(tm, tn), dims=dims, name=name)


def _mm_proj(h_all, win_q, q, nparts, fill, name):
    n, D = h_all.shape
    nbp = win_q.shape[2]
    nb = nbp * nparts
    tm, tn, tk = _tile(n, TM, SUB16), _tile(nbp, TN, LANE), _tile(D, TK, LANE)
    nbn = nbp // tn
    return _mm(h_all, win_q, grid=(n // tm, NDEV * nbn, D // tk),
               a_spec=pl.BlockSpec((tm, tk), lambda i, j, k: (i, k)),
               b_spec=pl.BlockSpec((None, tk, tn), lambda i, j, k: (j // nbn, k, j % nbn)),
               o_spec=pl.BlockSpec((tm, tn), lambda i, j, k: (i, (j // nbn) * (nb // tn) + q * nbn + j % nbn)),
               out_shape=_sds((n, NDEV * nb), F32), acc_shape=(tm, tn), dims=NN, name=name, fill=fill)


def _mm_dh(dproj, wins, name, dep):
    nparts = len(wins)
    n = dproj.shape[0]
    _, D, nbp = wins[0].shape
    nb = nbp * nparts
    tm, tn, tk = _tile(n, TM, SUB16), _tile(D, TN, LANE), _tile(nbp, TK, LANE)
    nbk = nbp // tk
    kq = NDEV * nbk
    nk = nparts * kq

    def a_index(i, j, k):
        r = k % kq
        return i, (r // nbk) * (nb // tk) + (k // kq) * nbk + r % nbk

    def b_index(q):
        def index(i, j, k):
            r = jnp.clip(k - q * kq, 0, kq - 1)
            return r // nbk, j, r % nbk
        return index

    def body(a_ref, *rest):
        b_refs, o_ref, acc_ref = rest[:nparts], rest[-2], rest[-1]
        k = pl.program_id(2)
        for q in range(nparts):
            def prod(b_ref=b_refs[q]):
                return lax.dot_general(a_ref[...], b_ref[...], NT, preferred_element_type=F32)

            lo, hi = q * kq + (q == 0), (q + 1) * kq - (q == nparts - 1)
            if q == 0:
                @pl.when(k == 0)
                def _(prod=prod):
                    acc_ref[...] = prod()

            if hi > lo:
                @pl.when((k >= lo) & (k < hi))
                def _(prod=prod):
                    acc_ref[...] += prod()

            if q == nparts - 1:
                @pl.when(k == nk - 1)
                def _(prod=prod):
                    o_ref[...] = acc_ref[...] + prod()

    assert nk >= 2
    blocks = [((tm, tk), dproj.dtype)] + [((tn, tk), wins[0].dtype)] * nparts + [((tm, tn), F32)]
    return pl.pallas_call(
        body, name=name, grid=(n // tm, D // tn, nk),
        in_specs=[pl.BlockSpec((tm, tk), a_index)]
        + [pl.BlockSpec((None, tn, tk), b_index(q)) for q in range(nparts)] + [ANY],
        out_specs=pl.BlockSpec((tm, tn), lambda i, j, k: (i, j)), out_shape=_sds((n, D), F32),
        scratch_shapes=[pltpu.VMEM((tm, tn), F32)],
        compiler_params=_params(blocks, [((tm, tn), F32)], ("parallel", "parallel", "arbitrary")),
    )(dproj, *wins, dep)


def _transpose(x, name):
    R, C = x.shape
    tr, tc = _tile(R, TL, LANE), _tile(C, 2 * TL, LANE)

    def body(x_ref, o_ref):
        o_ref[...] = x_ref[...].T

    return pl.pallas_call(
        body, name=name, grid=(R // tr, C // tc),
        in_specs=[pl.BlockSpec((tr, tc), lambda i, j: (i, j))],
        out_specs=pl.BlockSpec((tc, tr), lambda i, j: (j, i)),
        out_shape=_sds((C, R), x.dtype),
        compiler_params=_params([((tr, tc), x.dtype)] * 2, dims=("parallel", "parallel")),
    )(x)


def _mm_gwin(h_t, dproj, nb, part, nparts, name, dep=None):
    D, n = h_t.shape
    nbp = nb // nparts
    tm, tn, tk = _tile(D, TM, LANE), _tile(nbp, TN, LANE), n
    nbn = nbp // tn
    return _mm(h_t, dproj, grid=(D // tm, NDEV * nbn, n // tk),
               a_spec=pl.BlockSpec((tm, tk), lambda i, j, k: (i, k)),
               b_spec=pl.BlockSpec((tk, tn), lambda i, j, k: (k, (j // nbn) * (nb // tn) + part * nbn + j % nbn)),
               o_spec=pl.BlockSpec((None, tm, tn), lambda i, j, k: (j // nbn, i, j % nbn)),
               out_shape=_sds((NDEV, D, nbp), MXU), acc_shape=(tm, tn), dims=NN, name=name, dep=dep)


def _mm_group(a, b, mode, out_dtype, name):
    if mode == "wgrad":
        L, W = a.shape
        G = len(POOL_WINDOWS)
        pd = W // G
        tm, tn, tk = _tile(pd, TM, LANE), _tile(pd, TN, LANE), _tile(L, TK, SUB16)
        nm, nn = pd // tm, pd // tn
        return _mm(a, b, grid=(G, nm, nn, L // tk),
                   a_spec=pl.BlockSpec((tk, tm), lambda g, i, j, k: (k, g * nm + i)),
                   b_spec=pl.BlockSpec((tk, tn), lambda g, i, j, k: (k, g * nn + j)),
                   o_spec=pl.BlockSpec((None, tm, tn), lambda g, i, j, k: (g, i, j)),
                   out_shape=_sds((G, pd, pd), out_dtype), acc_shape=(tm, tn), dims=TN_DIMS, name=name)
    L, W = a.shape
    G, pd, _ = b.shape
    tm, tn, tk = _tile(L, TM, SUB16), _tile(pd, TN, LANE), _tile(pd, TK, LANE)
    nn, nk = pd // tn, pd // tk
    if mode == "fwd":
        b_spec = pl.BlockSpec((None, tk, tn), lambda g, i, j, k: (g, k, j))
        dims = NN
    else:
        b_spec = pl.BlockSpec((None, tn, tk), lambda g, i, j, k: (g, j, k))
        dims = NT
    return _mm(a, b, grid=(G, L // tm, nn, nk),
               a_spec=pl.BlockSpec((tm, tk), lambda g, i, j, k: (i, g * nk + k)),
               b_spec=b_spec,
               o_spec=pl.BlockSpec((tm, tn), lambda g, i, j, k: (i, g * nn + j)),
               out_shape=_sds((L, W), out_dtype), acc_shape=(tm, tn), dims=dims, name=name)


def _ada_fwd(cc, w_loc, b_loc, name):
    R, D = cc.shape
    na = w_loc.shape[1]
    tk = _tile(D, 512, LANE)

    def body(c_ref, w_ref, b_ref, mod_ref, s_ref):
        k = pl.program_id(0)
        cv = c_ref[...]
        s = cv * _sigmoid_small(cv)
        s_ref[...] = s

        @pl.when(k == 0)
        def _():
            mod_ref[...] = jnp.broadcast_to(b_ref[...], mod_ref.shape)

        mod_ref[...] += lax.dot_general(s.astype(MXU), w_ref[...].astype(MXU), NN, preferred_element_type=F32)

    blocks = [((R, tk), F32), ((tk, na), F32), ((1, na), F32), ((R, na), F32), ((R, tk), F32)]
    return pl.pallas_call(
        body, name=name, grid=(D // tk,),
        in_specs=[pl.BlockSpec((R, tk), lambda k: (0, k)), pl.BlockSpec((tk, na), lambda k: (k, 0)),
                  pl.BlockSpec((1, na), lambda k: (0, 0))],
        out_specs=[pl.BlockSpec((R, na), lambda k: (0, 0)), pl.BlockSpec((R, tk), lambda k: (0, k))],
        out_shape=[_sds((R, na), F32), _sds((R, D), F32)],
        compiler_params=_params(blocks, dims=("arbitrary",)),
    )(cc, w_loc, b_loc)


def _adam(w, g, m, v):
    m = ADAM_B1 * m + (1.0 - ADAM_B1) * g
    v = ADAM_B2 * v + (1.0 - ADAM_B2) * (g * g)
    m_hat = m / (1.0 - ADAM_B1 ** ADAM_STEP)
    v_hat = v / (1.0 - ADAM_B2 ** ADAM_STEP)
    delta = -ADAM_LR * (m_hat / (jnp.sqrt(v_hat) + ADAM_EPS) + ADAM_WD * w)
    return delta, m, v


def _ada_bwd(s_all, ga, gc, w_loc, m_loc, v_loc, name):
    D, na = w_loc.shape
    tr = _tile(D, 256, LANE)

    def body(s_ref, ga_ref, gc_ref, w_ref, m_ref, v_ref, g_ref, d_ref, nm_ref, nv_ref, pc_ref):
        dmc = gc_ref[0:1, :]
        for p in range(1, NDEV):
            dmc = dmc + gc_ref[p:p + 1, :]
        rows = lax.broadcasted_iota(jnp.int32, (NDEV, na), 0)
        dmc8 = jnp.where(rows == 0, jnp.broadcast_to(dmc, (NDEV, na)), 0.0)
        dm = jnp.concatenate([ga_ref[...], dmc8], axis=0).astype(MXU)
        dmc16 = jnp.concatenate([dmc8, jnp.zeros_like(dmc8)], axis=0).astype(MXU)
        w = w_ref[...]
        g = lax.dot_general(s_ref[...].astype(MXU), dm, TN_DIMS, preferred_element_type=F32)
        pc_ref[...] = lax.dot_general(dmc16, w.astype(MXU), NT, preferred_element_type=F32)
        delta, nm, nv = _adam(w, g, m_ref[...], v_ref[...])
        g_ref[...] = g
        d_ref[...] = delta
        nm_ref[...] = nm
        nv_ref[...] = nv

    big = pl.BlockSpec((tr, na), lambda i: (i, 0))
    full = pl.BlockSpec((NDEV, na), lambda i: (0, 0))
    srow = pl.BlockSpec((2 * NDEV, tr), lambda i: (0, i))
    blocks = [((2 * NDEV, tr), F32)] * 2 + [((NDEV, na), F32)] * 2 + [((tr, na), F32)] * 7
    return pl.pallas_call(
        body, name=name, grid=(D // tr,),
        in_specs=[srow, full, full, big, big, big],
        out_specs=[big, big, big, big, srow],
        out_shape=[_sds((D, na), F32)] * 4 + [_sds((2 * NDEV, D), F32)],
        compiler_params=_params(blocks, dims=("parallel",)),
    )(s_all, ga, gc, w_loc, m_loc, v_loc)


def _norm_mod(x2, g, shift, scale, n, row0, h_prev, name):
    R, D = x2.shape
    tl = _tile(R, TL, SUB16)
    assert row0 % tl == 0
    b0 = row0 // tl

    def body(x_ref, g_ref, sh_ref, sc_ref, *rest):
        o_ref = rest[-1]
        xv = x_ref[...]
        s = lax.rsqrt(jnp.mean(xv * xv, axis=-1, keepdims=True) + EPS)
        nrm = xv * s * g_ref[...]
        o_ref[...] = (nrm * (1.0 + sc_ref[...]) + sh_ref[...]).astype(o_ref.dtype)

    vec = pl.BlockSpec((1, D), lambda i: (0, 0))
    in_specs = [pl.BlockSpec((tl, D), lambda i: (i, 0)), vec, vec, vec]
    args = [x2, g, shift, scale]
    aliases = {}
    if h_prev is not None:
        in_specs.append(ANY)
        args.append(h_prev)
        aliases = {4: 0}
    blocks = [((tl, D), F32), ((tl, D), MXU)] + [((1, D), F32)] * 3
    return pl.pallas_call(
        body, name=name, grid=(R // tl,), in_specs=in_specs,
        out_specs=pl.BlockSpec((tl, D), lambda i: (i + b0, 0)),
        out_shape=_sds((n, D), MXU), input_output_aliases=aliases,
        compiler_params=_params(blocks, dims=("parallel",)),
    )(*args)


def _norm_bwd(x2, dh_all, row0, g, scale, dxn, ggn0, name):
    R, D = x2.shape
    tl = _tile(R, TL_FINAL, SUB)
    assert row0 % tl == 0
    b0 = row0 // tl
    with_x = dxn is not None

    def body(*refs):
        if with_x:
            x_ref, dh_ref, g_ref, sc_ref, gg0_ref, dxn_ref, gx_ref, dsh_ref, dsc_ref, gg_ref = refs
        else:
            x_ref, dh_ref, g_ref, sc_ref, gg0_ref, dsh_ref, dsc_ref, gg_ref = refs
        i = pl.program_id(0)

        @pl.when(i == 0)
        def _():
            dsh_ref[...] = jnp.zeros_like(dsh_ref)
            dsc_ref[...] = jnp.zeros_like(dsc_ref)
            gg_ref[...] = gg0_ref[...]

        xv = x_ref[...]
        dh = dh_ref[...]
        gv = g_ref[...]
        s = lax.rsqrt(jnp.mean(xv * xv, axis=-1, keepdims=True) + EPS)
        xh = xv * s
        dsh_ref[...] += jnp.sum(dh, axis=0, keepdims=True)
        dsc_ref[...] += jnp.sum(dh * (xh * gv), axis=0, keepdims=True)
        dn = dh * (1.0 + sc_ref[...])
        gg_ref[...] += jnp.sum(dn * xh, axis=0, keepdims=True)
        if with_x:
            dxh = dn * gv
            dx = s * (dxh - xh * jnp.mean(dxh * xh, axis=-1, keepdims=True))
            gx_ref[...] = dx + dxn_ref[...]

    vec = pl.BlockSpec((1, D), lambda i: (0, 0))
    row = pl.BlockSpec((tl, D), lambda i: (i, 0))
    in_specs = [row, pl.BlockSpec((tl, D), lambda i: (i + b0, 0)), vec, vec, vec]
    args = [x2, dh_all, g, scale, ggn0]
    out_specs = [vec, vec, vec]
    out_shape = [_sds((1, D), F32)] * 3
    if with_x:
        in_specs.append(row)
        args.append(dxn)
        out_specs = [row] + out_specs
        out_shape = [_sds((R, D), F32)] + out_shape
    blocks = [((tl, D), F32)] * (4 if with_x else 2) + [((1, D), F32)] * 6
    outs = pl.pallas_call(
        body, name=name, grid=(R // tl,), in_specs=in_specs, out_specs=out_specs, out_shape=out_shape,
        compiler_params=_params(blocks, dims=("arbitrary",)),
    )(*args)
    return tuple(outs) if with_x else (None,) + tuple(outs)


def _tap_valid(t, o, lc, n):
    tt = t + o
    in_ctx = t < lc
    return (tt >= jnp.where(in_ctx, 0, lc)) & (tt < jnp.where(in_ctx, lc, n))


def _conv_fwd(proj_all, cw, cb, lc, W, name):
    n = proj_all.shape[0]
    cbk = _tile(W, CB_SEQ, LANE)
    tr = _tile(n, TR_CONV, SUB16)
    ext = tr + 2 * SUB

    def body(x_ref, w_ref, b_ref, u_ref, xp_ref):
        xp_ref[0:SUB, :] = jnp.zeros((SUB, cbk), F32)
        xp_ref[n + SUB:n + 2 * SUB, :] = jnp.zeros((SUB, cbk), F32)
        xp_ref[SUB:n + SUB, :] = x_ref[...]
        w = w_ref[...]
        bias = b_ref[...]

        def chunk(ci, carry):
            r0 = pl.multiple_of(ci * tr, SUB16)
            xe = xp_ref[pl.ds(r0, ext), :]
            t = r0 + lax.broadcasted_iota(jnp.int32, (tr, cbk), 0)
            acc = jnp.broadcast_to(bias, (tr, cbk))
            for k in range(4):
                o = k - 1
                sh = xe if o == 0 else pltpu.roll(xe, (-o) % ext, 0)
                acc = acc + jnp.where(_tap_valid(t, o, lc, n), sh[SUB:tr + SUB], 0.0) * w[k:k + 1]
            u_ref[pl.ds(r0, tr), :] = acc
            return carry

        lax.fori_loop(0, n // tr, chunk, 0)

    blocks = [((n, cbk), F32)] * 2 + [((4, cbk), F32), ((1, cbk), F32)]
    scratch = [((n + 2 * SUB, cbk), F32)]
    return pl.pallas_call(
        body, name=name, grid=(W // cbk,),
        in_specs=[pl.BlockSpec((n, cbk), lambda j: (0, j)), pl.BlockSpec((4, cbk), lambda j: (0, j)),
                  pl.BlockSpec((1, cbk), lambda j: (0, j))],
        out_specs=pl.BlockSpec((n, cbk), lambda j: (0, j)),
        out_shape=_sds((n, W), F32),
        scratch_shapes=[pltpu.VMEM(s, d) for s, d in scratch],
        compiler_params=_params(blocks, scratch, ("parallel",)),
    )(proj_all, cw, cb)


def _conv_bwd(du_all, proj_all, cw, dproj, lc, W, name):
    n = du_all.shape[0]
    cbk = _tile(W, CB_SEQ, LANE)
    tr = _tile(n, TR_CONV, SUB16)
    ext = tr + 2 * SUB

    def body(du_ref, x_ref, w_ref, dp_in, dx_ref, gw_ref, gb_ref, dp_ref, xp_ref):
        del dp_in
        for ref, src in ((dp_ref, du_ref), (xp_ref, x_ref)):
            ref[0:SUB, :] = jnp.zeros((SUB, cbk), F32)
            ref[n + SUB:n + 2 * SUB, :] = jnp.zeros((SUB, cbk), F32)
            ref[SUB:n + SUB, :] = src[...]
        w = w_ref[...]

        def fold(v):
            return jnp.sum(v.reshape(tr // SUB, SUB, cbk), axis=0)

        def chunk(ci, carry):
            r0 = pl.multiple_of(ci * tr, SUB16)
            de = dp_ref[pl.ds(r0, ext), :]
            xe = xp_ref[pl.ds(r0, ext), :]
            t = r0 + lax.broadcasted_iota(jnp.int32, (tr, cbk), 0)
            d0 = de[SUB:tr + SUB]
            dx = jnp.zeros((tr, cbk), F32)
            new = []
            for k in range(4):
                o = k - 1
                dsh = de if o == 0 else pltpu.roll(de, o % ext, 0)
                dx = dx + jnp.where(_tap_valid(t, -o, lc, n), dsh[SUB:tr + SUB], 0.0) * w[k:k + 1]
                xsh = xe if o == 0 else pltpu.roll(xe, (-o) % ext, 0)
                new.append(carry[k] + fold(d0 * jnp.where(_tap_valid(t, o, lc, n), xsh[SUB:tr + SUB], 0.0)))
            new.append(carry[4] + fold(d0))
            dx_ref[pl.ds(r0, tr), :] = dx.astype(dx_ref.dtype)
            return tuple(new)

        zero = jnp.zeros((SUB, cbk), F32)
        acc = lax.fori_loop(0, n // tr, chunk, (zero,) * 5)
        for k in range(4):
            gw_ref[k:k + 1, :] = jnp.sum(acc[k], axis=0, keepdims=True)
        gb_ref[...] = jnp.sum(acc[4], axis=0, keepdims=True)

    col = pl.BlockSpec((n, cbk), lambda j: (0, j))
    blocks = [((n, cbk), F32)] * 2 + [((n, cbk), MXU), ((4, cbk), F32), ((4, cbk), F32), ((1, cbk), F32)]
    scratch = [((n + 2 * SUB, cbk), F32)] * 2
    return pl.pallas_call(
        body, name=name, grid=(W // cbk,),
        in_specs=[col, col, pl.BlockSpec((4, cbk), lambda j: (0, j)), ANY],
        out_specs=[col, pl.BlockSpec((4, cbk), lambda j: (0, j)), pl.BlockSpec((1, cbk), lambda j: (0, j))],
        out_shape=[_sds(dproj.shape, dproj.dtype), _sds((4, W), F32), _sds((1, W), F32)],
        input_output_aliases={3: 0},
        scratch_shapes=[pltpu.VMEM(s, d) for s, d in scratch],
        compiler_params=_params(blocks, scratch, ("parallel",)),
    )(du_all, proj_all, cw, dproj)


def _gate_coeffs(ub, u, d, wr_ref, wi_ref, br_ref, bi_ref, lam_ref):
    c = -LRU_C * _softplus(-lam_ref[d:d + 1, :])
    r = _sigmoid(lax.dot_general(ub, wr_ref[d], NN, preferred_element_type=F32) + br_ref[d:d + 1, :])
    ig = _sigmoid(lax.dot_general(ub, wi_ref[d], NN, preferred_element_type=F32) + bi_ref[d:d + 1, :])
    la = c * r
    a = jnp.exp(la)
    sq = jnp.sqrt(-jnp.tanh(la) * (1.0 + a * a))
    return c, r, ig, a, sq


def _gate_specs(tl, hd):
    w_spec = pl.BlockSpec((2, None, hd, hd), lambda h, i: (0, h, 0, 0))
    v_spec = pl.BlockSpec((2, hd), lambda h, i: (0, h))
    return w_spec, v_spec


def _gates_fwd(u_all, wr, wi, br, bi, lam, name):
    n, W = u_all.shape
    heads, hd = wr.shape[1], wr.shape[2]
    tl = _tile(n, TL_GATES, SUB16)

    def body(u_ref, wr_ref, wi_ref, br_ref, bi_ref, lam_ref, a_ref, b_ref):
        u = u_ref[...]
        ub = u.astype(MXU)
        for d in range(2):
            _, _, ig, a, sq = _gate_coeffs(ub, u, d, wr_ref, wi_ref, br_ref, bi_ref, lam_ref)
            a_ref[d] = a
            b_ref[d] = sq * (ig * u)

    w_spec, v_spec = _gate_specs(tl, hd)
    o_spec = pl.BlockSpec((2, tl, hd), lambda h, i: (0, i, h))
    blocks = [((tl, hd), F32), ((2, hd, hd), MXU), ((2, hd, hd), MXU)] + [((2, hd), F32)] * 3 + [((2, tl, hd), F32)] * 2
    return pl.pallas_call(
        body, name=name, grid=(heads, n // tl),
        in_specs=[pl.BlockSpec((tl, hd), lambda h, i: (i, h)), w_spec, w_spec, v_spec, v_spec, v_spec],
        out_specs=[o_spec, o_spec], out_shape=[_sds((2, n, W), F32)] * 2,
        compiler_params=_params(blocks, dims=("parallel", "parallel")),
    )(u_all, wr, wi, br, bi, lam)


def _gates_bwd(u_all, da, db, wr, wi, br, bi, lam, name):
    n, W = u_all.shape
    heads, hd = wr.shape[1], wr.shape[2]
    tl = _tile(n, TL_GATES, SUB16)
    ni = n // tl

    def body(u_ref, da_ref, db_ref, wr_ref, wi_ref, br_ref, bi_ref, lam_ref,
             du_ref, gwr_ref, gwi_ref, gbr_ref, gbi_ref, gc_ref, accr_ref, acci_ref):
        i = pl.program_id(1)

        @pl.when(i == 0)
        def _():
            accr_ref[...] = jnp.zeros_like(accr_ref)
            acci_ref[...] = jnp.zeros_like(acci_ref)
            gbr_ref[...] = jnp.zeros_like(gbr_ref)
            gbi_ref[...] = jnp.zeros_like(gbi_ref)
            gc_ref[...] = jnp.zeros_like(gc_ref)

        u = u_ref[...]
        ub = u.astype(MXU)
        du = jnp.zeros_like(u)
        for d in range(2):
            c, r, ig, a, sq = _gate_coeffs(ub, u, d, wr_ref, wi_ref, br_ref, bi_ref, lam_ref)
            dbv = db_ref[d]
            t = dbv * sq
            du = du + t * ig
            d_la = da_ref[d] * a - (dbv * ig * u) * (a * a) / sq
            gc_ref[d:d + 1, :] += jnp.sum(d_la * r, axis=0, keepdims=True)
            d_pr = (d_la * c) * (r * (1.0 - r))
            d_pi = (t * u) * (ig * (1.0 - ig))
            gbr_ref[d:d + 1, :] += jnp.sum(d_pr, axis=0, keepdims=True)
            gbi_ref[d:d + 1, :] += jnp.sum(d_pi, axis=0, keepdims=True)
            pb = d_pr.astype(MXU)
            qb = d_pi.astype(MXU)
            du = du + lax.dot_general(pb, wr_ref[d], NT, preferred_element_type=F32)
            du = du + lax.dot_general(qb, wi_ref[d], NT, preferred_element_type=F32)
            accr_ref[d] += lax.dot_general(ub, pb, TN_DIMS, preferred_element_type=F32)
            acci_ref[d] += lax.dot_general(ub, qb, TN_DIMS, preferred_element_type=F32)
        du_ref[...] = du

        @pl.when(i == ni - 1)
        def _():
            gwr_ref[...] = accr_ref[...].astype(gwr_ref.dtype)
            gwi_ref[...] = acci_ref[...].astype(gwi_ref.dtype)

    w_spec, v_spec = _gate_specs(tl, hd)
    u_spec = pl.BlockSpec((tl, hd), lambda h, i: (i, h))
    ab_spec = pl.BlockSpec((2, tl, hd), lambda h, i: (0, i, h))
    blocks = ([((tl, hd), F32)] * 2 + [((2, tl, hd), F32)] * 2 + [((2, hd, hd), MXU)] * 4 + [((2, hd), F32)] * 6)
    scratch = [((2, hd, hd), F32)] * 2
    return pl.pallas_call(
        body, name=name, grid=(heads, ni),
        in_specs=[u_spec, ab_spec, ab_spec, w_spec, w_spec, v_spec, v_spec, v_spec],
        out_specs=[u_spec, w_spec, w_spec, v_spec, v_spec, v_spec],
        out_shape=[_sds((n, W), F32), _sds(wr.shape, MXU), _sds(wi.shape, MXU)] + [_sds((2, W), F32)] * 3,
        scratch_shapes=[pltpu.VMEM(s, d) for s, d in scratch],
        compiler_params=_params(blocks, scratch, ("parallel", "arbitrary")),
    )(u_all, da, db, wr, wi, br, bi, lam)


def _tile_scan(A, B, rows, reverse):
    for s in (1, 2, 4):
        if reverse:
            As, Bs, m = pltpu.roll(A, SUB - s, 0), pltpu.roll(B, SUB - s, 0), rows < SUB - s
        else:
            As, Bs, m = pltpu.roll(A, s, 0), pltpu.roll(B, s, 0), rows >= s
        B = jnp.where(m, A * Bs + B, B)
        A = jnp.where(m, A * As, A)
    return A, B


def _scan_chunks(n, lc):
    tc = _tile(lc, TL, SUB)
    assert n % tc == 0 and lc % tc == 0
    return tc, n // tc, lc // tc


def _scan_fwd(a_all, b_all, lc, name):
    _, n, W = a_all.shape
    cb = _tile(W, CB_SCAN, LANE)
    tc, nch, ncc = _scan_chunks(n, lc)
    ntile = tc // SUB

    def chunk(d, t):
        return jnp.where(d == 0, t, jnp.where(t < ncc, ncc - 1 - t, nch - 1 - (t - ncc)))

    def body(a_ref, b_ref, h_ref, carry_ref):
        rows = lax.broadcasted_iota(jnp.int32, (SUB, cb), 0)

        @pl.when(pl.program_id(2) == 0)
        def _():
            carry_ref[...] = jnp.zeros_like(carry_ref)

        def run(reverse):
            def step(i, h):
                r = pl.multiple_of(((ntile - 1 - i) if reverse else i) * SUB, SUB)
                A, B = _tile_scan(a_ref[pl.ds(r, SUB), :], b_ref[pl.ds(r, SUB), :], rows, reverse)
                H = A * h + B
                h_ref[pl.ds(r, SUB), :] = H
                return H[0:1, :] if reverse else H[SUB - 1:SUB, :]

            carry_ref[...] = lax.fori_loop(0, ntile, step, carry_ref[...], unroll=2)

        @pl.when(pl.program_id(1) == 0)
        def _():
            run(False)

        @pl.when(pl.program_id(1) == 1)
        def _():
            run(True)

    spec = pl.BlockSpec((None, tc, cb), lambda j, d, t: (d, chunk(d, t), j))
    return pl.pallas_call(
        body, name=name, grid=(W // cb, 2, nch), in_specs=[spec, spec], out_specs=spec,
        out_shape=_sds((2, n, W), F32), scratch_shapes=[pltpu.VMEM((1, cb), F32)],
        compiler_params=_params([((tc, cb), F32)] * 3, [((1, cb), F32)], ("parallel", "arbitrary", "arbitrary")),
    )(a_all, b_all)


def _scan_bwd(a_all, h_all, dya, lc, name):
    _, n, W = a_all.shape
    cb = _tile(W, CB_SCAN, LANE)
    tc, nch, ncc = _scan_chunks(n, lc)
    ntile = tc // SUB
    nl = nch - ncc

    def chunk(d, t):
        return jnp.where(d == 0, nch - 1 - t, jnp.where(t < nl, ncc + t, t - nl))

    def neighbour(d, t):
        c = chunk(d, t)
        below = jnp.maximum(c * ntile - 1, 0)
        above = jnp.where(c == nch - 1, 0, jnp.minimum((c + 1) * ntile, nch * ntile - 1))
        return jnp.where(d == 0, below, above)

    def body(a_ref, h_ref, hn_ref, g_ref, da_ref, db_ref, mu_ref):
        rows = lax.broadcasted_iota(jnp.int32, (SUB, cb), 0)
        d, t = pl.program_id(1), pl.program_id(2)
        c = chunk(d, t)
        has_g = c >= ncc

        @pl.when(t == 0)
        def _():
            mu_ref[...] = jnp.zeros_like(mu_ref)

        def tile(ref, j):
            return ref[pl.ds(pl.multiple_of(j * SUB, SUB), SUB), :]

        def run(up):
            if up:
                edge = jnp.where(c == ncc - 1, 0.0, hn_ref[0:1, :])
            else:
                edge = jnp.where(c > 0, hn_ref[SUB - 1:SUB, :], 0.0)

            def step(i, mu):
                j = i if up else ntile - 1 - i
                a_t = tile(a_ref, j)
                g_t = jnp.where(has_g, tile(g_ref, j), 0.0)
                if up:
                    ap = jnp.where(rows >= 1, pltpu.roll(a_t, 1, 0), 1.0)
                    nb_row = jnp.where(j < ntile - 1, tile(h_ref, jnp.minimum(j + 1, ntile - 1))[0:1, :], edge)
                    hprev = jnp.where(rows < SUB - 1, pltpu.roll(tile(h_ref, j), SUB - 1, 0), nb_row)
                else:
                    ap = jnp.where(rows < SUB - 1, pltpu.roll(a_t, SUB - 1, 0), 1.0)
                    nb_row = jnp.where(j > 0, tile(h_ref, jnp.maximum(j - 1, 0))[SUB - 1:SUB, :], edge)
                    hprev = jnp.where(rows >= 1, pltpu.roll(tile(h_ref, j), 1, 0), nb_row)
                A, B = _tile_scan(ap, g_t, rows, not up)
                lam = A * mu + B
                r = pl.multiple_of(j * SUB, SUB)
                da_ref[pl.ds(r, SUB), :] = lam * hprev
                db_ref[pl.ds(r, SUB), :] = lam
                return a_t[SUB - 1:SUB, :] * lam[SUB - 1:SUB, :] if up else a_t[0:1, :] * lam[0:1, :]

            mu_ref[...] = lax.fori_loop(0, ntile, step, mu_ref[...], unroll=2)

        @pl.when(d == 0)
        def _():
            run(False)

        @pl.when(d == 1)
        def _():
            run(True)

    spec = pl.BlockSpec((None, tc, cb), lambda j, d, t: (d, chunk(d, t), j))
    n_spec = pl.BlockSpec((None, SUB, cb), lambda j, d, t: (d, neighbour(d, t), j))
    g_spec = pl.BlockSpec((tc, cb), lambda j, d, t: (jnp.maximum(chunk(d, t) - ncc, 0), j))
    blocks = [((tc, cb), F32)] * 5 + [((SUB, cb), F32)]
    return pl.pallas_call(
        body, name=name, grid=(W // cb, 2, nch), in_specs=[spec, spec, n_spec, g_spec], out_specs=[spec, spec],
        out_shape=[_sds((2, n, W), F32)] * 2, scratch_shapes=[pltpu.VMEM((1, cb), F32)],
        compiler_params=_params(blocks, [((1, cb), F32)], ("parallel", "arbitrary", "arbitrary")),
    )(a_all, h_all, h_all, dya)


def _pool_window(v, w, tl, cb, transpose):
    half = w // 2
    pos = lax.broadcasted_iota(jnp.int32, (tl, cb), 0) % GRID_W
    cnt = (jnp.minimum(pos + half - 1, GRID_W - 1) - jnp.maximum(pos - half, 0) + 1).astype(F32)
    src = v / cnt if transpose else v

    def run_sum(s, step):
        span = 1
        while span < half:
            ok = (pos + span < GRID_W) if step > 0 else (pos - span >= 0)
            s = s + jnp.where(ok, pltpu.roll(s, (-step * span) % tl, 0), 0.0)
            span *= 2
        return s

    ahead, behind = run_sum(src, 1), run_sum(src, -1)
    if transpose:
        return behind + jnp.where(pos + 1 < GRID_W, pltpu.roll(ahead, tl - 1, 0), 0.0) - v
    return (ahead + jnp.where(pos >= 1, pltpu.roll(behind, 1, 0), 0.0)) / cnt - v


def _pool_z(src, row0, col0, L, W, transpose, dproj, name):
    G = len(POOL_WINDOWS)
    pd = W // G
    tl = _tile(L, TL, GRID_W)
    cb = _tile(pd, CB_POOL, LANE)
    assert row0 % tl == 0 and col0 % cb == 0
    rb, cbk = row0 // tl, col0 // cb
    nj = pd // cb

    def body(x_ref, *rest):
        o_ref = rest[-1]
        for gi, w in enumerate(POOL_WINDOWS):
            @pl.when(pl.program_id(0) == gi)
            def _(w=w):
                o_ref[...] = _pool_window(x_ref[...], w, tl, cb, transpose).astype(o_ref.dtype)

    plain = pl.BlockSpec((tl, cb), lambda g, i, j: (i, g * nj + j))
    window = pl.BlockSpec((tl, cb), lambda g, i, j: (i + rb, cbk + g * nj + j))
    blocks = [((tl, cb), F32), ((tl, cb), MXU)]
    if transpose:
        return pl.pallas_call(
            body, name=name, grid=(G, L // tl, nj), in_specs=[plain, ANY], out_specs=window,
            out_shape=_sds(dproj.shape, dproj.dtype), input_output_aliases={1: 0},
            compiler_params=_params(blocks, dims=("parallel",) * 3),
        )(src, dproj)
    return pl.pallas_call(
        body, name=name, grid=(G, L // tl, nj), in_specs=[window], out_specs=plain,
        out_shape=_sds((L, W), MXU),
        compiler_params=_params(blocks, dims=("parallel",) * 3),
    )(src)


def _mix_fwd(hs, proj_all, ypre, b_pool, pool_scale, lc, name):
    L, W = ypre.shape
    tl = _tile(L, TL, SUB16)
    cb = _tile(W, CB_MIX, LANE)
    nj = W // cb
    assert lc % tl == 0
    rb = lc // tl

    def body(hs_ref, ga_ref, yp_ref, gb_ref, bp_ref, ps_ref, o_ref):
        p = pl.program_id(2)

        @pl.when(p == 0)
        def _():
            g = ga_ref[...]
            o_ref[...] = ((hs_ref[0] + hs_ref[1]) * (g * _sigmoid(g))).astype(o_ref.dtype)

        @pl.when(p == 1)
        def _():
            g = gb_ref[...]
            yb = (yp_ref[...] + bp_ref[...]) * ps_ref[...]
            o_ref[...] = (yb * (g * _sigmoid(g))).astype(o_ref.dtype)

    vec = pl.BlockSpec((1, cb), lambda i, j, p: (0, j))
    blocks = [((2, tl, cb), F32)] + [((tl, cb), F32)] * 3 + [((tl, cb), MXU)]
    return pl.pallas_call(
        body, name=name, grid=(L // tl, nj, 2),
        in_specs=[pl.BlockSpec((2, tl, cb), lambda i, j, p: (0, i + rb, j)),
                  pl.BlockSpec((tl, cb), lambda i, j, p: (i + rb, 2 * nj + j)),
                  pl.BlockSpec((tl, cb), lambda i, j, p: (i, j)),
                  pl.BlockSpec((tl, cb), lambda i, j, p: (i + rb, 3 * nj + j)), vec, vec],
        out_specs=pl.BlockSpec((tl, cb), lambda i, j, p: (i, p * nj + j)),
        out_shape=_sds((L, 2 * W), MXU),
        compiler_params=_params(blocks, dims=("parallel", "parallel", "arbitrary")),
    )(hs, proj_all, ypre, proj_all, b_pool, pool_scale)


def _dsilu(g, sg):
    return sg * (1.0 + g * (1.0 - sg))


def _mixa_bwd(dmixed, hs, proj_all, dproj, lc, W, name):
    L = dmixed.shape[0]
    tl = _tile(L, TL, SUB16)
    cb = _tile(W, CB_MIX, LANE)
    nj = W // cb
    rb = lc // tl

    def body(dm_ref, hs_ref, ga_ref, dp_in, dya_ref, dga_ref):
        del dp_in
        g = ga_ref[...]
        sg = _sigmoid(g)
        dm = dm_ref[...]
        dya_ref[...] = dm * (g * sg)
        dga_ref[...] = (dm * (hs_ref[0] + hs_ref[1]) * _dsilu(g, sg)).astype(dga_ref.dtype)

    blocks = [((tl, cb), F32)] * 3 + [((2, tl, cb), F32), ((tl, cb), MXU)]
    return pl.pallas_call(
        body, name=name, grid=(L // tl, nj),
        in_specs=[pl.BlockSpec((tl, cb), lambda i, j: (i, j)),
                  pl.BlockSpec((2, tl, cb), lambda i, j: (0, i + rb, j)),
                  pl.BlockSpec((tl, cb), lambda i, j: (i + rb, 2 * nj + j)), ANY],
        out_specs=[pl.BlockSpec((tl, cb), lambda i, j: (i, j)),
                   pl.BlockSpec((tl, cb), lambda i, j: (i + rb, 2 * nj + j))],
        out_shape=[_sds((L, W), F32), _sds(dproj.shape, dproj.dtype)],
        input_output_aliases={3: 1},
        compiler_params=_params(blocks, dims=("parallel", "parallel")),
    )(dmixed, hs, proj_all, dproj)


def _mixb_bwd(dmixed, ypre, proj_all, b_pool, pool_scale, dproj, lc, W, name):
    L = dmixed.shape[0]
    tl = _tile(L, TL, SUB16)
    cb = _tile(W, CB_MIX, LANE)
    nj = W // cb
    rb = lc // tl

    def body(dm_ref, yp_ref, gb_ref, bp_ref, ps_ref, dp_in, dyp_ref, dgb_ref, gbp_ref, gps_ref):
        del dp_in
        i = pl.program_id(1)

        @pl.when(i == 0)
        def _():
            gbp_ref[...] = jnp.zeros_like(gbp_ref)
            gps_ref[...] = jnp.zeros_like(gps_ref)

        g = gb_ref[...]
        sg = _sigmoid(g)
        dm = dm_ref[...]
        yp = yp_ref[...] + bp_ref[...]
        ps = ps_ref[...]
        dyb = dm * (g * sg)
        dyp = dyb * ps
        dgb_ref[...] = (dm * (yp * ps) * _dsilu(g, sg)).astype(dgb_ref.dtype)
        dyp_ref[...] = dyp.astype(dyp_ref.dtype)
        gbp_ref[...] += jnp.sum(dyp, axis=0, keepdims=True)
        gps_ref[...] += jnp.sum(dyb * yp, axis=0, keepdims=True)

    vec = pl.BlockSpec((1, cb), lambda j, i: (0, j))
    blocks = [((tl, cb), F32)] * 3 + [((tl, cb), MXU)] * 2 + [((1, cb), F32)] * 4
    return pl.pallas_call(
        body, name=name, grid=(nj, L // tl),
        in_specs=[pl.BlockSpec((tl, cb), lambda j, i: (i, nj + j)),
                  pl.BlockSpec((tl, cb), lambda j, i: (i, j)),
                  pl.BlockSpec((tl, cb), lambda j, i: (i + rb, 3 * nj + j)), vec, vec, ANY],
        out_specs=[pl.BlockSpec((tl, cb), lambda j, i: (i, j)),
                   pl.BlockSpec((tl, cb), lambda j, i: (i + rb, 3 * nj + j)), vec, vec],
        out_shape=[_sds((L, W), MXU), _sds(dproj.shape, dproj.dtype), _sds((1, W), F32), _sds((1, W), F32)],
        input_output_aliases={5: 1},
        compiler_params=_params(blocks, dims=("parallel", "arbitrary")),
    )(dmixed, ypre, proj_all, b_pool, pool_scale, dproj)


def _dproj_init(n, lc, W, name):
    cb = _tile(W, CB_MIX, LANE)
    nj = W // cb

    def body(o_ref):
        o_ref[...] = jnp.zeros_like(o_ref)

    return pl.pallas_call(
        body, name=name, grid=(3 * nj,), in_specs=[],
        out_specs=pl.BlockSpec((lc, cb), lambda j: (0, nj + j)),
        out_shape=_sds((n, 4 * W), MXU),
        compiler_params=_params([((lc, cb), MXU)], dims=("parallel",)),
    )()


def _final(x2, out, tgt, gate, gfin, name):
    L, D = x2.shape
    tl = _tile(L, TL_FINAL, SUB16)

    def body(x_ref, o_ref, t_ref, gate_ref, g_ref, dout_ref, dxn_ref, loss_ref, ggf_ref, dgate_ref):
        i = pl.program_id(0)

        @pl.when(i == 0)
        def _():
            loss_ref[...] = jnp.zeros_like(loss_ref)
            ggf_ref[...] = jnp.zeros_like(ggf_ref)
            dgate_ref[...] = jnp.zeros_like(dgate_ref)

        o = o_ref[...]
        gate_v = gate_ref[...]
        gv = g_ref[...]
        xn = x_ref[...] + gate_v * o
        s = lax.rsqrt(jnp.mean(xn * xn, axis=-1, keepdims=True) + EPS)
        xh = xn * s
        err = xh * gv - t_ref[...]
        tok = jnp.mean(err * err, axis=-1, keepdims=True)
        loss_ref[...] += 0.5 * jnp.sum(tok, axis=0, keepdims=True)
        dy = err / D
        ggf_ref[...] += jnp.sum(dy * xh, axis=0, keepdims=True)
        dxh = dy * gv
        dxn = s * (dxh - xh * jnp.mean(dxh * xh, axis=-1, keepdims=True))
        dgate_ref[...] += jnp.sum(dxn * o, axis=0, keepdims=True)
        dout_ref[...] = (gate_v * dxn).astype(dout_ref.dtype)
        dxn_ref[...] = dxn

    row = pl.BlockSpec((tl, D), lambda i: (i, 0))
    vec = pl.BlockSpec((1, D), lambda i: (0, 0))
    blocks = [((tl, D), F32)] * 4 + [((tl, D), MXU)] + [((1, D), F32)] * 4
    return pl.pallas_call(
        body, name=name, grid=(L // tl,), in_specs=[row, row, row, vec, vec],
        out_specs=[row, row, pl.BlockSpec((1, 1), lambda i: (0, 0)), vec, vec],
        out_shape=[_sds((L, D), MXU), _sds((L, D), F32), _sds((1, 1), F32), _sds((1, D), F32), _sds((1, D), F32)],
        compiler_params=_params(blocks, dims=("arbitrary",)),
    )(x2, out, tgt, gate, gfin)


def _adamw_parts(w2, parts, m2, v2, name):
    R, C = w2.shape
    nh = len(parts)
    ch = C // nh
    tr = _tile(R, max(SUB16, (512 * 1024) // (ch * (nh + 1))), SUB16)

    def body(w_ref, *rest):
        p_refs = rest[:nh]
        m_ref, v_ref, g_ref, d_ref, nm_ref, nv_ref = rest[nh:]
        for q in range(nh):
            @pl.when(pl.program_id(1) == q)
            def _(p_ref=p_refs[q]):
                g = p_ref[0].astype(F32)
                for p in range(1, NDEV):
                    g = g + p_ref[p].astype(F32)
                delta, nm, nv = _adam(w_ref[...], g, m_ref[...], v_ref[...])
                g_ref[...] = g
                d_ref[...] = delta
                nm_ref[...] = nm
                nv_ref[...] = nv

    blk = pl.BlockSpec((tr, ch), lambda i, h: (i, h))
    p_spec = pl.BlockSpec((NDEV, tr, ch), lambda i, h: (0, i, 0))
    blocks = [((tr, ch), F32)] * 7 + [((NDEV, tr, ch), parts[0].dtype)] * nh
    return pl.pallas_call(
        body, name=name, grid=(R // tr, nh),
        in_specs=[blk] + [p_spec] * nh + [blk, blk],
        out_specs=[blk] * 4, out_shape=[_sds((R, C), F32)] * 4,
        compiler_params=_params(blocks, dims=("parallel", "arbitrary")),
    )(w2, *parts, m2, v2)


def _small_sum(vs, ga, gc, name):
    ns, nm = vs.shape[1], ga.shape[1]

    def body(v_ref, ga_ref, gc_ref, tot_ref, gb_ref):
        tot = v_ref[0:1, :]
        gb = ga_ref[0:1, :]
        for p in range(1, NDEV):
            tot = tot + v_ref[p:p + 1, :]
            gb = gb + ga_ref[p:p + 1, :]
        for p in range(NDEV):
            gb = gb + gc_ref[p:p + 1, :]
        tot_ref[...] = tot
        gb_ref[...] = gb

    blocks = [((NDEV, ns), F32), ((NDEV, nm), F32), ((NDEV, nm), F32), ((1, ns), F32), ((1, nm), F32)]
    return pl.pallas_call(
        body, name=name, out_shape=[_sds((1, ns), F32), _sds((1, nm), F32)],
        compiler_params=_params(blocks),
    )(vs, ga, gc)


def _adamw_small(g_raw, w, m, v, lam_range, cctx_range, name):
    npk = w.shape[1]

    def body(g_ref, w_ref, m_ref, v_ref, go_ref, d_ref, nm_ref, nv_ref):
        wv = w_ref[...]
        g = g_ref[...]
        idx = lax.broadcasted_iota(jnp.int32, (1, npk), 1)
        in_lam = (idx >= lam_range[0]) & (idx < lam_range[1])
        in_cc = (idx >= cctx_range[0]) & (idx < cctx_range[1])
        sg = _sigmoid_small(wv)
        g = jnp.where(in_lam, g * (LRU_C * _sigmoid_small(-wv)), jnp.where(in_cc, g * _dsilu(wv, sg), g))
        delta, nm, nv = _adam(wv, g, m_ref[...], v_ref[...])
        go_ref[...] = g
        d_ref[...] = delta
        nm_ref[...] = nm
        nv_ref[...] = nv

    return pl.pallas_call(
        body, name=name, out_shape=[_sds((1, npk), F32)] * 4,
        compiler_params=_params([((1, npk), F32)] * 8),
    )(g_raw, w, m, v)


def _pack(pieces):
    return jnp.concatenate([p.reshape(1, -1) for p in pieces], axis=1)


def kernel(x, c, ctx, c_ctx, w_ada, b_ada, g_norm, w_in, conv_w, conv_b, lru_lambda, w_rgate, b_rgate, w_igate, b_igate, w_pool, b_pool, pool_scale, w_out, g_final, loss_target, m_c_ctx, m_w_ada, m_b_ada, m_g_norm, m_w_in, m_conv_w, m_conv_b, m_lru_lambda, m_w_rgate, m_b_rgate, m_w_igate, m_b_igate, m_w_pool, m_b_pool, m_pool_scale, m_w_out, m_g_final, v_c_ctx, v_w_ada, v_b_ada, v_g_norm, v_w_in, v_conv_w, v_conv_b, v_lru_lambda, v_w_rgate, v_b_rgate, v_w_igate, v_b_igate, v_w_pool, v_b_pool, v_pool_scale, v_w_out, v_g_final):
    L, D = x.shape[1], x.shape[2]
    lc = ctx.shape[1]
    n = lc + L
    W = conv_b.shape[1]
    heads, hd = w_rgate.shape[2], w_rgate.shape[4]
    G, pd = w_pool.shape[1], w_pool.shape[3]
    na = w_ada.shape[2]
    nb = w_in.shape[2]
    ws = W // NDEV
    me = 4 * lax.axis_index("x") + 2 * lax.axis_index("y") + lax.axis_index("c")

    nbp = nb // WIN_PARTS
    w_in_parts = [w_in[0, :, q * nbp:(q + 1) * nbp].astype(MXU) for q in range(WIN_PARTS)]
    (win_0, cw_all, lam_all, br_all, bi_all, c_all) = _all_gather(
        [w_in_parts[0], conv_w[0], lru_lambda[0], b_rgate[0], b_igate[0], c], "gather_w_in")
    win = [win_0]
    cw = cw_all.transpose(1, 0, 2).reshape(4, W)
    lam = lam_all.transpose(1, 0, 2).reshape(2, W)
    br = br_all.transpose(1, 0, 2).reshape(2, W)
    bi = bi_all.transpose(1, 0, 2).reshape(2, W)

    cc = jnp.concatenate([c_all.reshape(NDEV, D), c_ctx.reshape(1, D), jnp.zeros((NDEV - 1, D), F32)], axis=0)
    b_loc = lax.dynamic_slice(b_ada, (0, me * na), (1, na))
    mod_loc, s_all = _ada_fwd(cc, w_ada[0], b_loc, "ada_fwd")
    (mod_all,) = _all_gather([mod_loc], "gather_mod")
    gate_w = [w_rgate[0].astype(MXU), w_igate[0].astype(MXU)]
    rest_w = [w_pool[0].astype(MXU), w_out[0].astype(MXU)]
    tok = mod_all
    sent_win = []
    for q in range(1, WIN_PARTS):
        part = [w_in_parts[q]]
        sent_win.append(_send_start(part, _place(part, False, f"place_w_in_{q}", [tok]), "level1", f"start_w_in_{q}"))
        tok = sent_win[-1][4]
    sent_gw = _send_start(gate_w, _place(gate_w, False, "place_gate_w", [tok]), False, "start_gate_w")
    sent_rw = _send_start(rest_w, _place(rest_w, False, "place_rest_w", [sent_gw[4]]), False, "start_rest_w")
    mod = mod_all.transpose(1, 0, 2).reshape(2 * NDEV, NDEV * na)
    mod_me = lax.dynamic_slice(mod, (me, 0), (1, 3 * D))
    shift, scale, gate = mod_me[:, :D], mod_me[:, D:2 * D], mod_me[:, 2 * D:]
    shift = _tie(shift, [sent_gw[4], sent_rw[4]], "tie_weights")
    shift_c, scale_c = mod[NDEV:NDEV + 1, :D], mod[NDEV:NDEV + 1, D:2 * D]

    x2, ctx2, tgt = x[0], ctx[0], loss_target[0]
    gfin = g_final.reshape(1, D)
    h_all = _norm_mod(x2, g_norm, shift, scale, n, lc, None, "norm_lat")
    h_all = _norm_mod(ctx2, g_norm, shift_c, scale_c, n, 0, h_all, "norm_ctx")
    proj_all = _mm_proj(h_all, win[0], 0, WIN_PARTS, None, "mm_proj_0")
    for q in range(1, WIN_PARTS):
        lands = _send_wait(sent_win[q - 1], proj_all, "level1", f"wait_w_in_{q}")
        passed = _send_start([], lands, "level2", f"pass_w_in_{q}")
        win.append(_send_wait(passed, proj_all, "level2", f"wait_pass_w_in_{q}")[0])
        proj_all = _mm_proj(h_all, win[q], q, WIN_PARTS, proj_all, f"mm_proj_{q}")
    u_all = _conv_fwd(proj_all, cw, conv_b, lc, W, "conv_fwd")
    wr_all, wi_all = _send_wait(sent_gw, u_all, False, "wait_gate_w")
    wr = wr_all.transpose(1, 2, 0, 3, 4).reshape(2, heads, hd, hd)
    wi = wi_all.transpose(1, 2, 0, 3, 4).reshape(2, heads, hd, hd)
    a_all, b_all = _gates_fwd(u_all, wr, wi, br, bi, lam, "gates_fwd")
    hs = _scan_fwd(a_all, b_all, lc, "scan_fwd")
    z = _pool_z(proj_all, lc, W, L, W, False, None, "pool_z")
    wpool_all, wout_all = _send_wait(sent_rw, hs, False, "wait_rest_w")
    wpool = wpool_all.transpose(1, 0, 2, 3).reshape(G, pd, pd)
    wout = wout_all.reshape(2 * W, D)
    ypre = _mm_group(z, wpool, "fwd", F32, "mm_pool")
    mixed = _mix_fwd(hs, proj_all, ypre, b_pool, pool_scale, lc, "mix_fwd")
    out = _mm_plain(mixed, wout, NN, F32, "mm_out")
    d_out, dxn, loss_p, ggf, dgate = _final(x2, out, tgt, gate, gfin, "final")

    dmixed = _mm_plain(d_out, wout, NT, F32, "mm_dmixed")
    gwout = _mm_plain(mixed, d_out, TN_DIMS, MXU, "mm_gwout")
    ex_o = [gwout.reshape(NDEV, 2 * W // NDEV, D)]
    sent_o = _send_start(ex_o, _place(ex_o, True, "place_gwout"), True, "start_gwout")
    dproj = _dproj_init(n, lc, W, "dproj_init")
    dya, dproj = _mixa_bwd(dmixed, hs, proj_all, dproj, lc, W, "mixa_bwd")
    dypre, dproj, gbp, gps = _mixb_bwd(dmixed, ypre, proj_all, _tie(b_pool, [sent_o[4]], "tie_gwout"), pool_scale,
                                       dproj, lc, W, "mixb_bwd")
    dz = _mm_group(dypre, wpool, "bwd", F32, "mm_dz")
    gwpool = _mm_group(z, dypre, "wgrad", MXU, "mm_gwpool")
    dproj = _pool_z(dz, lc, W, L, W, True, dproj, "pool_z_bwd")
    da, db = _scan_bwd(a_all, hs, dya, lc, "scan_bwd")
    du, gwr, gwi, gbr, gbi, gcl = _gates_bwd(u_all, da, db, wr, wi, br, bi, lam, "gates_bwd")
    ex_s = [gwpool.reshape(G, NDEV, pd // NDEV, pd).transpose(1, 0, 2, 3),
            gwr.reshape(2, heads, NDEV, hd // NDEV, hd).transpose(2, 0, 1, 3, 4),
            gwi.reshape(2, heads, NDEV, hd // NDEV, hd).transpose(2, 0, 1, 3, 4)]
    sent_s = _send_start(ex_s, _place(ex_s, True, "place_gsmall"), True, "start_gsmall")
    dproj, gcw, gcb = _conv_bwd(du, proj_all, _tie(cw, [sent_s[4]], "tie_gsmall"), dproj, lc, W, "conv_bwd")
    h_t = _transpose(h_all, "transpose_h")
    sent_i, tok = [], None
    for q in range(GWIN_PARTS):
        part = _mm_gwin(h_t, dproj, nb, q, GWIN_PARTS, f"mm_gwin_{q}", dep=tok)
        part = pltpu.with_memory_space_constraint(part, pltpu.HBM)
        sent_i.append(_send_start([part], _place([part], True, f"place_gwin_{q}"), True, f"start_gwin_{q}"))
        tok = sent_i[-1][4]
    dh_all = _mm_dh(dproj, win, "mm_dh", tok)
    grad_x, dshift, dscale, ggn = _norm_bwd(x2, dh_all, lc, g_norm, scale, dxn, jnp.zeros((1, D), F32), "norm_bwd_lat")
    _, dshift_c, dscale_c, ggn = _norm_bwd(ctx2, dh_all, 0, g_norm, scale_c, None, ggn, "norm_bwd_ctx")

    dmod_me = jnp.concatenate([dshift, dscale, dgate], axis=1)
    dmod_c = jnp.concatenate([dshift_c, dscale_c, jnp.zeros((1, D), F32)], axis=1)
    smalls = [ggf, ggn, gcw, gcb, gcl, gbr, gbi, gbp, gps, jnp.pad(loss_p, ((0, 0), (0, LANE - 1)))]
    sizes = [s.size for s in smalls]
    small_all, dmod_all, dmodc_all = _all_gather([_pack(smalls), dmod_me, dmod_c], "gather_small")
    ga = lax.dynamic_slice(dmod_all.reshape(NDEV, 3 * D), (0, me * na), (NDEV, na))
    gc = lax.dynamic_slice(dmodc_all.reshape(NDEV, 3 * D), (0, me * na), (NDEV, na))
    g_wada, d_wada, nm_wada, nv_wada, pc = _ada_bwd(s_all, ga, gc, w_ada[0], m_w_ada[0], v_w_ada[0], "ada_bwd")
    (pc_all,) = _all_gather([pc[0:1]], "gather_cctx")
    tot, gb_ada = _small_sum(
        jnp.concatenate([small_all.reshape(NDEV, -1), pc_all.reshape(NDEV, D)], axis=1),
        dmod_all.reshape(NDEV, 3 * D), dmodc_all.reshape(NDEV, 3 * D), "small_sum")
    offs = [0]
    for s in sizes + [D]:
        offs.append(offs[-1] + s)
    t_ggf, t_ggn, t_gcw, t_gcb, t_gcl, t_gbr, t_gbi, t_gbp, t_gps, t_loss, t_pc = [
        tot[:, offs[i]:offs[i + 1]] for i in range(len(offs) - 1)]

    def shard(t, rows):
        return lax.dynamic_slice(t.reshape(rows, W), (0, me * ws), (rows, ws))

    def big(wv, parts, mv, vv, name):
        shp = wv.shape
        C = shp[-1]
        if not isinstance(parts, list):
            parts = [parts]
        parts = [p.reshape(NDEV, -1, C // len(parts)) for p in parts]
        outs = _adamw_parts(wv.reshape(-1, C), parts, mv.reshape(-1, C), vv.reshape(-1, C), name)
        return [o.reshape(shp) for o in outs]

    (recv_o,) = _send_wait(sent_o, tot, True, "wait_gwout")
    recv_p, recv_r, recv_i = _send_wait(sent_s, tot, True, "wait_gsmall")
    r_wout = big(w_out, recv_o, m_w_out, v_w_out, "adamw_w_out")
    r_wpool = big(w_pool, recv_p, m_w_pool, v_w_pool, "adamw_w_pool")
    r_wr = big(w_rgate, recv_r, m_w_rgate, v_w_rgate, "adamw_w_rgate")
    r_wi = big(w_igate, recv_i, m_w_igate, v_w_igate, "adamw_w_igate")
    r_wada = [o.reshape(w_ada.shape) for o in (g_wada, d_wada, nm_wada, nv_wada)]

    names = ["c_ctx", "b_ada", "g_norm", "conv_w", "conv_b", "lru_lambda", "b_rgate", "b_igate", "b_pool",
             "pool_scale", "g_final"]
    sw = [c_ctx, b_ada, g_norm, conv_w, conv_b, lru_lambda, b_rgate, b_igate, b_pool, pool_scale, g_final]
    sm = [m_c_ctx, m_b_ada, m_g_norm, m_conv_w, m_conv_b, m_lru_lambda, m_b_rgate, m_b_igate, m_b_pool,
          m_pool_scale, m_g_final]
    sv = [v_c_ctx, v_b_ada, v_g_norm, v_conv_w, v_conv_b, v_lru_lambda, v_b_rgate, v_b_igate, v_b_pool,
          v_pool_scale, v_g_final]
    sg = [t_pc, gb_ada, t_ggn, shard(t_gcw, 4), t_gcb, shard(t_gcl, 2), shard(t_gbr, 2), shard(t_gbi, 2), t_gbp,
          t_gps, t_ggf]
    poffs = [0]
    for wv in sw:
        poffs.append(poffs[-1] + wv.size)
    lam_range = (poffs[5], poffs[6])
    cctx_range = (poffs[0], poffs[1])
    small_out = _adamw_small(_pack(sg), _pack(sw), _pack(sm), _pack(sv), lam_range, cctx_range, "adamw_small")
    recv_w = [_send_wait(sent_i[q], small_out[0], True, f"wait_gwin_{q}")[0] for q in range(GWIN_PARTS)]
    r_win = big(w_in, recv_w, m_w_in, v_w_in, "adamw_w_in")
    r_small = {}
    for i, nm in enumerate(names):
        r_small[nm] = [o[:, poffs[i]:poffs[i + 1]].reshape(sw[i].shape) for o in small_out]

    res = dict(r_small)
    res.update(w_ada=r_wada, w_in=r_win, w_rgate=r_wr, w_igate=r_wi, w_pool=r_wpool, w_out=r_wout)
    order = ["c_ctx", "w_ada", "b_ada", "g_norm", "w_in", "conv_w", "conv_b", "lru_lambda", "w_rgate", "b_rgate",
             "w_igate", "b_igate", "w_pool", "b_pool", "pool_scale", "w_out", "g_final"]
    loss = t_loss[0, 0]
    outs = [loss, grad_x.reshape(x.shape)]
    for q in range(4):
        outs += [res[nm][q] for nm in order]
    return tuple(outs)
```

```python
import functools

import jax
import jax.numpy as jnp
from jax import lax
from jax.experimental import pallas as pl
from jax.experimental.pallas import tpu as pltpu

NDEV = 8
GRID_W = 64
POOL_WINDOWS = (2, 4, 8, 16)
LRU_C = 8.0
EPS = 1e-6
ADAM_LR = 0.001
ADAM_B1 = 0.9
ADAM_B2 = 0.999
ADAM_EPS = 1e-08
ADAM_WD = 0.01
ADAM_STEP = 10

F32 = jnp.float32
MXU = jnp.bfloat16

VMEM_BYTES = 64 * 1024 * 1024
VMEM_SLACK = 8 * 1024 * 1024
SUB = 8
SUB16 = 16
LANE = 128

TM = 1152
TN = 1024
TK = 2048
TL = 256
TL_FINAL = 128
TL_GATES = 1088
CB_POOL = 1024
CB_SEQ = 256
CB_SCAN = 1024
CB_MIX = 2048
TR_CONV = 576
GWIN_PARTS = 4
WIN_PARTS = 2

MESH_ID = pl.DeviceIdType.MESH


def _tile(n, pref, align):
    if n <= pref:
        return n
    for t in range(pref - pref % align, 0, -align):
        if n % t == 0:
            return t
    return n


def _nbytes(shape, dtype):
    n = 1
    for s in shape:
        if s is not None:
            n *= s
    return n * jnp.dtype(dtype).itemsize


def _params(blocks, scratch=(), dims=None):
    need = 2 * sum(_nbytes(s, d) for s, d in blocks) + sum(_nbytes(s, d) for s, d in scratch) + VMEM_SLACK
    kw = dict(vmem_limit_bytes=int(min(max(need, 2 * VMEM_SLACK), VMEM_BYTES - VMEM_SLACK // 2)))
    if dims is not None:
        kw["dimension_semantics"] = dims
    return pltpu.CompilerParams(**kw)


def _sds(shape, dtype):
    return jax.ShapeDtypeStruct(tuple(shape), dtype)


ANY = pl.BlockSpec(memory_space=pl.ANY)


def _ids():
    return lax.axis_index("x"), lax.axis_index("y"), lax.axis_index("c")


def _sigmoid(v):
    return 0.5 * jnp.tanh(0.5 * v) + 0.5


def _sigmoid_small(v):
    return jax.nn.sigmoid(v)


def _softplus(v):
    return jnp.maximum(v, 0.0) + jnp.log1p(jnp.exp(-jnp.abs(v)))


def _all_gather(xs, name):
    n = len(xs)

    def body(*refs):
        x_refs, o_refs = refs[:n], refs[n:2 * n]
        send_sems, recv_sems, local_sems = refs[2 * n:]
        x, y, c = _ids()
        me, sibling = (x, y, c), (x, y, 1 - c)
        chips = [(1 - x, y), (x, 1 - y), (1 - x, 1 - y)]

        def slot(a, p):
            return o_refs[a].at[4 * p[0] + 2 * p[1] + p[2]]

        def copy(a, k, block, to, src=None):
            return pltpu.make_async_remote_copy(
                src_ref=slot(a, block) if src is None else src, dst_ref=slot(a, block),
                send_sem=send_sems.at[7 * a + k], recv_sem=recv_sems.at[7 * a + k],
                device_id=to, device_id_type=MESH_ID)

        mine, first, passed = [], [], []
        for a in range(n):
            m = pltpu.make_async_copy(x_refs[a], slot(a, me), local_sems.at[a])
            m.start()
            mine.append(m)
            f = [copy(a, 0, me, sibling, src=x_refs[a])]
            f += [copy(a, 1 + j, me, (*chip, c), src=x_refs[a]) for j, chip in enumerate(chips)]
            for cp in f:
                cp.start()
            first += f
        for a in range(n):
            for j, chip in enumerate(chips):
                copy(a, 1 + j, (*chip, c), me).wait_recv()
                p = copy(a, 4 + j, (*chip, c), sibling)
                p.start()
                passed.append(p)
        for a in range(n):
            copy(a, 0, sibling, me).wait_recv()
            for j, chip in enumerate(chips):
                copy(a, 4 + j, (*chip, 1 - c), me).wait_recv()
        for cp in first + passed:
            cp.wait_send()
        for m in mine:
            m.wait()

    return pl.pallas_call(
        body, name=name,
        out_shape=[_sds((NDEV,) + v.shape, v.dtype) for v in xs],
        in_specs=[ANY] * n, out_specs=[ANY] * n,
        scratch_shapes=[pltpu.SemaphoreType.DMA((7 * n,)), pltpu.SemaphoreType.DMA((7 * n,)),
                        pltpu.SemaphoreType.DMA((n,))],
    )(*xs)


HBM = pl.BlockSpec(memory_space=pltpu.HBM)
SEM = pl.BlockSpec(memory_space=pltpu.SEMAPHORE)
EFFECT = pltpu.SideEffectType.DATAFLOW_SIDE_EFFECTING


def _peers():
    x, y, c = _ids()
    out = []
    for k in range(1, NDEV):
        px = 1 - x if k & 4 else x
        py = 1 - y if k & 2 else y
        pc = 1 - c if k & 1 else c
        out.append(((px, py, pc), 4 * px + 2 * py + pc))
    return out, 4 * x + 2 * y + c


def _tie(v, deps, name):
    def body(v_ref, *rest):
        rest[-1][...] = v_ref[...]

    vmem = pl.BlockSpec(memory_space=pltpu.VMEM)
    return pl.pallas_call(
        body, name=name, out_shape=_sds(v.shape, v.dtype), in_specs=[vmem] + [ANY] * len(deps), out_specs=vmem,
    )(v, *deps)


def _place(srcs, from_slot, name, deps=()):
    n = len(srcs)
    blks = [v.shape[1:] if from_slot else v.shape for v in srcs]

    nd = len(deps)

    def body(*refs):
        s_refs, l_refs = refs[:n], refs[n + nd:2 * n + nd]
        bufs, sems = refs[2 * n + nd:3 * n + nd], refs[3 * n + nd]
        x, y, c = _ids()
        me = 4 * x + 2 * y + c
        ins = [pltpu.make_async_copy(s_refs[a].at[me] if from_slot else s_refs[a], bufs[a], sems.at[a])
               for a in range(n)]
        outs = [pltpu.make_async_copy(bufs[a], l_refs[a].at[me], sems.at[n + a]) for a in range(n)]
        for cp in ins:
            cp.start()
        for a in range(n):
            ins[a].wait()
            outs[a].start()
        for cp in outs:
            cp.wait()

    scratch = [(b, v.dtype) for b, v in zip(blks, srcs)]
    return pl.pallas_call(
        body, name=name, out_shape=[_sds((NDEV,) + b, v.dtype) for b, v in zip(blks, srcs)],
        in_specs=[ANY] * (n + nd), out_specs=[ANY] * n,
        scratch_shapes=[pltpu.VMEM(b, d) for b, d in scratch] + [pltpu.SemaphoreType.DMA((2 * n,))],
        compiler_params=_params([], scratch),
    )(*srcs, *deps)


SEND_PEERS = {True: 7, False: 7, "level1": 4, "level2": 3}


def _send_copies(s_refs, l_refs, ssem, rsem, mode, receiving):
    peers, me = _peers()
    x, y, c = _ids()
    sibling = (x, y, 1 - c)
    chips = [(1 - x, y), (x, 1 - y), (1 - x, 1 - y)]
    npeer = SEND_PEERS[mode]
    out = []
    for a in range(len(l_refs)):
        if mode == "level2":
            for k, (px, py) in enumerate(chips):
                slot = 4 * px + 2 * py + (1 - c if receiving else c)
                out.append(pltpu.make_async_remote_copy(
                    src_ref=l_refs[a].at[slot], dst_ref=l_refs[a].at[slot], send_sem=ssem.at[npeer * a + k],
                    recv_sem=rsem.at[npeer * a + k], device_id=sibling, device_id_type=MESH_ID))
            continue
        targets = peers
        if mode == "level1":
            targets = [(sibling, 4 * x + 2 * y + 1 - c)] + [((px, py, c), 4 * px + 2 * py + c) for px, py in chips]
        for k, (dev, idx) in enumerate(targets):
            out.append(pltpu.make_async_remote_copy(
                src_ref=s_refs[a].at[idx] if mode is True else s_refs[a],
                dst_ref=l_refs[a].at[idx if receiving else me],
                send_sem=ssem.at[npeer * a + k], recv_sem=rsem.at[npeer * a + k], device_id=dev, device_id_type=MESH_ID))
    return out


def _send_start(srcs, lands, mode, name):
    ns, n = len(srcs), len(lands)
    nsem = SEND_PEERS[mode] * n

    def body(*refs):
        s_refs, l_refs = refs[:ns], refs[ns:ns + n]
        ssem, rsem = refs[ns + n], refs[ns + n + 1]
        token = refs[-1]
        for send in _send_copies(s_refs, l_refs, ssem, rsem, mode, False):
            send.start()
        token[...] = jnp.zeros_like(token)

    bufs = list(srcs) + list(lands)
    outs = pl.pallas_call(
        body, name=name,
        out_shape=[pltpu.SemaphoreType.DMA((nsem,)), pltpu.SemaphoreType.DMA((nsem,))]
        + [pltpu.HBM(v.shape, v.dtype) for v in bufs] + [_sds((SUB, LANE), F32)],
        in_specs=[HBM] * (ns + n), out_specs=[SEM, SEM] + [HBM] * (ns + n) + [pl.BlockSpec(memory_space=pltpu.VMEM)],
        input_output_aliases={i: 2 + i for i in range(ns + n)},
        compiler_params=pltpu.CompilerParams(has_side_effects=EFFECT),
    )(*[pltpu.with_memory_space_constraint(v, pltpu.HBM) for v in bufs])
    return outs[0], outs[1], list(outs[2:2 + ns]), list(outs[2 + ns:2 + ns + n]), outs[-1]


def _send_wait(started, after, mode, name):
    ssem, rsem, srcs, lands, _ = started
    ns, n = len(srcs), len(lands)

    def body(*refs):
        s_refs, l_refs = refs[:ns], refs[ns:ns + n]
        ssem_ref, rsem_ref = refs[ns + n], refs[ns + n + 1]
        for recv in _send_copies(s_refs, l_refs, ssem_ref, rsem_ref, mode, True):
            recv.wait_send()
            recv.wait_recv()

    bufs = list(srcs) + list(lands)
    outs = pl.pallas_call(
        body, name=name, out_shape=[pltpu.HBM(v.shape, v.dtype) for v in bufs],
        in_specs=[HBM] * (ns + n) + [SEM, SEM, ANY], out_specs=[HBM] * (ns + n),
        input_output_aliases={i: i for i in range(ns + n)},
        compiler_params=pltpu.CompilerParams(has_side_effects=EFFECT),
    )(*bufs, ssem, rsem, after)
    return list(outs[ns:])


NN = (((1,), (0,)), ((), ()))
NT = (((1,), (1,)), ((), ()))
TN_DIMS = (((0,), (0,)), ((), ()))


def _mm(a, b, *, grid, a_spec, b_spec, o_spec, out_shape, acc_shape, dims, name, dep=None, fill=None):
    k_axis = len(grid) - 1
    nk = grid[k_axis]
    extra = [v for v in (dep, fill) if v is not None]
    aliases = {} if fill is None else {1 + len(extra): 0}

    def body(a_ref, b_ref, *rest):
        o_ref, acc_ref = rest[-2], rest[-1]
        k = pl.program_id(k_axis)

        def prod():
            return lax.dot_general(a_ref[...], b_ref[...], dims, preferred_element_type=F32)

        if nk == 1:
            o_ref[...] = prod().astype(o_ref.dtype)
            return

        @pl.when(k == 0)
        def _():
            acc_ref[...] = prod()

        if nk > 2:
            @pl.when((k > 0) & (k < nk - 1))
            def _():
                acc_ref[...] += prod()

        @pl.when(k == nk - 1)
        def _():
            o_ref[...] = (acc_ref[...] + prod()).astype(o_ref.dtype)

    blocks = [(a_spec.block_shape, a.dtype), (b_spec.block_shape, b.dtype), (o_spec.block_shape, out_shape.dtype)]
    return pl.pallas_call(
        body, name=name, grid=grid, in_specs=[a_spec, b_spec] + [ANY] * len(extra), out_specs=o_spec,
        out_shape=out_shape, scratch_shapes=[pltpu.VMEM(acc_shape, F32)], input_output_aliases=aliases,
        compiler_params=_params(blocks, [(acc_shape, F32)], ("parallel",) * k_axis + ("arbitrary",)),
    )(a, b, *extra)


def _mm_plain(a, b, dims, out_dtype, name):
    if dims == TN_DIMS:
        (K, M), N = a.shape, b.shape[1]
    elif dims == NT:
        (M, K), N = a.shape, b.shape[0]
    else:
        (M, K), N = a.shape, b.shape[1]
    tm, tn = _tile(M, TM, LANE), _tile(N, TN, LANE)
    tk = _tile(K, TK, LANE if dims != TN_DIMS else SUB16)
    if dims == TN_DIMS:
        a_spec = pl.BlockSpec((tk, tm), lambda i, j, k: (k, i))
    else:
        a_spec = pl.BlockSpec((tm, tk), lambda i, j, k: (i, k))
    if dims == NT:
        b_spec = pl.BlockSpec((tn, tk), lambda i, j, k: (j, k))
    else:
        b_spec = pl.BlockSpec((tk, tn), lambda i, j, k: (k, j))
    return _mm(a, b, grid=(M // tm, N // tn, K // tk), a_spec=a_spec, b_spec=b_spec,
               o_spec=pl.BlockSpec((tm, tn), lambda i, j, k: (i, j)),
               out_shape=_sds((M, N), out_dtype), acc_shape=(tm, tn), dims=dims, name=name)


def _mm_proj(h_all, win_q, q, nparts, fill, name):
    n, D = h_all.shape
    nbp = win_q.shape[2]
    nb = nbp * nparts
    tm, tn, tk = _tile(n, TM, SUB16), _tile(nbp, TN, LANE), _tile(D, TK, LANE)
    nbn = nbp // tn
    return _mm(h_all, win_q, grid=(n // tm, NDEV * nbn, D // tk),
               a_spec=pl.BlockSpec((tm, tk), lambda i, j, k: (i, k)),
               b_spec=pl.BlockSpec((None, tk, tn), lambda i, j, k: (j // nbn, k, j % nbn)),
               o_spec=pl.BlockSpec((tm, tn), lambda i, j, k: (i, (j // nbn) * (nb // tn) + q * nbn + j % nbn)),
               out_shape=_sds((n, NDEV * nb), F32), acc_shape=(tm, tn), dims=NN, name=name, fill=fill)


def _mm_dh(dproj, wins, name, dep):
    nparts = len(wins)
    n = dproj.shape[0]
    _, D, nbp = wins[0].shape
    nb = nbp * nparts
    tm, tn, tk = _tile(n, TM, SUB16), _tile(D, 2 * TN, LANE), _tile(nbp, TK // 2, LANE)
    nbk = nbp // tk
    kq = NDEV * nbk
    nk = nparts * kq

    def a_index(i, j, k):
        r = k % kq
        return i, (r // nbk) * (nb // tk) + (k // kq) * nbk + r % nbk

    def b_index(q):
        def index(i, j, k):
            r = jnp.clip(k - q * kq, 0, kq - 1)
            return r // nbk, j, r % nbk
        return index

    def body(a_ref, *rest):
        b_refs, o_ref, acc_ref = rest[:nparts], rest[-2], rest[-1]
        k = pl.program_id(2)
        for q in range(nparts):
            def prod(b_ref=b_refs[q]):
                return lax.dot_general(a_ref[...], b_ref[...], NT, preferred_element_type=F32)

            lo, hi = q * kq + (q == 0), (q + 1) * kq - (q == nparts - 1)
            if q == 0:
                @pl.when(k == 0)
                def _(prod=prod):
                    acc_ref[...] = prod()

            if hi > lo:
                @pl.when((k >= lo) & (k < hi))
                def _(prod=prod):
                    acc_ref[...] += prod()

            if q == nparts - 1:
                @pl.when(k == nk - 1)
                def _(prod=prod):
                    o_ref[...] = acc_ref[...] + prod()

    assert nk >= 2
    blocks = [((tm, tk), dproj.dtype)] + [((tn, tk), wins[0].dtype)] * nparts + [((tm, tn), F32)]
    return pl.pallas_call(
        body, name=name, grid=(n // tm, D // tn, nk),
        in_specs=[pl.BlockSpec((tm, tk), a_index)]
        + [pl.BlockSpec((None, tn, tk), b_index(q)) for q in range(nparts)] + [ANY],
        out_specs=pl.BlockSpec((tm, tn), lambda i, j, k: (i, j)), out_shape=_sds((n, D), F32),
        scratch_shapes=[pltpu.VMEM((tm, tn), F32)],
        compiler_params=_params(blocks, [((tm, tn), F32)], ("parallel", "parallel", "arbitrary")),
    )(dproj, *wins, dep)


def _transpose(x, name):
    R, C = x.shape
    tr, tc = _tile(R, TL, LANE), _tile(C, 2 * TL, LANE)

    def body(x_ref, o_ref):
        o_ref[...] = x_ref[...].T

    return pl.pallas_call(
        body, name=name, grid=(R // tr, C // tc),
        in_specs=[pl.BlockSpec((tr, tc), lambda i, j: (i, j))],
        out_specs=pl.BlockSpec((tc, tr), lambda i, j: (j, i)),
        out_shape=_sds((C, R), x.dtype),
        compiler_params=_params([((tr, tc), x.dtype)] * 2, dims=("parallel", "parallel")),
    )(x)


def _mm_gwin(h_t, dproj, nb, part, nparts, name, dep=None):
    D, n = h_t.shape
    nbp = nb // nparts
    tm, tn, tk = _tile(D, TM, LANE), _tile(nbp, TN, LANE), n
    nbn = nbp // tn
    return _mm(h_t, dproj, grid=(D // tm, NDEV * nbn, n // tk),
               a_spec=pl.BlockSpec((tm, tk), lambda i, j, k: (i, k)),
               b_spec=pl.BlockSpec((tk, tn), lambda i, j, k: (k, (j // nbn) * (nb // tn) + part * nbn + j % nbn)),
               o_spec=pl.BlockSpec((None, tm, tn), lambda i, j, k: (j // nbn, i, j % nbn)),
               out_shape=_sds((NDEV, D, nbp), MXU), acc_shape=(tm, tn), dims=NN, name=name, dep=dep)


def _mm_group(a, b, mode, out_dtype, name):
    if mode == "wgrad":
        L, W = a.shape
        G = len(POOL_WINDOWS)
        pd = W // G
        tm, tn, tk = _tile(pd, TM, LANE), _tile(pd, TN, LANE), _tile(L, TK, SUB16)
        nm, nn = pd // tm, pd // tn
        return _mm(a, b, grid=(G, nm, nn, L // tk),
                   a_spec=pl.BlockSpec((tk, tm), lambda g, i, j, k: (k, g * nm + i)),
                   b_spec=pl.BlockSpec((tk, tn), lambda g, i, j, k: (k, g * nn + j)),
                   o_spec=pl.BlockSpec((None, tm, tn), lambda g, i, j, k: (g, i, j)),
                   out_shape=_sds((G, pd, pd), out_dtype), acc_shape=(tm, tn), dims=TN_DIMS, name=name)
    L, W = a.shape
    G, pd, _ = b.shape
    tm, tn, tk = _tile(L, TM, SUB16), _tile(pd, TN, LANE), _tile(pd, TK, LANE)
    nn, nk = pd // tn, pd // tk
    if mode == "fwd":
        b_spec = pl.BlockSpec((None, tk, tn), lambda g, i, j, k: (g, k, j))
        dims = NN
    else:
        b_spec = pl.BlockSpec((None, tn, tk), lambda g, i, j, k: (g, j, k))
        dims = NT
    return _mm(a, b, grid=(G, L // tm, nn, nk),
               a_spec=pl.BlockSpec((tm, tk), lambda g, i, j, k: (i, g * nk + k)),
               b_spec=b_spec,
               o_spec=pl.BlockSpec((tm, tn), lambda g, i, j, k: (i, g * nn + j)),
               out_shape=_sds((L, W), out_dtype), acc_shape=(tm, tn), dims=dims, name=name)


def _ada_fwd(cc, w_loc, b_loc, name):
    R, D = cc.shape
    na = w_loc.shape[1]
    tk = _tile(D, 512, LANE)

    def body(c_ref, w_ref, b_ref, mod_ref, s_ref):
        k = pl.program_id(0)
        cv = c_ref[...]
        s = cv * _sigmoid_small(cv)
        s_ref[...] = s

        @pl.when(k == 0)
        def _():
            mod_ref[...] = jnp.broadcast_to(b_ref[...], mod_ref.shape)

        mod_ref[...] += lax.dot_general(s.astype(MXU), w_ref[...].astype(MXU), NN, preferred_element_type=F32)

    blocks = [((R, tk), F32), ((tk, na), F32), ((1, na), F32), ((R, na), F32), ((R, tk), F32)]
    return pl.pallas_call(
        body, name=name, grid=(D // tk,),
        in_specs=[pl.BlockSpec((R, tk), lambda k: (0, k)), pl.BlockSpec((tk, na), lambda k: (k, 0)),
                  pl.BlockSpec((1, na), lambda k: (0, 0))],
        out_specs=[pl.BlockSpec((R, na), lambda k: (0, 0)), pl.BlockSpec((R, tk), lambda k: (0, k))],
        out_shape=[_sds((R, na), F32), _sds((R, D), F32)],
        compiler_params=_params(blocks, dims=("arbitrary",)),
    )(cc, w_loc, b_loc)


def _adam(w, g, m, v):
    m = ADAM_B1 * m + (1.0 - ADAM_B1) * g
    v = ADAM_B2 * v + (1.0 - ADAM_B2) * (g * g)
    m_hat = m / (1.0 - ADAM_B1 ** ADAM_STEP)
    v_hat = v / (1.0 - ADAM_B2 ** ADAM_STEP)
    delta = -ADAM_LR * (m_hat / (jnp.sqrt(v_hat) + ADAM_EPS) + ADAM_WD * w)
    return delta, m, v


def _ada_bwd(s_all, ga, gc, w_loc, m_loc, v_loc, name):
    D, na = w_loc.shape
    tr = _tile(D, 256, LANE)

    def body(s_ref, ga_ref, gc_ref, w_ref, m_ref, v_ref, g_ref, d_ref, nm_ref, nv_ref, pc_ref):
        dmc = gc_ref[0:1, :]
        for p in range(1, NDEV):
            dmc = dmc + gc_ref[p:p + 1, :]
        rows = lax.broadcasted_iota(jnp.int32, (NDEV, na), 0)
        dmc8 = jnp.where(rows == 0, jnp.broadcast_to(dmc, (NDEV, na)), 0.0)
        dm = jnp.concatenate([ga_ref[...], dmc8], axis=0).astype(MXU)
        dmc16 = jnp.concatenate([dmc8, jnp.zeros_like(dmc8)], axis=0).astype(MXU)
        w = w_ref[...]
        g = lax.dot_general(s_ref[...].astype(MXU), dm, TN_DIMS, preferred_element_type=F32)
        pc_ref[...] = lax.dot_general(dmc16, w.astype(MXU), NT, preferred_element_type=F32)
        delta, nm, nv = _adam(w, g, m_ref[...], v_ref[...])
        g_ref[...] = g
        d_ref[...] = delta
        nm_ref[...] = nm
        nv_ref[...] = nv

    big = pl.BlockSpec((tr, na), lambda i: (i, 0))
    full = pl.BlockSpec((NDEV, na), lambda i: (0, 0))
    srow = pl.BlockSpec((2 * NDEV, tr), lambda i: (0, i))
    blocks = [((2 * NDEV, tr), F32)] * 2 + [((NDEV, na), F32)] * 2 + [((tr, na), F32)] * 7
    return pl.pallas_call(
        body, name=name, grid=(D // tr,),
        in_specs=[srow, full, full, big, big, big],
        out_specs=[big, big, big, big, srow],
        out_shape=[_sds((D, na), F32)] * 4 + [_sds((2 * NDEV, D), F32)],
        compiler_params=_params(blocks, dims=("parallel",)),
    )(s_all, ga, gc, w_loc, m_loc, v_loc)


def _norm_mod(x2, g, shift, scale, n, row0, h_prev, name):
    R, D = x2.shape
    tl = _tile(R, TL, SUB16)
    assert row0 % tl == 0
    b0 = row0 // tl

    def body(x_ref, g_ref, sh_ref, sc_ref, *rest):
        o_ref = rest[-1]
        xv = x_ref[...]
        s = lax.rsqrt(jnp.mean(xv * xv, axis=-1, keepdims=True) + EPS)
        nrm = xv * s * g_ref[...]
        o_ref[...] = (nrm * (1.0 + sc_ref[...]) + sh_ref[...]).astype(o_ref.dtype)

    vec = pl.BlockSpec((1, D), lambda i: (0, 0))
    in_specs = [pl.BlockSpec((tl, D), lambda i: (i, 0)), vec, vec, vec]
    args = [x2, g, shift, scale]
    aliases = {}
    if h_prev is not None:
        in_specs.append(ANY)
        args.append(h_prev)
        aliases = {4: 0}
    blocks = [((tl, D), F32), ((tl, D), MXU)] + [((1, D), F32)] * 3
    return pl.pallas_call(
        body, name=name, grid=(R // tl,), in_specs=in_specs,
        out_specs=pl.BlockSpec((tl, D), lambda i: (i + b0, 0)),
        out_shape=_sds((n, D), MXU), input_output_aliases=aliases,
        compiler_params=_params(blocks, dims=("parallel",)),
    )(*args)


def _norm_bwd(x2, dh_all, row0, g, scale, dxn, ggn0, name):
    R, D = x2.shape
    tl = _tile(R, TL_FINAL, SUB)
    assert row0 % tl == 0
    b0 = row0 // tl
    with_x = dxn is not None

    def body(*refs):
        if with_x:
            x_ref, dh_ref, g_ref, sc_ref, gg0_ref, dxn_ref, gx_ref, dsh_ref, dsc_ref, gg_ref = refs
        else:
            x_ref, dh_ref, g_ref, sc_ref, gg0_ref, dsh_ref, dsc_ref, gg_ref = refs
        i = pl.program_id(0)

        @pl.when(i == 0)
        def _():
            dsh_ref[...] = jnp.zeros_like(dsh_ref)
            dsc_ref[...] = jnp.zeros_like(dsc_ref)
            gg_ref[...] = gg0_ref[...]

        xv = x_ref[...]
        dh = dh_ref[...]
        gv = g_ref[...]
        s = lax.rsqrt(jnp.mean(xv * xv, axis=-1, keepdims=True) + EPS)
        xh = xv * s
        dsh_ref[...] += jnp.sum(dh, axis=0, keepdims=True)
        dsc_ref[...] += jnp.sum(dh * (xh * gv), axis=0, keepdims=True)
        dn = dh * (1.0 + sc_ref[...])
        gg_ref[...] += jnp.sum(dn * xh, axis=0, keepdims=True)
        if with_x:
            dxh = dn * gv
            dx = s * (dxh - xh * jnp.mean(dxh * xh, axis=-1, keepdims=True))
            gx_ref[...] = dx + dxn_ref[...]

    vec = pl.BlockSpec((1, D), lambda i: (0, 0))
    row = pl.BlockSpec((tl, D), lambda i: (i, 0))
    in_specs = [row, pl.BlockSpec((tl, D), lambda i: (i + b0, 0)), vec, vec, vec]
    args = [x2, dh_all, g, scale, ggn0]
    out_specs = [vec, vec, vec]
    out_shape = [_sds((1, D), F32)] * 3
    if with_x:
        in_specs.append(row)
        args.append(dxn)
        out_specs = [row] + out_specs
        out_shape = [_sds((R, D), F32)] + out_shape
    blocks = [((tl, D), F32)] * (4 if with_x else 2) + [((1, D), F32)] * 6
    outs = pl.pallas_call(
        body, name=name, grid=(R // tl,), in_specs=in_specs, out_specs=out_specs, out_shape=out_shape,
        compiler_params=_params(blocks, dims=("arbitrary",)),
    )(*args)
    return tuple(outs) if with_x else (None,) + tuple(outs)


def _tap_valid(t, o, lc, n):
    tt = t + o
    in_ctx = t < lc
    return (tt >= jnp.where(in_ctx, 0, lc)) & (tt < jnp.where(in_ctx, lc, n))


def _conv_fwd(proj_all, cw, cb, lc, W, name):
    n = proj_all.shape[0]
    cbk = _tile(W, CB_SEQ, LANE)
    tr = _tile(n, TR_CONV, SUB16)
    ext = tr + 2 * SUB

    def body(x_ref, w_ref, b_ref, u_ref, xp_ref):
        xp_ref[0:SUB, :] = jnp.zeros((SUB, cbk), F32)
        xp_ref[n + SUB:n + 2 * SUB, :] = jnp.zeros((SUB, cbk), F32)
        xp_ref[SUB:n + SUB, :] = x_ref[...]
        w = w_ref[...]
        bias = b_ref[...]

        def chunk(ci, carry):
            r0 = pl.multiple_of(ci * tr, SUB16)
            xe = xp_ref[pl.ds(r0, ext), :]
            t = r0 + lax.broadcasted_iota(jnp.int32, (tr, cbk), 0)
            acc = jnp.broadcast_to(bias, (tr, cbk))
            for k in range(4):
                o = k - 1
                sh = xe if o == 0 else pltpu.roll(xe, (-o) % ext, 0)
                acc = acc + jnp.where(_tap_valid(t, o, lc, n), sh[SUB:tr + SUB], 0.0) * w[k:k + 1]
            u_ref[pl.ds(r0, tr), :] = acc
            return carry

        lax.fori_loop(0, n // tr, chunk, 0)

    blocks = [((n, cbk), F32)] * 2 + [((4, cbk), F32), ((1, cbk), F32)]
    scratch = [((n + 2 * SUB, cbk), F32)]
    return pl.pallas_call(
        body, name=name, grid=(W // cbk,),
        in_specs=[pl.BlockSpec((n, cbk), lambda j: (0, j)), pl.BlockSpec((4, cbk), lambda j: (0, j)),
                  pl.BlockSpec((1, cbk), lambda j: (0, j))],
        out_specs=pl.BlockSpec((n, cbk), lambda j: (0, j)),
        out_shape=_sds((n, W), F32),
        scratch_shapes=[pltpu.VMEM(s, d) for s, d in scratch],
        compiler_params=_params(blocks, scratch, ("parallel",)),
    )(proj_all, cw, cb)


def _conv_bwd(du_all, proj_all, cw, dproj, lc, W, name):
    n = du_all.shape[0]
    cbk = _tile(W, CB_SEQ, LANE)
    tr = _tile(n, TR_CONV, SUB16)
    ext = tr + 2 * SUB

    def body(du_ref, x_ref, w_ref, dp_in, dx_ref, gw_ref, gb_ref, dp_ref, xp_ref):
        del dp_in
        for ref, src in ((dp_ref, du_ref), (xp_ref, x_ref)):
            ref[0:SUB, :] = jnp.zeros((SUB, cbk), F32)
            ref[n + SUB:n + 2 * SUB, :] = jnp.zeros((SUB, cbk), F32)
            ref[SUB:n + SUB, :] = src[...]
        w = w_ref[...]

        def fold(v):
            return jnp.sum(v.reshape(tr // SUB, SUB, cbk), axis=0)

        def chunk(ci, carry):
            r0 = pl.multiple_of(ci * tr, SUB16)
            de = dp_ref[pl.ds(r0, ext), :]
            xe = xp_ref[pl.ds(r0, ext), :]
            t = r0 + lax.broadcasted_iota(jnp.int32, (tr, cbk), 0)
            d0 = de[SUB:tr + SUB]
            dx = jnp.zeros((tr, cbk), F32)
            new = []
            for k in range(4):
                o = k - 1
                dsh = de if o == 0 else pltpu.roll(de, o % ext, 0)
                dx = dx + jnp.where(_tap_valid(t, -o, lc, n), dsh[SUB:tr + SUB], 0.0) * w[k:k + 1]
                xsh = xe if o == 0 else pltpu.roll(xe, (-o) % ext, 0)
                new.append(carry[k] + fold(d0 * jnp.where(_tap_valid(t, o, lc, n), xsh[SUB:tr + SUB], 0.0)))
            new.append(carry[4] + fold(d0))
            dx_ref[pl.ds(r0, tr), :] = dx.astype(dx_ref.dtype)
            return tuple(new)

        zero = jnp.zeros((SUB, cbk), F32)
        acc = lax.fori_loop(0, n // tr, chunk, (zero,) * 5)
        for k in range(4):
            gw_ref[k:k + 1, :] = jnp.sum(acc[k], axis=0, keepdims=True)
        gb_ref[...] = jnp.sum(acc[4], axis=0, keepdims=True)

    col = pl.BlockSpec((n, cbk), lambda j: (0, j))
    blocks = [((n, cbk), F32)] * 2 + [((n, cbk), MXU), ((4, cbk), F32), ((4, cbk), F32), ((1, cbk), F32)]
    scratch = [((n + 2 * SUB, cbk), F32)] * 2
    return pl.pallas_call(
        body, name=name, grid=(W // cbk,),
        in_specs=[col, col, pl.BlockSpec((4, cbk), lambda j: (0, j)), ANY],
        out_specs=[col, pl.BlockSpec((4, cbk), lambda j: (0, j)), pl.BlockSpec((1, cbk), lambda j: (0, j))],
        out_shape=[_sds(dproj.shape, dproj.dtype), _sds((4, W), F32), _sds((1, W), F32)],
        input_output_aliases={3: 0},
        scratch_shapes=[pltpu.VMEM(s, d) for s, d in scratch],
        compiler_params=_params(blocks, scratch, ("parallel",)),
    )(du_all, proj_all, cw, dproj)


def _gate_coeffs(ub, u, d, wr_ref, wi_ref, br_ref, bi_ref, lam_ref):
    c = -LRU_C * _softplus(-lam_ref[d:d + 1, :])
    r = _sigmoid(lax.dot_general(ub, wr_ref[d], NN, preferred_element_type=F32) + br_ref[d:d + 1, :])
    ig = _sigmoid(lax.dot_general(ub, wi_ref[d], NN, preferred_element_type=F32) + bi_ref[d:d + 1, :])
    la = c * r
    a = jnp.exp(la)
    sq = jnp.sqrt(-jnp.tanh(la) * (1.0 + a * a))
    return c, r, ig, a, sq


def _gate_specs(tl, hd):
    w_spec = pl.BlockSpec((2, None, hd, hd), lambda h, i: (0, h, 0, 0))
    v_spec = pl.BlockSpec((2, hd), lambda h, i: (0, h))
    return w_spec, v_spec


def _gates_fwd(u_all, wr, wi, br, bi, lam, name):
    n, W = u_all.shape
    heads, hd = wr.shape[1], wr.shape[2]
    tl = _tile(n, TL_GATES, SUB16)

    def body(u_ref, wr_ref, wi_ref, br_ref, bi_ref, lam_ref, a_ref, b_ref):
        u = u_ref[...]
        ub = u.astype(MXU)
        for d in range(2):
            _, _, ig, a, sq = _gate_coeffs(ub, u, d, wr_ref, wi_ref, br_ref, bi_ref, lam_ref)
            a_ref[d] = a
            b_ref[d] = sq * (ig * u)

    w_spec, v_spec = _gate_specs(tl, hd)
    o_spec = pl.BlockSpec((2, tl, hd), lambda h, i: (0, i, h))
    blocks = [((tl, hd), F32), ((2, hd, hd), MXU), ((2, hd, hd), MXU)] + [((2, hd), F32)] * 3 + [((2, tl, hd), F32)] * 2
    return pl.pallas_call(
        body, name=name, grid=(heads, n // tl),
        in_specs=[pl.BlockSpec((tl, hd), lambda h, i: (i, h)), w_spec, w_spec, v_spec, v_spec, v_spec],
        out_specs=[o_spec, o_spec], out_shape=[_sds((2, n, W), F32)] * 2,
        compiler_params=_params(blocks, dims=("parallel", "parallel")),
    )(u_all, wr, wi, br, bi, lam)


def _gates_bwd(u_all, da, db, wr, wi, br, bi, lam, name):
    n, W = u_all.shape
    heads, hd = wr.shape[1], wr.shape[2]
    tl = _tile(n, TL_GATES, SUB16)
    ni = n // tl

    def body(u_ref, da_ref, db_ref, wr_ref, wi_ref, br_ref, bi_ref, lam_ref,
             du_ref, gwr_ref, gwi_ref, gbr_ref, gbi_ref, gc_ref, accr_ref, acci_ref):
        i = pl.program_id(1)

        @pl.when(i == 0)
        def _():
            accr_ref[...] = jnp.zeros_like(accr_ref)
            acci_ref[...] = jnp.zeros_like(acci_ref)
            gbr_ref[...] = jnp.zeros_like(gbr_ref)
            gbi_ref[...] = jnp.zeros_like(gbi_ref)
            gc_ref[...] = jnp.zeros_like(gc_ref)

        u = u_ref[...]
        ub = u.astype(MXU)
        du = jnp.zeros_like(u)
        for d in range(2):
            c, r, ig, a, sq = _gate_coeffs(ub, u, d, wr_ref, wi_ref, br_ref, bi_ref, lam_ref)
            dbv = db_ref[d]
            t = dbv * sq
            du = du + t * ig
            d_la = da_ref[d] * a - (dbv * ig * u) * (a * a) / sq
            gc_ref[d:d + 1, :] += jnp.sum(d_la * r, axis=0, keepdims=True)
            d_pr = (d_la * c) * (r * (1.0 - r))
            d_pi = (t * u) * (ig * (1.0 - ig))
            gbr_ref[d:d + 1, :] += jnp.sum(d_pr, axis=0, keepdims=True)
            gbi_ref[d:d + 1, :] += jnp.sum(d_pi, axis=0, keepdims=True)
            pb = d_pr.astype(MXU)
            qb = d_pi.astype(MXU)
            du = du + lax.dot_general(pb, wr_ref[d], NT, preferred_element_type=F32)
            du = du + lax.dot_general(qb, wi_ref[d], NT, preferred_element_type=F32)
            accr_ref[d] += lax.dot_general(ub, pb, TN_DIMS, preferred_element_type=F32)
            acci_ref[d] += lax.dot_general(ub, qb, TN_DIMS, preferred_element_type=F32)
        du_ref[...] = du

        @pl.when(i == ni - 1)
        def _():
            gwr_ref[...] = accr_ref[...].astype(gwr_ref.dtype)
            gwi_ref[...] = acci_ref[...].astype(gwi_ref.dtype)

    w_spec, v_spec = _gate_specs(tl, hd)
    u_spec = pl.BlockSpec((tl, hd), lambda h, i: (i, h))
    ab_spec = pl.BlockSpec((2, tl, hd), lambda h, i: (0, i, h))
    blocks = ([((tl, hd), F32)] * 2 + [((2, tl, hd), F32)] * 2 + [((2, hd, hd), MXU)] * 4 + [((2, hd), F32)] * 6)
    scratch = [((2, hd, hd), F32)] * 2
    return pl.pallas_call(
        body, name=name, grid=(heads, ni),
        in_specs=[u_spec, ab_spec, ab_spec, w_spec, w_spec, v_spec, v_spec, v_spec],
        out_specs=[u_spec, w_spec, w_spec, v_spec, v_spec, v_spec],
        out_shape=[_sds((n, W), F32), _sds(wr.shape, MXU), _sds(wi.shape, MXU)] + [_sds((2, W), F32)] * 3,
        scratch_shapes=[pltpu.VMEM(s, d) for s, d in scratch],
        compiler_params=_params(blocks, scratch, ("parallel", "arbitrary")),
    )(u_all, da, db, wr, wi, br, bi, lam)


def _tile_scan(A, B, rows, reverse):
    for s in (1, 2, 4):
        if reverse:
            As, Bs, m = pltpu.roll(A, SUB - s, 0), pltpu.roll(B, SUB - s, 0), rows < SUB - s
        else:
            As, Bs, m = pltpu.roll(A, s, 0), pltpu.roll(B, s, 0), rows >= s
        B = jnp.where(m, A * Bs + B, B)
        A = jnp.where(m, A * As, A)
    return A, B


def _scan_chunks(n, lc):
    tc = _tile(lc, TL, SUB)
    assert n % tc == 0 and lc % tc == 0
    return tc, n // tc, lc // tc


def _scan_fwd(a_all, b_all, lc, name):
    _, n, W = a_all.shape
    cb = _tile(W, CB_SCAN, LANE)
    tc, nch, ncc = _scan_chunks(n, lc)
    ntile = tc // SUB

    def chunk(d, t):
        return jnp.where(d == 0, t, jnp.where(t < ncc, ncc - 1 - t, nch - 1 - (t - ncc)))

    def body(a_ref, b_ref, h_ref, carry_ref):
        rows = lax.broadcasted_iota(jnp.int32, (SUB, cb), 0)

        @pl.when(pl.program_id(2) == 0)
        def _():
            carry_ref[...] = jnp.zeros_like(carry_ref)

        def run(reverse):
            def step(i, h):
                r = pl.multiple_of(((ntile - 1 - i) if reverse else i) * SUB, SUB)
                A, B = _tile_scan(a_ref[pl.ds(r, SUB), :], b_ref[pl.ds(r, SUB), :], rows, reverse)
                H = A * h + B
                h_ref[pl.ds(r, SUB), :] = H
                return H[0:1, :] if reverse else H[SUB - 1:SUB, :]

            carry_ref[...] = lax.fori_loop(0, ntile, step, carry_ref[...], unroll=2)

        @pl.when(pl.program_id(1) == 0)
        def _():
            run(False)

        @pl.when(pl.program_id(1) == 1)
        def _():
            run(True)

    spec = pl.BlockSpec((None, tc, cb), lambda j, d, t: (d, chunk(d, t), j))
    return pl.pallas_call(
        body, name=name, grid=(W // cb, 2, nch), in_specs=[spec, spec], out_specs=spec,
        out_shape=_sds((2, n, W), F32), scratch_shapes=[pltpu.VMEM((1, cb), F32)],
        compiler_params=_params([((tc, cb), F32)] * 3, [((1, cb), F32)], ("parallel", "arbitrary", "arbitrary")),
    )(a_all, b_all)


def _scan_bwd(a_all, h_all, dya, lc, name):
    _, n, W = a_all.shape
    cb = _tile(W, CB_SCAN, LANE)
    tc, nch, ncc = _scan_chunks(n, lc)
    ntile = tc // SUB
    nl = nch - ncc

    def chunk(d, t):
        return jnp.where(d == 0, nch - 1 - t, jnp.where(t < nl, ncc + t, t - nl))

    def neighbour(d, t):
        c = chunk(d, t)
        below = jnp.maximum(c * ntile - 1, 0)
        above = jnp.where(c == nch - 1, 0, jnp.minimum((c + 1) * ntile, nch * ntile - 1))
        return jnp.where(d == 0, below, above)

    def body(a_ref, h_ref, hn_ref, g_ref, da_ref, db_ref, mu_ref):
        rows = lax.broadcasted_iota(jnp.int32, (SUB, cb), 0)
        d, t = pl.program_id(1), pl.program_id(2)
        c = chunk(d, t)
        has_g = c >= ncc

        @pl.when(t == 0)
        def _():
            mu_ref[...] = jnp.zeros_like(mu_ref)

        def tile(ref, j):
            return ref[pl.ds(pl.multiple_of(j * SUB, SUB), SUB), :]

        def run(up):
            if up:
                edge = jnp.where(c == ncc - 1, 0.0, hn_ref[0:1, :])
            else:
                edge = jnp.where(c > 0, hn_ref[SUB - 1:SUB, :], 0.0)

            def step(i, mu):
                j = i if up else ntile - 1 - i
                a_t = tile(a_ref, j)
                g_t = jnp.where(has_g, tile(g_ref, j), 0.0)
                if up:
                    ap = jnp.where(rows >= 1, pltpu.roll(a_t, 1, 0), 1.0)
                    nb_row = jnp.where(j < ntile - 1, tile(h_ref, jnp.minimum(j + 1, ntile - 1))[0:1, :], edge)
                    hprev = jnp.where(rows < SUB - 1, pltpu.roll(tile(h_ref, j), SUB - 1, 0), nb_row)
                else:
                    ap = jnp.where(rows < SUB - 1, pltpu.roll(a_t, SUB - 1, 0), 1.0)
                    nb_row = jnp.where(j > 0, tile(h_ref, jnp.maximum(j - 1, 0))[SUB - 1:SUB, :], edge)
                    hprev = jnp.where(rows >= 1, pltpu.roll(tile(h_ref, j), 1, 0), nb_row)
                A, B = _tile_scan(ap, g_t, rows, not up)
                lam = A * mu + B
                r = pl.multiple_of(j * SUB, SUB)
                da_ref[pl.ds(r, SUB), :] = lam * hprev
                db_ref[pl.ds(r, SUB), :] = lam
                return a_t[SUB - 1:SUB, :] * lam[SUB - 1:SUB, :] if up else a_t[0:1, :] * lam[0:1, :]

            mu_ref[...] = lax.fori_loop(0, ntile, step, mu_ref[...], unroll=2)

        @pl.when(d == 0)
        def _():
            run(False)

        @pl.when(d == 1)
        def _():
            run(True)

    spec = pl.BlockSpec((None, tc, cb), lambda j, d, t: (d, chunk(d, t), j))
    n_spec = pl.BlockSpec((None, SUB, cb), lambda j, d, t: (d, neighbour(d, t), j))
    g_spec = pl.BlockSpec((tc, cb), lambda j, d, t: (jnp.maximum(chunk(d, t) - ncc, 0), j))
    blocks = [((tc, cb), F32)] * 5 + [((SUB, cb), F32)]
    return pl.pallas_call(
        body, name=name, grid=(W // cb, 2, nch), in_specs=[spec, spec, n_spec, g_spec], out_specs=[spec, spec],
        out_shape=[_sds((2, n, W), F32)] * 2, scratch_shapes=[pltpu.VMEM((1, cb), F32)],
        compiler_params=_params(blocks, [((1, cb), F32)], ("parallel", "arbitrary", "arbitrary")),
    )(a_all, h_all, h_all, dya)


def _pool_window(v, w, tl, cb, transpose):
    half = w // 2
    pos = lax.broadcasted_iota(jnp.int32, (tl, cb), 0) % GRID_W
    cnt = (jnp.minimum(pos + half - 1, GRID_W - 1) - jnp.maximum(pos - half, 0) + 1).astype(F32)
    src = v / cnt if transpose else v

    def run_sum(s, step):
        span = 1
        while span < half:
            ok = (pos + span < GRID_W) if step > 0 else (pos - span >= 0)
            s = s + jnp.where(ok, pltpu.roll(s, (-step * span) % tl, 0), 0.0)
            span *= 2
        return s

    ahead, behind = run_sum(src, 1), run_sum(src, -1)
    if transpose:
        return behind + jnp.where(pos + 1 < GRID_W, pltpu.roll(ahead, tl - 1, 0), 0.0) - v
    return (ahead + jnp.where(pos >= 1, pltpu.roll(behind, 1, 0), 0.0)) / cnt - v


def _pool_z(src, row0, col0, L, W, transpose, dproj, name):
    G = len(POOL_WINDOWS)
    pd = W // G
    tl = _tile(L, TL, GRID_W)
    cb = _tile(pd, CB_POOL, LANE)
    assert row0 % tl == 0 and col0 % cb == 0
    rb, cbk = row0 // tl, col0 // cb
    nj = pd // cb

    def body(x_ref, *rest):
        o_ref = rest[-1]
        for gi, w in enumerate(POOL_WINDOWS):
            @pl.when(pl.program_id(0) == gi)
            def _(w=w):
                o_ref[...] = _pool_window(x_ref[...], w, tl, cb, transpose).astype(o_ref.dtype)

    plain = pl.BlockSpec((tl, cb), lambda g, i, j: (i, g * nj + j))
    window = pl.BlockSpec((tl, cb), lambda g, i, j: (i + rb, cbk + g * nj + j))
    blocks = [((tl, cb), F32), ((tl, cb), MXU)]
    if transpose:
        return pl.pallas_call(
            body, name=name, grid=(G, L // tl, nj), in_specs=[plain, ANY], out_specs=window,
            out_shape=_sds(dproj.shape, dproj.dtype), input_output_aliases={1: 0},
            compiler_params=_params(blocks, dims=("parallel",) * 3),
        )(src, dproj)
    return pl.pallas_call(
        body, name=name, grid=(G, L // tl, nj), in_specs=[window], out_specs=plain,
        out_shape=_sds((L, W), MXU),
        compiler_params=_params(blocks, dims=("parallel",) * 3),
    )(src)


def _mix_fwd(hs, proj_all, ypre, b_pool, pool_scale, lc, name):
    L, W = ypre.shape
    tl = _tile(L, TL, SUB16)
    cb = _tile(W, CB_MIX, LANE)
    nj = W // cb
    assert lc % tl == 0
    rb = lc // tl

    def body(hs_ref, ga_ref, yp_ref, gb_ref, bp_ref, ps_ref, o_ref):
        p = pl.program_id(2)

        @pl.when(p == 0)
        def _():
            g = ga_ref[...]
            o_ref[...] = ((hs_ref[0] + hs_ref[1]) * (g * _sigmoid(g))).astype(o_ref.dtype)

        @pl.when(p == 1)
        def _():
            g = gb_ref[...]
            yb = (yp_ref[...] + bp_ref[...]) * ps_ref[...]
            o_ref[...] = (yb * (g * _sigmoid(g))).astype(o_ref.dtype)

    vec = pl.BlockSpec((1, cb), lambda i, j, p: (0, j))
    blocks = [((2, tl, cb), F32)] + [((tl, cb), F32)] * 3 + [((tl, cb), MXU)]
    return pl.pallas_call(
        body, name=name, grid=(L // tl, nj, 2),
        in_specs=[pl.BlockSpec((2, tl, cb), lambda i, j, p: (0, i + rb, j)),
                  pl.BlockSpec((tl, cb), lambda i, j, p: (i + rb, 2 * nj + j)),
                  pl.BlockSpec((tl, cb), lambda i, j, p: (i, j)),
                  pl.BlockSpec((tl, cb), lambda i, j, p: (i + rb, 3 * nj + j)), vec, vec],
        out_specs=pl.BlockSpec((tl, cb), lambda i, j, p: (i, p * nj + j)),
        out_shape=_sds((L, 2 * W), MXU),
        compiler_params=_params(blocks, dims=("parallel", "parallel", "arbitrary")),
    )(hs, proj_all, ypre, proj_all, b_pool, pool_scale)


def _dsilu(g, sg):
    return sg * (1.0 + g * (1.0 - sg))


def _mixa_bwd(dmixed, hs, proj_all, dproj, lc, W, name):
    L = dmixed.shape[0]
    tl = _tile(L, TL, SUB16)
    cb = _tile(W, CB_MIX, LANE)
    nj = W // cb
    rb = lc // tl

    def body(dm_ref, hs_ref, ga_ref, dp_in, dya_ref, dga_ref):
        del dp_in
        g = ga_ref[...]
        sg = _sigmoid(g)
        dm = dm_ref[...]
        dya_ref[...] = dm * (g * sg)
        dga_ref[...] = (dm * (hs_ref[0] + hs_ref[1]) * _dsilu(g, sg)).astype(dga_ref.dtype)

    blocks = [((tl, cb), F32)] * 3 + [((2, tl, cb), F32), ((tl, cb), MXU)]
    return pl.pallas_call(
        body, name=name, grid=(L // tl, nj),
        in_specs=[pl.BlockSpec((tl, cb), lambda i, j: (i, j)),
                  pl.BlockSpec((2, tl, cb), lambda i, j: (0, i + rb, j)),
                  pl.BlockSpec((tl, cb), lambda i, j: (i + rb, 2 * nj + j)), ANY],
        out_specs=[pl.BlockSpec((tl, cb), lambda i, j: (i, j)),
                   pl.BlockSpec((tl, cb), lambda i, j: (i + rb, 2 * nj + j))],
        out_shape=[_sds((L, W), F32), _sds(dproj.shape, dproj.dtype)],
        input_output_aliases={3: 1},
        compiler_params=_params(blocks, dims=("parallel", "parallel")),
    )(dmixed, hs, proj_all, dproj)


def _mixb_bwd(dmixed, ypre, proj_all, b_pool, pool_scale, dproj, lc, W, name):
    L = dmixed.shape[0]
    tl = _tile(L, TL, SUB16)
    cb = _tile(W, CB_MIX, LANE)
    nj = W // cb
    rb = lc // tl

    def body(dm_ref, yp_ref, gb_ref, bp_ref, ps_ref, dp_in, dyp_ref, dgb_ref, gbp_ref, gps_ref):
        del dp_in
        i = pl.program_id(1)

        @pl.when(i == 0)
        def _():
            gbp_ref[...] = jnp.zeros_like(gbp_ref)
            gps_ref[...] = jnp.zeros_like(gps_ref)

        g = gb_ref[...]
        sg = _sigmoid(g)
        dm = dm_ref[...]
        yp = yp_ref[...] + bp_ref[...]
        ps = ps_ref[...]
        dyb = dm * (g * sg)
        dyp = dyb * ps
        dgb_ref[...] = (dm * (yp * ps) * _dsilu(g, sg)).astype(dgb_ref.dtype)
        dyp_ref[...] = dyp.astype(dyp_ref.dtype)
        gbp_ref[...] += jnp.sum(dyp, axis=0, keepdims=True)
        gps_ref[...] += jnp.sum(dyb * yp, axis=0, keepdims=True)

    vec = pl.BlockSpec((1, cb), lambda j, i: (0, j))
    blocks = [((tl, cb), F32)] * 3 + [((tl, cb), MXU)] * 2 + [((1, cb), F32)] * 4
    return pl.pallas_call(
        body, name=name, grid=(nj, L // tl),
        in_specs=[pl.BlockSpec((tl, cb), lambda j, i: (i, nj + j)),
                  pl.BlockSpec((tl, cb), lambda j, i: (i, j)),
                  pl.BlockSpec((tl, cb), lambda j, i: (i + rb, 3 * nj + j)), vec, vec, ANY],
        out_specs=[pl.BlockSpec((tl, cb), lambda j, i: (i, j)),
                   pl.BlockSpec((tl, cb), lambda j, i: (i + rb, 3 * nj + j)), vec, vec],
        out_shape=[_sds((L, W), MXU), _sds(dproj.shape, dproj.dtype), _sds((1, W), F32), _sds((1, W), F32)],
        input_output_aliases={5: 1},
        compiler_params=_params(blocks, dims=("parallel", "arbitrary")),
    )(dmixed, ypre, proj_all, b_pool, pool_scale, dproj)


def _dproj_init(n, lc, W, name):
    cb = _tile(W, CB_MIX, LANE)
    nj = W // cb

    def body(o_ref):
        o_ref[...] = jnp.zeros_like(o_ref)

    return pl.pallas_call(
        body, name=name, grid=(3 * nj,), in_specs=[],
        out_specs=pl.BlockSpec((lc, cb), lambda j: (0, nj + j)),
        out_shape=_sds((n, 4 * W), MXU),
        compiler_params=_params([((lc, cb), MXU)], dims=("parallel",)),
    )()


def _final(x2, out, tgt, gate, gfin, name):
    L, D = x2.shape
    tl = _tile(L, TL_FINAL, SUB16)

    def body(x_ref, o_ref, t_ref, gate_ref, g_ref, dout_ref, dxn_ref, loss_ref, ggf_ref, dgate_ref):
        i = pl.program_id(0)

        @pl.when(i == 0)
        def _():
            loss_ref[...] = jnp.zeros_like(loss_ref)
            ggf_ref[...] = jnp.zeros_like(ggf_ref)
            dgate_ref[...] = jnp.zeros_like(dgate_ref)

        o = o_ref[...]
        gate_v = gate_ref[...]
        gv = g_ref[...]
        xn = x_ref[...] + gate_v * o
        s = lax.rsqrt(jnp.mean(xn * xn, axis=-1, keepdims=True) + EPS)
        xh = xn * s
        err = xh * gv - t_ref[...]
        tok = jnp.mean(err * err, axis=-1, keepdims=True)
        loss_ref[...] += 0.5 * jnp.sum(tok, axis=0, keepdims=True)
        dy = err / D
        ggf_ref[...] += jnp.sum(dy * xh, axis=0, keepdims=True)
        dxh = dy * gv
        dxn = s * (dxh - xh * jnp.mean(dxh * xh, axis=-1, keepdims=True))
        dgate_ref[...] += jnp.sum(dxn * o, axis=0, keepdims=True)
        dout_ref[...] = (gate_v * dxn).astype(dout_ref.dtype)
        dxn_ref[...] = dxn

    row = pl.BlockSpec((tl, D), lambda i: (i, 0))
    vec = pl.BlockSpec((1, D), lambda i: (0, 0))
    blocks = [((tl, D), F32)] * 4 + [((tl, D), MXU)] + [((1, D), F32)] * 4
    return pl.pallas_call(
        body, name=name, grid=(L // tl,), in_specs=[row, row, row, vec, vec],
        out_specs=[row, row, pl.BlockSpec((1, 1), lambda i: (0, 0)), vec, vec],
        out_shape=[_sds((L, D), MXU), _sds((L, D), F32), _sds((1, 1), F32), _sds((1, D), F32), _sds((1, D), F32)],
        compiler_params=_params(blocks, dims=("arbitrary",)),
    )(x2, out, tgt, gate, gfin)


def _adamw_parts(w2, parts, m2, v2, name):
    R, C = w2.shape
    nh = len(parts)
    ch = C // nh
    tr = _tile(R, max(SUB16, (512 * 1024) // (ch * (nh + 1))), SUB16)

    def body(w_ref, *rest):
        p_refs = rest[:nh]
        m_ref, v_ref, g_ref, d_ref, nm_ref, nv_ref = rest[nh:]
        for q in range(nh):
            @pl.when(pl.program_id(1) == q)
            def _(p_ref=p_refs[q]):
                g = p_ref[0].astype(F32)
                for p in range(1, NDEV):
                    g = g + p_ref[p].astype(F32)
                delta, nm, nv = _adam(w_ref[...], g, m_ref[...], v_ref[...])
                g_ref[...] = g
                d_ref[...] = delta
                nm_ref[...] = nm
                nv_ref[...] = nv

    blk = pl.BlockSpec((tr, ch), lambda i, h: (i, h))
    p_spec = pl.BlockSpec((NDEV, tr, ch), lambda i, h: (0, i, 0))
    blocks = [((tr, ch), F32)] * 7 + [((NDEV, tr, ch), parts[0].dtype)] * nh
    return pl.pallas_call(
        body, name=name, grid=(R // tr, nh),
        in_specs=[blk] + [p_spec] * nh + [blk, blk],
        out_specs=[blk] * 4, out_shape=[_sds((R, C), F32)] * 4,
        compiler_params=_params(blocks, dims=("parallel", "arbitrary")),
    )(w2, *parts, m2, v2)


def _small_sum(vs, ga, gc, name):
    ns, nm = vs.shape[1], ga.shape[1]

    def body(v_ref, ga_ref, gc_ref, tot_ref, gb_ref):
        tot = v_ref[0:1, :]
        gb = ga_ref[0:1, :]
        for p in range(1, NDEV):
            tot = tot + v_ref[p:p + 1, :]
            gb = gb + ga_ref[p:p + 1, :]
        for p in range(NDEV):
            gb = gb + gc_ref[p:p + 1, :]
        tot_ref[...] = tot
        gb_ref[...] = gb

    blocks = [((NDEV, ns), F32), ((NDEV, nm), F32), ((NDEV, nm), F32), ((1, ns), F32), ((1, nm), F32)]
    return pl.pallas_call(
        body, name=name, out_shape=[_sds((1, ns), F32), _sds((1, nm), F32)],
        compiler_params=_params(blocks),
    )(vs, ga, gc)


def _adamw_small(g_raw, w, m, v, lam_range, cctx_range, name):
    npk = w.shape[1]

    def body(g_ref, w_ref, m_ref, v_ref, go_ref, d_ref, nm_ref, nv_ref):
        wv = w_ref[...]
        g = g_ref[...]
        idx = lax.broadcasted_iota(jnp.int32, (1, npk), 1)
        in_lam = (idx >= lam_range[0]) & (idx < lam_range[1])
        in_cc = (idx >= cctx_range[0]) & (idx < cctx_range[1])
        sg = _sigmoid_small(wv)
        g = jnp.where(in_lam, g * (LRU_C * _sigmoid_small(-wv)), jnp.where(in_cc, g * _dsilu(wv, sg), g))
        delta, nm, nv = _adam(wv, g, m_ref[...], v_ref[...])
        go_ref[...] = g
        d_ref[...] = delta
        nm_ref[...] = nm
        nv_ref[...] = nv

    return pl.pallas_call(
        body, name=name, out_shape=[_sds((1, npk), F32)] * 4,
        compiler_params=_params([((1, npk), F32)] * 8),
    )(g_raw, w, m, v)


def _pack(pieces):
    return jnp.concatenate([p.reshape(1, -1) for p in pieces], axis=1)


def kernel(x, c, ctx, c_ctx, w_ada, b_ada, g_norm, w_in, conv_w, conv_b, lru_lambda, w_rgate, b_rgate, w_igate, b_igate, w_pool, b_pool, pool_scale, w_out, g_final, loss_target, m_c_ctx, m_w_ada, m_b_ada, m_g_norm, m_w_in, m_conv_w, m_conv_b, m_lru_lambda, m_w_rgate, m_b_rgate, m_w_igate, m_b_igate, m_w_pool, m_b_pool, m_pool_scale, m_w_out, m_g_final, v_c_ctx, v_w_ada, v_b_ada, v_g_norm, v_w_in, v_conv_w, v_conv_b, v_lru_lambda, v_w_rgate, v_b_rgate, v_w_igate, v_b_igate, v_w_pool, v_b_pool, v_pool_scale, v_w_out, v_g_final):
    L, D = x.shape[1], x.shape[2]
    lc = ctx.shape[1]
    n = lc + L
    W = conv_b.shape[1]
    heads, hd = w_rgate.shape[2], w_rgate.shape[4]
    G, pd = w_pool.shape[1], w_pool.shape[3]
    na = w_ada.shape[2]
    nb = w_in.shape[2]
    ws = W // NDEV
    me = 4 * lax.axis_index("x") + 2 * lax.axis_index("y") + lax.axis_index("c")

    nbp = nb // WIN_PARTS
    w_in_parts = [w_in[0, :, q * nbp:(q + 1) * nbp].astype(MXU) for q in range(WIN_PARTS)]
    (win_0, cw_all, lam_all, br_all, bi_all, c_all) = _all_gather(
        [w_in_parts[0], conv_w[0], lru_lambda[0], b_rgate[0], b_igate[0], c], "gather_w_in")
    win = [win_0]
    cw = cw_all.transpose(1, 0, 2).reshape(4, W)
    lam = lam_all.transpose(1, 0, 2).reshape(2, W)
    br = br_all.transpose(1, 0, 2).reshape(2, W)
    bi = bi_all.transpose(1, 0, 2).reshape(2, W)

    cc = jnp.concatenate([c_all.reshape(NDEV, D), c_ctx.reshape(1, D), jnp.zeros((NDEV - 1, D), F32)], axis=0)
    b_loc = lax.dynamic_slice(b_ada, (0, me * na), (1, na))
    mod_loc, s_all = _ada_fwd(cc, w_ada[0], b_loc, "ada_fwd")
    (mod_all,) = _all_gather([mod_loc], "gather_mod")
    gate_w = [w_rgate[0].astype(MXU), w_igate[0].astype(MXU)]
    rest_w = [w_pool[0].astype(MXU), w_out[0].astype(MXU)]
    tok = mod_all
    sent_win = []
    for q in range(1, WIN_PARTS):
        part = [w_in_parts[q]]
        sent_win.append(_send_start(part, _place(part, False, f"place_w_in_{q}", [tok]), "level1", f"start_w_in_{q}"))
        tok = sent_win[-1][4]
    sent_gw = _send_start(gate_w, _place(gate_w, False, "place_gate_w", [tok]), False, "start_gate_w")
    sent_rw = _send_start(rest_w, _place(rest_w, False, "place_rest_w", [sent_gw[4]]), False, "start_rest_w")
    mod = mod_all.transpose(1, 0, 2).reshape(2 * NDEV, NDEV * na)
    mod_me = lax.dynamic_slice(mod, (me, 0), (1, 3 * D))
    shift, scale, gate = mod_me[:, :D], mod_me[:, D:2 * D], mod_me[:, 2 * D:]
    shift = _tie(shift, [sent_gw[4], sent_rw[4]], "tie_weights")
    shift_c, scale_c = mod[NDEV:NDEV + 1, :D], mod[NDEV:NDEV + 1, D:2 * D]

    x2, ctx2, tgt = x[0], ctx[0], loss_target[0]
    gfin = g_final.reshape(1, D)
    h_all = _norm_mod(x2, g_norm, shift, scale, n, lc, None, "norm_lat")
    h_all = _norm_mod(ctx2, g_norm, shift_c, scale_c, n, 0, h_all, "norm_ctx")
    proj_all = _mm_proj(h_all, win[0], 0, WIN_PARTS, None, "mm_proj_0")
    for q in range(1, WIN_PARTS):
        lands = _send_wait(sent_win[q - 1], proj_all, "level1", f"wait_w_in_{q}")
        passed = _send_start([], lands, "level2", f"pass_w_in_{q}")
        win.append(_send_wait(passed, proj_all, "level2", f"wait_pass_w_in_{q}")[0])
        proj_all = _mm_proj(h_all, win[q], q, WIN_PARTS, proj_all, f"mm_proj_{q}")
    u_all = _conv_fwd(proj_all, cw, conv_b, lc, W, "conv_fwd")
    wr_all, wi_all = _send_wait(sent_gw, u_all, False, "wait_gate_w")
    wr = wr_all.transpose(1, 2, 0, 3, 4).reshape(2, heads, hd, hd)
    wi = wi_all.transpose(1, 2, 0, 3, 4).reshape(2, heads, hd, hd)
    a_all, b_all = _gates_fwd(u_all, wr, wi, br, bi, lam, "gates_fwd")
    hs = _scan_fwd(a_all, b_all, lc, "scan_fwd")
    z = _pool_z(proj_all, lc, W, L, W, False, None, "pool_z")
    wpool_all, wout_all = _send_wait(sent_rw, hs, False, "wait_rest_w")
    wpool = wpool_all.transpose(1, 0, 2, 3).reshape(G, pd, pd)
    wout = wout_all.reshape(2 * W, D)
    ypre = _mm_group(z, wpool, "fwd", F32, "mm_pool")
    mixed = _mix_fwd(hs, proj_all, ypre, b_pool, pool_scale, lc, "mix_fwd")
    out = _mm_plain(mixed, wout, NN, F32, "mm_out")
    d_out, dxn, loss_p, ggf, dgate = _final(x2, out, tgt, gate, gfin, "final")

    dmixed = _mm_plain(d_out, wout, NT, F32, "mm_dmixed")
    gwout = _mm_plain(mixed, d_out, TN_DIMS, MXU, "mm_gwout")
    ex_o = [gwout.reshape(NDEV, 2 * W // NDEV, D)]
    sent_o = _send_start(ex_o, _place(ex_o, True, "place_gwout"), True, "start_gwout")
    dproj = _dproj_init(n, lc, W, "dproj_init")
    dya, dproj = _mixa_bwd(dmixed, hs, proj_all, dproj, lc, W, "mixa_bwd")
    dypre, dproj, gbp, gps = _mixb_bwd(dmixed, ypre, proj_all, _tie(b_pool, [sent_o[4]], "tie_gwout"), pool_scale,
                                       dproj, lc, W, "mixb_bwd")
    dz = _mm_group(dypre, wpool, "bwd", F32, "mm_dz")
    gwpool = _mm_group(z, dypre, "wgrad", MXU, "mm_gwpool")
    dproj = _pool_z(dz, lc, W, L, W, True, dproj, "pool_z_bwd")
    da, db = _scan_bwd(a_all, hs, dya, lc, "scan_bwd")
    du, gwr, gwi, gbr, gbi, gcl = _gates_bwd(u_all, da, db, wr, wi, br, bi, lam, "gates_bwd")
    ex_s = [gwpool.reshape(G, NDEV, pd // NDEV, pd).transpose(1, 0, 2, 3),
            gwr.reshape(2, heads, NDEV, hd // NDEV, hd).transpose(2, 0, 1, 3, 4),
            gwi.reshape(2, heads, NDEV, hd // NDEV, hd).transpose(2, 0, 1, 3, 4)]
    sent_s = _send_start(ex_s, _place(ex_s, True, "place_gsmall"), True, "start_gsmall")
    dproj, gcw, gcb = _conv_bwd(du, proj_all, _tie(cw, [sent_s[4]], "tie_gsmall"), dproj, lc, W, "conv_bwd")
    h_t = _transpose(h_all, "transpose_h")
    sent_i, tok = [], None
    for q in range(GWIN_PARTS):
        part = _mm_gwin(h_t, dproj, nb, q, GWIN_PARTS, f"mm_gwin_{q}", dep=tok)
        part = pltpu.with_memory_space_constraint(part, pltpu.HBM)
        sent_i.append(_send_start([part], _place([part], True, f"place_gwin_{q}"), True, f"start_gwin_{q}"))
        tok = sent_i[-1][4]
    dh_all = _mm_dh(dproj, win, "mm_dh", tok)
    grad_x, dshift, dscale, ggn = _norm_bwd(x2, dh_all, lc, g_norm, scale, dxn, jnp.zeros((1, D), F32), "norm_bwd_lat")
    _, dshift_c, dscale_c, ggn = _norm_bwd(ctx2, dh_all, 0, g_norm, scale_c, None, ggn, "norm_bwd_ctx")

    dmod_me = jnp.concatenate([dshift, dscale, dgate], axis=1)
    dmod_c = jnp.concatenate([dshift_c, dscale_c, jnp.zeros((1, D), F32)], axis=1)
    smalls = [ggf, ggn, gcw, gcb, gcl, gbr, gbi, gbp, gps, jnp.pad(loss_p, ((0, 0), (0, LANE - 1)))]
    sizes = [s.size for s in smalls]
    small_all, dmod_all, dmodc_all = _all_gather([_pack(smalls), dmod_me, dmod_c], "gather_small")
    ga = lax.dynamic_slice(dmod_all.reshape(NDEV, 3 * D), (0, me * na), (NDEV, na))
    gc = lax.dynamic_slice(dmodc_all.reshape(NDEV, 3 * D), (0, me * na), (NDEV, na))
    g_wada, d_wada, nm_wada, nv_wada, pc = _ada_bwd(s_all, ga, gc, w_ada[0], m_w_ada[0], v_w_ada[0], "ada_bwd")
    (pc_all,) = _all_gather([pc[0:1]], "gather_cctx")
    tot, gb_ada = _small_sum(
        jnp.concatenate([small_all.reshape(NDEV, -1), pc_all.reshape(NDEV, D)], axis=1),
        dmod_all.reshape(NDEV, 3 * D), dmodc_all.reshape(NDEV, 3 * D), "small_sum")
    offs = [0]
    for s in sizes + [D]:
        offs.append(offs[-1] + s)
    t_ggf, t_ggn, t_gcw, t_gcb, t_gcl, t_gbr, t_gbi, t_gbp, t_gps, t_loss, t_pc = [
        tot[:, offs[i]:offs[i + 1]] for i in range(len(offs) - 1)]

    def shard(t, rows):
        return lax.dynamic_slice(t.reshape(rows, W), (0, me * ws), (rows, ws))

    def big(wv, parts, mv, vv, name):
        shp = wv.shape
        C = shp[-1]
        if not isinstance(parts, list):
            parts = [parts]
        parts = [p.reshape(NDEV, -1, C // len(parts)) for p in parts]
        outs = _adamw_parts(wv.reshape(-1, C), parts, mv.reshape(-1, C), vv.reshape(-1, C), name)
        return [o.reshape(shp) for o in outs]

    (recv_o,) = _send_wait(sent_o, tot, True, "wait_gwout")
    recv_p, recv_r, recv_i = _send_wait(sent_s, tot, True, "wait_gsmall")
    r_wout = big(w_out, recv_o, m_w_out, v_w_out, "adamw_w_out")
    r_wpool = big(w_pool, recv_p, m_w_pool, v_w_pool, "adamw_w_pool")
    r_wr = big(w_rgate, recv_r, m_w_rgate, v_w_rgate, "adamw_w_rgate")
    r_wi = big(w_igate, recv_i, m_w_igate, v_w_igate, "adamw_w_igate")
    r_wada = [o.reshape(w_ada.shape) for o in (g_wada, d_wada, nm_wada, nv_wada)]

    names = ["c_ctx", "b_ada", "g_norm", "conv_w", "conv_b", "lru_lambda", "b_rgate", "b_igate", "b_pool",
             "pool_scale", "g_final"]
    sw = [c_ctx, b_ada, g_norm, conv_w, conv_b, lru_lambda, b_rgate, b_igate, b_pool, pool_scale, g_final]
    sm = [m_c_ctx, m_b_ada, m_g_norm, m_conv_w, m_conv_b, m_lru_lambda, m_b_rgate, m_b_igate, m_b_pool,
          m_pool_scale, m_g_final]
    sv = [v_c_ctx, v_b_ada, v_g_norm, v_conv_w, v_conv_b, v_lru_lambda, v_b_rgate, v_b_igate, v_b_pool,
          v_pool_scale, v_g_final]
    sg = [t_pc, gb_ada, t_ggn, shard(t_gcw, 4), t_gcb, shard(t_gcl, 2), shard(t_gbr, 2), shard(t_gbi, 2), t_gbp,
          t_gps, t_ggf]
    poffs = [0]
    for wv in sw:
        poffs.append(poffs[-1] + wv.size)
    lam_range = (poffs[5], poffs[6])
    cctx_range = (poffs[0], poffs[1])
    small_out = _adamw_small(_pack(sg), _pack(sw), _pack(sm), _pack(sv), lam_range, cctx_range, "adamw_small")
    recv_w = [_send_wait(sent_i[q], small_out[0], True, f"wait_gwin_{q}")[0] for q in range(GWIN_PARTS)]
    r_win = big(w_in, recv_w, m_w_in, v_w_in, "adamw_w_in")
    r_small = {}
    for i, nm in enumerate(names):
        r_small[nm] = [o[:, poffs[i]:poffs[i + 1]].reshape(sw[i].shape) for o in small_out]

    res = dict(r_small)
    res.update(w_ada=r_wada, w_in=r_win, w_rgate=r_wr, w_igate=r_wi, w_pool=r_wpool, w_out=r_wout)
    order = ["c_ctx", "w_ada", "b_ada", "g_norm", "w_in", "conv_w", "conv_b", "lru_lambda", "w_rgate", "b_rgate",
             "w_igate", "b_igate", "w_pool", "b_pool", "pool_scale", "w_out", "g_final"]
    loss = t_loss[0, 0]
    outs = [loss, grad_x.reshape(x.shape)]
    for q in range(4):
        outs += [res[nm][q] for nm in order]
    return tuple(outs)
```

```python
import functools

import jax
import jax.numpy as jnp
from jax import lax
from jax.experimental import pallas as pl
from jax.experimental.pallas import tpu as pltpu

NDEV = 8
GRID_W = 64
POOL_WINDOWS = (2, 4, 8, 16)
LRU_C = 8.0
EPS = 1e-6
ADAM_LR = 0.001
ADAM_B1 = 0.9
ADAM_B2 = 0.999
ADAM_EPS = 1e-08
ADAM_WD = 0.01
ADAM_STEP = 10

F32 = jnp.float32
MXU = jnp.bfloat16

VMEM_BYTES = 64 * 1024 * 1024
VMEM_SLACK = 8 * 1024 * 1024
SUB = 8
SUB16 = 16
LANE = 128

TM = 1152
TN = 1024
TK = 2048
TL = 256
TL_FINAL = 128
TL_GATES = 1088
CB_POOL = 1024
CB_SEQ = 256
CB_SCAN = 1024
CB_MIX = 2048
TR_CONV = 576
GWIN_PARTS = 4
WIN_PARTS = 4

MESH_ID = pl.DeviceIdType.MESH


def _tile(n, pref, align):
    if n <= pref:
        return n
    for t in range(pref - pref % align, 0, -align):
        if n % t == 0:
            return t
    return n


def _nbytes(shape, dtype):
    n = 1
    for s in shape:
        if s is not None:
            n *= s
    return n * jnp.dtype(dtype).itemsize


def _params(blocks, scratch=(), dims=None):
    need = 2 * sum(_nbytes(s, d) for s, d in blocks) + sum(_nbytes(s, d) for s, d in scratch) + VMEM_SLACK
    kw = dict(vmem_limit_bytes=int(min(max(need, 2 * VMEM_SLACK), VMEM_BYTES - VMEM_SLACK // 2)))
    if dims is not None:
        kw["dimension_semantics"] = dims
    return pltpu.CompilerParams(**kw)


def _sds(shape, dtype):
    return jax.ShapeDtypeStruct(tuple(shape), dtype)


ANY = pl.BlockSpec(memory_space=pl.ANY)


def _ids():
    return lax.axis_index("x"), lax.axis_index("y"), lax.axis_index("c")


def _sigmoid(v):
    return 0.5 * jnp.tanh(0.5 * v) + 0.5


def _sigmoid_small(v):
    return jax.nn.sigmoid(v)


def _softplus(v):
    return jnp.maximum(v, 0.0) + jnp.log1p(jnp.exp(-jnp.abs(v)))


def _all_gather(xs, name):
    n = len(xs)

    def body(*refs):
        x_refs, o_refs = refs[:n], refs[n:2 * n]
        send_sems, recv_sems, local_sems = refs[2 * n:]
        x, y, c = _ids()
        me, sibling = (x, y, c), (x, y, 1 - c)
        chips = [(1 - x, y), (x, 1 - y), (1 - x, 1 - y)]

        def slot(a, p):
            return o_refs[a].at[4 * p[0] + 2 * p[1] + p[2]]

        def copy(a, k, block, to, src=None):
            return pltpu.make_async_remote_copy(
                src_ref=slot(a, block) if src is None else src, dst_ref=slot(a, block),
                send_sem=send_sems.at[7 * a + k], recv_sem=recv_sems.at[7 * a + k],
                device_id=to, device_id_type=MESH_ID)

        mine, first, passed = [], [], []
        for a in range(n):
            m = pltpu.make_async_copy(x_refs[a], slot(a, me), local_sems.at[a])
            m.start()
            mine.append(m)
            f = [copy(a, 0, me, sibling, src=x_refs[a])]
            f += [copy(a, 1 + j, me, (*chip, c), src=x_refs[a]) for j, chip in enumerate(chips)]
            for cp in f:
                cp.start()
            first += f
        for a in range(n):
            for j, chip in enumerate(chips):
                copy(a, 1 + j, (*chip, c), me).wait_recv()
                p = copy(a, 4 + j, (*chip, c), sibling)
                p.start()
                passed.append(p)
        for a in range(n):
            copy(a, 0, sibling, me).wait_recv()
            for j, chip in enumerate(chips):
                copy(a, 4 + j, (*chip, 1 - c), me).wait_recv()
        for cp in first + passed:
            cp.wait_send()
        for m in mine:
            m.wait()

    return pl.pallas_call(
        body, name=name,
        out_shape=[_sds((NDEV,) + v.shape, v.dtype) for v in xs],
        in_specs=[ANY] * n, out_specs=[ANY] * n,
        scratch_shapes=[pltpu.SemaphoreType.DMA((7 * n,)), pltpu.SemaphoreType.DMA((7 * n,)),
                        pltpu.SemaphoreType.DMA((n,))],
    )(*xs)


HBM = pl.BlockSpec(memory_space=pltpu.HBM)
SEM = pl.BlockSpec(memory_space=pltpu.SEMAPHORE)
EFFECT = pltpu.SideEffectType.DATAFLOW_SIDE_EFFECTING


def _peers():
    x, y, c = _ids()
    out = []
    for k in range(1, NDEV):
        px = 1 - x if k & 4 else x
        py = 1 - y if k & 2 else y
        pc = 1 - c if k & 1 else c
        out.append(((px, py, pc), 4 * px + 2 * py + pc))
    return out, 4 * x + 2 * y + c


def _tie(v, deps, name):
    def body(v_ref, *rest):
        rest[-1][...] = v_ref[...]

    vmem = pl.BlockSpec(memory_space=pltpu.VMEM)
    return pl.pallas_call(
        body, name=name, out_shape=_sds(v.shape, v.dtype), in_specs=[vmem] + [ANY] * len(deps), out_specs=vmem,
    )(v, *deps)


def _place(srcs, from_slot, name, deps=()):
    n = len(srcs)
    blks = [v.shape[1:] if from_slot else v.shape for v in srcs]

    nd = len(deps)

    def body(*refs):
        s_refs, l_refs = refs[:n], refs[n + nd:2 * n + nd]
        bufs, sems = refs[2 * n + nd:3 * n + nd], refs[3 * n + nd]
        x, y, c = _ids()
        me = 4 * x + 2 * y + c
        ins = [pltpu.make_async_copy(s_refs[a].at[me] if from_slot else s_refs[a], bufs[a], sems.at[a])
               for a in range(n)]
        outs = [pltpu.make_async_copy(bufs[a], l_refs[a].at[me], sems.at[n + a]) for a in range(n)]
        for cp in ins:
            cp.start()
        for a in range(n):
            ins[a].wait()
            outs[a].start()
        for cp in outs:
            cp.wait()

    scratch = [(b, v.dtype) for b, v in zip(blks, srcs)]
    return pl.pallas_call(
        body, name=name, out_shape=[_sds((NDEV,) + b, v.dtype) for b, v in zip(blks, srcs)],
        in_specs=[ANY] * (n + nd), out_specs=[ANY] * n,
        scratch_shapes=[pltpu.VMEM(b, d) for b, d in scratch] + [pltpu.SemaphoreType.DMA((2 * n,))],
        compiler_params=_params([], scratch),
    )(*srcs, *deps)


SEND_PEERS = {True: 7, False: 7, "level1": 4, "level2": 3}


def _send_copies(s_refs, l_refs, ssem, rsem, mode, receiving):
    peers, me = _peers()
    x, y, c = _ids()
    sibling = (x, y, 1 - c)
    chips = [(1 - x, y), (x, 1 - y), (1 - x, 1 - y)]
    npeer = SEND_PEERS[mode]
    out = []
    for a in range(len(l_refs)):
        if mode == "level2":
            for k, (px, py) in enumerate(chips):
                slot = 4 * px + 2 * py + (1 - c if receiving else c)
                out.append(pltpu.make_async_remote_copy(
                    src_ref=l_refs[a].at[slot], dst_ref=l_refs[a].at[slot], send_sem=ssem.at[npeer * a + k],
                    recv_sem=rsem.at[npeer * a + k], device_id=sibling, device_id_type=MESH_ID))
            continue
        targets = peers
        if mode == "level1":
            targets = [(sibling, 4 * x + 2 * y + 1 - c)] + [((px, py, c), 4 * px + 2 * py + c) for px, py in chips]
        for k, (dev, idx) in enumerate(targets):
            out.append(pltpu.make_async_remote_copy(
                src_ref=s_refs[a].at[idx] if mode is True else s_refs[a],
                dst_ref=l_refs[a].at[idx if receiving else me],
                send_sem=ssem.at[npeer * a + k], recv_sem=rsem.at[npeer * a + k], device_id=dev, device_id_type=MESH_ID))
    return out


def _send_start(srcs, lands, mode, name):
    ns, n = len(srcs), len(lands)
    nsem = SEND_PEERS[mode] * n

    def body(*refs):
        s_refs, l_refs = refs[:ns], refs[ns:ns + n]
        ssem, rsem = refs[ns + n], refs[ns + n + 1]
        token = refs[-1]
        for send in _send_copies(s_refs, l_refs, ssem, rsem, mode, False):
            send.start()
        token[...] = jnp.zeros_like(token)

    bufs = list(srcs) + list(lands)
    outs = pl.pallas_call(
        body, name=name,
        out_shape=[pltpu.SemaphoreType.DMA((nsem,)), pltpu.SemaphoreType.DMA((nsem,))]
        + [pltpu.HBM(v.shape, v.dtype) for v in bufs] + [_sds((SUB, LANE), F32)],
        in_specs=[HBM] * (ns + n), out_specs=[SEM, SEM] + [HBM] * (ns + n) + [pl.BlockSpec(memory_space=pltpu.VMEM)],
        input_output_aliases={i: 2 + i for i in range(ns + n)},
        compiler_params=pltpu.CompilerParams(has_side_effects=EFFECT),
    )(*[pltpu.with_memory_space_constraint(v, pltpu.HBM) for v in bufs])
    return outs[0], outs[1], list(outs[2:2 + ns]), list(outs[2 + ns:2 + ns + n]), outs[-1]


def _send_wait(started, after, mode, name):
    ssem, rsem, srcs, lands, _ = started
    ns, n = len(srcs), len(lands)

    def body(*refs):
        s_refs, l_refs = refs[:ns], refs[ns:ns + n]
        ssem_ref, rsem_ref = refs[ns + n], refs[ns + n + 1]
        for recv in _send_copies(s_refs, l_refs, ssem_ref, rsem_ref, mode, True):
            recv.wait_send()
            recv.wait_recv()

    bufs = list(srcs) + list(lands)
    outs = pl.pallas_call(
        body, name=name, out_shape=[pltpu.HBM(v.shape, v.dtype) for v in bufs],
        in_specs=[HBM] * (ns + n) + [SEM, SEM, ANY], out_specs=[HBM] * (ns + n),
        input_output_aliases={i: i for i in range(ns + n)},
        compiler_params=pltpu.CompilerParams(has_side_effects=EFFECT),
    )(*bufs, ssem, rsem, after)
    return list(outs[ns:])


NN = (((1,), (0,)), ((), ()))
NT = (((1,), (1,)), ((), ()))
TN_DIMS = (((0,), (0,)), ((), ()))


def _mm(a, b, *, grid, a_spec, b_spec, o_spec, out_shape, acc_shape, dims, name, dep=None, fill=None):
    k_axis = len(grid) - 1
    nk = grid[k_axis]
    extra = [v for v in (dep, fill) if v is not None]
    aliases = {} if fill is None else {1 + len(extra): 0}

    def body(a_ref, b_ref, *rest):
        o_ref, acc_ref = rest[-2], rest[-1]
        k = pl.program_id(k_axis)

        def prod():
            return lax.dot_general(a_ref[...], b_ref[...], dims, preferred_element_type=F32)

        if nk == 1:
            o_ref[...] = prod().astype(o_ref.dtype)
            return

        @pl.when(k == 0)
        def _():
            acc_ref[...] = prod()

        if nk > 2:
            @pl.when((k > 0) & (k < nk - 1))
            def _():
                acc_ref[...] += prod()

        @pl.when(k == nk - 1)
        def _():
            o_ref[...] = (acc_ref[...] + prod()).astype(o_ref.dtype)

    blocks = [(a_spec.block_shape, a.dtype), (b_spec.block_shape, b.dtype), (o_spec.block_shape, out_shape.dtype)]
    return pl.pallas_call(
        body, name=name, grid=grid, in_specs=[a_spec, b_spec] + [ANY] * len(extra), out_specs=o_spec,
        out_shape=out_shape, scratch_shapes=[pltpu.VMEM(acc_shape, F32)], input_output_aliases=aliases,
        compiler_params=_params(blocks, [(acc_shape, F32)], ("parallel",) * k_axis + ("arbitrary",)),
    )(a, b, *extra)


def _mm_plain(a, b, dims, out_dtype, name):
    if dims == TN_DIMS:
        (K, M), N = a.shape, b.shape[1]
    elif dims == NT:
        (M, K), N = a.shape, b.shape[0]
    else:
        (M, K), N = a.shape, b.shape[1]
    tm, tn = _tile(M, TM, LANE), _tile(N, TN, LANE)
    tk = _tile(K, TK, LANE if dims != TN_DIMS else SUB16)
    if dims == TN_DIMS:
        a_spec = pl.BlockSpec((tk, tm), lambda i, j, k: (k, i))
    else:
        a_spec = pl.BlockSpec((tm, tk), lambda i, j, k: (i, k))
    if dims == NT:
        b_spec = pl.BlockSpec((tn, tk), lambda i, j, k: (j, k))
    else:
        b_spec = pl.BlockSpec((tk, tn), lambda i, j, k: (k, j))
    return _mm(a, b, grid=(M // tm, N // tn, K // tk), a_spec=a_spec, b_spec=b_spec,
               o_spec=pl.BlockSpec((tm, tn), lambda i, j, k: (i, j)),
               out_shape=_sds((M, N), out_dtype), acc_shape=(tm, tn), dims=dims, name=name)


def _mm_proj(h_all, win_q, q, nparts, fill, name):
    n, D = h_all.shape
    nbp = win_q.shape[2]
    nb = nbp * nparts
    tm, tn, tk = _tile(n, TM, SUB16), _tile(nbp, TN, LANE), D
    nbn = nbp // tn
    return _mm(h_all, win_q, grid=(n // tm, NDEV * nbn, D // tk),
               a_spec=pl.BlockSpec((tm, tk), lambda i, j, k: (i, k)),
               b_spec=pl.BlockSpec((None, tk, tn), lambda i, j, k: (j // nbn, k, j % nbn)),
               o_spec=pl.BlockSpec((tm, tn), lambda i, j, k: (i, (j // nbn) * (nb // tn) + q * nbn + j % nbn)),
               out_shape=_sds((n, NDEV * nb), F32), acc_shape=(tm, tn), dims=NN, name=name, fill=fill)


def _mm_dh(dproj, wins, name, dep):
    nparts = len(wins)
    n = dproj.shape[0]
    _, D, nbp = wins[0].shape
    nb = nbp * nparts
    tm, tn, tk = _tile(n, TM, SUB16), _tile(D, 2 * TN, LANE), _tile(nbp, TK // 2, LANE)
    nbk = nbp // tk
    kq = NDEV * nbk
    nk = nparts * kq

    def a_index(i, j, k):
        r = k % kq
        return i, (r // nbk) * (nb // tk) + (k // kq) * nbk + r % nbk

    def b_index(q):
        def index(i, j, k):
            r = jnp.clip(k - q * kq, 0, kq - 1)
            return r // nbk, j, r % nbk
        return index

    def body(a_ref, *rest):
        b_refs, o_ref, acc_ref = rest[:nparts], rest[-2], rest[-1]
        k = pl.program_id(2)
        for q in range(nparts):
            def prod(b_ref=b_refs[q]):
                return lax.dot_general(a_ref[...], b_ref[...], NT, preferred_element_type=F32)

            lo, hi = q * kq + (q == 0), (q + 1) * kq - (q == nparts - 1)
            if q == 0:
                @pl.when(k == 0)
                def _(prod=prod):
                    acc_ref[...] = prod()

            if hi > lo:
                @pl.when((k >= lo) & (k < hi))
                def _(prod=prod):
                    acc_ref[...] += prod()

            if q == nparts - 1:
                @pl.when(k == nk - 1)
                def _(prod=prod):
                    o_ref[...] = acc_ref[...] + prod()

    assert nk >= 2
    blocks = [((tm, tk), dproj.dtype)] + [((tn, tk), wins[0].dtype)] * nparts + [((tm, tn), F32)]
    return pl.pallas_call(
        body, name=name, grid=(n // tm, D // tn, nk),
        in_specs=[pl.BlockSpec((tm, tk), a_index)]
        + [pl.BlockSpec((None, tn, tk), b_index(q)) for q in range(nparts)] + [ANY],
        out_specs=pl.BlockSpec((tm, tn), lambda i, j, k: (i, j)), out_shape=_sds((n, D), F32),
        scratch_shapes=[pltpu.VMEM((tm, tn), F32)],
        compiler_params=_params(blocks, [((tm, tn), F32)], ("parallel", "parallel", "arbitrary")),
    )(dproj, *wins, dep)


def _transpose(x, name):
    R, C = x.shape
    tr, tc = _tile(R, TL, LANE), _tile(C, 2 * TL, LANE)

    def body(x_ref, o_ref):
        o_ref[...] = x_ref[...].T

    return pl.pallas_call(
        body, name=name, grid=(R // tr, C // tc),
        in_specs=[pl.BlockSpec((tr, tc), lambda i, j: (i, j))],
        out_specs=pl.BlockSpec((tc, tr), lambda i, j: (j, i)),
        out_shape=_sds((C, R), x.dtype),
        compiler_params=_params([((tr, tc), x.dtype)] * 2, dims=("parallel", "parallel")),
    )(x)


def _mm_gwin(h_t, dproj, nb, part, nparts, name, dep=None):
    D, n = h_t.shape
    nbp = nb // nparts
    tm, tn, tk = _tile(D, TM, LANE), _tile(nbp, TN, LANE), n
    nbn = nbp // tn
    return _mm(h_t, dproj, grid=(D // tm, NDEV * nbn, n // tk),
               a_spec=pl.BlockSpec((tm, tk), lambda i, j, k: (i, k)),
               b_spec=pl.BlockSpec((tk, tn), lambda i, j, k: (k, (j // nbn) * (nb // tn) + part * nbn + j % nbn)),
               o_spec=pl.BlockSpec((None, tm, tn), lambda i, j, k: (j // nbn, i, j % nbn)),
               out_shape=_sds((NDEV, D, nbp), MXU), acc_shape=(tm, tn), dims=NN, name=name, dep=dep)


def _mm_group(a, b, mode, out_dtype, name):
    if mode == "wgrad":
        L, W = a.shape
        G = len(POOL_WINDOWS)
        pd = W // G
        tm, tn, tk = _tile(pd, TM, LANE), _tile(pd, TN, LANE), _tile(L, TK, SUB16)
        nm, nn = pd // tm, pd // tn
        return _mm(a, b, grid=(G, nm, nn, L // tk),
                   a_spec=pl.BlockSpec((tk, tm), lambda g, i, j, k: (k, g * nm + i)),
                   b_spec=pl.BlockSpec((tk, tn), lambda g, i, j, k: (k, g * nn + j)),
                   o_spec=pl.BlockSpec((None, tm, tn), lambda g, i, j, k: (g, i, j)),
                   out_shape=_sds((G, pd, pd), out_dtype), acc_shape=(tm, tn), dims=TN_DIMS, name=name)
    L, W = a.shape
    G, pd, _ = b.shape
    tm, tn, tk = _tile(L, TM, SUB16), _tile(pd, TN, LANE), _tile(pd, TK, LANE)
    nn, nk = pd // tn, pd // tk
    if mode == "fwd":
        b_spec = pl.BlockSpec((None, tk, tn), lambda g, i, j, k: (g, k, j))
        dims = NN
    else:
        b_spec = pl.BlockSpec((None, tn, tk), lambda g, i, j, k: (g, j, k))
        dims = NT
    return _mm(a, b, grid=(G, L // tm, nn, nk),
               a_spec=pl.BlockSpec((tm, tk), lambda g, i, j, k: (i, g * nk + k)),
               b_spec=b_spec,
               o_spec=pl.BlockSpec((tm, tn), lambda g, i, j, k: (i, g * nn + j)),
               out_shape=_sds((L, W), out_dtype), acc_shape=(tm, tn), dims=dims, name=name)


def _ada_fwd(cc, w_loc, b_loc, name):
    R, D = cc.shape
    na = w_loc.shape[1]
    tk = _tile(D, 512, LANE)

    def body(c_ref, w_ref, b_ref, mod_ref, s_ref):
        k = pl.program_id(0)
        cv = c_ref[...]
        s = cv * _sigmoid_small(cv)
        s_ref[...] = s

        @pl.when(k == 0)
        def _():
            mod_ref[...] = jnp.broadcast_to(b_ref[...], mod_ref.shape)

        mod_ref[...] += lax.dot_general(s.astype(MXU), w_ref[...].astype(MXU), NN, preferred_element_type=F32)

    blocks = [((R, tk), F32), ((tk, na), F32), ((1, na), F32), ((R, na), F32), ((R, tk), F32)]
    return pl.pallas_call(
        body, name=name, grid=(D // tk,),
        in_specs=[pl.BlockSpec((R, tk), lambda k: (0, k)), pl.BlockSpec((tk, na), lambda k: (k, 0)),
                  pl.BlockSpec((1, na), lambda k: (0, 0))],
        out_specs=[pl.BlockSpec((R, na), lambda k: (0, 0)), pl.BlockSpec((R, tk), lambda k: (0, k))],
        out_shape=[_sds((R, na), F32), _sds((R, D), F32)],
        compiler_params=_params(blocks, dims=("arbitrary",)),
    )(cc, w_loc, b_loc)


def _adam(w, g, m, v):
    m = ADAM_B1 * m + (1.0 - ADAM_B1) * g
    v = ADAM_B2 * v + (1.0 - ADAM_B2) * (g * g)
    m_hat = m / (1.0 - ADAM_B1 ** ADAM_STEP)
    v_hat = v / (1.0 - ADAM_B2 ** ADAM_STEP)
    delta = -ADAM_LR * (m_hat / (jnp.sqrt(v_hat) + ADAM_EPS) + ADAM_WD * w)
    return delta, m, v


def _ada_bwd(s_all, ga, gc, w_loc, m_loc, v_loc, name):
    D, na = w_loc.shape
    tr = _tile(D, 256, LANE)

    def body(s_ref, ga_ref, gc_ref, w_ref, m_ref, v_ref, g_ref, d_ref, nm_ref, nv_ref, pc_ref):
        dmc = gc_ref[0:1, :]
        for p in range(1, NDEV):
            dmc = dmc + gc_ref[p:p + 1, :]
        rows = lax.broadcasted_iota(jnp.int32, (NDEV, na), 0)
        dmc8 = jnp.where(rows == 0, jnp.broadcast_to(dmc, (NDEV, na)), 0.0)
        dm = jnp.concatenate([ga_ref[...], dmc8], axis=0).astype(MXU)
        dmc16 = jnp.concatenate([dmc8, jnp.zeros_like(dmc8)], axis=0).astype(MXU)
        w = w_ref[...]
        g = lax.dot_general(s_ref[...].astype(MXU), dm, TN_DIMS, preferred_element_type=F32)
        pc_ref[...] = lax.dot_general(dmc16, w.astype(MXU), NT, preferred_element_type=F32)
        delta, nm, nv = _adam(w, g, m_ref[...], v_ref[...])
        g_ref[...] = g
        d_ref[...] = delta
        nm_ref[...] = nm
        nv_ref[...] = nv

    big = pl.BlockSpec((tr, na), lambda i: (i, 0))
    full = pl.BlockSpec((NDEV, na), lambda i: (0, 0))
    srow = pl.BlockSpec((2 * NDEV, tr), lambda i: (0, i))
    blocks = [((2 * NDEV, tr), F32)] * 2 + [((NDEV, na), F32)] * 2 + [((tr, na), F32)] * 7
    return pl.pallas_call(
        body, name=name, grid=(D // tr,),
        in_specs=[srow, full, full, big, big, big],
        out_specs=[big, big, big, big, srow],
        out_shape=[_sds((D, na), F32)] * 4 + [_sds((2 * NDEV, D), F32)],
        compiler_params=_params(blocks, dims=("parallel",)),
    )(s_all, ga, gc, w_loc, m_loc, v_loc)


def _norm_mod(x2, g, shift, scale, n, row0, h_prev, name):
    R, D = x2.shape
    tl = _tile(R, TL, SUB16)
    assert row0 % tl == 0
    b0 = row0 // tl

    def body(x_ref, g_ref, sh_ref, sc_ref, *rest):
        o_ref = rest[-1]
        xv = x_ref[...]
        s = lax.rsqrt(jnp.mean(xv * xv, axis=-1, keepdims=True) + EPS)
        nrm = xv * s * g_ref[...]
        o_ref[...] = (nrm * (1.0 + sc_ref[...]) + sh_ref[...]).astype(o_ref.dtype)

    vec = pl.BlockSpec((1, D), lambda i: (0, 0))
    in_specs = [pl.BlockSpec((tl, D), lambda i: (i, 0)), vec, vec, vec]
    args = [x2, g, shift, scale]
    aliases = {}
    if h_prev is not None:
        in_specs.append(ANY)
        args.append(h_prev)
        aliases = {4: 0}
    blocks = [((tl, D), F32), ((tl, D), MXU)] + [((1, D), F32)] * 3
    return pl.pallas_call(
        body, name=name, grid=(R // tl,), in_specs=in_specs,
        out_specs=pl.BlockSpec((tl, D), lambda i: (i + b0, 0)),
        out_shape=_sds((n, D), MXU), input_output_aliases=aliases,
        compiler_params=_params(blocks, dims=("parallel",)),
    )(*args)


def _norm_bwd(x2, dh_all, row0, g, scale, dxn, ggn0, name):
    R, D = x2.shape
    tl = _tile(R, TL_FINAL, SUB)
    assert row0 % tl == 0
    b0 = row0 // tl
    with_x = dxn is not None

    def body(*refs):
        if with_x:
            x_ref, dh_ref, g_ref, sc_ref, gg0_ref, dxn_ref, gx_ref, dsh_ref, dsc_ref, gg_ref = refs
        else:
            x_ref, dh_ref, g_ref, sc_ref, gg0_ref, dsh_ref, dsc_ref, gg_ref = refs
        i = pl.program_id(0)

        @pl.when(i == 0)
        def _():
            dsh_ref[...] = jnp.zeros_like(dsh_ref)
            dsc_ref[...] = jnp.zeros_like(dsc_ref)
            gg_ref[...] = gg0_ref[...]

        xv = x_ref[...]
        dh = dh_ref[...]
        gv = g_ref[...]
        s = lax.rsqrt(jnp.mean(xv * xv, axis=-1, keepdims=True) + EPS)
        xh = xv * s
        dsh_ref[...] += jnp.sum(dh, axis=0, keepdims=True)
        dsc_ref[...] += jnp.sum(dh * (xh * gv), axis=0, keepdims=True)
        dn = dh * (1.0 + sc_ref[...])
        gg_ref[...] += jnp.sum(dn * xh, axis=0, keepdims=True)
        if with_x:
            dxh = dn * gv
            dx = s * (dxh - xh * jnp.mean(dxh * xh, axis=-1, keepdims=True))
            gx_ref[...] = dx + dxn_ref[...]

    vec = pl.BlockSpec((1, D), lambda i: (0, 0))
    row = pl.BlockSpec((tl, D), lambda i: (i, 0))
    in_specs = [row, pl.BlockSpec((tl, D), lambda i: (i + b0, 0)), vec, vec, vec]
    args = [x2, dh_all, g, scale, ggn0]
    out_specs = [vec, vec, vec]
    out_shape = [_sds((1, D), F32)] * 3
    if with_x:
        in_specs.append(row)
        args.append(dxn)
        out_specs = [row] + out_specs
        out_shape = [_sds((R, D), F32)] + out_shape
    blocks = [((tl, D), F32)] * (4 if with_x else 2) + [((1, D), F32)] * 6
    outs = pl.pallas_call(
        body, name=name, grid=(R // tl,), in_specs=in_specs, out_specs=out_specs, out_shape=out_shape,
        compiler_params=_params(blocks, dims=("arbitrary",)),
    )(*args)
    return tuple(outs) if with_x else (None,) + tuple(outs)


def _tap_valid(t, o, lc, n):
    tt = t + o
    in_ctx = t < lc
    return (tt >= jnp.where(in_ctx, 0, lc)) & (tt < jnp.where(in_ctx, lc, n))


def _conv_fwd(proj_all, cw, cb, lc, W, name):
    n = proj_all.shape[0]
    cbk = _tile(W, CB_SEQ, LANE)
    tr = _tile(n, TR_CONV, SUB16)
    ext = tr + 2 * SUB

    def body(x_ref, w_ref, b_ref, u_ref, xp_ref):
        xp_ref[0:SUB, :] = jnp.zeros((SUB, cbk), F32)
        xp_ref[n + SUB:n + 2 * SUB, :] = jnp.zeros((SUB, cbk), F32)
        xp_ref[SUB:n + SUB, :] = x_ref[...]
        w = w_ref[...]
        bias = b_ref[...]

        def chunk(ci, carry):
            r0 = pl.multiple_of(ci * tr, SUB16)
            xe = xp_ref[pl.ds(r0, ext), :]
            t = r0 + lax.broadcasted_iota(jnp.int32, (tr, cbk), 0)
            acc = jnp.broadcast_to(bias, (tr, cbk))
            for k in range(4):
                o = k - 1
                sh = xe if o == 0 else pltpu.roll(xe, (-o) % ext, 0)
                acc = acc + jnp.where(_tap_valid(t, o, lc, n), sh[SUB:tr + SUB], 0.0) * w[k:k + 1]
            u_ref[pl.ds(r0, tr), :] = acc
            return carry

        lax.fori_loop(0, n // tr, chunk, 0)

    blocks = [((n, cbk), F32)] * 2 + [((4, cbk), F32), ((1, cbk), F32)]
    scratch = [((n + 2 * SUB, cbk), F32)]
    return pl.pallas_call(
        body, name=name, grid=(W // cbk,),
        in_specs=[pl.BlockSpec((n, cbk), lambda j: (0, j)), pl.BlockSpec((4, cbk), lambda j: (0, j)),
                  pl.BlockSpec((1, cbk), lambda j: (0, j))],
        out_specs=pl.BlockSpec((n, cbk), lambda j: (0, j)),
        out_shape=_sds((n, W), F32),
        scratch_shapes=[pltpu.VMEM(s, d) for s, d in scratch],
        compiler_params=_params(blocks, scratch, ("parallel",)),
    )(proj_all, cw, cb)


def _conv_bwd(du_all, proj_all, cw, dproj, lc, W, name):
    n = du_all.shape[0]
    cbk = _tile(W, CB_SEQ, LANE)
    tr = _tile(n, TR_CONV, SUB16)
    ext = tr + 2 * SUB

    def body(du_ref, x_ref, w_ref, dp_in, dx_ref, gw_ref, gb_ref, dp_ref, xp_ref):
        del dp_in
        for ref, src in ((dp_ref, du_ref), (xp_ref, x_ref)):
            ref[0:SUB, :] = jnp.zeros((SUB, cbk), F32)
            ref[n + SUB:n + 2 * SUB, :] = jnp.zeros((SUB, cbk), F32)
            ref[SUB:n + SUB, :] = src[...]
        w = w_ref[...]

        def fold(v):
            return jnp.sum(v.reshape(tr // SUB, SUB, cbk), axis=0)

        def chunk(ci, carry):
            r0 = pl.multiple_of(ci * tr, SUB16)
            de = dp_ref[pl.ds(r0, ext), :]
            xe = xp_ref[pl.ds(r0, ext), :]
            t = r0 + lax.broadcasted_iota(jnp.int32, (tr, cbk), 0)
            d0 = de[SUB:tr + SUB]
            dx = jnp.zeros((tr, cbk), F32)
            new = []
            for k in range(4):
                o = k - 1
                dsh = de if o == 0 else pltpu.roll(de, o % ext, 0)
                dx = dx + jnp.where(_tap_valid(t, -o, lc, n), dsh[SUB:tr + SUB], 0.0) * w[k:k + 1]
                xsh = xe if o == 0 else pltpu.roll(xe, (-o) % ext, 0)
                new.append(carry[k] + fold(d0 * jnp.where(_tap_valid(t, o, lc, n), xsh[SUB:tr + SUB], 0.0)))
            new.append(carry[4] + fold(d0))
            dx_ref[pl.ds(r0, tr), :] = dx.astype(dx_ref.dtype)
            return tuple(new)

        zero = jnp.zeros((SUB, cbk), F32)
        acc = lax.fori_loop(0, n // tr, chunk, (zero,) * 5)
        for k in range(4):
            gw_ref[k:k + 1, :] = jnp.sum(acc[k], axis=0, keepdims=True)
        gb_ref[...] = jnp.sum(acc[4], axis=0, keepdims=True)

    col = pl.BlockSpec((n, cbk), lambda j: (0, j))
    blocks = [((n, cbk), F32)] * 2 + [((n, cbk), MXU), ((4, cbk), F32), ((4, cbk), F32), ((1, cbk), F32)]
    scratch = [((n + 2 * SUB, cbk), F32)] * 2
    return pl.pallas_call(
        body, name=name, grid=(W // cbk,),
        in_specs=[col, col, pl.BlockSpec((4, cbk), lambda j: (0, j)), ANY],
        out_specs=[col, pl.BlockSpec((4, cbk), lambda j: (0, j)), pl.BlockSpec((1, cbk), lambda j: (0, j))],
        out_shape=[_sds(dproj.shape, dproj.dtype), _sds((4, W), F32), _sds((1, W), F32)],
        input_output_aliases={3: 0},
        scratch_shapes=[pltpu.VMEM(s, d) for s, d in scratch],
        compiler_params=_params(blocks, scratch, ("parallel",)),
    )(du_all, proj_all, cw, dproj)


def _gate_coeffs(ub, u, d, wr_ref, wi_ref, br_ref, bi_ref, lam_ref):
    c = -LRU_C * _softplus(-lam_ref[d:d + 1, :])
    r = _sigmoid(lax.dot_general(ub, wr_ref[d], NN, preferred_element_type=F32) + br_ref[d:d + 1, :])
    ig = _sigmoid(lax.dot_general(ub, wi_ref[d], NN, preferred_element_type=F32) + bi_ref[d:d + 1, :])
    la = c * r
    a = jnp.exp(la)
    sq = jnp.sqrt(-jnp.tanh(la) * (1.0 + a * a))
    return c, r, ig, a, sq


def _gate_specs(tl, hd):
    w_spec = pl.BlockSpec((2, None, hd, hd), lambda h, i: (0, h, 0, 0))
    v_spec = pl.BlockSpec((2, hd), lambda h, i: (0, h))
    return w_spec, v_spec


def _gates_fwd(u_all, wr, wi, br, bi, lam, name):
    n, W = u_all.shape
    heads, hd = wr.shape[1], wr.shape[2]
    tl = _tile(n, TL_GATES, SUB16)

    def body(u_ref, wr_ref, wi_ref, br_ref, bi_ref, lam_ref, a_ref, b_ref):
        u = u_ref[...]
        ub = u.astype(MXU)
        for d in range(2):
            _, _, ig, a, sq = _gate_coeffs(ub, u, d, wr_ref, wi_ref, br_ref, bi_ref, lam_ref)
            a_ref[d] = a
            b_ref[d] = sq * (ig * u)

    w_spec, v_spec = _gate_specs(tl, hd)
    o_spec = pl.BlockSpec((2, tl, hd), lambda h, i: (0, i, h))
    blocks = [((tl, hd), F32), ((2, hd, hd), MXU), ((2, hd, hd), MXU)] + [((2, hd), F32)] * 3 + [((2, tl, hd), F32)] * 2
    return pl.pallas_call(
        body, name=name, grid=(heads, n // tl),
        in_specs=[pl.BlockSpec((tl, hd), lambda h, i: (i, h)), w_spec, w_spec, v_spec, v_spec, v_spec],
        out_specs=[o_spec, o_spec], out_shape=[_sds((2, n, W), F32)] * 2,
        compiler_params=_params(blocks, dims=("parallel", "parallel")),
    )(u_all, wr, wi, br, bi, lam)


def _gates_bwd(u_all, da, db, wr, wi, br, bi, lam, name):
    n, W = u_all.shape
    heads, hd = wr.shape[1], wr.shape[2]
    tl = _tile(n, TL_GATES, SUB16)
    ni = n // tl

    def body(u_ref, da_ref, db_ref, wr_ref, wi_ref, br_ref, bi_ref, lam_ref,
             du_ref, gwr_ref, gwi_ref, gbr_ref, gbi_ref, gc_ref, accr_ref, acci_ref):
        i = pl.program_id(1)

        @pl.when(i == 0)
        def _():
            accr_ref[...] = jnp.zeros_like(accr_ref)
            acci_ref[...] = jnp.zeros_like(acci_ref)
            gbr_ref[...] = jnp.zeros_like(gbr_ref)
            gbi_ref[...] = jnp.zeros_like(gbi_ref)
            gc_ref[...] = jnp.zeros_like(gc_ref)

        u = u_ref[...]
        ub = u.astype(MXU)
        du = jnp.zeros_like(u)
        for d in range(2):
            c, r, ig, a, sq = _gate_coeffs(ub, u, d, wr_ref, wi_ref, br_ref, bi_ref, lam_ref)
            dbv = db_ref[d]
            t = dbv * sq
            du = du + t * ig
            d_la = da_ref[d] * a - (dbv * ig * u) * (a * a) / sq
            gc_ref[d:d + 1, :] += jnp.sum(d_la * r, axis=0, keepdims=True)
            d_pr = (d_la * c) * (r * (1.0 - r))
            d_pi = (t * u) * (ig * (1.0 - ig))
            gbr_ref[d:d + 1, :] += jnp.sum(d_pr, axis=0, keepdims=True)
            gbi_ref[d:d + 1, :] += jnp.sum(d_pi, axis=0, keepdims=True)
            pb = d_pr.astype(MXU)
            qb = d_pi.astype(MXU)
            du = du + lax.dot_general(pb, wr_ref[d], NT, preferred_element_type=F32)
            du = du + lax.dot_general(qb, wi_ref[d], NT, preferred_element_type=F32)
            accr_ref[d] += lax.dot_general(ub, pb, TN_DIMS, preferred_element_type=F32)
            acci_ref[d] += lax.dot_general(ub, qb, TN_DIMS, preferred_element_type=F32)
        du_ref[...] = du

        @pl.when(i == ni - 1)
        def _():
            gwr_ref[...] = accr_ref[...].astype(gwr_ref.dtype)
            gwi_ref[...] = acci_ref[...].astype(gwi_ref.dtype)

    w_spec, v_spec = _gate_specs(tl, hd)
    u_spec = pl.BlockSpec((tl, hd), lambda h, i: (i, h))
    ab_spec = pl.BlockSpec((2, tl, hd), lambda h, i: (0, i, h))
    blocks = ([((tl, hd), F32)] * 2 + [((2, tl, hd), F32)] * 2 + [((2, hd, hd), MXU)] * 4 + [((2, hd), F32)] * 6)
    scratch = [((2, hd, hd), F32)] * 2
    return pl.pallas_call(
        body, name=name, grid=(heads, ni),
        in_specs=[u_spec, ab_spec, ab_spec, w_spec, w_spec, v_spec, v_spec, v_spec],
        out_specs=[u_spec, w_spec, w_spec, v_spec, v_spec, v_spec],
        out_shape=[_sds((n, W), F32), _sds(wr.shape, MXU), _sds(wi.shape, MXU)] + [_sds((2, W), F32)] * 3,
        scratch_shapes=[pltpu.VMEM(s, d) for s, d in scratch],
        compiler_params=_params(blocks, scratch, ("parallel", "arbitrary")),
    )(u_all, da, db, wr, wi, br, bi, lam)


def _tile_scan(A, B, rows, reverse):
    for s in (1, 2, 4):
        if reverse:
            As, Bs, m = pltpu.roll(A, SUB - s, 0), pltpu.roll(B, SUB - s, 0), rows < SUB - s
        else:
            As, Bs, m = pltpu.roll(A, s, 0), pltpu.roll(B, s, 0), rows >= s
        B = jnp.where(m, A * Bs + B, B)
        A = jnp.where(m, A * As, A)
    return A, B


def _scan_chunks(n, lc):
    tc = _tile(lc, TL, SUB)
    assert n % tc == 0 and lc % tc == 0
    return tc, n // tc, lc // tc


def _scan_fwd(a_all, b_all, lc, name):
    _, n, W = a_all.shape
    cb = _tile(W, CB_SCAN, LANE)
    tc, nch, ncc = _scan_chunks(n, lc)
    ntile = tc // SUB

    def chunk(d, t):
        return jnp.where(d == 0, t, jnp.where(t < ncc, ncc - 1 - t, nch - 1 - (t - ncc)))

    def body(a_ref, b_ref, h_ref, carry_ref):
        rows = lax.broadcasted_iota(jnp.int32, (SUB, cb), 0)

        @pl.when(pl.program_id(2) == 0)
        def _():
            carry_ref[...] = jnp.zeros_like(carry_ref)

        def run(reverse):
            def step(i, h):
                r = pl.multiple_of(((ntile - 1 - i) if reverse else i) * SUB, SUB)
                A, B = _tile_scan(a_ref[pl.ds(r, SUB), :], b_ref[pl.ds(r, SUB), :], rows, reverse)
                H = A * h + B
                h_ref[pl.ds(r, SUB), :] = H
                return H[0:1, :] if reverse else H[SUB - 1:SUB, :]

            carry_ref[...] = lax.fori_loop(0, ntile, step, carry_ref[...], unroll=2)

        @pl.when(pl.program_id(1) == 0)
        def _():
            run(False)

        @pl.when(pl.program_id(1) == 1)
        def _():
            run(True)

    spec = pl.BlockSpec((None, tc, cb), lambda j, d, t: (d, chunk(d, t), j))
    return pl.pallas_call(
        body, name=name, grid=(W // cb, 2, nch), in_specs=[spec, spec], out_specs=spec,
        out_shape=_sds((2, n, W), F32), scratch_shapes=[pltpu.VMEM((1, cb), F32)],
        compiler_params=_params([((tc, cb), F32)] * 3, [((1, cb), F32)], ("parallel", "arbitrary", "arbitrary")),
    )(a_all, b_all)


def _scan_bwd(a_all, h_all, dya, lc, name):
    _, n, W = a_all.shape
    cb = _tile(W, CB_SCAN, LANE)
    tc, nch, ncc = _scan_chunks(n, lc)
    ntile = tc // SUB
    nl = nch - ncc

    def chunk(d, t):
        return jnp.where(d == 0, nch - 1 - t, jnp.where(t < nl, ncc + t, t - nl))

    def neighbour(d, t):
        c = chunk(d, t)
        below = jnp.maximum(c * ntile - 1, 0)
        above = jnp.where(c == nch - 1, 0, jnp.minimum((c + 1) * ntile, nch * ntile - 1))
        return jnp.where(d == 0, below, above)

    def body(a_ref, h_ref, hn_ref, g_ref, da_ref, db_ref, mu_ref):
        rows = lax.broadcasted_iota(jnp.int32, (SUB, cb), 0)
        d, t = pl.program_id(1), pl.program_id(2)
        c = chunk(d, t)
        has_g = c >= ncc

        @pl.when(t == 0)
        def _():
            mu_ref[...] = jnp.zeros_like(mu_ref)

        def tile(ref, j):
            return ref[pl.ds(pl.multiple_of(j * SUB, SUB), SUB), :]

        def run(up):
            if up:
                edge = jnp.where(c == ncc - 1, 0.0, hn_ref[0:1, :])
            else:
                edge = jnp.where(c > 0, hn_ref[SUB - 1:SUB, :], 0.0)

            def step(i, mu):
                j = i if up else ntile - 1 - i
                a_t = tile(a_ref, j)
                g_t = jnp.where(has_g, tile(g_ref, j), 0.0)
                if up:
                    ap = jnp.where(rows >= 1, pltpu.roll(a_t, 1, 0), 1.0)
                    nb_row = jnp.where(j < ntile - 1, tile(h_ref, jnp.minimum(j + 1, ntile - 1))[0:1, :], edge)
                    hprev = jnp.where(rows < SUB - 1, pltpu.roll(tile(h_ref, j), SUB - 1, 0), nb_row)
                else:
                    ap = jnp.where(rows < SUB - 1, pltpu.roll(a_t, SUB - 1, 0), 1.0)
                    nb_row = jnp.where(j > 0, tile(h_ref, jnp.maximum(j - 1, 0))[SUB - 1:SUB, :], edge)
                    hprev = jnp.where(rows >= 1, pltpu.roll(tile(h_ref, j), 1, 0), nb_row)
                A, B = _tile_scan(ap, g_t, rows, not up)
                lam = A * mu + B
                r = pl.multiple_of(j * SUB, SUB)
                da_ref[pl.ds(r, SUB), :] = lam * hprev
                db_ref[pl.ds(r, SUB), :] = lam
                return a_t[SUB - 1:SUB, :] * lam[SUB - 1:SUB, :] if up else a_t[0:1, :] * lam[0:1, :]

            mu_ref[...] = lax.fori_loop(0, ntile, step, mu_ref[...], unroll=2)

        @pl.when(d == 0)
        def _():
            run(False)

        @pl.when(d == 1)
        def _():
            run(True)

    spec = pl.BlockSpec((None, tc, cb), lambda j, d, t: (d, chunk(d, t), j))
    n_spec = pl.BlockSpec((None, SUB, cb), lambda j, d, t: (d, neighbour(d, t), j))
    g_spec = pl.BlockSpec((tc, cb), lambda j, d, t: (jnp.maximum(chunk(d, t) - ncc, 0), j))
    blocks = [((tc, cb), F32)] * 5 + [((SUB, cb), F32)]
    return pl.pallas_call(
        body, name=name, grid=(W // cb, 2, nch), in_specs=[spec, spec, n_spec, g_spec], out_specs=[spec, spec],
        out_shape=[_sds((2, n, W), F32)] * 2, scratch_shapes=[pltpu.VMEM((1, cb), F32)],
        compiler_params=_params(blocks, [((1, cb), F32)], ("parallel", "arbitrary", "arbitrary")),
    )(a_all, h_all, h_all, dya)


def _pool_window(v, w, tl, cb, transpose):
    half = w // 2
    pos = lax.broadcasted_iota(jnp.int32, (tl, cb), 0) % GRID_W
    cnt = (jnp.minimum(pos + half - 1, GRID_W - 1) - jnp.maximum(pos - half, 0) + 1).astype(F32)
    src = v / cnt if transpose else v

    def run_sum(s, step):
        span = 1
        while span < half:
            ok = (pos + span < GRID_W) if step > 0 else (pos - span >= 0)
            s = s + jnp.where(ok, pltpu.roll(s, (-step * span) % tl, 0), 0.0)
            span *= 2
        return s

    ahead, behind = run_sum(src, 1), run_sum(src, -1)
    if transpose:
        return behind + jnp.where(pos + 1 < GRID_W, pltpu.roll(ahead, tl - 1, 0), 0.0) - v
    return (ahead + jnp.where(pos >= 1, pltpu.roll(behind, 1, 0), 0.0)) / cnt - v


def _pool_z(src, row0, col0, L, W, transpose, dproj, name):
    G = len(POOL_WINDOWS)
    pd = W // G
    tl = _tile(L, TL, GRID_W)
    cb = _tile(pd, CB_POOL, LANE)
    assert row0 % tl == 0 and col0 % cb == 0
    rb, cbk = row0 // tl, col0 // cb
    nj = pd // cb

    def body(x_ref, *rest):
        o_ref = rest[-1]
        for gi, w in enumerate(POOL_WINDOWS):
            @pl.when(pl.program_id(0) == gi)
            def _(w=w):
                o_ref[...] = _pool_window(x_ref[...], w, tl, cb, transpose).astype(o_ref.dtype)

    plain = pl.BlockSpec((tl, cb), lambda g, i, j: (i, g * nj + j))
    window = pl.BlockSpec((tl, cb), lambda g, i, j: (i + rb, cbk + g * nj + j))
    blocks = [((tl, cb), F32), ((tl, cb), MXU)]
    if transpose:
        return pl.pallas_call(
            body, name=name, grid=(G, L // tl, nj), in_specs=[plain, ANY], out_specs=window,
            out_shape=_sds(dproj.shape, dproj.dtype), input_output_aliases={1: 0},
            compiler_params=_params(blocks, dims=("parallel",) * 3),
        )(src, dproj)
    return pl.pallas_call(
        body, name=name, grid=(G, L // tl, nj), in_specs=[window], out_specs=plain,
        out_shape=_sds((L, W), MXU),
        compiler_params=_params(blocks, dims=("parallel",) * 3),
    )(src)


def _mix_fwd(hs, proj_all, ypre, b_pool, pool_scale, lc, name):
    L, W = ypre.shape
    tl = _tile(L, TL, SUB16)
    cb = _tile(W, CB_MIX, LANE)
    nj = W // cb
    assert lc % tl == 0
    rb = lc // tl

    def body(hs_ref, ga_ref, yp_ref, gb_ref, bp_ref, ps_ref, o_ref):
        p = pl.program_id(2)

        @pl.when(p == 0)
        def _():
            g = ga_ref[...]
            o_ref[...] = ((hs_ref[0] + hs_ref[1]) * (g * _sigmoid(g))).astype(o_ref.dtype)

        @pl.when(p == 1)
        def _():
            g = gb_ref[...]
            yb = (yp_ref[...] + bp_ref[...]) * ps_ref[...]
            o_ref[...] = (yb * (g * _sigmoid(g))).astype(o_ref.dtype)

    vec = pl.BlockSpec((1, cb), lambda i, j, p: (0, j))
    blocks = [((2, tl, cb), F32)] + [((tl, cb), F32)] * 3 + [((tl, cb), MXU)]
    return pl.pallas_call(
        body, name=name, grid=(L // tl, nj, 2),
        in_specs=[pl.BlockSpec((2, tl, cb), lambda i, j, p: (0, i + rb, j)),
                  pl.BlockSpec((tl, cb), lambda i, j, p: (i + rb, 2 * nj + j)),
                  pl.BlockSpec((tl, cb), lambda i, j, p: (i, j)),
                  pl.BlockSpec((tl, cb), lambda i, j, p: (i + rb, 3 * nj + j)), vec, vec],
        out_specs=pl.BlockSpec((tl, cb), lambda i, j, p: (i, p * nj + j)),
        out_shape=_sds((L, 2 * W), MXU),
        compiler_params=_params(blocks, dims=("parallel", "parallel", "arbitrary")),
    )(hs, proj_all, ypre, proj_all, b_pool, pool_scale)


def _dsilu(g, sg):
    return sg * (1.0 + g * (1.0 - sg))


def _mixa_bwd(dmixed, hs, proj_all, dproj, lc, W, name):
    L = dmixed.shape[0]
    tl = _tile(L, TL, SUB16)
    cb = _tile(W, CB_MIX, LANE)
    nj = W // cb
    rb = lc // tl

    def body(dm_ref, hs_ref, ga_ref, dp_in, dya_ref, dga_ref):
        del dp_in
        g = ga_ref[...]
        sg = _sigmoid(g)
        dm = dm_ref[...]
        dya_ref[...] = dm * (g * sg)
        dga_ref[...] = (dm * (hs_ref[0] + hs_ref[1]) * _dsilu(g, sg)).astype(dga_ref.dtype)

    blocks = [((tl, cb), F32)] * 3 + [((2, tl, cb), F32), ((tl, cb), MXU)]
    return pl.pallas_call(
        body, name=name, grid=(L // tl, nj),
        in_specs=[pl.BlockSpec((tl, cb), lambda i, j: (i, j)),
                  pl.BlockSpec((2, tl, cb), lambda i, j: (0, i + rb, j)),
                  pl.BlockSpec((tl, cb), lambda i, j: (i + rb, 2 * nj + j)), ANY],
        out_specs=[pl.BlockSpec((tl, cb), lambda i, j: (i, j)),
                   pl.BlockSpec((tl, cb), lambda i, j: (i + rb, 2 * nj + j))],
        out_shape=[_sds((L, W), F32), _sds(dproj.shape, dproj.dtype)],
        input_output_aliases={3: 1},
        compiler_params=_params(blocks, dims=("parallel", "parallel")),
    )(dmixed, hs, proj_all, dproj)


def _mixb_bwd(dmixed, ypre, proj_all, b_pool, pool_scale, dproj, lc, W, name):
    L = dmixed.shape[0]
    tl = _tile(L, TL, SUB16)
    cb = _tile(W, CB_MIX, LANE)
    nj = W // cb
    rb = lc // tl

    def body(dm_ref, yp_ref, gb_ref, bp_ref, ps_ref, dp_in, dyp_ref, dgb_ref, gbp_ref, gps_ref):
        del dp_in
        i = pl.program_id(1)

        @pl.when(i == 0)
        def _():
            gbp_ref[...] = jnp.zeros_like(gbp_ref)
            gps_ref[...] = jnp.zeros_like(gps_ref)

        g = gb_ref[...]
        sg = _sigmoid(g)
        dm = dm_ref[...]
        yp = yp_ref[...] + bp_ref[...]
        ps = ps_ref[...]
        dyb = dm * (g * sg)
        dyp = dyb * ps
        dgb_ref[...] = (dm * (yp * ps) * _dsilu(g, sg)).astype(dgb_ref.dtype)
        dyp_ref[...] = dyp.astype(dyp_ref.dtype)
        gbp_ref[...] += jnp.sum(dyp, axis=0, keepdims=True)
        gps_ref[...] += jnp.sum(dyb * yp, axis=0, keepdims=True)

    vec = pl.BlockSpec((1, cb), lambda j, i: (0, j))
    blocks = [((tl, cb), F32)] * 3 + [((tl, cb), MXU)] * 2 + [((1, cb), F32)] * 4
    return pl.pallas_call(
        body, name=name, grid=(nj, L // tl),
        in_specs=[pl.BlockSpec((tl, cb), lambda j, i: (i, nj + j)),
                  pl.BlockSpec((tl, cb), lambda j, i: (i, j)),
                  pl.BlockSpec((tl, cb), lambda j, i: (i + rb, 3 * nj + j)), vec, vec, ANY],
        out_specs=[pl.BlockSpec((tl, cb), lambda j, i: (i, j)),
                   pl.BlockSpec((tl, cb), lambda j, i: (i + rb, 3 * nj + j)), vec, vec],
        out_shape=[_sds((L, W), MXU), _sds(dproj.shape, dproj.dtype), _sds((1, W), F32), _sds((1, W), F32)],
        input_output_aliases={5: 1},
        compiler_params=_params(blocks, dims=("parallel", "arbitrary")),
    )(dmixed, ypre, proj_all, b_pool, pool_scale, dproj)


def _dproj_init(n, lc, W, name):
    cb = _tile(W, CB_MIX, LANE)
    nj = W // cb

    def body(o_ref):
        o_ref[...] = jnp.zeros_like(o_ref)

    return pl.pallas_call(
        body, name=name, grid=(3 * nj,), in_specs=[],
        out_specs=pl.BlockSpec((lc, cb), lambda j: (0, nj + j)),
        out_shape=_sds((n, 4 * W), MXU),
        compiler_params=_params([((lc, cb), MXU)], dims=("parallel",)),
    )()


def _final(x2, out, tgt, gate, gfin, name):
    L, D = x2.shape
    tl = _tile(L, TL_FINAL, SUB16)

    def body(x_ref, o_ref, t_ref, gate_ref, g_ref, dout_ref, dxn_ref, loss_ref, ggf_ref, dgate_ref):
        i = pl.program_id(0)

        @pl.when(i == 0)
        def _():
            loss_ref[...] = jnp.zeros_like(loss_ref)
            ggf_ref[...] = jnp.zeros_like(ggf_ref)
            dgate_ref[...] = jnp.zeros_like(dgate_ref)

        o = o_ref[...]
        gate_v = gate_ref[...]
        gv = g_ref[...]
        xn = x_ref[...] + gate_v * o
        s = lax.rsqrt(jnp.mean(xn * xn, axis=-1, keepdims=True) + EPS)
        xh = xn * s
        err = xh * gv - t_ref[...]
        tok = jnp.mean(err * err, axis=-1, keepdims=True)
        loss_ref[...] += 0.5 * jnp.sum(tok, axis=0, keepdims=True)
        dy = err / D
        ggf_ref[...] += jnp.sum(dy * xh, axis=0, keepdims=True)
        dxh = dy * gv
        dxn = s * (dxh - xh * jnp.mean(dxh * xh, axis=-1, keepdims=True))
        dgate_ref[...] += jnp.sum(dxn * o, axis=0, keepdims=True)
        dout_ref[...] = (gate_v * dxn).astype(dout_ref.dtype)
        dxn_ref[...] = dxn

    row = pl.BlockSpec((tl, D), lambda i: (i, 0))
    vec = pl.BlockSpec((1, D), lambda i: (0, 0))
    blocks = [((tl, D), F32)] * 4 + [((tl, D), MXU)] + [((1, D), F32)] * 4
    return pl.pallas_call(
        body, name=name, grid=(L // tl,), in_specs=[row, row, row, vec, vec],
        out_specs=[row, row, pl.BlockSpec((1, 1), lambda i: (0, 0)), vec, vec],
        out_shape=[_sds((L, D), MXU), _sds((L, D), F32), _sds((1, 1), F32), _sds((1, D), F32), _sds((1, D), F32)],
        compiler_params=_params(blocks, dims=("arbitrary",)),
    )(x2, out, tgt, gate, gfin)


def _adamw_parts(w2, parts, m2, v2, name):
    R, C = w2.shape
    nh = len(parts)
    ch = C // nh
    tr = _tile(R, max(SUB16, (512 * 1024) // (ch * (nh + 1))), SUB16)

    def body(w_ref, *rest):
        p_refs = rest[:nh]
        m_ref, v_ref, g_ref, d_ref, nm_ref, nv_ref = rest[nh:]
        for q in range(nh):
            @pl.when(pl.program_id(1) == q)
            def _(p_ref=p_refs[q]):
                g = p_ref[0].astype(F32)
                for p in range(1, NDEV):
                    g = g + p_ref[p].astype(F32)
                delta, nm, nv = _adam(w_ref[...], g, m_ref[...], v_ref[...])
                g_ref[...] = g
                d_ref[...] = delta
                nm_ref[...] = nm
                nv_ref[...] = nv

    blk = pl.BlockSpec((tr, ch), lambda i, h: (i, h))
    p_spec = pl.BlockSpec((NDEV, tr, ch), lambda i, h: (0, i, 0))
    blocks = [((tr, ch), F32)] * 7 + [((NDEV, tr, ch), parts[0].dtype)] * nh
    return pl.pallas_call(
        body, name=name, grid=(R // tr, nh),
        in_specs=[blk] + [p_spec] * nh + [blk, blk],
        out_specs=[blk] * 4, out_shape=[_sds((R, C), F32)] * 4,
        compiler_params=_params(blocks, dims=("parallel", "arbitrary")),
    )(w2, *parts, m2, v2)


def _small_sum(vs, ga, gc, name):
    ns, nm = vs.shape[1], ga.shape[1]

    def body(v_ref, ga_ref, gc_ref, tot_ref, gb_ref):
        tot = v_ref[0:1, :]
        gb = ga_ref[0:1, :]
        for p in range(1, NDEV):
            tot = tot + v_ref[p:p + 1, :]
            gb = gb + ga_ref[p:p + 1, :]
        for p in range(NDEV):
            gb = gb + gc_ref[p:p + 1, :]
        tot_ref[...] = tot
        gb_ref[...] = gb

    blocks = [((NDEV, ns), F32), ((NDEV, nm), F32), ((NDEV, nm), F32), ((1, ns), F32), ((1, nm), F32)]
    return pl.pallas_call(
        body, name=name, out_shape=[_sds((1, ns), F32), _sds((1, nm), F32)],
        compiler_params=_params(blocks),
    )(vs, ga, gc)


def _adamw_small(g_raw, w, m, v, lam_range, cctx_range, name):
    npk = w.shape[1]

    def body(g_ref, w_ref, m_ref, v_ref, go_ref, d_ref, nm_ref, nv_ref):
        wv = w_ref[...]
        g = g_ref[...]
        idx = lax.broadcasted_iota(jnp.int32, (1, npk), 1)
        in_lam = (idx >= lam_range[0]) & (idx < lam_range[1])
        in_cc = (idx >= cctx_range[0]) & (idx < cctx_range[1])
        sg = _sigmoid_small(wv)
        g = jnp.where(in_lam, g * (LRU_C * _sigmoid_small(-wv)), jnp.where(in_cc, g * _dsilu(wv, sg), g))
        delta, nm, nv = _adam(wv, g, m_ref[...], v_ref[...])
        go_ref[...] = g
        d_ref[...] = delta
        nm_ref[...] = nm
        nv_ref[...] = nv

    return pl.pallas_call(
        body, name=name, out_shape=[_sds((1, npk), F32)] * 4,
        compiler_params=_params([((1, npk), F32)] * 8),
    )(g_raw, w, m, v)


def _pack(pieces):
    return jnp.concatenate([p.reshape(1, -1) for p in pieces], axis=1)


def kernel(x, c, ctx, c_ctx, w_ada, b_ada, g_norm, w_in, conv_w, conv_b, lru_lambda, w_rgate, b_rgate, w_igate, b_igate, w_pool, b_pool, pool_scale, w_out, g_final, loss_target, m_c_ctx, m_w_ada, m_b_ada, m_g_norm, m_w_in, m_conv_w, m_conv_b, m_lru_lambda, m_w_rgate, m_b_rgate, m_w_igate, m_b_igate, m_w_pool, m_b_pool, m_pool_scale, m_w_out, m_g_final, v_c_ctx, v_w_ada, v_b_ada, v_g_norm, v_w_in, v_conv_w, v_conv_b, v_lru_lambda, v_w_rgate, v_b_rgate, v_w_igate, v_b_igate, v_w_pool, v_b_pool, v_pool_scale, v_w_out, v_g_final):
    L, D = x.shape[1], x.shape[2]
    lc = ctx.shape[1]
    n = lc + L
    W = conv_b.shape[1]
    heads, hd = w_rgate.shape[2], w_rgate.shape[4]
    G, pd = w_pool.shape[1], w_pool.shape[3]
    na = w_ada.shape[2]
    nb = w_in.shape[2]
    ws = W // NDEV
    me = 4 * lax.axis_index("x") + 2 * lax.axis_index("y") + lax.axis_index("c")

    nbp = nb // WIN_PARTS
    w_in_parts = [w_in[0, :, q * nbp:(q + 1) * nbp].astype(MXU) for q in range(WIN_PARTS)]
    (win_0, cw_all, lam_all, br_all, bi_all, c_all) = _all_gather(
        [w_in_parts[0], conv_w[0], lru_lambda[0], b_rgate[0], b_igate[0], c], "gather_w_in")
    win = [win_0]
    cw = cw_all.transpose(1, 0, 2).reshape(4, W)
    lam = lam_all.transpose(1, 0, 2).reshape(2, W)
    br = br_all.transpose(1, 0, 2).reshape(2, W)
    bi = bi_all.transpose(1, 0, 2).reshape(2, W)

    cc = jnp.concatenate([c_all.reshape(NDEV, D), c_ctx.reshape(1, D), jnp.zeros((NDEV - 1, D), F32)], axis=0)
    b_loc = lax.dynamic_slice(b_ada, (0, me * na), (1, na))
    mod_loc, s_all = _ada_fwd(cc, w_ada[0], b_loc, "ada_fwd")
    (mod_all,) = _all_gather([mod_loc], "gather_mod")
    gate_w = [w_rgate[0].astype(MXU), w_igate[0].astype(MXU)]
    rest_w = [w_pool[0].astype(MXU), w_out[0].astype(MXU)]
    tok = mod_all
    sent_win = []
    for q in range(1, WIN_PARTS):
        part = [w_in_parts[q]]
        sent_win.append(_send_start(part, _place(part, False, f"place_w_in_{q}", [tok]), "level1", f"start_w_in_{q}"))
        tok = sent_win[-1][4]
    sent_gw = _send_start(gate_w, _place(gate_w, False, "place_gate_w", [tok]), False, "start_gate_w")
    sent_rw = _send_start(rest_w, _place(rest_w, False, "place_rest_w", [sent_gw[4]]), False, "start_rest_w")
    mod = mod_all.transpose(1, 0, 2).reshape(2 * NDEV, NDEV * na)
    mod_me = lax.dynamic_slice(mod, (me, 0), (1, 3 * D))
    shift, scale, gate = mod_me[:, :D], mod_me[:, D:2 * D], mod_me[:, 2 * D:]
    shift = _tie(shift, [sent_gw[4], sent_rw[4]], "tie_weights")
    shift_c, scale_c = mod[NDEV:NDEV + 1, :D], mod[NDEV:NDEV + 1, D:2 * D]

    x2, ctx2, tgt = x[0], ctx[0], loss_target[0]
    gfin = g_final.reshape(1, D)
    h_all = _norm_mod(x2, g_norm, shift, scale, n, lc, None, "norm_lat")
    h_all = _norm_mod(ctx2, g_norm, shift_c, scale_c, n, 0, h_all, "norm_ctx")
    proj_all = _mm_proj(h_all, win[0], 0, WIN_PARTS, None, "mm_proj_0")
    for q in range(1, WIN_PARTS):
        lands = _send_wait(sent_win[q - 1], proj_all, "level1", f"wait_w_in_{q}")
        passed = _send_start([], lands, "level2", f"pass_w_in_{q}")
        win.append(_send_wait(passed, proj_all, "level2", f"wait_pass_w_in_{q}")[0])
        proj_all = _mm_proj(h_all, win[q], q, WIN_PARTS, proj_all, f"mm_proj_{q}")
    u_all = _conv_fwd(proj_all, cw, conv_b, lc, W, "conv_fwd")
    wr_all, wi_all = _send_wait(sent_gw, u_all, False, "wait_gate_w")
    wr = wr_all.transpose(1, 2, 0, 3, 4).reshape(2, heads, hd, hd)
    wi = wi_all.transpose(1, 2, 0, 3, 4).reshape(2, heads, hd, hd)
    a_all, b_all = _gates_fwd(u_all, wr, wi, br, bi, lam, "gates_fwd")
    hs = _scan_fwd(a_all, b_all, lc, "scan_fwd")
    z = _pool_z(proj_all, lc, W, L, W, False, None, "pool_z")
    wpool_all, wout_all = _send_wait(sent_rw, hs, False, "wait_rest_w")
    wpool = wpool_all.transpose(1, 0, 2, 3).reshape(G, pd, pd)
    wout = wout_all.reshape(2 * W, D)
    ypre = _mm_group(z, wpool, "fwd", F32, "mm_pool")
    mixed = _mix_fwd(hs, proj_all, ypre, b_pool, pool_scale, lc, "mix_fwd")
    out = _mm_plain(mixed, wout, NN, F32, "mm_out")
    d_out, dxn, loss_p, ggf, dgate = _final(x2, out, tgt, gate, gfin, "final")

    dmixed = _mm_plain(d_out, wout, NT, F32, "mm_dmixed")
    gwout = _mm_plain(mixed, d_out, TN_DIMS, MXU, "mm_gwout")
    ex_o = [gwout.reshape(NDEV, 2 * W // NDEV, D)]
    sent_o = _send_start(ex_o, _place(ex_o, True, "place_gwout"), True, "start_gwout")
    dproj = _dproj_init(n, lc, W, "dproj_init")
    dya, dproj = _mixa_bwd(dmixed, hs, proj_all, dproj, lc, W, "mixa_bwd")
    dypre, dproj, gbp, gps = _mixb_bwd(dmixed, ypre, proj_all, _tie(b_pool, [sent_o[4]], "tie_gwout"), pool_scale,
                                       dproj, lc, W, "mixb_bwd")
    dz = _mm_group(dypre, wpool, "bwd", F32, "mm_dz")
    gwpool = _mm_group(z, dypre, "wgrad", MXU, "mm_gwpool")
    dproj = _pool_z(dz, lc, W, L, W, True, dproj, "pool_z_bwd")
    da, db = _scan_bwd(a_all, hs, dya, lc, "scan_bwd")
    du, gwr, gwi, gbr, gbi, gcl = _gates_bwd(u_all, da, db, wr, wi, br, bi, lam, "gates_bwd")
    ex_s = [gwpool.reshape(G, NDEV, pd // NDEV, pd).transpose(1, 0, 2, 3),
            gwr.reshape(2, heads, NDEV, hd // NDEV, hd).transpose(2, 0, 1, 3, 4),
            gwi.reshape(2, heads, NDEV, hd // NDEV, hd).transpose(2, 0, 1, 3, 4)]
    sent_s = _send_start(ex_s, _place(ex_s, True, "place_gsmall"), True, "start_gsmall")
    dproj, gcw, gcb = _conv_bwd(du, proj_all, _tie(cw, [sent_s[4]], "tie_gsmall"), dproj, lc, W, "conv_bwd")
    h_t = _transpose(h_all, "transpose_h")
    sent_i, tok = [], None
    for q in range(GWIN_PARTS):
        part = _mm_gwin(h_t, dproj, nb, q, GWIN_PARTS, f"mm_gwin_{q}", dep=tok)
        part = pltpu.with_memory_space_constraint(part, pltpu.HBM)
        sent_i.append(_send_start([part], _place([part], True, f"place_gwin_{q}"), True, f"start_gwin_{q}"))
        tok = sent_i[-1][4]
    dh_all = _mm_dh(dproj, win, "mm_dh", tok)
    grad_x, dshift, dscale, ggn = _norm_bwd(x2, dh_all, lc, g_norm, scale, dxn, jnp.zeros((1, D), F32), "norm_bwd_lat")
    _, dshift_c, dscale_c, ggn = _norm_bwd(ctx2, dh_all, 0, g_norm, scale_c, None, ggn, "norm_bwd_ctx")

    dmod_me = jnp.concatenate([dshift, dscale, dgate], axis=1)
    dmod_c = jnp.concatenate([dshift_c, dscale_c, jnp.zeros((1, D), F32)], axis=1)
    smalls = [ggf, ggn, gcw, gcb, gcl, gbr, gbi, gbp, gps, jnp.pad(loss_p, ((0, 0), (0, LANE - 1)))]
    sizes = [s.size for s in smalls]
    small_all, dmod_all, dmodc_all = _all_gather([_pack(smalls), dmod_me, dmod_c], "gather_small")
    ga = lax.dynamic_slice(dmod_all.reshape(NDEV, 3 * D), (0, me * na), (NDEV, na))
    gc = lax.dynamic_slice(dmodc_all.reshape(NDEV, 3 * D), (0, me * na), (NDEV, na))
    g_wada, d_wada, nm_wada, nv_wada, pc = _ada_bwd(s_all, ga, gc, w_ada[0], m_w_ada[0], v_w_ada[0], "ada_bwd")
    (pc_all,) = _all_gather([pc[0:1]], "gather_cctx")
    tot, gb_ada = _small_sum(
        jnp.concatenate([small_all.reshape(NDEV, -1), pc_all.reshape(NDEV, D)], axis=1),
        dmod_all.reshape(NDEV, 3 * D), dmodc_all.reshape(NDEV, 3 * D), "small_sum")
    offs = [0]
    for s in sizes + [D]:
        offs.append(offs[-1] + s)
    t_ggf, t_ggn, t_gcw, t_gcb, t_gcl, t_gbr, t_gbi, t_gbp, t_gps, t_loss, t_pc = [
        tot[:, offs[i]:offs[i + 1]] for i in range(len(offs) - 1)]

    def shard(t, rows):
        return lax.dynamic_slice(t.reshape(rows, W), (0, me * ws), (rows, ws))

    def big(wv, parts, mv, vv, name):
        shp = wv.shape
        C = shp[-1]
        if not isinstance(parts, list):
            parts = [parts]
        parts = [p.reshape(NDEV, -1, C // len(parts)) for p in parts]
        outs = _adamw_parts(wv.reshape(-1, C), parts, mv.reshape(-1, C), vv.reshape(-1, C), name)
        return [o.reshape(shp) for o in outs]

    (recv_o,) = _send_wait(sent_o, tot, True, "wait_gwout")
    recv_p, recv_r, recv_i = _send_wait(sent_s, tot, True, "wait_gsmall")
    r_wout = big(w_out, recv_o, m_w_out, v_w_out, "adamw_w_out")
    r_wpool = big(w_pool, recv_p, m_w_pool, v_w_pool, "adamw_w_pool")
    r_wr = big(w_rgate, recv_r, m_w_rgate, v_w_rgate, "adamw_w_rgate")
    r_wi = big(w_igate, recv_i, m_w_igate, v_w_igate, "adamw_w_igate")
    r_wada = [o.reshape(w_ada.shape) for o in (g_wada, d_wada, nm_wada, nv_wada)]

    names = ["c_ctx", "b_ada", "g_norm", "conv_w", "conv_b", "lru_lambda", "b_rgate", "b_igate", "b_pool",
             "pool_scale", "g_final"]
    sw = [c_ctx, b_ada, g_norm, conv_w, conv_b, lru_lambda, b_rgate, b_igate, b_pool, pool_scale, g_final]
    sm = [m_c_ctx, m_b_ada, m_g_norm, m_conv_w, m_conv_b, m_lru_lambda, m_b_rgate, m_b_igate, m_b_pool,
          m_pool_scale, m_g_final]
    sv = [v_c_ctx, v_b_ada, v_g_norm, v_conv_w, v_conv_b, v_lru_lambda, v_b_rgate, v_b_igate, v_b_pool,
          v_pool_scale, v_g_final]
    sg = [t_pc, gb_ada, t_ggn, shard(t_gcw, 4), t_gcb, shard(t_gcl, 2), shard(t_gbr, 2), shard(t_gbi, 2), t_gbp,
          t_gps, t_ggf]
    poffs = [0]
    for wv in sw:
        poffs.append(poffs[-1] + wv.size)
    lam_range = (poffs[5], poffs[6])
    cctx_range = (poffs[0], poffs[1])
    small_out = _adamw_small(_pack(sg), _pack(sw), _pack(sm), _pack(sv), lam_range, cctx_range, "adamw_small")
    recv_w = [_send_wait(sent_i[q], small_out[0], True, f"wait_gwin_{q}")[0] for q in range(GWIN_PARTS)]
    r_win = big(w_in, recv_w, m_w_in, v_w_in, "adamw_w_in")
    r_small = {}
    for i, nm in enumerate(names):
        r_small[nm] = [o[:, poffs[i]:poffs[i + 1]].reshape(sw[i].shape) for o in small_out]

    res = dict(r_small)
    res.update(w_ada=r_wada, w_in=r_win, w_rgate=r_wr, w_igate=r_wi, w_pool=r_wpool, w_out=r_wout)
    order = ["c_ctx", "w_ada", "b_ada", "g_norm", "w_in", "conv_w", "conv_b", "lru_lambda", "w_rgate", "b_rgate",
             "w_igate", "b_igate", "w_pool", "b_pool", "pool_scale", "w_out", "g_final"]
    loss = t_loss[0, 0]
    outs = [loss, grad_x.reshape(x.shape)]
    for q in range(4):
        outs += [res[nm][q] for nm in order]
    return tuple(outs)
```

```python
import functools

import jax
import jax.numpy as jnp
from jax import lax
from jax.experimental import pallas as pl
from jax.experimental.pallas import tpu as pltpu

NDEV = 8
GRID_W = 64
POOL_WINDOWS = (2, 4, 8, 16)
LRU_C = 8.0
EPS = 1e-6
ADAM_LR = 0.001
ADAM_B1 = 0.9
ADAM_B2 = 0.999
ADAM_EPS = 1e-08
ADAM_WD = 0.01
ADAM_STEP = 10

F32 = jnp.float32
MXU = jnp.bfloat16

VMEM_BYTES = 64 * 1024 * 1024
VMEM_SLACK = 8 * 1024 * 1024
SUB = 8
SUB16 = 16
LANE = 128

TM = 1152
TN = 1024
TK = 2048
TL = 256
TL_FINAL = 128
TL_GATES = 1088
CB_POOL = 1024
CB_SEQ = 256
CB_SCAN = 1024
CB_MIX = 2048
TR_CONV = 576
GWIN_PARTS = 4
WIN_PARTS = 4

MESH_ID = pl.DeviceIdType.MESH


def _tile(n, pref, align):
    if n <= pref:
        return n
    for t in range(pref - pref % align, 0, -align):
        if n % t == 0:
            return t
    return n


def _nbytes(shape, dtype):
    n = 1
    for s in shape:
        if s is not None:
            n *= s
    return n * jnp.dtype(dtype).itemsize


def _params(blocks, scratch=(), dims=None):
    need = 2 * sum(_nbytes(s, d) for s, d in blocks) + sum(_nbytes(s, d) for s, d in scratch) + VMEM_SLACK
    kw = dict(vmem_limit_bytes=int(min(max(need, 2 * VMEM_SLACK), VMEM_BYTES - VMEM_SLACK // 2)))
    if dims is not None:
        kw["dimension_semantics"] = dims
    return pltpu.CompilerParams(**kw)


def _sds(shape, dtype):
    return jax.ShapeDtypeStruct(tuple(shape), dtype)


ANY = pl.BlockSpec(memory_space=pl.ANY)


def _ids():
    return lax.axis_index("x"), lax.axis_index("y"), lax.axis_index("c")


def _sigmoid(v):
    return 0.5 * jnp.tanh(0.5 * v) + 0.5


def _sigmoid_small(v):
    return jax.nn.sigmoid(v)


def _softplus(v):
    return jnp.maximum(v, 0.0) + jnp.log1p(jnp.exp(-jnp.abs(v)))


def _all_gather(xs, name):
    n = len(xs)

    def body(*refs):
        x_refs, o_refs = refs[:n], refs[n:2 * n]
        send_sems, recv_sems, local_sems = refs[2 * n:]
        x, y, c = _ids()
        me, sibling = (x, y, c), (x, y, 1 - c)
        chips = [(1 - x, y), (x, 1 - y), (1 - x, 1 - y)]

        def slot(a, p):
            return o_refs[a].at[4 * p[0] + 2 * p[1] + p[2]]

        def copy(a, k, block, to, src=None):
            return pltpu.make_async_remote_copy(
                src_ref=slot(a, block) if src is None else src, dst_ref=slot(a, block),
                send_sem=send_sems.at[7 * a + k], recv_sem=recv_sems.at[7 * a + k],
                device_id=to, device_id_type=MESH_ID)

        mine, first, passed = [], [], []
        for a in range(n):
            m = pltpu.make_async_copy(x_refs[a], slot(a, me), local_sems.at[a])
            m.start()
            mine.append(m)
            f = [copy(a, 0, me, sibling, src=x_refs[a])]
            f += [copy(a, 1 + j, me, (*chip, c), src=x_refs[a]) for j, chip in enumerate(chips)]
            for cp in f:
                cp.start()
            first += f
        for a in range(n):
            for j, chip in enumerate(chips):
                copy(a, 1 + j, (*chip, c), me).wait_recv()
                p = copy(a, 4 + j, (*chip, c), sibling)
                p.start()
                passed.append(p)
        for a in range(n):
            copy(a, 0, sibling, me).wait_recv()
            for j, chip in enumerate(chips):
                copy(a, 4 + j, (*chip, 1 - c), me).wait_recv()
        for cp in first + passed:
            cp.wait_send()
        for m in mine:
            m.wait()

    return pl.pallas_call(
        body, name=name,
        out_shape=[_sds((NDEV,) + v.shape, v.dtype) for v in xs],
        in_specs=[ANY] * n, out_specs=[ANY] * n,
        scratch_shapes=[pltpu.SemaphoreType.DMA((7 * n,)), pltpu.SemaphoreType.DMA((7 * n,)),
                        pltpu.SemaphoreType.DMA((n,))],
    )(*xs)


HBM = pl.BlockSpec(memory_space=pltpu.HBM)
SEM = pl.BlockSpec(memory_space=pltpu.SEMAPHORE)
EFFECT = pltpu.SideEffectType.DATAFLOW_SIDE_EFFECTING


def _peers():
    x, y, c = _ids()
    out = []
    for k in range(1, NDEV):
        px = 1 - x if k & 4 else x
        py = 1 - y if k & 2 else y
        pc = 1 - c if k & 1 else c
        out.append(((px, py, pc), 4 * px + 2 * py + pc))
    return out, 4 * x + 2 * y + c


def _tie(v, deps, name):
    def body(v_ref, *rest):
        rest[-1][...] = v_ref[...]

    vmem = pl.BlockSpec(memory_space=pltpu.VMEM)
    return pl.pallas_call(
        body, name=name, out_shape=_sds(v.shape, v.dtype), in_specs=[vmem] + [ANY] * len(deps), out_specs=vmem,
    )(v, *deps)


def _place(srcs, from_slot, name, deps=()):
    n = len(srcs)
    blks = [v.shape[1:] if from_slot else v.shape for v in srcs]

    nd = len(deps)

    def body(*refs):
        s_refs, l_refs = refs[:n], refs[n + nd:2 * n + nd]
        bufs, sems = refs[2 * n + nd:3 * n + nd], refs[3 * n + nd]
        x, y, c = _ids()
        me = 4 * x + 2 * y + c
        ins = [pltpu.make_async_copy(s_refs[a].at[me] if from_slot else s_refs[a], bufs[a], sems.at[a])
               for a in range(n)]
        outs = [pltpu.make_async_copy(bufs[a], l_refs[a].at[me], sems.at[n + a]) for a in range(n)]
        for cp in ins:
            cp.start()
        for a in range(n):
            ins[a].wait()
            outs[a].start()
        for cp in outs:
            cp.wait()

    scratch = [(b, v.dtype) for b, v in zip(blks, srcs)]
    return pl.pallas_call(
        body, name=name, out_shape=[_sds((NDEV,) + b, v.dtype) for b, v in zip(blks, srcs)],
        in_specs=[ANY] * (n + nd), out_specs=[ANY] * n,
        scratch_shapes=[pltpu.VMEM(b, d) for b, d in scratch] + [pltpu.SemaphoreType.DMA((2 * n,))],
        compiler_params=_params([], scratch),
    )(*srcs, *deps)


SEND_PEERS = {True: 7, False: 7, "level1": 4, "level2": 3}


def _send_copies(s_refs, l_refs, ssem, rsem, mode, receiving):
    peers, me = _peers()
    x, y, c = _ids()
    sibling = (x, y, 1 - c)
    chips = [(1 - x, y), (x, 1 - y), (1 - x, 1 - y)]
    npeer = SEND_PEERS[mode]
    out = []
    for a in range(len(l_refs)):
        if mode == "level2":
            for k, (px, py) in enumerate(chips):
                slot = 4 * px + 2 * py + (1 - c if receiving else c)
                out.append(pltpu.make_async_remote_copy(
                    src_ref=l_refs[a].at[slot], dst_ref=l_refs[a].at[slot], send_sem=ssem.at[npeer * a + k],
                    recv_sem=rsem.at[npeer * a + k], device_id=sibling, device_id_type=MESH_ID))
            continue
        targets = peers
        if mode == "level1":
            targets = [(sibling, 4 * x + 2 * y + 1 - c)] + [((px, py, c), 4 * px + 2 * py + c) for px, py in chips]
        for k, (dev, idx) in enumerate(targets):
            out.append(pltpu.make_async_remote_copy(
                src_ref=s_refs[a].at[idx] if mode is True else s_refs[a],
                dst_ref=l_refs[a].at[idx if receiving else me],
                send_sem=ssem.at[npeer * a + k], recv_sem=rsem.at[npeer * a + k], device_id=dev, device_id_type=MESH_ID))
    return out


def _send_start(srcs, lands, mode, name):
    ns, n = len(srcs), len(lands)
    nsem = SEND_PEERS[mode] * n

    def body(*refs):
        s_refs, l_refs = refs[:ns], refs[ns:ns + n]
        ssem, rsem = refs[ns + n], refs[ns + n + 1]
        token = refs[-1]
        for send in _send_copies(s_refs, l_refs, ssem, rsem, mode, False):
            send.start()
        token[...] = jnp.zeros_like(token)

    bufs = list(srcs) + list(lands)
    outs = pl.pallas_call(
        body, name=name,
        out_shape=[pltpu.SemaphoreType.DMA((nsem,)), pltpu.SemaphoreType.DMA((nsem,))]
        + [pltpu.HBM(v.shape, v.dtype) for v in bufs] + [_sds((SUB, LANE), F32)],
        in_specs=[HBM] * (ns + n), out_specs=[SEM, SEM] + [HBM] * (ns + n) + [pl.BlockSpec(memory_space=pltpu.VMEM)],
        input_output_aliases={i: 2 + i for i in range(ns + n)},
        compiler_params=pltpu.CompilerParams(has_side_effects=EFFECT),
    )(*[pltpu.with_memory_space_constraint(v, pltpu.HBM) for v in bufs])
    return outs[0], outs[1], list(outs[2:2 + ns]), list(outs[2 + ns:2 + ns + n]), outs[-1]


def _send_wait(started, after, mode, name):
    ssem, rsem, srcs, lands, _ = started
    ns, n = len(srcs), len(lands)

    def body(*refs):
        s_refs, l_refs = refs[:ns], refs[ns:ns + n]
        ssem_ref, rsem_ref = refs[ns + n], refs[ns + n + 1]
        for recv in _send_copies(s_refs, l_refs, ssem_ref, rsem_ref, mode, True):
            recv.wait_send()
            recv.wait_recv()

    bufs = list(srcs) + list(lands)
    outs = pl.pallas_call(
        body, name=name, out_shape=[pltpu.HBM(v.shape, v.dtype) for v in bufs],
        in_specs=[HBM] * (ns + n) + [SEM, SEM, ANY], out_specs=[HBM] * (ns + n),
        input_output_aliases={i: i for i in range(ns + n)},
        compiler_params=pltpu.CompilerParams(has_side_effects=EFFECT),
    )(*bufs, ssem, rsem, after)
    return list(outs[ns:])


NN = (((1,), (0,)), ((), ()))
NT = (((1,), (1,)), ((), ()))
TN_DIMS = (((0,), (0,)), ((), ()))


def _mm(a, b, *, grid, a_spec, b_spec, o_spec, out_shape, acc_shape, dims, name, dep=None, fill=None):
    k_axis = len(grid) - 1
    nk = grid[k_axis]
    extra = [v for v in (dep, fill) if v is not None]
    aliases = {} if fill is None else {1 + len(extra): 0}

    def body(a_ref, b_ref, *rest):
        o_ref, acc_ref = rest[-2], rest[-1]
        k = pl.program_id(k_axis)

        def prod():
            return lax.dot_general(a_ref[...], b_ref[...], dims, preferred_element_type=F32)

        if nk == 1:
            o_ref[...] = prod().astype(o_ref.dtype)
            return

        @pl.when(k == 0)
        def _():
            acc_ref[...] = prod()

        if nk > 2:
            @pl.when((k > 0) & (k < nk - 1))
            def _():
                acc_ref[...] += prod()

        @pl.when(k == nk - 1)
        def _():
            o_ref[...] = (acc_ref[...] + prod()).astype(o_ref.dtype)

    blocks = [(a_spec.block_shape, a.dtype), (b_spec.block_shape, b.dtype), (o_spec.block_shape, out_shape.dtype)]
    return pl.pallas_call(
        body, name=name, grid=grid, in_specs=[a_spec, b_spec] + [ANY] * len(extra), out_specs=o_spec,
        out_shape=out_shape, scratch_shapes=[pltpu.VMEM(acc_shape, F32)], input_output_aliases=aliases,
        compiler_params=_params(blocks, [(acc_shape, F32)], ("parallel",) * k_axis + ("arbitrary",)),
    )(a, b, *extra)


def _mm_plain(a, b, dims, out_dtype, name):
    if dims == TN_DIMS:
        (K, M), N = a.shape, b.shape[1]
    elif dims == NT:
        (M, K), N = a.shape, b.shape[0]
    else:
        (M, K), N = a.shape, b.shape[1]
    tm, tn = _tile(M, TM, LANE), _tile(N, TN, LANE)
    tk = _tile(K, TK, LANE if dims != TN_DIMS else SUB16)
    if dims == TN_DIMS:
        a_spec = pl.BlockSpec((tk, tm), lambda i, j, k: (k, i))
    else:
        a_spec = pl.BlockSpec((tm, tk), lambda i, j, k: (i, k))
    if dims == NT:
        b_spec = pl.BlockSpec((tn, tk), lambda i, j, k: (j, k))
    else:
        b_spec = pl.BlockSpec((tk, tn), lambda i, j, k: (k, j))
    return _mm(a, b, grid=(M // tm, N // tn, K // tk), a_spec=a_spec, b_spec=b_spec,
               o_spec=pl.BlockSpec((tm, tn), lambda i, j, k: (i, j)),
               out_shape=_sds((M, N), out_dtype), acc_shape=(tm, tn), dims=dims, name=name)


def _mm_proj(h_all, win_q, q, nparts, fill, name):
    n, D = h_all.shape
    nbp = win_q.shape[2]
    nb = nbp * nparts
    tm, tn, tk = _tile(n, TM, SUB16), _tile(nbp, TN, LANE), D
    nbn = nbp // tn
    return _mm(h_all, win_q, grid=(n // tm, NDEV * nbn, D // tk),
               a_spec=pl.BlockSpec((tm, tk), lambda i, j, k: (i, k)),
               b_spec=pl.BlockSpec((None, tk, tn), lambda i, j, k: (j // nbn, k, j % nbn)),
               o_spec=pl.BlockSpec((tm, tn), lambda i, j, k: (i, (j // nbn) * (nb // tn) + q * nbn + j % nbn)),
               out_shape=_sds((n, NDEV * nb), F32), acc_shape=(tm, tn), dims=NN, name=name, fill=fill)


def _mm_dh(dproj, wins, name, dep):
    nparts = len(wins)
    n = dproj.shape[0]
    _, D, nbp = wins[0].shape
    nb = nbp * nparts
    tm, tn = _tile(n, TM, SUB16), _tile(D, TN, LANE)

    def body(a_ref, *rest):
        b_refs, o_ref, acc_ref = rest[:nparts], rest[-2], rest[-1]
        k = pl.program_id(2)

        def prod():
            out = None
            for q in range(nparts):
                d = lax.dot_general(a_ref[:, q * nbp:(q + 1) * nbp], b_refs[q][...], NT, preferred_element_type=F32)
                out = d if out is None else out + d
            return out

        @pl.when(k == 0)
        def _():
            acc_ref[...] = prod()

        @pl.when((k > 0) & (k < NDEV - 1))
        def _():
            acc_ref[...] += prod()

        @pl.when(k == NDEV - 1)
        def _():
            o_ref[...] = acc_ref[...] + prod()

    blocks = [((tm, nb), dproj.dtype)] + [((tn, nbp), wins[0].dtype)] * nparts + [((tm, tn), F32)]
    return pl.pallas_call(
        body, name=name, grid=(n // tm, D // tn, NDEV),
        in_specs=[pl.BlockSpec((tm, nb), lambda i, j, k: (i, k))]
        + [pl.BlockSpec((None, tn, nbp), lambda i, j, k: (k, j, 0))] * nparts + [ANY],
        out_specs=pl.BlockSpec((tm, tn), lambda i, j, k: (i, j)), out_shape=_sds((n, D), F32),
        scratch_shapes=[pltpu.VMEM((tm, tn), F32)],
        compiler_params=_params(blocks, [((tm, tn), F32)], ("parallel", "parallel", "arbitrary")),
    )(dproj, *wins, dep)


def _transpose(x, name):
    R, C = x.shape
    tr, tc = _tile(R, TL, LANE), _tile(C, 2 * TL, LANE)

    def body(x_ref, o_ref):
        o_ref[...] = x_ref[...].T

    return pl.pallas_call(
        body, name=name, grid=(R // tr, C // tc),
        in_specs=[pl.BlockSpec((tr, tc), lambda i, j: (i, j))],
        out_specs=pl.BlockSpec((tc, tr), lambda i, j: (j, i)),
        out_shape=_sds((C, R), x.dtype),
        compiler_params=_params([((tr, tc), x.dtype)] * 2, dims=("parallel", "parallel")),
    )(x)


def _mm_gwin(h_t, dproj, nb, part, nparts, name, dep=None):
    D, n = h_t.shape
    nbp = nb // nparts
    tm, tn, tk = _tile(D, TM, LANE), _tile(nbp, TN, LANE), n
    nbn = nbp // tn
    return _mm(h_t, dproj, grid=(D // tm, NDEV * nbn, n // tk),
               a_spec=pl.BlockSpec((tm, tk), lambda i, j, k: (i, k)),
               b_spec=pl.BlockSpec((tk, tn), lambda i, j, k: (k, (j // nbn) * (nb // tn) + part * nbn + j % nbn)),
               o_spec=pl.BlockSpec((None, tm, tn), lambda i, j, k: (j // nbn, i, j % nbn)),
               out_shape=_sds((NDEV, D, nbp), MXU), acc_shape=(tm, tn), dims=NN, name=name, dep=dep)


def _mm_group(a, b, mode, out_dtype, name):
    if mode == "wgrad":
        L, W = a.shape
        G = len(POOL_WINDOWS)
        pd = W // G
        tm, tn, tk = _tile(pd, TM, LANE), _tile(pd, TN, LANE), _tile(L, TK, SUB16)
        nm, nn = pd // tm, pd // tn
        return _mm(a, b, grid=(G, nm, nn, L // tk),
                   a_spec=pl.BlockSpec((tk, tm), lambda g, i, j, k: (k, g * nm + i)),
                   b_spec=pl.BlockSpec((tk, tn), lambda g, i, j, k: (k, g * nn + j)),
                   o_spec=pl.BlockSpec((None, tm, tn), lambda g, i, j, k: (g, i, j)),
                   out_shape=_sds((G, pd, pd), out_dtype), acc_shape=(tm, tn), dims=TN_DIMS, name=name)
    L, W = a.shape
    G, pd, _ = b.shape
    tm, tn, tk = _tile(L, TM, SUB16), _tile(pd, TN, LANE), _tile(pd, TK, LANE)
    nn, nk = pd // tn, pd // tk
    if mode == "fwd":
        b_spec = pl.BlockSpec((None, tk, tn), lambda g, i, j, k: (g, k, j))
        dims = NN
    else:
        b_spec = pl.BlockSpec((None, tn, tk), lambda g, i, j, k: (g, j, k))
        dims = NT
    return _mm(a, b, grid=(G, L // tm, nn, nk),
               a_spec=pl.BlockSpec((tm, tk), lambda g, i, j, k: (i, g * nk + k)),
               b_spec=b_spec,
               o_spec=pl.BlockSpec((tm, tn), lambda g, i, j, k: (i, g * nn + j)),
               out_shape=_sds((L, W), out_dtype), acc_shape=(tm, tn), dims=dims, name=name)


def _ada_fwd(cc, w_loc, b_loc, name):
    R, D = cc.shape
    na = w_loc.shape[1]
    tk = _tile(D, 512, LANE)

    def body(c_ref, w_ref, b_ref, mod_ref, s_ref):
        k = pl.program_id(0)
        cv = c_ref[...]
        s = cv * _sigmoid_small(cv)
        s_ref[...] = s

        @pl.when(k == 0)
        def _():
            mod_ref[...] = jnp.broadcast_to(b_ref[...], mod_ref.shape)

        mod_ref[...] += lax.dot_general(s.astype(MXU), w_ref[...].astype(MXU), NN, preferred_element_type=F32)

    blocks = [((R, tk), F32), ((tk, na), F32), ((1, na), F32), ((R, na), F32), ((R, tk), F32)]
    return pl.pallas_call(
        body, name=name, grid=(D // tk,),
        in_specs=[pl.BlockSpec((R, tk), lambda k: (0, k)), pl.BlockSpec((tk, na), lambda k: (k, 0)),
                  pl.BlockSpec((1, na), lambda k: (0, 0))],
        out_specs=[pl.BlockSpec((R, na), lambda k: (0, 0)), pl.BlockSpec((R, tk), lambda k: (0, k))],
        out_shape=[_sds((R, na), F32), _sds((R, D), F32)],
        compiler_params=_params(blocks, dims=("arbitrary",)),
    )(cc, w_loc, b_loc)


def _adam(w, g, m, v):
    m = ADAM_B1 * m + (1.0 - ADAM_B1) * g
    v = ADAM_B2 * v + (1.0 - ADAM_B2) * (g * g)
    m_hat = m / (1.0 - ADAM_B1 ** ADAM_STEP)
    v_hat = v / (1.0 - ADAM_B2 ** ADAM_STEP)
    delta = -ADAM_LR * (m_hat / (jnp.sqrt(v_hat) + ADAM_EPS) + ADAM_WD * w)
    return delta, m, v


def _ada_bwd(s_all, ga, gc, w_loc, m_loc, v_loc, name):
    D, na = w_loc.shape
    tr = _tile(D, 256, LANE)

    def body(s_ref, ga_ref, gc_ref, w_ref, m_ref, v_ref, g_ref, d_ref, nm_ref, nv_ref, pc_ref):
        dmc = gc_ref[0:1, :]
        for p in range(1, NDEV):
            dmc = dmc + gc_ref[p:p + 1, :]
        rows = lax.broadcasted_iota(jnp.int32, (NDEV, na), 0)
        dmc8 = jnp.where(rows == 0, jnp.broadcast_to(dmc, (NDEV, na)), 0.0)
        dm = jnp.concatenate([ga_ref[...], dmc8], axis=0).astype(MXU)
        dmc16 = jnp.concatenate([dmc8, jnp.zeros_like(dmc8)], axis=0).astype(MXU)
        w = w_ref[...]
        g = lax.dot_general(s_ref[...].astype(MXU), dm, TN_DIMS, preferred_element_type=F32)
        pc_ref[...] = lax.dot_general(dmc16, w.astype(MXU), NT, preferred_element_type=F32)
        delta, nm, nv = _adam(w, g, m_ref[...], v_ref[...])
        g_ref[...] = g
        d_ref[...] = delta
        nm_ref[...] = nm
        nv_ref[...] = nv

    big = pl.BlockSpec((tr, na), lambda i: (i, 0))
    full = pl.BlockSpec((NDEV, na), lambda i: (0, 0))
    srow = pl.BlockSpec((2 * NDEV, tr), lambda i: (0, i))
    blocks = [((2 * NDEV, tr), F32)] * 2 + [((NDEV, na), F32)] * 2 + [((tr, na), F32)] * 7
    return pl.pallas_call(
        body, name=name, grid=(D // tr,),
        in_specs=[srow, full, full, big, big, big],
        out_specs=[big, big, big, big, srow],
        out_shape=[_sds((D, na), F32)] * 4 + [_sds((2 * NDEV, D), F32)],
        compiler_params=_params(blocks, dims=("parallel",)),
    )(s_all, ga, gc, w_loc, m_loc, v_loc)


def _norm_mod(x2, g, shift, scale, n, row0, h_prev, name):
    R, D = x2.shape
    tl = _tile(R, TL, SUB16)
    assert row0 % tl == 0
    b0 = row0 // tl

    def body(x_ref, g_ref, sh_ref, sc_ref, *rest):
        o_ref = rest[-1]
        xv = x_ref[...]
        s = lax.rsqrt(jnp.mean(xv * xv, axis=-1, keepdims=True) + EPS)
        nrm = xv * s * g_ref[...]
        o_ref[...] = (nrm * (1.0 + sc_ref[...]) + sh_ref[...]).astype(o_ref.dtype)

    vec = pl.BlockSpec((1, D), lambda i: (0, 0))
    in_specs = [pl.BlockSpec((tl, D), lambda i: (i, 0)), vec, vec, vec]
    args = [x2, g, shift, scale]
    aliases = {}
    if h_prev is not None:
        in_specs.append(ANY)
        args.append(h_prev)
        aliases = {4: 0}
    blocks = [((tl, D), F32), ((tl, D), MXU)] + [((1, D), F32)] * 3
    return pl.pallas_call(
        body, name=name, grid=(R // tl,), in_specs=in_specs,
        out_specs=pl.BlockSpec((tl, D), lambda i: (i + b0, 0)),
        out_shape=_sds((n, D), MXU), input_output_aliases=aliases,
        compiler_params=_params(blocks, dims=("parallel",)),
    )(*args)


def _norm_bwd(x2, dh_all, row0, g, scale, dxn, ggn0, name):
    R, D = x2.shape
    tl = _tile(R, TL_FINAL, SUB)
    assert row0 % tl == 0
    b0 = row0 // tl
    with_x = dxn is not None

    def body(*refs):
        if with_x:
            x_ref, dh_ref, g_ref, sc_ref, gg0_ref, dxn_ref, gx_ref, dsh_ref, dsc_ref, gg_ref = refs
        else:
            x_ref, dh_ref, g_ref, sc_ref, gg0_ref, dsh_ref, dsc_ref, gg_ref = refs
        i = pl.program_id(0)

        @pl.when(i == 0)
        def _():
            dsh_ref[...] = jnp.zeros_like(dsh_ref)
            dsc_ref[...] = jnp.zeros_like(dsc_ref)
            gg_ref[...] = gg0_ref[...]

        xv = x_ref[...]
        dh = dh_ref[...]
        gv = g_ref[...]
        s = lax.rsqrt(jnp.mean(xv * xv, axis=-1, keepdims=True) + EPS)
        xh = xv * s
        dsh_ref[...] += jnp.sum(dh, axis=0, keepdims=True)
        dsc_ref[...] += jnp.sum(dh * (xh * gv), axis=0, keepdims=True)
        dn = dh * (1.0 + sc_ref[...])
        gg_ref[...] += jnp.sum(dn * xh, axis=0, keepdims=True)
        if with_x:
            dxh = dn * gv
            dx = s * (dxh - xh * jnp.mean(dxh * xh, axis=-1, keepdims=True))
            gx_ref[...] = dx + dxn_ref[...]

    vec = pl.BlockSpec((1, D), lambda i: (0, 0))
    row = pl.BlockSpec((tl, D), lambda i: (i, 0))
    in_specs = [row, pl.BlockSpec((tl, D), lambda i: (i + b0, 0)), vec, vec, vec]
    args = [x2, dh_all, g, scale, ggn0]
    out_specs = [vec, vec, vec]
    out_shape = [_sds((1, D), F32)] * 3
    if with_x:
        in_specs.append(row)
        args.append(dxn)
        out_specs = [row] + out_specs
        out_shape = [_sds((R, D), F32)] + out_shape
    blocks = [((tl, D), F32)] * (4 if with_x else 2) + [((1, D), F32)] * 6
    outs = pl.pallas_call(
        body, name=name, grid=(R // tl,), in_specs=in_specs, out_specs=out_specs, out_shape=out_shape,
        compiler_params=_params(blocks, dims=("arbitrary",)),
    )(*args)
    return tuple(outs) if with_x else (None,) + tuple(outs)


def _tap_valid(t, o, lc, n):
    tt = t + o
    in_ctx = t < lc
    return (tt >= jnp.where(in_ctx, 0, lc)) & (tt < jnp.where(in_ctx, lc, n))


def _conv_fwd(proj_all, cw, cb, lc, W, name):
    n = proj_all.shape[0]
    cbk = _tile(W, CB_SEQ, LANE)
    tr = _tile(n, TR_CONV, SUB16)
    ext = tr + 2 * SUB

    def body(x_ref, w_ref, b_ref, u_ref, xp_ref):
        xp_ref[0:SUB, :] = jnp.zeros((SUB, cbk), F32)
        xp_ref[n + SUB:n + 2 * SUB, :] = jnp.zeros((SUB, cbk), F32)
        xp_ref[SUB:n + SUB, :] = x_ref[...]
        w = w_ref[...]
        bias = b_ref[...]

        def chunk(ci, carry):
            r0 = pl.multiple_of(ci * tr, SUB16)
            xe = xp_ref[pl.ds(r0, ext), :]
            t = r0 + lax.broadcasted_iota(jnp.int32, (tr, cbk), 0)
            acc = jnp.broadcast_to(bias, (tr, cbk))
            for k in range(4):
                o = k - 1
                sh = xe if o == 0 else pltpu.roll(xe, (-o) % ext, 0)
                acc = acc + jnp.where(_tap_valid(t, o, lc, n), sh[SUB:tr + SUB], 0.0) * w[k:k + 1]
            u_ref[pl.ds(r0, tr), :] = acc
            return carry

        lax.fori_loop(0, n // tr, chunk, 0)

    blocks = [((n, cbk), F32)] * 2 + [((4, cbk), F32), ((1, cbk), F32)]
    scratch = [((n + 2 * SUB, cbk), F32)]
    return pl.pallas_call(
        body, name=name, grid=(W // cbk,),
        in_specs=[pl.BlockSpec((n, cbk), lambda j: (0, j)), pl.BlockSpec((4, cbk), lambda j: (0, j)),
                  pl.BlockSpec((1, cbk), lambda j: (0, j))],
        out_specs=pl.BlockSpec((n, cbk), lambda j: (0, j)),
        out_shape=_sds((n, W), F32),
        scratch_shapes=[pltpu.VMEM(s, d) for s, d in scratch],
        compiler_params=_params(blocks, scratch, ("parallel",)),
    )(proj_all, cw, cb)


def _conv_bwd(du_all, proj_all, cw, dproj, lc, W, name):
    n = du_all.shape[0]
    cbk = _tile(W, CB_SEQ, LANE)
    tr = _tile(n, TR_CONV, SUB16)
    ext = tr + 2 * SUB

    def body(du_ref, x_ref, w_ref, dp_in, dx_ref, gw_ref, gb_ref, dp_ref, xp_ref):
        del dp_in
        for ref, src in ((dp_ref, du_ref), (xp_ref, x_ref)):
            ref[0:SUB, :] = jnp.zeros((SUB, cbk), F32)
            ref[n + SUB:n + 2 * SUB, :] = jnp.zeros((SUB, cbk), F32)
            ref[SUB:n + SUB, :] = src[...]
        w = w_ref[...]

        def fold(v):
            return jnp.sum(v.reshape(tr // SUB, SUB, cbk), axis=0)

        def chunk(ci, carry):
            r0 = pl.multiple_of(ci * tr, SUB16)
            de = dp_ref[pl.ds(r0, ext), :]
            xe = xp_ref[pl.ds(r0, ext), :]
            t = r0 + lax.broadcasted_iota(jnp.int32, (tr, cbk), 0)
            d0 = de[SUB:tr + SUB]
            dx = jnp.zeros((tr, cbk), F32)
            new = []
            for k in range(4):
                o = k - 1
                dsh = de if o == 0 else pltpu.roll(de, o % ext, 0)
                dx = dx + jnp.where(_tap_valid(t, -o, lc, n), dsh[SUB:tr + SUB], 0.0) * w[k:k + 1]
                xsh = xe if o == 0 else pltpu.roll(xe, (-o) % ext, 0)
                new.append(carry[k] + fold(d0 * jnp.where(_tap_valid(t, o, lc, n), xsh[SUB:tr + SUB], 0.0)))
            new.append(carry[4] + fold(d0))
            dx_ref[pl.ds(r0, tr), :] = dx.astype(dx_ref.dtype)
            return tuple(new)

        zero = jnp.zeros((SUB, cbk), F32)
        acc = lax.fori_loop(0, n // tr, chunk, (zero,) * 5)
        for k in range(4):
            gw_ref[k:k + 1, :] = jnp.sum(acc[k], axis=0, keepdims=True)
        gb_ref[...] = jnp.sum(acc[4], axis=0, keepdims=True)

    col = pl.BlockSpec((n, cbk), lambda j: (0, j))
    blocks = [((n, cbk), F32)] * 2 + [((n, cbk), MXU), ((4, cbk), F32), ((4, cbk), F32), ((1, cbk), F32)]
    scratch = [((n + 2 * SUB, cbk), F32)] * 2
    return pl.pallas_call(
        body, name=name, grid=(W // cbk,),
        in_specs=[col, col, pl.BlockSpec((4, cbk), lambda j: (0, j)), ANY],
        out_specs=[col, pl.BlockSpec((4, cbk), lambda j: (0, j)), pl.BlockSpec((1, cbk), lambda j: (0, j))],
        out_shape=[_sds(dproj.shape, dproj.dtype), _sds((4, W), F32), _sds((1, W), F32)],
        input_output_aliases={3: 0},
        scratch_shapes=[pltpu.VMEM(s, d) for s, d in scratch],
        compiler_params=_params(blocks, scratch, ("parallel",)),
    )(du_all, proj_all, cw, dproj)


def _gate_coeffs(ub, u, d, wr_ref, wi_ref, br_ref, bi_ref, lam_ref):
    c = -LRU_C * _softplus(-lam_ref[d:d + 1, :])
    r = _sigmoid(lax.dot_general(ub, wr_ref[d], NN, preferred_element_type=F32) + br_ref[d:d + 1, :])
    ig = _sigmoid(lax.dot_general(ub, wi_ref[d], NN, preferred_element_type=F32) + bi_ref[d:d + 1, :])
    la = c * r
    a = jnp.exp(la)
    sq = jnp.sqrt(-jnp.tanh(la) * (1.0 + a * a))
    return c, r, ig, a, sq


def _gate_specs(tl, hd):
    w_spec = pl.BlockSpec((2, None, hd, hd), lambda h, i: (0, h, 0, 0))
    v_spec = pl.BlockSpec((2, hd), lambda h, i: (0, h))
    return w_spec, v_spec


def _gates_fwd(u_all, wr, wi, br, bi, lam, name):
    n, W = u_all.shape
    heads, hd = wr.shape[1], wr.shape[2]
    tl = _tile(n, TL_GATES, SUB16)

    def body(u_ref, wr_ref, wi_ref, br_ref, bi_ref, lam_ref, a_ref, b_ref):
        u = u_ref[...]
        ub = u.astype(MXU)
        for d in range(2):
            _, _, ig, a, sq = _gate_coeffs(ub, u, d, wr_ref, wi_ref, br_ref, bi_ref, lam_ref)
            a_ref[d] = a
            b_ref[d] = sq * (ig * u)

    w_spec, v_spec = _gate_specs(tl, hd)
    o_spec = pl.BlockSpec((2, tl, hd), lambda h, i: (0, i, h))
    blocks = [((tl, hd), F32), ((2, hd, hd), MXU), ((2, hd, hd), MXU)] + [((2, hd), F32)] * 3 + [((2, tl, hd), F32)] * 2
    return pl.pallas_call(
        body, name=name, grid=(heads, n // tl),
        in_specs=[pl.BlockSpec((tl, hd), lambda h, i: (i, h)), w_spec, w_spec, v_spec, v_spec, v_spec],
        out_specs=[o_spec, o_spec], out_shape=[_sds((2, n, W), F32)] * 2,
        compiler_params=_params(blocks, dims=("parallel", "parallel")),
    )(u_all, wr, wi, br, bi, lam)


def _gates_bwd(u_all, da, db, wr, wi, br, bi, lam, name):
    n, W = u_all.shape
    heads, hd = wr.shape[1], wr.shape[2]
    tl = _tile(n, TL_GATES, SUB16)
    ni = n // tl

    def body(u_ref, da_ref, db_ref, wr_ref, wi_ref, br_ref, bi_ref, lam_ref,
             du_ref, gwr_ref, gwi_ref, gbr_ref, gbi_ref, gc_ref, accr_ref, acci_ref):
        i = pl.program_id(1)

        @pl.when(i == 0)
        def _():
            accr_ref[...] = jnp.zeros_like(accr_ref)
            acci_ref[...] = jnp.zeros_like(acci_ref)
            gbr_ref[...] = jnp.zeros_like(gbr_ref)
            gbi_ref[...] = jnp.zeros_like(gbi_ref)
            gc_ref[...] = jnp.zeros_like(gc_ref)

        u = u_ref[...]
        ub = u.astype(MXU)
        du = jnp.zeros_like(u)
        for d in range(2):
            c, r, ig, a, sq = _gate_coeffs(ub, u, d, wr_ref, wi_ref, br_ref, bi_ref, lam_ref)
            dbv = db_ref[d]
            t = dbv * sq
            du = du + t * ig
            d_la = da_ref[d] * a - (dbv * ig * u) * (a * a) / sq
            gc_ref[d:d + 1, :] += jnp.sum(d_la * r, axis=0, keepdims=True)
            d_pr = (d_la * c) * (r * (1.0 - r))
            d_pi = (t * u) * (ig * (1.0 - ig))
            gbr_ref[d:d + 1, :] += jnp.sum(d_pr, axis=0, keepdims=True)
            gbi_ref[d:d + 1, :] += jnp.sum(d_pi, axis=0, keepdims=True)
            pb = d_pr.astype(MXU)
            qb = d_pi.astype(MXU)
            du = du + lax.dot_general(pb, wr_ref[d], NT, preferred_element_type=F32)
            du = du + lax.dot_general(qb, wi_ref[d], NT, preferred_element_type=F32)
            accr_ref[d] += lax.dot_general(ub, pb, TN_DIMS, preferred_element_type=F32)
            acci_ref[d] += lax.dot_general(ub, qb, TN_DIMS, preferred_element_type=F32)
        du_ref[...] = du

        @pl.when(i == ni - 1)
        def _():
            gwr_ref[...] = accr_ref[...].astype(gwr_ref.dtype)
            gwi_ref[...] = acci_ref[...].astype(gwi_ref.dtype)

    w_spec, v_spec = _gate_specs(tl, hd)
    u_spec = pl.BlockSpec((tl, hd), lambda h, i: (i, h))
    ab_spec = pl.BlockSpec((2, tl, hd), lambda h, i: (0, i, h))
    blocks = ([((tl, hd), F32)] * 2 + [((2, tl, hd), F32)] * 2 + [((2, hd, hd), MXU)] * 4 + [((2, hd), F32)] * 6)
    scratch = [((2, hd, hd), F32)] * 2
    return pl.pallas_call(
        body, name=name, grid=(heads, ni),
        in_specs=[u_spec, ab_spec, ab_spec, w_spec, w_spec, v_spec, v_spec, v_spec],
        out_specs=[u_spec, w_spec, w_spec, v_spec, v_spec, v_spec],
        out_shape=[_sds((n, W), F32), _sds(wr.shape, MXU), _sds(wi.shape, MXU)] + [_sds((2, W), F32)] * 3,
        scratch_shapes=[pltpu.VMEM(s, d) for s, d in scratch],
        compiler_params=_params(blocks, scratch, ("parallel", "arbitrary")),
    )(u_all, da, db, wr, wi, br, bi, lam)


def _tile_scan(A, B, rows, reverse):
    for s in (1, 2, 4):
        if reverse:
            As, Bs, m = pltpu.roll(A, SUB - s, 0), pltpu.roll(B, SUB - s, 0), rows < SUB - s
        else:
            As, Bs, m = pltpu.roll(A, s, 0), pltpu.roll(B, s, 0), rows >= s
        B = jnp.where(m, A * Bs + B, B)
        A = jnp.where(m, A * As, A)
    return A, B


def _scan_chunks(n, lc):
    tc = _tile(lc, TL, SUB)
    assert n % tc == 0 and lc % tc == 0
    return tc, n // tc, lc // tc


def _scan_fwd(a_all, b_all, lc, name):
    _, n, W = a_all.shape
    cb = _tile(W, CB_SCAN, LANE)
    tc, nch, ncc = _scan_chunks(n, lc)
    ntile = tc // SUB

    def chunk(d, t):
        return jnp.where(d == 0, t, jnp.where(t < ncc, ncc - 1 - t, nch - 1 - (t - ncc)))

    def body(a_ref, b_ref, h_ref, carry_ref):
        rows = lax.broadcasted_iota(jnp.int32, (SUB, cb), 0)

        @pl.when(pl.program_id(2) == 0)
        def _():
            carry_ref[...] = jnp.zeros_like(carry_ref)

        def run(reverse):
            def step(i, h):
                r = pl.multiple_of(((ntile - 1 - i) if reverse else i) * SUB, SUB)
                A, B = _tile_scan(a_ref[pl.ds(r, SUB), :], b_ref[pl.ds(r, SUB), :], rows, reverse)
                H = A * h + B
                h_ref[pl.ds(r, SUB), :] = H
                return H[0:1, :] if reverse else H[SUB - 1:SUB, :]

            carry_ref[...] = lax.fori_loop(0, ntile, step, carry_ref[...], unroll=2)

        @pl.when(pl.program_id(1) == 0)
        def _():
            run(False)

        @pl.when(pl.program_id(1) == 1)
        def _():
            run(True)

    spec = pl.BlockSpec((None, tc, cb), lambda j, d, t: (d, chunk(d, t), j))
    return pl.pallas_call(
        body, name=name, grid=(W // cb, 2, nch), in_specs=[spec, spec], out_specs=spec,
        out_shape=_sds((2, n, W), F32), scratch_shapes=[pltpu.VMEM((1, cb), F32)],
        compiler_params=_params([((tc, cb), F32)] * 3, [((1, cb), F32)], ("parallel", "arbitrary", "arbitrary")),
    )(a_all, b_all)


def _scan_bwd(a_all, h_all, dya, lc, name):
    _, n, W = a_all.shape
    cb = _tile(W, CB_SCAN, LANE)
    tc, nch, ncc = _scan_chunks(n, lc)
    ntile = tc // SUB
    nl = nch - ncc

    def chunk(d, t):
        return jnp.where(d == 0, nch - 1 - t, jnp.where(t < nl, ncc + t, t - nl))

    def neighbour(d, t):
        c = chunk(d, t)
        below = jnp.maximum(c * ntile - 1, 0)
        above = jnp.where(c == nch - 1, 0, jnp.minimum((c + 1) * ntile, nch * ntile - 1))
        return jnp.where(d == 0, below, above)

    def body(a_ref, h_ref, hn_ref, g_ref, da_ref, db_ref, mu_ref):
        rows = lax.broadcasted_iota(jnp.int32, (SUB, cb), 0)
        d, t = pl.program_id(1), pl.program_id(2)
        c = chunk(d, t)
        has_g = c >= ncc

        @pl.when(t == 0)
        def _():
            mu_ref[...] = jnp.zeros_like(mu_ref)

        def tile(ref, j):
            return ref[pl.ds(pl.multiple_of(j * SUB, SUB), SUB), :]

        def run(up):
            if up:
                edge = jnp.where(c == ncc - 1, 0.0, hn_ref[0:1, :])
            else:
                edge = jnp.where(c > 0, hn_ref[SUB - 1:SUB, :], 0.0)

            def step(i, mu):
                j = i if up else ntile - 1 - i
                a_t = tile(a_ref, j)
                g_t = jnp.where(has_g, tile(g_ref, j), 0.0)
                if up:
                    ap = jnp.where(rows >= 1, pltpu.roll(a_t, 1, 0), 1.0)
                    nb_row = jnp.where(j < ntile - 1, tile(h_ref, jnp.minimum(j + 1, ntile - 1))[0:1, :], edge)
                    hprev = jnp.where(rows < SUB - 1, pltpu.roll(tile(h_ref, j), SUB - 1, 0), nb_row)
                else:
                    ap = jnp.where(rows < SUB - 1, pltpu.roll(a_t, SUB - 1, 0), 1.0)
                    nb_row = jnp.where(j > 0, tile(h_ref, jnp.maximum(j - 1, 0))[SUB - 1:SUB, :], edge)
                    hprev = jnp.where(rows >= 1, pltpu.roll(tile(h_ref, j), 1, 0), nb_row)
                A, B = _tile_scan(ap, g_t, rows, not up)
                lam = A * mu + B
                r = pl.multiple_of(j * SUB, SUB)
                da_ref[pl.ds(r, SUB), :] = lam * hprev
                db_ref[pl.ds(r, SUB), :] = lam
                return a_t[SUB - 1:SUB, :] * lam[SUB - 1:SUB, :] if up else a_t[0:1, :] * lam[0:1, :]

            mu_ref[...] = lax.fori_loop(0, ntile, step, mu_ref[...], unroll=2)

        @pl.when(d == 0)
        def _():
            run(False)

        @pl.when(d == 1)
        def _():
            run(True)

    spec = pl.BlockSpec((None, tc, cb), lambda j, d, t: (d, chunk(d, t), j))
    n_spec = pl.BlockSpec((None, SUB, cb), lambda j, d, t: (d, neighbour(d, t), j))
    g_spec = pl.BlockSpec((tc, cb), lambda j, d, t: (jnp.maximum(chunk(d, t) - ncc, 0), j))
    blocks = [((tc, cb), F32)] * 5 + [((SUB, cb), F32)]
    return pl.pallas_call(
        body, name=name, grid=(W // cb, 2, nch), in_specs=[spec, spec, n_spec, g_spec], out_specs=[spec, spec],
        out_shape=[_sds((2, n, W), F32)] * 2, scratch_shapes=[pltpu.VMEM((1, cb), F32)],
        compiler_params=_params(blocks, [((1, cb), F32)], ("parallel", "arbitrary", "arbitrary")),
    )(a_all, h_all, h_all, dya)


def _pool_window(v, w, tl, cb, transpose):
    half = w // 2
    pos = lax.broadcasted_iota(jnp.int32, (tl, cb), 0) % GRID_W
    cnt = (jnp.minimum(pos + half - 1, GRID_W - 1) - jnp.maximum(pos - half, 0) + 1).astype(F32)
    src = v / cnt if transpose else v

    def run_sum(s, step):
        span = 1
        while span < half:
            ok = (pos + span < GRID_W) if step > 0 else (pos - span >= 0)
            s = s + jnp.where(ok, pltpu.roll(s, (-step * span) % tl, 0), 0.0)
            span *= 2
        return s

    ahead, behind = run_sum(src, 1), run_sum(src, -1)
    if transpose:
        return behind + jnp.where(pos + 1 < GRID_W, pltpu.roll(ahead, tl - 1, 0), 0.0) - v
    return (ahead + jnp.where(pos >= 1, pltpu.roll(behind, 1, 0), 0.0)) / cnt - v


def _pool_z(src, row0, col0, L, W, transpose, dproj, name):
    G = len(POOL_WINDOWS)
    pd = W // G
    tl = _tile(L, TL, GRID_W)
    cb = _tile(pd, CB_POOL, LANE)
    assert row0 % tl == 0 and col0 % cb == 0
    rb, cbk = row0 // tl, col0 // cb
    nj = pd // cb

    def body(x_ref, *rest):
        o_ref = rest[-1]
        for gi, w in enumerate(POOL_WINDOWS):
            @pl.when(pl.program_id(0) == gi)
            def _(w=w):
                o_ref[...] = _pool_window(x_ref[...], w, tl, cb, transpose).astype(o_ref.dtype)

    plain = pl.BlockSpec((tl, cb), lambda g, i, j: (i, g * nj + j))
    window = pl.BlockSpec((tl, cb), lambda g, i, j: (i + rb, cbk + g * nj + j))
    blocks = [((tl, cb), F32), ((tl, cb), MXU)]
    if transpose:
        return pl.pallas_call(
            body, name=name, grid=(G, L // tl, nj), in_specs=[plain, ANY], out_specs=window,
            out_shape=_sds(dproj.shape, dproj.dtype), input_output_aliases={1: 0},
            compiler_params=_params(blocks, dims=("parallel",) * 3),
        )(src, dproj)
    return pl.pallas_call(
        body, name=name, grid=(G, L // tl, nj), in_specs=[window], out_specs=plain,
        out_shape=_sds((L, W), MXU),
        compiler_params=_params(blocks, dims=("parallel",) * 3),
    )(src)


def _mix_fwd(hs, proj_all, ypre, b_pool, pool_scale, lc, name):
    L, W = ypre.shape
    tl = _tile(L, TL, SUB16)
    cb = _tile(W, CB_MIX, LANE)
    nj = W // cb
    assert lc % tl == 0
    rb = lc // tl

    def body(hs_ref, ga_ref, yp_ref, gb_ref, bp_ref, ps_ref, o_ref):
        p = pl.program_id(2)

        @pl.when(p == 0)
        def _():
            g = ga_ref[...]
            o_ref[...] = ((hs_ref[0] + hs_ref[1]) * (g * _sigmoid(g))).astype(o_ref.dtype)

        @pl.when(p == 1)
        def _():
            g = gb_ref[...]
            yb = (yp_ref[...] + bp_ref[...]) * ps_ref[...]
            o_ref[...] = (yb * (g * _sigmoid(g))).astype(o_ref.dtype)

    vec = pl.BlockSpec((1, cb), lambda i, j, p: (0, j))
    blocks = [((2, tl, cb), F32)] + [((tl, cb), F32)] * 3 + [((tl, cb), MXU)]
    return pl.pallas_call(
        body, name=name, grid=(L // tl, nj, 2),
        in_specs=[pl.BlockSpec((2, tl, cb), lambda i, j, p: (0, i + rb, j)),
                  pl.BlockSpec((tl, cb), lambda i, j, p: (i + rb, 2 * nj + j)),
                  pl.BlockSpec((tl, cb), lambda i, j, p: (i, j)),
                  pl.BlockSpec((tl, cb), lambda i, j, p: (i + rb, 3 * nj + j)), vec, vec],
        out_specs=pl.BlockSpec((tl, cb), lambda i, j, p: (i, p * nj + j)),
        out_shape=_sds((L, 2 * W), MXU),
        compiler_params=_params(blocks, dims=("parallel", "parallel", "arbitrary")),
    )(hs, proj_all, ypre, proj_all, b_pool, pool_scale)


def _dsilu(g, sg):
    return sg * (1.0 + g * (1.0 - sg))


def _mixa_bwd(dmixed, hs, proj_all, dproj, lc, W, name):
    L = dmixed.shape[0]
    tl = _tile(L, TL, SUB16)
    cb = _tile(W, CB_MIX, LANE)
    nj = W // cb
    rb = lc // tl

    def body(dm_ref, hs_ref, ga_ref, dp_in, dya_ref, dga_ref):
        del dp_in
        g = ga_ref[...]
        sg = _sigmoid(g)
        dm = dm_ref[...]
        dya_ref[...] = dm * (g * sg)
        dga_ref[...] = (dm * (hs_ref[0] + hs_ref[1]) * _dsilu(g, sg)).astype(dga_ref.dtype)

    blocks = [((tl, cb), F32)] * 3 + [((2, tl, cb), F32), ((tl, cb), MXU)]
    return pl.pallas_call(
        body, name=name, grid=(L // tl, nj),
        in_specs=[pl.BlockSpec((tl, cb), lambda i, j: (i, j)),
                  pl.BlockSpec((2, tl, cb), lambda i, j: (0, i + rb, j)),
                  pl.BlockSpec((tl, cb), lambda i, j: (i + rb, 2 * nj + j)), ANY],
        out_specs=[pl.BlockSpec((tl, cb), lambda i, j: (i, j)),
                   pl.BlockSpec((tl, cb), lambda i, j: (i + rb, 2 * nj + j))],
        out_shape=[_sds((L, W), F32), _sds(dproj.shape, dproj.dtype)],
        input_output_aliases={3: 1},
        compiler_params=_params(blocks, dims=("parallel", "parallel")),
    )(dmixed, hs, proj_all, dproj)


def _mixb_bwd(dmixed, ypre, proj_all, b_pool, pool_scale, dproj, lc, W, name):
    L = dmixed.shape[0]
    tl = _tile(L, TL, SUB16)
    cb = _tile(W, CB_MIX, LANE)
    nj = W // cb
    rb = lc // tl

    def body(dm_ref, yp_ref, gb_ref, bp_ref, ps_ref, dp_in, dyp_ref, dgb_ref, gbp_ref, gps_ref):
        del dp_in
        i = pl.program_id(1)

        @pl.when(i == 0)
        def _():
            gbp_ref[...] = jnp.zeros_like(gbp_ref)
            gps_ref[...] = jnp.zeros_like(gps_ref)

        g = gb_ref[...]
        sg = _sigmoid(g)
        dm = dm_ref[...]
        yp = yp_ref[...] + bp_ref[...]
        ps = ps_ref[...]
        dyb = dm * (g * sg)
        dyp = dyb * ps
        dgb_ref[...] = (dm * (yp * ps) * _dsilu(g, sg)).astype(dgb_ref.dtype)
        dyp_ref[...] = dyp.astype(dyp_ref.dtype)
        gbp_ref[...] += jnp.sum(dyp, axis=0, keepdims=True)
        gps_ref[...] += jnp.sum(dyb * yp, axis=0, keepdims=True)

    vec = pl.BlockSpec((1, cb), lambda j, i: (0, j))
    blocks = [((tl, cb), F32)] * 3 + [((tl, cb), MXU)] * 2 + [((1, cb), F32)] * 4
    return pl.pallas_call(
        body, name=name, grid=(nj, L // tl),
        in_specs=[pl.BlockSpec((tl, cb), lambda j, i: (i, nj + j)),
                  pl.BlockSpec((tl, cb), lambda j, i: (i, j)),
                  pl.BlockSpec((tl, cb), lambda j, i: (i + rb, 3 * nj + j)), vec, vec, ANY],
        out_specs=[pl.BlockSpec((tl, cb), lambda j, i: (i, j)),
                   pl.BlockSpec((tl, cb), lambda j, i: (i + rb, 3 * nj + j)), vec, vec],
        out_shape=[_sds((L, W), MXU), _sds(dproj.shape, dproj.dtype), _sds((1, W), F32), _sds((1, W), F32)],
        input_output_aliases={5: 1},
        compiler_params=_params(blocks, dims=("parallel", "arbitrary")),
    )(dmixed, ypre, proj_all, b_pool, pool_scale, dproj)


def _dproj_init(n, lc, W, name):
    cb = _tile(W, CB_MIX, LANE)
    nj = W // cb

    def body(o_ref):
        o_ref[...] = jnp.zeros_like(o_ref)

    return pl.pallas_call(
        body, name=name, grid=(3 * nj,), in_specs=[],
        out_specs=pl.BlockSpec((lc, cb), lambda j: (0, nj + j)),
        out_shape=_sds((n, 4 * W), MXU),
        compiler_params=_params([((lc, cb), MXU)], dims=("parallel",)),
    )()


def _final(x2, out, tgt, gate, gfin, name):
    L, D = x2.shape
    tl = _tile(L, TL_FINAL, SUB16)

    def body(x_ref, o_ref, t_ref, gate_ref, g_ref, dout_ref, dxn_ref, loss_ref, ggf_ref, dgate_ref):
        i = pl.program_id(0)

        @pl.when(i == 0)
        def _():
            loss_ref[...] = jnp.zeros_like(loss_ref)
            ggf_ref[...] = jnp.zeros_like(ggf_ref)
            dgate_ref[...] = jnp.zeros_like(dgate_ref)

        o = o_ref[...]
        gate_v = gate_ref[...]
        gv = g_ref[...]
        xn = x_ref[...] + gate_v * o
        s = lax.rsqrt(jnp.mean(xn * xn, axis=-1, keepdims=True) + EPS)
        xh = xn * s
        err = xh * gv - t_ref[...]
        tok = jnp.mean(err * err, axis=-1, keepdims=True)
        loss_ref[...] += 0.5 * jnp.sum(tok, axis=0, keepdims=True)
        dy = err / D
        ggf_ref[...] += jnp.sum(dy * xh, axis=0, keepdims=True)
        dxh = dy * gv
        dxn = s * (dxh - xh * jnp.mean(dxh * xh, axis=-1, keepdims=True))
        dgate_ref[...] += jnp.sum(dxn * o, axis=0, keepdims=True)
        dout_ref[...] = (gate_v * dxn).astype(dout_ref.dtype)
        dxn_ref[...] = dxn

    row = pl.BlockSpec((tl, D), lambda i: (i, 0))
    vec = pl.BlockSpec((1, D), lambda i: (0, 0))
    blocks = [((tl, D), F32)] * 4 + [((tl, D), MXU)] + [((1, D), F32)] * 4
    return pl.pallas_call(
        body, name=name, grid=(L // tl,), in_specs=[row, row, row, vec, vec],
        out_specs=[row, row, pl.BlockSpec((1, 1), lambda i: (0, 0)), vec, vec],
        out_shape=[_sds((L, D), MXU), _sds((L, D), F32), _sds((1, 1), F32), _sds((1, D), F32), _sds((1, D), F32)],
        compiler_params=_params(blocks, dims=("arbitrary",)),
    )(x2, out, tgt, gate, gfin)


def _adamw_parts(w2, parts, m2, v2, name):
    R, C = w2.shape
    nh = len(parts)
    ch = C // nh
    tr = _tile(R, max(SUB16, (512 * 1024) // (ch * (nh + 1))), SUB16)

    def body(w_ref, *rest):
        p_refs = rest[:nh]
        m_ref, v_ref, g_ref, d_ref, nm_ref, nv_ref = rest[nh:]
        for q in range(nh):
            @pl.when(pl.program_id(1) == q)
            def _(p_ref=p_refs[q]):
                g = p_ref[0].astype(F32)
                for p in range(1, NDEV):
                    g = g + p_ref[p].astype(F32)
                delta, nm, nv = _adam(w_ref[...], g, m_ref[...], v_ref[...])
                g_ref[...] = g
                d_ref[...] = delta
                nm_ref[...] = nm
                nv_ref[...] = nv

    blk = pl.BlockSpec((tr, ch), lambda i, h: (i, h))
    p_spec = pl.BlockSpec((NDEV, tr, ch), lambda i, h: (0, i, 0))
    blocks = [((tr, ch), F32)] * 7 + [((NDEV, tr, ch), parts[0].dtype)] * nh
    return pl.pallas_call(
        body, name=name, grid=(R // tr, nh),
        in_specs=[blk] + [p_spec] * nh + [blk, blk],
        out_specs=[blk] * 4, out_shape=[_sds((R, C), F32)] * 4,
        compiler_params=_params(blocks, dims=("parallel", "arbitrary")),
    )(w2, *parts, m2, v2)


def _small_sum(vs, ga, gc, name):
    ns, nm = vs.shape[1], ga.shape[1]

    def body(v_ref, ga_ref, gc_ref, tot_ref, gb_ref):
        tot = v_ref[0:1, :]
        gb = ga_ref[0:1, :]
        for p in range(1, NDEV):
            tot = tot + v_ref[p:p + 1, :]
            gb = gb + ga_ref[p:p + 1, :]
        for p in range(NDEV):
            gb = gb + gc_ref[p:p + 1, :]
        tot_ref[...] = tot
        gb_ref[...] = gb

    blocks = [((NDEV, ns), F32), ((NDEV, nm), F32), ((NDEV, nm), F32), ((1, ns), F32), ((1, nm), F32)]
    return pl.pallas_call(
        body, name=name, out_shape=[_sds((1, ns), F32), _sds((1, nm), F32)],
        compiler_params=_params(blocks),
    )(vs, ga, gc)


def _adamw_small(g_raw, w, m, v, lam_range, cctx_range, name):
    npk = w.shape[1]

    def body(g_ref, w_ref, m_ref, v_ref, go_ref, d_ref, nm_ref, nv_ref):
        wv = w_ref[...]
        g = g_ref[...]
        idx = lax.broadcasted_iota(jnp.int32, (1, npk), 1)
        in_lam = (idx >= lam_range[0]) & (idx < lam_range[1])
        in_cc = (idx >= cctx_range[0]) & (idx < cctx_range[1])
        sg = _sigmoid_small(wv)
        g = jnp.where(in_lam, g * (LRU_C * _sigmoid_small(-wv)), jnp.where(in_cc, g * _dsilu(wv, sg), g))
        delta, nm, nv = _adam(wv, g, m_ref[...], v_ref[...])
        go_ref[...] = g
        d_ref[...] = delta
        nm_ref[...] = nm
        nv_ref[...] = nv

    return pl.pallas_call(
        body, name=name, out_shape=[_sds((1, npk), F32)] * 4,
        compiler_params=_params([((1, npk), F32)] * 8),
    )(g_raw, w, m, v)


def _pack(pieces):
    return jnp.concatenate([p.reshape(1, -1) for p in pieces], axis=1)


def kernel(x, c, ctx, c_ctx, w_ada, b_ada, g_norm, w_in, conv_w, conv_b, lru_lambda, w_rgate, b_rgate, w_igate, b_igate, w_pool, b_pool, pool_scale, w_out, g_final, loss_target, m_c_ctx, m_w_ada, m_b_ada, m_g_norm, m_w_in, m_conv_w, m_conv_b, m_lru_lambda, m_w_rgate, m_b_rgate, m_w_igate, m_b_igate, m_w_pool, m_b_pool, m_pool_scale, m_w_out, m_g_final, v_c_ctx, v_w_ada, v_b_ada, v_g_norm, v_w_in, v_conv_w, v_conv_b, v_lru_lambda, v_w_rgate, v_b_rgate, v_w_igate, v_b_igate, v_w_pool, v_b_pool, v_pool_scale, v_w_out, v_g_final):
    L, D = x.shape[1], x.shape[2]
    lc = ctx.shape[1]
    n = lc + L
    W = conv_b.shape[1]
    heads, hd = w_rgate.shape[2], w_rgate.shape[4]
    G, pd = w_pool.shape[1], w_pool.shape[3]
    na = w_ada.shape[2]
    nb = w_in.shape[2]
    ws = W // NDEV
    me = 4 * lax.axis_index("x") + 2 * lax.axis_index("y") + lax.axis_index("c")

    nbp = nb // WIN_PARTS
    w_in_parts = [w_in[0, :, q * nbp:(q + 1) * nbp].astype(MXU) for q in range(WIN_PARTS)]
    (win_0, cw_all, lam_all, br_all, bi_all, c_all) = _all_gather(
        [w_in_parts[0], conv_w[0], lru_lambda[0], b_rgate[0], b_igate[0], c], "gather_w_in")
    win = [win_0]
    cw = cw_all.transpose(1, 0, 2).reshape(4, W)
    lam = lam_all.transpose(1, 0, 2).reshape(2, W)
    br = br_all.transpose(1, 0, 2).reshape(2, W)
    bi = bi_all.transpose(1, 0, 2).reshape(2, W)

    cc = jnp.concatenate([c_all.reshape(NDEV, D), c_ctx.reshape(1, D), jnp.zeros((NDEV - 1, D), F32)], axis=0)
    b_loc = lax.dynamic_slice(b_ada, (0, me * na), (1, na))
    mod_loc, s_all = _ada_fwd(cc, w_ada[0], b_loc, "ada_fwd")
    (mod_all,) = _all_gather([mod_loc], "gather_mod")
    gate_w = [w_rgate[0].astype(MXU), w_igate[0].astype(MXU)]
    pool_w, out_w = [w_pool[0].astype(MXU)], [w_out[0].astype(MXU)]
    tok = mod_all
    sent_win = []
    for q in range(1, WIN_PARTS):
        part = [w_in_parts[q]]
        sent_win.append(_send_start(part, _place(part, False, f"place_w_in_{q}", [tok]), "level1", f"start_w_in_{q}"))
        tok = sent_win[-1][4]
    sent_gw = _send_start(gate_w, _place(gate_w, False, "place_gate_w", [tok]), False, "start_gate_w")
    sent_pw = _send_start(pool_w, _place(pool_w, False, "place_pool_w", [sent_gw[4]]), False, "start_pool_w")
    sent_ow = _send_start(out_w, _place(out_w, False, "place_out_w", [sent_pw[4]]), False, "start_out_w")
    mod = mod_all.transpose(1, 0, 2).reshape(2 * NDEV, NDEV * na)
    mod_me = lax.dynamic_slice(mod, (me, 0), (1, 3 * D))
    shift, scale, gate = mod_me[:, :D], mod_me[:, D:2 * D], mod_me[:, 2 * D:]
    shift = _tie(shift, [sent_gw[4], sent_ow[4]], "tie_weights")
    shift_c, scale_c = mod[NDEV:NDEV + 1, :D], mod[NDEV:NDEV + 1, D:2 * D]

    x2, ctx2, tgt = x[0], ctx[0], loss_target[0]
    gfin = g_final.reshape(1, D)
    h_all = _norm_mod(x2, g_norm, shift, scale, n, lc, None, "norm_lat")
    h_all = _norm_mod(ctx2, g_norm, shift_c, scale_c, n, 0, h_all, "norm_ctx")
    proj_all = _mm_proj(h_all, win[0], 0, WIN_PARTS, None, "mm_proj_0")
    for q in range(1, WIN_PARTS):
        lands = _send_wait(sent_win[q - 1], proj_all, "level1", f"wait_w_in_{q}")
        passed = _send_start([], lands, "level2", f"pass_w_in_{q}")
        win.append(_send_wait(passed, proj_all, "level2", f"wait_pass_w_in_{q}")[0])
        proj_all = _mm_proj(h_all, win[q], q, WIN_PARTS, proj_all, f"mm_proj_{q}")
    u_all = _conv_fwd(proj_all, cw, conv_b, lc, W, "conv_fwd")
    wr_all, wi_all = _send_wait(sent_gw, u_all, False, "wait_gate_w")
    wr = wr_all.transpose(1, 2, 0, 3, 4).reshape(2, heads, hd, hd)
    wi = wi_all.transpose(1, 2, 0, 3, 4).reshape(2, heads, hd, hd)
    a_all, b_all = _gates_fwd(u_all, wr, wi, br, bi, lam, "gates_fwd")
    hs = _scan_fwd(a_all, b_all, lc, "scan_fwd")
    z = _pool_z(proj_all, lc, W, L, W, False, None, "pool_z")
    (wpool_all,) = _send_wait(sent_pw, hs, False, "wait_pool_w")
    wpool = wpool_all.transpose(1, 0, 2, 3).reshape(G, pd, pd)
    ypre = _mm_group(z, wpool, "fwd", F32, "mm_pool")
    mixed = _mix_fwd(hs, proj_all, ypre, b_pool, pool_scale, lc, "mix_fwd")
    (wout_all,) = _send_wait(sent_ow, mixed, False, "wait_out_w")
    wout = wout_all.reshape(2 * W, D)
    out = _mm_plain(mixed, wout, NN, F32, "mm_out")
    d_out, dxn, loss_p, ggf, dgate = _final(x2, out, tgt, gate, gfin, "final")

    dmixed = _mm_plain(d_out, wout, NT, F32, "mm_dmixed")
    gwout = _mm_plain(mixed, d_out, TN_DIMS, MXU, "mm_gwout")
    ex_o = [gwout.reshape(NDEV, 2 * W // NDEV, D)]
    sent_o = _send_start(ex_o, _place(ex_o, True, "place_gwout"), True, "start_gwout")
    dproj = _dproj_init(n, lc, W, "dproj_init")
    dya, dproj = _mixa_bwd(dmixed, hs, proj_all, dproj, lc, W, "mixa_bwd")
    dypre, dproj, gbp, gps = _mixb_bwd(dmixed, ypre, proj_all, _tie(b_pool, [sent_o[4]], "tie_gwout"), pool_scale,
                                       dproj, lc, W, "mixb_bwd")
    dz = _mm_group(dypre, wpool, "bwd", F32, "mm_dz")
    gwpool = _mm_group(z, dypre, "wgrad", MXU, "mm_gwpool")
    dproj = _pool_z(dz, lc, W, L, W, True, dproj, "pool_z_bwd")
    da, db = _scan_bwd(a_all, hs, dya, lc, "scan_bwd")
    du, gwr, gwi, gbr, gbi, gcl = _gates_bwd(u_all, da, db, wr, wi, br, bi, lam, "gates_bwd")
    ex_s = [gwpool.reshape(G, NDEV, pd // NDEV, pd).transpose(1, 0, 2, 3),
            gwr.reshape(2, heads, NDEV, hd // NDEV, hd).transpose(2, 0, 1, 3, 4),
            gwi.reshape(2, heads, NDEV, hd // NDEV, hd).transpose(2, 0, 1, 3, 4)]
    sent_s = _send_start(ex_s, _place(ex_s, True, "place_gsmall"), True, "start_gsmall")
    dproj, gcw, gcb = _conv_bwd(du, proj_all, _tie(cw, [sent_s[4]], "tie_gsmall"), dproj, lc, W, "conv_bwd")
    h_t = _transpose(h_all, "transpose_h")
    sent_i, tok = [], None
    for q in range(GWIN_PARTS):
        part = _mm_gwin(h_t, dproj, nb, q, GWIN_PARTS, f"mm_gwin_{q}", dep=tok)
        part = pltpu.with_memory_space_constraint(part, pltpu.HBM)
        sent_i.append(_send_start([part], _place([part], True, f"place_gwin_{q}"), True, f"start_gwin_{q}"))
        tok = sent_i[-1][4]
    dh_all = _mm_dh(dproj, win, "mm_dh", tok)
    grad_x, dshift, dscale, ggn = _norm_bwd(x2, dh_all, lc, g_norm, scale, dxn, jnp.zeros((1, D), F32), "norm_bwd_lat")
    _, dshift_c, dscale_c, ggn = _norm_bwd(ctx2, dh_all, 0, g_norm, scale_c, None, ggn, "norm_bwd_ctx")

    dmod_me = jnp.concatenate([dshift, dscale, dgate], axis=1)
    dmod_c = jnp.concatenate([dshift_c, dscale_c, jnp.zeros((1, D), F32)], axis=1)
    smalls = [ggf, ggn, gcw, gcb, gcl, gbr, gbi, gbp, gps, jnp.pad(loss_p, ((0, 0), (0, LANE - 1)))]
    sizes = [s.size for s in smalls]
    small_all, dmod_all, dmodc_all = _all_gather([_pack(smalls), dmod_me, dmod_c], "gather_small")
    ga = lax.dynamic_slice(dmod_all.reshape(NDEV, 3 * D), (0, me * na), (NDEV, na))
    gc = lax.dynamic_slice(dmodc_all.reshape(NDEV, 3 * D), (0, me * na), (NDEV, na))
    g_wada, d_wada, nm_wada, nv_wada, pc = _ada_bwd(s_all, ga, gc, w_ada[0], m_w_ada[0], v_w_ada[0], "ada_bwd")
    (pc_all,) = _all_gather([pc[0:1]], "gather_cctx")
    tot, gb_ada = _small_sum(
        jnp.concatenate([small_all.reshape(NDEV, -1), pc_all.reshape(NDEV, D)], axis=1),
        dmod_all.reshape(NDEV, 3 * D), dmodc_all.reshape(NDEV, 3 * D), "small_sum")
    offs = [0]
    for s in sizes + [D]:
        offs.append(offs[-1] + s)
    t_ggf, t_ggn, t_gcw, t_gcb, t_gcl, t_gbr, t_gbi, t_gbp, t_gps, t_loss, t_pc = [
        tot[:, offs[i]:offs[i + 1]] for i in range(len(offs) - 1)]

    def shard(t, rows):
        return lax.dynamic_slice(t.reshape(rows, W), (0, me * ws), (rows, ws))

    def big(wv, parts, mv, vv, name):
        shp = wv.shape
        C = shp[-1]
        if not isinstance(parts, list):
            parts = [parts]
        parts = [p.reshape(NDEV, -1, C // len(parts)) for p in parts]
        outs = _adamw_parts(wv.reshape(-1, C), parts, mv.reshape(-1, C), vv.reshape(-1, C), name)
        return [o.reshape(shp) for o in outs]

    (recv_o,) = _send_wait(sent_o, tot, True, "wait_gwout")
    recv_p, recv_r, recv_i = _send_wait(sent_s, tot, True, "wait_gsmall")
    r_wout = big(w_out, recv_o, m_w_out, v_w_out, "adamw_w_out")
    r_wpool = big(w_pool, recv_p, m_w_pool, v_w_pool, "adamw_w_pool")
    r_wr = big(w_rgate, recv_r, m_w_rgate, v_w_rgate, "adamw_w_rgate")
    r_wi = big(w_igate, recv_i, m_w_igate, v_w_igate, "adamw_w_igate")
    r_wada = [o.reshape(w_ada.shape) for o in (g_wada, d_wada, nm_wada, nv_wada)]

    names = ["c_ctx", "b_ada", "g_norm", "conv_w", "conv_b", "lru_lambda", "b_rgate", "b_igate", "b_pool",
             "pool_scale", "g_final"]
    sw = [c_ctx, b_ada, g_norm, conv_w, conv_b, lru_lambda, b_rgate, b_igate, b_pool, pool_scale, g_final]
    sm = [m_c_ctx, m_b_ada, m_g_norm, m_conv_w, m_conv_b, m_lru_lambda, m_b_rgate, m_b_igate, m_b_pool,
          m_pool_scale, m_g_final]
    sv = [v_c_ctx, v_b_ada, v_g_norm, v_conv_w, v_conv_b, v_lru_lambda, v_b_rgate, v_b_igate, v_b_pool,
          v_pool_scale, v_g_final]
    sg = [t_pc, gb_ada, t_ggn, shard(t_gcw, 4), t_gcb, shard(t_gcl, 2), shard(t_gbr, 2), shard(t_gbi, 2), t_gbp,
          t_gps, t_ggf]
    poffs = [0]
    for wv in sw:
        poffs.append(poffs[-1] + wv.size)
    lam_range = (poffs[5], poffs[6])
    cctx_range = (poffs[0], poffs[1])
    small_out = _adamw_small(_pack(sg), _pack(sw), _pack(sm), _pack(sv), lam_range, cctx_range, "adamw_small")
    recv_w = [_send_wait(sent_i[q], small_out[0], True, f"wait_gwin_{q}")[0] for q in range(GWIN_PARTS)]
    r_win = big(w_in, recv_w, m_w_in, v_w_in, "adamw_w_in")
    r_small = {}
    for i, nm in enumerate(names):
        r_small[nm] = [o[:, poffs[i]:poffs[i + 1]].reshape(sw[i].shape) for o in small_out]

    res = dict(r_small)
    res.update(w_ada=r_wada, w_in=r_win, w_rgate=r_wr, w_igate=r_wi, w_pool=r_wpool, w_out=r_wout)
    order = ["c_ctx", "w_ada", "b_ada", "g_norm", "w_in", "conv_w", "conv_b", "lru_lambda", "w_rgate", "b_rgate",
             "w_igate", "b_igate", "w_pool", "b_pool", "pool_scale", "w_out", "g_final"]
    loss = t_loss[0, 0]
    outs = [loss, grad_x.reshape(x.shape)]
    for q in range(4):
        outs += [res[nm][q] for nm in order]
    return tuple(outs)
```

```python
import functools

import jax
import jax.numpy as jnp
from jax import lax
from jax.experimental import pallas as pl
from jax.experimental.pallas import tpu as pltpu

NDEV = 8
GRID_W = 64
POOL_WINDOWS = (2, 4, 8, 16)
LRU_C = 8.0
EPS = 1e-6
ADAM_LR = 0.001
ADAM_B1 = 0.9
ADAM_B2 = 0.999
ADAM_EPS = 1e-08
ADAM_WD = 0.01
ADAM_STEP = 10

F32 = jnp.float32
MXU = jnp.bfloat16

VMEM_BYTES = 64 * 1024 * 1024
VMEM_SLACK = 8 * 1024 * 1024
SUB = 8
SUB16 = 16
LANE = 128

TM = 1152
TN = 1024
TK = 2048
TL = 256
TL_FINAL = 128
TL_GATES = 1088
CB_POOL = 1024
CB_SEQ = 256
CB_SCAN = 1024
CB_MIX = 2048
TR_CONV = 576
GWIN_PARTS = 8
ADAMW_BLOCK_BYTES = 16 * 1024 * 1024
WIN_PARTS = 4

MESH_ID = pl.DeviceIdType.MESH


def _tile(n, pref, align):
    if n <= pref:
        return n
    for t in range(pref - pref % align, 0, -align):
        if n % t == 0:
            return t
    return n


def _nbytes(shape, dtype):
    n = 1
    for s in shape:
        if s is not None:
            n *= s
    return n * jnp.dtype(dtype).itemsize


def _params(blocks, scratch=(), dims=None):
    need = 2 * sum(_nbytes(s, d) for s, d in blocks) + sum(_nbytes(s, d) for s, d in scratch) + VMEM_SLACK
    kw = dict(vmem_limit_bytes=int(min(max(need, 2 * VMEM_SLACK), VMEM_BYTES - VMEM_SLACK // 2)))
    if dims is not None:
        kw["dimension_semantics"] = dims
    return pltpu.CompilerParams(**kw)


def _sds(shape, dtype):
    return jax.ShapeDtypeStruct(tuple(shape), dtype)


ANY = pl.BlockSpec(memory_space=pl.ANY)


def _ids():
    return lax.axis_index("x"), lax.axis_index("y"), lax.axis_index("c")


def _sigmoid(v):
    return 0.5 * jnp.tanh(0.5 * v) + 0.5


def _sigmoid_small(v):
    return jax.nn.sigmoid(v)


def _softplus(v):
    return jnp.maximum(v, 0.0) + jnp.log1p(jnp.exp(-jnp.abs(v)))


def _all_gather(xs, name):
    n = len(xs)

    def body(*refs):
        x_refs, o_refs = refs[:n], refs[n:2 * n]
        send_sems, recv_sems, local_sems = refs[2 * n:]
        x, y, c = _ids()
        me, sibling = (x, y, c), (x, y, 1 - c)
        chips = [(1 - x, y), (x, 1 - y), (1 - x, 1 - y)]

        def slot(a, p):
            return o_refs[a].at[4 * p[0] + 2 * p[1] + p[2]]

        def copy(a, k, block, to, src=None):
            return pltpu.make_async_remote_copy(
                src_ref=slot(a, block) if src is None else src, dst_ref=slot(a, block),
                send_sem=send_sems.at[7 * a + k], recv_sem=recv_sems.at[7 * a + k],
                device_id=to, device_id_type=MESH_ID)

        mine, first, passed = [], [], []
        for a in range(n):
            m = pltpu.make_async_copy(x_refs[a], slot(a, me), local_sems.at[a])
            m.start()
            mine.append(m)
            f = [copy(a, 0, me, sibling, src=x_refs[a])]
            f += [copy(a, 1 + j, me, (*chip, c), src=x_refs[a]) for j, chip in enumerate(chips)]
            for cp in f:
                cp.start()
            first += f
        for a in range(n):
            for j, chip in enumerate(chips):
                copy(a, 1 + j, (*chip, c), me).wait_recv()
                p = copy(a, 4 + j, (*chip, c), sibling)
                p.start()
                passed.append(p)
        for a in range(n):
            copy(a, 0, sibling, me).wait_recv()
            for j, chip in enumerate(chips):
                copy(a, 4 + j, (*chip, 1 - c), me).wait_recv()
        for cp in first + passed:
            cp.wait_send()
        for m in mine:
            m.wait()

    return pl.pallas_call(
        body, name=name,
        out_shape=[_sds((NDEV,) + v.shape, v.dtype) for v in xs],
        in_specs=[ANY] * n, out_specs=[ANY] * n,
        scratch_shapes=[pltpu.SemaphoreType.DMA((7 * n,)), pltpu.SemaphoreType.DMA((7 * n,)),
                        pltpu.SemaphoreType.DMA((n,))],
    )(*xs)


HBM = pl.BlockSpec(memory_space=pltpu.HBM)
SEM = pl.BlockSpec(memory_space=pltpu.SEMAPHORE)
EFFECT = pltpu.SideEffectType.DATAFLOW_SIDE_EFFECTING


def _peers():
    x, y, c = _ids()
    out = []
    for k in range(1, NDEV):
        px = 1 - x if k & 4 else x
        py = 1 - y if k & 2 else y
        pc = 1 - c if k & 1 else c
        out.append(((px, py, pc), 4 * px + 2 * py + pc))
    return out, 4 * x + 2 * y + c


def _tie(v, deps, name):
    def body(v_ref, *rest):
        rest[-1][...] = v_ref[...]

    vmem = pl.BlockSpec(memory_space=pltpu.VMEM)
    return pl.pallas_call(
        body, name=name, out_shape=_sds(v.shape, v.dtype), in_specs=[vmem] + [ANY] * len(deps), out_specs=vmem,
    )(v, *deps)


def _place(srcs, from_slot, name, deps=()):
    n = len(srcs)
    blks = [v.shape[1:] if from_slot else v.shape for v in srcs]

    nd = len(deps)

    def body(*refs):
        s_refs, l_refs = refs[:n], refs[n + nd:2 * n + nd]
        bufs, sems = refs[2 * n + nd:3 * n + nd], refs[3 * n + nd]
        x, y, c = _ids()
        me = 4 * x + 2 * y + c
        ins = [pltpu.make_async_copy(s_refs[a].at[me] if from_slot else s_refs[a], bufs[a], sems.at[a])
               for a in range(n)]
        outs = [pltpu.make_async_copy(bufs[a], l_refs[a].at[me], sems.at[n + a]) for a in range(n)]
        for cp in ins:
            cp.start()
        for a in range(n):
            ins[a].wait()
            outs[a].start()
        for cp in outs:
            cp.wait()

    scratch = [(b, v.dtype) for b, v in zip(blks, srcs)]
    return pl.pallas_call(
        body, name=name, out_shape=[_sds((NDEV,) + b, v.dtype) for b, v in zip(blks, srcs)],
        in_specs=[ANY] * (n + nd), out_specs=[ANY] * n,
        scratch_shapes=[pltpu.VMEM(b, d) for b, d in scratch] + [pltpu.SemaphoreType.DMA((2 * n,))],
        compiler_params=_params([], scratch),
    )(*srcs, *deps)


SEND_PEERS = {True: 7, False: 7, "level1": 4, "level2": 3}


def _send_copies(s_refs, l_refs, ssem, rsem, mode, receiving):
    peers, me = _peers()
    x, y, c = _ids()
    sibling = (x, y, 1 - c)
    chips = [(1 - x, y), (x, 1 - y), (1 - x, 1 - y)]
    npeer = SEND_PEERS[mode]
    out = []
    for a in range(len(l_refs)):
        if mode == "level2":
            for k, (px, py) in enumerate(chips):
                slot = 4 * px + 2 * py + (1 - c if receiving else c)
                out.append(pltpu.make_async_remote_copy(
                    src_ref=l_refs[a].at[slot], dst_ref=l_refs[a].at[slot], send_sem=ssem.at[npeer * a + k],
                    recv_sem=rsem.at[npeer * a + k], device_id=sibling, device_id_type=MESH_ID))
            continue
        targets = peers
        if mode == "level1":
            targets = [(sibling, 4 * x + 2 * y + 1 - c)] + [((px, py, c), 4 * px + 2 * py + c) for px, py in chips]
        for k, (dev, idx) in enumerate(targets):
            out.append(pltpu.make_async_remote_copy(
                src_ref=s_refs[a].at[idx] if mode is True else s_refs[a],
                dst_ref=l_refs[a].at[idx if receiving else me],
                send_sem=ssem.at[npeer * a + k], recv_sem=rsem.at[npeer * a + k], device_id=dev, device_id_type=MESH_ID))
    return out


def _send_start(srcs, lands, mode, name):
    ns, n = len(srcs), len(lands)
    nsem = SEND_PEERS[mode] * n

    def body(*refs):
        s_refs, l_refs = refs[:ns], refs[ns:ns + n]
        ssem, rsem = refs[ns + n], refs[ns + n + 1]
        token = refs[-1]
        for send in _send_copies(s_refs, l_refs, ssem, rsem, mode, False):
            send.start()
        token[...] = jnp.zeros_like(token)

    bufs = list(srcs) + list(lands)
    outs = pl.pallas_call(
        body, name=name,
        out_shape=[pltpu.SemaphoreType.DMA((nsem,)), pltpu.SemaphoreType.DMA((nsem,))]
        + [pltpu.HBM(v.shape, v.dtype) for v in bufs] + [_sds((SUB, LANE), F32)],
        in_specs=[HBM] * (ns + n), out_specs=[SEM, SEM] + [HBM] * (ns + n) + [pl.BlockSpec(memory_space=pltpu.VMEM)],
        input_output_aliases={i: 2 + i for i in range(ns + n)},
        compiler_params=pltpu.CompilerParams(has_side_effects=EFFECT),
    )(*[pltpu.with_memory_space_constraint(v, pltpu.HBM) for v in bufs])
    return outs[0], outs[1], list(outs[2:2 + ns]), list(outs[2 + ns:2 + ns + n]), outs[-1]


def _send_wait(started, after, mode, name):
    ssem, rsem, srcs, lands, _ = started
    ns, n = len(srcs), len(lands)

    def body(*refs):
        s_refs, l_refs = refs[:ns], refs[ns:ns + n]
        ssem_ref, rsem_ref = refs[ns + n], refs[ns + n + 1]
        for recv in _send_copies(s_refs, l_refs, ssem_ref, rsem_ref, mode, True):
            recv.wait_send()
            recv.wait_recv()

    bufs = list(srcs) + list(lands)
    outs = pl.pallas_call(
        body, name=name, out_shape=[pltpu.HBM(v.shape, v.dtype) for v in bufs],
        in_specs=[HBM] * (ns + n) + [SEM, SEM, ANY], out_specs=[HBM] * (ns + n),
        input_output_aliases={i: i for i in range(ns + n)},
        compiler_params=pltpu.CompilerParams(has_side_effects=EFFECT),
    )(*bufs, ssem, rsem, after)
    return list(outs[ns:])


NN = (((1,), (0,)), ((), ()))
NT = (((1,), (1,)), ((), ()))
TN_DIMS = (((0,), (0,)), ((), ()))


def _mm(a, b, *, grid, a_spec, b_spec, o_spec, out_shape, acc_shape, dims, name, dep=None, fill=None):
    k_axis = len(grid) - 1
    nk = grid[k_axis]
    extra = [v for v in (dep, fill) if v is not None]
    aliases = {} if fill is None else {1 + len(extra): 0}

    def body(a_ref, b_ref, *rest):
        o_ref, acc_ref = rest[-2], rest[-1]
        k = pl.program_id(k_axis)

        def prod():
            return lax.dot_general(a_ref[...], b_ref[...], dims, preferred_element_type=F32)

        if nk == 1:
            o_ref[...] = prod().astype(o_ref.dtype)
            return

        @pl.when(k == 0)
        def _():
            acc_ref[...] = prod()

        if nk > 2:
            @pl.when((k > 0) & (k < nk - 1))
            def _():
                acc_ref[...] += prod()

        @pl.when(k == nk - 1)
        def _():
            o_ref[...] = (acc_ref[...] + prod()).astype(o_ref.dtype)

    blocks = [(a_spec.block_shape, a.dtype), (b_spec.block_shape, b.dtype), (o_spec.block_shape, out_shape.dtype)]
    return pl.pallas_call(
        body, name=name, grid=grid, in_specs=[a_spec, b_spec] + [ANY] * len(extra), out_specs=o_spec,
        out_shape=out_shape, scratch_shapes=[pltpu.VMEM(acc_shape, F32)], input_output_aliases=aliases,
        compiler_params=_params(blocks, [(acc_shape, F32)], ("parallel",) * k_axis + ("arbitrary",)),
    )(a, b, *extra)


def _mm_plain(a, b, dims, out_dtype, name):
    if dims == TN_DIMS:
        (K, M), N = a.shape, b.shape[1]
    elif dims == NT:
        (M, K), N = a.shape, b.shape[0]
    else:
        (M, K), N = a.shape, b.shape[1]
    tm, tn = _tile(M, TM, LANE), _tile(N, TN, LANE)
    tk = _tile(K, TK, LANE if dims != TN_DIMS else SUB16)
    if dims == TN_DIMS:
        a_spec = pl.BlockSpec((tk, tm), lambda i, j, k: (k, i))
    else:
        a_spec = pl.BlockSpec((tm, tk), lambda i, j, k: (i, k))
    if dims == NT:
        b_spec = pl.BlockSpec((tn, tk), lambda i, j, k: (j, k))
    else:
        b_spec = pl.BlockSpec((tk, tn), lambda i, j, k: (k, j))
    return _mm(a, b, grid=(M // tm, N // tn, K // tk), a_spec=a_spec, b_spec=b_spec,
               o_spec=pl.BlockSpec((tm, tn), lambda i, j, k: (i, j)),
               out_shape=_sds((M, N), out_dtype), acc_shape=(tm, tn), dims=dims, name=name)


def _mm_proj(h_all, win_q, q, nparts, fill, name):
    n, D = h_all.shape
    nbp = win_q.shape[2]
    nb = nbp * nparts
    tm, tn, tk = _tile(n, TM, SUB16), _tile(nbp, TN, LANE), D
    nbn = nbp // tn
    return _mm(h_all, win_q, grid=(n // tm, NDEV * nbn, D // tk),
               a_spec=pl.BlockSpec((tm, tk), lambda i, j, k: (i, k)),
               b_spec=pl.BlockSpec((None, tk, tn), lambda i, j, k: (j // nbn, k, j % nbn)),
               o_spec=pl.BlockSpec((tm, tn), lambda i, j, k: (i, (j // nbn) * (nb // tn) + q * nbn + j % nbn)),
               out_shape=_sds((n, NDEV * nb), F32), acc_shape=(tm, tn), dims=NN, name=name, fill=fill)


def _mm_dh(dproj, wins, name, dep):
    nparts = len(wins)
    n = dproj.shape[0]
    _, D, nbp = wins[0].shape
    nb = nbp * nparts
    tm, tn = _tile(n, TM, SUB16), _tile(D, TN, LANE)

    def body(a_ref, *rest):
        b_refs, o_ref, acc_ref = rest[:nparts], rest[-2], rest[-1]
        k = pl.program_id(2)

        def prod():
            out = None
            for q in range(nparts):
                d = lax.dot_general(a_ref[:, q * nbp:(q + 1) * nbp], b_refs[q][...], NT, preferred_element_type=F32)
                out = d if out is None else out + d
            return out

        @pl.when(k == 0)
        def _():
            acc_ref[...] = prod()

        @pl.when((k > 0) & (k < NDEV - 1))
        def _():
            acc_ref[...] += prod()

        @pl.when(k == NDEV - 1)
        def _():
            o_ref[...] = acc_ref[...] + prod()

    blocks = [((tm, nb), dproj.dtype)] + [((tn, nbp), wins[0].dtype)] * nparts + [((tm, tn), F32)]
    return pl.pallas_call(
        body, name=name, grid=(n // tm, D // tn, NDEV),
        in_specs=[pl.BlockSpec((tm, nb), lambda i, j, k: (i, k))]
        + [pl.BlockSpec((None, tn, nbp), lambda i, j, k: (k, j, 0))] * nparts + [ANY],
        out_specs=pl.BlockSpec((tm, tn), lambda i, j, k: (i, j)), out_shape=_sds((n, D), F32),
        scratch_shapes=[pltpu.VMEM((tm, tn), F32)],
        compiler_params=_params(blocks, [((tm, tn), F32)], ("parallel", "parallel", "arbitrary")),
    )(dproj, *wins, dep)


def _transpose(x, name):
    R, C = x.shape
    tr, tc = _tile(R, TL, LANE), _tile(C, 2 * TL, LANE)

    def body(x_ref, o_ref):
        o_ref[...] = x_ref[...].T

    return pl.pallas_call(
        body, name=name, grid=(R // tr, C // tc),
        in_specs=[pl.BlockSpec((tr, tc), lambda i, j: (i, j))],
        out_specs=pl.BlockSpec((tc, tr), lambda i, j: (j, i)),
        out_shape=_sds((C, R), x.dtype),
        compiler_params=_params([((tr, tc), x.dtype)] * 2, dims=("parallel", "parallel")),
    )(x)


def _mm_gwin(h_t, dproj, nb, part, nparts, name, dep=None):
    D, n = h_t.shape
    nbp = nb // nparts
    tm, tn, tk = _tile(D, TM, LANE), _tile(nbp, TN, LANE), n
    nbn = nbp // tn
    return _mm(h_t, dproj, grid=(D // tm, NDEV * nbn, n // tk),
               a_spec=pl.BlockSpec((tm, tk), lambda i, j, k: (i, k)),
               b_spec=pl.BlockSpec((tk, tn), lambda i, j, k: (k, (j // nbn) * (nb // tn) + part * nbn + j % nbn)),
               o_spec=pl.BlockSpec((None, tm, tn), lambda i, j, k: (j // nbn, i, j % nbn)),
               out_shape=_sds((NDEV, D, nbp), MXU), acc_shape=(tm, tn), dims=NN, name=name, dep=dep)


def _mm_group(a, b, mode, out_dtype, name):
    if mode == "wgrad":
        L, W = a.shape
        G = len(POOL_WINDOWS)
        pd = W // G
        tm, tn, tk = _tile(pd, TM, LANE), _tile(pd, TN, LANE), _tile(L, TK, SUB16)
        nm, nn = pd // tm, pd // tn
        return _mm(a, b, grid=(G, nm, nn, L // tk),
                   a_spec=pl.BlockSpec((tk, tm), lambda g, i, j, k: (k, g * nm + i)),
                   b_spec=pl.BlockSpec((tk, tn), lambda g, i, j, k: (k, g * nn + j)),
                   o_spec=pl.BlockSpec((None, tm, tn), lambda g, i, j, k: (g, i, j)),
                   out_shape=_sds((G, pd, pd), out_dtype), acc_shape=(tm, tn), dims=TN_DIMS, name=name)
    L, W = a.shape
    G, pd, _ = b.shape
    tm, tn, tk = _tile(L, TM, SUB16), _tile(pd, TN, LANE), _tile(pd, TK, LANE)
    nn, nk = pd // tn, pd // tk
    if mode == "fwd":
        b_spec = pl.BlockSpec((None, tk, tn), lambda g, i, j, k: (g, k, j))
        dims = NN
    else:
        b_spec = pl.BlockSpec((None, tn, tk), lambda g, i, j, k: (g, j, k))
        dims = NT
    return _mm(a, b, grid=(G, L // tm, nn, nk),
               a_spec=pl.BlockSpec((tm, tk), lambda g, i, j, k: (i, g * nk + k)),
               b_spec=b_spec,
               o_spec=pl.BlockSpec((tm, tn), lambda g, i, j, k: (i, g * nn + j)),
               out_shape=_sds((L, W), out_dtype), acc_shape=(tm, tn), dims=dims, name=name)


def _ada_fwd(cc, w_loc, b_loc, name):
    R, D = cc.shape
    na = w_loc.shape[1]
    tk = _tile(D, 512, LANE)

    def body(c_ref, w_ref, b_ref, mod_ref, s_ref):
        k = pl.program_id(0)
        cv = c_ref[...]
        s = cv * _sigmoid_small(cv)
        s_ref[...] = s

        @pl.when(k == 0)
        def _():
            mod_ref[...] = jnp.broadcast_to(b_ref[...], mod_ref.shape)

        mod_ref[...] += lax.dot_general(s.astype(MXU), w_ref[...].astype(MXU), NN, preferred_element_type=F32)

    blocks = [((R, tk), F32), ((tk, na), F32), ((1, na), F32), ((R, na), F32), ((R, tk), F32)]
    return pl.pallas_call(
        body, name=name, grid=(D // tk,),
        in_specs=[pl.BlockSpec((R, tk), lambda k: (0, k)), pl.BlockSpec((tk, na), lambda k: (k, 0)),
                  pl.BlockSpec((1, na), lambda k: (0, 0))],
        out_specs=[pl.BlockSpec((R, na), lambda k: (0, 0)), pl.BlockSpec((R, tk), lambda k: (0, k))],
        out_shape=[_sds((R, na), F32), _sds((R, D), F32)],
        compiler_params=_params(blocks, dims=("arbitrary",)),
    )(cc, w_loc, b_loc)


def _adam(w, g, m, v):
    m = ADAM_B1 * m + (1.0 - ADAM_B1) * g
    v = ADAM_B2 * v + (1.0 - ADAM_B2) * (g * g)
    m_hat = m / (1.0 - ADAM_B1 ** ADAM_STEP)
    v_hat = v / (1.0 - ADAM_B2 ** ADAM_STEP)
    delta = -ADAM_LR * (m_hat / (jnp.sqrt(v_hat) + ADAM_EPS) + ADAM_WD * w)
    return delta, m, v


def _ada_bwd(s_all, ga, gc, w_loc, m_loc, v_loc, name):
    D, na = w_loc.shape
    tr = _tile(D, 256, LANE)

    def body(s_ref, ga_ref, gc_ref, w_ref, m_ref, v_ref, g_ref, d_ref, nm_ref, nv_ref, pc_ref):
        dmc = gc_ref[0:1, :]
        for p in range(1, NDEV):
            dmc = dmc + gc_ref[p:p + 1, :]
        rows = lax.broadcasted_iota(jnp.int32, (NDEV, na), 0)
        dmc8 = jnp.where(rows == 0, jnp.broadcast_to(dmc, (NDEV, na)), 0.0)
        dm = jnp.concatenate([ga_ref[...], dmc8], axis=0).astype(MXU)
        dmc16 = jnp.concatenate([dmc8, jnp.zeros_like(dmc8)], axis=0).astype(MXU)
        w = w_ref[...]
        g = lax.dot_general(s_ref[...].astype(MXU), dm, TN_DIMS, preferred_element_type=F32)
        pc_ref[...] = lax.dot_general(dmc16, w.astype(MXU), NT, preferred_element_type=F32)
        delta, nm, nv = _adam(w, g, m_ref[...], v_ref[...])
        g_ref[...] = g
        d_ref[...] = delta
        nm_ref[...] = nm
        nv_ref[...] = nv

    big = pl.BlockSpec((tr, na), lambda i: (i, 0))
    full = pl.BlockSpec((NDEV, na), lambda i: (0, 0))
    srow = pl.BlockSpec((2 * NDEV, tr), lambda i: (0, i))
    blocks = [((2 * NDEV, tr), F32)] * 2 + [((NDEV, na), F32)] * 2 + [((tr, na), F32)] * 7
    return pl.pallas_call(
        body, name=name, grid=(D // tr,),
        in_specs=[srow, full, full, big, big, big],
        out_specs=[big, big, big, big, srow],
        out_shape=[_sds((D, na), F32)] * 4 + [_sds((2 * NDEV, D), F32)],
        compiler_params=_params(blocks, dims=("parallel",)),
    )(s_all, ga, gc, w_loc, m_loc, v_loc)


def _norm_mod(x2, g, shift, scale, n, row0, h_prev, name):
    R, D = x2.shape
    tl = _tile(R, TL, SUB16)
    assert row0 % tl == 0
    b0 = row0 // tl

    def body(x_ref, g_ref, sh_ref, sc_ref, *rest):
        o_ref = rest[-1]
        xv = x_ref[...]
        s = lax.rsqrt(jnp.mean(xv * xv, axis=-1, keepdims=True) + EPS)
        nrm = xv * s * g_ref[...]
        o_ref[...] = (nrm * (1.0 + sc_ref[...]) + sh_ref[...]).astype(o_ref.dtype)

    vec = pl.BlockSpec((1, D), lambda i: (0, 0))
    in_specs = [pl.BlockSpec((tl, D), lambda i: (i, 0)), vec, vec, vec]
    args = [x2, g, shift, scale]
    aliases = {}
    if h_prev is not None:
        in_specs.append(ANY)
        args.append(h_prev)
        aliases = {4: 0}
    blocks = [((tl, D), F32), ((tl, D), MXU)] + [((1, D), F32)] * 3
    return pl.pallas_call(
        body, name=name, grid=(R // tl,), in_specs=in_specs,
        out_specs=pl.BlockSpec((tl, D), lambda i: (i + b0, 0)),
        out_shape=_sds((n, D), MXU), input_output_aliases=aliases,
        compiler_params=_params(blocks, dims=("parallel",)),
    )(*args)


def _norm_bwd(x2, dh_all, row0, g, scale, dxn, ggn0, name):
    R, D = x2.shape
    tl = _tile(R, TL_FINAL, SUB)
    assert row0 % tl == 0
    b0 = row0 // tl
    with_x = dxn is not None

    def body(*refs):
        if with_x:
            x_ref, dh_ref, g_ref, sc_ref, gg0_ref, dxn_ref, gx_ref, dsh_ref, dsc_ref, gg_ref = refs
        else:
            x_ref, dh_ref, g_ref, sc_ref, gg0_ref, dsh_ref, dsc_ref, gg_ref = refs
        i = pl.program_id(0)

        @pl.when(i == 0)
        def _():
            dsh_ref[...] = jnp.zeros_like(dsh_ref)
            dsc_ref[...] = jnp.zeros_like(dsc_ref)
            gg_ref[...] = gg0_ref[...]

        xv = x_ref[...]
        dh = dh_ref[...]
        gv = g_ref[...]
        s = lax.rsqrt(jnp.mean(xv * xv, axis=-1, keepdims=True) + EPS)
        xh = xv * s
        dsh_ref[...] += jnp.sum(dh, axis=0, keepdims=True)
        dsc_ref[...] += jnp.sum(dh * (xh * gv), axis=0, keepdims=True)
        dn = dh * (1.0 + sc_ref[...])
        gg_ref[...] += jnp.sum(dn * xh, axis=0, keepdims=True)
        if with_x:
            dxh = dn * gv
            dx = s * (dxh - xh * jnp.mean(dxh * xh, axis=-1, keepdims=True))
            gx_ref[...] = dx + dxn_ref[...]

    vec = pl.BlockSpec((1, D), lambda i: (0, 0))
    row = pl.BlockSpec((tl, D), lambda i: (i, 0))
    in_specs = [row, pl.BlockSpec((tl, D), lambda i: (i + b0, 0)), vec, vec, vec]
    args = [x2, dh_all, g, scale, ggn0]
    out_specs = [vec, vec, vec]
    out_shape = [_sds((1, D), F32)] * 3
    if with_x:
        in_specs.append(row)
        args.append(dxn)
        out_specs = [row] + out_specs
        out_shape = [_sds((R, D), F32)] + out_shape
    blocks = [((tl, D), F32)] * (4 if with_x else 2) + [((1, D), F32)] * 6
    outs = pl.pallas_call(
        body, name=name, grid=(R // tl,), in_specs=in_specs, out_specs=out_specs, out_shape=out_shape,
        compiler_params=_params(blocks, dims=("arbitrary",)),
    )(*args)
    return tuple(outs) if with_x else (None,) + tuple(outs)


def _tap_valid(t, o, lc, n):
    tt = t + o
    in_ctx = t < lc
    return (tt >= jnp.where(in_ctx, 0, lc)) & (tt < jnp.where(in_ctx, lc, n))


def _conv_fwd(proj_all, cw, cb, lc, W, name):
    n = proj_all.shape[0]
    cbk = _tile(W, CB_SEQ, LANE)
    tr = _tile(n, TR_CONV, SUB16)
    ext = tr + 2 * SUB

    def body(x_ref, w_ref, b_ref, u_ref, xp_ref):
        xp_ref[0:SUB, :] = jnp.zeros((SUB, cbk), F32)
        xp_ref[n + SUB:n + 2 * SUB, :] = jnp.zeros((SUB, cbk), F32)
        xp_ref[SUB:n + SUB, :] = x_ref[...]
        w = w_ref[...]
        bias = b_ref[...]

        def chunk(ci, carry):
            r0 = pl.multiple_of(ci * tr, SUB16)
            xe = xp_ref[pl.ds(r0, ext), :]
            t = r0 + lax.broadcasted_iota(jnp.int32, (tr, cbk), 0)
            acc = jnp.broadcast_to(bias, (tr, cbk))
            for k in range(4):
                o = k - 1
                sh = xe if o == 0 else pltpu.roll(xe, (-o) % ext, 0)
                acc = acc + jnp.where(_tap_valid(t, o, lc, n), sh[SUB:tr + SUB], 0.0) * w[k:k + 1]
            u_ref[pl.ds(r0, tr), :] = acc
            return carry

        lax.fori_loop(0, n // tr, chunk, 0)

    blocks = [((n, cbk), F32)] * 2 + [((4, cbk), F32), ((1, cbk), F32)]
    scratch = [((n + 2 * SUB, cbk), F32)]
    return pl.pallas_call(
        body, name=name, grid=(W // cbk,),
        in_specs=[pl.BlockSpec((n, cbk), lambda j: (0, j)), pl.BlockSpec((4, cbk), lambda j: (0, j)),
                  pl.BlockSpec((1, cbk), lambda j: (0, j))],
        out_specs=pl.BlockSpec((n, cbk), lambda j: (0, j)),
        out_shape=_sds((n, W), F32),
        scratch_shapes=[pltpu.VMEM(s, d) for s, d in scratch],
        compiler_params=_params(blocks, scratch, ("parallel",)),
    )(proj_all, cw, cb)


def _conv_bwd(du_all, proj_all, cw, dproj, lc, W, name):
    n = du_all.shape[0]
    cbk = _tile(W, CB_SEQ, LANE)
    tr = _tile(n, TR_CONV, SUB16)
    ext = tr + 2 * SUB

    def body(du_ref, x_ref, w_ref, dp_in, dx_ref, gw_ref, gb_ref, dp_ref, xp_ref):
        del dp_in
        for ref, src in ((dp_ref, du_ref), (xp_ref, x_ref)):
            ref[0:SUB, :] = jnp.zeros((SUB, cbk), F32)
            ref[n + SUB:n + 2 * SUB, :] = jnp.zeros((SUB, cbk), F32)
            ref[SUB:n + SUB, :] = src[...]
        w = w_ref[...]

        def fold(v):
            return jnp.sum(v.reshape(tr // SUB, SUB, cbk), axis=0)

        def chunk(ci, carry):
            r0 = pl.multiple_of(ci * tr, SUB16)
            de = dp_ref[pl.ds(r0, ext), :]
            xe = xp_ref[pl.ds(r0, ext), :]
            t = r0 + lax.broadcasted_iota(jnp.int32, (tr, cbk), 0)
            d0 = de[SUB:tr + SUB]
            dx = jnp.zeros((tr, cbk), F32)
            new = []
            for k in range(4):
                o = k - 1
                dsh = de if o == 0 else pltpu.roll(de, o % ext, 0)
                dx = dx + jnp.where(_tap_valid(t, -o, lc, n), dsh[SUB:tr + SUB], 0.0) * w[k:k + 1]
                xsh = xe if o == 0 else pltpu.roll(xe, (-o) % ext, 0)
                new.append(carry[k] + fold(d0 * jnp.where(_tap_valid(t, o, lc, n), xsh[SUB:tr + SUB], 0.0)))
            new.append(carry[4] + fold(d0))
            dx_ref[pl.ds(r0, tr), :] = dx.astype(dx_ref.dtype)
            return tuple(new)

        zero = jnp.zeros((SUB, cbk), F32)
        acc = lax.fori_loop(0, n // tr, chunk, (zero,) * 5)
        for k in range(4):
            gw_ref[k:k + 1, :] = jnp.sum(acc[k], axis=0, keepdims=True)
        gb_ref[...] = jnp.sum(acc[4], axis=0, keepdims=True)

    col = pl.BlockSpec((n, cbk), lambda j: (0, j))
    blocks = [((n, cbk), F32)] * 2 + [((n, cbk), MXU), ((4, cbk), F32), ((4, cbk), F32), ((1, cbk), F32)]
    scratch = [((n + 2 * SUB, cbk), F32)] * 2
    return pl.pallas_call(
        body, name=name, grid=(W // cbk,),
        in_specs=[col, col, pl.BlockSpec((4, cbk), lambda j: (0, j)), ANY],
        out_specs=[col, pl.BlockSpec((4, cbk), lambda j: (0, j)), pl.BlockSpec((1, cbk), lambda j: (0, j))],
        out_shape=[_sds(dproj.shape, dproj.dtype), _sds((4, W), F32), _sds((1, W), F32)],
        input_output_aliases={3: 0},
        scratch_shapes=[pltpu.VMEM(s, d) for s, d in scratch],
        compiler_params=_params(blocks, scratch, ("parallel",)),
    )(du_all, proj_all, cw, dproj)


def _gate_coeffs(ub, u, d, wr_ref, wi_ref, br_ref, bi_ref, lam_ref):
    c = -LRU_C * _softplus(-lam_ref[d:d + 1, :])
    r = _sigmoid(lax.dot_general(ub, wr_ref[d], NN, preferred_element_type=F32) + br_ref[d:d + 1, :])
    ig = _sigmoid(lax.dot_general(ub, wi_ref[d], NN, preferred_element_type=F32) + bi_ref[d:d + 1, :])
    la = c * r
    a = jnp.exp(la)
    sq = jnp.sqrt(-jnp.tanh(la) * (1.0 + a * a))
    return c, r, ig, a, sq


def _gate_specs(tl, hd):
    w_spec = pl.BlockSpec((2, None, hd, hd), lambda h, i: (0, h, 0, 0))
    v_spec = pl.BlockSpec((2, hd), lambda h, i: (0, h))
    return w_spec, v_spec


def _gates_fwd(u_all, wr, wi, br, bi, lam, name):
    n, W = u_all.shape
    heads, hd = wr.shape[1], wr.shape[2]
    tl = _tile(n, TL_GATES, SUB16)

    def body(u_ref, wr_ref, wi_ref, br_ref, bi_ref, lam_ref, a_ref, b_ref):
        u = u_ref[...]
        ub = u.astype(MXU)
        for d in range(2):
            _, _, ig, a, sq = _gate_coeffs(ub, u, d, wr_ref, wi_ref, br_ref, bi_ref, lam_ref)
            a_ref[d] = a
            b_ref[d] = sq * (ig * u)

    w_spec, v_spec = _gate_specs(tl, hd)
    o_spec = pl.BlockSpec((2, tl, hd), lambda h, i: (0, i, h))
    blocks = [((tl, hd), F32), ((2, hd, hd), MXU), ((2, hd, hd), MXU)] + [((2, hd), F32)] * 3 + [((2, tl, hd), F32)] * 2
    return pl.pallas_call(
        body, name=name, grid=(heads, n // tl),
        in_specs=[pl.BlockSpec((tl, hd), lambda h, i: (i, h)), w_spec, w_spec, v_spec, v_spec, v_spec],
        out_specs=[o_spec, o_spec], out_shape=[_sds((2, n, W), F32)] * 2,
        compiler_params=_params(blocks, dims=("parallel", "parallel")),
    )(u_all, wr, wi, br, bi, lam)


def _gates_bwd(u_all, da, db, wr, wi, br, bi, lam, name):
    n, W = u_all.shape
    heads, hd = wr.shape[1], wr.shape[2]
    tl = _tile(n, TL_GATES, SUB16)
    ni = n // tl

    def body(u_ref, da_ref, db_ref, wr_ref, wi_ref, br_ref, bi_ref, lam_ref,
             du_ref, gwr_ref, gwi_ref, gbr_ref, gbi_ref, gc_ref, accr_ref, acci_ref):
        i = pl.program_id(1)

        @pl.when(i == 0)
        def _():
            accr_ref[...] = jnp.zeros_like(accr_ref)
            acci_ref[...] = jnp.zeros_like(acci_ref)
            gbr_ref[...] = jnp.zeros_like(gbr_ref)
            gbi_ref[...] = jnp.zeros_like(gbi_ref)
            gc_ref[...] = jnp.zeros_like(gc_ref)

        u = u_ref[...]
        ub = u.astype(MXU)
        du = jnp.zeros_like(u)
        for d in range(2):
            c, r, ig, a, sq = _gate_coeffs(ub, u, d, wr_ref, wi_ref, br_ref, bi_ref, lam_ref)
            dbv = db_ref[d]
            t = dbv * sq
            du = du + t * ig
            d_la = da_ref[d] * a - (dbv * ig * u) * (a * a) / sq
            gc_ref[d:d + 1, :] += jnp.sum(d_la * r, axis=0, keepdims=True)
            d_pr = (d_la * c) * (r * (1.0 - r))
            d_pi = (t * u) * (ig * (1.0 - ig))
            gbr_ref[d:d + 1, :] += jnp.sum(d_pr, axis=0, keepdims=True)
            gbi_ref[d:d + 1, :] += jnp.sum(d_pi, axis=0, keepdims=True)
            pb = d_pr.astype(MXU)
            qb = d_pi.astype(MXU)
            du = du + lax.dot_general(pb, wr_ref[d], NT, preferred_element_type=F32)
            du = du + lax.dot_general(qb, wi_ref[d], NT, preferred_element_type=F32)
            accr_ref[d] += lax.dot_general(ub, pb, TN_DIMS, preferred_element_type=F32)
            acci_ref[d] += lax.dot_general(ub, qb, TN_DIMS, preferred_element_type=F32)
        du_ref[...] = du

        @pl.when(i == ni - 1)
        def _():
            gwr_ref[...] = accr_ref[...].astype(gwr_ref.dtype)
            gwi_ref[...] = acci_ref[...].astype(gwi_ref.dtype)

    w_spec, v_spec = _gate_specs(tl, hd)
    u_spec = pl.BlockSpec((tl, hd), lambda h, i: (i, h))
    ab_spec = pl.BlockSpec((2, tl, hd), lambda h, i: (0, i, h))
    blocks = ([((tl, hd), F32)] * 2 + [((2, tl, hd), F32)] * 2 + [((2, hd, hd), MXU)] * 4 + [((2, hd), F32)] * 6)
    scratch = [((2, hd, hd), F32)] * 2
    return pl.pallas_call(
        body, name=name, grid=(heads, ni),
        in_specs=[u_spec, ab_spec, ab_spec, w_spec, w_spec, v_spec, v_spec, v_spec],
        out_specs=[u_spec, w_spec, w_spec, v_spec, v_spec, v_spec],
        out_shape=[_sds((n, W), F32), _sds(wr.shape, MXU), _sds(wi.shape, MXU)] + [_sds((2, W), F32)] * 3,
        scratch_shapes=[pltpu.VMEM(s, d) for s, d in scratch],
        compiler_params=_params(blocks, scratch, ("parallel", "arbitrary")),
    )(u_all, da, db, wr, wi, br, bi, lam)


def _tile_scan(A, B, rows, reverse):
    for s in (1, 2, 4):
        if reverse:
            As, Bs, m = pltpu.roll(A, SUB - s, 0), pltpu.roll(B, SUB - s, 0), rows < SUB - s
        else:
            As, Bs, m = pltpu.roll(A, s, 0), pltpu.roll(B, s, 0), rows >= s
        B = jnp.where(m, A * Bs + B, B)
        A = jnp.where(m, A * As, A)
    return A, B


def _scan_chunks(n, lc):
    tc = _tile(lc, TL, SUB)
    assert n % tc == 0 and lc % tc == 0
    return tc, n // tc, lc // tc


def _scan_fwd(a_all, b_all, lc, name):
    _, n, W = a_all.shape
    cb = _tile(W, CB_SCAN, LANE)
    tc, nch, ncc = _scan_chunks(n, lc)
    ntile = tc // SUB

    def chunk(d, t):
        return jnp.where(d == 0, t, jnp.where(t < ncc, ncc - 1 - t, nch - 1 - (t - ncc)))

    def body(a_ref, b_ref, h_ref, carry_ref):
        rows = lax.broadcasted_iota(jnp.int32, (SUB, cb), 0)

        @pl.when(pl.program_id(2) == 0)
        def _():
            carry_ref[...] = jnp.zeros_like(carry_ref)

        def run(reverse):
            def step(i, h):
                r = pl.multiple_of(((ntile - 1 - i) if reverse else i) * SUB, SUB)
                A, B = _tile_scan(a_ref[pl.ds(r, SUB), :], b_ref[pl.ds(r, SUB), :], rows, reverse)
                H = A * h + B
                h_ref[pl.ds(r, SUB), :] = H
                return H[0:1, :] if reverse else H[SUB - 1:SUB, :]

            carry_ref[...] = lax.fori_loop(0, ntile, step, carry_ref[...], unroll=2)

        @pl.when(pl.program_id(1) == 0)
        def _():
            run(False)

        @pl.when(pl.program_id(1) == 1)
        def _():
            run(True)

    spec = pl.BlockSpec((None, tc, cb), lambda j, d, t: (d, chunk(d, t), j))
    return pl.pallas_call(
        body, name=name, grid=(W // cb, 2, nch), in_specs=[spec, spec], out_specs=spec,
        out_shape=_sds((2, n, W), F32), scratch_shapes=[pltpu.VMEM((1, cb), F32)],
        compiler_params=_params([((tc, cb), F32)] * 3, [((1, cb), F32)], ("parallel", "arbitrary", "arbitrary")),
    )(a_all, b_all)


def _scan_bwd(a_all, h_all, dya, lc, name):
    _, n, W = a_all.shape
    cb = _tile(W, CB_SCAN, LANE)
    tc, nch, ncc = _scan_chunks(n, lc)
    ntile = tc // SUB
    nl = nch - ncc

    def chunk(d, t):
        return jnp.where(d == 0, nch - 1 - t, jnp.where(t < nl, ncc + t, t - nl))

    def neighbour(d, t):
        c = chunk(d, t)
        below = jnp.maximum(c * ntile - 1, 0)
        above = jnp.where(c == nch - 1, 0, jnp.minimum((c + 1) * ntile, nch * ntile - 1))
        return jnp.where(d == 0, below, above)

    def body(a_ref, h_ref, hn_ref, g_ref, da_ref, db_ref, mu_ref):
        rows = lax.broadcasted_iota(jnp.int32, (SUB, cb), 0)
        d, t = pl.program_id(1), pl.program_id(2)
        c = chunk(d, t)
        has_g = c >= ncc

        @pl.when(t == 0)
        def _():
            mu_ref[...] = jnp.zeros_like(mu_ref)

        def tile(ref, j):
            return ref[pl.ds(pl.multiple_of(j * SUB, SUB), SUB), :]

        def run(up):
            if up:
                edge = jnp.where(c == ncc - 1, 0.0, hn_ref[0:1, :])
            else:
                edge = jnp.where(c > 0, hn_ref[SUB - 1:SUB, :], 0.0)

            def step(i, mu):
                j = i if up else ntile - 1 - i
                a_t = tile(a_ref, j)
                g_t = jnp.where(has_g, tile(g_ref, j), 0.0)
                if up:
                    ap = jnp.where(rows >= 1, pltpu.roll(a_t, 1, 0), 1.0)
                    nb_row = jnp.where(j < ntile - 1, tile(h_ref, jnp.minimum(j + 1, ntile - 1))[0:1, :], edge)
                    hprev = jnp.where(rows < SUB - 1, pltpu.roll(tile(h_ref, j), SUB - 1, 0), nb_row)
                else:
                    ap = jnp.where(rows < SUB - 1, pltpu.roll(a_t, SUB - 1, 0), 1.0)
                    nb_row = jnp.where(j > 0, tile(h_ref, jnp.maximum(j - 1, 0))[SUB - 1:SUB, :], edge)
                    hprev = jnp.where(rows >= 1, pltpu.roll(tile(h_ref, j), 1, 0), nb_row)
                A, B = _tile_scan(ap, g_t, rows, not up)
                lam = A * mu + B
                r = pl.multiple_of(j * SUB, SUB)
                da_ref[pl.ds(r, SUB), :] = lam * hprev
                db_ref[pl.ds(r, SUB), :] = lam
                return a_t[SUB - 1:SUB, :] * lam[SUB - 1:SUB, :] if up else a_t[0:1, :] * lam[0:1, :]

            mu_ref[...] = lax.fori_loop(0, ntile, step, mu_ref[...], unroll=2)

        @pl.when(d == 0)
        def _():
            run(False)

        @pl.when(d == 1)
        def _():
            run(True)

    spec = pl.BlockSpec((None, tc, cb), lambda j, d, t: (d, chunk(d, t), j))
    n_spec = pl.BlockSpec((None, SUB, cb), lambda j, d, t: (d, neighbour(d, t), j))
    g_spec = pl.BlockSpec((tc, cb), lambda j, d, t: (jnp.maximum(chunk(d, t) - ncc, 0), j))
    blocks = [((tc, cb), F32)] * 5 + [((SUB, cb), F32)]
    return pl.pallas_call(
        body, name=name, grid=(W // cb, 2, nch), in_specs=[spec, spec, n_spec, g_spec], out_specs=[spec, spec],
        out_shape=[_sds((2, n, W), F32)] * 2, scratch_shapes=[pltpu.VMEM((1, cb), F32)],
        compiler_params=_params(blocks, [((1, cb), F32)], ("parallel", "arbitrary", "arbitrary")),
    )(a_all, h_all, h_all, dya)


def _pool_window(v, w, tl, cb, transpose):
    half = w // 2
    pos = lax.broadcasted_iota(jnp.int32, (tl, cb), 0) % GRID_W
    cnt = (jnp.minimum(pos + half - 1, GRID_W - 1) - jnp.maximum(pos - half, 0) + 1).astype(F32)
    src = v / cnt if transpose else v

    def run_sum(s, step):
        span = 1
        while span < half:
            ok = (pos + span < GRID_W) if step > 0 else (pos - span >= 0)
            s = s + jnp.where(ok, pltpu.roll(s, (-step * span) % tl, 0), 0.0)
            span *= 2
        return s

    ahead, behind = run_sum(src, 1), run_sum(src, -1)
    if transpose:
        return behind + jnp.where(pos + 1 < GRID_W, pltpu.roll(ahead, tl - 1, 0), 0.0) - v
    return (ahead + jnp.where(pos >= 1, pltpu.roll(behind, 1, 0), 0.0)) / cnt - v


def _pool_z(src, row0, col0, L, W, transpose, dproj, name):
    G = len(POOL_WINDOWS)
    pd = W // G
    tl = _tile(L, TL, GRID_W)
    cb = _tile(pd, CB_POOL, LANE)
    assert row0 % tl == 0 and col0 % cb == 0
    rb, cbk = row0 // tl, col0 // cb
    nj = pd // cb

    def body(x_ref, *rest):
        o_ref = rest[-1]
        for gi, w in enumerate(POOL_WINDOWS):
            @pl.when(pl.program_id(0) == gi)
            def _(w=w):
                o_ref[...] = _pool_window(x_ref[...], w, tl, cb, transpose).astype(o_ref.dtype)

    plain = pl.BlockSpec((tl, cb), lambda g, i, j: (i, g * nj + j))
    window = pl.BlockSpec((tl, cb), lambda g, i, j: (i + rb, cbk + g * nj + j))
    blocks = [((tl, cb), F32), ((tl, cb), MXU)]
    if transpose:
        return pl.pallas_call(
            body, name=name, grid=(G, L // tl, nj), in_specs=[plain, ANY], out_specs=window,
            out_shape=_sds(dproj.shape, dproj.dtype), input_output_aliases={1: 0},
            compiler_params=_params(blocks, dims=("parallel",) * 3),
        )(src, dproj)
    return pl.pallas_call(
        body, name=name, grid=(G, L // tl, nj), in_specs=[window], out_specs=plain,
        out_shape=_sds((L, W), MXU),
        compiler_params=_params(blocks, dims=("parallel",) * 3),
    )(src)


def _mix_fwd(hs, proj_all, ypre, b_pool, pool_scale, lc, name):
    L, W = ypre.shape
    tl = _tile(L, TL, SUB16)
    cb = _tile(W, CB_MIX, LANE)
    nj = W // cb
    assert lc % tl == 0
    rb = lc // tl

    def body(hs_ref, ga_ref, yp_ref, gb_ref, bp_ref, ps_ref, o_ref):
        p = pl.program_id(2)

        @pl.when(p == 0)
        def _():
            g = ga_ref[...]
            o_ref[...] = ((hs_ref[0] + hs_ref[1]) * (g * _sigmoid(g))).astype(o_ref.dtype)

        @pl.when(p == 1)
        def _():
            g = gb_ref[...]
            yb = (yp_ref[...] + bp_ref[...]) * ps_ref[...]
            o_ref[...] = (yb * (g * _sigmoid(g))).astype(o_ref.dtype)

    vec = pl.BlockSpec((1, cb), lambda i, j, p: (0, j))
    blocks = [((2, tl, cb), F32)] + [((tl, cb), F32)] * 3 + [((tl, cb), MXU)]
    return pl.pallas_call(
        body, name=name, grid=(L // tl, nj, 2),
        in_specs=[pl.BlockSpec((2, tl, cb), lambda i, j, p: (0, i + rb, j)),
                  pl.BlockSpec((tl, cb), lambda i, j, p: (i + rb, 2 * nj + j)),
                  pl.BlockSpec((tl, cb), lambda i, j, p: (i, j)),
                  pl.BlockSpec((tl, cb), lambda i, j, p: (i + rb, 3 * nj + j)), vec, vec],
        out_specs=pl.BlockSpec((tl, cb), lambda i, j, p: (i, p * nj + j)),
        out_shape=_sds((L, 2 * W), MXU),
        compiler_params=_params(blocks, dims=("parallel", "parallel", "arbitrary")),
    )(hs, proj_all, ypre, proj_all, b_pool, pool_scale)


def _dsilu(g, sg):
    return sg * (1.0 + g * (1.0 - sg))


def _mixa_bwd(dmixed, hs, proj_all, dproj, lc, W, name):
    L = dmixed.shape[0]
    tl = _tile(L, TL, SUB16)
    cb = _tile(W, CB_MIX, LANE)
    nj = W // cb
    rb = lc // tl

    def body(dm_ref, hs_ref, ga_ref, dp_in, dya_ref, dga_ref):
        del dp_in
        g = ga_ref[...]
        sg = _sigmoid(g)
        dm = dm_ref[...]
        dya_ref[...] = dm * (g * sg)
        dga_ref[...] = (dm * (hs_ref[0] + hs_ref[1]) * _dsilu(g, sg)).astype(dga_ref.dtype)

    blocks = [((tl, cb), F32)] * 3 + [((2, tl, cb), F32), ((tl, cb), MXU)]
    return pl.pallas_call(
        body, name=name, grid=(L // tl, nj),
        in_specs=[pl.BlockSpec((tl, cb), lambda i, j: (i, j)),
                  pl.BlockSpec((2, tl, cb), lambda i, j: (0, i + rb, j)),
                  pl.BlockSpec((tl, cb), lambda i, j: (i + rb, 2 * nj + j)), ANY],
        out_specs=[pl.BlockSpec((tl, cb), lambda i, j: (i, j)),
                   pl.BlockSpec((tl, cb), lambda i, j: (i + rb, 2 * nj + j))],
        out_shape=[_sds((L, W), F32), _sds(dproj.shape, dproj.dtype)],
        input_output_aliases={3: 1},
        compiler_params=_params(blocks, dims=("parallel", "parallel")),
    )(dmixed, hs, proj_all, dproj)


def _mixb_bwd(dmixed, ypre, proj_all, b_pool, pool_scale, dproj, lc, W, name):
    L = dmixed.shape[0]
    tl = _tile(L, TL, SUB16)
    cb = _tile(W, CB_MIX, LANE)
    nj = W // cb
    rb = lc // tl

    def body(dm_ref, yp_ref, gb_ref, bp_ref, ps_ref, dp_in, dyp_ref, dgb_ref, gbp_ref, gps_ref):
        del dp_in
        i = pl.program_id(1)

        @pl.when(i == 0)
        def _():
            gbp_ref[...] = jnp.zeros_like(gbp_ref)
            gps_ref[...] = jnp.zeros_like(gps_ref)

        g = gb_ref[...]
        sg = _sigmoid(g)
        dm = dm_ref[...]
        yp = yp_ref[...] + bp_ref[...]
        ps = ps_ref[...]
        dyb = dm * (g * sg)
        dyp = dyb * ps
        dgb_ref[...] = (dm * (yp * ps) * _dsilu(g, sg)).astype(dgb_ref.dtype)
        dyp_ref[...] = dyp.astype(dyp_ref.dtype)
        gbp_ref[...] += jnp.sum(dyp, axis=0, keepdims=True)
        gps_ref[...] += jnp.sum(dyb * yp, axis=0, keepdims=True)

    vec = pl.BlockSpec((1, cb), lambda j, i: (0, j))
    blocks = [((tl, cb), F32)] * 3 + [((tl, cb), MXU)] * 2 + [((1, cb), F32)] * 4
    return pl.pallas_call(
        body, name=name, grid=(nj, L // tl),
        in_specs=[pl.BlockSpec((tl, cb), lambda j, i: (i, nj + j)),
                  pl.BlockSpec((tl, cb), lambda j, i: (i, j)),
                  pl.BlockSpec((tl, cb), lambda j, i: (i + rb, 3 * nj + j)), vec, vec, ANY],
        out_specs=[pl.BlockSpec((tl, cb), lambda j, i: (i, j)),
                   pl.BlockSpec((tl, cb), lambda j, i: (i + rb, 3 * nj + j)), vec, vec],
        out_shape=[_sds((L, W), MXU), _sds(dproj.shape, dproj.dtype), _sds((1, W), F32), _sds((1, W), F32)],
        input_output_aliases={5: 1},
        compiler_params=_params(blocks, dims=("parallel", "arbitrary")),
    )(dmixed, ypre, proj_all, b_pool, pool_scale, dproj)


def _dproj_init(n, lc, W, name):
    cb = _tile(W, CB_MIX, LANE)
    nj = W // cb

    def body(o_ref):
        o_ref[...] = jnp.zeros_like(o_ref)

    return pl.pallas_call(
        body, name=name, grid=(3 * nj,), in_specs=[],
        out_specs=pl.BlockSpec((lc, cb), lambda j: (0, nj + j)),
        out_shape=_sds((n, 4 * W), MXU),
        compiler_params=_params([((lc, cb), MXU)], dims=("parallel",)),
    )()


def _final(x2, out, tgt, gate, gfin, name):
    L, D = x2.shape
    tl = _tile(L, TL_FINAL, SUB16)

    def body(x_ref, o_ref, t_ref, gate_ref, g_ref, dout_ref, dxn_ref, loss_ref, ggf_ref, dgate_ref):
        i = pl.program_id(0)

        @pl.when(i == 0)
        def _():
            loss_ref[...] = jnp.zeros_like(loss_ref)
            ggf_ref[...] = jnp.zeros_like(ggf_ref)
            dgate_ref[...] = jnp.zeros_like(dgate_ref)

        o = o_ref[...]
        gate_v = gate_ref[...]
        gv = g_ref[...]
        xn = x_ref[...] + gate_v * o
        s = lax.rsqrt(jnp.mean(xn * xn, axis=-1, keepdims=True) + EPS)
        xh = xn * s
        err = xh * gv - t_ref[...]
        tok = jnp.mean(err * err, axis=-1, keepdims=True)
        loss_ref[...] += 0.5 * jnp.sum(tok, axis=0, keepdims=True)
        dy = err / D
        ggf_ref[...] += jnp.sum(dy * xh, axis=0, keepdims=True)
        dxh = dy * gv
        dxn = s * (dxh - xh * jnp.mean(dxh * xh, axis=-1, keepdims=True))
        dgate_ref[...] += jnp.sum(dxn * o, axis=0, keepdims=True)
        dout_ref[...] = (gate_v * dxn).astype(dout_ref.dtype)
        dxn_ref[...] = dxn

    row = pl.BlockSpec((tl, D), lambda i: (i, 0))
    vec = pl.BlockSpec((1, D), lambda i: (0, 0))
    blocks = [((tl, D), F32)] * 4 + [((tl, D), MXU)] + [((1, D), F32)] * 4
    return pl.pallas_call(
        body, name=name, grid=(L // tl,), in_specs=[row, row, row, vec, vec],
        out_specs=[row, row, pl.BlockSpec((1, 1), lambda i: (0, 0)), vec, vec],
        out_shape=[_sds((L, D), MXU), _sds((L, D), F32), _sds((1, 1), F32), _sds((1, D), F32), _sds((1, D), F32)],
        compiler_params=_params(blocks, dims=("arbitrary",)),
    )(x2, out, tgt, gate, gfin)


def _adamw_parts(w2, parts, m2, v2, name):
    R, C = w2.shape
    nh = len(parts)
    ch = C // nh
    row_bytes = ch * (7 * 4 + nh * NDEV * jnp.dtype(parts[0].dtype).itemsize)
    tr = _tile(R, max(SUB16, ADAMW_BLOCK_BYTES // row_bytes), SUB16)

    def body(w_ref, *rest):
        p_refs = rest[:nh]
        m_ref, v_ref, g_ref, d_ref, nm_ref, nv_ref = rest[nh:]
        for q in range(nh):
            @pl.when(pl.program_id(1) == q)
            def _(p_ref=p_refs[q]):
                g = p_ref[0].astype(F32)
                for p in range(1, NDEV):
                    g = g + p_ref[p].astype(F32)
                delta, nm, nv = _adam(w_ref[...], g, m_ref[...], v_ref[...])
                g_ref[...] = g
                d_ref[...] = delta
                nm_ref[...] = nm
                nv_ref[...] = nv

    blk = pl.BlockSpec((tr, ch), lambda i, h: (i, h))
    p_spec = pl.BlockSpec((NDEV, tr, ch), lambda i, h: (0, i, 0))
    blocks = [((tr, ch), F32)] * 7 + [((NDEV, tr, ch), parts[0].dtype)] * nh
    return pl.pallas_call(
        body, name=name, grid=(R // tr, nh),
        in_specs=[blk] + [p_spec] * nh + [blk, blk],
        out_specs=[blk] * 4, out_shape=[_sds((R, C), F32)] * 4,
        compiler_params=_params(blocks, dims=("parallel", "arbitrary")),
    )(w2, *parts, m2, v2)


def _small_sum(vs, ga, gc, name):
    ns, nm = vs.shape[1], ga.shape[1]

    def body(v_ref, ga_ref, gc_ref, tot_ref, gb_ref):
        tot = v_ref[0:1, :]
        gb = ga_ref[0:1, :]
        for p in range(1, NDEV):
            tot = tot + v_ref[p:p + 1, :]
            gb = gb + ga_ref[p:p + 1, :]
        for p in range(NDEV):
            gb = gb + gc_ref[p:p + 1, :]
        tot_ref[...] = tot
        gb_ref[...] = gb

    blocks = [((NDEV, ns), F32), ((NDEV, nm), F32), ((NDEV, nm), F32), ((1, ns), F32), ((1, nm), F32)]
    return pl.pallas_call(
        body, name=name, out_shape=[_sds((1, ns), F32), _sds((1, nm), F32)],
        compiler_params=_params(blocks),
    )(vs, ga, gc)


def _adamw_small(g_raw, w, m, v, lam_range, cctx_range, name):
    npk = w.shape[1]

    def body(g_ref, w_ref, m_ref, v_ref, go_ref, d_ref, nm_ref, nv_ref):
        wv = w_ref[...]
        g = g_ref[...]
        idx = lax.broadcasted_iota(jnp.int32, (1, npk), 1)
        in_lam = (idx >= lam_range[0]) & (idx < lam_range[1])
        in_cc = (idx >= cctx_range[0]) & (idx < cctx_range[1])
        sg = _sigmoid_small(wv)
        g = jnp.where(in_lam, g * (LRU_C * _sigmoid_small(-wv)), jnp.where(in_cc, g * _dsilu(wv, sg), g))
        delta, nm, nv = _adam(wv, g, m_ref[...], v_ref[...])
        go_ref[...] = g
        d_ref[...] = delta
        nm_ref[...] = nm
        nv_ref[...] = nv

    return pl.pallas_call(
        body, name=name, out_shape=[_sds((1, npk), F32)] * 4,
        compiler_params=_params([((1, npk), F32)] * 8),
    )(g_raw, w, m, v)


def _pack(pieces):
    return jnp.concatenate([p.reshape(1, -1) for p in pieces], axis=1)


def kernel(x, c, ctx, c_ctx, w_ada, b_ada, g_norm, w_in, conv_w, conv_b, lru_lambda, w_rgate, b_rgate, w_igate, b_igate, w_pool, b_pool, pool_scale, w_out, g_final, loss_target, m_c_ctx, m_w_ada, m_b_ada, m_g_norm, m_w_in, m_conv_w, m_conv_b, m_lru_lambda, m_w_rgate, m_b_rgate, m_w_igate, m_b_igate, m_w_pool, m_b_pool, m_pool_scale, m_w_out, m_g_final, v_c_ctx, v_w_ada, v_b_ada, v_g_norm, v_w_in, v_conv_w, v_conv_b, v_lru_lambda, v_w_rgate, v_b_rgate, v_w_igate, v_b_igate, v_w_pool, v_b_pool, v_pool_scale, v_w_out, v_g_final):
    L, D = x.shape[1], x.shape[2]
    lc = ctx.shape[1]
    n = lc + L
    W = conv_b.shape[1]
    heads, hd = w_rgate.shape[2], w_rgate.shape[4]
    G, pd = w_pool.shape[1], w_pool.shape[3]
    na = w_ada.shape[2]
    nb = w_in.shape[2]
    ws = W // NDEV
    me = 4 * lax.axis_index("x") + 2 * lax.axis_index("y") + lax.axis_index("c")

    nbp = nb // WIN_PARTS
    w_in_parts = [w_in[0, :, q * nbp:(q + 1) * nbp].astype(MXU) for q in range(WIN_PARTS)]
    (win_0, cw_all, lam_all, br_all, bi_all, c_all) = _all_gather(
        [w_in_parts[0], conv_w[0], lru_lambda[0], b_rgate[0], b_igate[0], c], "gather_w_in")
    win = [win_0]
    cw = cw_all.transpose(1, 0, 2).reshape(4, W)
    lam = lam_all.transpose(1, 0, 2).reshape(2, W)
    br = br_all.transpose(1, 0, 2).reshape(2, W)
    bi = bi_all.transpose(1, 0, 2).reshape(2, W)

    cc = jnp.concatenate([c_all.reshape(NDEV, D), c_ctx.reshape(1, D), jnp.zeros((NDEV - 1, D), F32)], axis=0)
    b_loc = lax.dynamic_slice(b_ada, (0, me * na), (1, na))
    mod_loc, s_all = _ada_fwd(cc, w_ada[0], b_loc, "ada_fwd")
    (mod_all,) = _all_gather([mod_loc], "gather_mod")
    gate_w = [w_rgate[0].astype(MXU), w_igate[0].astype(MXU)]
    pool_w, out_w = [w_pool[0].astype(MXU)], [w_out[0].astype(MXU)]
    tok = mod_all
    sent_win = []
    for q in range(1, WIN_PARTS):
        part = [w_in_parts[q]]
        sent_win.append(_send_start(part, _place(part, False, f"place_w_in_{q}", [tok]), "level1", f"start_w_in_{q}"))
        tok = sent_win[-1][4]
    sent_gw = _send_start(gate_w, _place(gate_w, False, "place_gate_w", [tok]), False, "start_gate_w")
    sent_pw = _send_start(pool_w, _place(pool_w, False, "place_pool_w", [sent_gw[4]]), False, "start_pool_w")
    sent_ow = _send_start(out_w, _place(out_w, False, "place_out_w", [sent_pw[4]]), False, "start_out_w")
    mod = mod_all.transpose(1, 0, 2).reshape(2 * NDEV, NDEV * na)
    mod_me = lax.dynamic_slice(mod, (me, 0), (1, 3 * D))
    shift, scale, gate = mod_me[:, :D], mod_me[:, D:2 * D], mod_me[:, 2 * D:]
    shift = _tie(shift, [sent_gw[4], sent_ow[4]], "tie_weights")
    shift_c, scale_c = mod[NDEV:NDEV + 1, :D], mod[NDEV:NDEV + 1, D:2 * D]

    x2, ctx2, tgt = x[0], ctx[0], loss_target[0]
    gfin = g_final.reshape(1, D)
    h_all = _norm_mod(x2, g_norm, shift, scale, n, lc, None, "norm_lat")
    h_all = _norm_mod(ctx2, g_norm, shift_c, scale_c, n, 0, h_all, "norm_ctx")
    proj_all = _mm_proj(h_all, win[0], 0, WIN_PARTS, None, "mm_proj_0")
    for q in range(1, WIN_PARTS):
        lands = _send_wait(sent_win[q - 1], proj_all, "level1", f"wait_w_in_{q}")
        passed = _send_start([], lands, "level2", f"pass_w_in_{q}")
        win.append(_send_wait(passed, proj_all, "level2", f"wait_pass_w_in_{q}")[0])
        proj_all = _mm_proj(h_all, win[q], q, WIN_PARTS, proj_all, f"mm_proj_{q}")
    u_all = _conv_fwd(proj_all, cw, conv_b, lc, W, "conv_fwd")
    wr_all, wi_all = _send_wait(sent_gw, u_all, False, "wait_gate_w")
    wr = wr_all.transpose(1, 2, 0, 3, 4).reshape(2, heads, hd, hd)
    wi = wi_all.transpose(1, 2, 0, 3, 4).reshape(2, heads, hd, hd)
    a_all, b_all = _gates_fwd(u_all, wr, wi, br, bi, lam, "gates_fwd")
    hs = _scan_fwd(a_all, b_all, lc, "scan_fwd")
    z = _pool_z(proj_all, lc, W, L, W, False, None, "pool_z")
    (wpool_all,) = _send_wait(sent_pw, hs, False, "wait_pool_w")
    wpool = wpool_all.transpose(1, 0, 2, 3).reshape(G, pd, pd)
    ypre = _mm_group(z, wpool, "fwd", F32, "mm_pool")
    mixed = _mix_fwd(hs, proj_all, ypre, b_pool, pool_scale, lc, "mix_fwd")
    (wout_all,) = _send_wait(sent_ow, mixed, False, "wait_out_w")
    wout = wout_all.reshape(2 * W, D)
    out = _mm_plain(mixed, wout, NN, F32, "mm_out")
    d_out, dxn, loss_p, ggf, dgate = _final(x2, out, tgt, gate, gfin, "final")

    dmixed = _mm_plain(d_out, wout, NT, F32, "mm_dmixed")
    gwout = _mm_plain(mixed, d_out, TN_DIMS, MXU, "mm_gwout")
    ex_o = [gwout.reshape(NDEV, 2 * W // NDEV, D)]
    sent_o = _send_start(ex_o, _place(ex_o, True, "place_gwout"), True, "start_gwout")
    dproj = _dproj_init(n, lc, W, "dproj_init")
    dya, dproj = _mixa_bwd(dmixed, hs, proj_all, dproj, lc, W, "mixa_bwd")
    dypre, dproj, gbp, gps = _mixb_bwd(dmixed, ypre, proj_all, _tie(b_pool, [sent_o[4]], "tie_gwout"), pool_scale,
                                       dproj, lc, W, "mixb_bwd")
    dz = _mm_group(dypre, wpool, "bwd", F32, "mm_dz")
    gwpool = _mm_group(z, dypre, "wgrad", MXU, "mm_gwpool")
    dproj = _pool_z(dz, lc, W, L, W, True, dproj, "pool_z_bwd")
    da, db = _scan_bwd(a_all, hs, dya, lc, "scan_bwd")
    du, gwr, gwi, gbr, gbi, gcl = _gates_bwd(u_all, da, db, wr, wi, br, bi, lam, "gates_bwd")
    ex_s = [gwpool.reshape(G, NDEV, pd // NDEV, pd).transpose(1, 0, 2, 3),
            gwr.reshape(2, heads, NDEV, hd // NDEV, hd).transpose(2, 0, 1, 3, 4),
            gwi.reshape(2, heads, NDEV, hd // NDEV, hd).transpose(2, 0, 1, 3, 4)]
    sent_s = _send_start(ex_s, _place(ex_s, True, "place_gsmall"), True, "start_gsmall")
    dproj, gcw, gcb = _conv_bwd(du, proj_all, _tie(cw, [sent_s[4]], "tie_gsmall"), dproj, lc, W, "conv_bwd")
    h_t = _transpose(h_all, "transpose_h")
    sent_i, tok = [], None
    for q in range(GWIN_PARTS):
        part = _mm_gwin(h_t, dproj, nb, q, GWIN_PARTS, f"mm_gwin_{q}", dep=tok)
        part = pltpu.with_memory_space_constraint(part, pltpu.HBM)
        sent_i.append(_send_start([part], _place([part], True, f"place_gwin_{q}"), True, f"start_gwin_{q}"))
        tok = sent_i[-1][4]
    dh_all = _mm_dh(dproj, win, "mm_dh", tok)
    grad_x, dshift, dscale, ggn = _norm_bwd(x2, dh_all, lc, g_norm, scale, dxn, jnp.zeros((1, D), F32), "norm_bwd_lat")
    _, dshift_c, dscale_c, ggn = _norm_bwd(ctx2, dh_all, 0, g_norm, scale_c, None, ggn, "norm_bwd_ctx")

    dmod_me = jnp.concatenate([dshift, dscale, dgate], axis=1)
    dmod_c = jnp.concatenate([dshift_c, dscale_c, jnp.zeros((1, D), F32)], axis=1)
    smalls = [ggf, ggn, gcw, gcb, gcl, gbr, gbi, gbp, gps, jnp.pad(loss_p, ((0, 0), (0, LANE - 1)))]
    sizes = [s.size for s in smalls]
    small_all, dmod_all, dmodc_all = _all_gather([_pack(smalls), dmod_me, dmod_c], "gather_small")
    ga = lax.dynamic_slice(dmod_all.reshape(NDEV, 3 * D), (0, me * na), (NDEV, na))
    gc = lax.dynamic_slice(dmodc_all.reshape(NDEV, 3 * D), (0, me * na), (NDEV, na))
    g_wada, d_wada, nm_wada, nv_wada, pc = _ada_bwd(s_all, ga, gc, w_ada[0], m_w_ada[0], v_w_ada[0], "ada_bwd")
    (pc_all,) = _all_gather([pc[0:1]], "gather_cctx")
    tot, gb_ada = _small_sum(
        jnp.concatenate([small_all.reshape(NDEV, -1), pc_all.reshape(NDEV, D)], axis=1),
        dmod_all.reshape(NDEV, 3 * D), dmodc_all.reshape(NDEV, 3 * D), "small_sum")
    offs = [0]
    for s in sizes + [D]:
        offs.append(offs[-1] + s)
    t_ggf, t_ggn, t_gcw, t_gcb, t_gcl, t_gbr, t_gbi, t_gbp, t_gps, t_loss, t_pc = [
        tot[:, offs[i]:offs[i + 1]] for i in range(len(offs) - 1)]

    def shard(t, rows):
        return lax.dynamic_slice(t.reshape(rows, W), (0, me * ws), (rows, ws))

    def big(wv, parts, mv, vv, name):
        shp = wv.shape
        C = shp[-1]
        if not isinstance(parts, list):
            parts = [parts]
        parts = [p.reshape(NDEV, -1, C // len(parts)) for p in parts]
        outs = _adamw_parts(wv.reshape(-1, C), parts, mv.reshape(-1, C), vv.reshape(-1, C), name)
        return [o.reshape(shp) for o in outs]

    (recv_o,) = _send_wait(sent_o, tot, True, "wait_gwout")
    recv_p, recv_r, recv_i = _send_wait(sent_s, tot, True, "wait_gsmall")
    r_wout = big(w_out, recv_o, m_w_out, v_w_out, "adamw_w_out")
    r_wpool = big(w_pool, recv_p, m_w_pool, v_w_pool, "adamw_w_pool")
    r_wr = big(w_rgate, recv_r, m_w_rgate, v_w_rgate, "adamw_w_rgate")
    r_wi = big(w_igate, recv_i, m_w_igate, v_w_igate, "adamw_w_igate")
    r_wada = [o.reshape(w_ada.shape) for o in (g_wada, d_wada, nm_wada, nv_wada)]

    names = ["c_ctx", "b_ada", "g_norm", "conv_w", "conv_b", "lru_lambda", "b_rgate", "b_igate", "b_pool",
             "pool_scale", "g_final"]
    sw = [c_ctx, b_ada, g_norm, conv_w, conv_b, lru_lambda, b_rgate, b_igate, b_pool, pool_scale, g_final]
    sm = [m_c_ctx, m_b_ada, m_g_norm, m_conv_w, m_conv_b, m_lru_lambda, m_b_rgate, m_b_igate, m_b_pool,
          m_pool_scale, m_g_final]
    sv = [v_c_ctx, v_b_ada, v_g_norm, v_conv_w, v_conv_b, v_lru_lambda, v_b_rgate, v_b_igate, v_b_pool,
          v_pool_scale, v_g_final]
    sg = [t_pc, gb_ada, t_ggn, shard(t_gcw, 4), t_gcb, shard(t_gcl, 2), shard(t_gbr, 2), shard(t_gbi, 2), t_gbp,
          t_gps, t_ggf]
    poffs = [0]
    for wv in sw:
        poffs.append(poffs[-1] + wv.size)
    lam_range = (poffs[5], poffs[6])
    cctx_range = (poffs[0], poffs[1])
    small_out = _adamw_small(_pack(sg), _pack(sw), _pack(sm), _pack(sv), lam_range, cctx_range, "adamw_small")
    recv_w = [_send_wait(sent_i[q], small_out[0], True, f"wait_gwin_{q}")[0] for q in range(GWIN_PARTS)]
    r_win = big(w_in, recv_w, m_w_in, v_w_in, "adamw_w_in")
    r_small = {}
    for i, nm in enumerate(names):
        r_small[nm] = [o[:, poffs[i]:poffs[i + 1]].reshape(sw[i].shape) for o in small_out]

    res = dict(r_small)
    res.update(w_ada=r_wada, w_in=r_win, w_rgate=r_wr, w_igate=r_wi, w_pool=r_wpool, w_out=r_wout)
    order = ["c_ctx", "w_ada", "b_ada", "g_norm", "w_in", "conv_w", "conv_b", "lru_lambda", "w_rgate", "b_rgate",
             "w_igate", "b_igate", "w_pool", "b_pool", "pool_scale", "w_out", "g_final"]
    loss = t_loss[0, 0]
    outs = [loss, grad_x.reshape(x.shape)]
    for q in range(4):
        outs += [res[nm][q] for nm in order]
    return tuple(outs)
```

```python
import functools

import jax
import jax.numpy as jnp
from jax import lax
from jax.experimental import pallas as pl
from jax.experimental.pallas import tpu as pltpu

NDEV = 8
GRID_W = 64
POOL_WINDOWS = (2, 4, 8, 16)
LRU_C = 8.0
EPS = 1e-6
ADAM_LR = 0.001
ADAM_B1 = 0.9
ADAM_B2 = 0.999
ADAM_EPS = 1e-08
ADAM_WD = 0.01
ADAM_STEP = 10

F32 = jnp.float32
MXU = jnp.bfloat16

VMEM_BYTES = 64 * 1024 * 1024
VMEM_SLACK = 8 * 1024 * 1024
SUB = 8
SUB16 = 16
LANE = 128

TM = 1152
TN = 1024
TK = 2048
TL = 256
TL_FINAL = 128
TL_GATES = 1088
CB_POOL = 1024
CB_SEQ = 256
CB_SCAN = 2048
CB_MIX = 2048
TR_CONV = 576
GWIN_PARTS = 4
WIN_PARTS = 4

MESH_ID = pl.DeviceIdType.MESH


def _tile(n, pref, align):
    if n <= pref:
        return n
    for t in range(pref - pref % align, 0, -align):
        if n % t == 0:
            return t
    return n


def _nbytes(shape, dtype):
    n = 1
    for s in shape:
        if s is not None:
            n *= s
    return n * jnp.dtype(dtype).itemsize


def _params(blocks, scratch=(), dims=None):
    need = 2 * sum(_nbytes(s, d) for s, d in blocks) + sum(_nbytes(s, d) for s, d in scratch) + VMEM_SLACK
    kw = dict(vmem_limit_bytes=int(min(max(need, 2 * VMEM_SLACK), VMEM_BYTES - VMEM_SLACK // 2)))
    if dims is not None:
        kw["dimension_semantics"] = dims
    return pltpu.CompilerParams(**kw)


def _sds(shape, dtype):
    return jax.ShapeDtypeStruct(tuple(shape), dtype)


ANY = pl.BlockSpec(memory_space=pl.ANY)


def _ids():
    return lax.axis_index("x"), lax.axis_index("y"), lax.axis_index("c")


def _sigmoid(v):
    return 0.5 * jnp.tanh(0.5 * v) + 0.5


def _sigmoid_small(v):
    return jax.nn.sigmoid(v)


def _softplus(v):
    return jnp.maximum(v, 0.0) + jnp.log1p(jnp.exp(-jnp.abs(v)))


def _all_gather(xs, name):
    n = len(xs)

    def body(*refs):
        x_refs, o_refs = refs[:n], refs[n:2 * n]
        send_sems, recv_sems, local_sems = refs[2 * n:]
        x, y, c = _ids()
        me, sibling = (x, y, c), (x, y, 1 - c)
        chips = [(1 - x, y), (x, 1 - y), (1 - x, 1 - y)]

        def slot(a, p):
            return o_refs[a].at[4 * p[0] + 2 * p[1] + p[2]]

        def copy(a, k, block, to, src=None):
            return pltpu.make_async_remote_copy(
                src_ref=slot(a, block) if src is None else src, dst_ref=slot(a, block),
                send_sem=send_sems.at[7 * a + k], recv_sem=recv_sems.at[7 * a + k],
                device_id=to, device_id_type=MESH_ID)

        mine, first, passed = [], [], []
        for a in range(n):
            m = pltpu.make_async_copy(x_refs[a], slot(a, me), local_sems.at[a])
            m.start()
            mine.append(m)
            f = [copy(a, 0, me, sibling, src=x_refs[a])]
            f += [copy(a, 1 + j, me, (*chip, c), src=x_refs[a]) for j, chip in enumerate(chips)]
            for cp in f:
                cp.start()
            first += f
        for a in range(n):
            for j, chip in enumerate(chips):
                copy(a, 1 + j, (*chip, c), me).wait_recv()
                p = copy(a, 4 + j, (*chip, c), sibling)
                p.start()
                passed.append(p)
        for a in range(n):
            copy(a, 0, sibling, me).wait_recv()
            for j, chip in enumerate(chips):
                copy(a, 4 + j, (*chip, 1 - c), me).wait_recv()
        for cp in first + passed:
            cp.wait_send()
        for m in mine:
            m.wait()

    return pl.pallas_call(
        body, name=name,
        out_shape=[_sds((NDEV,) + v.shape, v.dtype) for v in xs],
        in_specs=[ANY] * n, out_specs=[ANY] * n,
        scratch_shapes=[pltpu.SemaphoreType.DMA((7 * n,)), pltpu.SemaphoreType.DMA((7 * n,)),
                        pltpu.SemaphoreType.DMA((n,))],
    )(*xs)


HBM = pl.BlockSpec(memory_space=pltpu.HBM)
SEM = pl.BlockSpec(memory_space=pltpu.SEMAPHORE)
EFFECT = pltpu.SideEffectType.DATAFLOW_SIDE_EFFECTING


def _peers():
    x, y, c = _ids()
    out = []
    for k in range(1, NDEV):
        px = 1 - x if k & 4 else x
        py = 1 - y if k & 2 else y
        pc = 1 - c if k & 1 else c
        out.append(((px, py, pc), 4 * px + 2 * py + pc))
    return out, 4 * x + 2 * y + c


def _tie(v, deps, name):
    def body(v_ref, *rest):
        rest[-1][...] = v_ref[...]

    vmem = pl.BlockSpec(memory_space=pltpu.VMEM)
    return pl.pallas_call(
        body, name=name, out_shape=_sds(v.shape, v.dtype), in_specs=[vmem] + [ANY] * len(deps), out_specs=vmem,
    )(v, *deps)


def _place(srcs, from_slot, name, deps=()):
    n = len(srcs)
    blks = [v.shape[1:] if from_slot else v.shape for v in srcs]

    nd = len(deps)

    def body(*refs):
        s_refs, l_refs = refs[:n], refs[n + nd:2 * n + nd]
        bufs, sems = refs[2 * n + nd:3 * n + nd], refs[3 * n + nd]
        x, y, c = _ids()
        me = 4 * x + 2 * y + c
        ins = [pltpu.make_async_copy(s_refs[a].at[me] if from_slot else s_refs[a], bufs[a], sems.at[a])
               for a in range(n)]
        outs = [pltpu.make_async_copy(bufs[a], l_refs[a].at[me], sems.at[n + a]) for a in range(n)]
        for cp in ins:
            cp.start()
        for a in range(n):
            ins[a].wait()
            outs[a].start()
        for cp in outs:
            cp.wait()

    scratch = [(b, v.dtype) for b, v in zip(blks, srcs)]
    return pl.pallas_call(
        body, name=name, out_shape=[_sds((NDEV,) + b, v.dtype) for b, v in zip(blks, srcs)],
        in_specs=[ANY] * (n + nd), out_specs=[ANY] * n,
        scratch_shapes=[pltpu.VMEM(b, d) for b, d in scratch] + [pltpu.SemaphoreType.DMA((2 * n,))],
        compiler_params=_params([], scratch),
    )(*srcs, *deps)


SEND_PEERS = {True: 7, False: 7, "level1": 4, "level2": 3}


def _send_copies(s_refs, l_refs, ssem, rsem, mode, receiving):
    peers, me = _peers()
    x, y, c = _ids()
    sibling = (x, y, 1 - c)
    chips = [(1 - x, y), (x, 1 - y), (1 - x, 1 - y)]
    npeer = SEND_PEERS[mode]
    out = []
    for a in range(len(l_refs)):
        if mode == "level2":
            for k, (px, py) in enumerate(chips):
                slot = 4 * px + 2 * py + (1 - c if receiving else c)
                out.append(pltpu.make_async_remote_copy(
                    src_ref=l_refs[a].at[slot], dst_ref=l_refs[a].at[slot], send_sem=ssem.at[npeer * a + k],
                    recv_sem=rsem.at[npeer * a + k], device_id=sibling, device_id_type=MESH_ID))
            continue
        targets = peers
        if mode == "level1":
            targets = [(sibling, 4 * x + 2 * y + 1 - c)] + [((px, py, c), 4 * px + 2 * py + c) for px, py in chips]
        for k, (dev, idx) in enumerate(targets):
            out.append(pltpu.make_async_remote_copy(
                src_ref=s_refs[a].at[idx] if mode is True else s_refs[a],
                dst_ref=l_refs[a].at[idx if receiving else me],
                send_sem=ssem.at[npeer * a + k], recv_sem=rsem.at[npeer * a + k], device_id=dev, device_id_type=MESH_ID))
    return out


def _send_start(srcs, lands, mode, name):
    ns, n = len(srcs), len(lands)
    nsem = SEND_PEERS[mode] * n

    def body(*refs):
        s_refs, l_refs = refs[:ns], refs[ns:ns + n]
        ssem, rsem = refs[ns + n], refs[ns + n + 1]
        token = refs[-1]
        for send in _send_copies(s_refs, l_refs, ssem, rsem, mode, False):
            send.start()
        token[...] = jnp.zeros_like(token)

    bufs = list(srcs) + list(lands)
    outs = pl.pallas_call(
        body, name=name,
        out_shape=[pltpu.SemaphoreType.DMA((nsem,)), pltpu.SemaphoreType.DMA((nsem,))]
        + [pltpu.HBM(v.shape, v.dtype) for v in bufs] + [_sds((SUB, LANE), F32)],
        in_specs=[HBM] * (ns + n), out_specs=[SEM, SEM] + [HBM] * (ns + n) + [pl.BlockSpec(memory_space=pltpu.VMEM)],
        input_output_aliases={i: 2 + i for i in range(ns + n)},
        compiler_params=pltpu.CompilerParams(has_side_effects=EFFECT),
    )(*[pltpu.with_memory_space_constraint(v, pltpu.HBM) for v in bufs])
    return outs[0], outs[1], list(outs[2:2 + ns]), list(outs[2 + ns:2 + ns + n]), outs[-1]


def _send_wait(started, after, mode, name):
    ssem, rsem, srcs, lands, _ = started
    ns, n = len(srcs), len(lands)

    def body(*refs):
        s_refs, l_refs = refs[:ns], refs[ns:ns + n]
        ssem_ref, rsem_ref = refs[ns + n], refs[ns + n + 1]
        for recv in _send_copies(s_refs, l_refs, ssem_ref, rsem_ref, mode, True):
            recv.wait_send()
            recv.wait_recv()

    bufs = list(srcs) + list(lands)
    outs = pl.pallas_call(
        body, name=name, out_shape=[pltpu.HBM(v.shape, v.dtype) for v in bufs],
        in_specs=[HBM] * (ns + n) + [SEM, SEM, ANY], out_specs=[HBM] * (ns + n),
        input_output_aliases={i: i for i in range(ns + n)},
        compiler_params=pltpu.CompilerParams(has_side_effects=EFFECT),
    )(*bufs, ssem, rsem, after)
    return list(outs[ns:])


NN = (((1,), (0,)), ((), ()))
NT = (((1,), (1,)), ((), ()))
TN_DIMS = (((0,), (0,)), ((), ()))


def _mm(a, b, *, grid, a_spec, b_spec, o_spec, out_shape, acc_shape, dims, name, dep=None, fill=None):
    k_axis = len(grid) - 1
    nk = grid[k_axis]
    extra = [v for v in (dep, fill) if v is not None]
    aliases = {} if fill is None else {1 + len(extra): 0}

    def body(a_ref, b_ref, *rest):
        o_ref, acc_ref = rest[-2], rest[-1]
        k = pl.program_id(k_axis)

        def prod():
            return lax.dot_general(a_ref[...], b_ref[...], dims, preferred_element_type=F32)

        if nk == 1:
            o_ref[...] = prod().astype(o_ref.dtype)
            return

        @pl.when(k == 0)
        def _():
            acc_ref[...] = prod()

        if nk > 2:
            @pl.when((k > 0) & (k < nk - 1))
            def _():
                acc_ref[...] += prod()

        @pl.when(k == nk - 1)
        def _():
            o_ref[...] = (acc_ref[...] + prod()).astype(o_ref.dtype)

    blocks = [(a_spec.block_shape, a.dtype), (b_spec.block_shape, b.dtype), (o_spec.block_shape, out_shape.dtype)]
    return pl.pallas_call(
        body, name=name, grid=grid, in_specs=[a_spec, b_spec] + [ANY] * len(extra), out_specs=o_spec,
        out_shape=out_shape, scratch_shapes=[pltpu.VMEM(acc_shape, F32)], input_output_aliases=aliases,
        compiler_params=_params(blocks, [(acc_shape, F32)], ("parallel",) * k_axis + ("arbitrary",)),
    )(a, b, *extra)


def _mm_plain(a, b, dims, out_dtype, name):
    if dims == TN_DIMS:
        (K, M), N = a.shape, b.shape[1]
    elif dims == NT:
        (M, K), N = a.shape, b.shape[0]
    else:
        (M, K), N = a.shape, b.shape[1]
    tm, tn = _tile(M, TM, LANE), _tile(N, TN, LANE)
    tk = _tile(K, TK, LANE if dims != TN_DIMS else SUB16)
    if dims == TN_DIMS:
        a_spec = pl.BlockSpec((tk, tm), lambda i, j, k: (k, i))
    else:
        a_spec = pl.BlockSpec((tm, tk), lambda i, j, k: (i, k))
    if dims == NT:
        b_spec = pl.BlockSpec((tn, tk), lambda i, j, k: (j, k))
    else:
        b_spec = pl.BlockSpec((tk, tn), lambda i, j, k: (k, j))
    return _mm(a, b, grid=(M // tm, N // tn, K // tk), a_spec=a_spec, b_spec=b_spec,
               o_spec=pl.BlockSpec((tm, tn), lambda i, j, k: (i, j)),
               out_shape=_sds((M, N), out_dtype), acc_shape=(tm, tn), dims=dims, name=name)


def _mm_proj(h_all, win_q, q, nparts, fill, name):
    n, D = h_all.shape
    nbp = win_q.shape[2]
    nb = nbp * nparts
    tm, tn, tk = _tile(n, TM, SUB16), _tile(nbp, TN, LANE), D
    nbn = nbp // tn
    return _mm(h_all, win_q, grid=(n // tm, NDEV * nbn, D // tk),
               a_spec=pl.BlockSpec((tm, tk), lambda i, j, k: (i, k)),
               b_spec=pl.BlockSpec((None, tk, tn), lambda i, j, k: (j // nbn, k, j % nbn)),
               o_spec=pl.BlockSpec((tm, tn), lambda i, j, k: (i, (j // nbn) * (nb // tn) + q * nbn + j % nbn)),
               out_shape=_sds((n, NDEV * nb), F32), acc_shape=(tm, tn), dims=NN, name=name, fill=fill)


def _mm_dh(dproj, wins, name, dep):
    nparts = len(wins)
    n = dproj.shape[0]
    _, D, nbp = wins[0].shape
    nb = nbp * nparts
    tm, tn = _tile(n, TM, SUB16), _tile(D, TN, LANE)

    def body(a_ref, *rest):
        b_refs, o_ref, acc_ref = rest[:nparts], rest[-2], rest[-1]
        k = pl.program_id(2)

        def prod():
            out = None
            for q in range(nparts):
                d = lax.dot_general(a_ref[:, q * nbp:(q + 1) * nbp], b_refs[q][...], NT, preferred_element_type=F32)
                out = d if out is None else out + d
            return out

        @pl.when(k == 0)
        def _():
            acc_ref[...] = prod()

        @pl.when((k > 0) & (k < NDEV - 1))
        def _():
            acc_ref[...] += prod()

        @pl.when(k == NDEV - 1)
        def _():
            o_ref[...] = acc_ref[...] + prod()

    blocks = [((tm, nb), dproj.dtype)] + [((tn, nbp), wins[0].dtype)] * nparts + [((tm, tn), F32)]
    return pl.pallas_call(
        body, name=name, grid=(n // tm, D // tn, NDEV),
        in_specs=[pl.BlockSpec((tm, nb), lambda i, j, k: (i, k))]
        + [pl.BlockSpec((None, tn, nbp), lambda i, j, k: (k, j, 0))] * nparts + [ANY],
        out_specs=pl.BlockSpec((tm, tn), lambda i, j, k: (i, j)), out_shape=_sds((n, D), F32),
        scratch_shapes=[pltpu.VMEM((tm, tn), F32)],
        compiler_params=_params(blocks, [((tm, tn), F32)], ("parallel", "parallel", "arbitrary")),
    )(dproj, *wins, dep)


def _transpose(x, name):
    R, C = x.shape
    tr, tc = _tile(R, TL, LANE), _tile(C, 2 * TL, LANE)

    def body(x_ref, o_ref):
        o_ref[...] = x_ref[...].T

    return pl.pallas_call(
        body, name=name, grid=(R // tr, C // tc),
        in_specs=[pl.BlockSpec((tr, tc), lambda i, j: (i, j))],
        out_specs=pl.BlockSpec((tc, tr), lambda i, j: (j, i)),
        out_shape=_sds((C, R), x.dtype),
        compiler_params=_params([((tr, tc), x.dtype)] * 2, dims=("parallel", "parallel")),
    )(x)


def _mm_gwin(h_t, dproj, nb, part, nparts, name, dep=None):
    D, n = h_t.shape
    nbp = nb // nparts
    tm, tn, tk = _tile(D, TM, LANE), _tile(nbp, TN, LANE), n
    nbn = nbp // tn
    return _mm(h_t, dproj, grid=(D // tm, NDEV * nbn, n // tk),
               a_spec=pl.BlockSpec((tm, tk), lambda i, j, k: (i, k)),
               b_spec=pl.BlockSpec((tk, tn), lambda i, j, k: (k, (j // nbn) * (nb // tn) + part * nbn + j % nbn)),
               o_spec=pl.BlockSpec((None, tm, tn), lambda i, j, k: (j // nbn, i, j % nbn)),
               out_shape=_sds((NDEV, D, nbp), MXU), acc_shape=(tm, tn), dims=NN, name=name, dep=dep)


def _mm_group(a, b, mode, out_dtype, name):
    if mode == "wgrad":
        L, W = a.shape
        G = len(POOL_WINDOWS)
        pd = W // G
        tm, tn, tk = _tile(pd, TM, LANE), _tile(pd, TN, LANE), _tile(L, TK, SUB16)
        nm, nn = pd // tm, pd // tn
        return _mm(a, b, grid=(G, nm, nn, L // tk),
                   a_spec=pl.BlockSpec((tk, tm), lambda g, i, j, k: (k, g * nm + i)),
                   b_spec=pl.BlockSpec((tk, tn), lambda g, i, j, k: (k, g * nn + j)),
                   o_spec=pl.BlockSpec((None, tm, tn), lambda g, i, j, k: (g, i, j)),
                   out_shape=_sds((G, pd, pd), out_dtype), acc_shape=(tm, tn), dims=TN_DIMS, name=name)
    L, W = a.shape
    G, pd, _ = b.shape
    tm, tn, tk = _tile(L, TM, SUB16), _tile(pd, TN, LANE), _tile(pd, TK, LANE)
    nn, nk = pd // tn, pd // tk
    if mode == "fwd":
        b_spec = pl.BlockSpec((None, tk, tn), lambda g, i, j, k: (g, k, j))
        dims = NN
    else:
        b_spec = pl.BlockSpec((None, tn, tk), lambda g, i, j, k: (g, j, k))
        dims = NT
    return _mm(a, b, grid=(G, L // tm, nn, nk),
               a_spec=pl.BlockSpec((tm, tk), lambda g, i, j, k: (i, g * nk + k)),
               b_spec=b_spec,
               o_spec=pl.BlockSpec((tm, tn), lambda g, i, j, k: (i, g * nn + j)),
               out_shape=_sds((L, W), out_dtype), acc_shape=(tm, tn), dims=dims, name=name)


def _ada_fwd(cc, w_loc, b_loc, name):
    R, D = cc.shape
    na = w_loc.shape[1]
    tk = _tile(D, 512, LANE)

    def body(c_ref, w_ref, b_ref, mod_ref, s_ref):
        k = pl.program_id(0)
        cv = c_ref[...]
        s = cv * _sigmoid_small(cv)
        s_ref[...] = s

        @pl.when(k == 0)
        def _():
            mod_ref[...] = jnp.broadcast_to(b_ref[...], mod_ref.shape)

        mod_ref[...] += lax.dot_general(s.astype(MXU), w_ref[...].astype(MXU), NN, preferred_element_type=F32)

    blocks = [((R, tk), F32), ((tk, na), F32), ((1, na), F32), ((R, na), F32), ((R, tk), F32)]
    return pl.pallas_call(
        body, name=name, grid=(D // tk,),
        in_specs=[pl.BlockSpec((R, tk), lambda k: (0, k)), pl.BlockSpec((tk, na), lambda k: (k, 0)),
                  pl.BlockSpec((1, na), lambda k: (0, 0))],
        out_specs=[pl.BlockSpec((R, na), lambda k: (0, 0)), pl.BlockSpec((R, tk), lambda k: (0, k))],
        out_shape=[_sds((R, na), F32), _sds((R, D), F32)],
        compiler_params=_params(blocks, dims=("arbitrary",)),
    )(cc, w_loc, b_loc)


def _adam(w, g, m, v):
    m = ADAM_B1 * m + (1.0 - ADAM_B1) * g
    v = ADAM_B2 * v + (1.0 - ADAM_B2) * (g * g)
    m_hat = m / (1.0 - ADAM_B1 ** ADAM_STEP)
    v_hat = v / (1.0 - ADAM_B2 ** ADAM_STEP)
    delta = -ADAM_LR * (m_hat / (jnp.sqrt(v_hat) + ADAM_EPS) + ADAM_WD * w)
    return delta, m, v


def _ada_bwd(s_all, ga, gc, w_loc, m_loc, v_loc, name):
    D, na = w_loc.shape
    tr = _tile(D, 256, LANE)

    def body(s_ref, ga_ref, gc_ref, w_ref, m_ref, v_ref, g_ref, d_ref, nm_ref, nv_ref, pc_ref):
        dmc = gc_ref[0:1, :]
        for p in range(1, NDEV):
            dmc = dmc + gc_ref[p:p + 1, :]
        rows = lax.broadcasted_iota(jnp.int32, (NDEV, na), 0)
        dmc8 = jnp.where(rows == 0, jnp.broadcast_to(dmc, (NDEV, na)), 0.0)
        dm = jnp.concatenate([ga_ref[...], dmc8], axis=0).astype(MXU)
        dmc16 = jnp.concatenate([dmc8, jnp.zeros_like(dmc8)], axis=0).astype(MXU)
        w = w_ref[...]
        g = lax.dot_general(s_ref[...].astype(MXU), dm, TN_DIMS, preferred_element_type=F32)
        pc_ref[...] = lax.dot_general(dmc16, w.astype(MXU), NT, preferred_element_type=F32)
        delta, nm, nv = _adam(w, g, m_ref[...], v_ref[...])
        g_ref[...] = g
        d_ref[...] = delta
        nm_ref[...] = nm
        nv_ref[...] = nv

    big = pl.BlockSpec((tr, na), lambda i: (i, 0))
    full = pl.BlockSpec((NDEV, na), lambda i: (0, 0))
    srow = pl.BlockSpec((2 * NDEV, tr), lambda i: (0, i))
    blocks = [((2 * NDEV, tr), F32)] * 2 + [((NDEV, na), F32)] * 2 + [((tr, na), F32)] * 7
    return pl.pallas_call(
        body, name=name, grid=(D // tr,),
        in_specs=[srow, full, full, big, big, big],
        out_specs=[big, big, big, big, srow],
        out_shape=[_sds((D, na), F32)] * 4 + [_sds((2 * NDEV, D), F32)],
        compiler_params=_params(blocks, dims=("parallel",)),
    )(s_all, ga, gc, w_loc, m_loc, v_loc)


def _norm_mod(x2, g, shift, scale, n, row0, h_prev, name):
    R, D = x2.shape
    tl = _tile(R, TL, SUB16)
    assert row0 % tl == 0
    b0 = row0 // tl

    def body(x_ref, g_ref, sh_ref, sc_ref, *rest):
        o_ref = rest[-1]
        xv = x_ref[...]
        s = lax.rsqrt(jnp.mean(xv * xv, axis=-1, keepdims=True) + EPS)
        nrm = xv * s * g_ref[...]
        o_ref[...] = (nrm * (1.0 + sc_ref[...]) + sh_ref[...]).astype(o_ref.dtype)

    vec = pl.BlockSpec((1, D), lambda i: (0, 0))
    in_specs = [pl.BlockSpec((tl, D), lambda i: (i, 0)), vec, vec, vec]
    args = [x2, g, shift, scale]
    aliases = {}
    if h_prev is not None:
        in_specs.append(ANY)
        args.append(h_prev)
        aliases = {4: 0}
    blocks = [((tl, D), F32), ((tl, D), MXU)] + [((1, D), F32)] * 3
    return pl.pallas_call(
        body, name=name, grid=(R // tl,), in_specs=in_specs,
        out_specs=pl.BlockSpec((tl, D), lambda i: (i + b0, 0)),
        out_shape=_sds((n, D), MXU), input_output_aliases=aliases,
        compiler_params=_params(blocks, dims=("parallel",)),
    )(*args)


def _norm_bwd(x2, dh_all, row0, g, scale, dxn, ggn0, name):
    R, D = x2.shape
    tl = _tile(R, TL_FINAL, SUB)
    assert row0 % tl == 0
    b0 = row0 // tl
    with_x = dxn is not None

    def body(*refs):
        if with_x:
            x_ref, dh_ref, g_ref, sc_ref, gg0_ref, dxn_ref, gx_ref, dsh_ref, dsc_ref, gg_ref = refs
        else:
            x_ref, dh_ref, g_ref, sc_ref, gg0_ref, dsh_ref, dsc_ref, gg_ref = refs
        i = pl.program_id(0)

        @pl.when(i == 0)
        def _():
            dsh_ref[...] = jnp.zeros_like(dsh_ref)
            dsc_ref[...] = jnp.zeros_like(dsc_ref)
            gg_ref[...] = gg0_ref[...]

        xv = x_ref[...]
        dh = dh_ref[...]
        gv = g_ref[...]
        s = lax.rsqrt(jnp.mean(xv * xv, axis=-1, keepdims=True) + EPS)
        xh = xv * s
        dsh_ref[...] += jnp.sum(dh, axis=0, keepdims=True)
        dsc_ref[...] += jnp.sum(dh * (xh * gv), axis=0, keepdims=True)
        dn = dh * (1.0 + sc_ref[...])
        gg_ref[...] += jnp.sum(dn * xh, axis=0, keepdims=True)
        if with_x:
            dxh = dn * gv
            dx = s * (dxh - xh * jnp.mean(dxh * xh, axis=-1, keepdims=True))
            gx_ref[...] = dx + dxn_ref[...]

    vec = pl.BlockSpec((1, D), lambda i: (0, 0))
    row = pl.BlockSpec((tl, D), lambda i: (i, 0))
    in_specs = [row, pl.BlockSpec((tl, D), lambda i: (i + b0, 0)), vec, vec, vec]
    args = [x2, dh_all, g, scale, ggn0]
    out_specs = [vec, vec, vec]
    out_shape = [_sds((1, D), F32)] * 3
    if with_x:
        in_specs.append(row)
        args.append(dxn)
        out_specs = [row] + out_specs
        out_shape = [_sds((R, D), F32)] + out_shape
    blocks = [((tl, D), F32)] * (4 if with_x else 2) + [((1, D), F32)] * 6
    outs = pl.pallas_call(
        body, name=name, grid=(R // tl,), in_specs=in_specs, out_specs=out_specs, out_shape=out_shape,
        compiler_params=_params(blocks, dims=("arbitrary",)),
    )(*args)
    return tuple(outs) if with_x else (None,) + tuple(outs)


def _tap_valid(t, o, lc, n):
    tt = t + o
    in_ctx = t < lc
    return (tt >= jnp.where(in_ctx, 0, lc)) & (tt < jnp.where(in_ctx, lc, n))


def _conv_fwd(proj_all, cw, cb, lc, W, name):
    n = proj_all.shape[0]
    cbk = _tile(W, CB_SEQ, LANE)
    tr = _tile(n, TR_CONV, SUB16)
    ext = tr + 2 * SUB

    def body(x_ref, w_ref, b_ref, u_ref, xp_ref):
        xp_ref[0:SUB, :] = jnp.zeros((SUB, cbk), F32)
        xp_ref[n + SUB:n + 2 * SUB, :] = jnp.zeros((SUB, cbk), F32)
        xp_ref[SUB:n + SUB, :] = x_ref[...]
        w = w_ref[...]
        bias = b_ref[...]

        def chunk(ci, carry):
            r0 = pl.multiple_of(ci * tr, SUB16)
            xe = xp_ref[pl.ds(r0, ext), :]
            t = r0 + lax.broadcasted_iota(jnp.int32, (tr, cbk), 0)
            acc = jnp.broadcast_to(bias, (tr, cbk))
            for k in range(4):
                o = k - 1
                sh = xe if o == 0 else pltpu.roll(xe, (-o) % ext, 0)
                acc = acc + jnp.where(_tap_valid(t, o, lc, n), sh[SUB:tr + SUB], 0.0) * w[k:k + 1]
            u_ref[pl.ds(r0, tr), :] = acc
            return carry

        lax.fori_loop(0, n // tr, chunk, 0)

    blocks = [((n, cbk), F32)] * 2 + [((4, cbk), F32), ((1, cbk), F32)]
    scratch = [((n + 2 * SUB, cbk), F32)]
    return pl.pallas_call(
        body, name=name, grid=(W // cbk,),
        in_specs=[pl.BlockSpec((n, cbk), lambda j: (0, j)), pl.BlockSpec((4, cbk), lambda j: (0, j)),
                  pl.BlockSpec((1, cbk), lambda j: (0, j))],
        out_specs=pl.BlockSpec((n, cbk), lambda j: (0, j)),
        out_shape=_sds((n, W), F32),
        scratch_shapes=[pltpu.VMEM(s, d) for s, d in scratch],
        compiler_params=_params(blocks, scratch, ("parallel",)),
    )(proj_all, cw, cb)


def _conv_bwd(du_all, proj_all, cw, dproj, lc, W, name):
    n = du_all.shape[0]
    cbk = _tile(W, CB_SEQ, LANE)
    tr = _tile(n, TR_CONV, SUB16)
    ext = tr + 2 * SUB

    def body(du_ref, x_ref, w_ref, dp_in, dx_ref, gw_ref, gb_ref, dp_ref, xp_ref):
        del dp_in
        for ref, src in ((dp_ref, du_ref), (xp_ref, x_ref)):
            ref[0:SUB, :] = jnp.zeros((SUB, cbk), F32)
            ref[n + SUB:n + 2 * SUB, :] = jnp.zeros((SUB, cbk), F32)
            ref[SUB:n + SUB, :] = src[...]
        w = w_ref[...]

        def fold(v):
            return jnp.sum(v.reshape(tr // SUB, SUB, cbk), axis=0)

        def chunk(ci, carry):
            r0 = pl.multiple_of(ci * tr, SUB16)
            de = dp_ref[pl.ds(r0, ext), :]
            xe = xp_ref[pl.ds(r0, ext), :]
            t = r0 + lax.broadcasted_iota(jnp.int32, (tr, cbk), 0)
            d0 = de[SUB:tr + SUB]
            dx = jnp.zeros((tr, cbk), F32)
            new = []
            for k in range(4):
                o = k - 1
                dsh = de if o == 0 else pltpu.roll(de, o % ext, 0)
                dx = dx + jnp.where(_tap_valid(t, -o, lc, n), dsh[SUB:tr + SUB], 0.0) * w[k:k + 1]
                xsh = xe if o == 0 else pltpu.roll(xe, (-o) % ext, 0)
                new.append(carry[k] + fold(d0 * jnp.where(_tap_valid(t, o, lc, n), xsh[SUB:tr + SUB], 0.0)))
            new.append(carry[4] + fold(d0))
            dx_ref[pl.ds(r0, tr), :] = dx.astype(dx_ref.dtype)
            return tuple(new)

        zero = jnp.zeros((SUB, cbk), F32)
        acc = lax.fori_loop(0, n // tr, chunk, (zero,) * 5)
        for k in range(4):
            gw_ref[k:k + 1, :] = jnp.sum(acc[k], axis=0, keepdims=True)
        gb_ref[...] = jnp.sum(acc[4], axis=0, keepdims=True)

    col = pl.BlockSpec((n, cbk), lambda j: (0, j))
    blocks = [((n, cbk), F32)] * 2 + [((n, cbk), MXU), ((4, cbk), F32), ((4, cbk), F32), ((1, cbk), F32)]
    scratch = [((n + 2 * SUB, cbk), F32)] * 2
    return pl.pallas_call(
        body, name=name, grid=(W // cbk,),
        in_specs=[col, col, pl.BlockSpec((4, cbk), lambda j: (0, j)), ANY],
        out_specs=[col, pl.BlockSpec((4, cbk), lambda j: (0, j)), pl.BlockSpec((1, cbk), lambda j: (0, j))],
        out_shape=[_sds(dproj.shape, dproj.dtype), _sds((4, W), F32), _sds((1, W), F32)],
        input_output_aliases={3: 0},
        scratch_shapes=[pltpu.VMEM(s, d) for s, d in scratch],
        compiler_params=_params(blocks, scratch, ("parallel",)),
    )(du_all, proj_all, cw, dproj)


def _gate_coeffs(ub, u, d, wr_ref, wi_ref, br_ref, bi_ref, lam_ref):
    c = -LRU_C * _softplus(-lam_ref[d:d + 1, :])
    r = _sigmoid(lax.dot_general(ub, wr_ref[d], NN, preferred_element_type=F32) + br_ref[d:d + 1, :])
    ig = _sigmoid(lax.dot_general(ub, wi_ref[d], NN, preferred_element_type=F32) + bi_ref[d:d + 1, :])
    la = c * r
    a = jnp.exp(la)
    sq = jnp.sqrt(-jnp.tanh(la) * (1.0 + a * a))
    return c, r, ig, a, sq


def _gate_specs(tl, hd):
    w_spec = pl.BlockSpec((2, None, hd, hd), lambda h, i: (0, h, 0, 0))
    v_spec = pl.BlockSpec((2, hd), lambda h, i: (0, h))
    return w_spec, v_spec


def _gates_fwd(u_all, wr, wi, br, bi, lam, name):
    n, W = u_all.shape
    heads, hd = wr.shape[1], wr.shape[2]
    tl = _tile(n, TL_GATES, SUB16)

    def body(u_ref, wr_ref, wi_ref, br_ref, bi_ref, lam_ref, a_ref, b_ref):
        u = u_ref[...]
        ub = u.astype(MXU)
        for d in range(2):
            _, _, ig, a, sq = _gate_coeffs(ub, u, d, wr_ref, wi_ref, br_ref, bi_ref, lam_ref)
            a_ref[d] = a
            b_ref[d] = sq * (ig * u)

    w_spec, v_spec = _gate_specs(tl, hd)
    o_spec = pl.BlockSpec((2, tl, hd), lambda h, i: (0, i, h))
    blocks = [((tl, hd), F32), ((2, hd, hd), MXU), ((2, hd, hd), MXU)] + [((2, hd), F32)] * 3 + [((2, tl, hd), F32)] * 2
    return pl.pallas_call(
        body, name=name, grid=(heads, n // tl),
        in_specs=[pl.BlockSpec((tl, hd), lambda h, i: (i, h)), w_spec, w_spec, v_spec, v_spec, v_spec],
        out_specs=[o_spec, o_spec], out_shape=[_sds((2, n, W), F32)] * 2,
        compiler_params=_params(blocks, dims=("parallel", "parallel")),
    )(u_all, wr, wi, br, bi, lam)


def _gates_bwd(u_all, da, db, wr, wi, br, bi, lam, name):
    n, W = u_all.shape
    heads, hd = wr.shape[1], wr.shape[2]
    tl = _tile(n, TL_GATES, SUB16)
    ni = n // tl

    def body(u_ref, da_ref, db_ref, wr_ref, wi_ref, br_ref, bi_ref, lam_ref,
             du_ref, gwr_ref, gwi_ref, gbr_ref, gbi_ref, gc_ref, accr_ref, acci_ref):
        i = pl.program_id(1)

        @pl.when(i == 0)
        def _():
            accr_ref[...] = jnp.zeros_like(accr_ref)
            acci_ref[...] = jnp.zeros_like(acci_ref)
            gbr_ref[...] = jnp.zeros_like(gbr_ref)
            gbi_ref[...] = jnp.zeros_like(gbi_ref)
            gc_ref[...] = jnp.zeros_like(gc_ref)

        u = u_ref[...]
        ub = u.astype(MXU)
        du = jnp.zeros_like(u)
        for d in range(2):
            c, r, ig, a, sq = _gate_coeffs(ub, u, d, wr_ref, wi_ref, br_ref, bi_ref, lam_ref)
            dbv = db_ref[d]
            t = dbv * sq
            du = du + t * ig
            d_la = da_ref[d] * a - (dbv * ig * u) * (a * a) / sq
            gc_ref[d:d + 1, :] += jnp.sum(d_la * r, axis=0, keepdims=True)
            d_pr = (d_la * c) * (r * (1.0 - r))
            d_pi = (t * u) * (ig * (1.0 - ig))
            gbr_ref[d:d + 1, :] += jnp.sum(d_pr, axis=0, keepdims=True)
            gbi_ref[d:d + 1, :] += jnp.sum(d_pi, axis=0, keepdims=True)
            pb = d_pr.astype(MXU)
            qb = d_pi.astype(MXU)
            du = du + lax.dot_general(pb, wr_ref[d], NT, preferred_element_type=F32)
            du = du + lax.dot_general(qb, wi_ref[d], NT, preferred_element_type=F32)
            accr_ref[d] += lax.dot_general(ub, pb, TN_DIMS, preferred_element_type=F32)
            acci_ref[d] += lax.dot_general(ub, qb, TN_DIMS, preferred_element_type=F32)
        du_ref[...] = du

        @pl.when(i == ni - 1)
        def _():
            gwr_ref[...] = accr_ref[...].astype(gwr_ref.dtype)
            gwi_ref[...] = acci_ref[...].astype(gwi_ref.dtype)

    w_spec, v_spec = _gate_specs(tl, hd)
    u_spec = pl.BlockSpec((tl, hd), lambda h, i: (i, h))
    ab_spec = pl.BlockSpec((2, tl, hd), lambda h, i: (0, i, h))
    blocks = ([((tl, hd), F32)] * 2 + [((2, tl, hd), F32)] * 2 + [((2, hd, hd), MXU)] * 4 + [((2, hd), F32)] * 6)
    scratch = [((2, hd, hd), F32)] * 2
    return pl.pallas_call(
        body, name=name, grid=(heads, ni),
        in_specs=[u_spec, ab_spec, ab_spec, w_spec, w_spec, v_spec, v_spec, v_spec],
        out_specs=[u_spec, w_spec, w_spec, v_spec, v_spec, v_spec],
        out_shape=[_sds((n, W), F32), _sds(wr.shape, MXU), _sds(wi.shape, MXU)] + [_sds((2, W), F32)] * 3,
        scratch_shapes=[pltpu.VMEM(s, d) for s, d in scratch],
        compiler_params=_params(blocks, scratch, ("parallel", "arbitrary")),
    )(u_all, da, db, wr, wi, br, bi, lam)


def _tile_scan(A, B, rows, reverse):
    for s in (1, 2, 4):
        if reverse:
            As, Bs, m = pltpu.roll(A, SUB - s, 0), pltpu.roll(B, SUB - s, 0), rows < SUB - s
        else:
            As, Bs, m = pltpu.roll(A, s, 0), pltpu.roll(B, s, 0), rows >= s
        B = jnp.where(m, A * Bs + B, B)
        A = jnp.where(m, A * As, A)
    return A, B


def _scan_chunks(n, lc):
    tc = _tile(lc, TL, SUB)
    assert n % tc == 0 and lc % tc == 0
    return tc, n // tc, lc // tc


def _scan_fwd(a_all, b_all, lc, name):
    _, n, W = a_all.shape
    cb = _tile(W, CB_SCAN, LANE)
    tc, nch, ncc = _scan_chunks(n, lc)
    ntile = tc // SUB

    def chunk(d, t):
        return jnp.where(d == 0, t, jnp.where(t < ncc, ncc - 1 - t, nch - 1 - (t - ncc)))

    def body(a_ref, b_ref, h_ref, carry_ref):
        rows = lax.broadcasted_iota(jnp.int32, (SUB, cb), 0)

        @pl.when(pl.program_id(2) == 0)
        def _():
            carry_ref[...] = jnp.zeros_like(carry_ref)

        def run(reverse):
            def step(i, h):
                r = pl.multiple_of(((ntile - 1 - i) if reverse else i) * SUB, SUB)
                A, B = _tile_scan(a_ref[pl.ds(r, SUB), :], b_ref[pl.ds(r, SUB), :], rows, reverse)
                H = A * h + B
                h_ref[pl.ds(r, SUB), :] = H
                return H[0:1, :] if reverse else H[SUB - 1:SUB, :]

            carry_ref[...] = lax.fori_loop(0, ntile, step, carry_ref[...], unroll=2)

        @pl.when(pl.program_id(1) == 0)
        def _():
            run(False)

        @pl.when(pl.program_id(1) == 1)
        def _():
            run(True)

    spec = pl.BlockSpec((None, tc, cb), lambda j, d, t: (d, chunk(d, t), j))
    return pl.pallas_call(
        body, name=name, grid=(W // cb, 2, nch), in_specs=[spec, spec], out_specs=spec,
        out_shape=_sds((2, n, W), F32), scratch_shapes=[pltpu.VMEM((1, cb), F32)],
        compiler_params=_params([((tc, cb), F32)] * 3, [((1, cb), F32)], ("parallel", "arbitrary", "arbitrary")),
    )(a_all, b_all)


def _scan_bwd(a_all, h_all, dya, lc, name):
    _, n, W = a_all.shape
    cb = _tile(W, CB_SCAN, LANE)
    tc, nch, ncc = _scan_chunks(n, lc)
    ntile = tc // SUB
    nl = nch - ncc

    def chunk(d, t):
        return jnp.where(d == 0, nch - 1 - t, jnp.where(t < nl, ncc + t, t - nl))

    def neighbour(d, t):
        c = chunk(d, t)
        below = jnp.maximum(c * ntile - 1, 0)
        above = jnp.where(c == nch - 1, 0, jnp.minimum((c + 1) * ntile, nch * ntile - 1))
        return jnp.where(d == 0, below, above)

    def body(a_ref, h_ref, hn_ref, g_ref, da_ref, db_ref, mu_ref):
        rows = lax.broadcasted_iota(jnp.int32, (SUB, cb), 0)
        d, t = pl.program_id(1), pl.program_id(2)
        c = chunk(d, t)
        has_g = c >= ncc

        @pl.when(t == 0)
        def _():
            mu_ref[...] = jnp.zeros_like(mu_ref)

        def tile(ref, j):
            return ref[pl.ds(pl.multiple_of(j * SUB, SUB), SUB), :]

        def run(up):
            if up:
                edge = jnp.where(c == ncc - 1, 0.0, hn_ref[0:1, :])
            else:
                edge = jnp.where(c > 0, hn_ref[SUB - 1:SUB, :], 0.0)

            def step(i, mu):
                j = i if up else ntile - 1 - i
                a_t = tile(a_ref, j)
                g_t = jnp.where(has_g, tile(g_ref, j), 0.0)
                if up:
                    ap = jnp.where(rows >= 1, pltpu.roll(a_t, 1, 0), 1.0)
                    nb_row = jnp.where(j < ntile - 1, tile(h_ref, jnp.minimum(j + 1, ntile - 1))[0:1, :], edge)
                    hprev = jnp.where(rows < SUB - 1, pltpu.roll(tile(h_ref, j), SUB - 1, 0), nb_row)
                else:
                    ap = jnp.where(rows < SUB - 1, pltpu.roll(a_t, SUB - 1, 0), 1.0)
                    nb_row = jnp.where(j > 0, tile(h_ref, jnp.maximum(j - 1, 0))[SUB - 1:SUB, :], edge)
                    hprev = jnp.where(rows >= 1, pltpu.roll(tile(h_ref, j), 1, 0), nb_row)
                A, B = _tile_scan(ap, g_t, rows, not up)
                lam = A * mu + B
                r = pl.multiple_of(j * SUB, SUB)
                da_ref[pl.ds(r, SUB), :] = lam * hprev
                db_ref[pl.ds(r, SUB), :] = lam
                return a_t[SUB - 1:SUB, :] * lam[SUB - 1:SUB, :] if up else a_t[0:1, :] * lam[0:1, :]

            mu_ref[...] = lax.fori_loop(0, ntile, step, mu_ref[...], unroll=2)

        @pl.when(d == 0)
        def _():
            run(False)

        @pl.when(d == 1)
        def _():
            run(True)

    spec = pl.BlockSpec((None, tc, cb), lambda j, d, t: (d, chunk(d, t), j))
    n_spec = pl.BlockSpec((None, SUB, cb), lambda j, d, t: (d, neighbour(d, t), j))
    g_spec = pl.BlockSpec((tc, cb), lambda j, d, t: (jnp.maximum(chunk(d, t) - ncc, 0), j))
    blocks = [((tc, cb), F32)] * 5 + [((SUB, cb), F32)]
    return pl.pallas_call(
        body, name=name, grid=(W // cb, 2, nch), in_specs=[spec, spec, n_spec, g_spec], out_specs=[spec, spec],
        out_shape=[_sds((2, n, W), F32)] * 2, scratch_shapes=[pltpu.VMEM((1, cb), F32)],
        compiler_params=_params(blocks, [((1, cb), F32)], ("parallel", "arbitrary", "arbitrary")),
    )(a_all, h_all, h_all, dya)


def _pool_window(v, w, tl, cb, transpose):
    half = w // 2
    pos = lax.broadcasted_iota(jnp.int32, (tl, cb), 0) % GRID_W
    cnt = (jnp.minimum(pos + half - 1, GRID_W - 1) - jnp.maximum(pos - half, 0) + 1).astype(F32)
    src = v / cnt if transpose else v

    def run_sum(s, step):
        span = 1
        while span < half:
            ok = (pos + span < GRID_W) if step > 0 else (pos - span >= 0)
            s = s + jnp.where(ok, pltpu.roll(s, (-step * span) % tl, 0), 0.0)
            span *= 2
        return s

    ahead, behind = run_sum(src, 1), run_sum(src, -1)
    if transpose:
        return behind + jnp.where(pos + 1 < GRID_W, pltpu.roll(ahead, tl - 1, 0), 0.0) - v
    return (ahead + jnp.where(pos >= 1, pltpu.roll(behind, 1, 0), 0.0)) / cnt - v


def _pool_z(src, row0, col0, L, W, transpose, dproj, name):
    G = len(POOL_WINDOWS)
    pd = W // G
    tl = _tile(L, TL, GRID_W)
    cb = _tile(pd, CB_POOL, LANE)
    assert row0 % tl == 0 and col0 % cb == 0
    rb, cbk = row0 // tl, col0 // cb
    nj = pd // cb

    def body(x_ref, *rest):
        o_ref = rest[-1]
        for gi, w in enumerate(POOL_WINDOWS):
            @pl.when(pl.program_id(0) == gi)
            def _(w=w):
                o_ref[...] = _pool_window(x_ref[...], w, tl, cb, transpose).astype(o_ref.dtype)

    plain = pl.BlockSpec((tl, cb), lambda g, i, j: (i, g * nj + j))
    window = pl.BlockSpec((tl, cb), lambda g, i, j: (i + rb, cbk + g * nj + j))
    blocks = [((tl, cb), F32), ((tl, cb), MXU)]
    if transpose:
        return pl.pallas_call(
            body, name=name, grid=(G, L // tl, nj), in_specs=[plain, ANY], out_specs=window,
            out_shape=_sds(dproj.shape, dproj.dtype), input_output_aliases={1: 0},
            compiler_params=_params(blocks, dims=("parallel",) * 3),
        )(src, dproj)
    return pl.pallas_call(
        body, name=name, grid=(G, L // tl, nj), in_specs=[window], out_specs=plain,
        out_shape=_sds((L, W), MXU),
        compiler_params=_params(blocks, dims=("parallel",) * 3),
    )(src)


def _mix_fwd(hs, proj_all, ypre, b_pool, pool_scale, lc, name):
    L, W = ypre.shape
    tl = _tile(L, TL, SUB16)
    cb = _tile(W, CB_MIX, LANE)
    nj = W // cb
    assert lc % tl == 0
    rb = lc // tl

    def body(hs_ref, ga_ref, yp_ref, gb_ref, bp_ref, ps_ref, o_ref):
        p = pl.program_id(2)

        @pl.when(p == 0)
        def _():
            g = ga_ref[...]
            o_ref[...] = ((hs_ref[0] + hs_ref[1]) * (g * _sigmoid(g))).astype(o_ref.dtype)

        @pl.when(p == 1)
        def _():
            g = gb_ref[...]
            yb = (yp_ref[...] + bp_ref[...]) * ps_ref[...]
            o_ref[...] = (yb * (g * _sigmoid(g))).astype(o_ref.dtype)

    vec = pl.BlockSpec((1, cb), lambda i, j, p: (0, j))
    blocks = [((2, tl, cb), F32)] + [((tl, cb), F32)] * 3 + [((tl, cb), MXU)]
    return pl.pallas_call(
        body, name=name, grid=(L // tl, nj, 2),
        in_specs=[pl.BlockSpec((2, tl, cb), lambda i, j, p: (0, i + rb, j)),
                  pl.BlockSpec((tl, cb), lambda i, j, p: (i + rb, 2 * nj + j)),
                  pl.BlockSpec((tl, cb), lambda i, j, p: (i, j)),
                  pl.BlockSpec((tl, cb), lambda i, j, p: (i + rb, 3 * nj + j)), vec, vec],
        out_specs=pl.BlockSpec((tl, cb), lambda i, j, p: (i, p * nj + j)),
        out_shape=_sds((L, 2 * W), MXU),
        compiler_params=_params(blocks, dims=("parallel", "parallel", "arbitrary")),
    )(hs, proj_all, ypre, proj_all, b_pool, pool_scale)


def _dsilu(g, sg):
    return sg * (1.0 + g * (1.0 - sg))


def _mixa_bwd(dmixed, hs, proj_all, dproj, lc, W, name):
    L = dmixed.shape[0]
    tl = _tile(L, TL, SUB16)
    cb = _tile(W, CB_MIX, LANE)
    nj = W // cb
    rb = lc // tl

    def body(dm_ref, hs_ref, ga_ref, dp_in, dya_ref, dga_ref):
        del dp_in
        g = ga_ref[...]
        sg = _sigmoid(g)
        dm = dm_ref[...]
        dya_ref[...] = dm * (g * sg)
        dga_ref[...] = (dm * (hs_ref[0] + hs_ref[1]) * _dsilu(g, sg)).astype(dga_ref.dtype)

    blocks = [((tl, cb), F32)] * 3 + [((2, tl, cb), F32), ((tl, cb), MXU)]
    return pl.pallas_call(
        body, name=name, grid=(L // tl, nj),
        in_specs=[pl.BlockSpec((tl, cb), lambda i, j: (i, j)),
                  pl.BlockSpec((2, tl, cb), lambda i, j: (0, i + rb, j)),
                  pl.BlockSpec((tl, cb), lambda i, j: (i + rb, 2 * nj + j)), ANY],
        out_specs=[pl.BlockSpec((tl, cb), lambda i, j: (i, j)),
                   pl.BlockSpec((tl, cb), lambda i, j: (i + rb, 2 * nj + j))],
        out_shape=[_sds((L, W), F32), _sds(dproj.shape, dproj.dtype)],
        input_output_aliases={3: 1},
        compiler_params=_params(blocks, dims=("parallel", "parallel")),
    )(dmixed, hs, proj_all, dproj)


def _mixb_bwd(dmixed, ypre, proj_all, b_pool, pool_scale, dproj, lc, W, name):
    L = dmixed.shape[0]
    tl = _tile(L, TL, SUB16)
    cb = _tile(W, CB_MIX, LANE)
    nj = W // cb
    rb = lc // tl

    def body(dm_ref, yp_ref, gb_ref, bp_ref, ps_ref, dp_in, dyp_ref, dgb_ref, gbp_ref, gps_ref):
        del dp_in
        i = pl.program_id(1)

        @pl.when(i == 0)
        def _():
            gbp_ref[...] = jnp.zeros_like(gbp_ref)
            gps_ref[...] = jnp.zeros_like(gps_ref)

        g = gb_ref[...]
        sg = _sigmoid(g)
        dm = dm_ref[...]
        yp = yp_ref[...] + bp_ref[...]
        ps = ps_ref[...]
        dyb = dm * (g * sg)
        dyp = dyb * ps
        dgb_ref[...] = (dm * (yp * ps) * _dsilu(g, sg)).astype(dgb_ref.dtype)
        dyp_ref[...] = dyp.astype(dyp_ref.dtype)
        gbp_ref[...] += jnp.sum(dyp, axis=0, keepdims=True)
        gps_ref[...] += jnp.sum(dyb * yp, axis=0, keepdims=True)

    vec = pl.BlockSpec((1, cb), lambda j, i: (0, j))
    blocks = [((tl, cb), F32)] * 3 + [((tl, cb), MXU)] * 2 + [((1, cb), F32)] * 4
    return pl.pallas_call(
        body, name=name, grid=(nj, L // tl),
        in_specs=[pl.BlockSpec((tl, cb), lambda j, i: (i, nj + j)),
                  pl.BlockSpec((tl, cb), lambda j, i: (i, j)),
                  pl.BlockSpec((tl, cb), lambda j, i: (i + rb, 3 * nj + j)), vec, vec, ANY],
        out_specs=[pl.BlockSpec((tl, cb), lambda j, i: (i, j)),
                   pl.BlockSpec((tl, cb), lambda j, i: (i + rb, 3 * nj + j)), vec, vec],
        out_shape=[_sds((L, W), MXU), _sds(dproj.shape, dproj.dtype), _sds((1, W), F32), _sds((1, W), F32)],
        input_output_aliases={5: 1},
        compiler_params=_params(blocks, dims=("parallel", "arbitrary")),
    )(dmixed, ypre, proj_all, b_pool, pool_scale, dproj)


def _dproj_init(n, lc, W, name):
    cb = _tile(W, CB_MIX, LANE)
    nj = W // cb

    def body(o_ref):
        o_ref[...] = jnp.zeros_like(o_ref)

    return pl.pallas_call(
        body, name=name, grid=(3 * nj,), in_specs=[],
        out_specs=pl.BlockSpec((lc, cb), lambda j: (0, nj + j)),
        out_shape=_sds((n, 4 * W), MXU),
        compiler_params=_params([((lc, cb), MXU)], dims=("parallel",)),
    )()


def _final(x2, out, tgt, gate, gfin, name):
    L, D = x2.shape
    tl = _tile(L, TL_FINAL, SUB16)

    def body(x_ref, o_ref, t_ref, gate_ref, g_ref, dout_ref, dxn_ref, loss_ref, ggf_ref, dgate_ref):
        i = pl.program_id(0)

        @pl.when(i == 0)
        def _():
            loss_ref[...] = jnp.zeros_like(loss_ref)
            ggf_ref[...] = jnp.zeros_like(ggf_ref)
            dgate_ref[...] = jnp.zeros_like(dgate_ref)

        o = o_ref[...]
        gate_v = gate_ref[...]
        gv = g_ref[...]
        xn = x_ref[...] + gate_v * o
        s = lax.rsqrt(jnp.mean(xn * xn, axis=-1, keepdims=True) + EPS)
        xh = xn * s
        err = xh * gv - t_ref[...]
        tok = jnp.mean(err * err, axis=-1, keepdims=True)
        loss_ref[...] += 0.5 * jnp.sum(tok, axis=0, keepdims=True)
        dy = err / D
        ggf_ref[...] += jnp.sum(dy * xh, axis=0, keepdims=True)
        dxh = dy * gv
        dxn = s * (dxh - xh * jnp.mean(dxh * xh, axis=-1, keepdims=True))
        dgate_ref[...] += jnp.sum(dxn * o, axis=0, keepdims=True)
        dout_ref[...] = (gate_v * dxn).astype(dout_ref.dtype)
        dxn_ref[...] = dxn

    row = pl.BlockSpec((tl, D), lambda i: (i, 0))
    vec = pl.BlockSpec((1, D), lambda i: (0, 0))
    blocks = [((tl, D), F32)] * 4 + [((tl, D), MXU)] + [((1, D), F32)] * 4
    return pl.pallas_call(
        body, name=name, grid=(L // tl,), in_specs=[row, row, row, vec, vec],
        out_specs=[row, row, pl.BlockSpec((1, 1), lambda i: (0, 0)), vec, vec],
        out_shape=[_sds((L, D), MXU), _sds((L, D), F32), _sds((1, 1), F32), _sds((1, D), F32), _sds((1, D), F32)],
        compiler_params=_params(blocks, dims=("arbitrary",)),
    )(x2, out, tgt, gate, gfin)


def _adamw_parts(w2, parts, m2, v2, name):
    R, C = w2.shape
    nh = len(parts)
    ch = C // nh
    tr = _tile(R, max(SUB16, (512 * 1024) // (ch * (nh + 1))), SUB16)

    def body(w_ref, *rest):
        p_refs = rest[:nh]
        m_ref, v_ref, g_ref, d_ref, nm_ref, nv_ref = rest[nh:]
        for q in range(nh):
            @pl.when(pl.program_id(1) == q)
            def _(p_ref=p_refs[q]):
                g = p_ref[0].astype(F32)
                for p in range(1, NDEV):
                    g = g + p_ref[p].astype(F32)
                delta, nm, nv = _adam(w_ref[...], g, m_ref[...], v_ref[...])
                g_ref[...] = g
                d_ref[...] = delta
                nm_ref[...] = nm
                nv_ref[...] = nv

    blk = pl.BlockSpec((tr, ch), lambda i, h: (i, h))
    p_spec = pl.BlockSpec((NDEV, tr, ch), lambda i, h: (0, i, 0))
    blocks = [((tr, ch), F32)] * 7 + [((NDEV, tr, ch), parts[0].dtype)] * nh
    return pl.pallas_call(
        body, name=name, grid=(R // tr, nh),
        in_specs=[blk] + [p_spec] * nh + [blk, blk],
        out_specs=[blk] * 4, out_shape=[_sds((R, C), F32)] * 4,
        compiler_params=_params(blocks, dims=("parallel", "arbitrary")),
    )(w2, *parts, m2, v2)


def _small_sum(vs, ga, gc, name):
    ns, nm = vs.shape[1], ga.shape[1]

    def body(v_ref, ga_ref, gc_ref, tot_ref, gb_ref):
        tot = v_ref[0:1, :]
        gb = ga_ref[0:1, :]
        for p in range(1, NDEV):
            tot = tot + v_ref[p:p + 1, :]
            gb = gb + ga_ref[p:p + 1, :]
        for p in range(NDEV):
            gb = gb + gc_ref[p:p + 1, :]
        tot_ref[...] = tot
        gb_ref[...] = gb

    blocks = [((NDEV, ns), F32), ((NDEV, nm), F32), ((NDEV, nm), F32), ((1, ns), F32), ((1, nm), F32)]
    return pl.pallas_call(
        body, name=name, out_shape=[_sds((1, ns), F32), _sds((1, nm), F32)],
        compiler_params=_params(blocks),
    )(vs, ga, gc)


def _adamw_small(g_raw, w, m, v, lam_range, cctx_range, name):
    npk = w.shape[1]

    def body(g_ref, w_ref, m_ref, v_ref, go_ref, d_ref, nm_ref, nv_ref):
        wv = w_ref[...]
        g = g_ref[...]
        idx = lax.broadcasted_iota(jnp.int32, (1, npk), 1)
        in_lam = (idx >= lam_range[0]) & (idx < lam_range[1])
        in_cc = (idx >= cctx_range[0]) & (idx < cctx_range[1])
        sg = _sigmoid_small(wv)
        g = jnp.where(in_lam, g * (LRU_C * _sigmoid_small(-wv)), jnp.where(in_cc, g * _dsilu(wv, sg), g))
        delta, nm, nv = _adam(wv, g, m_ref[...], v_ref[...])
        go_ref[...] = g
        d_ref[...] = delta
        nm_ref[...] = nm
        nv_ref[...] = nv

    return pl.pallas_call(
        body, name=name, out_shape=[_sds((1, npk), F32)] * 4,
        compiler_params=_params([((1, npk), F32)] * 8),
    )(g_raw, w, m, v)


def _pack(pieces):
    return jnp.concatenate([p.reshape(1, -1) for p in pieces], axis=1)


def kernel(x, c, ctx, c_ctx, w_ada, b_ada, g_norm, w_in, conv_w, conv_b, lru_lambda, w_rgate, b_rgate, w_igate, b_igate, w_pool, b_pool, pool_scale, w_out, g_final, loss_target, m_c_ctx, m_w_ada, m_b_ada, m_g_norm, m_w_in, m_conv_w, m_conv_b, m_lru_lambda, m_w_rgate, m_b_rgate, m_w_igate, m_b_igate, m_w_pool, m_b_pool, m_pool_scale, m_w_out, m_g_final, v_c_ctx, v_w_ada, v_b_ada, v_g_norm, v_w_in, v_conv_w, v_conv_b, v_lru_lambda, v_w_rgate, v_b_rgate, v_w_igate, v_b_igate, v_w_pool, v_b_pool, v_pool_scale, v_w_out, v_g_final):
    L, D = x.shape[1], x.shape[2]
    lc = ctx.shape[1]
    n = lc + L
    W = conv_b.shape[1]
    heads, hd = w_rgate.shape[2], w_rgate.shape[4]
    G, pd = w_pool.shape[1], w_pool.shape[3]
    na = w_ada.shape[2]
    nb = w_in.shape[2]
    ws = W // NDEV
    me = 4 * lax.axis_index("x") + 2 * lax.axis_index("y") + lax.axis_index("c")

    nbp = nb // WIN_PARTS
    w_in_parts = [w_in[0, :, q * nbp:(q + 1) * nbp].astype(MXU) for q in range(WIN_PARTS)]
    (win_0, cw_all, lam_all, br_all, bi_all, c_all) = _all_gather(
        [w_in_parts[0], conv_w[0], lru_lambda[0], b_rgate[0], b_igate[0], c], "gather_w_in")
    win = [win_0]
    cw = cw_all.transpose(1, 0, 2).reshape(4, W)
    lam = lam_all.transpose(1, 0, 2).reshape(2, W)
    br = br_all.transpose(1, 0, 2).reshape(2, W)
    bi = bi_all.transpose(1, 0, 2).reshape(2, W)

    cc = jnp.concatenate([c_all.reshape(NDEV, D), c_ctx.reshape(1, D), jnp.zeros((NDEV - 1, D), F32)], axis=0)
    b_loc = lax.dynamic_slice(b_ada, (0, me * na), (1, na))
    mod_loc, s_all = _ada_fwd(cc, w_ada[0], b_loc, "ada_fwd")
    (mod_all,) = _all_gather([mod_loc], "gather_mod")
    gate_w = [w_rgate[0].astype(MXU), w_igate[0].astype(MXU)]
    pool_w, out_w = [w_pool[0].astype(MXU)], [w_out[0].astype(MXU)]
    tok = mod_all
    sent_win = []
    for q in range(1, WIN_PARTS):
        part = [w_in_parts[q]]
        sent_win.append(_send_start(part, _place(part, False, f"place_w_in_{q}", [tok]), "level1", f"start_w_in_{q}"))
        tok = sent_win[-1][4]
    sent_gw = _send_start(gate_w, _place(gate_w, False, "place_gate_w", [tok]), False, "start_gate_w")
    sent_pw = _send_start(pool_w, _place(pool_w, False, "place_pool_w", [sent_gw[4]]), False, "start_pool_w")
    sent_ow = _send_start(out_w, _place(out_w, False, "place_out_w", [sent_pw[4]]), False, "start_out_w")
    mod = mod_all.transpose(1, 0, 2).reshape(2 * NDEV, NDEV * na)
    mod_me = lax.dynamic_slice(mod, (me, 0), (1, 3 * D))
    shift, scale, gate = mod_me[:, :D], mod_me[:, D:2 * D], mod_me[:, 2 * D:]
    shift = _tie(shift, [sent_gw[4], sent_ow[4]], "tie_weights")
    shift_c, scale_c = mod[NDEV:NDEV + 1, :D], mod[NDEV:NDEV + 1, D:2 * D]

    x2, ctx2, tgt = x[0], ctx[0], loss_target[0]
    gfin = g_final.reshape(1, D)
    h_all = _norm_mod(x2, g_norm, shift, scale, n, lc, None, "norm_lat")
    h_all = _norm_mod(ctx2, g_norm, shift_c, scale_c, n, 0, h_all, "norm_ctx")
    proj_all = _mm_proj(h_all, win[0], 0, WIN_PARTS, None, "mm_proj_0")
    for q in range(1, WIN_PARTS):
        lands = _send_wait(sent_win[q - 1], proj_all, "level1", f"wait_w_in_{q}")
        passed = _send_start([], lands, "level2", f"pass_w_in_{q}")
        win.append(_send_wait(passed, proj_all, "level2", f"wait_pass_w_in_{q}")[0])
        proj_all = _mm_proj(h_all, win[q], q, WIN_PARTS, proj_all, f"mm_proj_{q}")
    u_all = _conv_fwd(proj_all, cw, conv_b, lc, W, "conv_fwd")
    wr_all, wi_all = _send_wait(sent_gw, u_all, False, "wait_gate_w")
    wr = wr_all.transpose(1, 2, 0, 3, 4).reshape(2, heads, hd, hd)
    wi = wi_all.transpose(1, 2, 0, 3, 4).reshape(2, heads, hd, hd)
    a_all, b_all = _gates_fwd(u_all, wr, wi, br, bi, lam, "gates_fwd")
    hs = _scan_fwd(a_all, b_all, lc, "scan_fwd")
    z = _pool_z(proj_all, lc, W, L, W, False, None, "pool_z")
    (wpool_all,) = _send_wait(sent_pw, hs, False, "wait_pool_w")
    wpool = wpool_all.transpose(1, 0, 2, 3).reshape(G, pd, pd)
    ypre = _mm_group(z, wpool, "fwd", F32, "mm_pool")
    mixed = _mix_fwd(hs, proj_all, ypre, b_pool, pool_scale, lc, "mix_fwd")
    (wout_all,) = _send_wait(sent_ow, mixed, False, "wait_out_w")
    wout = wout_all.reshape(2 * W, D)
    out = _mm_plain(mixed, wout, NN, F32, "mm_out")
    d_out, dxn, loss_p, ggf, dgate = _final(x2, out, tgt, gate, gfin, "final")

    dmixed = _mm_plain(d_out, wout, NT, F32, "mm_dmixed")
    gwout = _mm_plain(mixed, d_out, TN_DIMS, MXU, "mm_gwout")
    ex_o = [gwout.reshape(NDEV, 2 * W // NDEV, D)]
    sent_o = _send_start(ex_o, _place(ex_o, True, "place_gwout"), True, "start_gwout")
    dproj = _dproj_init(n, lc, W, "dproj_init")
    dya, dproj = _mixa_bwd(dmixed, hs, proj_all, dproj, lc, W, "mixa_bwd")
    dypre, dproj, gbp, gps = _mixb_bwd(dmixed, ypre, proj_all, _tie(b_pool, [sent_o[4]], "tie_gwout"), pool_scale,
                                       dproj, lc, W, "mixb_bwd")
    dz = _mm_group(dypre, wpool, "bwd", F32, "mm_dz")
    gwpool = _mm_group(z, dypre, "wgrad", MXU, "mm_gwpool")
    dproj = _pool_z(dz, lc, W, L, W, True, dproj, "pool_z_bwd")
    da, db = _scan_bwd(a_all, hs, dya, lc, "scan_bwd")
    du, gwr, gwi, gbr, gbi, gcl = _gates_bwd(u_all, da, db, wr, wi, br, bi, lam, "gates_bwd")
    ex_s = [gwpool.reshape(G, NDEV, pd // NDEV, pd).transpose(1, 0, 2, 3),
            gwr.reshape(2, heads, NDEV, hd // NDEV, hd).transpose(2, 0, 1, 3, 4),
            gwi.reshape(2, heads, NDEV, hd // NDEV, hd).transpose(2, 0, 1, 3, 4)]
    sent_s = _send_start(ex_s, _place(ex_s, True, "place_gsmall"), True, "start_gsmall")
    dproj, gcw, gcb = _conv_bwd(du, proj_all, _tie(cw, [sent_s[4]], "tie_gsmall"), dproj, lc, W, "conv_bwd")
    h_t = _transpose(h_all, "transpose_h")
    sent_i, tok = [], None
    for q in range(GWIN_PARTS):
        part = _mm_gwin(h_t, dproj, nb, q, GWIN_PARTS, f"mm_gwin_{q}", dep=tok)
        part = pltpu.with_memory_space_constraint(part, pltpu.HBM)
        sent_i.append(_send_start([part], _place([part], True, f"place_gwin_{q}"), True, f"start_gwin_{q}"))
        tok = sent_i[-1][4]
    dh_all = _mm_dh(dproj, win, "mm_dh", tok)
    grad_x, dshift, dscale, ggn = _norm_bwd(x2, dh_all, lc, g_norm, scale, dxn, jnp.zeros((1, D), F32), "norm_bwd_lat")
    _, dshift_c, dscale_c, ggn = _norm_bwd(ctx2, dh_all, 0, g_norm, scale_c, None, ggn, "norm_bwd_ctx")

    dmod_me = jnp.concatenate([dshift, dscale, dgate], axis=1)
    dmod_c = jnp.concatenate([dshift_c, dscale_c, jnp.zeros((1, D), F32)], axis=1)
    smalls = [ggf, ggn, gcw, gcb, gcl, gbr, gbi, gbp, gps, jnp.pad(loss_p, ((0, 0), (0, LANE - 1)))]
    sizes = [s.size for s in smalls]
    small_all, dmod_all, dmodc_all = _all_gather([_pack(smalls), dmod_me, dmod_c], "gather_small")
    ga = lax.dynamic_slice(dmod_all.reshape(NDEV, 3 * D), (0, me * na), (NDEV, na))
    gc = lax.dynamic_slice(dmodc_all.reshape(NDEV, 3 * D), (0, me * na), (NDEV, na))
    g_wada, d_wada, nm_wada, nv_wada, pc = _ada_bwd(s_all, ga, gc, w_ada[0], m_w_ada[0], v_w_ada[0], "ada_bwd")
    (pc_all,) = _all_gather([pc[0:1]], "gather_cctx")
    tot, gb_ada = _small_sum(
        jnp.concatenate([small_all.reshape(NDEV, -1), pc_all.reshape(NDEV, D)], axis=1),
        dmod_all.reshape(NDEV, 3 * D), dmodc_all.reshape(NDEV, 3 * D), "small_sum")
    offs = [0]
    for s in sizes + [D]:
        offs.append(offs[-1] + s)
    t_ggf, t_ggn, t_gcw, t_gcb, t_gcl, t_gbr, t_gbi, t_gbp, t_gps, t_loss, t_pc = [
        tot[:, offs[i]:offs[i + 1]] for i in range(len(offs) - 1)]

    def shard(t, rows):
        return lax.dynamic_slice(t.reshape(rows, W), (0, me * ws), (rows, ws))

    def big(wv, parts, mv, vv, name):
        shp = wv.shape
        C = shp[-1]
        if not isinstance(parts, list):
            parts = [parts]
        parts = [p.reshape(NDEV, -1, C // len(parts)) for p in parts]
        outs = _adamw_parts(wv.reshape(-1, C), parts, mv.reshape(-1, C), vv.reshape(-1, C), name)
        return [o.reshape(shp) for o in outs]

    (recv_o,) = _send_wait(sent_o, tot, True, "wait_gwout")
    recv_p, recv_r, recv_i = _send_wait(sent_s, tot, True, "wait_gsmall")
    r_wout = big(w_out, recv_o, m_w_out, v_w_out, "adamw_w_out")
    r_wpool = big(w_pool, recv_p, m_w_pool, v_w_pool, "adamw_w_pool")
    r_wr = big(w_rgate, recv_r, m_w_rgate, v_w_rgate, "adamw_w_rgate")
    r_wi = big(w_igate, recv_i, m_w_igate, v_w_igate, "adamw_w_igate")
    r_wada = [o.reshape(w_ada.shape) for o in (g_wada, d_wada, nm_wada, nv_wada)]

    names = ["c_ctx", "b_ada", "g_norm", "conv_w", "conv_b", "lru_lambda", "b_rgate", "b_igate", "b_pool",
             "pool_scale", "g_final"]
    sw = [c_ctx, b_ada, g_norm, conv_w, conv_b, lru_lambda, b_rgate, b_igate, b_pool, pool_scale, g_final]
    sm = [m_c_ctx, m_b_ada, m_g_norm, m_conv_w, m_conv_b, m_lru_lambda, m_b_rgate, m_b_igate, m_b_pool,
          m_pool_scale, m_g_final]
    sv = [v_c_ctx, v_b_ada, v_g_norm, v_conv_w, v_conv_b, v_lru_lambda, v_b_rgate, v_b_igate, v_b_pool,
          v_pool_scale, v_g_final]
    sg = [t_pc, gb_ada, t_ggn, shard(t_gcw, 4), t_gcb, shard(t_gcl, 2), shard(t_gbr, 2), shard(t_gbi, 2), t_gbp,
          t_gps, t_ggf]
    poffs = [0]
    for wv in sw:
        poffs.append(poffs[-1] + wv.size)
    lam_range = (poffs[5], poffs[6])
    cctx_range = (poffs[0], poffs[1])
    small_out = _adamw_small(_pack(sg), _pack(sw), _pack(sm), _pack(sv), lam_range, cctx_range, "adamw_small")
    recv_w = [_send_wait(sent_i[q], small_out[0], True, f"wait_gwin_{q}")[0] for q in range(GWIN_PARTS)]
    r_win = big(w_in, recv_w, m_w_in, v_w_in, "adamw_w_in")
    r_small = {}
    for i, nm in enumerate(names):
        r_small[nm] = [o[:, poffs[i]:poffs[i + 1]].reshape(sw[i].shape) for o in small_out]

    res = dict(r_small)
    res.update(w_ada=r_wada, w_in=r_win, w_rgate=r_wr, w_igate=r_wi, w_pool=r_wpool, w_out=r_wout)
    order = ["c_ctx", "w_ada", "b_ada", "g_norm", "w_in", "conv_w", "conv_b", "lru_lambda", "w_rgate", "b_rgate",
             "w_igate", "b_igate", "w_pool", "b_pool", "pool_scale", "w_out", "g_final"]
    loss = t_loss[0, 0]
    outs = [loss, grad_x.reshape(x.shape)]
    for q in range(4):
        outs += [res[nm][q] for nm in order]
    return tuple(outs)
```

```python
import functools

import jax
import jax.numpy as jnp
from jax import lax
from jax.experimental import pallas as pl
from jax.experimental.pallas import tpu as pltpu

NDEV = 8
GRID_W = 64
POOL_WINDOWS = (2, 4, 8, 16)
LRU_C = 8.0
EPS = 1e-6
ADAM_LR = 0.001
ADAM_B1 = 0.9
ADAM_B2 = 0.999
ADAM_EPS = 1e-08
ADAM_WD = 0.01
ADAM_STEP = 10

F32 = jnp.float32
MXU = jnp.bfloat16

VMEM_BYTES = 64 * 1024 * 1024
VMEM_SLACK = 8 * 1024 * 1024
SUB = 8
SUB16 = 16
LANE = 128

TM = 1152
TN = 1024
TK = 2048
TL = 256
TL_FINAL = 128
TL_GATES = 1088
CB_POOL = 1024
CB_SEQ = 256
CB_SCAN = 4096
CB_MIX = 2048
TR_CONV = 576
GWIN_PARTS = 4
WIN_PARTS = 4

MESH_ID = pl.DeviceIdType.MESH


def _tile(n, pref, align):
    if n <= pref:
        return n
    for t in range(pref - pref % align, 0, -align):
        if n % t == 0:
            return t
    return n


def _nbytes(shape, dtype):
    n = 1
    for s in shape:
        if s is not None:
            n *= s
    return n * jnp.dtype(dtype).itemsize


def _params(blocks, scratch=(), dims=None):
    need = 2 * sum(_nbytes(s, d) for s, d in blocks) + sum(_nbytes(s, d) for s, d in scratch) + VMEM_SLACK
    kw = dict(vmem_limit_bytes=int(min(max(need, 2 * VMEM_SLACK), VMEM_BYTES - VMEM_SLACK // 2)))
    if dims is not None:
        kw["dimension_semantics"] = dims
    return pltpu.CompilerParams(**kw)


def _sds(shape, dtype):
    return jax.ShapeDtypeStruct(tuple(shape), dtype)


ANY = pl.BlockSpec(memory_space=pl.ANY)


def _ids():
    return lax.axis_index("x"), lax.axis_index("y"), lax.axis_index("c")


def _sigmoid(v):
    return 0.5 * jnp.tanh(0.5 * v) + 0.5


def _sigmoid_small(v):
    return jax.nn.sigmoid(v)


def _softplus(v):
    return jnp.maximum(v, 0.0) + jnp.log1p(jnp.exp(-jnp.abs(v)))


def _all_gather(xs, name):
    n = len(xs)

    def body(*refs):
        x_refs, o_refs = refs[:n], refs[n:2 * n]
        send_sems, recv_sems, local_sems = refs[2 * n:]
        x, y, c = _ids()
        me, sibling = (x, y, c), (x, y, 1 - c)
        chips = [(1 - x, y), (x, 1 - y), (1 - x, 1 - y)]

        def slot(a, p):
            return o_refs[a].at[4 * p[0] + 2 * p[1] + p[2]]

        def copy(a, k, block, to, src=None):
            return pltpu.make_async_remote_copy(
                src_ref=slot(a, block) if src is None else src, dst_ref=slot(a, block),
                send_sem=send_sems.at[7 * a + k], recv_sem=recv_sems.at[7 * a + k],
                device_id=to, device_id_type=MESH_ID)

        mine, first, passed = [], [], []
        for a in range(n):
            m = pltpu.make_async_copy(x_refs[a], slot(a, me), local_sems.at[a])
            m.start()
            mine.append(m)
            f = [copy(a, 0, me, sibling, src=x_refs[a])]
            f += [copy(a, 1 + j, me, (*chip, c), src=x_refs[a]) for j, chip in enumerate(chips)]
            for cp in f:
                cp.start()
            first += f
        for a in range(n):
            for j, chip in enumerate(chips):
                copy(a, 1 + j, (*chip, c), me).wait_recv()
                p = copy(a, 4 + j, (*chip, c), sibling)
                p.start()
                passed.append(p)
        for a in range(n):
            copy(a, 0, sibling, me).wait_recv()
            for j, chip in enumerate(chips):
                copy(a, 4 + j, (*chip, 1 - c), me).wait_recv()
        for cp in first + passed:
            cp.wait_send()
        for m in mine:
            m.wait()

    return pl.pallas_call(
        body, name=name,
        out_shape=[_sds((NDEV,) + v.shape, v.dtype) for v in xs],
        in_specs=[ANY] * n, out_specs=[ANY] * n,
        scratch_shapes=[pltpu.SemaphoreType.DMA((7 * n,)), pltpu.SemaphoreType.DMA((7 * n,)),
                        pltpu.SemaphoreType.DMA((n,))],
    )(*xs)


HBM = pl.BlockSpec(memory_space=pltpu.HBM)
SEM = pl.BlockSpec(memory_space=pltpu.SEMAPHORE)
EFFECT = pltpu.SideEffectType.DATAFLOW_SIDE_EFFECTING


def _peers():
    x, y, c = _ids()
    out = []
    for k in range(1, NDEV):
        px = 1 - x if k & 4 else x
        py = 1 - y if k & 2 else y
        pc = 1 - c if k & 1 else c
        out.append(((px, py, pc), 4 * px + 2 * py + pc))
    return out, 4 * x + 2 * y + c


def _tie(v, deps, name):
    def body(v_ref, *rest):
        rest[-1][...] = v_ref[...]

    vmem = pl.BlockSpec(memory_space=pltpu.VMEM)
    return pl.pallas_call(
        body, name=name, out_shape=_sds(v.shape, v.dtype), in_specs=[vmem] + [ANY] * len(deps), out_specs=vmem,
    )(v, *deps)


def _place(srcs, from_slot, name, deps=()):
    n = len(srcs)
    blks = [v.shape[1:] if from_slot else v.shape for v in srcs]

    nd = len(deps)

    def body(*refs):
        s_refs, l_refs = refs[:n], refs[n + nd:2 * n + nd]
        bufs, sems = refs[2 * n + nd:3 * n + nd], refs[3 * n + nd]
        x, y, c = _ids()
        me = 4 * x + 2 * y + c
        ins = [pltpu.make_async_copy(s_refs[a].at[me] if from_slot else s_refs[a], bufs[a], sems.at[a])
               for a in range(n)]
        outs = [pltpu.make_async_copy(bufs[a], l_refs[a].at[me], sems.at[n + a]) for a in range(n)]
        for cp in ins:
            cp.start()
        for a in range(n):
            ins[a].wait()
            outs[a].start()
        for cp in outs:
            cp.wait()

    scratch = [(b, v.dtype) for b, v in zip(blks, srcs)]
    return pl.pallas_call(
        body, name=name, out_shape=[_sds((NDEV,) + b, v.dtype) for b, v in zip(blks, srcs)],
        in_specs=[ANY] * (n + nd), out_specs=[ANY] * n,
        scratch_shapes=[pltpu.VMEM(b, d) for b, d in scratch] + [pltpu.SemaphoreType.DMA((2 * n,))],
        compiler_params=_params([], scratch),
    )(*srcs, *deps)


SEND_PEERS = {True: 7, False: 7, "level1": 4, "level2": 3}


def _send_copies(s_refs, l_refs, ssem, rsem, mode, receiving):
    peers, me = _peers()
    x, y, c = _ids()
    sibling = (x, y, 1 - c)
    chips = [(1 - x, y), (x, 1 - y), (1 - x, 1 - y)]
    npeer = SEND_PEERS[mode]
    out = []
    for a in range(len(l_refs)):
        if mode == "level2":
            for k, (px, py) in enumerate(chips):
                slot = 4 * px + 2 * py + (1 - c if receiving else c)
                out.append(pltpu.make_async_remote_copy(
                    src_ref=l_refs[a].at[slot], dst_ref=l_refs[a].at[slot], send_sem=ssem.at[npeer * a + k],
                    recv_sem=rsem.at[npeer * a + k], device_id=sibling, device_id_type=MESH_ID))
            continue
        targets = peers
        if mode == "level1":
            targets = [(sibling, 4 * x + 2 * y + 1 - c)] + [((px, py, c), 4 * px + 2 * py + c) for px, py in chips]
        for k, (dev, idx) in enumerate(targets):
            out.append(pltpu.make_async_remote_copy(
                src_ref=s_refs[a].at[idx] if mode is True else s_refs[a],
                dst_ref=l_refs[a].at[idx if receiving else me],
                send_sem=ssem.at[npeer * a + k], recv_sem=rsem.at[npeer * a + k], device_id=dev, device_id_type=MESH_ID))
    return out


def _send_start(srcs, lands, mode, name):
    ns, n = len(srcs), len(lands)
    nsem = SEND_PEERS[mode] * n

    def body(*refs):
        s_refs, l_refs = refs[:ns], refs[ns:ns + n]
        ssem, rsem = refs[ns + n], refs[ns + n + 1]
        token = refs[-1]
        for send in _send_copies(s_refs, l_refs, ssem, rsem, mode, False):
            send.start()
        token[...] = jnp.zeros_like(token)

    bufs = list(srcs) + list(lands)
    outs = pl.pallas_call(
        body, name=name,
        out_shape=[pltpu.SemaphoreType.DMA((nsem,)), pltpu.SemaphoreType.DMA((nsem,))]
        + [pltpu.HBM(v.shape, v.dtype) for v in bufs] + [_sds((SUB, LANE), F32)],
        in_specs=[HBM] * (ns + n), out_specs=[SEM, SEM] + [HBM] * (ns + n) + [pl.BlockSpec(memory_space=pltpu.VMEM)],
        input_output_aliases={i: 2 + i for i in range(ns + n)},
        compiler_params=pltpu.CompilerParams(has_side_effects=EFFECT),
    )(*[pltpu.with_memory_space_constraint(v, pltpu.HBM) for v in bufs])
    return outs[0], outs[1], list(outs[2:2 + ns]), list(outs[2 + ns:2 + ns + n]), outs[-1]


def _send_wait(started, after, mode, name):
    ssem, rsem, srcs, lands, _ = started
    ns, n = len(srcs), len(lands)

    def body(*refs):
        s_refs, l_refs = refs[:ns], refs[ns:ns + n]
        ssem_ref, rsem_ref = refs[ns + n], refs[ns + n + 1]
        for recv in _send_copies(s_refs, l_refs, ssem_ref, rsem_ref, mode, True):
            recv.wait_send()
            recv.wait_recv()

    bufs = list(srcs) + list(lands)
    outs = pl.pallas_call(
        body, name=name, out_shape=[pltpu.HBM(v.shape, v.dtype) for v in bufs],
        in_specs=[HBM] * (ns + n) + [SEM, SEM, ANY], out_specs=[HBM] * (ns + n),
        input_output_aliases={i: i for i in range(ns + n)},
        compiler_params=pltpu.CompilerParams(has_side_effects=EFFECT),
    )(*bufs, ssem, rsem, after)
    return list(outs[ns:])


NN = (((1,), (0,)), ((), ()))
NT = (((1,), (1,)), ((), ()))
TN_DIMS = (((0,), (0,)), ((), ()))


def _mm(a, b, *, grid, a_spec, b_spec, o_spec, out_shape, acc_shape, dims, name, dep=None, fill=None):
    k_axis = len(grid) - 1
    nk = grid[k_axis]
    extra = [v for v in (dep, fill) if v is not None]
    aliases = {} if fill is None else {1 + len(extra): 0}

    def body(a_ref, b_ref, *rest):
        o_ref, acc_ref = rest[-2], rest[-1]
        k = pl.program_id(k_axis)

        def prod():
            return lax.dot_general(a_ref[...], b_ref[...], dims, preferred_element_type=F32)

        if nk == 1:
            o_ref[...] = prod().astype(o_ref.dtype)
            return

        @pl.when(k == 0)
        def _():
            acc_ref[...] = prod()

        if nk > 2:
            @pl.when((k > 0) & (k < nk - 1))
            def _():
                acc_ref[...] += prod()

        @pl.when(k == nk - 1)
        def _():
            o_ref[...] = (acc_ref[...] + prod()).astype(o_ref.dtype)

    blocks = [(a_spec.block_shape, a.dtype), (b_spec.block_shape, b.dtype), (o_spec.block_shape, out_shape.dtype)]
    return pl.pallas_call(
        body, name=name, grid=grid, in_specs=[a_spec, b_spec] + [ANY] * len(extra), out_specs=o_spec,
        out_shape=out_shape, scratch_shapes=[pltpu.VMEM(acc_shape, F32)], input_output_aliases=aliases,
        compiler_params=_params(blocks, [(acc_shape, F32)], ("parallel",) * k_axis + ("arbitrary",)),
    )(a, b, *extra)


def _mm_plain(a, b, dims, out_dtype, name):
    if dims == TN_DIMS:
        (K, M), N = a.shape, b.shape[1]
    elif dims == NT:
        (M, K), N = a.shape, b.shape[0]
    else:
        (M, K), N = a.shape, b.shape[1]
    tm, tn = _tile(M, TM, LANE), _tile(N, TN, LANE)
    tk = _tile(K, TK, LANE if dims != TN_DIMS else SUB16)
    if dims == TN_DIMS:
        a_spec = pl.BlockSpec((tk, tm), lambda i, j, k: (k, i))
    else:
        a_spec = pl.BlockSpec((tm, tk), lambda i, j, k: (i, k))
    if dims == NT:
        b_spec = pl.BlockSpec((tn, tk), lambda i, j, k: (j, k))
    else:
        b_spec = pl.BlockSpec((tk, tn), lambda i, j, k: (k, j))
    return _mm(a, b, grid=(M // tm, N // tn, K // tk), a_spec=a_spec, b_spec=b_spec,
               o_spec=pl.BlockSpec((tm, tn), lambda i, j, k: (i, j)),
               out_shape=_sds((M, N), out_dtype), acc_shape=(tm, tn), dims=dims, name=name)


def _mm_proj(h_all, win_q, q, nparts, fill, name):
    n, D = h_all.shape
    nbp = win_q.shape[2]
    nb = nbp * nparts
    tm, tn, tk = _tile(n, TM, SUB16), _tile(nbp, TN, LANE), D
    nbn = nbp // tn
    return _mm(h_all, win_q, grid=(n // tm, NDEV * nbn, D // tk),
               a_spec=pl.BlockSpec((tm, tk), lambda i, j, k: (i, k)),
               b_spec=pl.BlockSpec((None, tk, tn), lambda i, j, k: (j // nbn, k, j % nbn)),
               o_spec=pl.BlockSpec((tm, tn), lambda i, j, k: (i, (j // nbn) * (nb // tn) + q * nbn + j % nbn)),
               out_shape=_sds((n, NDEV * nb), F32), acc_shape=(tm, tn), dims=NN, name=name, fill=fill)


def _mm_dh(dproj, wins, name, dep):
    nparts = len(wins)
    n = dproj.shape[0]
    _, D, nbp = wins[0].shape
    nb = nbp * nparts
    tm, tn = _tile(n, TM, SUB16), _tile(D, TN, LANE)

    def body(a_ref, *rest):
        b_refs, o_ref, acc_ref = rest[:nparts], rest[-2], rest[-1]
        k = pl.program_id(2)

        def prod():
            out = None
            for q in range(nparts):
                d = lax.dot_general(a_ref[:, q * nbp:(q + 1) * nbp], b_refs[q][...], NT, preferred_element_type=F32)
                out = d if out is None else out + d
            return out

        @pl.when(k == 0)
        def _():
            acc_ref[...] = prod()

        @pl.when((k > 0) & (k < NDEV - 1))
        def _():
            acc_ref[...] += prod()

        @pl.when(k == NDEV - 1)
        def _():
            o_ref[...] = acc_ref[...] + prod()

    blocks = [((tm, nb), dproj.dtype)] + [((tn, nbp), wins[0].dtype)] * nparts + [((tm, tn), F32)]
    return pl.pallas_call(
        body, name=name, grid=(n // tm, D // tn, NDEV),
        in_specs=[pl.BlockSpec((tm, nb), lambda i, j, k: (i, k))]
        + [pl.BlockSpec((None, tn, nbp), lambda i, j, k: (k, j, 0))] * nparts + [ANY],
        out_specs=pl.BlockSpec((tm, tn), lambda i, j, k: (i, j)), out_shape=_sds((n, D), F32),
        scratch_shapes=[pltpu.VMEM((tm, tn), F32)],
        compiler_params=_params(blocks, [((tm, tn), F32)], ("parallel", "parallel", "arbitrary")),
    )(dproj, *wins, dep)


def _transpose(x, name):
    R, C = x.shape
    tr, tc = _tile(R, TL, LANE), _tile(C, 2 * TL, LANE)

    def body(x_ref, o_ref):
        o_ref[...] = x_ref[...].T

    return pl.pallas_call(
        body, name=name, grid=(R // tr, C // tc),
        in_specs=[pl.BlockSpec((tr, tc), lambda i, j: (i, j))],
        out_specs=pl.BlockSpec((tc, tr), lambda i, j: (j, i)),
        out_shape=_sds((C, R), x.dtype),
        compiler_params=_params([((tr, tc), x.dtype)] * 2, dims=("parallel", "parallel")),
    )(x)


def _mm_gwin(h_t, dproj, nb, part, nparts, name, dep=None):
    D, n = h_t.shape
    nbp = nb // nparts
    tm, tn, tk = _tile(D, TM, LANE), _tile(nbp, TN, LANE), n
    nbn = nbp // tn
    return _mm(h_t, dproj, grid=(D // tm, NDEV * nbn, n // tk),
               a_spec=pl.BlockSpec((tm, tk), lambda i, j, k: (i, k)),
               b_spec=pl.BlockSpec((tk, tn), lambda i, j, k: (k, (j // nbn) * (nb // tn) + part * nbn + j % nbn)),
               o_spec=pl.BlockSpec((None, tm, tn), lambda i, j, k: (j // nbn, i, j % nbn)),
               out_shape=_sds((NDEV, D, nbp), MXU), acc_shape=(tm, tn), dims=NN, name=name, dep=dep)


def _mm_group(a, b, mode, out_dtype, name):
    if mode == "wgrad":
        L, W = a.shape
        G = len(POOL_WINDOWS)
        pd = W // G
        tm, tn, tk = _tile(pd, TM, LANE), _tile(pd, TN, LANE), _tile(L, TK, SUB16)
        nm, nn = pd // tm, pd // tn
        return _mm(a, b, grid=(G, nm, nn, L // tk),
                   a_spec=pl.BlockSpec((tk, tm), lambda g, i, j, k: (k, g * nm + i)),
                   b_spec=pl.BlockSpec((tk, tn), lambda g, i, j, k: (k, g * nn + j)),
                   o_spec=pl.BlockSpec((None, tm, tn), lambda g, i, j, k: (g, i, j)),
                   out_shape=_sds((G, pd, pd), out_dtype), acc_shape=(tm, tn), dims=TN_DIMS, name=name)
    L, W = a.shape
    G, pd, _ = b.shape
    tm, tn, tk = _tile(L, TM, SUB16), _tile(pd, TN, LANE), _tile(pd, TK, LANE)
    nn, nk = pd // tn, pd // tk
    if mode == "fwd":
        b_spec = pl.BlockSpec((None, tk, tn), lambda g, i, j, k: (g, k, j))
        dims = NN
    else:
        b_spec = pl.BlockSpec((None, tn, tk), lambda g, i, j, k: (g, j, k))
        dims = NT
    return _mm(a, b, grid=(G, L // tm, nn, nk),
               a_spec=pl.BlockSpec((tm, tk), lambda g, i, j, k: (i, g * nk + k)),
               b_spec=b_spec,
               o_spec=pl.BlockSpec((tm, tn), lambda g, i, j, k: (i, g * nn + j)),
               out_shape=_sds((L, W), out_dtype), acc_shape=(tm, tn), dims=dims, name=name)


def _ada_fwd(cc, w_loc, b_loc, name):
    R, D = cc.shape
    na = w_loc.shape[1]
    tk = _tile(D, 512, LANE)

    def body(c_ref, w_ref, b_ref, mod_ref, s_ref):
        k = pl.program_id(0)
        cv = c_ref[...]
        s = cv * _sigmoid_small(cv)
        s_ref[...] = s

        @pl.when(k == 0)
        def _():
            mod_ref[...] = jnp.broadcast_to(b_ref[...], mod_ref.shape)

        mod_ref[...] += lax.dot_general(s.astype(MXU), w_ref[...].astype(MXU), NN, preferred_element_type=F32)

    blocks = [((R, tk), F32), ((tk, na), F32), ((1, na), F32), ((R, na), F32), ((R, tk), F32)]
    return pl.pallas_call(
        body, name=name, grid=(D // tk,),
        in_specs=[pl.BlockSpec((R, tk), lambda k: (0, k)), pl.BlockSpec((tk, na), lambda k: (k, 0)),
                  pl.BlockSpec((1, na), lambda k: (0, 0))],
        out_specs=[pl.BlockSpec((R, na), lambda k: (0, 0)), pl.BlockSpec((R, tk), lambda k: (0, k))],
        out_shape=[_sds((R, na), F32), _sds((R, D), F32)],
        compiler_params=_params(blocks, dims=("arbitrary",)),
    )(cc, w_loc, b_loc)


def _adam(w, g, m, v):
    m = ADAM_B1 * m + (1.0 - ADAM_B1) * g
    v = ADAM_B2 * v + (1.0 - ADAM_B2) * (g * g)
    m_hat = m / (1.0 - ADAM_B1 ** ADAM_STEP)
    v_hat = v / (1.0 - ADAM_B2 ** ADAM_STEP)
    delta = -ADAM_LR * (m_hat / (jnp.sqrt(v_hat) + ADAM_EPS) + ADAM_WD * w)
    return delta, m, v


def _ada_bwd(s_all, ga, gc, w_loc, m_loc, v_loc, name):
    D, na = w_loc.shape
    tr = _tile(D, 256, LANE)

    def body(s_ref, ga_ref, gc_ref, w_ref, m_ref, v_ref, g_ref, d_ref, nm_ref, nv_ref, pc_ref):
        dmc = gc_ref[0:1, :]
        for p in range(1, NDEV):
            dmc = dmc + gc_ref[p:p + 1, :]
        rows = lax.broadcasted_iota(jnp.int32, (NDEV, na), 0)
        dmc8 = jnp.where(rows == 0, jnp.broadcast_to(dmc, (NDEV, na)), 0.0)
        dm = jnp.concatenate([ga_ref[...], dmc8], axis=0).astype(MXU)
        dmc16 = jnp.concatenate([dmc8, jnp.zeros_like(dmc8)], axis=0).astype(MXU)
        w = w_ref[...]
        g = lax.dot_general(s_ref[...].astype(MXU), dm, TN_DIMS, preferred_element_type=F32)
        pc_ref[...] = lax.dot_general(dmc16, w.astype(MXU), NT, preferred_element_type=F32)
        delta, nm, nv = _adam(w, g, m_ref[...], v_ref[...])
        g_ref[...] = g
        d_ref[...] = delta
        nm_ref[...] = nm
        nv_ref[...] = nv

    big = pl.BlockSpec((tr, na), lambda i: (i, 0))
    full = pl.BlockSpec((NDEV, na), lambda i: (0, 0))
    srow = pl.BlockSpec((2 * NDEV, tr), lambda i: (0, i))
    blocks = [((2 * NDEV, tr), F32)] * 2 + [((NDEV, na), F32)] * 2 + [((tr, na), F32)] * 7
    return pl.pallas_call(
        body, name=name, grid=(D // tr,),
        in_specs=[srow, full, full, big, big, big],
        out_specs=[big, big, big, big, srow],
        out_shape=[_sds((D, na), F32)] * 4 + [_sds((2 * NDEV, D), F32)],
        compiler_params=_params(blocks, dims=("parallel",)),
    )(s_all, ga, gc, w_loc, m_loc, v_loc)


def _norm_mod(x2, g, shift, scale, n, row0, h_prev, name):
    R, D = x2.shape
    tl = _tile(R, TL, SUB16)
    assert row0 % tl == 0
    b0 = row0 // tl

    def body(x_ref, g_ref, sh_ref, sc_ref, *rest):
        o_ref = rest[-1]
        xv = x_ref[...]
        s = lax.rsqrt(jnp.mean(xv * xv, axis=-1, keepdims=True) + EPS)
        nrm = xv * s * g_ref[...]
        o_ref[...] = (nrm * (1.0 + sc_ref[...]) + sh_ref[...]).astype(o_ref.dtype)

    vec = pl.BlockSpec((1, D), lambda i: (0, 0))
    in_specs = [pl.BlockSpec((tl, D), lambda i: (i, 0)), vec, vec, vec]
    args = [x2, g, shift, scale]
    aliases = {}
    if h_prev is not None:
        in_specs.append(ANY)
        args.append(h_prev)
        aliases = {4: 0}
    blocks = [((tl, D), F32), ((tl, D), MXU)] + [((1, D), F32)] * 3
    return pl.pallas_call(
        body, name=name, grid=(R // tl,), in_specs=in_specs,
        out_specs=pl.BlockSpec((tl, D), lambda i: (i + b0, 0)),
        out_shape=_sds((n, D), MXU), input_output_aliases=aliases,
        compiler_params=_params(blocks, dims=("parallel",)),
    )(*args)


def _norm_bwd(x2, dh_all, row0, g, scale, dxn, ggn0, name):
    R, D = x2.shape
    tl = _tile(R, TL_FINAL, SUB)
    assert row0 % tl == 0
    b0 = row0 // tl
    with_x = dxn is not None

    def body(*refs):
        if with_x:
            x_ref, dh_ref, g_ref, sc_ref, gg0_ref, dxn_ref, gx_ref, dsh_ref, dsc_ref, gg_ref = refs
        else:
            x_ref, dh_ref, g_ref, sc_ref, gg0_ref, dsh_ref, dsc_ref, gg_ref = refs
        i = pl.program_id(0)

        @pl.when(i == 0)
        def _():
            dsh_ref[...] = jnp.zeros_like(dsh_ref)
            dsc_ref[...] = jnp.zeros_like(dsc_ref)
            gg_ref[...] = gg0_ref[...]

        xv = x_ref[...]
        dh = dh_ref[...]
        gv = g_ref[...]
        s = lax.rsqrt(jnp.mean(xv * xv, axis=-1, keepdims=True) + EPS)
        xh = xv * s
        dsh_ref[...] += jnp.sum(dh, axis=0, keepdims=True)
        dsc_ref[...] += jnp.sum(dh * (xh * gv), axis=0, keepdims=True)
        dn = dh * (1.0 + sc_ref[...])
        gg_ref[...] += jnp.sum(dn * xh, axis=0, keepdims=True)
        if with_x:
            dxh = dn * gv
            dx = s * (dxh - xh * jnp.mean(dxh * xh, axis=-1, keepdims=True))
            gx_ref[...] = dx + dxn_ref[...]

    vec = pl.BlockSpec((1, D), lambda i: (0, 0))
    row = pl.BlockSpec((tl, D), lambda i: (i, 0))
    in_specs = [row, pl.BlockSpec((tl, D), lambda i: (i + b0, 0)), vec, vec, vec]
    args = [x2, dh_all, g, scale, ggn0]
    out_specs = [vec, vec, vec]
    out_shape = [_sds((1, D), F32)] * 3
    if with_x:
        in_specs.append(row)
        args.append(dxn)
        out_specs = [row] + out_specs
        out_shape = [_sds((R, D), F32)] + out_shape
    blocks = [((tl, D), F32)] * (4 if with_x else 2) + [((1, D), F32)] * 6
    outs = pl.pallas_call(
        body, name=name, grid=(R // tl,), in_specs=in_specs, out_specs=out_specs, out_shape=out_shape,
        compiler_params=_params(blocks, dims=("arbitrary",)),
    )(*args)
    return tuple(outs) if with_x else (None,) + tuple(outs)


def _tap_valid(t, o, lc, n):
    tt = t + o
    in_ctx = t < lc
    return (tt >= jnp.where(in_ctx, 0, lc)) & (tt < jnp.where(in_ctx, lc, n))


def _conv_fwd(proj_all, cw, cb, lc, W, name):
    n = proj_all.shape[0]
    cbk = _tile(W, CB_SEQ, LANE)
    tr = _tile(n, TR_CONV, SUB16)
    ext = tr + 2 * SUB

    def body(x_ref, w_ref, b_ref, u_ref, xp_ref):
        xp_ref[0:SUB, :] = jnp.zeros((SUB, cbk), F32)
        xp_ref[n + SUB:n + 2 * SUB, :] = jnp.zeros((SUB, cbk), F32)
        xp_ref[SUB:n + SUB, :] = x_ref[...]
        w = w_ref[...]
        bias = b_ref[...]

        def chunk(ci, carry):
            r0 = pl.multiple_of(ci * tr, SUB16)
            xe = xp_ref[pl.ds(r0, ext), :]
            t = r0 + lax.broadcasted_iota(jnp.int32, (tr, cbk), 0)
            acc = jnp.broadcast_to(bias, (tr, cbk))
            for k in range(4):
                o = k - 1
                sh = xe if o == 0 else pltpu.roll(xe, (-o) % ext, 0)
                acc = acc + jnp.where(_tap_valid(t, o, lc, n), sh[SUB:tr + SUB], 0.0) * w[k:k + 1]
            u_ref[pl.ds(r0, tr), :] = acc
            return carry

        lax.fori_loop(0, n // tr, chunk, 0)

    blocks = [((n, cbk), F32)] * 2 + [((4, cbk), F32), ((1, cbk), F32)]
    scratch = [((n + 2 * SUB, cbk), F32)]
    return pl.pallas_call(
        body, name=name, grid=(W // cbk,),
        in_specs=[pl.BlockSpec((n, cbk), lambda j: (0, j)), pl.BlockSpec((4, cbk), lambda j: (0, j)),
                  pl.BlockSpec((1, cbk), lambda j: (0, j))],
        out_specs=pl.BlockSpec((n, cbk), lambda j: (0, j)),
        out_shape=_sds((n, W), F32),
        scratch_shapes=[pltpu.VMEM(s, d) for s, d in scratch],
        compiler_params=_params(blocks, scratch, ("parallel",)),
    )(proj_all, cw, cb)


def _conv_bwd(du_all, proj_all, cw, dproj, lc, W, name):
    n = du_all.shape[0]
    cbk = _tile(W, CB_SEQ, LANE)
    tr = _tile(n, TR_CONV, SUB16)
    ext = tr + 2 * SUB

    def body(du_ref, x_ref, w_ref, dp_in, dx_ref, gw_ref, gb_ref, dp_ref, xp_ref):
        del dp_in
        for ref, src in ((dp_ref, du_ref), (xp_ref, x_ref)):
            ref[0:SUB, :] = jnp.zeros((SUB, cbk), F32)
            ref[n + SUB:n + 2 * SUB, :] = jnp.zeros((SUB, cbk), F32)
            ref[SUB:n + SUB, :] = src[...]
        w = w_ref[...]

        def fold(v):
            return jnp.sum(v.reshape(tr // SUB, SUB, cbk), axis=0)

        def chunk(ci, carry):
            r0 = pl.multiple_of(ci * tr, SUB16)
            de = dp_ref[pl.ds(r0, ext), :]
            xe = xp_ref[pl.ds(r0, ext), :]
            t = r0 + lax.broadcasted_iota(jnp.int32, (tr, cbk), 0)
            d0 = de[SUB:tr + SUB]
            dx = jnp.zeros((tr, cbk), F32)
            new = []
            for k in range(4):
                o = k - 1
                dsh = de if o == 0 else pltpu.roll(de, o % ext, 0)
                dx = dx + jnp.where(_tap_valid(t, -o, lc, n), dsh[SUB:tr + SUB], 0.0) * w[k:k + 1]
                xsh = xe if o == 0 else pltpu.roll(xe, (-o) % ext, 0)
                new.append(carry[k] + fold(d0 * jnp.where(_tap_valid(t, o, lc, n), xsh[SUB:tr + SUB], 0.0)))
            new.append(carry[4] + fold(d0))
            dx_ref[pl.ds(r0, tr), :] = dx.astype(dx_ref.dtype)
            return tuple(new)

        zero = jnp.zeros((SUB, cbk), F32)
        acc = lax.fori_loop(0, n // tr, chunk, (zero,) * 5)
        for k in range(4):
            gw_ref[k:k + 1, :] = jnp.sum(acc[k], axis=0, keepdims=True)
        gb_ref[...] = jnp.sum(acc[4], axis=0, keepdims=True)

    col = pl.BlockSpec((n, cbk), lambda j: (0, j))
    blocks = [((n, cbk), F32)] * 2 + [((n, cbk), MXU), ((4, cbk), F32), ((4, cbk), F32), ((1, cbk), F32)]
    scratch = [((n + 2 * SUB, cbk), F32)] * 2
    return pl.pallas_call(
        body, name=name, grid=(W // cbk,),
        in_specs=[col, col, pl.BlockSpec((4, cbk), lambda j: (0, j)), ANY],
        out_specs=[col, pl.BlockSpec((4, cbk), lambda j: (0, j)), pl.BlockSpec((1, cbk), lambda j: (0, j))],
        out_shape=[_sds(dproj.shape, dproj.dtype), _sds((4, W), F32), _sds((1, W), F32)],
        input_output_aliases={3: 0},
        scratch_shapes=[pltpu.VMEM(s, d) for s, d in scratch],
        compiler_params=_params(blocks, scratch, ("parallel",)),
    )(du_all, proj_all, cw, dproj)


def _gate_coeffs(ub, u, d, wr_ref, wi_ref, br_ref, bi_ref, lam_ref):
    c = -LRU_C * _softplus(-lam_ref[d:d + 1, :])
    r = _sigmoid(lax.dot_general(ub, wr_ref[d], NN, preferred_element_type=F32) + br_ref[d:d + 1, :])
    ig = _sigmoid(lax.dot_general(ub, wi_ref[d], NN, preferred_element_type=F32) + bi_ref[d:d + 1, :])
    la = c * r
    a = jnp.exp(la)
    sq = jnp.sqrt(-jnp.tanh(la) * (1.0 + a * a))
    return c, r, ig, a, sq


def _gate_specs(tl, hd):
    w_spec = pl.BlockSpec((2, None, hd, hd), lambda h, i: (0, h, 0, 0))
    v_spec = pl.BlockSpec((2, hd), lambda h, i: (0, h))
    return w_spec, v_spec


def _gates_fwd(u_all, wr, wi, br, bi, lam, name):
    n, W = u_all.shape
    heads, hd = wr.shape[1], wr.shape[2]
    tl = _tile(n, TL_GATES, SUB16)

    def body(u_ref, wr_ref, wi_ref, br_ref, bi_ref, lam_ref, a_ref, b_ref):
        u = u_ref[...]
        ub = u.astype(MXU)
        for d in range(2):
            _, _, ig, a, sq = _gate_coeffs(ub, u, d, wr_ref, wi_ref, br_ref, bi_ref, lam_ref)
            a_ref[d] = a
            b_ref[d] = sq * (ig * u)

    w_spec, v_spec = _gate_specs(tl, hd)
    o_spec = pl.BlockSpec((2, tl, hd), lambda h, i: (0, i, h))
    blocks = [((tl, hd), F32), ((2, hd, hd), MXU), ((2, hd, hd), MXU)] + [((2, hd), F32)] * 3 + [((2, tl, hd), F32)] * 2
    return pl.pallas_call(
        body, name=name, grid=(heads, n // tl),
        in_specs=[pl.BlockSpec((tl, hd), lambda h, i: (i, h)), w_spec, w_spec, v_spec, v_spec, v_spec],
        out_specs=[o_spec, o_spec], out_shape=[_sds((2, n, W), F32)] * 2,
        compiler_params=_params(blocks, dims=("parallel", "parallel")),
    )(u_all, wr, wi, br, bi, lam)


def _gates_bwd(u_all, da, db, wr, wi, br, bi, lam, name):
    n, W = u_all.shape
    heads, hd = wr.shape[1], wr.shape[2]
    tl = _tile(n, TL_GATES, SUB16)
    ni = n // tl

    def body(u_ref, da_ref, db_ref, wr_ref, wi_ref, br_ref, bi_ref, lam_ref,
             du_ref, gwr_ref, gwi_ref, gbr_ref, gbi_ref, gc_ref, accr_ref, acci_ref):
        i = pl.program_id(1)

        @pl.when(i == 0)
        def _():
            accr_ref[...] = jnp.zeros_like(accr_ref)
            acci_ref[...] = jnp.zeros_like(acci_ref)
            gbr_ref[...] = jnp.zeros_like(gbr_ref)
            gbi_ref[...] = jnp.zeros_like(gbi_ref)
            gc_ref[...] = jnp.zeros_like(gc_ref)

        u = u_ref[...]
        ub = u.astype(MXU)
        du = jnp.zeros_like(u)
        for d in range(2):
            c, r, ig, a, sq = _gate_coeffs(ub, u, d, wr_ref, wi_ref, br_ref, bi_ref, lam_ref)
            dbv = db_ref[d]
            t = dbv * sq
            du = du + t * ig
            d_la = da_ref[d] * a - (dbv * ig * u) * (a * a) / sq
            gc_ref[d:d + 1, :] += jnp.sum(d_la * r, axis=0, keepdims=True)
            d_pr = (d_la * c) * (r * (1.0 - r))
            d_pi = (t * u) * (ig * (1.0 - ig))
            gbr_ref[d:d + 1, :] += jnp.sum(d_pr, axis=0, keepdims=True)
            gbi_ref[d:d + 1, :] += jnp.sum(d_pi, axis=0, keepdims=True)
            pb = d_pr.astype(MXU)
            qb = d_pi.astype(MXU)
            du = du + lax.dot_general(pb, wr_ref[d], NT, preferred_element_type=F32)
            du = du + lax.dot_general(qb, wi_ref[d], NT, preferred_element_type=F32)
            accr_ref[d] += lax.dot_general(ub, pb, TN_DIMS, preferred_element_type=F32)
            acci_ref[d] += lax.dot_general(ub, qb, TN_DIMS, preferred_element_type=F32)
        du_ref[...] = du

        @pl.when(i == ni - 1)
        def _():
            gwr_ref[...] = accr_ref[...].astype(gwr_ref.dtype)
            gwi_ref[...] = acci_ref[...].astype(gwi_ref.dtype)

    w_spec, v_spec = _gate_specs(tl, hd)
    u_spec = pl.BlockSpec((tl, hd), lambda h, i: (i, h))
    ab_spec = pl.BlockSpec((2, tl, hd), lambda h, i: (0, i, h))
    blocks = ([((tl, hd), F32)] * 2 + [((2, tl, hd), F32)] * 2 + [((2, hd, hd), MXU)] * 4 + [((2, hd), F32)] * 6)
    scratch = [((2, hd, hd), F32)] * 2
    return pl.pallas_call(
        body, name=name, grid=(heads, ni),
        in_specs=[u_spec, ab_spec, ab_spec, w_spec, w_spec, v_spec, v_spec, v_spec],
        out_specs=[u_spec, w_spec, w_spec, v_spec, v_spec, v_spec],
        out_shape=[_sds((n, W), F32), _sds(wr.shape, MXU), _sds(wi.shape, MXU)] + [_sds((2, W), F32)] * 3,
        scratch_shapes=[pltpu.VMEM(s, d) for s, d in scratch],
        compiler_params=_params(blocks, scratch, ("parallel", "arbitrary")),
    )(u_all, da, db, wr, wi, br, bi, lam)


def _tile_scan(A, B, rows, reverse):
    for s in (1, 2, 4):
        if reverse:
            As, Bs, m = pltpu.roll(A, SUB - s, 0), pltpu.roll(B, SUB - s, 0), rows < SUB - s
        else:
            As, Bs, m = pltpu.roll(A, s, 0), pltpu.roll(B, s, 0), rows >= s
        B = jnp.where(m, A * Bs + B, B)
        A = jnp.where(m, A * As, A)
    return A, B


def _scan_chunks(n, lc):
    tc = _tile(lc, TL, SUB)
    assert n % tc == 0 and lc % tc == 0
    return tc, n // tc, lc // tc


def _scan_fwd(a_all, b_all, lc, name):
    _, n, W = a_all.shape
    cb = _tile(W, CB_SCAN, LANE)
    tc, nch, ncc = _scan_chunks(n, lc)
    ntile = tc // SUB

    def chunk(d, t):
        return jnp.where(d == 0, t, jnp.where(t < ncc, ncc - 1 - t, nch - 1 - (t - ncc)))

    def body(a_ref, b_ref, h_ref, carry_ref):
        rows = lax.broadcasted_iota(jnp.int32, (SUB, cb), 0)

        @pl.when(pl.program_id(2) == 0)
        def _():
            carry_ref[...] = jnp.zeros_like(carry_ref)

        def run(reverse):
            def step(i, h):
                r = pl.multiple_of(((ntile - 1 - i) if reverse else i) * SUB, SUB)
                A, B = _tile_scan(a_ref[pl.ds(r, SUB), :], b_ref[pl.ds(r, SUB), :], rows, reverse)
                H = A * h + B
                h_ref[pl.ds(r, SUB), :] = H
                return H[0:1, :] if reverse else H[SUB - 1:SUB, :]

            carry_ref[...] = lax.fori_loop(0, ntile, step, carry_ref[...], unroll=2)

        @pl.when(pl.program_id(1) == 0)
        def _():
            run(False)

        @pl.when(pl.program_id(1) == 1)
        def _():
            run(True)

    spec = pl.BlockSpec((None, tc, cb), lambda j, d, t: (d, chunk(d, t), j))
    return pl.pallas_call(
        body, name=name, grid=(W // cb, 2, nch), in_specs=[spec, spec], out_specs=spec,
        out_shape=_sds((2, n, W), F32), scratch_shapes=[pltpu.VMEM((1, cb), F32)],
        compiler_params=_params([((tc, cb), F32)] * 3, [((1, cb), F32)], ("parallel", "arbitrary", "arbitrary")),
    )(a_all, b_all)


def _scan_bwd(a_all, h_all, dya, lc, name):
    _, n, W = a_all.shape
    cb = _tile(W, CB_SCAN, LANE)
    tc, nch, ncc = _scan_chunks(n, lc)
    ntile = tc // SUB
    nl = nch - ncc

    def chunk(d, t):
        return jnp.where(d == 0, nch - 1 - t, jnp.where(t < nl, ncc + t, t - nl))

    def neighbour(d, t):
        c = chunk(d, t)
        below = jnp.maximum(c * ntile - 1, 0)
        above = jnp.where(c == nch - 1, 0, jnp.minimum((c + 1) * ntile, nch * ntile - 1))
        return jnp.where(d == 0, below, above)

    def body(a_ref, h_ref, hn_ref, g_ref, da_ref, db_ref, mu_ref):
        rows = lax.broadcasted_iota(jnp.int32, (SUB, cb), 0)
        d, t = pl.program_id(1), pl.program_id(2)
        c = chunk(d, t)
        has_g = c >= ncc

        @pl.when(t == 0)
        def _():
            mu_ref[...] = jnp.zeros_like(mu_ref)

        def tile(ref, j):
            return ref[pl.ds(pl.multiple_of(j * SUB, SUB), SUB), :]

        def run(up):
            if up:
                edge = jnp.where(c == ncc - 1, 0.0, hn_ref[0:1, :])
            else:
                edge = jnp.where(c > 0, hn_ref[SUB - 1:SUB, :], 0.0)

            def step(i, mu):
                j = i if up else ntile - 1 - i
                a_t = tile(a_ref, j)
                g_t = jnp.where(has_g, tile(g_ref, j), 0.0)
                if up:
                    ap = jnp.where(rows >= 1, pltpu.roll(a_t, 1, 0), 1.0)
                    nb_row = jnp.where(j < ntile - 1, tile(h_ref, jnp.minimum(j + 1, ntile - 1))[0:1, :], edge)
                    hprev = jnp.where(rows < SUB - 1, pltpu.roll(tile(h_ref, j), SUB - 1, 0), nb_row)
                else:
                    ap = jnp.where(rows < SUB - 1, pltpu.roll(a_t, SUB - 1, 0), 1.0)
                    nb_row = jnp.where(j > 0, tile(h_ref, jnp.maximum(j - 1, 0))[SUB - 1:SUB, :], edge)
                    hprev = jnp.where(rows >= 1, pltpu.roll(tile(h_ref, j), 1, 0), nb_row)
                A, B = _tile_scan(ap, g_t, rows, not up)
                lam = A * mu + B
                r = pl.multiple_of(j * SUB, SUB)
                da_ref[pl.ds(r, SUB), :] = lam * hprev
                db_ref[pl.ds(r, SUB), :] = lam
                return a_t[SUB - 1:SUB, :] * lam[SUB - 1:SUB, :] if up else a_t[0:1, :] * lam[0:1, :]

            mu_ref[...] = lax.fori_loop(0, ntile, step, mu_ref[...], unroll=2)

        @pl.when(d == 0)
        def _():
            run(False)

        @pl.when(d == 1)
        def _():
            run(True)

    spec = pl.BlockSpec((None, tc, cb), lambda j, d, t: (d, chunk(d, t), j))
    n_spec = pl.BlockSpec((None, SUB, cb), lambda j, d, t: (d, neighbour(d, t), j))
    g_spec = pl.BlockSpec((tc, cb), lambda j, d, t: (jnp.maximum(chunk(d, t) - ncc, 0), j))
    blocks = [((tc, cb), F32)] * 5 + [((SUB, cb), F32)]
    return pl.pallas_call(
        body, name=name, grid=(W // cb, 2, nch), in_specs=[spec, spec, n_spec, g_spec], out_specs=[spec, spec],
        out_shape=[_sds((2, n, W), F32)] * 2, scratch_shapes=[pltpu.VMEM((1, cb), F32)],
        compiler_params=_params(blocks, [((1, cb), F32)], ("parallel", "arbitrary", "arbitrary")),
    )(a_all, h_all, h_all, dya)


def _pool_window(v, w, tl, cb, transpose):
    half = w // 2
    pos = lax.broadcasted_iota(jnp.int32, (tl, cb), 0) % GRID_W
    cnt = (jnp.minimum(pos + half - 1, GRID_W - 1) - jnp.maximum(pos - half, 0) + 1).astype(F32)
    src = v / cnt if transpose else v

    def run_sum(s, step):
        span = 1
        while span < half:
            ok = (pos + span < GRID_W) if step > 0 else (pos - span >= 0)
            s = s + jnp.where(ok, pltpu.roll(s, (-step * span) % tl, 0), 0.0)
            span *= 2
        return s

    ahead, behind = run_sum(src, 1), run_sum(src, -1)
    if transpose:
        return behind + jnp.where(pos + 1 < GRID_W, pltpu.roll(ahead, tl - 1, 0), 0.0) - v
    return (ahead + jnp.where(pos >= 1, pltpu.roll(behind, 1, 0), 0.0)) / cnt - v


def _pool_z(src, row0, col0, L, W, transpose, dproj, name):
    G = len(POOL_WINDOWS)
    pd = W // G
    tl = _tile(L, TL, GRID_W)
    cb = _tile(pd, CB_POOL, LANE)
    assert row0 % tl == 0 and col0 % cb == 0
    rb, cbk = row0 // tl, col0 // cb
    nj = pd // cb

    def body(x_ref, *rest):
        o_ref = rest[-1]
        for gi, w in enumerate(POOL_WINDOWS):
            @pl.when(pl.program_id(0) == gi)
            def _(w=w):
                o_ref[...] = _pool_window(x_ref[...], w, tl, cb, transpose).astype(o_ref.dtype)

    plain = pl.BlockSpec((tl, cb), lambda g, i, j: (i, g * nj + j))
    window = pl.BlockSpec((tl, cb), lambda g, i, j: (i + rb, cbk + g * nj + j))
    blocks = [((tl, cb), F32), ((tl, cb), MXU)]
    if transpose:
        return pl.pallas_call(
            body, name=name, grid=(G, L // tl, nj), in_specs=[plain, ANY], out_specs=window,
            out_shape=_sds(dproj.shape, dproj.dtype), input_output_aliases={1: 0},
            compiler_params=_params(blocks, dims=("parallel",) * 3),
        )(src, dproj)
    return pl.pallas_call(
        body, name=name, grid=(G, L // tl, nj), in_specs=[window], out_specs=plain,
        out_shape=_sds((L, W), MXU),
        compiler_params=_params(blocks, dims=("parallel",) * 3),
    )(src)


def _mix_fwd(hs, proj_all, ypre, b_pool, pool_scale, lc, name):
    L, W = ypre.shape
    tl = _tile(L, TL, SUB16)
    cb = _tile(W, CB_MIX, LANE)
    nj = W // cb
    assert lc % tl == 0
    rb = lc // tl

    def body(hs_ref, ga_ref, yp_ref, gb_ref, bp_ref, ps_ref, o_ref):
        p = pl.program_id(2)

        @pl.when(p == 0)
        def _():
            g = ga_ref[...]
            o_ref[...] = ((hs_ref[0] + hs_ref[1]) * (g * _sigmoid(g))).astype(o_ref.dtype)

        @pl.when(p == 1)
        def _():
            g = gb_ref[...]
            yb = (yp_ref[...] + bp_ref[...]) * ps_ref[...]
            o_ref[...] = (yb * (g * _sigmoid(g))).astype(o_ref.dtype)

    vec = pl.BlockSpec((1, cb), lambda i, j, p: (0, j))
    blocks = [((2, tl, cb), F32)] + [((tl, cb), F32)] * 3 + [((tl, cb), MXU)]
    return pl.pallas_call(
        body, name=name, grid=(L // tl, nj, 2),
        in_specs=[pl.BlockSpec((2, tl, cb), lambda i, j, p: (0, i + rb, j)),
                  pl.BlockSpec((tl, cb), lambda i, j, p: (i + rb, 2 * nj + j)),
                  pl.BlockSpec((tl, cb), lambda i, j, p: (i, j)),
                  pl.BlockSpec((tl, cb), lambda i, j, p: (i + rb, 3 * nj + j)), vec, vec],
        out_specs=pl.BlockSpec((tl, cb), lambda i, j, p: (i, p * nj + j)),
        out_shape=_sds((L, 2 * W), MXU),
        compiler_params=_params(blocks, dims=("parallel", "parallel", "arbitrary")),
    )(hs, proj_all, ypre, proj_all, b_pool, pool_scale)


def _dsilu(g, sg):
    return sg * (1.0 + g * (1.0 - sg))


def _mixa_bwd(dmixed, hs, proj_all, dproj, lc, W, name):
    L = dmixed.shape[0]
    tl = _tile(L, TL, SUB16)
    cb = _tile(W, CB_MIX, LANE)
    nj = W // cb
    rb = lc // tl

    def body(dm_ref, hs_ref, ga_ref, dp_in, dya_ref, dga_ref):
        del dp_in
        g = ga_ref[...]
        sg = _sigmoid(g)
        dm = dm_ref[...]
        dya_ref[...] = dm * (g * sg)
        dga_ref[...] = (dm * (hs_ref[0] + hs_ref[1]) * _dsilu(g, sg)).astype(dga_ref.dtype)

    blocks = [((tl, cb), F32)] * 3 + [((2, tl, cb), F32), ((tl, cb), MXU)]
    return pl.pallas_call(
        body, name=name, grid=(L // tl, nj),
        in_specs=[pl.BlockSpec((tl, cb), lambda i, j: (i, j)),
                  pl.BlockSpec((2, tl, cb), lambda i, j: (0, i + rb, j)),
                  pl.BlockSpec((tl, cb), lambda i, j: (i + rb, 2 * nj + j)), ANY],
        out_specs=[pl.BlockSpec((tl, cb), lambda i, j: (i, j)),
                   pl.BlockSpec((tl, cb), lambda i, j: (i + rb, 2 * nj + j))],
        out_shape=[_sds((L, W), F32), _sds(dproj.shape, dproj.dtype)],
        input_output_aliases={3: 1},
        compiler_params=_params(blocks, dims=("parallel", "parallel")),
    )(dmixed, hs, proj_all, dproj)


def _mixb_bwd(dmixed, ypre, proj_all, b_pool, pool_scale, dproj, lc, W, name):
    L = dmixed.shape[0]
    tl = _tile(L, TL, SUB16)
    cb = _tile(W, CB_MIX, LANE)
    nj = W // cb
    rb = lc // tl

    def body(dm_ref, yp_ref, gb_ref, bp_ref, ps_ref, dp_in, dyp_ref, dgb_ref, gbp_ref, gps_ref):
        del dp_in
        i = pl.program_id(1)

        @pl.when(i == 0)
        def _():
            gbp_ref[...] = jnp.zeros_like(gbp_ref)
            gps_ref[...] = jnp.zeros_like(gps_ref)

        g = gb_ref[...]
        sg = _sigmoid(g)
        dm = dm_ref[...]
        yp = yp_ref[...] + bp_ref[...]
        ps = ps_ref[...]
        dyb = dm * (g * sg)
        dyp = dyb * ps
        dgb_ref[...] = (dm * (yp * ps) * _dsilu(g, sg)).astype(dgb_ref.dtype)
        dyp_ref[...] = dyp.astype(dyp_ref.dtype)
        gbp_ref[...] += jnp.sum(dyp, axis=0, keepdims=True)
        gps_ref[...] += jnp.sum(dyb * yp, axis=0, keepdims=True)

    vec = pl.BlockSpec((1, cb), lambda j, i: (0, j))
    blocks = [((tl, cb), F32)] * 3 + [((tl, cb), MXU)] * 2 + [((1, cb), F32)] * 4
    return pl.pallas_call(
        body, name=name, grid=(nj, L // tl),
        in_specs=[pl.BlockSpec((tl, cb), lambda j, i: (i, nj + j)),
                  pl.BlockSpec((tl, cb), lambda j, i: (i, j)),
                  pl.BlockSpec((tl, cb), lambda j, i: (i + rb, 3 * nj + j)), vec, vec, ANY],
        out_specs=[pl.BlockSpec((tl, cb), lambda j, i: (i, j)),
                   pl.BlockSpec((tl, cb), lambda j, i: (i + rb, 3 * nj + j)), vec, vec],
        out_shape=[_sds((L, W), MXU), _sds(dproj.shape, dproj.dtype), _sds((1, W), F32), _sds((1, W), F32)],
        input_output_aliases={5: 1},
        compiler_params=_params(blocks, dims=("parallel", "arbitrary")),
    )(dmixed, ypre, proj_all, b_pool, pool_scale, dproj)


def _dproj_init(n, lc, W, name):
    cb = _tile(W, CB_MIX, LANE)
    nj = W // cb

    def body(o_ref):
        o_ref[...] = jnp.zeros_like(o_ref)

    return pl.pallas_call(
        body, name=name, grid=(3 * nj,), in_specs=[],
        out_specs=pl.BlockSpec((lc, cb), lambda j: (0, nj + j)),
        out_shape=_sds((n, 4 * W), MXU),
        compiler_params=_params([((lc, cb), MXU)], dims=("parallel",)),
    )()


def _final(x2, out, tgt, gate, gfin, name):
    L, D = x2.shape
    tl = _tile(L, TL_FINAL, SUB16)

    def body(x_ref, o_ref, t_ref, gate_ref, g_ref, dout_ref, dxn_ref, loss_ref, ggf_ref, dgate_ref):
        i = pl.program_id(0)

        @pl.when(i == 0)
        def _():
            loss_ref[...] = jnp.zeros_like(loss_ref)
            ggf_ref[...] = jnp.zeros_like(ggf_ref)
            dgate_ref[...] = jnp.zeros_like(dgate_ref)

        o = o_ref[...]
        gate_v = gate_ref[...]
        gv = g_ref[...]
        xn = x_ref[...] + gate_v * o
        s = lax.rsqrt(jnp.mean(xn * xn, axis=-1, keepdims=True) + EPS)
        xh = xn * s
        err = xh * gv - t_ref[...]
        tok = jnp.mean(err * err, axis=-1, keepdims=True)
        loss_ref[...] += 0.5 * jnp.sum(tok, axis=0, keepdims=True)
        dy = err / D
        ggf_ref[...] += jnp.sum(dy * xh, axis=0, keepdims=True)
        dxh = dy * gv
        dxn = s * (dxh - xh * jnp.mean(dxh * xh, axis=-1, keepdims=True))
        dgate_ref[...] += jnp.sum(dxn * o, axis=0, keepdims=True)
        dout_ref[...] = (gate_v * dxn).astype(dout_ref.dtype)
        dxn_ref[...] = dxn

    row = pl.BlockSpec((tl, D), lambda i: (i, 0))
    vec = pl.BlockSpec((1, D), lambda i: (0, 0))
    blocks = [((tl, D), F32)] * 4 + [((tl, D), MXU)] + [((1, D), F32)] * 4
    return pl.pallas_call(
        body, name=name, grid=(L // tl,), in_specs=[row, row, row, vec, vec],
        out_specs=[row, row, pl.BlockSpec((1, 1), lambda i: (0, 0)), vec, vec],
        out_shape=[_sds((L, D), MXU), _sds((L, D), F32), _sds((1, 1), F32), _sds((1, D), F32), _sds((1, D), F32)],
        compiler_params=_params(blocks, dims=("arbitrary",)),
    )(x2, out, tgt, gate, gfin)


def _adamw_parts(w2, parts, m2, v2, name):
    R, C = w2.shape
    nh = len(parts)
    ch = C // nh
    tr = _tile(R, max(SUB16, (512 * 1024) // (ch * (nh + 1))), SUB16)

    def body(w_ref, *rest):
        p_refs = rest[:nh]
        m_ref, v_ref, g_ref, d_ref, nm_ref, nv_ref = rest[nh:]
        for q in range(nh):
            @pl.when(pl.program_id(1) == q)
            def _(p_ref=p_refs[q]):
                g = p_ref[0].astype(F32)
                for p in range(1, NDEV):
                    g = g + p_ref[p].astype(F32)
                delta, nm, nv = _adam(w_ref[...], g, m_ref[...], v_ref[...])
                g_ref[...] = g
                d_ref[...] = delta
                nm_ref[...] = nm
                nv_ref[...] = nv

    blk = pl.BlockSpec((tr, ch), lambda i, h: (i, h))
    p_spec = pl.BlockSpec((NDEV, tr, ch), lambda i, h: (0, i, 0))
    blocks = [((tr, ch), F32)] * 7 + [((NDEV, tr, ch), parts[0].dtype)] * nh
    return pl.pallas_call(
        body, name=name, grid=(R // tr, nh),
        in_specs=[blk] + [p_spec] * nh + [blk, blk],
        out_specs=[blk] * 4, out_shape=[_sds((R, C), F32)] * 4,
        compiler_params=_params(blocks, dims=("parallel", "arbitrary")),
    )(w2, *parts, m2, v2)


def _small_sum(vs, ga, gc, name):
    ns, nm = vs.shape[1], ga.shape[1]

    def body(v_ref, ga_ref, gc_ref, tot_ref, gb_ref):
        tot = v_ref[0:1, :]
        gb = ga_ref[0:1, :]
        for p in range(1, NDEV):
            tot = tot + v_ref[p:p + 1, :]
            gb = gb + ga_ref[p:p + 1, :]
        for p in range(NDEV):
            gb = gb + gc_ref[p:p + 1, :]
        tot_ref[...] = tot
        gb_ref[...] = gb

    blocks = [((NDEV, ns), F32), ((NDEV, nm), F32), ((NDEV, nm), F32), ((1, ns), F32), ((1, nm), F32)]
    return pl.pallas_call(
        body, name=name, out_shape=[_sds((1, ns), F32), _sds((1, nm), F32)],
        compiler_params=_params(blocks),
    )(vs, ga, gc)


def _adamw_small(g_raw, w, m, v, lam_range, cctx_range, name):
    npk = w.shape[1]

    def body(g_ref, w_ref, m_ref, v_ref, go_ref, d_ref, nm_ref, nv_ref):
        wv = w_ref[...]
        g = g_ref[...]
        idx = lax.broadcasted_iota(jnp.int32, (1, npk), 1)
        in_lam = (idx >= lam_range[0]) & (idx < lam_range[1])
        in_cc = (idx >= cctx_range[0]) & (idx < cctx_range[1])
        sg = _sigmoid_small(wv)
        g = jnp.where(in_lam, g * (LRU_C * _sigmoid_small(-wv)), jnp.where(in_cc, g * _dsilu(wv, sg), g))
        delta, nm, nv = _adam(wv, g, m_ref[...], v_ref[...])
        go_ref[...] = g
        d_ref[...] = delta
        nm_ref[...] = nm
        nv_ref[...] = nv

    return pl.pallas_call(
        body, name=name, out_shape=[_sds((1, npk), F32)] * 4,
        compiler_params=_params([((1, npk), F32)] * 8),
    )(g_raw, w, m, v)


def _pack(pieces):
    return jnp.concatenate([p.reshape(1, -1) for p in pieces], axis=1)


def kernel(x, c, ctx, c_ctx, w_ada, b_ada, g_norm, w_in, conv_w, conv_b, lru_lambda, w_rgate, b_rgate, w_igate, b_igate, w_pool, b_pool, pool_scale, w_out, g_final, loss_target, m_c_ctx, m_w_ada, m_b_ada, m_g_norm, m_w_in, m_conv_w, m_conv_b, m_lru_lambda, m_w_rgate, m_b_rgate, m_w_igate, m_b_igate, m_w_pool, m_b_pool, m_pool_scale, m_w_out, m_g_final, v_c_ctx, v_w_ada, v_b_ada, v_g_norm, v_w_in, v_conv_w, v_conv_b, v_lru_lambda, v_w_rgate, v_b_rgate, v_w_igate, v_b_igate, v_w_pool, v_b_pool, v_pool_scale, v_w_out, v_g_final):
    L, D = x.shape[1], x.shape[2]
    lc = ctx.shape[1]
    n = lc + L
    W = conv_b.shape[1]
    heads, hd = w_rgate.shape[2], w_rgate.shape[4]
    G, pd = w_pool.shape[1], w_pool.shape[3]
    na = w_ada.shape[2]
    nb = w_in.shape[2]
    ws = W // NDEV
    me = 4 * lax.axis_index("x") + 2 * lax.axis_index("y") + lax.axis_index("c")

    nbp = nb // WIN_PARTS
    w_in_parts = [w_in[0, :, q * nbp:(q + 1) * nbp].astype(MXU) for q in range(WIN_PARTS)]
    (win_0, cw_all, lam_all, br_all, bi_all, c_all) = _all_gather(
        [w_in_parts[0], conv_w[0], lru_lambda[0], b_rgate[0], b_igate[0], c], "gather_w_in")
    win = [win_0]
    cw = cw_all.transpose(1, 0, 2).reshape(4, W)
    lam = lam_all.transpose(1, 0, 2).reshape(2, W)
    br = br_all.transpose(1, 0, 2).reshape(2, W)
    bi = bi_all.transpose(1, 0, 2).reshape(2, W)

    cc = jnp.concatenate([c_all.reshape(NDEV, D), c_ctx.reshape(1, D), jnp.zeros((NDEV - 1, D), F32)], axis=0)
    b_loc = lax.dynamic_slice(b_ada, (0, me * na), (1, na))
    mod_loc, s_all = _ada_fwd(cc, w_ada[0], b_loc, "ada_fwd")
    (mod_all,) = _all_gather([mod_loc], "gather_mod")
    gate_w = [w_rgate[0].astype(MXU), w_igate[0].astype(MXU)]
    pool_w, out_w = [w_pool[0].astype(MXU)], [w_out[0].astype(MXU)]
    tok = mod_all
    sent_win = []
    for q in range(1, WIN_PARTS):
        part = [w_in_parts[q]]
        sent_win.append(_send_start(part, _place(part, False, f"place_w_in_{q}", [tok]), "level1", f"start_w_in_{q}"))
        tok = sent_win[-1][4]
    sent_gw = _send_start(gate_w, _place(gate_w, False, "place_gate_w", [tok]), False, "start_gate_w")
    sent_pw = _send_start(pool_w, _place(pool_w, False, "place_pool_w", [sent_gw[4]]), False, "start_pool_w")
    sent_ow = _send_start(out_w, _place(out_w, False, "place_out_w", [sent_pw[4]]), False, "start_out_w")
    mod = mod_all.transpose(1, 0, 2).reshape(2 * NDEV, NDEV * na)
    mod_me = lax.dynamic_slice(mod, (me, 0), (1, 3 * D))
    shift, scale, gate = mod_me[:, :D], mod_me[:, D:2 * D], mod_me[:, 2 * D:]
    shift = _tie(shift, [sent_gw[4], sent_ow[4]], "tie_weights")
    shift_c, scale_c = mod[NDEV:NDEV + 1, :D], mod[NDEV:NDEV + 1, D:2 * D]

    x2, ctx2, tgt = x[0], ctx[0], loss_target[0]
    gfin = g_final.reshape(1, D)
    h_all = _norm_mod(x2, g_norm, shift, scale, n, lc, None, "norm_lat")
    h_all = _norm_mod(ctx2, g_norm, shift_c, scale_c, n, 0, h_all, "norm_ctx")
    proj_all = _mm_proj(h_all, win[0], 0, WIN_PARTS, None, "mm_proj_0")
    for q in range(1, WIN_PARTS):
        lands = _send_wait(sent_win[q - 1], proj_all, "level1", f"wait_w_in_{q}")
        passed = _send_start([], lands, "level2", f"pass_w_in_{q}")
        win.append(_send_wait(passed, proj_all, "level2", f"wait_pass_w_in_{q}")[0])
        proj_all = _mm_proj(h_all, win[q], q, WIN_PARTS, proj_all, f"mm_proj_{q}")
    u_all = _conv_fwd(proj_all, cw, conv_b, lc, W, "conv_fwd")
    wr_all, wi_all = _send_wait(sent_gw, u_all, False, "wait_gate_w")
    wr = wr_all.transpose(1, 2, 0, 3, 4).reshape(2, heads, hd, hd)
    wi = wi_all.transpose(1, 2, 0, 3, 4).reshape(2, heads, hd, hd)
    a_all, b_all = _gates_fwd(u_all, wr, wi, br, bi, lam, "gates_fwd")
    hs = _scan_fwd(a_all, b_all, lc, "scan_fwd")
    z = _pool_z(proj_all, lc, W, L, W, False, None, "pool_z")
    (wpool_all,) = _send_wait(sent_pw, hs, False, "wait_pool_w")
    wpool = wpool_all.transpose(1, 0, 2, 3).reshape(G, pd, pd)
    ypre = _mm_group(z, wpool, "fwd", F32, "mm_pool")
    mixed = _mix_fwd(hs, proj_all, ypre, b_pool, pool_scale, lc, "mix_fwd")
    (wout_all,) = _send_wait(sent_ow, mixed, False, "wait_out_w")
    wout = wout_all.reshape(2 * W, D)
    out = _mm_plain(mixed, wout, NN, F32, "mm_out")
    d_out, dxn, loss_p, ggf, dgate = _final(x2, out, tgt, gate, gfin, "final")

    dmixed = _mm_plain(d_out, wout, NT, F32, "mm_dmixed")
    gwout = _mm_plain(mixed, d_out, TN_DIMS, MXU, "mm_gwout")
    ex_o = [gwout.reshape(NDEV, 2 * W // NDEV, D)]
    sent_o = _send_start(ex_o, _place(ex_o, True, "place_gwout"), True, "start_gwout")
    dproj = _dproj_init(n, lc, W, "dproj_init")
    dya, dproj = _mixa_bwd(dmixed, hs, proj_all, dproj, lc, W, "mixa_bwd")
    dypre, dproj, gbp, gps = _mixb_bwd(dmixed, ypre, proj_all, _tie(b_pool, [sent_o[4]], "tie_gwout"), pool_scale,
                                       dproj, lc, W, "mixb_bwd")
    dz = _mm_group(dypre, wpool, "bwd", F32, "mm_dz")
    gwpool = _mm_group(z, dypre, "wgrad", MXU, "mm_gwpool")
    dproj = _pool_z(dz, lc, W, L, W, True, dproj, "pool_z_bwd")
    da, db = _scan_bwd(a_all, hs, dya, lc, "scan_bwd")
    du, gwr, gwi, gbr, gbi, gcl = _gates_bwd(u_all, da, db, wr, wi, br, bi, lam, "gates_bwd")
    ex_s = [gwpool.reshape(G, NDEV, pd // NDEV, pd).transpose(1, 0, 2, 3),
            gwr.reshape(2, heads, NDEV, hd // NDEV, hd).transpose(2, 0, 1, 3, 4),
            gwi.reshape(2, heads, NDEV, hd // NDEV, hd).transpose(2, 0, 1, 3, 4)]
    sent_s = _send_start(ex_s, _place(ex_s, True, "place_gsmall"), True, "start_gsmall")
    dproj, gcw, gcb = _conv_bwd(du, proj_all, _tie(cw, [sent_s[4]], "tie_gsmall"), dproj, lc, W, "conv_bwd")
    h_t = _transpose(h_all, "transpose_h")
    sent_i, tok = [], None
    for q in range(GWIN_PARTS):
        part = _mm_gwin(h_t, dproj, nb, q, GWIN_PARTS, f"mm_gwin_{q}", dep=tok)
        part = pltpu.with_memory_space_constraint(part, pltpu.HBM)
        sent_i.append(_send_start([part], _place([part], True, f"place_gwin_{q}"), True, f"start_gwin_{q}"))
        tok = sent_i[-1][4]
    dh_all = _mm_dh(dproj, win, "mm_dh", tok)
    grad_x, dshift, dscale, ggn = _norm_bwd(x2, dh_all, lc, g_norm, scale, dxn, jnp.zeros((1, D), F32), "norm_bwd_lat")
    _, dshift_c, dscale_c, ggn = _norm_bwd(ctx2, dh_all, 0, g_norm, scale_c, None, ggn, "norm_bwd_ctx")

    dmod_me = jnp.concatenate([dshift, dscale, dgate], axis=1)
    dmod_c = jnp.concatenate([dshift_c, dscale_c, jnp.zeros((1, D), F32)], axis=1)
    smalls = [ggf, ggn, gcw, gcb, gcl, gbr, gbi, gbp, gps, jnp.pad(loss_p, ((0, 0), (0, LANE - 1)))]
    sizes = [s.size for s in smalls]
    small_all, dmod_all, dmodc_all = _all_gather([_pack(smalls), dmod_me, dmod_c], "gather_small")
    ga = lax.dynamic_slice(dmod_all.reshape(NDEV, 3 * D), (0, me * na), (NDEV, na))
    gc = lax.dynamic_slice(dmodc_all.reshape(NDEV, 3 * D), (0, me * na), (NDEV, na))
    g_wada, d_wada, nm_wada, nv_wada, pc = _ada_bwd(s_all, ga, gc, w_ada[0], m_w_ada[0], v_w_ada[0], "ada_bwd")
    (pc_all,) = _all_gather([pc[0:1]], "gather_cctx")
    tot, gb_ada = _small_sum(
        jnp.concatenate([small_all.reshape(NDEV, -1), pc_all.reshape(NDEV, D)], axis=1),
        dmod_all.reshape(NDEV, 3 * D), dmodc_all.reshape(NDEV, 3 * D), "small_sum")
    offs = [0]
    for s in sizes + [D]:
        offs.append(offs[-1] + s)
    t_ggf, t_ggn, t_gcw, t_gcb, t_gcl, t_gbr, t_gbi, t_gbp, t_gps, t_loss, t_pc = [
        tot[:, offs[i]:offs[i + 1]] for i in range(len(offs) - 1)]

    def shard(t, rows):
        return lax.dynamic_slice(t.reshape(rows, W), (0, me * ws), (rows, ws))

    def big(wv, parts, mv, vv, name):
        shp = wv.shape
        C = shp[-1]
        if not isinstance(parts, list):
            parts = [parts]
        parts = [p.reshape(NDEV, -1, C // len(parts)) for p in parts]
        outs = _adamw_parts(wv.reshape(-1, C), parts, mv.reshape(-1, C), vv.reshape(-1, C), name)
        return [o.reshape(shp) for o in outs]

    (recv_o,) = _send_wait(sent_o, tot, True, "wait_gwout")
    recv_p, recv_r, recv_i = _send_wait(sent_s, tot, True, "wait_gsmall")
    r_wout = big(w_out, recv_o, m_w_out, v_w_out, "adamw_w_out")
    r_wpool = big(w_pool, recv_p, m_w_pool, v_w_pool, "adamw_w_pool")
    r_wr = big(w_rgate, recv_r, m_w_rgate, v_w_rgate, "adamw_w_rgate")
    r_wi = big(w_igate, recv_i, m_w_igate, v_w_igate, "adamw_w_igate")
    r_wada = [o.reshape(w_ada.shape) for o in (g_wada, d_wada, nm_wada, nv_wada)]

    names = ["c_ctx", "b_ada", "g_norm", "conv_w", "conv_b", "lru_lambda", "b_rgate", "b_igate", "b_pool",
             "pool_scale", "g_final"]
    sw = [c_ctx, b_ada, g_norm, conv_w, conv_b, lru_lambda, b_rgate, b_igate, b_pool, pool_scale, g_final]
    sm = [m_c_ctx, m_b_ada, m_g_norm, m_conv_w, m_conv_b, m_lru_lambda, m_b_rgate, m_b_igate, m_b_pool,
          m_pool_scale, m_g_final]
    sv = [v_c_ctx, v_b_ada, v_g_norm, v_conv_w, v_conv_b, v_lru_lambda, v_b_rgate, v_b_igate, v_b_pool,
          v_pool_scale, v_g_final]
    sg = [t_pc, gb_ada, t_ggn, shard(t_gcw, 4), t_gcb, shard(t_gcl, 2), shard(t_gbr, 2), shard(t_gbi, 2), t_gbp,
          t_gps, t_ggf]
    poffs = [0]
    for wv in sw:
        poffs.append(poffs[-1] + wv.size)
    lam_range = (poffs[5], poffs[6])
    cctx_range = (poffs[0], poffs[1])
    small_out = _adamw_small(_pack(sg), _pack(sw), _pack(sm), _pack(sv), lam_range, cctx_range, "adamw_small")
    recv_w = [_send_wait(sent_i[q], small_out[0], True, f"wait_gwin_{q}")[0] for q in range(GWIN_PARTS)]
    r_win = big(w_in, recv_w, m_w_in, v_w_in, "adamw_w_in")
    r_small = {}
    for i, nm in enumerate(names):
        r_small[nm] = [o[:, poffs[i]:poffs[i + 1]].reshape(sw[i].shape) for o in small_out]

    res = dict(r_small)
    res.update(w_ada=r_wada, w_in=r_win, w_rgate=r_wr, w_igate=r_wi, w_pool=r_wpool, w_out=r_wout)
    order = ["c_ctx", "w_ada", "b_ada", "g_norm", "w_in", "conv_w", "conv_b", "lru_lambda", "w_rgate", "b_rgate",
             "w_igate", "b_igate", "w_pool", "b_pool", "pool_scale", "w_out", "g_final"]
    loss = t_loss[0, 0]
    outs = [loss, grad_x.reshape(x.shape)]
    for q in range(4):
        outs += [res[nm][q] for nm in order]
    return tuple(outs)
```

```python
import functools

import jax
import jax.numpy as jnp
from jax import lax
from jax.experimental import pallas as pl
from jax.experimental.pallas import tpu as pltpu

NDEV = 8
GRID_W = 64
POOL_WINDOWS = (2, 4, 8, 16)
LRU_C = 8.0
EPS = 1e-6
ADAM_LR = 0.001
ADAM_B1 = 0.9
ADAM_B2 = 0.999
ADAM_EPS = 1e-08
ADAM_WD = 0.01
ADAM_STEP = 10

F32 = jnp.float32
MXU = jnp.bfloat16

VMEM_BYTES = 64 * 1024 * 1024
VMEM_SLACK = 8 * 1024 * 1024
SUB = 8
SUB16 = 16
LANE = 128

TM = 1152
TN = 1024
TK = 2048
TL = 256
TL_FINAL = 128
TL_GATES = 1088
CB_POOL = 1024
CB_SEQ = 256
CB_SCAN = 4096
CB_MIX = 4096
TR_CONV = 576
GWIN_PARTS = 4
WIN_PARTS = 4

MESH_ID = pl.DeviceIdType.MESH


def _tile(n, pref, align):
    if n <= pref:
        return n
    for t in range(pref - pref % align, 0, -align):
        if n % t == 0:
            return t
    return n


def _nbytes(shape, dtype):
    n = 1
    for s in shape:
        if s is not None:
            n *= s
    return n * jnp.dtype(dtype).itemsize


def _params(blocks, scratch=(), dims=None):
    need = 2 * sum(_nbytes(s, d) for s, d in blocks) + sum(_nbytes(s, d) for s, d in scratch) + VMEM_SLACK
    kw = dict(vmem_limit_bytes=int(min(max(need, 2 * VMEM_SLACK), VMEM_BYTES - VMEM_SLACK // 2)))
    if dims is not None:
        kw["dimension_semantics"] = dims
    return pltpu.CompilerParams(**kw)


def _sds(shape, dtype):
    return jax.ShapeDtypeStruct(tuple(shape), dtype)


ANY = pl.BlockSpec(memory_space=pl.ANY)


def _ids():
    return lax.axis_index("x"), lax.axis_index("y"), lax.axis_index("c")


def _sigmoid(v):
    return 0.5 * jnp.tanh(0.5 * v) + 0.5


def _sigmoid_small(v):
    return jax.nn.sigmoid(v)


def _softplus(v):
    return jnp.maximum(v, 0.0) + jnp.log1p(jnp.exp(-jnp.abs(v)))


def _all_gather(xs, name):
    n = len(xs)

    def body(*refs):
        x_refs, o_refs = refs[:n], refs[n:2 * n]
        send_sems, recv_sems, local_sems = refs[2 * n:]
        x, y, c = _ids()
        me, sibling = (x, y, c), (x, y, 1 - c)
        chips = [(1 - x, y), (x, 1 - y), (1 - x, 1 - y)]

        def slot(a, p):
            return o_refs[a].at[4 * p[0] + 2 * p[1] + p[2]]

        def copy(a, k, block, to, src=None):
            return pltpu.make_async_remote_copy(
                src_ref=slot(a, block) if src is None else src, dst_ref=slot(a, block),
                send_sem=send_sems.at[7 * a + k], recv_sem=recv_sems.at[7 * a + k],
                device_id=to, device_id_type=MESH_ID)

        mine, first, passed = [], [], []
        for a in range(n):
            m = pltpu.make_async_copy(x_refs[a], slot(a, me), local_sems.at[a])
            m.start()
            mine.append(m)
            f = [copy(a, 0, me, sibling, src=x_refs[a])]
            f += [copy(a, 1 + j, me, (*chip, c), src=x_refs[a]) for j, chip in enumerate(chips)]
            for cp in f:
                cp.start()
            first += f
        for a in range(n):
            for j, chip in enumerate(chips):
                copy(a, 1 + j, (*chip, c), me).wait_recv()
                p = copy(a, 4 + j, (*chip, c), sibling)
                p.start()
                passed.append(p)
        for a in range(n):
            copy(a, 0, sibling, me).wait_recv()
            for j, chip in enumerate(chips):
                copy(a, 4 + j, (*chip, 1 - c), me).wait_recv()
        for cp in first + passed:
            cp.wait_send()
        for m in mine:
            m.wait()

    return pl.pallas_call(
        body, name=name,
        out_shape=[_sds((NDEV,) + v.shape, v.dtype) for v in xs],
        in_specs=[ANY] * n, out_specs=[ANY] * n,
        scratch_shapes=[pltpu.SemaphoreType.DMA((7 * n,)), pltpu.SemaphoreType.DMA((7 * n,)),
                        pltpu.SemaphoreType.DMA((n,))],
    )(*xs)


HBM = pl.BlockSpec(memory_space=pltpu.HBM)
SEM = pl.BlockSpec(memory_space=pltpu.SEMAPHORE)
EFFECT = pltpu.SideEffectType.DATAFLOW_SIDE_EFFECTING


def _peers():
    x, y, c = _ids()
    out = []
    for k in range(1, NDEV):
        px = 1 - x if k & 4 else x
        py = 1 - y if k & 2 else y
        pc = 1 - c if k & 1 else c
        out.append(((px, py, pc), 4 * px + 2 * py + pc))
    return out, 4 * x + 2 * y + c


def _tie(v, deps, name):
    def body(v_ref, *rest):
        rest[-1][...] = v_ref[...]

    vmem = pl.BlockSpec(memory_space=pltpu.VMEM)
    return pl.pallas_call(
        body, name=name, out_shape=_sds(v.shape, v.dtype), in_specs=[vmem] + [ANY] * len(deps), out_specs=vmem,
    )(v, *deps)


def _place(srcs, from_slot, name, deps=()):
    n = len(srcs)
    blks = [v.shape[1:] if from_slot else v.shape for v in srcs]

    nd = len(deps)

    def body(*refs):
        s_refs, l_refs = refs[:n], refs[n + nd:2 * n + nd]
        bufs, sems = refs[2 * n + nd:3 * n + nd], refs[3 * n + nd]
        x, y, c = _ids()
        me = 4 * x + 2 * y + c
        ins = [pltpu.make_async_copy(s_refs[a].at[me] if from_slot else s_refs[a], bufs[a], sems.at[a])
               for a in range(n)]
        outs = [pltpu.make_async_copy(bufs[a], l_refs[a].at[me], sems.at[n + a]) for a in range(n)]
        for cp in ins:
            cp.start()
        for a in range(n):
            ins[a].wait()
            outs[a].start()
        for cp in outs:
            cp.wait()

    scratch = [(b, v.dtype) for b, v in zip(blks, srcs)]
    return pl.pallas_call(
        body, name=name, out_shape=[_sds((NDEV,) + b, v.dtype) for b, v in zip(blks, srcs)],
        in_specs=[ANY] * (n + nd), out_specs=[ANY] * n,
        scratch_shapes=[pltpu.VMEM(b, d) for b, d in scratch] + [pltpu.SemaphoreType.DMA((2 * n,))],
        compiler_params=_params([], scratch),
    )(*srcs, *deps)


SEND_PEERS = {True: 7, False: 7, "level1": 4, "level2": 3}


def _send_copies(s_refs, l_refs, ssem, rsem, mode, receiving):
    peers, me = _peers()
    x, y, c = _ids()
    sibling = (x, y, 1 - c)
    chips = [(1 - x, y), (x, 1 - y), (1 - x, 1 - y)]
    npeer = SEND_PEERS[mode]
    out = []
    for a in range(len(l_refs)):
        if mode == "level2":
            for k, (px, py) in enumerate(chips):
                slot = 4 * px + 2 * py + (1 - c if receiving else c)
                out.append(pltpu.make_async_remote_copy(
                    src_ref=l_refs[a].at[slot], dst_ref=l_refs[a].at[slot], send_sem=ssem.at[npeer * a + k],
                    recv_sem=rsem.at[npeer * a + k], device_id=sibling, device_id_type=MESH_ID))
            continue
        targets = peers
        if mode == "level1":
            targets = [(sibling, 4 * x + 2 * y + 1 - c)] + [((px, py, c), 4 * px + 2 * py + c) for px, py in chips]
        for k, (dev, idx) in enumerate(targets):
            out.append(pltpu.make_async_remote_copy(
                src_ref=s_refs[a].at[idx] if mode is True else s_refs[a],
                dst_ref=l_refs[a].at[idx if receiving else me],
                send_sem=ssem.at[npeer * a + k], recv_sem=rsem.at[npeer * a + k], device_id=dev, device_id_type=MESH_ID))
    return out


def _send_start(srcs, lands, mode, name):
    ns, n = len(srcs), len(lands)
    nsem = SEND_PEERS[mode] * n

    def body(*refs):
        s_refs, l_refs = refs[:ns], refs[ns:ns + n]
        ssem, rsem = refs[ns + n], refs[ns + n + 1]
        token = refs[-1]
        for send in _send_copies(s_refs, l_refs, ssem, rsem, mode, False):
            send.start()
        token[...] = jnp.zeros_like(token)

    bufs = list(srcs) + list(lands)
    outs = pl.pallas_call(
        body, name=name,
        out_shape=[pltpu.SemaphoreType.DMA((nsem,)), pltpu.SemaphoreType.DMA((nsem,))]
        + [pltpu.HBM(v.shape, v.dtype) for v in bufs] + [_sds((SUB, LANE), F32)],
        in_specs=[HBM] * (ns + n), out_specs=[SEM, SEM] + [HBM] * (ns + n) + [pl.BlockSpec(memory_space=pltpu.VMEM)],
        input_output_aliases={i: 2 + i for i in range(ns + n)},
        compiler_params=pltpu.CompilerParams(has_side_effects=EFFECT),
    )(*[pltpu.with_memory_space_constraint(v, pltpu.HBM) for v in bufs])
    return outs[0], outs[1], list(outs[2:2 + ns]), list(outs[2 + ns:2 + ns + n]), outs[-1]


def _send_wait(started, after, mode, name):
    ssem, rsem, srcs, lands, _ = started
    ns, n = len(srcs), len(lands)

    def body(*refs):
        s_refs, l_refs = refs[:ns], refs[ns:ns + n]
        ssem_ref, rsem_ref = refs[ns + n], refs[ns + n + 1]
        for recv in _send_copies(s_refs, l_refs, ssem_ref, rsem_ref, mode, True):
            recv.wait_send()
            recv.wait_recv()

    bufs = list(srcs) + list(lands)
    outs = pl.pallas_call(
        body, name=name, out_shape=[pltpu.HBM(v.shape, v.dtype) for v in bufs],
        in_specs=[HBM] * (ns + n) + [SEM, SEM, ANY], out_specs=[HBM] * (ns + n),
        input_output_aliases={i: i for i in range(ns + n)},
        compiler_params=pltpu.CompilerParams(has_side_effects=EFFECT),
    )(*bufs, ssem, rsem, after)
    return list(outs[ns:])


NN = (((1,), (0,)), ((), ()))
NT = (((1,), (1,)), ((), ()))
TN_DIMS = (((0,), (0,)), ((), ()))


def _mm(a, b, *, grid, a_spec, b_spec, o_spec, out_shape, acc_shape, dims, name, dep=None, fill=None):
    k_axis = len(grid) - 1
    nk = grid[k_axis]
    extra = [v for v in (dep, fill) if v is not None]
    aliases = {} if fill is None else {1 + len(extra): 0}

    def body(a_ref, b_ref, *rest):
        o_ref, acc_ref = rest[-2], rest[-1]
        k = pl.program_id(k_axis)

        def prod():
            return lax.dot_general(a_ref[...], b_ref[...], dims, preferred_element_type=F32)

        if nk == 1:
            o_ref[...] = prod().astype(o_ref.dtype)
            return

        @pl.when(k == 0)
        def _():
            acc_ref[...] = prod()

        if nk > 2:
            @pl.when((k > 0) & (k < nk - 1))
            def _():
                acc_ref[...] += prod()

        @pl.when(k == nk - 1)
        def _():
            o_ref[...] = (acc_ref[...] + prod()).astype(o_ref.dtype)

    blocks = [(a_spec.block_shape, a.dtype), (b_spec.block_shape, b.dtype), (o_spec.block_shape, out_shape.dtype)]
    return pl.pallas_call(
        body, name=name, grid=grid, in_specs=[a_spec, b_spec] + [ANY] * len(extra), out_specs=o_spec,
        out_shape=out_shape, scratch_shapes=[pltpu.VMEM(acc_shape, F32)], input_output_aliases=aliases,
        compiler_params=_params(blocks, [(acc_shape, F32)], ("parallel",) * k_axis + ("arbitrary",)),
    )(a, b, *extra)


def _mm_plain(a, b, dims, out_dtype, name):
    if dims == TN_DIMS:
        (K, M), N = a.shape, b.shape[1]
    elif dims == NT:
        (M, K), N = a.shape, b.shape[0]
    else:
        (M, K), N = a.shape, b.shape[1]
    tm, tn = _tile(M, TM, LANE), _tile(N, TN, LANE)
    tk = _tile(K, TK, LANE if dims != TN_DIMS else SUB16)
    if dims == TN_DIMS:
        a_spec = pl.BlockSpec((tk, tm), lambda i, j, k: (k, i))
    else:
        a_spec = pl.BlockSpec((tm, tk), lambda i, j, k: (i, k))
    if dims == NT:
        b_spec = pl.BlockSpec((tn, tk), lambda i, j, k: (j, k))
    else:
        b_spec = pl.BlockSpec((tk, tn), lambda i, j, k: (k, j))
    return _mm(a, b, grid=(M // tm, N // tn, K // tk), a_spec=a_spec, b_spec=b_spec,
               o_spec=pl.BlockSpec((tm, tn), lambda i, j, k: (i, j)),
               out_shape=_sds((M, N), out_dtype), acc_shape=(tm, tn), dims=dims, name=name)


def _mm_proj(h_all, win_q, q, nparts, fill, name):
    n, D = h_all.shape
    nbp = win_q.shape[2]
    nb = nbp * nparts
    tm, tn, tk = _tile(n, TM, SUB16), _tile(nbp, TN, LANE), D
    nbn = nbp // tn
    return _mm(h_all, win_q, grid=(n // tm, NDEV * nbn, D // tk),
               a_spec=pl.BlockSpec((tm, tk), lambda i, j, k: (i, k)),
               b_spec=pl.BlockSpec((None, tk, tn), lambda i, j, k: (j // nbn, k, j % nbn)),
               o_spec=pl.BlockSpec((tm, tn), lambda i, j, k: (i, (j // nbn) * (nb // tn) + q * nbn + j % nbn)),
               out_shape=_sds((n, NDEV * nb), F32), acc_shape=(tm, tn), dims=NN, name=name, fill=fill)


def _mm_dh(dproj, wins, name, dep):
    nparts = len(wins)
    n = dproj.shape[0]
    _, D, nbp = wins[0].shape
    nb = nbp * nparts
    tm, tn = _tile(n, TM, SUB16), _tile(D, TN, LANE)

    def body(a_ref, *rest):
        b_refs, o_ref, acc_ref = rest[:nparts], rest[-2], rest[-1]
        k = pl.program_id(2)

        def prod():
            out = None
            for q in range(nparts):
                d = lax.dot_general(a_ref[:, q * nbp:(q + 1) * nbp], b_refs[q][...], NT, preferred_element_type=F32)
                out = d if out is None else out + d
            return out

        @pl.when(k == 0)
        def _():
            acc_ref[...] = prod()

        @pl.when((k > 0) & (k < NDEV - 1))
        def _():
            acc_ref[...] += prod()

        @pl.when(k == NDEV - 1)
        def _():
            o_ref[...] = acc_ref[...] + prod()

    blocks = [((tm, nb), dproj.dtype)] + [((tn, nbp), wins[0].dtype)] * nparts + [((tm, tn), F32)]
    return pl.pallas_call(
        body, name=name, grid=(n // tm, D // tn, NDEV),
        in_specs=[pl.BlockSpec((tm, nb), lambda i, j, k: (i, k))]
        + [pl.BlockSpec((None, tn, nbp), lambda i, j, k: (k, j, 0))] * nparts + [ANY],
        out_specs=pl.BlockSpec((tm, tn), lambda i, j, k: (i, j)), out_shape=_sds((n, D), F32),
        scratch_shapes=[pltpu.VMEM((tm, tn), F32)],
        compiler_params=_params(blocks, [((tm, tn), F32)], ("parallel", "parallel", "arbitrary")),
    )(dproj, *wins, dep)


def _transpose(x, name):
    R, C = x.shape
    tr, tc = _tile(R, TL, LANE), _tile(C, 2 * TL, LANE)

    def body(x_ref, o_ref):
        o_ref[...] = x_ref[...].T

    return pl.pallas_call(
        body, name=name, grid=(R // tr, C // tc),
        in_specs=[pl.BlockSpec((tr, tc), lambda i, j: (i, j))],
        out_specs=pl.BlockSpec((tc, tr), lambda i, j: (j, i)),
        out_shape=_sds((C, R), x.dtype),
        compiler_params=_params([((tr, tc), x.dtype)] * 2, dims=("parallel", "parallel")),
    )(x)


def _mm_gwin(h_t, dproj, nb, part, nparts, name, dep=None):
    D, n = h_t.shape
    nbp = nb // nparts
    tm, tn, tk = _tile(D, TM, LANE), _tile(nbp, TN, LANE), n
    nbn = nbp // tn
    return _mm(h_t, dproj, grid=(D // tm, NDEV * nbn, n // tk),
               a_spec=pl.BlockSpec((tm, tk), lambda i, j, k: (i, k)),
               b_spec=pl.BlockSpec((tk, tn), lambda i, j, k: (k, (j // nbn) * (nb // tn) + part * nbn + j % nbn)),
               o_spec=pl.BlockSpec((None, tm, tn), lambda i, j, k: (j // nbn, i, j % nbn)),
               out_shape=_sds((NDEV, D, nbp), MXU), acc_shape=(tm, tn), dims=NN, name=name, dep=dep)


def _mm_group(a, b, mode, out_dtype, name):
    if mode == "wgrad":
        L, W = a.shape
        G = len(POOL_WINDOWS)
        pd = W // G
        tm, tn, tk = _tile(pd, TM, LANE), _tile(pd, TN, LANE), _tile(L, TK, SUB16)
        nm, nn = pd // tm, pd // tn
        return _mm(a, b, grid=(G, nm, nn, L // tk),
                   a_spec=pl.BlockSpec((tk, tm), lambda g, i, j, k: (k, g * nm + i)),
                   b_spec=pl.BlockSpec((tk, tn), lambda g, i, j, k: (k, g * nn + j)),
                   o_spec=pl.BlockSpec((None, tm, tn), lambda g, i, j, k: (g, i, j)),
                   out_shape=_sds((G, pd, pd), out_dtype), acc_shape=(tm, tn), dims=TN_DIMS, name=name)
    L, W = a.shape
    G, pd, _ = b.shape
    tm, tn, tk = _tile(L, TM, SUB16), _tile(pd, TN, LANE), _tile(pd, TK, LANE)
    nn, nk = pd // tn, pd // tk
    if mode == "fwd":
        b_spec = pl.BlockSpec((None, tk, tn), lambda g, i, j, k: (g, k, j))
        dims = NN
    else:
        b_spec = pl.BlockSpec((None, tn, tk), lambda g, i, j, k: (g, j, k))
        dims = NT
    return _mm(a, b, grid=(G, L // tm, nn, nk),
               a_spec=pl.BlockSpec((tm, tk), lambda g, i, j, k: (i, g * nk + k)),
               b_spec=b_spec,
               o_spec=pl.BlockSpec((tm, tn), lambda g, i, j, k: (i, g * nn + j)),
               out_shape=_sds((L, W), out_dtype), acc_shape=(tm, tn), dims=dims, name=name)


def _ada_fwd(cc, w_loc, b_loc, name):
    R, D = cc.shape
    na = w_loc.shape[1]
    tk = _tile(D, 512, LANE)

    def body(c_ref, w_ref, b_ref, mod_ref, s_ref):
        k = pl.program_id(0)
        cv = c_ref[...]
        s = cv * _sigmoid_small(cv)
        s_ref[...] = s

        @pl.when(k == 0)
        def _():
            mod_ref[...] = jnp.broadcast_to(b_ref[...], mod_ref.shape)

        mod_ref[...] += lax.dot_general(s.astype(MXU), w_ref[...].astype(MXU), NN, preferred_element_type=F32)

    blocks = [((R, tk), F32), ((tk, na), F32), ((1, na), F32), ((R, na), F32), ((R, tk), F32)]
    return pl.pallas_call(
        body, name=name, grid=(D // tk,),
        in_specs=[pl.BlockSpec((R, tk), lambda k: (0, k)), pl.BlockSpec((tk, na), lambda k: (k, 0)),
                  pl.BlockSpec((1, na), lambda k: (0, 0))],
        out_specs=[pl.BlockSpec((R, na), lambda k: (0, 0)), pl.BlockSpec((R, tk), lambda k: (0, k))],
        out_shape=[_sds((R, na), F32), _sds((R, D), F32)],
        compiler_params=_params(blocks, dims=("arbitrary",)),
    )(cc, w_loc, b_loc)


def _adam(w, g, m, v):
    m = ADAM_B1 * m + (1.0 - ADAM_B1) * g
    v = ADAM_B2 * v + (1.0 - ADAM_B2) * (g * g)
    m_hat = m / (1.0 - ADAM_B1 ** ADAM_STEP)
    v_hat = v / (1.0 - ADAM_B2 ** ADAM_STEP)
    delta = -ADAM_LR * (m_hat / (jnp.sqrt(v_hat) + ADAM_EPS) + ADAM_WD * w)
    return delta, m, v


def _ada_bwd(s_all, ga, gc, w_loc, m_loc, v_loc, name):
    D, na = w_loc.shape
    tr = _tile(D, 256, LANE)

    def body(s_ref, ga_ref, gc_ref, w_ref, m_ref, v_ref, g_ref, d_ref, nm_ref, nv_ref, pc_ref):
        dmc = gc_ref[0:1, :]
        for p in range(1, NDEV):
            dmc = dmc + gc_ref[p:p + 1, :]
        rows = lax.broadcasted_iota(jnp.int32, (NDEV, na), 0)
        dmc8 = jnp.where(rows == 0, jnp.broadcast_to(dmc, (NDEV, na)), 0.0)
        dm = jnp.concatenate([ga_ref[...], dmc8], axis=0).astype(MXU)
        dmc16 = jnp.concatenate([dmc8, jnp.zeros_like(dmc8)], axis=0).astype(MXU)
        w = w_ref[...]
        g = lax.dot_general(s_ref[...].astype(MXU), dm, TN_DIMS, preferred_element_type=F32)
        pc_ref[...] = lax.dot_general(dmc16, w.astype(MXU), NT, preferred_element_type=F32)
        delta, nm, nv = _adam(w, g, m_ref[...], v_ref[...])
        g_ref[...] = g
        d_ref[...] = delta
        nm_ref[...] = nm
        nv_ref[...] = nv

    big = pl.BlockSpec((tr, na), lambda i: (i, 0))
    full = pl.BlockSpec((NDEV, na), lambda i: (0, 0))
    srow = pl.BlockSpec((2 * NDEV, tr), lambda i: (0, i))
    blocks = [((2 * NDEV, tr), F32)] * 2 + [((NDEV, na), F32)] * 2 + [((tr, na), F32)] * 7
    return pl.pallas_call(
        body, name=name, grid=(D // tr,),
        in_specs=[srow, full, full, big, big, big],
        out_specs=[big, big, big, big, srow],
        out_shape=[_sds((D, na), F32)] * 4 + [_sds((2 * NDEV, D), F32)],
        compiler_params=_params(blocks, dims=("parallel",)),
    )(s_all, ga, gc, w_loc, m_loc, v_loc)


def _norm_mod(x2, g, shift, scale, n, row0, h_prev, name):
    R, D = x2.shape
    tl = _tile(R, TL, SUB16)
    assert row0 % tl == 0
    b0 = row0 // tl

    def body(x_ref, g_ref, sh_ref, sc_ref, *rest):
        o_ref = rest[-1]
        xv = x_ref[...]
        s = lax.rsqrt(jnp.mean(xv * xv, axis=-1, keepdims=True) + EPS)
        nrm = xv * s * g_ref[...]
        o_ref[...] = (nrm * (1.0 + sc_ref[...]) + sh_ref[...]).astype(o_ref.dtype)

    vec = pl.BlockSpec((1, D), lambda i: (0, 0))
    in_specs = [pl.BlockSpec((tl, D), lambda i: (i, 0)), vec, vec, vec]
    args = [x2, g, shift, scale]
    aliases = {}
    if h_prev is not None:
        in_specs.append(ANY)
        args.append(h_prev)
        aliases = {4: 0}
    blocks = [((tl, D), F32), ((tl, D), MXU)] + [((1, D), F32)] * 3
    return pl.pallas_call(
        body, name=name, grid=(R // tl,), in_specs=in_specs,
        out_specs=pl.BlockSpec((tl, D), lambda i: (i + b0, 0)),
        out_shape=_sds((n, D), MXU), input_output_aliases=aliases,
        compiler_params=_params(blocks, dims=("parallel",)),
    )(*args)


def _norm_bwd(x2, dh_all, row0, g, scale, dxn, ggn0, name):
    R, D = x2.shape
    tl = _tile(R, TL_FINAL, SUB)
    assert row0 % tl == 0
    b0 = row0 // tl
    with_x = dxn is not None

    def body(*refs):
        if with_x:
            x_ref, dh_ref, g_ref, sc_ref, gg0_ref, dxn_ref, gx_ref, dsh_ref, dsc_ref, gg_ref = refs
        else:
            x_ref, dh_ref, g_ref, sc_ref, gg0_ref, dsh_ref, dsc_ref, gg_ref = refs
        i = pl.program_id(0)

        @pl.when(i == 0)
        def _():
            dsh_ref[...] = jnp.zeros_like(dsh_ref)
            dsc_ref[...] = jnp.zeros_like(dsc_ref)
            gg_ref[...] = gg0_ref[...]

        xv = x_ref[...]
        dh = dh_ref[...]
        gv = g_ref[...]
        s = lax.rsqrt(jnp.mean(xv * xv, axis=-1, keepdims=True) + EPS)
        xh = xv * s
        dsh_ref[...] += jnp.sum(dh, axis=0, keepdims=True)
        dsc_ref[...] += jnp.sum(dh * (xh * gv), axis=0, keepdims=True)
        dn = dh * (1.0 + sc_ref[...])
        gg_ref[...] += jnp.sum(dn * xh, axis=0, keepdims=True)
        if with_x:
            dxh = dn * gv
            dx = s * (dxh - xh * jnp.mean(dxh * xh, axis=-1, keepdims=True))
            gx_ref[...] = dx + dxn_ref[...]

    vec = pl.BlockSpec((1, D), lambda i: (0, 0))
    row = pl.BlockSpec((tl, D), lambda i: (i, 0))
    in_specs = [row, pl.BlockSpec((tl, D), lambda i: (i + b0, 0)), vec, vec, vec]
    args = [x2, dh_all, g, scale, ggn0]
    out_specs = [vec, vec, vec]
    out_shape = [_sds((1, D), F32)] * 3
    if with_x:
        in_specs.append(row)
        args.append(dxn)
        out_specs = [row] + out_specs
        out_shape = [_sds((R, D), F32)] + out_shape
    blocks = [((tl, D), F32)] * (4 if with_x else 2) + [((1, D), F32)] * 6
    outs = pl.pallas_call(
        body, name=name, grid=(R // tl,), in_specs=in_specs, out_specs=out_specs, out_shape=out_shape,
        compiler_params=_params(blocks, dims=("arbitrary",)),
    )(*args)
    return tuple(outs) if with_x else (None,) + tuple(outs)


def _tap_valid(t, o, lc, n):
    tt = t + o
    in_ctx = t < lc
    return (tt >= jnp.where(in_ctx, 0, lc)) & (tt < jnp.where(in_ctx, lc, n))


def _conv_fwd(proj_all, cw, cb, lc, W, name):
    n = proj_all.shape[0]
    cbk = _tile(W, CB_SEQ, LANE)
    tr = _tile(n, TR_CONV, SUB16)
    ext = tr + 2 * SUB

    def body(x_ref, w_ref, b_ref, u_ref, xp_ref):
        xp_ref[0:SUB, :] = jnp.zeros((SUB, cbk), F32)
        xp_ref[n + SUB:n + 2 * SUB, :] = jnp.zeros((SUB, cbk), F32)
        xp_ref[SUB:n + SUB, :] = x_ref[...]
        w = w_ref[...]
        bias = b_ref[...]

        def chunk(ci, carry):
            r0 = pl.multiple_of(ci * tr, SUB16)
            xe = xp_ref[pl.ds(r0, ext), :]
            t = r0 + lax.broadcasted_iota(jnp.int32, (tr, cbk), 0)
            acc = jnp.broadcast_to(bias, (tr, cbk))
            for k in range(4):
                o = k - 1
                sh = xe if o == 0 else pltpu.roll(xe, (-o) % ext, 0)
                acc = acc + jnp.where(_tap_valid(t, o, lc, n), sh[SUB:tr + SUB], 0.0) * w[k:k + 1]
            u_ref[pl.ds(r0, tr), :] = acc
            return carry

        lax.fori_loop(0, n // tr, chunk, 0)

    blocks = [((n, cbk), F32)] * 2 + [((4, cbk), F32), ((1, cbk), F32)]
    scratch = [((n + 2 * SUB, cbk), F32)]
    return pl.pallas_call(
        body, name=name, grid=(W // cbk,),
        in_specs=[pl.BlockSpec((n, cbk), lambda j: (0, j)), pl.BlockSpec((4, cbk), lambda j: (0, j)),
                  pl.BlockSpec((1, cbk), lambda j: (0, j))],
        out_specs=pl.BlockSpec((n, cbk), lambda j: (0, j)),
        out_shape=_sds((n, W), F32),
        scratch_shapes=[pltpu.VMEM(s, d) for s, d in scratch],
        compiler_params=_params(blocks, scratch, ("parallel",)),
    )(proj_all, cw, cb)


def _conv_bwd(du_all, proj_all, cw, dproj, lc, W, name):
    n = du_all.shape[0]
    cbk = _tile(W, CB_SEQ, LANE)
    tr = _tile(n, TR_CONV, SUB16)
    ext = tr + 2 * SUB

    def body(du_ref, x_ref, w_ref, dp_in, dx_ref, gw_ref, gb_ref, dp_ref, xp_ref):
        del dp_in
        for ref, src in ((dp_ref, du_ref), (xp_ref, x_ref)):
            ref[0:SUB, :] = jnp.zeros((SUB, cbk), F32)
            ref[n + SUB:n + 2 * SUB, :] = jnp.zeros((SUB, cbk), F32)
            ref[SUB:n + SUB, :] = src[...]
        w = w_ref[...]

        def fold(v):
            return jnp.sum(v.reshape(tr // SUB, SUB, cbk), axis=0)

        def chunk(ci, carry):
            r0 = pl.multiple_of(ci * tr, SUB16)
            de = dp_ref[pl.ds(r0, ext), :]
            xe = xp_ref[pl.ds(r0, ext), :]
            t = r0 + lax.broadcasted_iota(jnp.int32, (tr, cbk), 0)
            d0 = de[SUB:tr + SUB]
            dx = jnp.zeros((tr, cbk), F32)
            new = []
            for k in range(4):
                o = k - 1
                dsh = de if o == 0 else pltpu.roll(de, o % ext, 0)
                dx = dx + jnp.where(_tap_valid(t, -o, lc, n), dsh[SUB:tr + SUB], 0.0) * w[k:k + 1]
                xsh = xe if o == 0 else pltpu.roll(xe, (-o) % ext, 0)
                new.append(carry[k] + fold(d0 * jnp.where(_tap_valid(t, o, lc, n), xsh[SUB:tr + SUB], 0.0)))
            new.append(carry[4] + fold(d0))
            dx_ref[pl.ds(r0, tr), :] = dx.astype(dx_ref.dtype)
            return tuple(new)

        zero = jnp.zeros((SUB, cbk), F32)
        acc = lax.fori_loop(0, n // tr, chunk, (zero,) * 5)
        for k in range(4):
            gw_ref[k:k + 1, :] = jnp.sum(acc[k], axis=0, keepdims=True)
        gb_ref[...] = jnp.sum(acc[4], axis=0, keepdims=True)

    col = pl.BlockSpec((n, cbk), lambda j: (0, j))
    blocks = [((n, cbk), F32)] * 2 + [((n, cbk), MXU), ((4, cbk), F32), ((4, cbk), F32), ((1, cbk), F32)]
    scratch = [((n + 2 * SUB, cbk), F32)] * 2
    return pl.pallas_call(
        body, name=name, grid=(W // cbk,),
        in_specs=[col, col, pl.BlockSpec((4, cbk), lambda j: (0, j)), ANY],
        out_specs=[col, pl.BlockSpec((4, cbk), lambda j: (0, j)), pl.BlockSpec((1, cbk), lambda j: (0, j))],
        out_shape=[_sds(dproj.shape, dproj.dtype), _sds((4, W), F32), _sds((1, W), F32)],
        input_output_aliases={3: 0},
        scratch_shapes=[pltpu.VMEM(s, d) for s, d in scratch],
        compiler_params=_params(blocks, scratch, ("parallel",)),
    )(du_all, proj_all, cw, dproj)


def _gate_coeffs(ub, u, d, wr_ref, wi_ref, br_ref, bi_ref, lam_ref):
    c = -LRU_C * _softplus(-lam_ref[d:d + 1, :])
    r = _sigmoid(lax.dot_general(ub, wr_ref[d], NN, preferred_element_type=F32) + br_ref[d:d + 1, :])
    ig = _sigmoid(lax.dot_general(ub, wi_ref[d], NN, preferred_element_type=F32) + bi_ref[d:d + 1, :])
    la = c * r
    a = jnp.exp(la)
    sq = jnp.sqrt(-jnp.tanh(la) * (1.0 + a * a))
    return c, r, ig, a, sq


def _gate_specs(tl, hd):
    w_spec = pl.BlockSpec((2, None, hd, hd), lambda h, i: (0, h, 0, 0))
    v_spec = pl.BlockSpec((2, hd), lambda h, i: (0, h))
    return w_spec, v_spec


def _gates_fwd(u_all, wr, wi, br, bi, lam, name):
    n, W = u_all.shape
    heads, hd = wr.shape[1], wr.shape[2]
    tl = _tile(n, TL_GATES, SUB16)

    def body(u_ref, wr_ref, wi_ref, br_ref, bi_ref, lam_ref, a_ref, b_ref):
        u = u_ref[...]
        ub = u.astype(MXU)
        for d in range(2):
            _, _, ig, a, sq = _gate_coeffs(ub, u, d, wr_ref, wi_ref, br_ref, bi_ref, lam_ref)
            a_ref[d] = a
            b_ref[d] = sq * (ig * u)

    w_spec, v_spec = _gate_specs(tl, hd)
    o_spec = pl.BlockSpec((2, tl, hd), lambda h, i: (0, i, h))
    blocks = [((tl, hd), F32), ((2, hd, hd), MXU), ((2, hd, hd), MXU)] + [((2, hd), F32)] * 3 + [((2, tl, hd), F32)] * 2
    return pl.pallas_call(
        body, name=name, grid=(heads, n // tl),
        in_specs=[pl.BlockSpec((tl, hd), lambda h, i: (i, h)), w_spec, w_spec, v_spec, v_spec, v_spec],
        out_specs=[o_spec, o_spec], out_shape=[_sds((2, n, W), F32)] * 2,
        compiler_params=_params(blocks, dims=("parallel", "parallel")),
    )(u_all, wr, wi, br, bi, lam)


def _gates_bwd(u_all, da, db, wr, wi, br, bi, lam, name):
    n, W = u_all.shape
    heads, hd = wr.shape[1], wr.shape[2]
    tl = _tile(n, TL_GATES, SUB16)
    ni = n // tl

    def body(u_ref, da_ref, db_ref, wr_ref, wi_ref, br_ref, bi_ref, lam_ref,
             du_ref, gwr_ref, gwi_ref, gbr_ref, gbi_ref, gc_ref, accr_ref, acci_ref):
        i = pl.program_id(1)

        @pl.when(i == 0)
        def _():
            accr_ref[...] = jnp.zeros_like(accr_ref)
            acci_ref[...] = jnp.zeros_like(acci_ref)
            gbr_ref[...] = jnp.zeros_like(gbr_ref)
            gbi_ref[...] = jnp.zeros_like(gbi_ref)
            gc_ref[...] = jnp.zeros_like(gc_ref)

        u = u_ref[...]
        ub = u.astype(MXU)
        du = jnp.zeros_like(u)
        for d in range(2):
            c, r, ig, a, sq = _gate_coeffs(ub, u, d, wr_ref, wi_ref, br_ref, bi_ref, lam_ref)
            dbv = db_ref[d]
            t = dbv * sq
            du = du + t * ig
            d_la = da_ref[d] * a - (dbv * ig * u) * (a * a) / sq
            gc_ref[d:d + 1, :] += jnp.sum(d_la * r, axis=0, keepdims=True)
            d_pr = (d_la * c) * (r * (1.0 - r))
            d_pi = (t * u) * (ig * (1.0 - ig))
            gbr_ref[d:d + 1, :] += jnp.sum(d_pr, axis=0, keepdims=True)
            gbi_ref[d:d + 1, :] += jnp.sum(d_pi, axis=0, keepdims=True)
            pb = d_pr.astype(MXU)
            qb = d_pi.astype(MXU)
            du = du + lax.dot_general(pb, wr_ref[d], NT, preferred_element_type=F32)
            du = du + lax.dot_general(qb, wi_ref[d], NT, preferred_element_type=F32)
            accr_ref[d] += lax.dot_general(ub, pb, TN_DIMS, preferred_element_type=F32)
            acci_ref[d] += lax.dot_general(ub, qb, TN_DIMS, preferred_element_type=F32)
        du_ref[...] = du

        @pl.when(i == ni - 1)
        def _():
            gwr_ref[...] = accr_ref[...].astype(gwr_ref.dtype)
            gwi_ref[...] = acci_ref[...].astype(gwi_ref.dtype)

    w_spec, v_spec = _gate_specs(tl, hd)
    u_spec = pl.BlockSpec((tl, hd), lambda h, i: (i, h))
    ab_spec = pl.BlockSpec((2, tl, hd), lambda h, i: (0, i, h))
    blocks = ([((tl, hd), F32)] * 2 + [((2, tl, hd), F32)] * 2 + [((2, hd, hd), MXU)] * 4 + [((2, hd), F32)] * 6)
    scratch = [((2, hd, hd), F32)] * 2
    return pl.pallas_call(
        body, name=name, grid=(heads, ni),
        in_specs=[u_spec, ab_spec, ab_spec, w_spec, w_spec, v_spec, v_spec, v_spec],
        out_specs=[u_spec, w_spec, w_spec, v_spec, v_spec, v_spec],
        out_shape=[_sds((n, W), F32), _sds(wr.shape, MXU), _sds(wi.shape, MXU)] + [_sds((2, W), F32)] * 3,
        scratch_shapes=[pltpu.VMEM(s, d) for s, d in scratch],
        compiler_params=_params(blocks, scratch, ("parallel", "arbitrary")),
    )(u_all, da, db, wr, wi, br, bi, lam)


def _tile_scan(A, B, rows, reverse):
    for s in (1, 2, 4):
        if reverse:
            As, Bs, m = pltpu.roll(A, SUB - s, 0), pltpu.roll(B, SUB - s, 0), rows < SUB - s
        else:
            As, Bs, m = pltpu.roll(A, s, 0), pltpu.roll(B, s, 0), rows >= s
        B = jnp.where(m, A * Bs + B, B)
        A = jnp.where(m, A * As, A)
    return A, B


def _scan_chunks(n, lc):
    tc = _tile(lc, TL, SUB)
    assert n % tc == 0 and lc % tc == 0
    return tc, n // tc, lc // tc


def _scan_fwd(a_all, b_all, lc, name):
    _, n, W = a_all.shape
    cb = _tile(W, CB_SCAN, LANE)
    tc, nch, ncc = _scan_chunks(n, lc)
    ntile = tc // SUB

    def chunk(d, t):
        return jnp.where(d == 0, t, jnp.where(t < ncc, ncc - 1 - t, nch - 1 - (t - ncc)))

    def body(a_ref, b_ref, h_ref, carry_ref):
        rows = lax.broadcasted_iota(jnp.int32, (SUB, cb), 0)

        @pl.when(pl.program_id(2) == 0)
        def _():
            carry_ref[...] = jnp.zeros_like(carry_ref)

        def run(reverse):
            def step(i, h):
                r = pl.multiple_of(((ntile - 1 - i) if reverse else i) * SUB, SUB)
                A, B = _tile_scan(a_ref[pl.ds(r, SUB), :], b_ref[pl.ds(r, SUB), :], rows, reverse)
                H = A * h + B
                h_ref[pl.ds(r, SUB), :] = H
                return H[0:1, :] if reverse else H[SUB - 1:SUB, :]

            carry_ref[...] = lax.fori_loop(0, ntile, step, carry_ref[...], unroll=2)

        @pl.when(pl.program_id(1) == 0)
        def _():
            run(False)

        @pl.when(pl.program_id(1) == 1)
        def _():
            run(True)

    spec = pl.BlockSpec((None, tc, cb), lambda j, d, t: (d, chunk(d, t), j))
    return pl.pallas_call(
        body, name=name, grid=(W // cb, 2, nch), in_specs=[spec, spec], out_specs=spec,
        out_shape=_sds((2, n, W), F32), scratch_shapes=[pltpu.VMEM((1, cb), F32)],
        compiler_params=_params([((tc, cb), F32)] * 3, [((1, cb), F32)], ("parallel", "arbitrary", "arbitrary")),
    )(a_all, b_all)


def _scan_bwd(a_all, h_all, dya, lc, name):
    _, n, W = a_all.shape
    cb = _tile(W, CB_SCAN, LANE)
    tc, nch, ncc = _scan_chunks(n, lc)
    ntile = tc // SUB
    nl = nch - ncc

    def chunk(d, t):
        return jnp.where(d == 0, nch - 1 - t, jnp.where(t < nl, ncc + t, t - nl))

    def neighbour(d, t):
        c = chunk(d, t)
        below = jnp.maximum(c * ntile - 1, 0)
        above = jnp.where(c == nch - 1, 0, jnp.minimum((c + 1) * ntile, nch * ntile - 1))
        return jnp.where(d == 0, below, above)

    def body(a_ref, h_ref, hn_ref, g_ref, da_ref, db_ref, mu_ref):
        rows = lax.broadcasted_iota(jnp.int32, (SUB, cb), 0)
        d, t = pl.program_id(1), pl.program_id(2)
        c = chunk(d, t)
        has_g = c >= ncc

        @pl.when(t == 0)
        def _():
            mu_ref[...] = jnp.zeros_like(mu_ref)

        def tile(ref, j):
            return ref[pl.ds(pl.multiple_of(j * SUB, SUB), SUB), :]

        def run(up):
            if up:
                edge = jnp.where(c == ncc - 1, 0.0, hn_ref[0:1, :])
            else:
                edge = jnp.where(c > 0, hn_ref[SUB - 1:SUB, :], 0.0)

            def step(i, mu):
                j = i if up else ntile - 1 - i
                a_t = tile(a_ref, j)
                g_t = jnp.where(has_g, tile(g_ref, j), 0.0)
                if up:
                    ap = jnp.where(rows >= 1, pltpu.roll(a_t, 1, 0), 1.0)
                    nb_row = jnp.where(j < ntile - 1, tile(h_ref, jnp.minimum(j + 1, ntile - 1))[0:1, :], edge)
                    hprev = jnp.where(rows < SUB - 1, pltpu.roll(tile(h_ref, j), SUB - 1, 0), nb_row)
                else:
                    ap = jnp.where(rows < SUB - 1, pltpu.roll(a_t, SUB - 1, 0), 1.0)
                    nb_row = jnp.where(j > 0, tile(h_ref, jnp.maximum(j - 1, 0))[SUB - 1:SUB, :], edge)
                    hprev = jnp.where(rows >= 1, pltpu.roll(tile(h_ref, j), 1, 0), nb_row)
                A, B = _tile_scan(ap, g_t, rows, not up)
                lam = A * mu + B
                r = pl.multiple_of(j * SUB, SUB)
                da_ref[pl.ds(r, SUB), :] = lam * hprev
                db_ref[pl.ds(r, SUB), :] = lam
                return a_t[SUB - 1:SUB, :] * lam[SUB - 1:SUB, :] if up else a_t[0:1, :] * lam[0:1, :]

            mu_ref[...] = lax.fori_loop(0, ntile, step, mu_ref[...], unroll=2)

        @pl.when(d == 0)
        def _():
            run(False)

        @pl.when(d == 1)
        def _():
            run(True)

    spec = pl.BlockSpec((None, tc, cb), lambda j, d, t: (d, chunk(d, t), j))
    n_spec = pl.BlockSpec((None, SUB, cb), lambda j, d, t: (d, neighbour(d, t), j))
    g_spec = pl.BlockSpec((tc, cb), lambda j, d, t: (jnp.maximum(chunk(d, t) - ncc, 0), j))
    blocks = [((tc, cb), F32)] * 5 + [((SUB, cb), F32)]
    return pl.pallas_call(
        body, name=name, grid=(W // cb, 2, nch), in_specs=[spec, spec, n_spec, g_spec], out_specs=[spec, spec],
        out_shape=[_sds((2, n, W), F32)] * 2, scratch_shapes=[pltpu.VMEM((1, cb), F32)],
        compiler_params=_params(blocks, [((1, cb), F32)], ("parallel", "arbitrary", "arbitrary")),
    )(a_all, h_all, h_all, dya)


def _pool_window(v, w, tl, cb, transpose):
    half = w // 2
    pos = lax.broadcasted_iota(jnp.int32, (tl, cb), 0) % GRID_W
    cnt = (jnp.minimum(pos + half - 1, GRID_W - 1) - jnp.maximum(pos - half, 0) + 1).astype(F32)
    src = v / cnt if transpose else v

    def run_sum(s, step):
        span = 1
        while span < half:
            ok = (pos + span < GRID_W) if step > 0 else (pos - span >= 0)
            s = s + jnp.where(ok, pltpu.roll(s, (-step * span) % tl, 0), 0.0)
            span *= 2
        return s

    ahead, behind = run_sum(src, 1), run_sum(src, -1)
    if transpose:
        return behind + jnp.where(pos + 1 < GRID_W, pltpu.roll(ahead, tl - 1, 0), 0.0) - v
    return (ahead + jnp.where(pos >= 1, pltpu.roll(behind, 1, 0), 0.0)) / cnt - v


def _pool_z(src, row0, col0, L, W, transpose, dproj, name):
    G = len(POOL_WINDOWS)
    pd = W // G
    tl = _tile(L, TL, GRID_W)
    cb = _tile(pd, CB_POOL, LANE)
    assert row0 % tl == 0 and col0 % cb == 0
    rb, cbk = row0 // tl, col0 // cb
    nj = pd // cb

    def body(x_ref, *rest):
        o_ref = rest[-1]
        for gi, w in enumerate(POOL_WINDOWS):
            @pl.when(pl.program_id(0) == gi)
            def _(w=w):
                o_ref[...] = _pool_window(x_ref[...], w, tl, cb, transpose).astype(o_ref.dtype)

    plain = pl.BlockSpec((tl, cb), lambda g, i, j: (i, g * nj + j))
    window = pl.BlockSpec((tl, cb), lambda g, i, j: (i + rb, cbk + g * nj + j))
    blocks = [((tl, cb), F32), ((tl, cb), MXU)]
    if transpose:
        return pl.pallas_call(
            body, name=name, grid=(G, L // tl, nj), in_specs=[plain, ANY], out_specs=window,
            out_shape=_sds(dproj.shape, dproj.dtype), input_output_aliases={1: 0},
            compiler_params=_params(blocks, dims=("parallel",) * 3),
        )(src, dproj)
    return pl.pallas_call(
        body, name=name, grid=(G, L // tl, nj), in_specs=[window], out_specs=plain,
        out_shape=_sds((L, W), MXU),
        compiler_params=_params(blocks, dims=("parallel",) * 3),
    )(src)


def _mix_fwd(hs, proj_all, ypre, b_pool, pool_scale, lc, name):
    L, W = ypre.shape
    tl = _tile(L, TL, SUB16)
    cb = _tile(W, CB_MIX, LANE)
    nj = W // cb
    assert lc % tl == 0
    rb = lc // tl

    def body(hs_ref, ga_ref, yp_ref, gb_ref, bp_ref, ps_ref, o_ref):
        p = pl.program_id(2)

        @pl.when(p == 0)
        def _():
            g = ga_ref[...]
            o_ref[...] = ((hs_ref[0] + hs_ref[1]) * (g * _sigmoid(g))).astype(o_ref.dtype)

        @pl.when(p == 1)
        def _():
            g = gb_ref[...]
            yb = (yp_ref[...] + bp_ref[...]) * ps_ref[...]
            o_ref[...] = (yb * (g * _sigmoid(g))).astype(o_ref.dtype)

    vec = pl.BlockSpec((1, cb), lambda i, j, p: (0, j))
    blocks = [((2, tl, cb), F32)] + [((tl, cb), F32)] * 3 + [((tl, cb), MXU)]
    return pl.pallas_call(
        body, name=name, grid=(L // tl, nj, 2),
        in_specs=[pl.BlockSpec((2, tl, cb), lambda i, j, p: (0, i + rb, j)),
                  pl.BlockSpec((tl, cb), lambda i, j, p: (i + rb, 2 * nj + j)),
                  pl.BlockSpec((tl, cb), lambda i, j, p: (i, j)),
                  pl.BlockSpec((tl, cb), lambda i, j, p: (i + rb, 3 * nj + j)), vec, vec],
        out_specs=pl.BlockSpec((tl, cb), lambda i, j, p: (i, p * nj + j)),
        out_shape=_sds((L, 2 * W), MXU),
        compiler_params=_params(blocks, dims=("parallel", "parallel", "arbitrary")),
    )(hs, proj_all, ypre, proj_all, b_pool, pool_scale)


def _dsilu(g, sg):
    return sg * (1.0 + g * (1.0 - sg))


def _mixa_bwd(dmixed, hs, proj_all, dproj, lc, W, name):
    L = dmixed.shape[0]
    tl = _tile(L, TL, SUB16)
    cb = _tile(W, CB_MIX, LANE)
    nj = W // cb
    rb = lc // tl

    def body(dm_ref, hs_ref, ga_ref, dp_in, dya_ref, dga_ref):
        del dp_in
        g = ga_ref[...]
        sg = _sigmoid(g)
        dm = dm_ref[...]
        dya_ref[...] = dm * (g * sg)
        dga_ref[...] = (dm * (hs_ref[0] + hs_ref[1]) * _dsilu(g, sg)).astype(dga_ref.dtype)

    blocks = [((tl, cb), F32)] * 3 + [((2, tl, cb), F32), ((tl, cb), MXU)]
    return pl.pallas_call(
        body, name=name, grid=(L // tl, nj),
        in_specs=[pl.BlockSpec((tl, cb), lambda i, j: (i, j)),
                  pl.BlockSpec((2, tl, cb), lambda i, j: (0, i + rb, j)),
                  pl.BlockSpec((tl, cb), lambda i, j: (i + rb, 2 * nj + j)), ANY],
        out_specs=[pl.BlockSpec((tl, cb), lambda i, j: (i, j)),
                   pl.BlockSpec((tl, cb), lambda i, j: (i + rb, 2 * nj + j))],
        out_shape=[_sds((L, W), F32), _sds(dproj.shape, dproj.dtype)],
        input_output_aliases={3: 1},
        compiler_params=_params(blocks, dims=("parallel", "parallel")),
    )(dmixed, hs, proj_all, dproj)


def _mixb_bwd(dmixed, ypre, proj_all, b_pool, pool_scale, dproj, lc, W, name):
    L = dmixed.shape[0]
    tl = _tile(L, TL, SUB16)
    cb = _tile(W, CB_MIX, LANE)
    nj = W // cb
    rb = lc // tl

    def body(dm_ref, yp_ref, gb_ref, bp_ref, ps_ref, dp_in, dyp_ref, dgb_ref, gbp_ref, gps_ref):
        del dp_in
        i = pl.program_id(1)

        @pl.when(i == 0)
        def _():
            gbp_ref[...] = jnp.zeros_like(gbp_ref)
            gps_ref[...] = jnp.zeros_like(gps_ref)

        g = gb_ref[...]
        sg = _sigmoid(g)
        dm = dm_ref[...]
        yp = yp_ref[...] + bp_ref[...]
        ps = ps_ref[...]
        dyb = dm * (g * sg)
        dyp = dyb * ps
        dgb_ref[...] = (dm * (yp * ps) * _dsilu(g, sg)).astype(dgb_ref.dtype)
        dyp_ref[...] = dyp.astype(dyp_ref.dtype)
        gbp_ref[...] += jnp.sum(dyp, axis=0, keepdims=True)
        gps_ref[...] += jnp.sum(dyb * yp, axis=0, keepdims=True)

    vec = pl.BlockSpec((1, cb), lambda j, i: (0, j))
    blocks = [((tl, cb), F32)] * 3 + [((tl, cb), MXU)] * 2 + [((1, cb), F32)] * 4
    return pl.pallas_call(
        body, name=name, grid=(nj, L // tl),
        in_specs=[pl.BlockSpec((tl, cb), lambda j, i: (i, nj + j)),
                  pl.BlockSpec((tl, cb), lambda j, i: (i, j)),
                  pl.BlockSpec((tl, cb), lambda j, i: (i + rb, 3 * nj + j)), vec, vec, ANY],
        out_specs=[pl.BlockSpec((tl, cb), lambda j, i: (i, j)),
                   pl.BlockSpec((tl, cb), lambda j, i: (i + rb, 3 * nj + j)), vec, vec],
        out_shape=[_sds((L, W), MXU), _sds(dproj.shape, dproj.dtype), _sds((1, W), F32), _sds((1, W), F32)],
        input_output_aliases={5: 1},
        compiler_params=_params(blocks, dims=("parallel", "arbitrary")),
    )(dmixed, ypre, proj_all, b_pool, pool_scale, dproj)


def _dproj_init(n, lc, W, name):
    cb = _tile(W, CB_MIX, LANE)
    nj = W // cb

    def body(o_ref):
        o_ref[...] = jnp.zeros_like(o_ref)

    return pl.pallas_call(
        body, name=name, grid=(3 * nj,), in_specs=[],
        out_specs=pl.BlockSpec((lc, cb), lambda j: (0, nj + j)),
        out_shape=_sds((n, 4 * W), MXU),
        compiler_params=_params([((lc, cb), MXU)], dims=("parallel",)),
    )()


def _final(x2, out, tgt, gate, gfin, name):
    L, D = x2.shape
    tl = _tile(L, TL_FINAL, SUB16)

    def body(x_ref, o_ref, t_ref, gate_ref, g_ref, dout_ref, dxn_ref, loss_ref, ggf_ref, dgate_ref):
        i = pl.program_id(0)

        @pl.when(i == 0)
        def _():
            loss_ref[...] = jnp.zeros_like(loss_ref)
            ggf_ref[...] = jnp.zeros_like(ggf_ref)
            dgate_ref[...] = jnp.zeros_like(dgate_ref)

        o = o_ref[...]
        gate_v = gate_ref[...]
        gv = g_ref[...]
        xn = x_ref[...] + gate_v * o
        s = lax.rsqrt(jnp.mean(xn * xn, axis=-1, keepdims=True) + EPS)
        xh = xn * s
        err = xh * gv - t_ref[...]
        tok = jnp.mean(err * err, axis=-1, keepdims=True)
        loss_ref[...] += 0.5 * jnp.sum(tok, axis=0, keepdims=True)
        dy = err / D
        ggf_ref[...] += jnp.sum(dy * xh, axis=0, keepdims=True)
        dxh = dy * gv
        dxn = s * (dxh - xh * jnp.mean(dxh * xh, axis=-1, keepdims=True))
        dgate_ref[...] += jnp.sum(dxn * o, axis=0, keepdims=True)
        dout_ref[...] = (gate_v * dxn).astype(dout_ref.dtype)
        dxn_ref[...] = dxn

    row = pl.BlockSpec((tl, D), lambda i: (i, 0))
    vec = pl.BlockSpec((1, D), lambda i: (0, 0))
    blocks = [((tl, D), F32)] * 4 + [((tl, D), MXU)] + [((1, D), F32)] * 4
    return pl.pallas_call(
        body, name=name, grid=(L // tl,), in_specs=[row, row, row, vec, vec],
        out_specs=[row, row, pl.BlockSpec((1, 1), lambda i: (0, 0)), vec, vec],
        out_shape=[_sds((L, D), MXU), _sds((L, D), F32), _sds((1, 1), F32), _sds((1, D), F32), _sds((1, D), F32)],
        compiler_params=_params(blocks, dims=("arbitrary",)),
    )(x2, out, tgt, gate, gfin)


def _adamw_parts(w2, parts, m2, v2, name):
    R, C = w2.shape
    nh = len(parts)
    ch = C // nh
    tr = _tile(R, max(SUB16, (512 * 1024) // (ch * (nh + 1))), SUB16)

    def body(w_ref, *rest):
        p_refs = rest[:nh]
        m_ref, v_ref, g_ref, d_ref, nm_ref, nv_ref = rest[nh:]
        for q in range(nh):
            @pl.when(pl.program_id(1) == q)
            def _(p_ref=p_refs[q]):
                g = p_ref[0].astype(F32)
                for p in range(1, NDEV):
                    g = g + p_ref[p].astype(F32)
                delta, nm, nv = _adam(w_ref[...], g, m_ref[...], v_ref[...])
                g_ref[...] = g
                d_ref[...] = delta
                nm_ref[...] = nm
                nv_ref[...] = nv

    blk = pl.BlockSpec((tr, ch), lambda i, h: (i, h))
    p_spec = pl.BlockSpec((NDEV, tr, ch), lambda i, h: (0, i, 0))
    blocks = [((tr, ch), F32)] * 7 + [((NDEV, tr, ch), parts[0].dtype)] * nh
    return pl.pallas_call(
        body, name=name, grid=(R // tr, nh),
        in_specs=[blk] + [p_spec] * nh + [blk, blk],
        out_specs=[blk] * 4, out_shape=[_sds((R, C), F32)] * 4,
        compiler_params=_params(blocks, dims=("parallel", "arbitrary")),
    )(w2, *parts, m2, v2)


def _small_sum(vs, ga, gc, name):
    ns, nm = vs.shape[1], ga.shape[1]

    def body(v_ref, ga_ref, gc_ref, tot_ref, gb_ref):
        tot = v_ref[0:1, :]
        gb = ga_ref[0:1, :]
        for p in range(1, NDEV):
            tot = tot + v_ref[p:p + 1, :]
            gb = gb + ga_ref[p:p + 1, :]
        for p in range(NDEV):
            gb = gb + gc_ref[p:p + 1, :]
        tot_ref[...] = tot
        gb_ref[...] = gb

    blocks = [((NDEV, ns), F32), ((NDEV, nm), F32), ((NDEV, nm), F32), ((1, ns), F32), ((1, nm), F32)]
    return pl.pallas_call(
        body, name=name, out_shape=[_sds((1, ns), F32), _sds((1, nm), F32)],
        compiler_params=_params(blocks),
    )(vs, ga, gc)


def _adamw_small(g_raw, w, m, v, lam_range, cctx_range, name):
    npk = w.shape[1]

    def body(g_ref, w_ref, m_ref, v_ref, go_ref, d_ref, nm_ref, nv_ref):
        wv = w_ref[...]
        g = g_ref[...]
        idx = lax.broadcasted_iota(jnp.int32, (1, npk), 1)
        in_lam = (idx >= lam_range[0]) & (idx < lam_range[1])
        in_cc = (idx >= cctx_range[0]) & (idx < cctx_range[1])
        sg = _sigmoid_small(wv)
        g = jnp.where(in_lam, g * (LRU_C * _sigmoid_small(-wv)), jnp.where(in_cc, g * _dsilu(wv, sg), g))
        delta, nm, nv = _adam(wv, g, m_ref[...], v_ref[...])
        go_ref[...] = g
        d_ref[...] = delta
        nm_ref[...] = nm
        nv_ref[...] = nv

    return pl.pallas_call(
        body, name=name, out_shape=[_sds((1, npk), F32)] * 4,
        compiler_params=_params([((1, npk), F32)] * 8),
    )(g_raw, w, m, v)


def _pack(pieces):
    return jnp.concatenate([p.reshape(1, -1) for p in pieces], axis=1)


def kernel(x, c, ctx, c_ctx, w_ada, b_ada, g_norm, w_in, conv_w, conv_b, lru_lambda, w_rgate, b_rgate, w_igate, b_igate, w_pool, b_pool, pool_scale, w_out, g_final, loss_target, m_c_ctx, m_w_ada, m_b_ada, m_g_norm, m_w_in, m_conv_w, m_conv_b, m_lru_lambda, m_w_rgate, m_b_rgate, m_w_igate, m_b_igate, m_w_pool, m_b_pool, m_pool_scale, m_w_out, m_g_final, v_c_ctx, v_w_ada, v_b_ada, v_g_norm, v_w_in, v_conv_w, v_conv_b, v_lru_lambda, v_w_rgate, v_b_rgate, v_w_igate, v_b_igate, v_w_pool, v_b_pool, v_pool_scale, v_w_out, v_g_final):
    L, D = x.shape[1], x.shape[2]
    lc = ctx.shape[1]
    n = lc + L
    W = conv_b.shape[1]
    heads, hd = w_rgate.shape[2], w_rgate.shape[4]
    G, pd = w_pool.shape[1], w_pool.shape[3]
    na = w_ada.shape[2]
    nb = w_in.shape[2]
    ws = W // NDEV
    me = 4 * lax.axis_index("x") + 2 * lax.axis_index("y") + lax.axis_index("c")

    nbp = nb // WIN_PARTS
    w_in_parts = [w_in[0, :, q * nbp:(q + 1) * nbp].astype(MXU) for q in range(WIN_PARTS)]
    (win_0, cw_all, lam_all, br_all, bi_all, c_all) = _all_gather(
        [w_in_parts[0], conv_w[0], lru_lambda[0], b_rgate[0], b_igate[0], c], "gather_w_in")
    win = [win_0]
    cw = cw_all.transpose(1, 0, 2).reshape(4, W)
    lam = lam_all.transpose(1, 0, 2).reshape(2, W)
    br = br_all.transpose(1, 0, 2).reshape(2, W)
    bi = bi_all.transpose(1, 0, 2).reshape(2, W)

    cc = jnp.concatenate([c_all.reshape(NDEV, D), c_ctx.reshape(1, D), jnp.zeros((NDEV - 1, D), F32)], axis=0)
    b_loc = lax.dynamic_slice(b_ada, (0, me * na), (1, na))
    mod_loc, s_all = _ada_fwd(cc, w_ada[0], b_loc, "ada_fwd")
    (mod_all,) = _all_gather([mod_loc], "gather_mod")
    gate_w = [w_rgate[0].astype(MXU), w_igate[0].astype(MXU)]
    pool_w, out_w = [w_pool[0].astype(MXU)], [w_out[0].astype(MXU)]
    tok = mod_all
    sent_win = []
    for q in range(1, WIN_PARTS):
        part = [w_in_parts[q]]
        sent_win.append(_send_start(part, _place(part, False, f"place_w_in_{q}", [tok]), "level1", f"start_w_in_{q}"))
        tok = sent_win[-1][4]
    sent_gw = _send_start(gate_w, _place(gate_w, False, "place_gate_w", [tok]), False, "start_gate_w")
    sent_pw = _send_start(pool_w, _place(pool_w, False, "place_pool_w", [sent_gw[4]]), False, "start_pool_w")
    sent_ow = _send_start(out_w, _place(out_w, False, "place_out_w", [sent_pw[4]]), False, "start_out_w")
    mod = mod_all.transpose(1, 0, 2).reshape(2 * NDEV, NDEV * na)
    mod_me = lax.dynamic_slice(mod, (me, 0), (1, 3 * D))
    shift, scale, gate = mod_me[:, :D], mod_me[:, D:2 * D], mod_me[:, 2 * D:]
    shift = _tie(shift, [sent_gw[4], sent_ow[4]], "tie_weights")
    shift_c, scale_c = mod[NDEV:NDEV + 1, :D], mod[NDEV:NDEV + 1, D:2 * D]

    x2, ctx2, tgt = x[0], ctx[0], loss_target[0]
    gfin = g_final.reshape(1, D)
    h_all = _norm_mod(x2, g_norm, shift, scale, n, lc, None, "norm_lat")
    h_all = _norm_mod(ctx2, g_norm, shift_c, scale_c, n, 0, h_all, "norm_ctx")
    proj_all = _mm_proj(h_all, win[0], 0, WIN_PARTS, None, "mm_proj_0")
    for q in range(1, WIN_PARTS):
        lands = _send_wait(sent_win[q - 1], proj_all, "level1", f"wait_w_in_{q}")
        passed = _send_start([], lands, "level2", f"pass_w_in_{q}")
        win.append(_send_wait(passed, proj_all, "level2", f"wait_pass_w_in_{q}")[0])
        proj_all = _mm_proj(h_all, win[q], q, WIN_PARTS, proj_all, f"mm_proj_{q}")
    u_all = _conv_fwd(proj_all, cw, conv_b, lc, W, "conv_fwd")
    wr_all, wi_all = _send_wait(sent_gw, u_all, False, "wait_gate_w")
    wr = wr_all.transpose(1, 2, 0, 3, 4).reshape(2, heads, hd, hd)
    wi = wi_all.transpose(1, 2, 0, 3, 4).reshape(2, heads, hd, hd)
    a_all, b_all = _gates_fwd(u_all, wr, wi, br, bi, lam, "gates_fwd")
    hs = _scan_fwd(a_all, b_all, lc, "scan_fwd")
    z = _pool_z(proj_all, lc, W, L, W, False, None, "pool_z")
    (wpool_all,) = _send_wait(sent_pw, hs, False, "wait_pool_w")
    wpool = wpool_all.transpose(1, 0, 2, 3).reshape(G, pd, pd)
    ypre = _mm_group(z, wpool, "fwd", F32, "mm_pool")
    mixed = _mix_fwd(hs, proj_all, ypre, b_pool, pool_scale, lc, "mix_fwd")
    (wout_all,) = _send_wait(sent_ow, mixed, False, "wait_out_w")
    wout = wout_all.reshape(2 * W, D)
    out = _mm_plain(mixed, wout, NN, F32, "mm_out")
    d_out, dxn, loss_p, ggf, dgate = _final(x2, out, tgt, gate, gfin, "final")

    dmixed = _mm_plain(d_out, wout, NT, F32, "mm_dmixed")
    gwout = _mm_plain(mixed, d_out, TN_DIMS, MXU, "mm_gwout")
    ex_o = [gwout.reshape(NDEV, 2 * W // NDEV, D)]
    sent_o = _send_start(ex_o, _place(ex_o, True, "place_gwout"), True, "start_gwout")
    dproj = _dproj_init(n, lc, W, "dproj_init")
    dya, dproj = _mixa_bwd(dmixed, hs, proj_all, dproj, lc, W, "mixa_bwd")
    dypre, dproj, gbp, gps = _mixb_bwd(dmixed, ypre, proj_all, _tie(b_pool, [sent_o[4]], "tie_gwout"), pool_scale,
                                       dproj, lc, W, "mixb_bwd")
    dz = _mm_group(dypre, wpool, "bwd", F32, "mm_dz")
    gwpool = _mm_group(z, dypre, "wgrad", MXU, "mm_gwpool")
    dproj = _pool_z(dz, lc, W, L, W, True, dproj, "pool_z_bwd")
    da, db = _scan_bwd(a_all, hs, dya, lc, "scan_bwd")
    du, gwr, gwi, gbr, gbi, gcl = _gates_bwd(u_all, da, db, wr, wi, br, bi, lam, "gates_bwd")
    ex_s = [gwpool.reshape(G, NDEV, pd // NDEV, pd).transpose(1, 0, 2, 3),
            gwr.reshape(2, heads, NDEV, hd // NDEV, hd).transpose(2, 0, 1, 3, 4),
            gwi.reshape(2, heads, NDEV, hd // NDEV, hd).transpose(2, 0, 1, 3, 4)]
    sent_s = _send_start(ex_s, _place(ex_s, True, "place_gsmall"), True, "start_gsmall")
    dproj, gcw, gcb = _conv_bwd(du, proj_all, _tie(cw, [sent_s[4]], "tie_gsmall"), dproj, lc, W, "conv_bwd")
    h_t = _transpose(h_all, "transpose_h")
    sent_i, tok = [], None
    for q in range(GWIN_PARTS):
        part = _mm_gwin(h_t, dproj, nb, q, GWIN_PARTS, f"mm_gwin_{q}", dep=tok)
        part = pltpu.with_memory_space_constraint(part, pltpu.HBM)
        sent_i.append(_send_start([part], _place([part], True, f"place_gwin_{q}"), True, f"start_gwin_{q}"))
        tok = sent_i[-1][4]
    dh_all = _mm_dh(dproj, win, "mm_dh", tok)
    grad_x, dshift, dscale, ggn = _norm_bwd(x2, dh_all, lc, g_norm, scale, dxn, jnp.zeros((1, D), F32), "norm_bwd_lat")
    _, dshift_c, dscale_c, ggn = _norm_bwd(ctx2, dh_all, 0, g_norm, scale_c, None, ggn, "norm_bwd_ctx")

    dmod_me = jnp.concatenate([dshift, dscale, dgate], axis=1)
    dmod_c = jnp.concatenate([dshift_c, dscale_c, jnp.zeros((1, D), F32)], axis=1)
    smalls = [ggf, ggn, gcw, gcb, gcl, gbr, gbi, gbp, gps, jnp.pad(loss_p, ((0, 0), (0, LANE - 1)))]
    sizes = [s.size for s in smalls]
    small_all, dmod_all, dmodc_all = _all_gather([_pack(smalls), dmod_me, dmod_c], "gather_small")
    ga = lax.dynamic_slice(dmod_all.reshape(NDEV, 3 * D), (0, me * na), (NDEV, na))
    gc = lax.dynamic_slice(dmodc_all.reshape(NDEV, 3 * D), (0, me * na), (NDEV, na))
    g_wada, d_wada, nm_wada, nv_wada, pc = _ada_bwd(s_all, ga, gc, w_ada[0], m_w_ada[0], v_w_ada[0], "ada_bwd")
    (pc_all,) = _all_gather([pc[0:1]], "gather_cctx")
    tot, gb_ada = _small_sum(
        jnp.concatenate([small_all.reshape(NDEV, -1), pc_all.reshape(NDEV, D)], axis=1),
        dmod_all.reshape(NDEV, 3 * D), dmodc_all.reshape(NDEV, 3 * D), "small_sum")
    offs = [0]
    for s in sizes + [D]:
        offs.append(offs[-1] + s)
    t_ggf, t_ggn, t_gcw, t_gcb, t_gcl, t_gbr, t_gbi, t_gbp, t_gps, t_loss, t_pc = [
        tot[:, offs[i]:offs[i + 1]] for i in range(len(offs) - 1)]

    def shard(t, rows):
        return lax.dynamic_slice(t.reshape(rows, W), (0, me * ws), (rows, ws))

    def big(wv, parts, mv, vv, name):
        shp = wv.shape
        C = shp[-1]
        if not isinstance(parts, list):
            parts = [parts]
        parts = [p.reshape(NDEV, -1, C // len(parts)) for p in parts]
        outs = _adamw_parts(wv.reshape(-1, C), parts, mv.reshape(-1, C), vv.reshape(-1, C), name)
        return [o.reshape(shp) for o in outs]

    (recv_o,) = _send_wait(sent_o, tot, True, "wait_gwout")
    recv_p, recv_r, recv_i = _send_wait(sent_s, tot, True, "wait_gsmall")
    r_wout = big(w_out, recv_o, m_w_out, v_w_out, "adamw_w_out")
    r_wpool = big(w_pool, recv_p, m_w_pool, v_w_pool, "adamw_w_pool")
    r_wr = big(w_rgate, recv_r, m_w_rgate, v_w_rgate, "adamw_w_rgate")
    r_wi = big(w_igate, recv_i, m_w_igate, v_w_igate, "adamw_w_igate")
    r_wada = [o.reshape(w_ada.shape) for o in (g_wada, d_wada, nm_wada, nv_wada)]

    names = ["c_ctx", "b_ada", "g_norm", "conv_w", "conv_b", "lru_lambda", "b_rgate", "b_igate", "b_pool",
             "pool_scale", "g_final"]
    sw = [c_ctx, b_ada, g_norm, conv_w, conv_b, lru_lambda, b_rgate, b_igate, b_pool, pool_scale, g_final]
    sm = [m_c_ctx, m_b_ada, m_g_norm, m_conv_w, m_conv_b, m_lru_lambda, m_b_rgate, m_b_igate, m_b_pool,
          m_pool_scale, m_g_final]
    sv = [v_c_ctx, v_b_ada, v_g_norm, v_conv_w, v_conv_b, v_lru_lambda, v_b_rgate, v_b_igate, v_b_pool,
          v_pool_scale, v_g_final]
    sg = [t_pc, gb_ada, t_ggn, shard(t_gcw, 4), t_gcb, shard(t_gcl, 2), shard(t_gbr, 2), shard(t_gbi, 2), t_gbp,
          t_gps, t_ggf]
    poffs = [0]
    for wv in sw:
        poffs.append(poffs[-1] + wv.size)
    lam_range = (poffs[5], poffs[6])
    cctx_range = (poffs[0], poffs[1])
    small_out = _adamw_small(_pack(sg), _pack(sw), _pack(sm), _pack(sv), lam_range, cctx_range, "adamw_small")
    recv_w = [_send_wait(sent_i[q], small_out[0], True, f"wait_gwin_{q}")[0] for q in range(GWIN_PARTS)]
    r_win = big(w_in, recv_w, m_w_in, v_w_in, "adamw_w_in")
    r_small = {}
    for i, nm in enumerate(names):
        r_small[nm] = [o[:, poffs[i]:poffs[i + 1]].reshape(sw[i].shape) for o in small_out]

    res = dict(r_small)
    res.update(w_ada=r_wada, w_in=r_win, w_rgate=r_wr, w_igate=r_wi, w_pool=r_wpool, w_out=r_wout)
    order = ["c_ctx", "w_ada", "b_ada", "g_norm", "w_in", "conv_w", "conv_b", "lru_lambda", "w_rgate", "b_rgate",
             "w_igate", "b_igate", "w_pool", "b_pool", "pool_scale", "w_out", "g_final"]
    loss = t_loss[0, 0]
    outs = [loss, grad_x.reshape(x.shape)]
    for q in range(4):
        outs += [res[nm][q] for nm in order]
    return tuple(outs)
```
